```python
import jax, jax.numpy as jnp
from jax import lax
import numpy as np

D_MODEL = 1024
BATCH = 8
SEQ = 4096
DEPTH = 1

CONV_WIDTH = D_MODEL
CONV_GROUPS = 8
CONV_KERNEL = 31
SGU_WIDTH = D_MODEL
SGU_HEADS = 8
SGU_HEAD_DIM = SGU_WIDTH // SGU_HEADS
CHUNK = 128
EPS = 1e-6

OFF_A_VAL = 0
OFF_A_GLU = OFF_A_VAL + CONV_WIDTH
OFF_A_SILU = OFF_A_GLU + CONV_WIDTH
OFF_B_U = OFF_A_SILU + CONV_WIDTH
OFF_B_V = OFF_B_U + SGU_WIDTH
OFF_B_SILU = OFF_B_V + SGU_WIDTH
OFF_G_A = OFF_B_SILU + SGU_WIDTH
OFF_G_B = OFF_G_A + D_MODEL
IN_COLS = OFF_G_B + D_MODEL

kernel_name = "hybrid_conformer_conv_gmlp_adaln"


def rmsnorm(x, g):
    x32 = x.astype(jnp.float32)
    y = x32 * lax.rsqrt(jnp.mean(x32 * x32, axis=-1, keepdims=True) + EPS)
    return y.astype(x.dtype) * g


def layernorm(x, g, b):
    x32 = x.astype(jnp.float32)
    mu = jnp.mean(x32, axis=-1, keepdims=True)
    xc = x32 - mu
    var = jnp.mean(xc * xc, axis=-1, keepdims=True)
    return (xc * lax.rsqrt(var + EPS)).astype(x.dtype) * g + b


def conformer_conv_branch(val, glu, z, conv_w, conv_b, ln_g, ln_b, w_out):
    a = val * jax.nn.sigmoid(glu)
    kern = conv_w.reshape(CONV_KERNEL, 1, CONV_WIDTH)
    y = lax.conv_general_dilated(
        a, kern, window_strides=(1,), padding=[(CONV_KERNEL - 1, 0)],
        dimension_numbers=("NWC", "WIO", "NWC"),
        feature_group_count=CONV_WIDTH) + conv_b
    y = jax.nn.silu(layernorm(y, ln_g, ln_b))
    y = y * jax.nn.silu(z)
    return y @ w_out


def sgu_branch(u, v, z, ln_g, ln_b, w_s, b_s, w_out):
    bsz, seq, _ = u.shape
    u = jax.nn.gelu(u, approximate=False)
    v = layernorm(jax.nn.gelu(v, approximate=False), ln_g, ln_b)
    vc = v.reshape(bsz, seq // CHUNK, CHUNK, SGU_HEADS, SGU_HEAD_DIM)
    causal = jnp.tril(jnp.ones((CHUNK, CHUNK), dtype=bool))
    ws = jnp.where(causal[None], w_s, jnp.zeros((), w_s.dtype))
    s = jnp.einsum("hts,bcshd->bcthd", ws, vc) + b_s.T[:, :, None]
    s = s.reshape(bsz, seq, SGU_WIDTH)
    y = u * s * jax.nn.silu(z)
    return y @ w_out


def _fwd_setup_inputs(seed: int = 0) -> dict:
    key = jax.random.key(seed)
    ks = jax.random.split(key, 20)
    f32 = jnp.float32
    n = lambda k, shape, s: (jax.random.normal(k, shape, f32) * s)
    x = jax.random.normal(ks[0], (BATCH, SEQ, D_MODEL), f32)
    c = jax.random.normal(ks[1], (BATCH, D_MODEL), f32)
    w_ada = n(ks[2], (DEPTH, D_MODEL, 3 * D_MODEL), 0.3 * D_MODEL ** -0.5)
    b_ada = n(ks[3], (DEPTH, 3 * D_MODEL), 0.02)
    g_pre = 1.0 + n(ks[4], (DEPTH, D_MODEL), 0.02)
    w_in = n(ks[5], (DEPTH, D_MODEL, IN_COLS), D_MODEL ** -0.5)
    conv_w = n(ks[6], (DEPTH, CONV_KERNEL, CONV_WIDTH), CONV_KERNEL ** -0.5)
    conv_b = n(ks[7], (DEPTH, CONV_WIDTH), 0.02)
    conv_ln_g = 1.0 + n(ks[8], (DEPTH, CONV_WIDTH), 0.02)
    conv_ln_b = n(ks[9], (DEPTH, CONV_WIDTH), 0.02)
    w_conv_out = n(ks[10], (DEPTH, CONV_WIDTH, D_MODEL), CONV_WIDTH ** -0.5)
    sgu_ln_g = 1.0 + n(ks[11], (DEPTH, SGU_WIDTH), 0.02)
    sgu_ln_b = n(ks[12], (DEPTH, SGU_WIDTH), 0.02)
    w_sgu = n(ks[13], (DEPTH, SGU_HEADS, CHUNK, CHUNK), 0.5 * CHUNK ** -0.5)
    b_sgu = 1.0 + n(ks[14], (DEPTH, SGU_HEADS, CHUNK), 0.02)
    w_sgu_out = n(ks[15], (DEPTH, SGU_WIDTH, D_MODEL), SGU_WIDTH ** -0.5)
    w_o = n(ks[16], (DEPTH, D_MODEL, D_MODEL), D_MODEL ** -0.5)
    g_final = 1.0 + n(ks[17], (D_MODEL,), 0.02)
    return {"x": x, "c": c, "w_ada": w_ada, "b_ada": b_ada, "g_pre": g_pre, "w_in": w_in,
            "conv_w": conv_w, "conv_b": conv_b, "conv_ln_g": conv_ln_g, "conv_ln_b": conv_ln_b,
            "w_conv_out": w_conv_out, "sgu_ln_g": sgu_ln_g, "sgu_ln_b": sgu_ln_b,
            "w_sgu": w_sgu, "b_sgu": b_sgu, "w_sgu_out": w_sgu_out, "w_o": w_o,
            "g_final": g_final}


def _fwd_reference(x, c, w_ada, b_ada, g_pre, w_in, conv_w, conv_b, conv_ln_g, conv_ln_b,
              w_conv_out, sgu_ln_g, sgu_ln_b, w_sgu, b_sgu, w_sgu_out, w_o, g_final):
    for l in range(DEPTH):
        mod = c @ w_ada[l] + b_ada[l]
        shift, scale, gate = jnp.split(mod, 3, axis=-1)
        h = rmsnorm(x, g_pre[l]) * (1.0 + scale[:, None, :]) + shift[:, None, :]
        p = h @ w_in[l]
        y_a = conformer_conv_branch(
            p[..., OFF_A_VAL:OFF_A_GLU], p[..., OFF_A_GLU:OFF_A_SILU], p[..., OFF_A_SILU:OFF_B_U],
            conv_w[l], conv_b[l], conv_ln_g[l], conv_ln_b[l], w_conv_out[l])
        y_b = sgu_branch(
            p[..., OFF_B_U:OFF_B_V], p[..., OFF_B_V:OFF_B_SILU], p[..., OFF_B_SILU:OFF_G_A],
            sgu_ln_g[l], sgu_ln_b[l], w_sgu[l], b_sgu[l], w_sgu_out[l])
        merged = (jax.nn.sigmoid(p[..., OFF_G_A:OFF_G_B]) * y_a
                  + jax.nn.sigmoid(p[..., OFF_G_B:IN_COLS]) * y_b)
        x = x + gate[:, None, :] * (merged @ w_o[l])
    return rmsnorm(x, g_final)


import jax as _jax
import jax.numpy as _jnp

TWIN_FORMAT = 'train_step'
FWD_PARAMS = ['x', 'c', 'w_ada', 'b_ada', 'g_pre', 'w_in', 'conv_w', 'conv_b', 'conv_ln_g', 'conv_ln_b', 'w_conv_out', 'sgu_ln_g', 'sgu_ln_b', 'w_sgu', 'b_sgu', 'w_sgu_out', 'w_o', 'g_final']
TWIN_WEIGHTS = ['w_ada', 'b_ada', 'g_pre', 'w_in', 'conv_w', 'conv_b', 'conv_ln_g', 'conv_ln_b', 'w_conv_out', 'sgu_ln_g', 'sgu_ln_b', 'w_sgu', 'b_sgu', 'w_sgu_out', 'w_o', 'g_final']
TWIN_DIFF_INPUT = 'x'
TWIN_INPUTS = ['x', 'c', 'w_ada', 'b_ada', 'g_pre', 'w_in', 'conv_w', 'conv_b', 'conv_ln_g', 'conv_ln_b', 'w_conv_out', 'sgu_ln_g', 'sgu_ln_b', 'w_sgu', 'b_sgu', 'w_sgu_out', 'w_o', 'g_final', 'loss_target', 'm_w_ada', 'm_b_ada', 'm_g_pre', 'm_w_in', 'm_conv_w', 'm_conv_b', 'm_conv_ln_g', 'm_conv_ln_b', 'm_w_conv_out', 'm_sgu_ln_g', 'm_sgu_ln_b', 'm_w_sgu', 'm_b_sgu', 'm_w_sgu_out', 'm_w_o', 'm_g_final', 'v_w_ada', 'v_b_ada', 'v_g_pre', 'v_w_in', 'v_conv_w', 'v_conv_b', 'v_conv_ln_g', 'v_conv_ln_b', 'v_w_conv_out', 'v_sgu_ln_g', 'v_sgu_ln_b', 'v_w_sgu', 'v_b_sgu', 'v_w_sgu_out', 'v_w_o', 'v_g_final']
TWIN_OUTPUTS = ['loss', 'grad_x', 'grad_w_ada', 'grad_b_ada', 'grad_g_pre', 'grad_w_in', 'grad_conv_w', 'grad_conv_b', 'grad_conv_ln_g', 'grad_conv_ln_b', 'grad_w_conv_out', 'grad_sgu_ln_g', 'grad_sgu_ln_b', 'grad_w_sgu', 'grad_b_sgu', 'grad_w_sgu_out', 'grad_w_o', 'grad_g_final', 'delta_w_ada', 'delta_b_ada', 'delta_g_pre', 'delta_w_in', 'delta_conv_w', 'delta_conv_b', 'delta_conv_ln_g', 'delta_conv_ln_b', 'delta_w_conv_out', 'delta_sgu_ln_g', 'delta_sgu_ln_b', 'delta_w_sgu', 'delta_b_sgu', 'delta_w_sgu_out', 'delta_w_o', 'delta_g_final', 'new_m_w_ada', 'new_m_b_ada', 'new_m_g_pre', 'new_m_w_in', 'new_m_conv_w', 'new_m_conv_b', 'new_m_conv_ln_g', 'new_m_conv_ln_b', 'new_m_w_conv_out', 'new_m_sgu_ln_g', 'new_m_sgu_ln_b', 'new_m_w_sgu', 'new_m_b_sgu', 'new_m_w_sgu_out', 'new_m_w_o', 'new_m_g_final', 'new_v_w_ada', 'new_v_b_ada', 'new_v_g_pre', 'new_v_w_in', 'new_v_conv_w', 'new_v_conv_b', 'new_v_conv_ln_g', 'new_v_conv_ln_b', 'new_v_w_conv_out', 'new_v_sgu_ln_g', 'new_v_sgu_ln_b', 'new_v_w_sgu', 'new_v_b_sgu', 'new_v_w_sgu_out', 'new_v_w_o', 'new_v_g_final']
TWIN_LEAF_KINDS = {'loss': 'loss', 'grad_x': 'grad_x', 'grad_w_ada': 'grad_w', 'grad_b_ada': 'grad_w', 'grad_g_pre': 'grad_w', 'grad_w_in': 'grad_w', 'grad_conv_w': 'grad_w', 'grad_conv_b': 'grad_w', 'grad_conv_ln_g': 'grad_w', 'grad_conv_ln_b': 'grad_w', 'grad_w_conv_out': 'grad_w', 'grad_sgu_ln_g': 'grad_w', 'grad_sgu_ln_b': 'grad_w', 'grad_w_sgu': 'grad_w', 'grad_b_sgu': 'grad_w', 'grad_w_sgu_out': 'grad_w', 'grad_w_o': 'grad_w', 'grad_g_final': 'grad_w', 'delta_w_ada': 'delta_w', 'delta_b_ada': 'delta_w', 'delta_g_pre': 'delta_w', 'delta_w_in': 'delta_w', 'delta_conv_w': 'delta_w', 'delta_conv_b': 'delta_w', 'delta_conv_ln_g': 'delta_w', 'delta_conv_ln_b': 'delta_w', 'delta_w_conv_out': 'delta_w', 'delta_sgu_ln_g': 'delta_w', 'delta_sgu_ln_b': 'delta_w', 'delta_w_sgu': 'delta_w', 'delta_b_sgu': 'delta_w', 'delta_w_sgu_out': 'delta_w', 'delta_w_o': 'delta_w', 'delta_g_final': 'delta_w', 'new_m_w_ada': 'new_m', 'new_m_b_ada': 'new_m', 'new_m_g_pre': 'new_m', 'new_m_w_in': 'new_m', 'new_m_conv_w': 'new_m', 'new_m_conv_b': 'new_m', 'new_m_conv_ln_g': 'new_m', 'new_m_conv_ln_b': 'new_m', 'new_m_w_conv_out': 'new_m', 'new_m_sgu_ln_g': 'new_m', 'new_m_sgu_ln_b': 'new_m', 'new_m_w_sgu': 'new_m', 'new_m_b_sgu': 'new_m', 'new_m_w_sgu_out': 'new_m', 'new_m_w_o': 'new_m', 'new_m_g_final': 'new_m', 'new_v_w_ada': 'new_v', 'new_v_b_ada': 'new_v', 'new_v_g_pre': 'new_v', 'new_v_w_in': 'new_v', 'new_v_conv_w': 'new_v', 'new_v_conv_b': 'new_v', 'new_v_conv_ln_g': 'new_v', 'new_v_conv_ln_b': 'new_v', 'new_v_w_conv_out': 'new_v', 'new_v_sgu_ln_g': 'new_v', 'new_v_sgu_ln_b': 'new_v', 'new_v_w_sgu': 'new_v', 'new_v_b_sgu': 'new_v', 'new_v_w_sgu_out': 'new_v', 'new_v_w_o': 'new_v', 'new_v_g_final': 'new_v'}


def _forward(args):
    return _fwd_reference(*[args[k] for k in FWD_PARAMS])


def _output_shape():
    out = _jax.eval_shape(lambda: _forward(_fwd_setup_inputs(0)))
    return out.shape, out.dtype

N_MICROBATCH = 1
ADAM_LR = 0.001
ADAM_B1 = 0.9
ADAM_B2 = 0.999
ADAM_EPS = 1e-08
ADAM_WD = 0.01
ADAM_STEP = 10
PER_EXAMPLE_BATCH_AXIS = {'x': 0, 'c': 0, 'loss_target': 0}
SHARED_INPUTS = []
_WEIGHT_DTYPES = {'w_ada': _jnp.float32, 'b_ada': _jnp.float32, 'g_pre': _jnp.float32, 'w_in': _jnp.float32, 'conv_w': _jnp.float32, 'conv_b': _jnp.float32, 'conv_ln_g': _jnp.float32, 'conv_ln_b': _jnp.float32, 'w_conv_out': _jnp.float32, 'sgu_ln_g': _jnp.float32, 'sgu_ln_b': _jnp.float32, 'w_sgu': _jnp.float32, 'b_sgu': _jnp.float32, 'w_sgu_out': _jnp.float32, 'w_o': _jnp.float32, 'g_final': _jnp.float32}
MOMENT_SCALE = {'w_ada': 4.543250e-02, 'b_ada': 4.393390e-02, 'g_pre': 3.006425e-02, 'w_in': 1.082090e-02, 'conv_w': 1.261840e-02, 'conv_b': 2.405605e-02, 'conv_ln_g': 1.527892e-02, 'conv_ln_b': 1.278267e-02, 'w_conv_out': 1.202432e-02, 'sgu_ln_g': 5.500701e-03, 'sgu_ln_b': 5.058242e-03, 'w_sgu': 1.028899e-02, 'b_sgu': 1.450266e-02, 'w_sgu_out': 1.529533e-02, 'w_o': 1.945753e-02, 'g_final': 3.199963e+01}


def _to_microbatches(a, axis):
    t = _jnp.moveaxis(a, axis, 0)
    t = t.reshape((N_MICROBATCH, t.shape[0] // N_MICROBATCH) + t.shape[1:])
    return _jnp.moveaxis(t, 1, axis + 1)


def setup_inputs(seed: int = 0) -> dict:
    inp = _fwd_setup_inputs(seed)
    key = _jax.random.fold_in(_jax.random.key(seed), 7919)
    shape, _ = _output_shape()
    out = dict(inp)
    out["loss_target"] = _jax.random.normal(_jax.random.fold_in(key, 0), shape, _jnp.float32)
    for i, name in enumerate(TWIN_WEIGHTS):
        w = inp[name].astype(_jnp.float32)
        if MOMENT_SCALE is None:
            s = _jnp.sqrt(_jnp.mean(_jnp.square(w)) + 1e-30)
        else:
            s = MOMENT_SCALE[name]
        km, kv = _jax.random.split(_jax.random.fold_in(key, i + 1))
        out[name] = w
        out["m_" + name] = s * _jax.random.normal(km, w.shape, _jnp.float32)
        out["v_" + name] = (s * s) * _jax.random.uniform(kv, w.shape, _jnp.float32, 0.5, 1.5)
    if N_MICROBATCH > 1:
        for name, axis in PER_EXAMPLE_BATCH_AXIS.items():
            out[name] = _to_microbatches(out[name], axis)
    return {'x': out['x'], 'c': out['c'], 'w_ada': out['w_ada'], 'b_ada': out['b_ada'], 'g_pre': out['g_pre'], 'w_in': out['w_in'], 'conv_w': out['conv_w'], 'conv_b': out['conv_b'], 'conv_ln_g': out['conv_ln_g'], 'conv_ln_b': out['conv_ln_b'], 'w_conv_out': out['w_conv_out'], 'sgu_ln_g': out['sgu_ln_g'], 'sgu_ln_b': out['sgu_ln_b'], 'w_sgu': out['w_sgu'], 'b_sgu': out['b_sgu'], 'w_sgu_out': out['w_sgu_out'], 'w_o': out['w_o'], 'g_final': out['g_final'], 'loss_target': out['loss_target'], 'm_w_ada': out['m_w_ada'], 'm_b_ada': out['m_b_ada'], 'm_g_pre': out['m_g_pre'], 'm_w_in': out['m_w_in'], 'm_conv_w': out['m_conv_w'], 'm_conv_b': out['m_conv_b'], 'm_conv_ln_g': out['m_conv_ln_g'], 'm_conv_ln_b': out['m_conv_ln_b'], 'm_w_conv_out': out['m_w_conv_out'], 'm_sgu_ln_g': out['m_sgu_ln_g'], 'm_sgu_ln_b': out['m_sgu_ln_b'], 'm_w_sgu': out['m_w_sgu'], 'm_b_sgu': out['m_b_sgu'], 'm_w_sgu_out': out['m_w_sgu_out'], 'm_w_o': out['m_w_o'], 'm_g_final': out['m_g_final'], 'v_w_ada': out['v_w_ada'], 'v_b_ada': out['v_b_ada'], 'v_g_pre': out['v_g_pre'], 'v_w_in': out['v_w_in'], 'v_conv_w': out['v_conv_w'], 'v_conv_b': out['v_conv_b'], 'v_conv_ln_g': out['v_conv_ln_g'], 'v_conv_ln_b': out['v_conv_ln_b'], 'v_w_conv_out': out['v_w_conv_out'], 'v_sgu_ln_g': out['v_sgu_ln_g'], 'v_sgu_ln_b': out['v_sgu_ln_b'], 'v_w_sgu': out['v_w_sgu'], 'v_b_sgu': out['v_b_sgu'], 'v_w_sgu_out': out['v_w_sgu_out'], 'v_w_o': out['v_w_o'], 'v_g_final': out['v_g_final']}


def _loss(weights, diff, rest, loss_target):
    with _jax.named_scope("forward"):
        args = {**rest, TWIN_DIFF_INPUT: diff, **{k: w.astype(_WEIGHT_DTYPES[k]) for k, w in weights.items()}}
        y = _forward(args)
    with _jax.named_scope("loss_head"):
        err = _jnp.square(y.astype(_jnp.float32) - loss_target)
        return 0.5 * _jnp.sum(_jnp.mean(err, axis=-1)) if err.ndim else 0.5 * err


def _adamw(w, g, m, v):
    m = ADAM_B1 * m + (1.0 - ADAM_B1) * g
    v = ADAM_B2 * v + (1.0 - ADAM_B2) * _jnp.square(g)
    m_hat = m / (1.0 - ADAM_B1 ** ADAM_STEP)
    v_hat = v / (1.0 - ADAM_B2 ** ADAM_STEP)
    delta = -ADAM_LR * (m_hat / (_jnp.sqrt(v_hat) + ADAM_EPS) + ADAM_WD * w)
    return delta, m, v


def reference(x, c, w_ada, b_ada, g_pre, w_in, conv_w, conv_b, conv_ln_g, conv_ln_b, w_conv_out, sgu_ln_g, sgu_ln_b, w_sgu, b_sgu, w_sgu_out, w_o, g_final, loss_target, m_w_ada, m_b_ada, m_g_pre, m_w_in, m_conv_w, m_conv_b, m_conv_ln_g, m_conv_ln_b, m_w_conv_out, m_sgu_ln_g, m_sgu_ln_b, m_w_sgu, m_b_sgu, m_w_sgu_out, m_w_o, m_g_final, v_w_ada, v_b_ada, v_g_pre, v_w_in, v_conv_w, v_conv_b, v_conv_ln_g, v_conv_ln_b, v_w_conv_out, v_sgu_ln_g, v_sgu_ln_b, v_w_sgu, v_b_sgu, v_w_sgu_out, v_w_o, v_g_final):
    given = dict(x=x, c=c, w_ada=w_ada, b_ada=b_ada, g_pre=g_pre, w_in=w_in, conv_w=conv_w, conv_b=conv_b, conv_ln_g=conv_ln_g, conv_ln_b=conv_ln_b, w_conv_out=w_conv_out, sgu_ln_g=sgu_ln_g, sgu_ln_b=sgu_ln_b, w_sgu=w_sgu, b_sgu=b_sgu, w_sgu_out=w_sgu_out, w_o=w_o, g_final=g_final, loss_target=loss_target, m_w_ada=m_w_ada, m_b_ada=m_b_ada, m_g_pre=m_g_pre, m_w_in=m_w_in, m_conv_w=m_conv_w, m_conv_b=m_conv_b, m_conv_ln_g=m_conv_ln_g, m_conv_ln_b=m_conv_ln_b, m_w_conv_out=m_w_conv_out, m_sgu_ln_g=m_sgu_ln_g, m_sgu_ln_b=m_sgu_ln_b, m_w_sgu=m_w_sgu, m_b_sgu=m_b_sgu, m_w_sgu_out=m_w_sgu_out, m_w_o=m_w_o, m_g_final=m_g_final, v_w_ada=v_w_ada, v_b_ada=v_b_ada, v_g_pre=v_g_pre, v_w_in=v_w_in, v_conv_w=v_conv_w, v_conv_b=v_conv_b, v_conv_ln_g=v_conv_ln_g, v_conv_ln_b=v_conv_ln_b, v_w_conv_out=v_w_conv_out, v_sgu_ln_g=v_sgu_ln_g, v_sgu_ln_b=v_sgu_ln_b, v_w_sgu=v_w_sgu, v_b_sgu=v_b_sgu, v_w_sgu_out=v_w_sgu_out, v_w_o=v_w_o, v_g_final=v_g_final)
    weights = {n: given[n] for n in TWIN_WEIGHTS}
    shared = {n: given[n] for n in SHARED_INPUTS}
    per_example = {n: given[n] for n in ['x', 'c']}
    grad_fn = _jax.value_and_grad(_loss, argnums=(0, 1))

    def one_microbatch(ex, loss_target):
        ex = dict(ex)
        diff = ex.pop(TWIN_DIFF_INPUT)
        return grad_fn(weights, diff, {**shared, **ex}, loss_target)

    if N_MICROBATCH == 1:
        loss, (grad_w, grad_x) = one_microbatch(per_example, given["loss_target"])
    else:
        def body(carry, xs):
            loss_sum, grad_sum = carry
            l_k, (gw_k, gx_k) = one_microbatch(xs[0], xs[1])
            with _jax.named_scope("update"):
                return (loss_sum + l_k, _jax.tree.map(_jnp.add, grad_sum, gw_k)), gx_k

        init = (_jnp.zeros((), _jnp.float32), _jax.tree.map(_jnp.zeros_like, weights))
        (loss, grad_w), grad_x = _jax.lax.scan(body, init, (per_example, given["loss_target"]))
    with _jax.named_scope("update"):
        delta_w, new_m, new_v = {}, {}, {}
        for n in TWIN_WEIGHTS:
            delta_w[n], new_m[n], new_v[n] = _adamw(weights[n], grad_w[n], given["m_" + n], given["v_" + n])
    return (loss, grad_x, *[grad_w[n] for n in TWIN_WEIGHTS], *[delta_w[n] for n in TWIN_WEIGHTS],
            *[new_m[n] for n in TWIN_WEIGHTS], *[new_v[n] for n in TWIN_WEIGHTS])
```

```python
import functools

import jax
import jax.numpy as jnp
from jax import lax
from jax.experimental import pallas as pl
from jax.experimental.pallas import tpu as pltpu

F32 = jnp.float32
BF16 = jnp.bfloat16
MESH = pl.DeviceIdType.MESH

D = 1024
N_SEC = 8
N_CHIP = 4
N_DEV = 8
EPS = 1e-6
CONV_K = 31
HALO = 32
CHUNK = 128
HEADS = 8
LANE = 128
SUB = 8
PACK = 16
VMEM_LIMIT = 56 * 1024 * 1024

ADAM_LR, ADAM_B1, ADAM_B2, ADAM_EPS, ADAM_WD, ADAM_STEP = 0.001, 0.9, 0.999, 1e-08, 0.01, 10

_SQRT_HALF = 0.7071067811865476
_INV_SQRT_2PI = 0.3989422804014327


def _sds(shape, dtype):
    return jax.ShapeDtypeStruct(shape, dtype)


def _params(sem=None):
    if sem is None:
        return pltpu.CompilerParams(vmem_limit_bytes=VMEM_LIMIT)
    return pltpu.CompilerParams(dimension_semantics=sem, vmem_limit_bytes=VMEM_LIMIT)


def _strips(n_rows, rows, fn):
    def step(s, carry):
        fn(pl.multiple_of(s * rows, rows))
        return carry
    lax.fori_loop(0, n_rows // rows, step, 0)


def _sigmoid(v):
    return 1.0 / (1.0 + jnp.exp(-v))


def _gelu(v):
    return 0.5 * v * (1.0 + lax.erf(v * _SQRT_HALF))


def _dgelu(v):
    return 0.5 * (1.0 + lax.erf(v * _SQRT_HALF)) + v * jnp.exp(-0.5 * v * v) * _INV_SQRT_2PI


def _dsilu(v, sg):
    return sg * (1.0 + v * (1.0 - sg))


def _rowmean(v):
    return jnp.mean(v, axis=-1, keepdims=True)


def _vec_spec(grid_rank):
    zeros = (0, 0)
    if grid_rank == 1:
        return pl.BlockSpec((1, D), lambda i: zeros)
    return pl.BlockSpec((1, D), lambda i, j: zeros)


def _in_proj(x, shift, scale, g_pre, wg_in):
    s_len = x.shape[0]
    tm = min(512, s_len)
    n_i = s_len // tm
    wn = wg_in.shape[2]

    def body(x_ref, sh_ref, sc_ref, g_ref, w_ref, p_ref, hb_ref):
        @pl.when(pl.program_id(1) == 0)
        def _():
            def strip(r0):
                xs = x_ref[pl.ds(r0, PACK), :]
                r = lax.rsqrt(_rowmean(xs * xs) + EPS)
                h = (xs * r) * g_ref[...] * (1.0 + sc_ref[...]) + sh_ref[...]
                hb_ref[pl.ds(r0, PACK), :] = h.astype(BF16)
            _strips(tm, PACK, strip)
        p_ref[...] = jnp.dot(hb_ref[...], w_ref[0], preferred_element_type=F32)

    return pl.pallas_call(
        body, name="in_proj", grid=(n_i, N_CHIP),
        in_specs=[pl.BlockSpec((tm, D), lambda i, j: (i, 0)), _vec_spec(2), _vec_spec(2), _vec_spec(2),
                  pl.BlockSpec((1, D, wn), lambda i, j: (j, 0, 0))],
        out_specs=[pl.BlockSpec((tm, wn), lambda i, j: (i, j)), pl.BlockSpec((tm, D), lambda i, j: (i, 0))],
        out_shape=[_sds((s_len, N_SEC * D), F32), _sds((s_len, D), BF16)],
        compiler_params=_params(("arbitrary", "arbitrary")),
    )(x, shift, scale, g_pre, wg_in)


def _conv_taps(win_ref, r0, lt, weight_of_offset, rows):
    lanes = pl.ds(lt * LANE, LANE)
    win = win_ref[pl.ds(r0, rows + HALO), lanes]
    n_out = rows // SUB
    acc = [jnp.zeros((SUB, LANE), F32) for _ in range(n_out)]
    for phase in range(SUB):
        offs = [o for o in weight_of_offset if o % SUB == phase]
        if not offs:
            continue
        q_max = max(o // SUB for o in offs)
        span = (n_out + q_max) * SUB
        sh = win[phase:phase + span, :]
        for o in offs:
            q = o // SUB
            w = weight_of_offset[o](lanes)
            for m in range(n_out):
                acc[m] = acc[m] + w * sh[(m + q) * SUB:(m + q + 1) * SUB, :]
    return acc


def _branch_a_fwd(p, conv_wb, conv_b, ln_g, ln_b):
    s_len = p.shape[0]
    tm = min(256, s_len)
    n_i = s_len // tm
    rows = 32

    def body(p_ref, wb_ref, cb_ref, g_ref, b_ref, ya_ref, y1_ref, abuf):
        @pl.when(pl.program_id(0) == 0)
        def _():
            abuf[pl.ds(0, HALO), :] = jnp.zeros((HALO, D), F32)

        def glu(r0):
            val = p_ref[pl.ds(r0, PACK), pl.ds(0, D)]
            gl = p_ref[pl.ds(r0, PACK), pl.ds(D, D)]
            abuf[pl.ds(HALO + r0, PACK), :] = val * _sigmoid(gl)
        _strips(tm, PACK,glu)

        taps = {HALO - (CONV_K - 1) + k: (lambda lanes, k=k: wb_ref[pl.ds(k * SUB, SUB), lanes]) for k in range(CONV_K)}

        def conv(r0):
            for lt in range(D // LANE):
                acc = _conv_taps(abuf, r0, lt, taps, rows)
                cb = cb_ref[:, pl.ds(lt * LANE, LANE)]
                for m, v in enumerate(acc):
                    y1_ref[pl.ds(r0 + m * SUB, SUB), pl.ds(lt * LANE, LANE)] = v + cb
        _strips(tm, rows, conv)

        def norm(r0):
            y1 = y1_ref[pl.ds(r0, PACK), :]
            mu = _rowmean(y1)
            yc = y1 - mu
            rstd = lax.rsqrt(_rowmean(yc * yc) + EPS)
            l1 = (yc * rstd) * g_ref[...] + b_ref[...]
            z = p_ref[pl.ds(r0, PACK), pl.ds(2 * D, D)]
            ya_ref[pl.ds(r0, PACK), :] = ((l1 * _sigmoid(l1)) * (z * _sigmoid(z))).astype(BF16)
        _strips(tm, PACK,norm)

        abuf[pl.ds(0, HALO), :] = abuf[pl.ds(tm, HALO), :]

    return pl.pallas_call(
        body, name="branch_a_fwd", grid=(n_i,),
        in_specs=[pl.BlockSpec((tm, 3 * D), lambda i: (i, 0)),
                  pl.BlockSpec((CONV_K * SUB, D), lambda i: (0, 0)), _vec_spec(1), _vec_spec(1), _vec_spec(1)],
        out_specs=[pl.BlockSpec((tm, D), lambda i: (i, 0)), pl.BlockSpec((tm, D), lambda i: (i, 0))],
        out_shape=[_sds((s_len, D), BF16), _sds((s_len, D), F32)],
        scratch_shapes=[pltpu.VMEM((tm + HALO, D), F32)],
        compiler_params=_params(("arbitrary",)),
    )(p, conv_wb, conv_b, ln_g, ln_b)


def _branch_b_fwd(p, wt, bias_full, ln_g, ln_b):
    s_len = p.shape[0]
    tm = min(256, s_len)
    n_i = s_len // tm

    def body(p_ref, wt_ref, bias_ref, g_ref, b_ref, yb_ref, vb, sbuf):
        def norm(r0):
            gv = _gelu(p_ref[pl.ds(r0, PACK), pl.ds(D, D)])
            mu = _rowmean(gv)
            vc = gv - mu
            rstd = lax.rsqrt(_rowmean(vc * vc) + EPS)
            vb[pl.ds(r0, PACK), :] = ((vc * rstd) * g_ref[...] + b_ref[...]).astype(BF16)
        _strips(tm, PACK,norm)

        for ck in range(tm // CHUNK):
            for h in range(HEADS):
                blk = (pl.ds(ck * CHUNK, CHUNK), pl.ds(h * LANE, LANE))
                sbuf[blk] = jnp.dot(wt_ref[h], vb[blk], preferred_element_type=F32) + bias_ref[:, pl.ds(h * LANE, LANE)]

        def gate(r0):
            u = _gelu(p_ref[pl.ds(r0, PACK), pl.ds(0, D)])
            z = p_ref[pl.ds(r0, PACK), pl.ds(2 * D, D)]
            yb_ref[pl.ds(r0, PACK), :] = (u * sbuf[pl.ds(r0, PACK), :] * (z * _sigmoid(z))).astype(BF16)
        _strips(tm, PACK,gate)

    return pl.pallas_call(
        body, name="branch_b_fwd", grid=(n_i,),
        in_specs=[pl.BlockSpec((tm, 3 * D), lambda i: (i, 1)),
                  pl.BlockSpec((HEADS, CHUNK, CHUNK), lambda i: (0, 0, 0)),
                  pl.BlockSpec((CHUNK, D), lambda i: (0, 0)), _vec_spec(1), _vec_spec(1)],
        out_specs=pl.BlockSpec((tm, D), lambda i: (i, 0)),
        out_shape=_sds((s_len, D), BF16),
        scratch_shapes=[pltpu.VMEM((tm, D), BF16), pltpu.VMEM((tm, D), F32)],
        compiler_params=_params(("arbitrary",)),
    )(p, wt, bias_full, ln_g, ln_b)


def _dot_t(a, b):
    return lax.dot_general(a, b, (((1,), (1,)), ((), ())), preferred_element_type=F32)


def _out_proj(p, ya_in, yb_in, x, target, gate, g_final, w_co, w_so, w_o):
    s_len = x.shape[0]
    tm = min(256, s_len)
    n_i = s_len // tm

    def body(pg_ref, ya_ref, yb_ref, x_ref, t_ref, gate_ref, gf_ref, wco_ref, wso_ref, wo_ref,
             dx2_ref, dya_ref, dyb_ref, dp_ref, mb_ref, dob_ref, dyab_ref, dybb_ref, sums_ref):
        @pl.when(pl.program_id(0) == 0)
        def _():
            sums_ref[...] = jnp.zeros((SUB, D), F32)

        y_a = jnp.dot(ya_ref[...], wco_ref[...], preferred_element_type=F32)
        y_b = jnp.dot(yb_ref[...], wso_ref[...], preferred_element_type=F32)
        ga = _sigmoid(pg_ref[:, pl.ds(0, D)])
        gb = _sigmoid(pg_ref[:, pl.ds(D, D)])
        mb = (ga * y_a + gb * y_b).astype(BF16)
        mb_ref[...] = mb
        o = jnp.dot(mb, wo_ref[...], preferred_element_type=F32)
        x2 = x_ref[...] + gate_ref[...] * o
        r2 = lax.rsqrt(_rowmean(x2 * x2) + EPS)
        xh = x2 * r2
        e = xh * gf_ref[...] - t_ref[...]
        dy = e * (1.0 / D)
        dxh = dy * gf_ref[...]
        dx2 = r2 * (dxh - xh * _rowmean(dxh * xh))
        dx2_ref[...] = dx2
        sums_ref[pl.ds(0, 1), :] += jnp.sum(dy * xh, axis=0, keepdims=True)
        sums_ref[pl.ds(1, 1), :] += jnp.sum(dx2 * o, axis=0, keepdims=True)
        sums_ref[pl.ds(2, 1), :] += jnp.sum(e * e, axis=0, keepdims=True) * (0.5 / D)
        dob = (gate_ref[...] * dx2).astype(BF16)
        dob_ref[...] = dob
        dm = _dot_t(dob, wo_ref[...])
        dy_a = (ga * dm).astype(BF16)
        dy_b = (gb * dm).astype(BF16)
        dyab_ref[...] = dy_a
        dybb_ref[...] = dy_b
        dp_ref[:, pl.ds(0, D)] = (dm * y_a * ga * (1.0 - ga)).astype(BF16)
        dp_ref[:, pl.ds(D, D)] = (dm * y_b * gb * (1.0 - gb)).astype(BF16)
        dya_ref[...] = _dot_t(dy_a, wco_ref[...])
        dyb_ref[...] = _dot_t(dy_b, wso_ref[...])

    tile = pl.BlockSpec((tm, D), lambda i: (i, 0))
    wspec = pl.BlockSpec((D, D), lambda i: (0, 0))
    return pl.pallas_call(
        body, name="out_proj", grid=(n_i,),
        in_specs=[pl.BlockSpec((tm, 2 * D), lambda i: (i, 3)), tile, tile, tile, tile, _vec_spec(1), _vec_spec(1),
                  wspec, wspec, wspec],
        out_specs=[tile, tile, tile, pl.BlockSpec((tm, 2 * D), lambda i: (i, 3)), tile, tile, tile, tile,
                   pl.BlockSpec((SUB, D), lambda i: (0, 0))],
        out_shape=[_sds((s_len, D), F32), _sds((s_len, D), F32), _sds((s_len, D), F32), _sds((s_len, N_SEC * D), BF16),
                   _sds((s_len, D), BF16), _sds((s_len, D), BF16), _sds((s_len, D), BF16), _sds((s_len, D), BF16),
                   _sds((SUB, D), F32)],
        compiler_params=_params(("arbitrary",)),
    )(p, ya_in, yb_in, x, target, gate, g_final, w_co, w_so, w_o)


A_STATS_ROWS = 8 + HALO


def _branch_a_bwd(p, p_halo_src, y1, dya_in, dp, conv_wb, ln_g, ln_b):
    s_len = p.shape[0]
    tm = min(256, s_len)
    n_i = s_len // tm
    rows = 32
    halo_blocks = tm // HALO

    def tile_of(i):
        return n_i - 1 - i

    def body(p_ref, ph_ref, y1_ref, dya_ref, dp_in, wb_ref, g_ref, b_ref, dp_ref, st_ref, abuf, dybuf, acc8, tapacc):
        del dp_in
        i = pl.program_id(0)
        first_tile = tile_of(i) == 0

        @pl.when(i == 0)
        def _():
            dybuf[pl.ds(tm, HALO), :] = jnp.zeros((HALO, D), F32)
            st_ref[...] = jnp.zeros((A_STATS_ROWS, D), F32)
            acc8[...] = jnp.zeros((3 * PACK, D), F32)
            tapacc[...] = jnp.zeros((CONV_K * SUB, D), F32)

        def glu(r0):
            val = p_ref[pl.ds(r0, PACK), pl.ds(0, D)]
            gl = p_ref[pl.ds(r0, PACK), pl.ds(D, D)]
            abuf[pl.ds(HALO + r0, PACK), :] = val * _sigmoid(gl)
        _strips(tm, PACK,glu)

        def glu_halo(r0):
            val = ph_ref[pl.ds(r0, PACK), pl.ds(0, D)]
            gl = ph_ref[pl.ds(r0, PACK), pl.ds(D, D)]
            abuf[pl.ds(r0, PACK), :] = jnp.where(first_tile, 0.0, val * _sigmoid(gl))
        _strips(HALO, PACK,glu_halo)

        def norm_bwd(r0):
            y1 = y1_ref[pl.ds(r0, PACK), :]
            mu = _rowmean(y1)
            yc = y1 - mu
            rstd = lax.rsqrt(_rowmean(yc * yc) + EPS)
            n1 = yc * rstd
            l1 = n1 * g_ref[...] + b_ref[...]
            sg = _sigmoid(l1)
            z = p_ref[pl.ds(r0, PACK), pl.ds(2 * D, D)]
            sz = _sigmoid(z)
            dya = dya_ref[pl.ds(r0, PACK), :]
            dp_ref[pl.ds(r0, PACK), pl.ds(2 * D, D)] = (dya * (l1 * sg) * _dsilu(z, sz)).astype(BF16)
            dl1 = dya * (z * sz) * _dsilu(l1, sg)
            acc8[pl.ds(0, PACK), :] += dl1 * n1
            acc8[pl.ds(PACK, PACK), :] += dl1
            dn1 = dl1 * g_ref[...]
            dy1 = rstd * (dn1 - _rowmean(dn1) - n1 * _rowmean(dn1 * n1))
            acc8[pl.ds(2 * PACK, PACK), :] += dy1
            dybuf[pl.ds(r0, PACK), :] = dy1
        _strips(tm, PACK,norm_bwd)

        taps_d = {CONV_K - 1 - k: (lambda lanes, k=k: wb_ref[pl.ds(k * SUB, SUB), lanes]) for k in range(CONV_K)}

        def conv_bwd_data(r0):
            for lt in range(D // LANE):
                lanes = pl.ds(lt * LANE, LANE)
                acc = _conv_taps(dybuf, r0, lt, taps_d, rows)
                for m in range(0, len(acc), PACK // SUB):
                    da = jnp.concatenate(acc[m:m + PACK // SUB], axis=0)
                    rr = pl.ds(r0 + m * SUB, PACK)
                    val = p_ref[rr, pl.ds(lt * LANE, LANE)]
                    sg = _sigmoid(p_ref[rr, pl.ds(D + lt * LANE, LANE)])
                    dp_ref[rr, lanes] = (da * sg).astype(BF16)
                    dp_ref[rr, pl.ds(D + lt * LANE, LANE)] = (da * val * sg * (1.0 - sg)).astype(BF16)
        _strips(tm, rows, conv_bwd_data)

        n_out = rows // SUB

        def conv_bwd_w(r0):
            for lt in range(D // LANE):
                lanes = pl.ds(lt * LANE, LANE)
                win = abuf[pl.ds(r0, rows + HALO), lanes]
                dy = [dybuf[pl.ds(r0 + m * SUB, SUB), lanes] for m in range(n_out)]
                for phase in range(SUB):
                    ks = [k for k in range(CONV_K) if (HALO - (CONV_K - 1) + k) % SUB == phase]
                    q_max = max((HALO - (CONV_K - 1) + k) // SUB for k in ks)
                    sh = win[phase:phase + (n_out + q_max) * SUB, :]
                    for k in ks:
                        q = (HALO - (CONV_K - 1) + k) // SUB
                        part = dy[0] * sh[q * SUB:(q + 1) * SUB, :]
                        for m in range(1, n_out):
                            part = part + dy[m] * sh[(m + q) * SUB:(m + q + 1) * SUB, :]
                        tapacc[pl.ds(k * SUB, SUB), lanes] += part
        _strips(tm, rows, conv_bwd_w)

        dybuf[pl.ds(tm, HALO), :] = dybuf[pl.ds(0, HALO), :]

        @pl.when(i == n_i - 1)
        def _():
            for j in range(3):
                st_ref[pl.ds(j, 1), :] = jnp.sum(acc8[pl.ds(j * PACK, PACK), :], axis=0, keepdims=True)
            for k in range(CONV_K):
                st_ref[pl.ds(SUB + k, 1), :] = jnp.sum(tapacc[pl.ds(k * SUB, SUB), :], axis=0, keepdims=True)

    return pl.pallas_call(
        body, name="branch_a_bwd", grid=(n_i,),
        in_specs=[pl.BlockSpec((tm, 3 * D), lambda i: (tile_of(i), 0)),
                  pl.BlockSpec((HALO, 2 * D), lambda i: (jnp.maximum(tile_of(i) * halo_blocks - 1, 0), 0)),
                  pl.BlockSpec((tm, D), lambda i: (tile_of(i), 0)),
                  pl.BlockSpec((tm, D), lambda i: (tile_of(i), 0)),
                  pl.BlockSpec(memory_space=pl.ANY),
                  pl.BlockSpec((CONV_K * SUB, D), lambda i: (0, 0)), _vec_spec(1), _vec_spec(1)],
        out_specs=[pl.BlockSpec((tm, 3 * D), lambda i: (tile_of(i), 0)),
                   pl.BlockSpec((A_STATS_ROWS, D), lambda i: (0, 0))],
        out_shape=[_sds(dp.shape, BF16), _sds((A_STATS_ROWS, D), F32)],
        scratch_shapes=[pltpu.VMEM((tm + HALO, D), F32), pltpu.VMEM((tm + HALO, D), F32), pltpu.VMEM((3 * PACK, D), F32),
                        pltpu.VMEM((CONV_K * SUB, D), F32)],
        input_output_aliases={4: 0},
        compiler_params=_params(("arbitrary",)),
    )(p, p_halo_src, y1, dya_in, dp, conv_wb, ln_g, ln_b)


def _branch_b_bwd(p, dyb_in, dp, wt, wtt, bias_full, ln_g, ln_b):
    s_len = p.shape[0]
    tm = min(256, s_len)
    n_i = s_len // tm

    def body(p_ref, dyb_ref, dp_in, wt_ref, wtt_ref, bias_ref, g_ref, b_ref, dp_ref, st_ref, gbt_ref, gw_ref,
             vb, n2buf, rstdbuf, sbuf, dsb, dvbuf, acc8, gb_ref):
        del dp_in
        i = pl.program_id(0)

        @pl.when(i == 0)
        def _():
            st_ref[...] = jnp.zeros((SUB, D), F32)
            gbt_ref[...] = jnp.zeros((CHUNK, LANE), F32)
            gb_ref[...] = jnp.zeros((CHUNK, D), F32)
            gw_ref[...] = jnp.zeros((HEADS, CHUNK, CHUNK), F32)
            acc8[...] = jnp.zeros((2 * PACK, D), F32)

        def norm(r0):
            gv = _gelu(p_ref[pl.ds(r0, PACK), pl.ds(D, D)])
            mu = _rowmean(gv)
            vc = gv - mu
            rstd = lax.rsqrt(_rowmean(vc * vc) + EPS)
            n2 = vc * rstd
            n2buf[pl.ds(r0, PACK), :] = n2
            rstdbuf[pl.ds(r0, PACK), :] = jnp.broadcast_to(rstd, (PACK, LANE))
            vb[pl.ds(r0, PACK), :] = (n2 * g_ref[...] + b_ref[...]).astype(BF16)
        _strips(tm, PACK,norm)

        for ck in range(tm // CHUNK):
            for h in range(HEADS):
                blk = (pl.ds(ck * CHUNK, CHUNK), pl.ds(h * LANE, LANE))
                sbuf[blk] = jnp.dot(wt_ref[h], vb[blk], preferred_element_type=F32) + bias_ref[:, pl.ds(h * LANE, LANE)]

        def gate_bwd(r0):
            pu = p_ref[pl.ds(r0, PACK), pl.ds(0, D)]
            u = _gelu(pu)
            z = p_ref[pl.ds(r0, PACK), pl.ds(2 * D, D)]
            sg = _sigmoid(z)
            sz = z * sg
            s = sbuf[pl.ds(r0, PACK), :]
            dyb = dyb_ref[pl.ds(r0, PACK), :]
            ds = dyb * u * sz
            dsb[pl.ds(r0, PACK), :] = ds.astype(BF16)
            gb_ref[pl.ds(pl.multiple_of(r0 % CHUNK, PACK), PACK), :] += ds
            dp_ref[pl.ds(r0, PACK), pl.ds(0, D)] = (dyb * s * sz * _dgelu(pu)).astype(BF16)
            dp_ref[pl.ds(r0, PACK), pl.ds(2 * D, D)] = (dyb * u * s * _dsilu(z, sg)).astype(BF16)
        _strips(tm, PACK,gate_bwd)

        for ck in range(tm // CHUNK):
            for h in range(HEADS):
                blk = (pl.ds(ck * CHUNK, CHUNK), pl.ds(h * LANE, LANE))
                d_s = dsb[blk]
                dvbuf[blk] = jnp.dot(wtt_ref[h], d_s, preferred_element_type=F32)
                gw_ref[h] += _dot_t(d_s, vb[blk])

        def norm_bwd(r0):
            dv = dvbuf[pl.ds(r0, PACK), :]
            n2 = n2buf[pl.ds(r0, PACK), :]
            rstd = rstdbuf[pl.ds(r0, PACK), pl.ds(0, 1)]
            acc8[pl.ds(0, PACK), :] += dv * n2
            acc8[pl.ds(PACK, PACK), :] += dv
            dn2 = dv * g_ref[...]
            dgv = rstd * (dn2 - _rowmean(dn2) - n2 * _rowmean(dn2 * n2))
            dp_ref[pl.ds(r0, PACK), pl.ds(D, D)] = (dgv * _dgelu(p_ref[pl.ds(r0, PACK), pl.ds(D, D)])).astype(BF16)
        _strips(tm, PACK,norm_bwd)

        @pl.when(i == n_i - 1)
        def _():
            for j in range(2):
                st_ref[pl.ds(j, 1), :] = jnp.sum(acc8[pl.ds(j * PACK, PACK), :], axis=0, keepdims=True)
            row = lax.broadcasted_iota(jnp.int32, (CHUNK, CHUNK), 0)
            col = lax.broadcasted_iota(jnp.int32, (CHUNK, CHUNK), 1)
            for h in range(HEADS):
                gw_ref[h] = jnp.where(row >= col, gw_ref[h], 0.0)
            lane = lax.broadcasted_iota(jnp.int32, (CHUNK, LANE), 1)
            gbt = jnp.zeros((CHUNK, LANE), F32)
            for h in range(HEADS):
                gbt = jnp.where(lane == h, jnp.sum(gb_ref[:, pl.ds(h * LANE, LANE)], axis=1, keepdims=True), gbt)
            gbt_ref[...] = gbt

    wspec = pl.BlockSpec((HEADS, CHUNK, CHUNK), lambda i: (0, 0, 0))
    return pl.pallas_call(
        body, name="branch_b_bwd", grid=(n_i,),
        in_specs=[pl.BlockSpec((tm, 3 * D), lambda i: (i, 1)), pl.BlockSpec((tm, D), lambda i: (i, 0)),
                  pl.BlockSpec(memory_space=pl.ANY), wspec, wspec,
                  pl.BlockSpec((CHUNK, D), lambda i: (0, 0)), _vec_spec(1), _vec_spec(1)],
        out_specs=[pl.BlockSpec((tm, 3 * D), lambda i: (i, 1)), pl.BlockSpec((SUB, D), lambda i: (0, 0)),
                   pl.BlockSpec((CHUNK, LANE), lambda i: (0, 0)), wspec],
        out_shape=[_sds(dp.shape, BF16), _sds((SUB, D), F32), _sds((CHUNK, LANE), F32), _sds((HEADS, CHUNK, CHUNK), F32)],
        scratch_shapes=[pltpu.VMEM((tm, D), BF16), pltpu.VMEM((tm, D), F32), pltpu.VMEM((tm, LANE), F32),
                        pltpu.VMEM((tm, D), F32), pltpu.VMEM((tm, D), BF16), pltpu.VMEM((tm, D), F32),
                        pltpu.VMEM((2 * PACK, D), F32), pltpu.VMEM((CHUNK, D), F32)],
        input_output_aliases={2: 0},
        compiler_params=_params(("arbitrary",)),
    )(p, dyb_in, dp, wt, wtt, bias_full, ln_g, ln_b)


def _in_proj_bwd(dp, wg_in, x, dx2, shift, scale, g_pre):
    del shift
    s_len = x.shape[0]
    tm = min(512, s_len)
    n_i = s_len // tm
    wn = wg_in.shape[2]

    def body(dp_ref, w_ref, x_ref, dx2_ref, sc_ref, g_ref, gx_ref, st_ref, acc, acc8):
        i, j = pl.program_id(0), pl.program_id(1)

        @pl.when((i == 0) & (j == 0))
        def _():
            st_ref[...] = jnp.zeros((SUB, D), F32)
            acc8[...] = jnp.zeros((3 * PACK, D), F32)

        part = _dot_t(dp_ref[...], w_ref[0])

        @pl.when(j == 0)
        def _():
            acc[...] = part

        @pl.when(j > 0)
        def _():
            acc[...] += part

        @pl.when(j == N_CHIP - 1)
        def _():
            def strip(r0):
                xs = x_ref[pl.ds(r0, PACK), :]
                r = lax.rsqrt(_rowmean(xs * xs) + EPS)
                xn = xs * r
                dh = acc[pl.ds(r0, PACK), :]
                acc8[pl.ds(0, PACK), :] += dh
                acc8[pl.ds(PACK, PACK), :] += dh * (xn * g_ref[...])
                dhp = dh * (1.0 + sc_ref[...])
                acc8[pl.ds(2 * PACK, PACK), :] += dhp * xn
                dxn = dhp * g_ref[...]
                gx_ref[pl.ds(r0, PACK), :] = dx2_ref[pl.ds(r0, PACK), :] + r * (dxn - xn * _rowmean(dxn * xn))
            _strips(tm, PACK,strip)

        @pl.when((i == n_i - 1) & (j == N_CHIP - 1))
        def _():
            for k in range(3):
                st_ref[pl.ds(k, 1), :] = jnp.sum(acc8[pl.ds(k * PACK, PACK), :], axis=0, keepdims=True)

    tile = pl.BlockSpec((tm, D), lambda i, j: (i, 0))
    return pl.pallas_call(
        body, name="in_proj_bwd", grid=(n_i, N_CHIP),
        in_specs=[pl.BlockSpec((tm, wn), lambda i, j: (i, j)), pl.BlockSpec((1, D, wn), lambda i, j: (j, 0, 0)),
                  tile, tile, _vec_spec(2), _vec_spec(2)],
        out_specs=[tile, pl.BlockSpec((SUB, D), lambda i, j: (0, 0))],
        out_shape=[_sds((s_len, D), F32), _sds((SUB, D), F32)],
        scratch_shapes=[pltpu.VMEM((tm, D), F32), pltpu.VMEM((3 * PACK, D), F32)],
        compiler_params=_params(("arbitrary", "arbitrary")),
    )(dp, wg_in, x, dx2, scale, g_pre)


def _grad_matmul(a, b, name):
    s_len, n = b.shape
    tn = min(2048, n)
    tk = min(512, s_len)
    n_k = s_len // tk

    def body(a_ref, b_ref, o_ref, ob_ref):
        k = pl.program_id(1)
        part = lax.dot_general(a_ref[...], b_ref[...], (((0,), (0,)), ((), ())), preferred_element_type=F32)

        @pl.when(k == 0)
        def _():
            o_ref[0] = part

        @pl.when(k > 0)
        def _():
            o_ref[0] += part

        @pl.when(k == n_k - 1)
        def _():
            ob_ref[0] = o_ref[0].astype(BF16)

    out_spec = pl.BlockSpec((1, D, tn), lambda j, k: (j, 0, 0))
    return pl.pallas_call(
        body, name=name, grid=(n // tn, n_k),
        in_specs=[pl.BlockSpec((tk, D), lambda j, k: (k, 0)), pl.BlockSpec((tk, tn), lambda j, k: (k, j))],
        out_specs=[out_spec, out_spec],
        out_shape=[_sds((n // tn, D, tn), F32), _sds((n // tn, D, tn), BF16)],
        compiler_params=_params(("arbitrary", "arbitrary")),
    )(a, b)


def _local_step(x, target, shift, scale, gate, g_pre, conv_w_full, conv_b, conv_ln_g, conv_ln_b,
                sgu_ln_g, sgu_ln_b, w_sgu, b_sgu, g_final, wg_in, w_co, w_so, w_o):
    conv_wb = jnp.repeat(conv_w_full, SUB, axis=0)
    causal = jnp.tril(jnp.ones((CHUNK, CHUNK), dtype=bool))
    wt = jnp.where(causal[None], w_sgu, 0.0).astype(BF16)
    wtt = jnp.swapaxes(wt, 1, 2)
    bias_full = jnp.repeat(b_sgu.T, LANE, axis=1)

    p, hb = _in_proj(x, shift, scale, g_pre, wg_in)
    ya_in, y1 = _branch_a_fwd(p, conv_wb, conv_b, conv_ln_g, conv_ln_b)
    yb_in = _branch_b_fwd(p, wt, bias_full, sgu_ln_g, sgu_ln_b)
    dx2, dya_in, dyb_in, dp, mb, dob, dyab, dybb, sums_o = _out_proj(
        p, ya_in, yb_in, x, target, gate, g_final, w_co, w_so, w_o)
    dp, st_a = _branch_a_bwd(p, p, y1, dya_in, dp, conv_wb, conv_ln_g, conv_ln_b)
    dp, st_b, gbt, gws = _branch_b_bwd(p, dyb_in, dp, wt, wtt, bias_full, sgu_ln_g, sgu_ln_b)
    grad_x, st_i = _in_proj_bwd(dp, wg_in, x, dx2, shift, scale, g_pre)
    gw_in = _grad_matmul(hb, dp, "grad_w_in")
    gw_o = _grad_matmul(mb, dob, "grad_w_o")
    gw_co = _grad_matmul(ya_in, dyab, "grad_w_conv_out")
    gw_so = _grad_matmul(yb_in, dybb, "grad_w_sgu_out")
    return dict(
        grad_x=grad_x, loss_cols=sums_o[2:3], g_final=sums_o[0:1], d_gate=sums_o[1:2],
        d_shift=st_i[0:1], d_scale=st_i[1:2], g_pre=st_i[2:3],
        conv_ln_g=st_a[0:1], conv_ln_b=st_a[1:2], conv_b=st_a[2:3], conv_w=st_a[SUB:SUB + CONV_K],
        sgu_ln_g=st_b[0:1], sgu_ln_b=st_b[1:2], b_sgu=gbt[:, :HEADS].T, w_sgu=gws,
        w_in=gw_in, w_o=gw_o, w_conv_out=gw_co, w_sgu_out=gw_so)


ANY_SPEC = pl.BlockSpec(memory_space=pl.ANY)
VMEM_SPEC = pl.BlockSpec(memory_space=pltpu.VMEM)


def _place():
    return lax.axis_index("x"), lax.axis_index("y"), lax.axis_index("c")


def _peer(k):
    x, y, c = _place()
    return (1 - x if k & 4 else x, 1 - y if k & 2 else y, 1 - c if k & 1 else c)


def _dev_of(p):
    return 4 * p[0] + 2 * p[1] + p[2]


def _chip_of(p):
    return 2 * p[0] + p[1]


def _rdma(src, dst, send_sem, recv_sem, to):
    return pltpu.make_async_remote_copy(src_ref=src, dst_ref=dst, send_sem=send_sem, recv_sem=recv_sem,
                                        device_id=to, device_id_type=MESH)


CHIP_PEERS = (2, 4, 6)
ALL_PEERS = tuple(range(1, N_DEV))
SIBLING = 1


def _setup_comm(c8, w_ada_s, b_ada_s, convw_s):
    n_mod = w_ada_s.shape[1]
    rows = SUB * N_DEV

    def body(c8_ref, wada_ref, bada_ref, cw_ref, call_ref, mod_ref, cwall_ref, csend, crecv, wsend, wrecv, msend, mrecv):
        me = _place()
        dev, chip = _dev_of(me), _chip_of(me)

        def c_rows(d):
            return call_ref.at[pl.ds(pl.multiple_of(d * SUB, SUB), SUB), :]

        call_ref[pl.ds(pl.multiple_of(dev * SUB, SUB), SUB), :] = c8_ref[...]
        cwall_ref[chip] = cw_ref[...]
        c_out = [_rdma(c8_ref, c_rows(dev), csend.at[k], crecv.at[k], _peer(k)) for k in ALL_PEERS]
        w_out = [_rdma(cw_ref, cwall_ref.at[chip], wsend.at[k], wrecv.at[k], _peer(k)) for k in CHIP_PEERS]
        for cp in c_out + w_out:
            cp.start()
        for k in ALL_PEERS:
            _rdma(c8_ref, c_rows(_dev_of(_peer(k))), csend.at[k], crecv.at[k], _peer(k)).wait_recv()
        part = jnp.dot(call_ref[...].astype(BF16), wada_ref[...].astype(BF16), preferred_element_type=F32) + bada_ref[...]
        mod_ref[chip] = part
        m_out = [_rdma(mod_ref.at[chip], mod_ref.at[chip], msend.at[k], mrecv.at[k], _peer(k)) for k in CHIP_PEERS]
        for cp in m_out:
            cp.start()
        for k in CHIP_PEERS:
            pc = _chip_of(_peer(k))
            _rdma(cw_ref, cwall_ref.at[pc], wsend.at[k], wrecv.at[k], _peer(k)).wait_recv()
            _rdma(mod_ref.at[pc], mod_ref.at[pc], msend.at[k], mrecv.at[k], _peer(k)).wait_recv()
        for cp in c_out + w_out + m_out:
            cp.wait_send()

    return pl.pallas_call(
        body, name="setup_comm",
        in_specs=[VMEM_SPEC] * 4, out_specs=[VMEM_SPEC] * 3,
        out_shape=[_sds((rows, D), F32), _sds((N_CHIP, rows, n_mod), F32), _sds((N_CHIP,) + convw_s.shape, F32)],
        scratch_shapes=[pltpu.SemaphoreType.DMA((N_DEV,))] * 6,
        compiler_params=_params(),
    )(c8, w_ada_s, b_ada_s, convw_s)


def _gather_weights(shards):
    n = len(shards)

    def body(*refs):
        ins, outs = refs[:n], refs[n:2 * n]
        lsem, isend, irecv, dsend, drecv = refs[2 * n:]
        me = _place()
        chip, c = _chip_of(me), me[2]
        local = [pltpu.make_async_copy(ins[t], outs[t].at[chip], lsem.at[t]) for t in range(n)]
        for cp in local:
            cp.start()

        def half(t, which):
            hr = shards[t].shape[0] // 2
            return pl.ds(pl.multiple_of(which * hr, hr), hr)

        sends = []
        for t in range(n):
            for j, k in enumerate(CHIP_PEERS):
                cp = _rdma(ins[t].at[half(t, c)], outs[t].at[chip, half(t, c)], isend.at[t, j], irecv.at[t, j], _peer(k))
                cp.start()
                sends.append(cp)
        for t in range(n):
            for j, k in enumerate(CHIP_PEERS):
                blk = outs[t].at[_chip_of(_peer(k)), half(t, c)]
                _rdma(blk, blk, isend.at[t, j], irecv.at[t, j], _peer(k)).wait_recv()
                cp = _rdma(blk, blk, dsend.at[t, j], drecv.at[t, j], _peer(SIBLING))
                cp.start()
                sends.append(cp)
        for t in range(n):
            for j, k in enumerate(CHIP_PEERS):
                blk = outs[t].at[_chip_of(_peer(k)), half(t, 1 - c)]
                _rdma(blk, blk, dsend.at[t, j], drecv.at[t, j], _peer(SIBLING)).wait_recv()
        for cp in sends:
            cp.wait_send()
        for cp in local:
            cp.wait()

    return pl.pallas_call(
        body, name="gather_weights",
        in_specs=[ANY_SPEC] * n, out_specs=[ANY_SPEC] * n,
        out_shape=[_sds((N_CHIP,) + s.shape, s.dtype) for s in shards],
        scratch_shapes=[pltpu.SemaphoreType.DMA((n,))] + [pltpu.SemaphoreType.DMA((n, len(CHIP_PEERS)))] * 4,
        compiler_params=_params(),
    )(*shards)


def _scatter_grads(grads):
    n = len(grads)

    def body(*refs):
        ins, outs = refs[:n], refs[n:2 * n]
        lsem, send, recv = refs[2 * n:]
        me = _place()
        dev, chip, c = _dev_of(me), _chip_of(me), me[2]
        local = [pltpu.make_async_copy(ins[t].at[chip, c], outs[t].at[dev], lsem.at[t]) for t in range(n)]
        for cp in local:
            cp.start()
        sends = []
        for t in range(n):
            for k in ALL_PEERS:
                to = _peer(k)
                cp = _rdma(ins[t].at[_chip_of(to), to[2]], outs[t].at[dev], send.at[t, k], recv.at[t, k], to)
                cp.start()
                sends.append(cp)
        for t in range(n):
            for k in ALL_PEERS:
                blk = outs[t].at[_dev_of(_peer(k))]
                _rdma(blk, blk, send.at[t, k], recv.at[t, k], _peer(k)).wait_recv()
        for cp in sends:
            cp.wait_send()
        for cp in local:
            cp.wait()

    return pl.pallas_call(
        body, name="scatter_grads",
        in_specs=[ANY_SPEC] * n, out_specs=[ANY_SPEC] * n,
        out_shape=[_sds((N_DEV,) + g.shape[2:], g.dtype) for g in grads],
        scratch_shapes=[pltpu.SemaphoreType.DMA((n,))] + [pltpu.SemaphoreType.DMA((n, N_DEV))] * 2,
        compiler_params=_params(),
    )(*grads)


def _sum_devices(parts, name):
    _, r, cols = parts.shape
    tr = min(r, 128)

    def body(in_ref, o_ref):
        acc = in_ref[0].astype(F32)
        for d in range(1, N_DEV):
            acc = acc + in_ref[d].astype(F32)
        o_ref[...] = acc

    return pl.pallas_call(
        body, name=name, grid=(r // tr,),
        in_specs=[pl.BlockSpec((N_DEV, tr, cols), lambda i: (0, i, 0))],
        out_specs=pl.BlockSpec((tr, cols), lambda i: (i, 0)),
        out_shape=_sds((r, cols), F32),
        compiler_params=_params(("arbitrary",)),
    )(parts)


def _share_halves(reds):
    n = len(reds)

    def body(*refs):
        ins, outs = refs[:n], refs[n:2 * n]
        lsem, send, recv = refs[2 * n:]
        me = _place()
        c = me[2]
        local = [pltpu.make_async_copy(ins[t], outs[t].at[c], lsem.at[t]) for t in range(n)]
        sends = [_rdma(ins[t], outs[t].at[c], send.at[t], recv.at[t], _peer(SIBLING)) for t in range(n)]
        for cp in local + sends:
            cp.start()
        for t in range(n):
            _rdma(ins[t], outs[t].at[1 - c], send.at[t], recv.at[t], _peer(SIBLING)).wait_recv()
        for cp in sends:
            cp.wait_send()
        for cp in local:
            cp.wait()

    return pl.pallas_call(
        body, name="share_halves",
        in_specs=[ANY_SPEC] * n, out_specs=[ANY_SPEC] * n,
        out_shape=[_sds((2,) + r.shape, r.dtype) for r in reds],
        scratch_shapes=[pltpu.SemaphoreType.DMA((n,))] * 3,
        compiler_params=_params(),
    )(*reds)


def _sum_small(blob):
    rows = blob.shape[0]

    def body(b_ref, o_ref, pbuf, buf4, psend, precv, send, recv):
        me = _place()
        chip = _chip_of(me)
        pair = _rdma(b_ref, pbuf, psend, precv, _peer(SIBLING))
        pair.start()
        pair.wait()
        buf4[chip] = b_ref[...] + pbuf[...]
        out = [_rdma(buf4.at[chip], buf4.at[chip], send.at[k], recv.at[k], _peer(k)) for k in CHIP_PEERS]
        for cp in out:
            cp.start()
        for k in CHIP_PEERS:
            blk = buf4.at[_chip_of(_peer(k))]
            _rdma(blk, blk, send.at[k], recv.at[k], _peer(k)).wait_recv()
        o_ref[...] = (buf4[0] + buf4[1]) + (buf4[2] + buf4[3])
        for cp in out:
            cp.wait_send()

    return pl.pallas_call(
        body, name="sum_small",
        in_specs=[VMEM_SPEC], out_specs=VMEM_SPEC, out_shape=_sds(blob.shape, F32),
        scratch_shapes=[pltpu.VMEM((rows, D), F32), pltpu.VMEM((N_CHIP, rows, D), F32),
                        pltpu.SemaphoreType.DMA, pltpu.SemaphoreType.DMA,
                        pltpu.SemaphoreType.DMA((N_DEV,)), pltpu.SemaphoreType.DMA((N_DEV,))],
        compiler_params=_params(),
    )(blob)


def _adamw_math(w, g, m, v):
    m = ADAM_B1 * m + (1.0 - ADAM_B1) * g
    v = ADAM_B2 * v + (1.0 - ADAM_B2) * (g * g)
    m_hat = m / (1.0 - ADAM_B1 ** ADAM_STEP)
    v_hat = v / (1.0 - ADAM_B2 ** ADAM_STEP)
    delta = -ADAM_LR * (m_hat / (jnp.sqrt(v_hat) + ADAM_EPS) + ADAM_WD * w)
    return delta, m, v


def _row_tile(r, cols):
    if r * cols * 4 <= 2 ** 20:
        return r
    return next(t for t in (512, 256, 128, 64, 32, 16, 8) if r % t == 0 and t * cols * 4 <= 2 ** 20)


def _adamw(w, g, m, v, name):
    r, cols = w.shape
    tr = _row_tile(r, cols)

    def body(w_ref, g_ref, m_ref, v_ref, d_ref, nm_ref, nv_ref):
        d_ref[...], nm_ref[...], nv_ref[...] = _adamw_math(w_ref[...], g_ref[...], m_ref[...], v_ref[...])

    spec = pl.BlockSpec((tr, cols), lambda i: (i, 0))
    return pl.pallas_call(
        body, name=name, grid=(r // tr,), in_specs=[spec] * 4, out_specs=[spec] * 3,
        out_shape=[_sds((r, cols), F32)] * 3, compiler_params=_params(("arbitrary",)),
    )(w, g, m, v)


def _adamw_ada(w, ct, dm, m, v):
    r, cols = w.shape
    tr = _row_tile(r, cols)

    def body(w_ref, ct_ref, dm_ref, m_ref, v_ref, g_ref, d_ref, nm_ref, nv_ref):
        g = jnp.dot(ct_ref[...], dm_ref[...], preferred_element_type=F32)
        g_ref[...] = g
        d_ref[...], nm_ref[...], nv_ref[...] = _adamw_math(w_ref[...], g, m_ref[...], v_ref[...])

    spec = pl.BlockSpec((tr, cols), lambda i: (i, 0))
    return pl.pallas_call(
        body, name="adamw_ada", grid=(r // tr,),
        in_specs=[spec, pl.BlockSpec((tr, LANE), lambda i: (i, 0)), pl.BlockSpec((LANE, cols), lambda i: (0, 0)), spec, spec],
        out_specs=[spec] * 4, out_shape=[_sds((r, cols), F32)] * 4, compiler_params=_params(("arbitrary",)),
    )(w, ct, dm, m, v)


BLOB_VEC, BLOB_CONV, BLOB_SGU, BLOB_ADA, BLOB_DMOD, BLOB_LOSS, BLOB_ROWS = 0, 8, 40, 168, 176, 200, 208
SMALL_CONV, SMALL_SGU, SMALL_ADA, SMALL_ROWS = 8, 16, 144, 152


def _set_rows(buf, row, val):
    return lax.dynamic_update_slice(buf, val.astype(F32), (row, 0))


def _pack_small(vecs, b_sgu, conv_w_s, w_sgu, b_ada):
    buf = jnp.zeros((SMALL_ROWS, D), F32)
    for i, vec in enumerate(vecs):
        buf = _set_rows(buf, i, vec.reshape(1, D))
    buf = _set_rows(buf, 7, b_sgu.reshape(1, D))
    buf = _set_rows(buf, SMALL_CONV, jnp.pad(conv_w_s, ((0, 1), (0, 0))).reshape(SUB, D))
    buf = _set_rows(buf, SMALL_SGU, w_sgu.reshape(CHUNK, D))
    return _set_rows(buf, SMALL_ADA, b_ada.reshape(3, D))


def _unpack_small(buf, conv_cols):
    vecs = [buf[i] for i in range(7)]
    b_sgu = buf[7].reshape(HEADS, CHUNK)
    conv_w_s = buf[SMALL_CONV:SMALL_CONV + SUB].reshape(HALO, conv_cols)[:CONV_K]
    w_sgu = buf[SMALL_SGU:SMALL_SGU + CHUNK].reshape(HEADS, CHUNK, CHUNK)
    b_ada = buf[SMALL_ADA:SMALL_ADA + 3].reshape(1, 3 * D)
    return vecs, b_sgu, conv_w_s, w_sgu, b_ada


def kernel(x, c, w_ada, b_ada, g_pre, w_in, conv_w, conv_b, conv_ln_g, conv_ln_b, w_conv_out, sgu_ln_g, sgu_ln_b, w_sgu, b_sgu, w_sgu_out, w_o, g_final, loss_target, m_w_ada, m_b_ada, m_g_pre, m_w_in, m_conv_w, m_conv_b, m_conv_ln_g, m_conv_ln_b, m_w_conv_out, m_sgu_ln_g, m_sgu_ln_b, m_w_sgu, m_b_sgu, m_w_sgu_out, m_w_o, m_g_final, v_w_ada, v_b_ada, v_g_pre, v_w_in, v_conv_w, v_conv_b, v_conv_ln_g, v_conv_ln_b, v_w_conv_out, v_sgu_ln_g, v_sgu_ln_b, v_w_sgu, v_b_sgu, v_w_sgu_out, v_w_o, v_g_final):
    me = _place()
    dev, chip = _dev_of(me), _chip_of(me)
    n_ada = w_ada.shape[2]
    conv_cols = conv_w.shape[2]

    b_ada_s = lax.dynamic_slice(b_ada, (0, chip * n_ada), (1, n_ada))
    c_all, mod_all, cw_all = _setup_comm(
        jnp.broadcast_to(c, (SUB, D)), w_ada[0], b_ada_s, jnp.pad(conv_w[0], ((0, HALO - CONV_K), (0, 0))))
    mod = lax.dynamic_slice(mod_all, (0, dev * SUB, 0), (N_CHIP, 1, n_ada)).reshape(1, 3 * D)
    shift, scale, gate = mod[:, :D], mod[:, D:2 * D], mod[:, 2 * D:]
    conv_w_full = jnp.swapaxes(cw_all, 0, 1).reshape(HALO, D)[:CONV_K]

    wg_in, wg_co, wg_so, wg_o = _gather_weights(
        [w_in[0].astype(BF16), w_conv_out[0].astype(BF16), w_sgu_out[0].astype(BF16), w_o[0].astype(BF16)])

    loc = _local_step(x[0], loss_target[0], shift, scale, gate, g_pre, conv_w_full, conv_b, conv_ln_g, conv_ln_b,
                      sgu_ln_g, sgu_ln_b, w_sgu[0], b_sgu[0], g_final.reshape(1, D),
                      wg_in, wg_co.reshape(D, D), wg_so.reshape(D, D), wg_o.reshape(D, D))

    big = ["w_in", "w_conv_out", "w_sgu_out", "w_o"]
    contrib = []
    for name in big:
        g16 = loc[name][1]
        rows_half = (g16.shape[0] * g16.shape[1]) // (2 * N_CHIP) if name != "w_in" else g16.shape[1] // 2
        contrib.append(g16.reshape(N_CHIP, 2, rows_half, g16.shape[2]))
    parts = _scatter_grads(contrib)
    reds = [_sum_devices(p, "sum_" + name) for p, name in zip(parts, big)]
    full = _share_halves(reds)
    g_big = {name: f.reshape(2 * f.shape[1], f.shape[2]) for name, f in zip(big, full)}

    d_mod = jnp.concatenate([loc["d_shift"], loc["d_scale"], loc["d_gate"]], axis=0)
    blob = jnp.zeros((BLOB_ROWS, D), F32)
    for i, name in enumerate(["g_pre", "conv_b", "conv_ln_g", "conv_ln_b", "sgu_ln_g", "sgu_ln_b", "g_final"]):
        blob = _set_rows(blob, BLOB_VEC + i, loc[name])
    blob = _set_rows(blob, BLOB_VEC + 7, loc["b_sgu"].reshape(1, D))
    blob = _set_rows(blob, BLOB_CONV, loc["conv_w"])
    blob = _set_rows(blob, BLOB_SGU, loc["w_sgu"].reshape(CHUNK, D))
    blob = _set_rows(blob, BLOB_ADA, d_mod)
    blob = lax.dynamic_update_slice(blob, d_mod, (BLOB_DMOD + 3 * dev, 0))
    blob = _set_rows(blob, BLOB_LOSS, loc["loss_cols"])
    tot = _sum_small(blob)

    loss = jnp.sum(tot[BLOB_LOSS])
    g_vecs = [tot[BLOB_VEC + i] for i in range(7)]
    g_b_sgu = tot[BLOB_VEC + 7].reshape(HEADS, CHUNK)
    g_conv_s = lax.dynamic_slice(tot, (BLOB_CONV, chip * conv_cols), (CONV_K, conv_cols))
    g_w_sgu = tot[BLOB_SGU:BLOB_SGU + CHUNK].reshape(HEADS, CHUNK, CHUNK)
    g_b_ada = tot[BLOB_ADA:BLOB_ADA + 3].reshape(1, 3 * D)
    d_mod_all = tot[BLOB_DMOD:BLOB_DMOD + 3 * N_DEV].reshape(N_DEV, 3 * D)

    ct = jnp.pad(c_all[::SUB].T, ((0, 0), (0, LANE - N_DEV))).astype(BF16)
    dm = jnp.pad(lax.dynamic_slice(d_mod_all, (0, chip * n_ada), (N_DEV, n_ada)), ((0, LANE - N_DEV), (0, 0))).astype(BF16)
    g_ada, d_ada, nm_ada, nv_ada = _adamw_ada(w_ada[0], ct, dm, m_w_ada[0], v_w_ada[0])

    upd = {}
    for name, w, m, v in [("w_in", w_in, m_w_in, v_w_in), ("w_conv_out", w_conv_out, m_w_conv_out, v_w_conv_out),
                          ("w_sgu_out", w_sgu_out, m_w_sgu_out, v_w_sgu_out), ("w_o", w_o, m_w_o, v_w_o)]:
        upd[name] = _adamw(w[0], g_big[name], m[0], v[0], "adamw_" + name)

    def small(g_p, conv_p, sgu_p, ada_p, vec_ps):
        return _pack_small(vec_ps, g_p, conv_p, sgu_p, ada_p)

    w_s = small(b_sgu[0], conv_w[0], w_sgu[0], b_ada, [g_pre, conv_b, conv_ln_g, conv_ln_b, sgu_ln_g, sgu_ln_b, g_final])
    m_s = small(m_b_sgu[0], m_conv_w[0], m_w_sgu[0], m_b_ada,
                [m_g_pre, m_conv_b, m_conv_ln_g, m_conv_ln_b, m_sgu_ln_g, m_sgu_ln_b, m_g_final])
    v_s = small(v_b_sgu[0], v_conv_w[0], v_w_sgu[0], v_b_ada,
                [v_g_pre, v_conv_b, v_conv_ln_g, v_conv_ln_b, v_sgu_ln_g, v_sgu_ln_b, v_g_final])
    g_s = small(g_b_sgu, g_conv_s, g_w_sgu, g_b_ada, g_vecs)
    small_out = [_unpack_small(a, conv_cols) for a in (g_s,) + tuple(_adamw(w_s, g_s, m_s, v_s, "adamw_small"))]

    def leaves(kind):
        vecs, o_b_sgu, o_conv, o_w_sgu, o_b_ada = small_out[kind]
        ada = (g_ada, d_ada, nm_ada, nv_ada)[kind]
        def bigk(name):
            return (g_big[name] if kind == 0 else upd[name][kind - 1])[None]
        return [ada[None], o_b_ada, vecs[0][None], bigk("w_in"), o_conv[None], vecs[1][None], vecs[2][None], vecs[3][None],
                bigk("w_conv_out"), vecs[4][None], vecs[5][None], o_w_sgu[None], o_b_sgu[None], bigk("w_sgu_out"),
                bigk("w_o"), vecs[6]]

    return (loss, loc["grad_x"][None], *leaves(0), *leaves(1), *leaves(2), *leaves(3))
```

```python
import functools

import jax
import jax.numpy as jnp
from jax import lax
from jax.experimental import pallas as pl
from jax.experimental.pallas import tpu as pltpu

F32 = jnp.float32
BF16 = jnp.bfloat16
MESH = pl.DeviceIdType.MESH

D = 1024
N_SEC = 8
N_CHIP = 4
N_DEV = 8
EPS = 1e-6
CONV_K = 31
HALO = 32
CHUNK = 128
HEADS = 8
LANE = 128
SUB = 8
PACK = 16
VMEM_LIMIT = 56 * 1024 * 1024

ADAM_LR, ADAM_B1, ADAM_B2, ADAM_EPS, ADAM_WD, ADAM_STEP = 0.001, 0.9, 0.999, 1e-08, 0.01, 10

_SQRT_HALF = 0.7071067811865476
_INV_SQRT_2PI = 0.3989422804014327


def _sds(shape, dtype):
    return jax.ShapeDtypeStruct(shape, dtype)


def _params(sem=None):
    if sem is None:
        return pltpu.CompilerParams(vmem_limit_bytes=VMEM_LIMIT)
    return pltpu.CompilerParams(dimension_semantics=sem, vmem_limit_bytes=VMEM_LIMIT)


def _strips(n_rows, rows, fn):
    def step(s, carry):
        fn(pl.multiple_of(s * rows, rows))
        return carry
    lax.fori_loop(0, n_rows // rows, step, 0)


def _sigmoid(v):
    return 1.0 / (1.0 + jnp.exp(-v))


def _gelu(v):
    return 0.5 * v * (1.0 + lax.erf(v * _SQRT_HALF))


def _dgelu(v):
    return 0.5 * (1.0 + lax.erf(v * _SQRT_HALF)) + v * jnp.exp(-0.5 * v * v) * _INV_SQRT_2PI


def _dsilu(v, sg):
    return sg * (1.0 + v * (1.0 - sg))


def _rowmean(v):
    return jnp.mean(v, axis=-1, keepdims=True)


def _vec_spec(grid_rank):
    zeros = (0, 0)
    if grid_rank == 1:
        return pl.BlockSpec((1, D), lambda i: zeros)
    return pl.BlockSpec((1, D), lambda i, j: zeros)


def _in_proj(x, shift, scale, g_pre, wg_in):
    s_len = x.shape[0]
    tm = min(512, s_len)
    n_i = s_len // tm
    wn = wg_in.shape[2]

    def body(x_ref, sh_ref, sc_ref, g_ref, w_ref, p_ref, hb_ref):
        @pl.when(pl.program_id(1) == 0)
        def _():
            def strip(r0):
                xs = x_ref[pl.ds(r0, PACK), :]
                r = lax.rsqrt(_rowmean(xs * xs) + EPS)
                h = (xs * r) * g_ref[...] * (1.0 + sc_ref[...]) + sh_ref[...]
                hb_ref[pl.ds(r0, PACK), :] = h.astype(BF16)
            _strips(tm, PACK, strip)
        p_ref[...] = jnp.dot(hb_ref[...], w_ref[0], preferred_element_type=F32).astype(BF16)

    return pl.pallas_call(
        body, name="in_proj", grid=(n_i, N_CHIP),
        in_specs=[pl.BlockSpec((tm, D), lambda i, j: (i, 0)), _vec_spec(2), _vec_spec(2), _vec_spec(2),
                  pl.BlockSpec((1, D, wn), lambda i, j: (j, 0, 0))],
        out_specs=[pl.BlockSpec((tm, wn), lambda i, j: (i, j)), pl.BlockSpec((tm, D), lambda i, j: (i, 0))],
        out_shape=[_sds((s_len, N_SEC * D), BF16), _sds((s_len, D), BF16)],
        compiler_params=_params(("arbitrary", "arbitrary")),
    )(x, shift, scale, g_pre, wg_in)


def _conv_taps(win_ref, r0, lt, weight_of_offset, rows):
    lanes = pl.ds(lt * LANE, LANE)
    win = win_ref[pl.ds(r0, rows + HALO), lanes]
    n_out = rows // SUB
    acc = [jnp.zeros((SUB, LANE), F32) for _ in range(n_out)]
    for phase in range(SUB):
        offs = [o for o in weight_of_offset if o % SUB == phase]
        if not offs:
            continue
        q_max = max(o // SUB for o in offs)
        span = (n_out + q_max) * SUB
        sh = win[phase:phase + span, :]
        for o in offs:
            q = o // SUB
            w = weight_of_offset[o](lanes)
            for m in range(n_out):
                acc[m] = acc[m] + w * sh[(m + q) * SUB:(m + q + 1) * SUB, :]
    return acc


def _branch_a_fwd(p, conv_wb, conv_b, ln_g, ln_b):
    s_len = p.shape[0]
    tm = min(256, s_len)
    n_i = s_len // tm
    rows = 32

    def body(p_ref, wb_ref, cb_ref, g_ref, b_ref, ya_ref, y1_ref, abuf):
        @pl.when(pl.program_id(0) == 0)
        def _():
            abuf[pl.ds(0, HALO), :] = jnp.zeros((HALO, D), F32)

        def glu(r0):
            val = p_ref[pl.ds(r0, PACK), pl.ds(0, D)].astype(F32)
            gl = p_ref[pl.ds(r0, PACK), pl.ds(D, D)].astype(F32)
            abuf[pl.ds(HALO + r0, PACK), :] = val * _sigmoid(gl)
        _strips(tm, PACK,glu)

        taps = {HALO - (CONV_K - 1) + k: (lambda lanes, k=k: wb_ref[pl.ds(k * SUB, SUB), lanes]) for k in range(CONV_K)}

        def conv(r0):
            for lt in range(D // LANE):
                acc = _conv_taps(abuf, r0, lt, taps, rows)
                cb = cb_ref[:, pl.ds(lt * LANE, LANE)]
                for m, v in enumerate(acc):
                    y1_ref[pl.ds(r0 + m * SUB, SUB), pl.ds(lt * LANE, LANE)] = v + cb
        _strips(tm, rows, conv)

        def norm(r0):
            y1 = y1_ref[pl.ds(r0, PACK), :]
            mu = _rowmean(y1)
            yc = y1 - mu
            rstd = lax.rsqrt(_rowmean(yc * yc) + EPS)
            l1 = (yc * rstd) * g_ref[...] + b_ref[...]
            z = p_ref[pl.ds(r0, PACK), pl.ds(2 * D, D)].astype(F32)
            ya_ref[pl.ds(r0, PACK), :] = ((l1 * _sigmoid(l1)) * (z * _sigmoid(z))).astype(BF16)
        _strips(tm, PACK,norm)

        abuf[pl.ds(0, HALO), :] = abuf[pl.ds(tm, HALO), :]

    return pl.pallas_call(
        body, name="branch_a_fwd", grid=(n_i,),
        in_specs=[pl.BlockSpec((tm, 3 * D), lambda i: (i, 0)),
                  pl.BlockSpec((CONV_K * SUB, D), lambda i: (0, 0)), _vec_spec(1), _vec_spec(1), _vec_spec(1)],
        out_specs=[pl.BlockSpec((tm, D), lambda i: (i, 0)), pl.BlockSpec((tm, D), lambda i: (i, 0))],
        out_shape=[_sds((s_len, D), BF16), _sds((s_len, D), F32)],
        scratch_shapes=[pltpu.VMEM((tm + HALO, D), F32)],
        compiler_params=_params(("arbitrary",)),
    )(p, conv_wb, conv_b, ln_g, ln_b)


def _branch_b_fwd(p, wt, bias_full, ln_g, ln_b):
    s_len = p.shape[0]
    tm = min(256, s_len)
    n_i = s_len // tm

    def body(p_ref, wt_ref, bias_ref, g_ref, b_ref, yb_ref, vb, sbuf):
        def norm(r0):
            gv = _gelu(p_ref[pl.ds(r0, PACK), pl.ds(D, D)].astype(F32))
            mu = _rowmean(gv)
            vc = gv - mu
            rstd = lax.rsqrt(_rowmean(vc * vc) + EPS)
            vb[pl.ds(r0, PACK), :] = ((vc * rstd) * g_ref[...] + b_ref[...]).astype(BF16)
        _strips(tm, PACK,norm)

        for ck in range(tm // CHUNK):
            for h in range(HEADS):
                blk = (pl.ds(ck * CHUNK, CHUNK), pl.ds(h * LANE, LANE))
                sbuf[blk] = jnp.dot(wt_ref[h], vb[blk], preferred_element_type=F32) + bias_ref[:, pl.ds(h * LANE, LANE)]

        def gate(r0):
            u = _gelu(p_ref[pl.ds(r0, PACK), pl.ds(0, D)].astype(F32))
            z = p_ref[pl.ds(r0, PACK), pl.ds(2 * D, D)].astype(F32)
            yb_ref[pl.ds(r0, PACK), :] = (u * sbuf[pl.ds(r0, PACK), :] * (z * _sigmoid(z))).astype(BF16)
        _strips(tm, PACK,gate)

    return pl.pallas_call(
        body, name="branch_b_fwd", grid=(n_i,),
        in_specs=[pl.BlockSpec((tm, 3 * D), lambda i: (i, 1)),
                  pl.BlockSpec((HEADS, CHUNK, CHUNK), lambda i: (0, 0, 0)),
                  pl.BlockSpec((CHUNK, D), lambda i: (0, 0)), _vec_spec(1), _vec_spec(1)],
        out_specs=pl.BlockSpec((tm, D), lambda i: (i, 0)),
        out_shape=_sds((s_len, D), BF16),
        scratch_shapes=[pltpu.VMEM((tm, D), BF16), pltpu.VMEM((tm, D), F32)],
        compiler_params=_params(("arbitrary",)),
    )(p, wt, bias_full, ln_g, ln_b)


def _dot_t(a, b):
    return lax.dot_general(a, b, (((1,), (1,)), ((), ())), preferred_element_type=F32)


def _out_proj(p, ya_in, yb_in, x, target, gate, g_final, w_co, w_so, w_o):
    s_len = x.shape[0]
    tm = min(256, s_len)
    n_i = s_len // tm

    def body(pg_ref, ya_ref, yb_ref, x_ref, t_ref, gate_ref, gf_ref, wco_ref, wso_ref, wo_ref,
             dx2_ref, dya_ref, dyb_ref, dp_ref, mb_ref, dob_ref, dyab_ref, dybb_ref, sums_ref):
        @pl.when(pl.program_id(0) == 0)
        def _():
            sums_ref[...] = jnp.zeros((SUB, D), F32)

        y_a = jnp.dot(ya_ref[...], wco_ref[...], preferred_element_type=F32)
        y_b = jnp.dot(yb_ref[...], wso_ref[...], preferred_element_type=F32)
        ga = _sigmoid(pg_ref[:, pl.ds(0, D)].astype(F32))
        gb = _sigmoid(pg_ref[:, pl.ds(D, D)].astype(F32))
        mb = (ga * y_a + gb * y_b).astype(BF16)
        mb_ref[...] = mb
        o = jnp.dot(mb, wo_ref[...], preferred_element_type=F32)
        x2 = x_ref[...] + gate_ref[...] * o
        r2 = lax.rsqrt(_rowmean(x2 * x2) + EPS)
        xh = x2 * r2
        e = xh * gf_ref[...] - t_ref[...]
        dy = e * (1.0 / D)
        dxh = dy * gf_ref[...]
        dx2 = r2 * (dxh - xh * _rowmean(dxh * xh))
        dx2_ref[...] = dx2
        sums_ref[pl.ds(0, 1), :] += jnp.sum(dy * xh, axis=0, keepdims=True)
        sums_ref[pl.ds(1, 1), :] += jnp.sum(dx2 * o, axis=0, keepdims=True)
        sums_ref[pl.ds(2, 1), :] += jnp.sum(e * e, axis=0, keepdims=True) * (0.5 / D)
        dob = (gate_ref[...] * dx2).astype(BF16)
        dob_ref[...] = dob
        dm = _dot_t(dob, wo_ref[...])
        dy_a = (ga * dm).astype(BF16)
        dy_b = (gb * dm).astype(BF16)
        dyab_ref[...] = dy_a
        dybb_ref[...] = dy_b
        dp_ref[:, pl.ds(0, D)] = (dm * y_a * ga * (1.0 - ga)).astype(BF16)
        dp_ref[:, pl.ds(D, D)] = (dm * y_b * gb * (1.0 - gb)).astype(BF16)
        dya_ref[...] = _dot_t(dy_a, wco_ref[...])
        dyb_ref[...] = _dot_t(dy_b, wso_ref[...])

    tile = pl.BlockSpec((tm, D), lambda i: (i, 0))
    wspec = pl.BlockSpec((D, D), lambda i: (0, 0))
    return pl.pallas_call(
        body, name="out_proj", grid=(n_i,),
        in_specs=[pl.BlockSpec((tm, 2 * D), lambda i: (i, 3)), tile, tile, tile, tile, _vec_spec(1), _vec_spec(1),
                  wspec, wspec, wspec],
        out_specs=[tile, tile, tile, pl.BlockSpec((tm, 2 * D), lambda i: (i, 3)), tile, tile, tile, tile,
                   pl.BlockSpec((SUB, D), lambda i: (0, 0))],
        out_shape=[_sds((s_len, D), F32), _sds((s_len, D), F32), _sds((s_len, D), F32), _sds((s_len, N_SEC * D), BF16),
                   _sds((s_len, D), BF16), _sds((s_len, D), BF16), _sds((s_len, D), BF16), _sds((s_len, D), BF16),
                   _sds((SUB, D), F32)],
        compiler_params=_params(("arbitrary",)),
    )(p, ya_in, yb_in, x, target, gate, g_final, w_co, w_so, w_o)


A_STATS_ROWS = 8 + HALO


def _branch_a_bwd(p, p_halo_src, y1, dya_in, dp, conv_wb, ln_g, ln_b):
    s_len = p.shape[0]
    tm = min(256, s_len)
    n_i = s_len // tm
    rows = 32
    halo_blocks = tm // HALO

    def tile_of(i):
        return n_i - 1 - i

    def body(p_ref, ph_ref, y1_ref, dya_ref, dp_in, wb_ref, g_ref, b_ref, dp_ref, st_ref, abuf, dybuf, acc8, tapacc):
        del dp_in
        i = pl.program_id(0)
        first_tile = tile_of(i) == 0

        @pl.when(i == 0)
        def _():
            dybuf[pl.ds(tm, HALO), :] = jnp.zeros((HALO, D), F32)
            st_ref[...] = jnp.zeros((A_STATS_ROWS, D), F32)
            acc8[...] = jnp.zeros((3 * PACK, D), F32)
            tapacc[...] = jnp.zeros((CONV_K * SUB, D), F32)

        def glu(r0):
            val = p_ref[pl.ds(r0, PACK), pl.ds(0, D)].astype(F32)
            gl = p_ref[pl.ds(r0, PACK), pl.ds(D, D)].astype(F32)
            abuf[pl.ds(HALO + r0, PACK), :] = val * _sigmoid(gl)
        _strips(tm, PACK,glu)

        def glu_halo(r0):
            val = ph_ref[pl.ds(r0, PACK), pl.ds(0, D)].astype(F32)
            gl = ph_ref[pl.ds(r0, PACK), pl.ds(D, D)].astype(F32)
            abuf[pl.ds(r0, PACK), :] = jnp.where(first_tile, 0.0, val * _sigmoid(gl))
        _strips(HALO, PACK,glu_halo)

        def norm_bwd(r0):
            y1 = y1_ref[pl.ds(r0, PACK), :]
            mu = _rowmean(y1)
            yc = y1 - mu
            rstd = lax.rsqrt(_rowmean(yc * yc) + EPS)
            n1 = yc * rstd
            l1 = n1 * g_ref[...] + b_ref[...]
            sg = _sigmoid(l1)
            z = p_ref[pl.ds(r0, PACK), pl.ds(2 * D, D)].astype(F32)
            sz = _sigmoid(z)
            dya = dya_ref[pl.ds(r0, PACK), :]
            dp_ref[pl.ds(r0, PACK), pl.ds(2 * D, D)] = (dya * (l1 * sg) * _dsilu(z, sz)).astype(BF16)
            dl1 = dya * (z * sz) * _dsilu(l1, sg)
            acc8[pl.ds(0, PACK), :] += dl1 * n1
            acc8[pl.ds(PACK, PACK), :] += dl1
            dn1 = dl1 * g_ref[...]
            dy1 = rstd * (dn1 - _rowmean(dn1) - n1 * _rowmean(dn1 * n1))
            acc8[pl.ds(2 * PACK, PACK), :] += dy1
            dybuf[pl.ds(r0, PACK), :] = dy1
        _strips(tm, PACK,norm_bwd)

        taps_d = {CONV_K - 1 - k: (lambda lanes, k=k: wb_ref[pl.ds(k * SUB, SUB), lanes]) for k in range(CONV_K)}

        def conv_bwd_data(r0):
            for lt in range(D // LANE):
                lanes = pl.ds(lt * LANE, LANE)
                acc = _conv_taps(dybuf, r0, lt, taps_d, rows)
                for m in range(0, len(acc), PACK // SUB):
                    da = jnp.concatenate(acc[m:m + PACK // SUB], axis=0)
                    rr = pl.ds(r0 + m * SUB, PACK)
                    val = p_ref[rr, pl.ds(lt * LANE, LANE)].astype(F32)
                    sg = _sigmoid(p_ref[rr, pl.ds(D + lt * LANE, LANE)].astype(F32))
                    dp_ref[rr, lanes] = (da * sg).astype(BF16)
                    dp_ref[rr, pl.ds(D + lt * LANE, LANE)] = (da * val * sg * (1.0 - sg)).astype(BF16)
        _strips(tm, rows, conv_bwd_data)

        n_out = rows // SUB

        def conv_bwd_w(r0):
            for lt in range(D // LANE):
                lanes = pl.ds(lt * LANE, LANE)
                win = abuf[pl.ds(r0, rows + HALO), lanes]
                dy = [dybuf[pl.ds(r0 + m * SUB, SUB), lanes] for m in range(n_out)]
                for phase in range(SUB):
                    ks = [k for k in range(CONV_K) if (HALO - (CONV_K - 1) + k) % SUB == phase]
                    q_max = max((HALO - (CONV_K - 1) + k) // SUB for k in ks)
                    sh = win[phase:phase + (n_out + q_max) * SUB, :]
                    for k in ks:
                        q = (HALO - (CONV_K - 1) + k) // SUB
                        part = dy[0] * sh[q * SUB:(q + 1) * SUB, :]
                        for m in range(1, n_out):
                            part = part + dy[m] * sh[(m + q) * SUB:(m + q + 1) * SUB, :]
                        tapacc[pl.ds(k * SUB, SUB), lanes] += part
        _strips(tm, rows, conv_bwd_w)

        dybuf[pl.ds(tm, HALO), :] = dybuf[pl.ds(0, HALO), :]

        @pl.when(i == n_i - 1)
        def _():
            for j in range(3):
                st_ref[pl.ds(j, 1), :] = jnp.sum(acc8[pl.ds(j * PACK, PACK), :], axis=0, keepdims=True)
            for k in range(CONV_K):
                st_ref[pl.ds(SUB + k, 1), :] = jnp.sum(tapacc[pl.ds(k * SUB, SUB), :], axis=0, keepdims=True)

    return pl.pallas_call(
        body, name="branch_a_bwd", grid=(n_i,),
        in_specs=[pl.BlockSpec((tm, 3 * D), lambda i: (tile_of(i), 0)),
                  pl.BlockSpec((HALO, 2 * D), lambda i: (jnp.maximum(tile_of(i) * halo_blocks - 1, 0), 0)),
                  pl.BlockSpec((tm, D), lambda i: (tile_of(i), 0)),
                  pl.BlockSpec((tm, D), lambda i: (tile_of(i), 0)),
                  pl.BlockSpec(memory_space=pl.ANY),
                  pl.BlockSpec((CONV_K * SUB, D), lambda i: (0, 0)), _vec_spec(1), _vec_spec(1)],
        out_specs=[pl.BlockSpec((tm, 3 * D), lambda i: (tile_of(i), 0)),
                   pl.BlockSpec((A_STATS_ROWS, D), lambda i: (0, 0))],
        out_shape=[_sds(dp.shape, BF16), _sds((A_STATS_ROWS, D), F32)],
        scratch_shapes=[pltpu.VMEM((tm + HALO, D), F32), pltpu.VMEM((tm + HALO, D), F32), pltpu.VMEM((3 * PACK, D), F32),
                        pltpu.VMEM((CONV_K * SUB, D), F32)],
        input_output_aliases={4: 0},
        compiler_params=_params(("arbitrary",)),
    )(p, p_halo_src, y1, dya_in, dp, conv_wb, ln_g, ln_b)


def _branch_b_bwd(p, dyb_in, dp, wt, wtt, bias_full, ln_g, ln_b):
    s_len = p.shape[0]
    tm = min(256, s_len)
    n_i = s_len // tm

    def body(p_ref, dyb_ref, dp_in, wt_ref, wtt_ref, bias_ref, g_ref, b_ref, dp_ref, st_ref, gbt_ref, gw_ref,
             vb, n2buf, rstdbuf, sbuf, dsb, dvbuf, acc8, gb_ref):
        del dp_in
        i = pl.program_id(0)

        @pl.when(i == 0)
        def _():
            st_ref[...] = jnp.zeros((SUB, D), F32)
            gbt_ref[...] = jnp.zeros((CHUNK, LANE), F32)
            gb_ref[...] = jnp.zeros((CHUNK, D), F32)
            gw_ref[...] = jnp.zeros((HEADS, CHUNK, CHUNK), F32)
            acc8[...] = jnp.zeros((2 * PACK, D), F32)

        def norm(r0):
            gv = _gelu(p_ref[pl.ds(r0, PACK), pl.ds(D, D)].astype(F32))
            mu = _rowmean(gv)
            vc = gv - mu
            rstd = lax.rsqrt(_rowmean(vc * vc) + EPS)
            n2 = vc * rstd
            n2buf[pl.ds(r0, PACK), :] = n2
            rstdbuf[pl.ds(r0, PACK), :] = jnp.broadcast_to(rstd, (PACK, LANE))
            vb[pl.ds(r0, PACK), :] = (n2 * g_ref[...] + b_ref[...]).astype(BF16)
        _strips(tm, PACK,norm)

        for ck in range(tm // CHUNK):
            for h in range(HEADS):
                blk = (pl.ds(ck * CHUNK, CHUNK), pl.ds(h * LANE, LANE))
                sbuf[blk] = jnp.dot(wt_ref[h], vb[blk], preferred_element_type=F32) + bias_ref[:, pl.ds(h * LANE, LANE)]

        def gate_bwd(r0):
            pu = p_ref[pl.ds(r0, PACK), pl.ds(0, D)].astype(F32)
            u = _gelu(pu)
            z = p_ref[pl.ds(r0, PACK), pl.ds(2 * D, D)].astype(F32)
            sg = _sigmoid(z)
            sz = z * sg
            s = sbuf[pl.ds(r0, PACK), :]
            dyb = dyb_ref[pl.ds(r0, PACK), :]
            ds = dyb * u * sz
            dsb[pl.ds(r0, PACK), :] = ds.astype(BF16)
            gb_ref[pl.ds(pl.multiple_of(r0 % CHUNK, PACK), PACK), :] += ds
            dp_ref[pl.ds(r0, PACK), pl.ds(0, D)] = (dyb * s * sz * _dgelu(pu)).astype(BF16)
            dp_ref[pl.ds(r0, PACK), pl.ds(2 * D, D)] = (dyb * u * s * _dsilu(z, sg)).astype(BF16)
        _strips(tm, PACK,gate_bwd)

        for ck in range(tm // CHUNK):
            for h in range(HEADS):
                blk = (pl.ds(ck * CHUNK, CHUNK), pl.ds(h * LANE, LANE))
                d_s = dsb[blk]
                dvbuf[blk] = jnp.dot(wtt_ref[h], d_s, preferred_element_type=F32)
                gw_ref[h] += _dot_t(d_s, vb[blk])

        def norm_bwd(r0):
            dv = dvbuf[pl.ds(r0, PACK), :]
            n2 = n2buf[pl.ds(r0, PACK), :]
            rstd = rstdbuf[pl.ds(r0, PACK), pl.ds(0, 1)]
            acc8[pl.ds(0, PACK), :] += dv * n2
            acc8[pl.ds(PACK, PACK), :] += dv
            dn2 = dv * g_ref[...]
            dgv = rstd * (dn2 - _rowmean(dn2) - n2 * _rowmean(dn2 * n2))
            dp_ref[pl.ds(r0, PACK), pl.ds(D, D)] = (dgv * _dgelu(p_ref[pl.ds(r0, PACK), pl.ds(D, D)].astype(F32))).astype(BF16)
        _strips(tm, PACK,norm_bwd)

        @pl.when(i == n_i - 1)
        def _():
            for j in range(2):
                st_ref[pl.ds(j, 1), :] = jnp.sum(acc8[pl.ds(j * PACK, PACK), :], axis=0, keepdims=True)
            row = lax.broadcasted_iota(jnp.int32, (CHUNK, CHUNK), 0)
            col = lax.broadcasted_iota(jnp.int32, (CHUNK, CHUNK), 1)
            for h in range(HEADS):
                gw_ref[h] = jnp.where(row >= col, gw_ref[h], 0.0)
            lane = lax.broadcasted_iota(jnp.int32, (CHUNK, LANE), 1)
            gbt = jnp.zeros((CHUNK, LANE), F32)
            for h in range(HEADS):
                gbt = jnp.where(lane == h, jnp.sum(gb_ref[:, pl.ds(h * LANE, LANE)], axis=1, keepdims=True), gbt)
            gbt_ref[...] = gbt

    wspec = pl.BlockSpec((HEADS, CHUNK, CHUNK), lambda i: (0, 0, 0))
    return pl.pallas_call(
        body, name="branch_b_bwd", grid=(n_i,),
        in_specs=[pl.BlockSpec((tm, 3 * D), lambda i: (i, 1)), pl.BlockSpec((tm, D), lambda i: (i, 0)),
                  pl.BlockSpec(memory_space=pl.ANY), wspec, wspec,
                  pl.BlockSpec((CHUNK, D), lambda i: (0, 0)), _vec_spec(1), _vec_spec(1)],
        out_specs=[pl.BlockSpec((tm, 3 * D), lambda i: (i, 1)), pl.BlockSpec((SUB, D), lambda i: (0, 0)),
                   pl.BlockSpec((CHUNK, LANE), lambda i: (0, 0)), wspec],
        out_shape=[_sds(dp.shape, BF16), _sds((SUB, D), F32), _sds((CHUNK, LANE), F32), _sds((HEADS, CHUNK, CHUNK), F32)],
        scratch_shapes=[pltpu.VMEM((tm, D), BF16), pltpu.VMEM((tm, D), F32), pltpu.VMEM((tm, LANE), F32),
                        pltpu.VMEM((tm, D), F32), pltpu.VMEM((tm, D), BF16), pltpu.VMEM((tm, D), F32),
                        pltpu.VMEM((2 * PACK, D), F32), pltpu.VMEM((CHUNK, D), F32)],
        input_output_aliases={2: 0},
        compiler_params=_params(("arbitrary",)),
    )(p, dyb_in, dp, wt, wtt, bias_full, ln_g, ln_b)


def _in_proj_bwd(dp, wg_in, x, dx2, shift, scale, g_pre):
    del shift
    s_len = x.shape[0]
    tm = min(512, s_len)
    n_i = s_len // tm
    wn = wg_in.shape[2]

    def body(dp_ref, w_ref, x_ref, dx2_ref, sc_ref, g_ref, gx_ref, st_ref, acc, acc8):
        i, j = pl.program_id(0), pl.program_id(1)

        @pl.when((i == 0) & (j == 0))
        def _():
            st_ref[...] = jnp.zeros((SUB, D), F32)
            acc8[...] = jnp.zeros((3 * PACK, D), F32)

        part = _dot_t(dp_ref[...], w_ref[0])

        @pl.when(j == 0)
        def _():
            acc[...] = part

        @pl.when(j > 0)
        def _():
            acc[...] += part

        @pl.when(j == N_CHIP - 1)
        def _():
            def strip(r0):
                xs = x_ref[pl.ds(r0, PACK), :]
                r = lax.rsqrt(_rowmean(xs * xs) + EPS)
                xn = xs * r
                dh = acc[pl.ds(r0, PACK), :]
                acc8[pl.ds(0, PACK), :] += dh
                acc8[pl.ds(PACK, PACK), :] += dh * (xn * g_ref[...])
                dhp = dh * (1.0 + sc_ref[...])
                acc8[pl.ds(2 * PACK, PACK), :] += dhp * xn
                dxn = dhp * g_ref[...]
                gx_ref[pl.ds(r0, PACK), :] = dx2_ref[pl.ds(r0, PACK), :] + r * (dxn - xn * _rowmean(dxn * xn))
            _strips(tm, PACK,strip)

        @pl.when((i == n_i - 1) & (j == N_CHIP - 1))
        def _():
            for k in range(3):
                st_ref[pl.ds(k, 1), :] = jnp.sum(acc8[pl.ds(k * PACK, PACK), :], axis=0, keepdims=True)

    tile = pl.BlockSpec((tm, D), lambda i, j: (i, 0))
    return pl.pallas_call(
        body, name="in_proj_bwd", grid=(n_i, N_CHIP),
        in_specs=[pl.BlockSpec((tm, wn), lambda i, j: (i, j)), pl.BlockSpec((1, D, wn), lambda i, j: (j, 0, 0)),
                  tile, tile, _vec_spec(2), _vec_spec(2)],
        out_specs=[tile, pl.BlockSpec((SUB, D), lambda i, j: (0, 0))],
        out_shape=[_sds((s_len, D), F32), _sds((SUB, D), F32)],
        scratch_shapes=[pltpu.VMEM((tm, D), F32), pltpu.VMEM((3 * PACK, D), F32)],
        compiler_params=_params(("arbitrary", "arbitrary")),
    )(dp, wg_in, x, dx2, scale, g_pre)


def _grad_matmul(a, b, name):
    s_len, n = b.shape
    tn = min(2048, n)
    tk = min(512, s_len)
    n_k = s_len // tk

    def body(a_ref, b_ref, o_ref, ob_ref):
        k = pl.program_id(1)
        part = lax.dot_general(a_ref[...], b_ref[...], (((0,), (0,)), ((), ())), preferred_element_type=F32)

        @pl.when(k == 0)
        def _():
            o_ref[0] = part

        @pl.when(k > 0)
        def _():
            o_ref[0] += part

        @pl.when(k == n_k - 1)
        def _():
            ob_ref[0] = o_ref[0].astype(BF16)

    out_spec = pl.BlockSpec((1, D, tn), lambda j, k: (j, 0, 0))
    return pl.pallas_call(
        body, name=name, grid=(n // tn, n_k),
        in_specs=[pl.BlockSpec((tk, D), lambda j, k: (k, 0)), pl.BlockSpec((tk, tn), lambda j, k: (k, j))],
        out_specs=[out_spec, out_spec],
        out_shape=[_sds((n // tn, D, tn), F32), _sds((n // tn, D, tn), BF16)],
        compiler_params=_params(("arbitrary", "arbitrary")),
    )(a, b)


def _local_step(x, target, shift, scale, gate, g_pre, conv_w_full, conv_b, conv_ln_g, conv_ln_b,
                sgu_ln_g, sgu_ln_b, w_sgu, b_sgu, g_final, wg_in, w_co, w_so, w_o):
    conv_wb = jnp.repeat(conv_w_full, SUB, axis=0)
    causal = jnp.tril(jnp.ones((CHUNK, CHUNK), dtype=bool))
    wt = jnp.where(causal[None], w_sgu, 0.0).astype(BF16)
    wtt = jnp.swapaxes(wt, 1, 2)
    bias_full = jnp.repeat(b_sgu.T, LANE, axis=1)

    p, hb = _in_proj(x, shift, scale, g_pre, wg_in)
    ya_in, y1 = _branch_a_fwd(p, conv_wb, conv_b, conv_ln_g, conv_ln_b)
    yb_in = _branch_b_fwd(p, wt, bias_full, sgu_ln_g, sgu_ln_b)
    dx2, dya_in, dyb_in, dp, mb, dob, dyab, dybb, sums_o = _out_proj(
        p, ya_in, yb_in, x, target, gate, g_final, w_co, w_so, w_o)
    dp, st_a = _branch_a_bwd(p, p, y1, dya_in, dp, conv_wb, conv_ln_g, conv_ln_b)
    dp, st_b, gbt, gws = _branch_b_bwd(p, dyb_in, dp, wt, wtt, bias_full, sgu_ln_g, sgu_ln_b)
    grad_x, st_i = _in_proj_bwd(dp, wg_in, x, dx2, shift, scale, g_pre)
    gw_in = _grad_matmul(hb, dp, "grad_w_in")
    gw_o = _grad_matmul(mb, dob, "grad_w_o")
    gw_co = _grad_matmul(ya_in, dyab, "grad_w_conv_out")
    gw_so = _grad_matmul(yb_in, dybb, "grad_w_sgu_out")
    return dict(
        grad_x=grad_x, loss_cols=sums_o[2:3], g_final=sums_o[0:1], d_gate=sums_o[1:2],
        d_shift=st_i[0:1], d_scale=st_i[1:2], g_pre=st_i[2:3],
        conv_ln_g=st_a[0:1], conv_ln_b=st_a[1:2], conv_b=st_a[2:3], conv_w=st_a[SUB:SUB + CONV_K],
        sgu_ln_g=st_b[0:1], sgu_ln_b=st_b[1:2], b_sgu=gbt[:, :HEADS].T, w_sgu=gws,
        w_in=gw_in, w_o=gw_o, w_conv_out=gw_co, w_sgu_out=gw_so)


ANY_SPEC = pl.BlockSpec(memory_space=pl.ANY)
VMEM_SPEC = pl.BlockSpec(memory_space=pltpu.VMEM)


def _place():
    return lax.axis_index("x"), lax.axis_index("y"), lax.axis_index("c")


def _peer(k):
    x, y, c = _place()
    return (1 - x if k & 4 else x, 1 - y if k & 2 else y, 1 - c if k & 1 else c)


def _dev_of(p):
    return 4 * p[0] + 2 * p[1] + p[2]


def _chip_of(p):
    return 2 * p[0] + p[1]


def _rdma(src, dst, send_sem, recv_sem, to):
    return pltpu.make_async_remote_copy(src_ref=src, dst_ref=dst, send_sem=send_sem, recv_sem=recv_sem,
                                        device_id=to, device_id_type=MESH)


CHIP_PEERS = (2, 4, 6)
ALL_PEERS = tuple(range(1, N_DEV))
SIBLING = 1


def _setup_comm(c8, w_ada_s, b_ada_s, convw_s):
    n_mod = w_ada_s.shape[1]
    rows = SUB * N_DEV

    def body(c8_ref, wada_ref, bada_ref, cw_ref, call_ref, mod_ref, cwall_ref, csend, crecv, wsend, wrecv, msend, mrecv):
        me = _place()
        dev, chip = _dev_of(me), _chip_of(me)

        def c_rows(d):
            return call_ref.at[pl.ds(pl.multiple_of(d * SUB, SUB), SUB), :]

        call_ref[pl.ds(pl.multiple_of(dev * SUB, SUB), SUB), :] = c8_ref[...]
        cwall_ref[chip] = cw_ref[...]
        c_out = [_rdma(c8_ref, c_rows(dev), csend.at[k], crecv.at[k], _peer(k)) for k in ALL_PEERS]
        w_out = [_rdma(cw_ref, cwall_ref.at[chip], wsend.at[k], wrecv.at[k], _peer(k)) for k in CHIP_PEERS]
        for cp in c_out + w_out:
            cp.start()
        for k in ALL_PEERS:
            _rdma(c8_ref, c_rows(_dev_of(_peer(k))), csend.at[k], crecv.at[k], _peer(k)).wait_recv()
        part = jnp.dot(call_ref[...].astype(BF16), wada_ref[...].astype(BF16), preferred_element_type=F32) + bada_ref[...]
        mod_ref[chip] = part
        m_out = [_rdma(mod_ref.at[chip], mod_ref.at[chip], msend.at[k], mrecv.at[k], _peer(k)) for k in CHIP_PEERS]
        for cp in m_out:
            cp.start()
        for k in CHIP_PEERS:
            pc = _chip_of(_peer(k))
            _rdma(cw_ref, cwall_ref.at[pc], wsend.at[k], wrecv.at[k], _peer(k)).wait_recv()
            _rdma(mod_ref.at[pc], mod_ref.at[pc], msend.at[k], mrecv.at[k], _peer(k)).wait_recv()
        for cp in c_out + w_out + m_out:
            cp.wait_send()

    return pl.pallas_call(
        body, name="setup_comm",
        in_specs=[VMEM_SPEC] * 4, out_specs=[VMEM_SPEC] * 3,
        out_shape=[_sds((rows, D), F32), _sds((N_CHIP, rows, n_mod), F32), _sds((N_CHIP,) + convw_s.shape, F32)],
        scratch_shapes=[pltpu.SemaphoreType.DMA((N_DEV,))] * 6,
        compiler_params=_params(),
    )(c8, w_ada_s, b_ada_s, convw_s)


def _gather_weights(shards):
    n = len(shards)

    def body(*refs):
        ins, outs = refs[:n], refs[n:2 * n]
        lsem, isend, irecv, dsend, drecv = refs[2 * n:]
        me = _place()
        chip, c = _chip_of(me), me[2]
        local = [pltpu.make_async_copy(ins[t], outs[t].at[chip], lsem.at[t]) for t in range(n)]
        for cp in local:
            cp.start()

        def half(t, which):
            hr = shards[t].shape[0] // 2
            return pl.ds(pl.multiple_of(which * hr, hr), hr)

        sends = []
        for t in range(n):
            for j, k in enumerate(CHIP_PEERS):
                cp = _rdma(ins[t].at[half(t, c)], outs[t].at[chip, half(t, c)], isend.at[t, j], irecv.at[t, j], _peer(k))
                cp.start()
                sends.append(cp)
        for t in range(n):
            for j, k in enumerate(CHIP_PEERS):
                blk = outs[t].at[_chip_of(_peer(k)), half(t, c)]
                _rdma(blk, blk, isend.at[t, j], irecv.at[t, j], _peer(k)).wait_recv()
                cp = _rdma(blk, blk, dsend.at[t, j], drecv.at[t, j], _peer(SIBLING))
                cp.start()
                sends.append(cp)
        for t in range(n):
            for j, k in enumerate(CHIP_PEERS):
                blk = outs[t].at[_chip_of(_peer(k)), half(t, 1 - c)]
                _rdma(blk, blk, dsend.at[t, j], drecv.at[t, j], _peer(SIBLING)).wait_recv()
        for cp in sends:
            cp.wait_send()
        for cp in local:
            cp.wait()

    return pl.pallas_call(
        body, name="gather_weights",
        in_specs=[ANY_SPEC] * n, out_specs=[ANY_SPEC] * n,
        out_shape=[_sds((N_CHIP,) + s.shape, s.dtype) for s in shards],
        scratch_shapes=[pltpu.SemaphoreType.DMA((n,))] + [pltpu.SemaphoreType.DMA((n, len(CHIP_PEERS)))] * 4,
        compiler_params=_params(),
    )(*shards)


def _scatter_grads(grads):
    n = len(grads)

    def body(*refs):
        ins, outs = refs[:n], refs[n:2 * n]
        lsem, send, recv = refs[2 * n:]
        me = _place()
        dev, chip, c = _dev_of(me), _chip_of(me), me[2]
        local = [pltpu.make_async_copy(ins[t].at[chip, c], outs[t].at[dev], lsem.at[t]) for t in range(n)]
        for cp in local:
            cp.start()
        sends = []
        for t in range(n):
            for k in ALL_PEERS:
                to = _peer(k)
                cp = _rdma(ins[t].at[_chip_of(to), to[2]], outs[t].at[dev], send.at[t, k], recv.at[t, k], to)
                cp.start()
                sends.append(cp)
        for t in range(n):
            for k in ALL_PEERS:
                blk = outs[t].at[_dev_of(_peer(k))]
                _rdma(blk, blk, send.at[t, k], recv.at[t, k], _peer(k)).wait_recv()
        for cp in sends:
            cp.wait_send()
        for cp in local:
            cp.wait()

    return pl.pallas_call(
        body, name="scatter_grads",
        in_specs=[ANY_SPEC] * n, out_specs=[ANY_SPEC] * n,
        out_shape=[_sds((N_DEV,) + g.shape[2:], g.dtype) for g in grads],
        scratch_shapes=[pltpu.SemaphoreType.DMA((n,))] + [pltpu.SemaphoreType.DMA((n, N_DEV))] * 2,
        compiler_params=_params(),
    )(*grads)


def _sum_devices(parts, name):
    _, r, cols = parts.shape
    tr = min(r, 128)

    def body(in_ref, o_ref):
        acc = in_ref[0].astype(F32)
        for d in range(1, N_DEV):
            acc = acc + in_ref[d].astype(F32)
        o_ref[...] = acc

    return pl.pallas_call(
        body, name=name, grid=(r // tr,),
        in_specs=[pl.BlockSpec((N_DEV, tr, cols), lambda i: (0, i, 0))],
        out_specs=pl.BlockSpec((tr, cols), lambda i: (i, 0)),
        out_shape=_sds((r, cols), F32),
        compiler_params=_params(("arbitrary",)),
    )(parts)


def _share_halves(reds):
    n = len(reds)

    def body(*refs):
        ins, outs = refs[:n], refs[n:2 * n]
        lsem, send, recv = refs[2 * n:]
        me = _place()
        c = me[2]
        local = [pltpu.make_async_copy(ins[t], outs[t].at[c], lsem.at[t]) for t in range(n)]
        sends = [_rdma(ins[t], outs[t].at[c], send.at[t], recv.at[t], _peer(SIBLING)) for t in range(n)]
        for cp in local + sends:
            cp.start()
        for t in range(n):
            _rdma(ins[t], outs[t].at[1 - c], send.at[t], recv.at[t], _peer(SIBLING)).wait_recv()
        for cp in sends:
            cp.wait_send()
        for cp in local:
            cp.wait()

    return pl.pallas_call(
        body, name="share_halves",
        in_specs=[VMEM_SPEC] * n, out_specs=[VMEM_SPEC] * n,
        out_shape=[_sds((2,) + r.shape, r.dtype) for r in reds],
        scratch_shapes=[pltpu.SemaphoreType.DMA((n,))] * 3,
        compiler_params=_params(),
    )(*reds)


def _sum_small(blob):
    rows = blob.shape[0]

    def body(b_ref, o_ref, pbuf, buf4, psend, precv, send, recv):
        me = _place()
        chip = _chip_of(me)
        pair = _rdma(b_ref, pbuf, psend, precv, _peer(SIBLING))
        pair.start()
        pair.wait()
        buf4[chip] = b_ref[...] + pbuf[...]
        out = [_rdma(buf4.at[chip], buf4.at[chip], send.at[k], recv.at[k], _peer(k)) for k in CHIP_PEERS]
        for cp in out:
            cp.start()
        for k in CHIP_PEERS:
            blk = buf4.at[_chip_of(_peer(k))]
            _rdma(blk, blk, send.at[k], recv.at[k], _peer(k)).wait_recv()
        o_ref[...] = (buf4[0] + buf4[1]) + (buf4[2] + buf4[3])
        for cp in out:
            cp.wait_send()

    return pl.pallas_call(
        body, name="sum_small",
        in_specs=[VMEM_SPEC], out_specs=VMEM_SPEC, out_shape=_sds(blob.shape, F32),
        scratch_shapes=[pltpu.VMEM((rows, D), F32), pltpu.VMEM((N_CHIP, rows, D), F32),
                        pltpu.SemaphoreType.DMA, pltpu.SemaphoreType.DMA,
                        pltpu.SemaphoreType.DMA((N_DEV,)), pltpu.SemaphoreType.DMA((N_DEV,))],
        compiler_params=_params(),
    )(blob)


def _adamw_math(w, g, m, v):
    m = ADAM_B1 * m + (1.0 - ADAM_B1) * g
    v = ADAM_B2 * v + (1.0 - ADAM_B2) * (g * g)
    m_hat = m / (1.0 - ADAM_B1 ** ADAM_STEP)
    v_hat = v / (1.0 - ADAM_B2 ** ADAM_STEP)
    delta = -ADAM_LR * (m_hat / (jnp.sqrt(v_hat) + ADAM_EPS) + ADAM_WD * w)
    return delta, m, v


def _row_tile(r, cols):
    if r * cols * 4 <= 2 ** 20:
        return r
    return next(t for t in (512, 256, 128, 64, 32, 16, 8) if r % t == 0 and t * cols * 4 <= 2 ** 20)


def _adamw(w, g, m, v, name):
    r, cols = w.shape
    tr = _row_tile(r, cols)

    def body(w_ref, g_ref, m_ref, v_ref, d_ref, nm_ref, nv_ref):
        d_ref[...], nm_ref[...], nv_ref[...] = _adamw_math(w_ref[...], g_ref[...], m_ref[...], v_ref[...])

    spec = pl.BlockSpec((tr, cols), lambda i: (i, 0))
    return pl.pallas_call(
        body, name=name, grid=(r // tr,), in_specs=[spec] * 4, out_specs=[spec] * 3,
        out_shape=[_sds((r, cols), F32)] * 3, compiler_params=_params(("arbitrary",)),
    )(w, g, m, v)


def _adamw_ada(w, ct, dm, m, v):
    r, cols = w.shape
    tr = _row_tile(r, cols)

    def body(w_ref, ct_ref, dm_ref, m_ref, v_ref, g_ref, d_ref, nm_ref, nv_ref):
        g = jnp.dot(ct_ref[...], dm_ref[...], preferred_element_type=F32)
        g_ref[...] = g
        d_ref[...], nm_ref[...], nv_ref[...] = _adamw_math(w_ref[...], g, m_ref[...], v_ref[...])

    spec = pl.BlockSpec((tr, cols), lambda i: (i, 0))
    return pl.pallas_call(
        body, name="adamw_ada", grid=(r // tr,),
        in_specs=[spec, pl.BlockSpec((tr, LANE), lambda i: (i, 0)), pl.BlockSpec((LANE, cols), lambda i: (0, 0)), spec, spec],
        out_specs=[spec] * 4, out_shape=[_sds((r, cols), F32)] * 4, compiler_params=_params(("arbitrary",)),
    )(w, ct, dm, m, v)


BLOB_VEC, BLOB_CONV, BLOB_SGU, BLOB_ADA, BLOB_DMOD, BLOB_LOSS, BLOB_ROWS = 0, 8, 40, 168, 176, 200, 208
SMALL_CONV, SMALL_SGU, SMALL_ADA, SMALL_ROWS = 8, 16, 144, 152


def _set_rows(buf, row, val):
    return lax.dynamic_update_slice(buf, val.astype(F32), (row, 0))


def _pack_small(vecs, b_sgu, conv_w_s, w_sgu, b_ada):
    buf = jnp.zeros((SMALL_ROWS, D), F32)
    for i, vec in enumerate(vecs):
        buf = _set_rows(buf, i, vec.reshape(1, D))
    buf = _set_rows(buf, 7, b_sgu.reshape(1, D))
    buf = _set_rows(buf, SMALL_CONV, jnp.pad(conv_w_s, ((0, 1), (0, 0))).reshape(SUB, D))
    buf = _set_rows(buf, SMALL_SGU, w_sgu.reshape(CHUNK, D))
    return _set_rows(buf, SMALL_ADA, b_ada.reshape(3, D))


def _unpack_small(buf, conv_cols):
    vecs = [buf[i] for i in range(7)]
    b_sgu = buf[7].reshape(HEADS, CHUNK)
    conv_w_s = buf[SMALL_CONV:SMALL_CONV + SUB].reshape(HALO, conv_cols)[:CONV_K]
    w_sgu = buf[SMALL_SGU:SMALL_SGU + CHUNK].reshape(HEADS, CHUNK, CHUNK)
    b_ada = buf[SMALL_ADA:SMALL_ADA + 3].reshape(1, 3 * D)
    return vecs, b_sgu, conv_w_s, w_sgu, b_ada


def kernel(x, c, w_ada, b_ada, g_pre, w_in, conv_w, conv_b, conv_ln_g, conv_ln_b, w_conv_out, sgu_ln_g, sgu_ln_b, w_sgu, b_sgu, w_sgu_out, w_o, g_final, loss_target, m_w_ada, m_b_ada, m_g_pre, m_w_in, m_conv_w, m_conv_b, m_conv_ln_g, m_conv_ln_b, m_w_conv_out, m_sgu_ln_g, m_sgu_ln_b, m_w_sgu, m_b_sgu, m_w_sgu_out, m_w_o, m_g_final, v_w_ada, v_b_ada, v_g_pre, v_w_in, v_conv_w, v_conv_b, v_conv_ln_g, v_conv_ln_b, v_w_conv_out, v_sgu_ln_g, v_sgu_ln_b, v_w_sgu, v_b_sgu, v_w_sgu_out, v_w_o, v_g_final):
    me = _place()
    dev, chip = _dev_of(me), _chip_of(me)
    n_ada = w_ada.shape[2]
    conv_cols = conv_w.shape[2]

    b_ada_s = lax.dynamic_slice(b_ada, (0, chip * n_ada), (1, n_ada))
    c_all, mod_all, cw_all = _setup_comm(
        jnp.broadcast_to(c, (SUB, D)), w_ada[0], b_ada_s, jnp.pad(conv_w[0], ((0, HALO - CONV_K), (0, 0))))
    mod = lax.dynamic_slice(mod_all, (0, dev * SUB, 0), (N_CHIP, 1, n_ada)).reshape(1, 3 * D)
    shift, scale, gate = mod[:, :D], mod[:, D:2 * D], mod[:, 2 * D:]
    conv_w_full = jnp.swapaxes(cw_all, 0, 1).reshape(HALO, D)[:CONV_K]

    wg_in, wg_co, wg_so, wg_o = _gather_weights(
        [w_in[0].astype(BF16), w_conv_out[0].astype(BF16), w_sgu_out[0].astype(BF16), w_o[0].astype(BF16)])

    loc = _local_step(x[0], loss_target[0], shift, scale, gate, g_pre, conv_w_full, conv_b, conv_ln_g, conv_ln_b,
                      sgu_ln_g, sgu_ln_b, w_sgu[0], b_sgu[0], g_final.reshape(1, D),
                      wg_in, wg_co.reshape(D, D), wg_so.reshape(D, D), wg_o.reshape(D, D))

    big = ["w_in", "w_conv_out", "w_sgu_out", "w_o"]
    contrib = []
    for name in big:
        g16 = loc[name][1]
        rows_half = (g16.shape[0] * g16.shape[1]) // (2 * N_CHIP) if name != "w_in" else g16.shape[1] // 2
        contrib.append(g16.reshape(N_CHIP, 2, rows_half, g16.shape[2]))
    parts = _scatter_grads(contrib)
    reds = [_sum_devices(p, "sum_" + name) for p, name in zip(parts, big)]
    full = _share_halves(reds)
    g_big = {name: f.reshape(2 * f.shape[1], f.shape[2]) for name, f in zip(big, full)}

    d_mod = jnp.concatenate([loc["d_shift"], loc["d_scale"], loc["d_gate"]], axis=0)
    blob = jnp.zeros((BLOB_ROWS, D), F32)
    for i, name in enumerate(["g_pre", "conv_b", "conv_ln_g", "conv_ln_b", "sgu_ln_g", "sgu_ln_b", "g_final"]):
        blob = _set_rows(blob, BLOB_VEC + i, loc[name])
    blob = _set_rows(blob, BLOB_VEC + 7, loc["b_sgu"].reshape(1, D))
    blob = _set_rows(blob, BLOB_CONV, loc["conv_w"])
    blob = _set_rows(blob, BLOB_SGU, loc["w_sgu"].reshape(CHUNK, D))
    blob = _set_rows(blob, BLOB_ADA, d_mod)
    blob = lax.dynamic_update_slice(blob, d_mod, (BLOB_DMOD + 3 * dev, 0))
    blob = _set_rows(blob, BLOB_LOSS, loc["loss_cols"])
    tot = _sum_small(blob)

    loss = jnp.sum(tot[BLOB_LOSS])
    g_vecs = [tot[BLOB_VEC + i] for i in range(7)]
    g_b_sgu = tot[BLOB_VEC + 7].reshape(HEADS, CHUNK)
    g_conv_s = lax.dynamic_slice(tot, (BLOB_CONV, chip * conv_cols), (CONV_K, conv_cols))
    g_w_sgu = tot[BLOB_SGU:BLOB_SGU + CHUNK].reshape(HEADS, CHUNK, CHUNK)
    g_b_ada = tot[BLOB_ADA:BLOB_ADA + 3].reshape(1, 3 * D)
    d_mod_all = tot[BLOB_DMOD:BLOB_DMOD + 3 * N_DEV].reshape(N_DEV, 3 * D)

    ct = jnp.pad(c_all[::SUB].T, ((0, 0), (0, LANE - N_DEV))).astype(BF16)
    dm = jnp.pad(lax.dynamic_slice(d_mod_all, (0, chip * n_ada), (N_DEV, n_ada)), ((0, LANE - N_DEV), (0, 0))).astype(BF16)
    g_ada, d_ada, nm_ada, nv_ada = _adamw_ada(w_ada[0], ct, dm, m_w_ada[0], v_w_ada[0])

    upd = {}
    for name, w, m, v in [("w_in", w_in, m_w_in, v_w_in), ("w_conv_out", w_conv_out, m_w_conv_out, v_w_conv_out),
                          ("w_sgu_out", w_sgu_out, m_w_sgu_out, v_w_sgu_out), ("w_o", w_o, m_w_o, v_w_o)]:
        upd[name] = _adamw(w[0], g_big[name], m[0], v[0], "adamw_" + name)

    def small(g_p, conv_p, sgu_p, ada_p, vec_ps):
        return _pack_small(vec_ps, g_p, conv_p, sgu_p, ada_p)

    w_s = small(b_sgu[0], conv_w[0], w_sgu[0], b_ada, [g_pre, conv_b, conv_ln_g, conv_ln_b, sgu_ln_g, sgu_ln_b, g_final])
    m_s = small(m_b_sgu[0], m_conv_w[0], m_w_sgu[0], m_b_ada,
                [m_g_pre, m_conv_b, m_conv_ln_g, m_conv_ln_b, m_sgu_ln_g, m_sgu_ln_b, m_g_final])
    v_s = small(v_b_sgu[0], v_conv_w[0], v_w_sgu[0], v_b_ada,
                [v_g_pre, v_conv_b, v_conv_ln_g, v_conv_ln_b, v_sgu_ln_g, v_sgu_ln_b, v_g_final])
    g_s = small(g_b_sgu, g_conv_s, g_w_sgu, g_b_ada, g_vecs)
    small_out = [_unpack_small(a, conv_cols) for a in (g_s,) + tuple(_adamw(w_s, g_s, m_s, v_s, "adamw_small"))]

    def leaves(kind):
        vecs, o_b_sgu, o_conv, o_w_sgu, o_b_ada = small_out[kind]
        ada = (g_ada, d_ada, nm_ada, nv_ada)[kind]
        def bigk(name):
            return (g_big[name] if kind == 0 else upd[name][kind - 1])[None]
        return [ada[None], o_b_ada, vecs[0][None], bigk("w_in"), o_conv[None], vecs[1][None], vecs[2][None], vecs[3][None],
                bigk("w_conv_out"), vecs[4][None], vecs[5][None], o_w_sgu[None], o_b_sgu[None], bigk("w_sgu_out"),
                bigk("w_o"), vecs[6]]

    return (loss, loc["grad_x"][None], *leaves(0), *leaves(1), *leaves(2), *leaves(3))
```

```python
import functools

import jax
import jax.numpy as jnp
from jax import lax
from jax.experimental import pallas as pl
from jax.experimental.pallas import tpu as pltpu

F32 = jnp.float32
BF16 = jnp.bfloat16
MESH = pl.DeviceIdType.MESH

D = 1024
N_SEC = 8
N_CHIP = 4
N_DEV = 8
EPS = 1e-6
CONV_K = 31
HALO = 32
CHUNK = 128
HEADS = 8
LANE = 128
SUB = 8
PACK = 16
VMEM_LIMIT = 56 * 1024 * 1024

ADAM_LR, ADAM_B1, ADAM_B2, ADAM_EPS, ADAM_WD, ADAM_STEP = 0.001, 0.9, 0.999, 1e-08, 0.01, 10

_SQRT_HALF = 0.7071067811865476
_INV_SQRT_2PI = 0.3989422804014327


def _sds(shape, dtype):
    return jax.ShapeDtypeStruct(shape, dtype)


def _params(sem=None):
    if sem is None:
        return pltpu.CompilerParams(vmem_limit_bytes=VMEM_LIMIT)
    return pltpu.CompilerParams(dimension_semantics=sem, vmem_limit_bytes=VMEM_LIMIT)


def _strips(n_rows, rows, fn):
    def step(s, carry):
        fn(pl.multiple_of(s * rows, rows))
        return carry
    lax.fori_loop(0, n_rows // rows, step, 0)


def _sigmoid(v):
    return 1.0 / (1.0 + jnp.exp(-v))


def _gelu(v):
    return 0.5 * v * (1.0 + lax.erf(v * _SQRT_HALF))


def _dgelu(v):
    return 0.5 * (1.0 + lax.erf(v * _SQRT_HALF)) + v * jnp.exp(-0.5 * v * v) * _INV_SQRT_2PI


def _dsilu(v, sg):
    return sg * (1.0 + v * (1.0 - sg))


def _rowmean(v):
    return jnp.mean(v, axis=-1, keepdims=True)


def _vec_spec(grid_rank):
    zeros = (0, 0)
    if grid_rank == 1:
        return pl.BlockSpec((1, D), lambda i: zeros)
    return pl.BlockSpec((1, D), lambda i, j: zeros)


def _in_proj(x, shift, scale, g_pre, wg_in):
    s_len = x.shape[0]
    tm = min(256, s_len)
    n_i = s_len // tm
    wn = wg_in.shape[2]

    def body(x_ref, sh_ref, sc_ref, g_ref, w_ref, p_ref, hb_ref):
        def strip(r0):
            xs = x_ref[pl.ds(r0, PACK), :]
            r = lax.rsqrt(_rowmean(xs * xs) + EPS)
            h = (xs * r) * g_ref[...] * (1.0 + sc_ref[...]) + sh_ref[...]
            hb_ref[pl.ds(r0, PACK), :] = h.astype(BF16)
        _strips(tm, PACK, strip)
        hb = hb_ref[...]
        for j in range(N_CHIP):
            p_ref[:, pl.ds(j * wn, wn)] = jnp.dot(hb, w_ref[j], preferred_element_type=F32).astype(BF16)

    return pl.pallas_call(
        body, name="in_proj", grid=(n_i,),
        in_specs=[pl.BlockSpec((tm, D), lambda i: (i, 0)), _vec_spec(1), _vec_spec(1), _vec_spec(1),
                  pl.BlockSpec((N_CHIP, D, wn), lambda i: (0, 0, 0), pipeline_mode=pl.Buffered(1))],
        out_specs=[pl.BlockSpec((tm, N_CHIP * wn), lambda i: (i, 0)), pl.BlockSpec((tm, D), lambda i: (i, 0))],
        out_shape=[_sds((s_len, N_SEC * D), BF16), _sds((s_len, D), BF16)],
        compiler_params=_params(("arbitrary",)),
    )(x, shift, scale, g_pre, wg_in)


def _conv_taps(win_ref, r0, lt, weight_of_offset, rows):
    lanes = pl.ds(lt * LANE, LANE)
    win = win_ref[pl.ds(r0, rows + HALO), lanes]
    n_out = rows // SUB
    acc = [jnp.zeros((SUB, LANE), F32) for _ in range(n_out)]
    for phase in range(SUB):
        offs = [o for o in weight_of_offset if o % SUB == phase]
        if not offs:
            continue
        q_max = max(o // SUB for o in offs)
        span = (n_out + q_max) * SUB
        sh = win[phase:phase + span, :]
        for o in offs:
            q = o // SUB
            w = weight_of_offset[o](lanes)
            for m in range(n_out):
                acc[m] = acc[m] + w * sh[(m + q) * SUB:(m + q + 1) * SUB, :]
    return acc


def _branch_a_fwd(p, conv_wb, conv_b, ln_g, ln_b):
    s_len = p.shape[0]
    tm = min(256, s_len)
    n_i = s_len // tm
    rows = 32

    def body(p_ref, wb_ref, cb_ref, g_ref, b_ref, ya_ref, y1_ref, abuf):
        @pl.when(pl.program_id(0) == 0)
        def _():
            abuf[pl.ds(0, HALO), :] = jnp.zeros((HALO, D), F32)

        def glu(r0):
            val = p_ref[pl.ds(r0, PACK), pl.ds(0, D)].astype(F32)
            gl = p_ref[pl.ds(r0, PACK), pl.ds(D, D)].astype(F32)
            abuf[pl.ds(HALO + r0, PACK), :] = val * _sigmoid(gl)
        _strips(tm, PACK,glu)

        taps = {HALO - (CONV_K - 1) + k: (lambda lanes, k=k: wb_ref[pl.ds(k * SUB, SUB), lanes]) for k in range(CONV_K)}

        def conv(r0):
            for lt in range(D // LANE):
                acc = _conv_taps(abuf, r0, lt, taps, rows)
                cb = cb_ref[:, pl.ds(lt * LANE, LANE)]
                for m, v in enumerate(acc):
                    y1_ref[pl.ds(r0 + m * SUB, SUB), pl.ds(lt * LANE, LANE)] = v + cb
        _strips(tm, rows, conv)

        def norm(r0):
            y1 = y1_ref[pl.ds(r0, PACK), :]
            mu = _rowmean(y1)
            yc = y1 - mu
            rstd = lax.rsqrt(_rowmean(yc * yc) + EPS)
            l1 = (yc * rstd) * g_ref[...] + b_ref[...]
            z = p_ref[pl.ds(r0, PACK), pl.ds(2 * D, D)].astype(F32)
            ya_ref[pl.ds(r0, PACK), :] = ((l1 * _sigmoid(l1)) * (z * _sigmoid(z))).astype(BF16)
        _strips(tm, PACK,norm)

        abuf[pl.ds(0, HALO), :] = abuf[pl.ds(tm, HALO), :]

    return pl.pallas_call(
        body, name="branch_a_fwd", grid=(n_i,),
        in_specs=[pl.BlockSpec((tm, 3 * D), lambda i: (i, 0)),
                  pl.BlockSpec((CONV_K * SUB, D), lambda i: (0, 0)), _vec_spec(1), _vec_spec(1), _vec_spec(1)],
        out_specs=[pl.BlockSpec((tm, D), lambda i: (i, 0)), pl.BlockSpec((tm, D), lambda i: (i, 0))],
        out_shape=[_sds((s_len, D), BF16), _sds((s_len, D), F32)],
        scratch_shapes=[pltpu.VMEM((tm + HALO, D), F32)],
        compiler_params=_params(("arbitrary",)),
    )(p, conv_wb, conv_b, ln_g, ln_b)


def _branch_b_fwd(p, wt, bias_full, ln_g, ln_b):
    s_len = p.shape[0]
    tm = min(256, s_len)
    n_i = s_len // tm

    def body(p_ref, wt_ref, bias_ref, g_ref, b_ref, yb_ref, vb, sbuf):
        def norm(r0):
            gv = _gelu(p_ref[pl.ds(r0, PACK), pl.ds(D, D)].astype(F32))
            mu = _rowmean(gv)
            vc = gv - mu
            rstd = lax.rsqrt(_rowmean(vc * vc) + EPS)
            vb[pl.ds(r0, PACK), :] = ((vc * rstd) * g_ref[...] + b_ref[...]).astype(BF16)
        _strips(tm, PACK,norm)

        for ck in range(tm // CHUNK):
            for h in range(HEADS):
                blk = (pl.ds(ck * CHUNK, CHUNK), pl.ds(h * LANE, LANE))
                sbuf[blk] = jnp.dot(wt_ref[h], vb[blk], preferred_element_type=F32) + bias_ref[:, pl.ds(h * LANE, LANE)]

        def gate(r0):
            u = _gelu(p_ref[pl.ds(r0, PACK), pl.ds(0, D)].astype(F32))
            z = p_ref[pl.ds(r0, PACK), pl.ds(2 * D, D)].astype(F32)
            yb_ref[pl.ds(r0, PACK), :] = (u * sbuf[pl.ds(r0, PACK), :] * (z * _sigmoid(z))).astype(BF16)
        _strips(tm, PACK,gate)

    return pl.pallas_call(
        body, name="branch_b_fwd", grid=(n_i,),
        in_specs=[pl.BlockSpec((tm, 3 * D), lambda i: (i, 1)),
                  pl.BlockSpec((HEADS, CHUNK, CHUNK), lambda i: (0, 0, 0)),
                  pl.BlockSpec((CHUNK, D), lambda i: (0, 0)), _vec_spec(1), _vec_spec(1)],
        out_specs=pl.BlockSpec((tm, D), lambda i: (i, 0)),
        out_shape=_sds((s_len, D), BF16),
        scratch_shapes=[pltpu.VMEM((tm, D), BF16), pltpu.VMEM((tm, D), F32)],
        compiler_params=_params(("arbitrary",)),
    )(p, wt, bias_full, ln_g, ln_b)


def _dot_t(a, b):
    return lax.dot_general(a, b, (((1,), (1,)), ((), ())), preferred_element_type=F32)


def _out_proj(p, ya_in, yb_in, x, target, gate, g_final, w_co, w_so, w_o):
    s_len = x.shape[0]
    tm = min(256, s_len)
    n_i = s_len // tm

    def body(pg_ref, ya_ref, yb_ref, x_ref, t_ref, gate_ref, gf_ref, wco_ref, wso_ref, wo_ref,
             dx2_ref, dya_ref, dyb_ref, dp_ref, mb_ref, dob_ref, dyab_ref, dybb_ref, sums_ref):
        @pl.when(pl.program_id(0) == 0)
        def _():
            sums_ref[...] = jnp.zeros((SUB, D), F32)

        y_a = jnp.dot(ya_ref[...], wco_ref[...], preferred_element_type=F32)
        y_b = jnp.dot(yb_ref[...], wso_ref[...], preferred_element_type=F32)
        ga = _sigmoid(pg_ref[:, pl.ds(0, D)].astype(F32))
        gb = _sigmoid(pg_ref[:, pl.ds(D, D)].astype(F32))
        mb = (ga * y_a + gb * y_b).astype(BF16)
        mb_ref[...] = mb
        o = jnp.dot(mb, wo_ref[...], preferred_element_type=F32)
        x2 = x_ref[...] + gate_ref[...] * o
        r2 = lax.rsqrt(_rowmean(x2 * x2) + EPS)
        xh = x2 * r2
        e = xh * gf_ref[...] - t_ref[...]
        dy = e * (1.0 / D)
        dxh = dy * gf_ref[...]
        dx2 = r2 * (dxh - xh * _rowmean(dxh * xh))
        dx2_ref[...] = dx2
        sums_ref[pl.ds(0, 1), :] += jnp.sum(dy * xh, axis=0, keepdims=True)
        sums_ref[pl.ds(1, 1), :] += jnp.sum(dx2 * o, axis=0, keepdims=True)
        sums_ref[pl.ds(2, 1), :] += jnp.sum(e * e, axis=0, keepdims=True) * (0.5 / D)
        dob = (gate_ref[...] * dx2).astype(BF16)
        dob_ref[...] = dob
        dm = _dot_t(dob, wo_ref[...])
        dy_a = (ga * dm).astype(BF16)
        dy_b = (gb * dm).astype(BF16)
        dyab_ref[...] = dy_a
        dybb_ref[...] = dy_b
        dp_ref[:, pl.ds(0, D)] = (dm * y_a * ga * (1.0 - ga)).astype(BF16)
        dp_ref[:, pl.ds(D, D)] = (dm * y_b * gb * (1.0 - gb)).astype(BF16)
        dya_ref[...] = _dot_t(dy_a, wco_ref[...])
        dyb_ref[...] = _dot_t(dy_b, wso_ref[...])

    tile = pl.BlockSpec((tm, D), lambda i: (i, 0))
    wspec = pl.BlockSpec((D, D), lambda i: (0, 0))
    return pl.pallas_call(
        body, name="out_proj", grid=(n_i,),
        in_specs=[pl.BlockSpec((tm, 2 * D), lambda i: (i, 3)), tile, tile, tile, tile, _vec_spec(1), _vec_spec(1),
                  wspec, wspec, wspec],
        out_specs=[tile, tile, tile, pl.BlockSpec((tm, 2 * D), lambda i: (i, 3)), tile, tile, tile, tile,
                   pl.BlockSpec((SUB, D), lambda i: (0, 0))],
        out_shape=[_sds((s_len, D), F32), _sds((s_len, D), F32), _sds((s_len, D), F32), _sds((s_len, N_SEC * D), BF16),
                   _sds((s_len, D), BF16), _sds((s_len, D), BF16), _sds((s_len, D), BF16), _sds((s_len, D), BF16),
                   _sds((SUB, D), F32)],
        compiler_params=_params(("arbitrary",)),
    )(p, ya_in, yb_in, x, target, gate, g_final, w_co, w_so, w_o)


A_STATS_ROWS = 8 + HALO


def _branch_a_bwd(p, p_halo_src, y1, dya_in, dp, conv_wb, ln_g, ln_b):
    s_len = p.shape[0]
    tm = min(256, s_len)
    n_i = s_len // tm
    rows = 32
    halo_blocks = tm // HALO

    def tile_of(i):
        return n_i - 1 - i

    def body(p_ref, ph_ref, y1_ref, dya_ref, dp_in, wb_ref, g_ref, b_ref, dp_ref, st_ref, abuf, dybuf, acc8, tapacc):
        del dp_in
        i = pl.program_id(0)
        first_tile = tile_of(i) == 0

        @pl.when(i == 0)
        def _():
            dybuf[pl.ds(tm, HALO), :] = jnp.zeros((HALO, D), F32)
            st_ref[...] = jnp.zeros((A_STATS_ROWS, D), F32)
            acc8[...] = jnp.zeros((3 * PACK, D), F32)
            tapacc[...] = jnp.zeros((CONV_K * SUB, D), F32)

        def glu(r0):
            val = p_ref[pl.ds(r0, PACK), pl.ds(0, D)].astype(F32)
            gl = p_ref[pl.ds(r0, PACK), pl.ds(D, D)].astype(F32)
            abuf[pl.ds(HALO + r0, PACK), :] = val * _sigmoid(gl)
        _strips(tm, PACK,glu)

        def glu_halo(r0):
            val = ph_ref[pl.ds(r0, PACK), pl.ds(0, D)].astype(F32)
            gl = ph_ref[pl.ds(r0, PACK), pl.ds(D, D)].astype(F32)
            abuf[pl.ds(r0, PACK), :] = jnp.where(first_tile, 0.0, val * _sigmoid(gl))
        _strips(HALO, PACK,glu_halo)

        def norm_bwd(r0):
            y1 = y1_ref[pl.ds(r0, PACK), :]
            mu = _rowmean(y1)
            yc = y1 - mu
            rstd = lax.rsqrt(_rowmean(yc * yc) + EPS)
            n1 = yc * rstd
            l1 = n1 * g_ref[...] + b_ref[...]
            sg = _sigmoid(l1)
            z = p_ref[pl.ds(r0, PACK), pl.ds(2 * D, D)].astype(F32)
            sz = _sigmoid(z)
            dya = dya_ref[pl.ds(r0, PACK), :]
            dp_ref[pl.ds(r0, PACK), pl.ds(2 * D, D)] = (dya * (l1 * sg) * _dsilu(z, sz)).astype(BF16)
            dl1 = dya * (z * sz) * _dsilu(l1, sg)
            acc8[pl.ds(0, PACK), :] += dl1 * n1
            acc8[pl.ds(PACK, PACK), :] += dl1
            dn1 = dl1 * g_ref[...]
            dy1 = rstd * (dn1 - _rowmean(dn1) - n1 * _rowmean(dn1 * n1))
            acc8[pl.ds(2 * PACK, PACK), :] += dy1
            dybuf[pl.ds(r0, PACK), :] = dy1
        _strips(tm, PACK,norm_bwd)

        taps_d = {CONV_K - 1 - k: (lambda lanes, k=k: wb_ref[pl.ds(k * SUB, SUB), lanes]) for k in range(CONV_K)}

        def conv_bwd_data(r0):
            for lt in range(D // LANE):
                lanes = pl.ds(lt * LANE, LANE)
                acc = _conv_taps(dybuf, r0, lt, taps_d, rows)
                for m in range(0, len(acc), PACK // SUB):
                    da = jnp.concatenate(acc[m:m + PACK // SUB], axis=0)
                    rr = pl.ds(r0 + m * SUB, PACK)
                    val = p_ref[rr, pl.ds(lt * LANE, LANE)].astype(F32)
                    sg = _sigmoid(p_ref[rr, pl.ds(D + lt * LANE, LANE)].astype(F32))
                    dp_ref[rr, lanes] = (da * sg).astype(BF16)
                    dp_ref[rr, pl.ds(D + lt * LANE, LANE)] = (da * val * sg * (1.0 - sg)).astype(BF16)
        _strips(tm, rows, conv_bwd_data)

        n_out = rows // SUB

        def conv_bwd_w(r0):
            for lt in range(D // LANE):
                lanes = pl.ds(lt * LANE, LANE)
                win = abuf[pl.ds(r0, rows + HALO), lanes]
                dy = [dybuf[pl.ds(r0 + m * SUB, SUB), lanes] for m in range(n_out)]
                for phase in range(SUB):
                    ks = [k for k in range(CONV_K) if (HALO - (CONV_K - 1) + k) % SUB == phase]
                    q_max = max((HALO - (CONV_K - 1) + k) // SUB for k in ks)
                    sh = win[phase:phase + (n_out + q_max) * SUB, :]
                    for k in ks:
                        q = (HALO - (CONV_K - 1) + k) // SUB
                        part = dy[0] * sh[q * SUB:(q + 1) * SUB, :]
                        for m in range(1, n_out):
                            part = part + dy[m] * sh[(m + q) * SUB:(m + q + 1) * SUB, :]
                        tapacc[pl.ds(k * SUB, SUB), lanes] += part
        _strips(tm, rows, conv_bwd_w)

        dybuf[pl.ds(tm, HALO), :] = dybuf[pl.ds(0, HALO), :]

        @pl.when(i == n_i - 1)
        def _():
            for j in range(3):
                st_ref[pl.ds(j, 1), :] = jnp.sum(acc8[pl.ds(j * PACK, PACK), :], axis=0, keepdims=True)
            for k in range(CONV_K):
                st_ref[pl.ds(SUB + k, 1), :] = jnp.sum(tapacc[pl.ds(k * SUB, SUB), :], axis=0, keepdims=True)

    return pl.pallas_call(
        body, name="branch_a_bwd", grid=(n_i,),
        in_specs=[pl.BlockSpec((tm, 3 * D), lambda i: (tile_of(i), 0)),
                  pl.BlockSpec((HALO, 2 * D), lambda i: (jnp.maximum(tile_of(i) * halo_blocks - 1, 0), 0)),
                  pl.BlockSpec((tm, D), lambda i: (tile_of(i), 0)),
                  pl.BlockSpec((tm, D), lambda i: (tile_of(i), 0)),
                  pl.BlockSpec(memory_space=pl.ANY),
                  pl.BlockSpec((CONV_K * SUB, D), lambda i: (0, 0)), _vec_spec(1), _vec_spec(1)],
        out_specs=[pl.BlockSpec((tm, 3 * D), lambda i: (tile_of(i), 0)),
                   pl.BlockSpec((A_STATS_ROWS, D), lambda i: (0, 0))],
        out_shape=[_sds(dp.shape, BF16), _sds((A_STATS_ROWS, D), F32)],
        scratch_shapes=[pltpu.VMEM((tm + HALO, D), F32), pltpu.VMEM((tm + HALO, D), F32), pltpu.VMEM((3 * PACK, D), F32),
                        pltpu.VMEM((CONV_K * SUB, D), F32)],
        input_output_aliases={4: 0},
        compiler_params=_params(("arbitrary",)),
    )(p, p_halo_src, y1, dya_in, dp, conv_wb, ln_g, ln_b)


def _branch_b_bwd(p, dyb_in, dp, wt, wtt, bias_full, ln_g, ln_b):
    s_len = p.shape[0]
    tm = min(256, s_len)
    n_i = s_len // tm

    def body(p_ref, dyb_ref, dp_in, wt_ref, wtt_ref, bias_ref, g_ref, b_ref, dp_ref, st_ref, gbt_ref, gw_ref,
             vb, n2buf, rstdbuf, sbuf, dsb, dvbuf, acc8, gb_ref):
        del dp_in
        i = pl.program_id(0)

        @pl.when(i == 0)
        def _():
            st_ref[...] = jnp.zeros((SUB, D), F32)
            gbt_ref[...] = jnp.zeros((CHUNK, LANE), F32)
            gb_ref[...] = jnp.zeros((CHUNK, D), F32)
            gw_ref[...] = jnp.zeros((HEADS, CHUNK, CHUNK), F32)
            acc8[...] = jnp.zeros((2 * PACK, D), F32)

        def norm(r0):
            gv = _gelu(p_ref[pl.ds(r0, PACK), pl.ds(D, D)].astype(F32))
            mu = _rowmean(gv)
            vc = gv - mu
            rstd = lax.rsqrt(_rowmean(vc * vc) + EPS)
            n2 = vc * rstd
            n2buf[pl.ds(r0, PACK), :] = n2
            rstdbuf[pl.ds(r0, PACK), :] = jnp.broadcast_to(rstd, (PACK, LANE))
            vb[pl.ds(r0, PACK), :] = (n2 * g_ref[...] + b_ref[...]).astype(BF16)
        _strips(tm, PACK,norm)

        for ck in range(tm // CHUNK):
            for h in range(HEADS):
                blk = (pl.ds(ck * CHUNK, CHUNK), pl.ds(h * LANE, LANE))
                sbuf[blk] = jnp.dot(wt_ref[h], vb[blk], preferred_element_type=F32) + bias_ref[:, pl.ds(h * LANE, LANE)]

        def gate_bwd(r0):
            pu = p_ref[pl.ds(r0, PACK), pl.ds(0, D)].astype(F32)
            u = _gelu(pu)
            z = p_ref[pl.ds(r0, PACK), pl.ds(2 * D, D)].astype(F32)
            sg = _sigmoid(z)
            sz = z * sg
            s = sbuf[pl.ds(r0, PACK), :]
            dyb = dyb_ref[pl.ds(r0, PACK), :]
            ds = dyb * u * sz
            dsb[pl.ds(r0, PACK), :] = ds.astype(BF16)
            gb_ref[pl.ds(pl.multiple_of(r0 % CHUNK, PACK), PACK), :] += ds
            dp_ref[pl.ds(r0, PACK), pl.ds(0, D)] = (dyb * s * sz * _dgelu(pu)).astype(BF16)
            dp_ref[pl.ds(r0, PACK), pl.ds(2 * D, D)] = (dyb * u * s * _dsilu(z, sg)).astype(BF16)
        _strips(tm, PACK,gate_bwd)

        for ck in range(tm // CHUNK):
            for h in range(HEADS):
                blk = (pl.ds(ck * CHUNK, CHUNK), pl.ds(h * LANE, LANE))
                d_s = dsb[blk]
                dvbuf[blk] = jnp.dot(wtt_ref[h], d_s, preferred_element_type=F32)
                gw_ref[h] += _dot_t(d_s, vb[blk])

        def norm_bwd(r0):
            dv = dvbuf[pl.ds(r0, PACK), :]
            n2 = n2buf[pl.ds(r0, PACK), :]
            rstd = rstdbuf[pl.ds(r0, PACK), pl.ds(0, 1)]
            acc8[pl.ds(0, PACK), :] += dv * n2
            acc8[pl.ds(PACK, PACK), :] += dv
            dn2 = dv * g_ref[...]
            dgv = rstd * (dn2 - _rowmean(dn2) - n2 * _rowmean(dn2 * n2))
            dp_ref[pl.ds(r0, PACK), pl.ds(D, D)] = (dgv * _dgelu(p_ref[pl.ds(r0, PACK), pl.ds(D, D)].astype(F32))).astype(BF16)
        _strips(tm, PACK,norm_bwd)

        @pl.when(i == n_i - 1)
        def _():
            for j in range(2):
                st_ref[pl.ds(j, 1), :] = jnp.sum(acc8[pl.ds(j * PACK, PACK), :], axis=0, keepdims=True)
            row = lax.broadcasted_iota(jnp.int32, (CHUNK, CHUNK), 0)
            col = lax.broadcasted_iota(jnp.int32, (CHUNK, CHUNK), 1)
            for h in range(HEADS):
                gw_ref[h] = jnp.where(row >= col, gw_ref[h], 0.0)
            lane = lax.broadcasted_iota(jnp.int32, (CHUNK, LANE), 1)
            gbt = jnp.zeros((CHUNK, LANE), F32)
            for h in range(HEADS):
                gbt = jnp.where(lane == h, jnp.sum(gb_ref[:, pl.ds(h * LANE, LANE)], axis=1, keepdims=True), gbt)
            gbt_ref[...] = gbt

    wspec = pl.BlockSpec((HEADS, CHUNK, CHUNK), lambda i: (0, 0, 0))
    return pl.pallas_call(
        body, name="branch_b_bwd", grid=(n_i,),
        in_specs=[pl.BlockSpec((tm, 3 * D), lambda i: (i, 1)), pl.BlockSpec((tm, D), lambda i: (i, 0)),
                  pl.BlockSpec(memory_space=pl.ANY), wspec, wspec,
                  pl.BlockSpec((CHUNK, D), lambda i: (0, 0)), _vec_spec(1), _vec_spec(1)],
        out_specs=[pl.BlockSpec((tm, 3 * D), lambda i: (i, 1)), pl.BlockSpec((SUB, D), lambda i: (0, 0)),
                   pl.BlockSpec((CHUNK, LANE), lambda i: (0, 0)), wspec],
        out_shape=[_sds(dp.shape, BF16), _sds((SUB, D), F32), _sds((CHUNK, LANE), F32), _sds((HEADS, CHUNK, CHUNK), F32)],
        scratch_shapes=[pltpu.VMEM((tm, D), BF16), pltpu.VMEM((tm, D), F32), pltpu.VMEM((tm, LANE), F32),
                        pltpu.VMEM((tm, D), F32), pltpu.VMEM((tm, D), BF16), pltpu.VMEM((tm, D), F32),
                        pltpu.VMEM((2 * PACK, D), F32), pltpu.VMEM((CHUNK, D), F32)],
        input_output_aliases={2: 0},
        compiler_params=_params(("arbitrary",)),
    )(p, dyb_in, dp, wt, wtt, bias_full, ln_g, ln_b)


def _in_proj_bwd(dp, wg_in, x, dx2, shift, scale, g_pre):
    del shift
    s_len = x.shape[0]
    tm = min(256, s_len)
    n_i = s_len // tm
    wn = wg_in.shape[2]

    def body(dp_ref, w_ref, x_ref, dx2_ref, sc_ref, g_ref, gx_ref, st_ref, acc, acc8):
        i = pl.program_id(0)

        @pl.when(i == 0)
        def _():
            st_ref[...] = jnp.zeros((SUB, D), F32)
            acc8[...] = jnp.zeros((3 * PACK, D), F32)

        dh = _dot_t(dp_ref[:, pl.ds(0, wn)], w_ref[0])
        for j in range(1, N_CHIP):
            dh = dh + _dot_t(dp_ref[:, pl.ds(j * wn, wn)], w_ref[j])
        acc[...] = dh

        def strip(r0):
            xs = x_ref[pl.ds(r0, PACK), :]
            r = lax.rsqrt(_rowmean(xs * xs) + EPS)
            xn = xs * r
            dhs = acc[pl.ds(r0, PACK), :]
            acc8[pl.ds(0, PACK), :] += dhs
            acc8[pl.ds(PACK, PACK), :] += dhs * (xn * g_ref[...])
            dhp = dhs * (1.0 + sc_ref[...])
            acc8[pl.ds(2 * PACK, PACK), :] += dhp * xn
            dxn = dhp * g_ref[...]
            gx_ref[pl.ds(r0, PACK), :] = dx2_ref[pl.ds(r0, PACK), :] + r * (dxn - xn * _rowmean(dxn * xn))
        _strips(tm, PACK, strip)

        @pl.when(i == n_i - 1)
        def _():
            for k in range(3):
                st_ref[pl.ds(k, 1), :] = jnp.sum(acc8[pl.ds(k * PACK, PACK), :], axis=0, keepdims=True)

    tile = pl.BlockSpec((tm, D), lambda i: (i, 0))
    return pl.pallas_call(
        body, name="in_proj_bwd", grid=(n_i,),
        in_specs=[pl.BlockSpec((tm, N_CHIP * wn), lambda i: (i, 0)),
                  pl.BlockSpec((N_CHIP, D, wn), lambda i: (0, 0, 0), pipeline_mode=pl.Buffered(1)),
                  tile, tile, _vec_spec(1), _vec_spec(1)],
        out_specs=[tile, pl.BlockSpec((SUB, D), lambda i: (0, 0))],
        out_shape=[_sds((s_len, D), F32), _sds((SUB, D), F32)],
        scratch_shapes=[pltpu.VMEM((tm, D), F32), pltpu.VMEM((3 * PACK, D), F32)],
        compiler_params=_params(("arbitrary",)),
    )(dp, wg_in, x, dx2, scale, g_pre)


def _grad_matmul(a, b, name):
    s_len, n = b.shape
    cb = min(2 * D, n)
    tn = 512
    per = cb // tn

    def body(a_ref, b_ref, ob_ref):
        ob_ref[0] = lax.dot_general(a_ref[...], b_ref[...], (((0,), (0,)), ((), ())),
                                    preferred_element_type=F32).astype(BF16)

    return pl.pallas_call(
        body, name=name, grid=(n // tn,),
        in_specs=[pl.BlockSpec((s_len, D), lambda j: (0, 0), pipeline_mode=pl.Buffered(1)),
                  pl.BlockSpec((s_len, tn), lambda j: (0, j))],
        out_specs=pl.BlockSpec((1, D, tn), lambda j: (j // per, 0, j % per)),
        out_shape=_sds((n // cb, D, cb), BF16),
        compiler_params=_params(("arbitrary",)),
    )(a, b)


def _local_step(x, target, shift, scale, gate, g_pre, conv_w_full, conv_b, conv_ln_g, conv_ln_b,
                sgu_ln_g, sgu_ln_b, w_sgu, b_sgu, g_final, wg_in, w_co, w_so, w_o):
    conv_wb = jnp.repeat(conv_w_full, SUB, axis=0)
    causal = jnp.tril(jnp.ones((CHUNK, CHUNK), dtype=bool))
    wt = jnp.where(causal[None], w_sgu, 0.0).astype(BF16)
    wtt = jnp.swapaxes(wt, 1, 2)
    bias_full = jnp.repeat(b_sgu.T, LANE, axis=1)

    p, hb = _in_proj(x, shift, scale, g_pre, wg_in)
    ya_in, y1 = _branch_a_fwd(p, conv_wb, conv_b, conv_ln_g, conv_ln_b)
    yb_in = _branch_b_fwd(p, wt, bias_full, sgu_ln_g, sgu_ln_b)
    dx2, dya_in, dyb_in, dp, mb, dob, dyab, dybb, sums_o = _out_proj(
        p, ya_in, yb_in, x, target, gate, g_final, w_co, w_so, w_o)
    dp, st_a = _branch_a_bwd(p, p, y1, dya_in, dp, conv_wb, conv_ln_g, conv_ln_b)
    dp, st_b, gbt, gws = _branch_b_bwd(p, dyb_in, dp, wt, wtt, bias_full, sgu_ln_g, sgu_ln_b)
    grad_x, st_i = _in_proj_bwd(dp, wg_in, x, dx2, shift, scale, g_pre)
    gw_in = _grad_matmul(hb, dp, "grad_w_in")
    gw_o = _grad_matmul(mb, dob, "grad_w_o")
    gw_co = _grad_matmul(ya_in, dyab, "grad_w_conv_out")
    gw_so = _grad_matmul(yb_in, dybb, "grad_w_sgu_out")
    return dict(
        grad_x=grad_x, loss_cols=sums_o[2:3], g_final=sums_o[0:1], d_gate=sums_o[1:2],
        d_shift=st_i[0:1], d_scale=st_i[1:2], g_pre=st_i[2:3],
        conv_ln_g=st_a[0:1], conv_ln_b=st_a[1:2], conv_b=st_a[2:3], conv_w=st_a[SUB:SUB + CONV_K],
        sgu_ln_g=st_b[0:1], sgu_ln_b=st_b[1:2], b_sgu=gbt[:, :HEADS].T, w_sgu=gws,
        w_in=gw_in, w_o=gw_o, w_conv_out=gw_co, w_sgu_out=gw_so)


ANY_SPEC = pl.BlockSpec(memory_space=pl.ANY)
VMEM_SPEC = pl.BlockSpec(memory_space=pltpu.VMEM)


def _place():
    return lax.axis_index("x"), lax.axis_index("y"), lax.axis_index("c")


def _peer(k):
    x, y, c = _place()
    return (1 - x if k & 4 else x, 1 - y if k & 2 else y, 1 - c if k & 1 else c)


def _dev_of(p):
    return 4 * p[0] + 2 * p[1] + p[2]


def _chip_of(p):
    return 2 * p[0] + p[1]


def _rdma(src, dst, send_sem, recv_sem, to):
    return pltpu.make_async_remote_copy(src_ref=src, dst_ref=dst, send_sem=send_sem, recv_sem=recv_sem,
                                        device_id=to, device_id_type=MESH)


CHIP_PEERS = (2, 4, 6)
ALL_PEERS = tuple(range(1, N_DEV))
SIBLING = 1


def _setup_comm(c8, w_ada_s, b_ada_s, convw_s):
    n_mod = w_ada_s.shape[1]
    rows = SUB * N_DEV

    def body(c8_ref, wada_ref, bada_ref, cw_ref, call_ref, mod_ref, cwall_ref, csend, crecv, wsend, wrecv, msend, mrecv):
        me = _place()
        dev, chip = _dev_of(me), _chip_of(me)

        def c_rows(d):
            return call_ref.at[pl.ds(pl.multiple_of(d * SUB, SUB), SUB), :]

        call_ref[pl.ds(pl.multiple_of(dev * SUB, SUB), SUB), :] = c8_ref[...]
        cwall_ref[chip] = cw_ref[...]
        c_out = [_rdma(c8_ref, c_rows(dev), csend.at[k], crecv.at[k], _peer(k)) for k in ALL_PEERS]
        w_out = [_rdma(cw_ref, cwall_ref.at[chip], wsend.at[k], wrecv.at[k], _peer(k)) for k in CHIP_PEERS]
        for cp in c_out + w_out:
            cp.start()
        for k in ALL_PEERS:
            _rdma(c8_ref, c_rows(_dev_of(_peer(k))), csend.at[k], crecv.at[k], _peer(k)).wait_recv()
        part = jnp.dot(call_ref[...].astype(BF16), wada_ref[...].astype(BF16), preferred_element_type=F32) + bada_ref[...]
        mod_ref[chip] = part
        m_out = [_rdma(mod_ref.at[chip], mod_ref.at[chip], msend.at[k], mrecv.at[k], _peer(k)) for k in CHIP_PEERS]
        for cp in m_out:
            cp.start()
        for k in CHIP_PEERS:
            pc = _chip_of(_peer(k))
            _rdma(cw_ref, cwall_ref.at[pc], wsend.at[k], wrecv.at[k], _peer(k)).wait_recv()
            _rdma(mod_ref.at[pc], mod_ref.at[pc], msend.at[k], mrecv.at[k], _peer(k)).wait_recv()
        for cp in c_out + w_out + m_out:
            cp.wait_send()

    return pl.pallas_call(
        body, name="setup_comm",
        in_specs=[VMEM_SPEC] * 4, out_specs=[VMEM_SPEC] * 3,
        out_shape=[_sds((rows, D), F32), _sds((N_CHIP, rows, n_mod), F32), _sds((N_CHIP,) + convw_s.shape, F32)],
        scratch_shapes=[pltpu.SemaphoreType.DMA((N_DEV,))] * 6,
        compiler_params=_params(),
    )(c8, w_ada_s, b_ada_s, convw_s)


def _gather_weights(shards):
    n = len(shards)

    def body(*refs):
        ins, outs = refs[:n], refs[n:2 * n]
        lsem, isend, irecv, dsend, drecv = refs[2 * n:]
        me = _place()
        chip, c = _chip_of(me), me[2]
        local = [pltpu.make_async_copy(ins[t], outs[t].at[chip], lsem.at[t]) for t in range(n)]
        for cp in local:
            cp.start()

        def half(t, which):
            hr = shards[t].shape[0] // 2
            return pl.ds(pl.multiple_of(which * hr, hr), hr)

        sends = []
        for t in range(n):
            for j, k in enumerate(CHIP_PEERS):
                cp = _rdma(ins[t].at[half(t, c)], outs[t].at[chip, half(t, c)], isend.at[t, j], irecv.at[t, j], _peer(k))
                cp.start()
                sends.append(cp)
        for t in range(n):
            for j, k in enumerate(CHIP_PEERS):
                blk = outs[t].at[_chip_of(_peer(k)), half(t, c)]
                _rdma(blk, blk, isend.at[t, j], irecv.at[t, j], _peer(k)).wait_recv()
                cp = _rdma(blk, blk, dsend.at[t, j], drecv.at[t, j], _peer(SIBLING))
                cp.start()
                sends.append(cp)
        for t in range(n):
            for j, k in enumerate(CHIP_PEERS):
                blk = outs[t].at[_chip_of(_peer(k)), half(t, 1 - c)]
                _rdma(blk, blk, dsend.at[t, j], drecv.at[t, j], _peer(SIBLING)).wait_recv()
        for cp in sends:
            cp.wait_send()
        for cp in local:
            cp.wait()

    return pl.pallas_call(
        body, name="gather_weights",
        in_specs=[VMEM_SPEC] * n, out_specs=[VMEM_SPEC] * n,
        out_shape=[_sds((N_CHIP,) + s.shape, s.dtype) for s in shards],
        scratch_shapes=[pltpu.SemaphoreType.DMA((n,))] + [pltpu.SemaphoreType.DMA((n, len(CHIP_PEERS)))] * 4,
        compiler_params=_params(),
    )(*shards)


def _scatter_grads(grads):
    n = len(grads)

    def body(*refs):
        ins, outs = refs[:n], refs[n:2 * n]
        lsem, send, recv = refs[2 * n:]
        me = _place()
        dev, chip, c = _dev_of(me), _chip_of(me), me[2]
        local = [pltpu.make_async_copy(ins[t].at[chip, c], outs[t].at[dev], lsem.at[t]) for t in range(n)]
        for cp in local:
            cp.start()
        sends = []
        for t in range(n):
            for k in ALL_PEERS:
                to = _peer(k)
                cp = _rdma(ins[t].at[_chip_of(to), to[2]], outs[t].at[dev], send.at[t, k], recv.at[t, k], to)
                cp.start()
                sends.append(cp)
        for t in range(n):
            for k in ALL_PEERS:
                blk = outs[t].at[_dev_of(_peer(k))]
                _rdma(blk, blk, send.at[t, k], recv.at[t, k], _peer(k)).wait_recv()
        for cp in sends:
            cp.wait_send()
        for cp in local:
            cp.wait()

    return pl.pallas_call(
        body, name="scatter_grads",
        in_specs=[ANY_SPEC] * n, out_specs=[ANY_SPEC] * n,
        out_shape=[_sds((N_DEV,) + g.shape[2:], g.dtype) for g in grads],
        scratch_shapes=[pltpu.SemaphoreType.DMA((n,))] + [pltpu.SemaphoreType.DMA((n, N_DEV))] * 2,
        compiler_params=_params(),
    )(*grads)


def _sum_devices(parts, name):
    _, r, cols = parts.shape
    tr = min(r, 128)

    def body(in_ref, o_ref):
        acc = in_ref[0].astype(F32)
        for d in range(1, N_DEV):
            acc = acc + in_ref[d].astype(F32)
        o_ref[...] = acc

    return pl.pallas_call(
        body, name=name, grid=(r // tr,),
        in_specs=[pl.BlockSpec((N_DEV, tr, cols), lambda i: (0, i, 0))],
        out_specs=pl.BlockSpec((tr, cols), lambda i: (i, 0)),
        out_shape=_sds((r, cols), F32),
        compiler_params=_params(("arbitrary",)),
    )(parts)


def _share_halves(reds):
    n = len(reds)

    def body(*refs):
        ins, outs = refs[:n], refs[n:2 * n]
        lsem, send, recv = refs[2 * n:]
        me = _place()
        c = me[2]
        local = [pltpu.make_async_copy(ins[t], outs[t].at[c], lsem.at[t]) for t in range(n)]
        sends = [_rdma(ins[t], outs[t].at[c], send.at[t], recv.at[t], _peer(SIBLING)) for t in range(n)]
        for cp in local + sends:
            cp.start()
        for t in range(n):
            _rdma(ins[t], outs[t].at[1 - c], send.at[t], recv.at[t], _peer(SIBLING)).wait_recv()
        for cp in sends:
            cp.wait_send()
        for cp in local:
            cp.wait()

    return pl.pallas_call(
        body, name="share_halves",
        in_specs=[VMEM_SPEC] * n, out_specs=[VMEM_SPEC] * n,
        out_shape=[_sds((2,) + r.shape, r.dtype) for r in reds],
        scratch_shapes=[pltpu.SemaphoreType.DMA((n,))] * 3,
        compiler_params=_params(),
    )(*reds)


def _sum_small(blob):
    rows = blob.shape[0]

    def body(b_ref, o_ref, pbuf, buf4, psend, precv, send, recv):
        me = _place()
        chip = _chip_of(me)
        pair = _rdma(b_ref, pbuf, psend, precv, _peer(SIBLING))
        pair.start()
        pair.wait()
        buf4[chip] = b_ref[...] + pbuf[...]
        out = [_rdma(buf4.at[chip], buf4.at[chip], send.at[k], recv.at[k], _peer(k)) for k in CHIP_PEERS]
        for cp in out:
            cp.start()
        for k in CHIP_PEERS:
            blk = buf4.at[_chip_of(_peer(k))]
            _rdma(blk, blk, send.at[k], recv.at[k], _peer(k)).wait_recv()
        o_ref[...] = (buf4[0] + buf4[1]) + (buf4[2] + buf4[3])
        for cp in out:
            cp.wait_send()

    return pl.pallas_call(
        body, name="sum_small",
        in_specs=[VMEM_SPEC], out_specs=VMEM_SPEC, out_shape=_sds(blob.shape, F32),
        scratch_shapes=[pltpu.VMEM((rows, D), F32), pltpu.VMEM((N_CHIP, rows, D), F32),
                        pltpu.SemaphoreType.DMA, pltpu.SemaphoreType.DMA,
                        pltpu.SemaphoreType.DMA((N_DEV,)), pltpu.SemaphoreType.DMA((N_DEV,))],
        compiler_params=_params(),
    )(blob)


def _adamw_math(w, g, m, v):
    m = ADAM_B1 * m + (1.0 - ADAM_B1) * g
    v = ADAM_B2 * v + (1.0 - ADAM_B2) * (g * g)
    m_hat = m / (1.0 - ADAM_B1 ** ADAM_STEP)
    v_hat = v / (1.0 - ADAM_B2 ** ADAM_STEP)
    delta = -ADAM_LR * (m_hat / (jnp.sqrt(v_hat) + ADAM_EPS) + ADAM_WD * w)
    return delta, m, v


def _row_tile(r, cols):
    if r * cols * 4 <= 2 ** 20:
        return r
    return next(t for t in (512, 256, 128, 64, 32, 16, 8) if r % t == 0 and t * cols * 4 <= 2 ** 20)


def _adamw(w, g, m, v, name):
    r, cols = w.shape
    tr = _row_tile(r, cols)

    def body(w_ref, g_ref, m_ref, v_ref, d_ref, nm_ref, nv_ref):
        d_ref[...], nm_ref[...], nv_ref[...] = _adamw_math(w_ref[...], g_ref[...], m_ref[...], v_ref[...])

    spec = pl.BlockSpec((tr, cols), lambda i: (i, 0))
    return pl.pallas_call(
        body, name=name, grid=(r // tr,), in_specs=[spec] * 4, out_specs=[spec] * 3,
        out_shape=[_sds((r, cols), F32)] * 3, compiler_params=_params(("arbitrary",)),
    )(w, g, m, v)


def _adamw_ada(w, ct, dm, m, v):
    r, cols = w.shape
    tr = _row_tile(r, cols)

    def body(w_ref, ct_ref, dm_ref, m_ref, v_ref, g_ref, d_ref, nm_ref, nv_ref):
        g = jnp.dot(ct_ref[...], dm_ref[...], preferred_element_type=F32)
        g_ref[...] = g
        d_ref[...], nm_ref[...], nv_ref[...] = _adamw_math(w_ref[...], g, m_ref[...], v_ref[...])

    spec = pl.BlockSpec((tr, cols), lambda i: (i, 0))
    return pl.pallas_call(
        body, name="adamw_ada", grid=(r // tr,),
        in_specs=[spec, pl.BlockSpec((tr, LANE), lambda i: (i, 0)), pl.BlockSpec((LANE, cols), lambda i: (0, 0)), spec, spec],
        out_specs=[spec] * 4, out_shape=[_sds((r, cols), F32)] * 4, compiler_params=_params(("arbitrary",)),
    )(w, ct, dm, m, v)


BLOB_VEC, BLOB_CONV, BLOB_SGU, BLOB_ADA, BLOB_DMOD, BLOB_LOSS, BLOB_ROWS = 0, 8, 40, 168, 176, 200, 208
SMALL_CONV, SMALL_SGU, SMALL_ADA, SMALL_ROWS = 8, 16, 144, 152


def _set_rows(buf, row, val):
    return lax.dynamic_update_slice(buf, val.astype(F32), (row, 0))


def _pack_small(vecs, b_sgu, conv_w_s, w_sgu, b_ada):
    buf = jnp.zeros((SMALL_ROWS, D), F32)
    for i, vec in enumerate(vecs):
        buf = _set_rows(buf, i, vec.reshape(1, D))
    buf = _set_rows(buf, 7, b_sgu.reshape(1, D))
    buf = _set_rows(buf, SMALL_CONV, jnp.pad(conv_w_s, ((0, 1), (0, 0))).reshape(SUB, D))
    buf = _set_rows(buf, SMALL_SGU, w_sgu.reshape(CHUNK, D))
    return _set_rows(buf, SMALL_ADA, b_ada.reshape(3, D))


def _unpack_small(buf, conv_cols):
    vecs = [buf[i] for i in range(7)]
    b_sgu = buf[7].reshape(HEADS, CHUNK)
    conv_w_s = buf[SMALL_CONV:SMALL_CONV + SUB].reshape(HALO, conv_cols)[:CONV_K]
    w_sgu = buf[SMALL_SGU:SMALL_SGU + CHUNK].reshape(HEADS, CHUNK, CHUNK)
    b_ada = buf[SMALL_ADA:SMALL_ADA + 3].reshape(1, 3 * D)
    return vecs, b_sgu, conv_w_s, w_sgu, b_ada


def kernel(x, c, w_ada, b_ada, g_pre, w_in, conv_w, conv_b, conv_ln_g, conv_ln_b, w_conv_out, sgu_ln_g, sgu_ln_b, w_sgu, b_sgu, w_sgu_out, w_o, g_final, loss_target, m_w_ada, m_b_ada, m_g_pre, m_w_in, m_conv_w, m_conv_b, m_conv_ln_g, m_conv_ln_b, m_w_conv_out, m_sgu_ln_g, m_sgu_ln_b, m_w_sgu, m_b_sgu, m_w_sgu_out, m_w_o, m_g_final, v_w_ada, v_b_ada, v_g_pre, v_w_in, v_conv_w, v_conv_b, v_conv_ln_g, v_conv_ln_b, v_w_conv_out, v_sgu_ln_g, v_sgu_ln_b, v_w_sgu, v_b_sgu, v_w_sgu_out, v_w_o, v_g_final):
    me = _place()
    dev, chip = _dev_of(me), _chip_of(me)
    n_ada = w_ada.shape[2]
    conv_cols = conv_w.shape[2]

    b_ada_s = lax.dynamic_slice(b_ada, (0, chip * n_ada), (1, n_ada))
    c_all, mod_all, cw_all = _setup_comm(
        jnp.broadcast_to(c, (SUB, D)), w_ada[0], b_ada_s, jnp.pad(conv_w[0], ((0, HALO - CONV_K), (0, 0))))
    mod = lax.dynamic_slice(mod_all, (0, dev * SUB, 0), (N_CHIP, 1, n_ada)).reshape(1, 3 * D)
    shift, scale, gate = mod[:, :D], mod[:, D:2 * D], mod[:, 2 * D:]
    conv_w_full = jnp.swapaxes(cw_all, 0, 1).reshape(HALO, D)[:CONV_K]

    wg_in, wg_co, wg_so, wg_o = _gather_weights(
        [w_in[0].astype(BF16), w_conv_out[0].astype(BF16), w_sgu_out[0].astype(BF16), w_o[0].astype(BF16)])

    loc = _local_step(x[0], loss_target[0], shift, scale, gate, g_pre, conv_w_full, conv_b, conv_ln_g, conv_ln_b,
                      sgu_ln_g, sgu_ln_b, w_sgu[0], b_sgu[0], g_final.reshape(1, D),
                      wg_in, wg_co.reshape(D, D), wg_so.reshape(D, D), wg_o.reshape(D, D))

    big = ["w_in", "w_conv_out", "w_sgu_out", "w_o"]
    contrib = []
    for name in big:
        g16 = loc[name]
        rows_half = (g16.shape[0] * g16.shape[1]) // (2 * N_CHIP) if name != "w_in" else g16.shape[1] // 2
        contrib.append(g16.reshape(N_CHIP, 2, rows_half, g16.shape[2]))
    parts = _scatter_grads(contrib)
    reds = [_sum_devices(p, "sum_" + name) for p, name in zip(parts, big)]
    full = _share_halves(reds)
    g_big = {name: f.reshape(2 * f.shape[1], f.shape[2]) for name, f in zip(big, full)}

    d_mod = jnp.concatenate([loc["d_shift"], loc["d_scale"], loc["d_gate"]], axis=0)
    blob = jnp.zeros((BLOB_ROWS, D), F32)
    for i, name in enumerate(["g_pre", "conv_b", "conv_ln_g", "conv_ln_b", "sgu_ln_g", "sgu_ln_b", "g_final"]):
        blob = _set_rows(blob, BLOB_VEC + i, loc[name])
    blob = _set_rows(blob, BLOB_VEC + 7, loc["b_sgu"].reshape(1, D))
    blob = _set_rows(blob, BLOB_CONV, loc["conv_w"])
    blob = _set_rows(blob, BLOB_SGU, loc["w_sgu"].reshape(CHUNK, D))
    blob = _set_rows(blob, BLOB_ADA, d_mod)
    blob = lax.dynamic_update_slice(blob, d_mod, (BLOB_DMOD + 3 * dev, 0))
    blob = _set_rows(blob, BLOB_LOSS, loc["loss_cols"])
    tot = _sum_small(blob)

    loss = jnp.sum(tot[BLOB_LOSS])
    g_vecs = [tot[BLOB_VEC + i] for i in range(7)]
    g_b_sgu = tot[BLOB_VEC + 7].reshape(HEADS, CHUNK)
    g_conv_s = lax.dynamic_slice(tot, (BLOB_CONV, chip * conv_cols), (CONV_K, conv_cols))
    g_w_sgu = tot[BLOB_SGU:BLOB_SGU + CHUNK].reshape(HEADS, CHUNK, CHUNK)
    g_b_ada = tot[BLOB_ADA:BLOB_ADA + 3].reshape(1, 3 * D)
    d_mod_all = tot[BLOB_DMOD:BLOB_DMOD + 3 * N_DEV].reshape(N_DEV, 3 * D)

    ct = jnp.pad(c_all[::SUB].T, ((0, 0), (0, LANE - N_DEV))).astype(BF16)
    dm = jnp.pad(lax.dynamic_slice(d_mod_all, (0, chip * n_ada), (N_DEV, n_ada)), ((0, LANE - N_DEV), (0, 0))).astype(BF16)
    g_ada, d_ada, nm_ada, nv_ada = _adamw_ada(w_ada[0], ct, dm, m_w_ada[0], v_w_ada[0])

    upd = {}
    for name, w, m, v in [("w_in", w_in, m_w_in, v_w_in), ("w_conv_out", w_conv_out, m_w_conv_out, v_w_conv_out),
                          ("w_sgu_out", w_sgu_out, m_w_sgu_out, v_w_sgu_out), ("w_o", w_o, m_w_o, v_w_o)]:
        upd[name] = _adamw(w[0], g_big[name], m[0], v[0], "adamw_" + name)

    def small(g_p, conv_p, sgu_p, ada_p, vec_ps):
        return _pack_small(vec_ps, g_p, conv_p, sgu_p, ada_p)

    w_s = small(b_sgu[0], conv_w[0], w_sgu[0], b_ada, [g_pre, conv_b, conv_ln_g, conv_ln_b, sgu_ln_g, sgu_ln_b, g_final])
    m_s = small(m_b_sgu[0], m_conv_w[0], m_w_sgu[0], m_b_ada,
                [m_g_pre, m_conv_b, m_conv_ln_g, m_conv_ln_b, m_sgu_ln_g, m_sgu_ln_b, m_g_final])
    v_s = small(v_b_sgu[0], v_conv_w[0], v_w_sgu[0], v_b_ada,
                [v_g_pre, v_conv_b, v_conv_ln_g, v_conv_ln_b, v_sgu_ln_g, v_sgu_ln_b, v_g_final])
    g_s = small(g_b_sgu, g_conv_s, g_w_sgu, g_b_ada, g_vecs)
    small_out = [_unpack_small(a, conv_cols) for a in (g_s,) + tuple(_adamw(w_s, g_s, m_s, v_s, "adamw_small"))]

    def leaves(kind):
        vecs, o_b_sgu, o_conv, o_w_sgu, o_b_ada = small_out[kind]
        ada = (g_ada, d_ada, nm_ada, nv_ada)[kind]
        def bigk(name):
            return (g_big[name] if kind == 0 else upd[name][kind - 1])[None]
        return [ada[None], o_b_ada, vecs[0][None], bigk("w_in"), o_conv[None], vecs[1][None], vecs[2][None], vecs[3][None],
                bigk("w_conv_out"), vecs[4][None], vecs[5][None], o_w_sgu[None], o_b_sgu[None], bigk("w_sgu_out"),
                bigk("w_o"), vecs[6]]

    return (loss, loc["grad_x"][None], *leaves(0), *leaves(1), *leaves(2), *leaves(3))
```

```python
import functools

import jax
import jax.numpy as jnp
from jax import lax
from jax.experimental import pallas as pl
from jax.experimental.pallas import tpu as pltpu

F32 = jnp.float32
BF16 = jnp.bfloat16
MESH = pl.DeviceIdType.MESH

D = 1024
N_SEC = 8
N_CHIP = 4
N_DEV = 8
EPS = 1e-6
CONV_K = 31
HALO = 32
CHUNK = 128
HEADS = 8
LANE = 128
SUB = 8
PACK = 16
VMEM_LIMIT = 56 * 1024 * 1024

ADAM_LR, ADAM_B1, ADAM_B2, ADAM_EPS, ADAM_WD, ADAM_STEP = 0.001, 0.9, 0.999, 1e-08, 0.01, 10

_SQRT_HALF = 0.7071067811865476
_INV_SQRT_2PI = 0.3989422804014327


def _sds(shape, dtype):
    return jax.ShapeDtypeStruct(shape, dtype)


def _params(sem=None):
    if sem is None:
        return pltpu.CompilerParams(vmem_limit_bytes=VMEM_LIMIT)
    return pltpu.CompilerParams(dimension_semantics=sem, vmem_limit_bytes=VMEM_LIMIT)


def _strips(n_rows, rows, fn):
    def step(s, carry):
        fn(pl.multiple_of(s * rows, rows))
        return carry
    lax.fori_loop(0, n_rows // rows, step, 0)


def _sigmoid(v):
    return 1.0 / (1.0 + jnp.exp(-v))


def _gelu(v):
    return 0.5 * v * (1.0 + lax.erf(v * _SQRT_HALF))


def _dgelu(v):
    return 0.5 * (1.0 + lax.erf(v * _SQRT_HALF)) + v * jnp.exp(-0.5 * v * v) * _INV_SQRT_2PI


def _dsilu(v, sg):
    return sg * (1.0 + v * (1.0 - sg))


def _rowmean(v):
    return jnp.mean(v, axis=-1, keepdims=True)


def _vec_spec(grid_rank):
    zeros = (0, 0)
    if grid_rank == 1:
        return pl.BlockSpec((1, D), lambda i: zeros)
    return pl.BlockSpec((1, D), lambda i, j: zeros)


def _in_proj(x, shift, scale, g_pre, wg_in):
    s_len = x.shape[0]
    tm = min(256, s_len)
    n_i = s_len // tm
    wn = wg_in.shape[2]

    def body(x_ref, sh_ref, sc_ref, g_ref, w_ref, p_ref, hb_ref):
        def strip(r0):
            xs = x_ref[pl.ds(r0, PACK), :]
            r = lax.rsqrt(_rowmean(xs * xs) + EPS)
            h = (xs * r) * g_ref[...] * (1.0 + sc_ref[...]) + sh_ref[...]
            hb_ref[pl.ds(r0, PACK), :] = h.astype(BF16)
        _strips(tm, PACK, strip)
        hb = hb_ref[...]
        for j in range(N_CHIP):
            p_ref[:, pl.ds(j * wn, wn)] = jnp.dot(hb, w_ref[j], preferred_element_type=F32).astype(BF16)

    return pl.pallas_call(
        body, name="in_proj", grid=(n_i,),
        in_specs=[pl.BlockSpec((tm, D), lambda i: (i, 0)), _vec_spec(1), _vec_spec(1), _vec_spec(1),
                  pl.BlockSpec((N_CHIP, D, wn), lambda i: (0, 0, 0), pipeline_mode=pl.Buffered(1))],
        out_specs=[pl.BlockSpec((tm, N_CHIP * wn), lambda i: (i, 0)), pl.BlockSpec((tm, D), lambda i: (i, 0))],
        out_shape=[_sds((s_len, N_SEC * D), BF16), _sds((s_len, D), BF16)],
        compiler_params=_params(("arbitrary",)),
    )(x, shift, scale, g_pre, wg_in)


def _conv_taps(win_ref, r0, lt, weight_of_offset, rows):
    lanes = pl.ds(lt * LANE, LANE)
    win = win_ref[pl.ds(r0, rows + HALO), lanes]
    n_out = rows // SUB
    acc = [jnp.zeros((SUB, LANE), F32) for _ in range(n_out)]
    for phase in range(SUB):
        offs = [o for o in weight_of_offset if o % SUB == phase]
        if not offs:
            continue
        q_max = max(o // SUB for o in offs)
        span = (n_out + q_max) * SUB
        sh = win[phase:phase + span, :]
        for o in offs:
            q = o // SUB
            w = weight_of_offset[o](lanes)
            for m in range(n_out):
                acc[m] = acc[m] + w * sh[(m + q) * SUB:(m + q + 1) * SUB, :]
    return acc


def _branch_a_fwd(p, conv_wb, conv_b, ln_g, ln_b):
    s_len = p.shape[0]
    tm = min(256, s_len)
    n_i = s_len // tm
    rows = 32

    def body(p_ref, wb_ref, cb_ref, g_ref, b_ref, ya_ref, y1_ref, abuf):
        @pl.when(pl.program_id(0) == 0)
        def _():
            abuf[pl.ds(0, HALO), :] = jnp.zeros((HALO, D), F32)

        def glu(r0):
            val = p_ref[pl.ds(r0, PACK), pl.ds(0, D)].astype(F32)
            gl = p_ref[pl.ds(r0, PACK), pl.ds(D, D)].astype(F32)
            abuf[pl.ds(HALO + r0, PACK), :] = val * _sigmoid(gl)
        _strips(tm, PACK,glu)

        taps = {HALO - (CONV_K - 1) + k: (lambda lanes, k=k: wb_ref[pl.ds(k * SUB, SUB), lanes]) for k in range(CONV_K)}

        def conv(r0):
            for lt in range(D // LANE):
                acc = _conv_taps(abuf, r0, lt, taps, rows)
                cb = cb_ref[:, pl.ds(lt * LANE, LANE)]
                for m, v in enumerate(acc):
                    y1_ref[pl.ds(r0 + m * SUB, SUB), pl.ds(lt * LANE, LANE)] = v + cb
        _strips(tm, rows, conv)

        def norm(r0):
            y1 = y1_ref[pl.ds(r0, PACK), :]
            mu = _rowmean(y1)
            yc = y1 - mu
            rstd = lax.rsqrt(_rowmean(yc * yc) + EPS)
            l1 = (yc * rstd) * g_ref[...] + b_ref[...]
            z = p_ref[pl.ds(r0, PACK), pl.ds(2 * D, D)].astype(F32)
            ya_ref[pl.ds(r0, PACK), :] = ((l1 * _sigmoid(l1)) * (z * _sigmoid(z))).astype(BF16)
        _strips(tm, PACK,norm)

        abuf[pl.ds(0, HALO), :] = abuf[pl.ds(tm, HALO), :]

    return pl.pallas_call(
        body, name="branch_a_fwd", grid=(n_i,),
        in_specs=[pl.BlockSpec((tm, 3 * D), lambda i: (i, 0)),
                  pl.BlockSpec((CONV_K * SUB, D), lambda i: (0, 0)), _vec_spec(1), _vec_spec(1), _vec_spec(1)],
        out_specs=[pl.BlockSpec((tm, D), lambda i: (i, 0)), pl.BlockSpec((tm, D), lambda i: (i, 0))],
        out_shape=[_sds((s_len, D), BF16), _sds((s_len, D), F32)],
        scratch_shapes=[pltpu.VMEM((tm + HALO, D), F32)],
        compiler_params=_params(("arbitrary",)),
    )(p, conv_wb, conv_b, ln_g, ln_b)


def _branch_b_fwd(p, wt, bias_full, ln_g, ln_b):
    s_len = p.shape[0]
    tm = min(256, s_len)
    n_i = s_len // tm

    def body(p_ref, wt_ref, bias_ref, g_ref, b_ref, yb_ref, vb, sbuf):
        def norm(r0):
            gv = _gelu(p_ref[pl.ds(r0, PACK), pl.ds(D, D)].astype(F32))
            mu = _rowmean(gv)
            vc = gv - mu
            rstd = lax.rsqrt(_rowmean(vc * vc) + EPS)
            vb[pl.ds(r0, PACK), :] = ((vc * rstd) * g_ref[...] + b_ref[...]).astype(BF16)
        _strips(tm, PACK,norm)

        for ck in range(tm // CHUNK):
            for h in range(HEADS):
                blk = (pl.ds(ck * CHUNK, CHUNK), pl.ds(h * LANE, LANE))
                sbuf[blk] = jnp.dot(wt_ref[h], vb[blk], preferred_element_type=F32) + bias_ref[:, pl.ds(h * LANE, LANE)]

        def gate(r0):
            u = _gelu(p_ref[pl.ds(r0, PACK), pl.ds(0, D)].astype(F32))
            z = p_ref[pl.ds(r0, PACK), pl.ds(2 * D, D)].astype(F32)
            yb_ref[pl.ds(r0, PACK), :] = (u * sbuf[pl.ds(r0, PACK), :] * (z * _sigmoid(z))).astype(BF16)
        _strips(tm, PACK,gate)

    return pl.pallas_call(
        body, name="branch_b_fwd", grid=(n_i,),
        in_specs=[pl.BlockSpec((tm, 3 * D), lambda i: (i, 1)),
                  pl.BlockSpec((HEADS, CHUNK, CHUNK), lambda i: (0, 0, 0)),
                  pl.BlockSpec((CHUNK, D), lambda i: (0, 0)), _vec_spec(1), _vec_spec(1)],
        out_specs=pl.BlockSpec((tm, D), lambda i: (i, 0)),
        out_shape=_sds((s_len, D), BF16),
        scratch_shapes=[pltpu.VMEM((tm, D), BF16), pltpu.VMEM((tm, D), F32)],
        compiler_params=_params(("arbitrary",)),
    )(p, wt, bias_full, ln_g, ln_b)


def _dot_t(a, b):
    return lax.dot_general(a, b, (((1,), (1,)), ((), ())), preferred_element_type=F32)


def _out_proj(p, ya_in, yb_in, x, target, gate, g_final, w_co, w_so, w_o):
    s_len = x.shape[0]
    tm = min(256, s_len)
    n_i = s_len // tm

    def body(pg_ref, ya_ref, yb_ref, x_ref, t_ref, gate_ref, gf_ref, wco_ref, wso_ref, wo_ref,
             dx2_ref, dya_ref, dyb_ref, dp_ref, mb_ref, dob_ref, dyab_ref, dybb_ref, sums_ref):
        @pl.when(pl.program_id(0) == 0)
        def _():
            sums_ref[...] = jnp.zeros((SUB, D), F32)

        y_a = jnp.dot(ya_ref[...], wco_ref[...], preferred_element_type=F32)
        y_b = jnp.dot(yb_ref[...], wso_ref[...], preferred_element_type=F32)
        ga = _sigmoid(pg_ref[:, pl.ds(0, D)].astype(F32))
        gb = _sigmoid(pg_ref[:, pl.ds(D, D)].astype(F32))
        mb = (ga * y_a + gb * y_b).astype(BF16)
        mb_ref[...] = mb
        o = jnp.dot(mb, wo_ref[...], preferred_element_type=F32)
        x2 = x_ref[...] + gate_ref[...] * o
        r2 = lax.rsqrt(_rowmean(x2 * x2) + EPS)
        xh = x2 * r2
        e = xh * gf_ref[...] - t_ref[...]
        dy = e * (1.0 / D)
        dxh = dy * gf_ref[...]
        dx2 = r2 * (dxh - xh * _rowmean(dxh * xh))
        dx2_ref[...] = dx2
        sums_ref[pl.ds(0, 1), :] += jnp.sum(dy * xh, axis=0, keepdims=True)
        sums_ref[pl.ds(1, 1), :] += jnp.sum(dx2 * o, axis=0, keepdims=True)
        sums_ref[pl.ds(2, 1), :] += jnp.sum(e * e, axis=0, keepdims=True) * (0.5 / D)
        dob = (gate_ref[...] * dx2).astype(BF16)
        dob_ref[...] = dob
        dm = _dot_t(dob, wo_ref[...])
        dy_a = (ga * dm).astype(BF16)
        dy_b = (gb * dm).astype(BF16)
        dyab_ref[...] = dy_a
        dybb_ref[...] = dy_b
        dp_ref[:, pl.ds(0, D)] = (dm * y_a * ga * (1.0 - ga)).astype(BF16)
        dp_ref[:, pl.ds(D, D)] = (dm * y_b * gb * (1.0 - gb)).astype(BF16)
        dya_ref[...] = _dot_t(dy_a, wco_ref[...])
        dyb_ref[...] = _dot_t(dy_b, wso_ref[...])

    tile = pl.BlockSpec((tm, D), lambda i: (i, 0))
    wspec = pl.BlockSpec((D, D), lambda i: (0, 0))
    return pl.pallas_call(
        body, name="out_proj", grid=(n_i,),
        in_specs=[pl.BlockSpec((tm, 2 * D), lambda i: (i, 3)), tile, tile, tile, tile, _vec_spec(1), _vec_spec(1),
                  wspec, wspec, wspec],
        out_specs=[tile, tile, tile, pl.BlockSpec((tm, 2 * D), lambda i: (i, 3)), tile, tile, tile, tile,
                   pl.BlockSpec((SUB, D), lambda i: (0, 0))],
        out_shape=[_sds((s_len, D), F32), _sds((s_len, D), F32), _sds((s_len, D), F32), _sds((s_len, N_SEC * D), BF16),
                   _sds((s_len, D), BF16), _sds((s_len, D), BF16), _sds((s_len, D), BF16), _sds((s_len, D), BF16),
                   _sds((SUB, D), F32)],
        compiler_params=_params(("arbitrary",)),
    )(p, ya_in, yb_in, x, target, gate, g_final, w_co, w_so, w_o)


A_STATS_ROWS = 8 + HALO


def _branch_a_bwd(p, p_halo_src, y1, dya_in, dp, conv_wb, ln_g, ln_b):
    s_len = p.shape[0]
    tm = min(256, s_len)
    n_i = s_len // tm
    rows = 32
    halo_blocks = tm // HALO

    def tile_of(i):
        return n_i - 1 - i

    def body(p_ref, ph_ref, y1_ref, dya_ref, dp_in, wb_ref, g_ref, b_ref, dp_ref, st_ref, abuf, dybuf, acc8, tapacc):
        del dp_in
        i = pl.program_id(0)
        first_tile = tile_of(i) == 0

        @pl.when(i == 0)
        def _():
            dybuf[pl.ds(tm, HALO), :] = jnp.zeros((HALO, D), F32)
            st_ref[...] = jnp.zeros((A_STATS_ROWS, D), F32)
            acc8[...] = jnp.zeros((3 * PACK, D), F32)
            tapacc[...] = jnp.zeros((CONV_K * SUB, D), F32)

        def glu(r0):
            val = p_ref[pl.ds(r0, PACK), pl.ds(0, D)].astype(F32)
            gl = p_ref[pl.ds(r0, PACK), pl.ds(D, D)].astype(F32)
            abuf[pl.ds(HALO + r0, PACK), :] = val * _sigmoid(gl)
        _strips(tm, PACK,glu)

        def glu_halo(r0):
            val = ph_ref[pl.ds(r0, PACK), pl.ds(0, D)].astype(F32)
            gl = ph_ref[pl.ds(r0, PACK), pl.ds(D, D)].astype(F32)
            abuf[pl.ds(r0, PACK), :] = jnp.where(first_tile, 0.0, val * _sigmoid(gl))
        _strips(HALO, PACK,glu_halo)

        def norm_bwd(r0):
            y1 = y1_ref[pl.ds(r0, PACK), :]
            mu = _rowmean(y1)
            yc = y1 - mu
            rstd = lax.rsqrt(_rowmean(yc * yc) + EPS)
            n1 = yc * rstd
            l1 = n1 * g_ref[...] + b_ref[...]
            sg = _sigmoid(l1)
            z = p_ref[pl.ds(r0, PACK), pl.ds(2 * D, D)].astype(F32)
            sz = _sigmoid(z)
            dya = dya_ref[pl.ds(r0, PACK), :]
            dp_ref[pl.ds(r0, PACK), pl.ds(2 * D, D)] = (dya * (l1 * sg) * _dsilu(z, sz)).astype(BF16)
            dl1 = dya * (z * sz) * _dsilu(l1, sg)
            acc8[pl.ds(0, PACK), :] += dl1 * n1
            acc8[pl.ds(PACK, PACK), :] += dl1
            dn1 = dl1 * g_ref[...]
            dy1 = rstd * (dn1 - _rowmean(dn1) - n1 * _rowmean(dn1 * n1))
            acc8[pl.ds(2 * PACK, PACK), :] += dy1
            dybuf[pl.ds(r0, PACK), :] = dy1
        _strips(tm, PACK,norm_bwd)

        taps_d = {CONV_K - 1 - k: (lambda lanes, k=k: wb_ref[pl.ds(k * SUB, SUB), lanes]) for k in range(CONV_K)}

        def conv_bwd_data(r0):
            for lt in range(D // LANE):
                lanes = pl.ds(lt * LANE, LANE)
                acc = _conv_taps(dybuf, r0, lt, taps_d, rows)
                for m in range(0, len(acc), PACK // SUB):
                    da = jnp.concatenate(acc[m:m + PACK // SUB], axis=0)
                    rr = pl.ds(r0 + m * SUB, PACK)
                    val = p_ref[rr, pl.ds(lt * LANE, LANE)].astype(F32)
                    sg = _sigmoid(p_ref[rr, pl.ds(D + lt * LANE, LANE)].astype(F32))
                    dp_ref[rr, lanes] = (da * sg).astype(BF16)
                    dp_ref[rr, pl.ds(D + lt * LANE, LANE)] = (da * val * sg * (1.0 - sg)).astype(BF16)
        _strips(tm, rows, conv_bwd_data)

        n_out = rows // SUB

        def conv_bwd_w(r0):
            for lt in range(D // LANE):
                lanes = pl.ds(lt * LANE, LANE)
                win = abuf[pl.ds(r0, rows + HALO), lanes]
                dy = [dybuf[pl.ds(r0 + m * SUB, SUB), lanes] for m in range(n_out)]
                for phase in range(SUB):
                    ks = [k for k in range(CONV_K) if (HALO - (CONV_K - 1) + k) % SUB == phase]
                    q_max = max((HALO - (CONV_K - 1) + k) // SUB for k in ks)
                    sh = win[phase:phase + (n_out + q_max) * SUB, :]
                    for k in ks:
                        q = (HALO - (CONV_K - 1) + k) // SUB
                        part = dy[0] * sh[q * SUB:(q + 1) * SUB, :]
                        for m in range(1, n_out):
                            part = part + dy[m] * sh[(m + q) * SUB:(m + q + 1) * SUB, :]
                        tapacc[pl.ds(k * SUB, SUB), lanes] += part
        _strips(tm, rows, conv_bwd_w)

        dybuf[pl.ds(tm, HALO), :] = dybuf[pl.ds(0, HALO), :]

        @pl.when(i == n_i - 1)
        def _():
            for j in range(3):
                st_ref[pl.ds(j, 1), :] = jnp.sum(acc8[pl.ds(j * PACK, PACK), :], axis=0, keepdims=True)
            for k in range(CONV_K):
                st_ref[pl.ds(SUB + k, 1), :] = jnp.sum(tapacc[pl.ds(k * SUB, SUB), :], axis=0, keepdims=True)

    return pl.pallas_call(
        body, name="branch_a_bwd", grid=(n_i,),
        in_specs=[pl.BlockSpec((tm, 3 * D), lambda i: (tile_of(i), 0)),
                  pl.BlockSpec((HALO, 2 * D), lambda i: (jnp.maximum(tile_of(i) * halo_blocks - 1, 0), 0)),
                  pl.BlockSpec((tm, D), lambda i: (tile_of(i), 0)),
                  pl.BlockSpec((tm, D), lambda i: (tile_of(i), 0)),
                  pl.BlockSpec(memory_space=pl.ANY),
                  pl.BlockSpec((CONV_K * SUB, D), lambda i: (0, 0)), _vec_spec(1), _vec_spec(1)],
        out_specs=[pl.BlockSpec((tm, 3 * D), lambda i: (tile_of(i), 0)),
                   pl.BlockSpec((A_STATS_ROWS, D), lambda i: (0, 0))],
        out_shape=[_sds(dp.shape, BF16), _sds((A_STATS_ROWS, D), F32)],
        scratch_shapes=[pltpu.VMEM((tm + HALO, D), F32), pltpu.VMEM((tm + HALO, D), F32), pltpu.VMEM((3 * PACK, D), F32),
                        pltpu.VMEM((CONV_K * SUB, D), F32)],
        input_output_aliases={4: 0},
        compiler_params=_params(("arbitrary",)),
    )(p, p_halo_src, y1, dya_in, dp, conv_wb, ln_g, ln_b)


def _branch_b_bwd(p, dyb_in, dp, wt, wtt, bias_full, ln_g, ln_b):
    s_len = p.shape[0]
    tm = min(256, s_len)
    n_i = s_len // tm

    def body(p_ref, dyb_ref, dp_in, wt_ref, wtt_ref, bias_ref, g_ref, b_ref, dp_ref, st_ref, gbt_ref, gw_ref,
             vb, n2buf, rstdbuf, sbuf, dsb, dvbuf, acc8, gb_ref):
        del dp_in
        i = pl.program_id(0)

        @pl.when(i == 0)
        def _():
            st_ref[...] = jnp.zeros((SUB, D), F32)
            gbt_ref[...] = jnp.zeros((CHUNK, LANE), F32)
            gb_ref[...] = jnp.zeros((CHUNK, D), F32)
            gw_ref[...] = jnp.zeros((HEADS, CHUNK, CHUNK), F32)
            acc8[...] = jnp.zeros((2 * PACK, D), F32)

        def norm(r0):
            gv = _gelu(p_ref[pl.ds(r0, PACK), pl.ds(D, D)].astype(F32))
            mu = _rowmean(gv)
            vc = gv - mu
            rstd = lax.rsqrt(_rowmean(vc * vc) + EPS)
            n2 = vc * rstd
            n2buf[pl.ds(r0, PACK), :] = n2
            rstdbuf[pl.ds(r0, PACK), :] = jnp.broadcast_to(rstd, (PACK, LANE))
            vb[pl.ds(r0, PACK), :] = (n2 * g_ref[...] + b_ref[...]).astype(BF16)
        _strips(tm, PACK,norm)

        for ck in range(tm // CHUNK):
            for h in range(HEADS):
                blk = (pl.ds(ck * CHUNK, CHUNK), pl.ds(h * LANE, LANE))
                sbuf[blk] = jnp.dot(wt_ref[h], vb[blk], preferred_element_type=F32) + bias_ref[:, pl.ds(h * LANE, LANE)]

        def gate_bwd(r0):
            pu = p_ref[pl.ds(r0, PACK), pl.ds(0, D)].astype(F32)
            u = _gelu(pu)
            z = p_ref[pl.ds(r0, PACK), pl.ds(2 * D, D)].astype(F32)
            sg = _sigmoid(z)
            sz = z * sg
            s = sbuf[pl.ds(r0, PACK), :]
            dyb = dyb_ref[pl.ds(r0, PACK), :]
            ds = dyb * u * sz
            dsb[pl.ds(r0, PACK), :] = ds.astype(BF16)
            gb_ref[pl.ds(pl.multiple_of(r0 % CHUNK, PACK), PACK), :] += ds
            dp_ref[pl.ds(r0, PACK), pl.ds(0, D)] = (dyb * s * sz * _dgelu(pu)).astype(BF16)
            dp_ref[pl.ds(r0, PACK), pl.ds(2 * D, D)] = (dyb * u * s * _dsilu(z, sg)).astype(BF16)
        _strips(tm, PACK,gate_bwd)

        for ck in range(tm // CHUNK):
            for h in range(HEADS):
                blk = (pl.ds(ck * CHUNK, CHUNK), pl.ds(h * LANE, LANE))
                d_s = dsb[blk]
                dvbuf[blk] = jnp.dot(wtt_ref[h], d_s, preferred_element_type=F32)
                gw_ref[h] += _dot_t(d_s, vb[blk])

        def norm_bwd(r0):
            dv = dvbuf[pl.ds(r0, PACK), :]
            n2 = n2buf[pl.ds(r0, PACK), :]
            rstd = rstdbuf[pl.ds(r0, PACK), pl.ds(0, 1)]
            acc8[pl.ds(0, PACK), :] += dv * n2
            acc8[pl.ds(PACK, PACK), :] += dv
            dn2 = dv * g_ref[...]
            dgv = rstd * (dn2 - _rowmean(dn2) - n2 * _rowmean(dn2 * n2))
            dp_ref[pl.ds(r0, PACK), pl.ds(D, D)] = (dgv * _dgelu(p_ref[pl.ds(r0, PACK), pl.ds(D, D)].astype(F32))).astype(BF16)
        _strips(tm, PACK,norm_bwd)

        @pl.when(i == n_i - 1)
        def _():
            for j in range(2):
                st_ref[pl.ds(j, 1), :] = jnp.sum(acc8[pl.ds(j * PACK, PACK), :], axis=0, keepdims=True)
            row = lax.broadcasted_iota(jnp.int32, (CHUNK, CHUNK), 0)
            col = lax.broadcasted_iota(jnp.int32, (CHUNK, CHUNK), 1)
            for h in range(HEADS):
                gw_ref[h] = jnp.where(row >= col, gw_ref[h], 0.0)
            lane = lax.broadcasted_iota(jnp.int32, (CHUNK, LANE), 1)
            gbt = jnp.zeros((CHUNK, LANE), F32)
            for h in range(HEADS):
                gbt = jnp.where(lane == h, jnp.sum(gb_ref[:, pl.ds(h * LANE, LANE)], axis=1, keepdims=True), gbt)
            gbt_ref[...] = gbt

    wspec = pl.BlockSpec((HEADS, CHUNK, CHUNK), lambda i: (0, 0, 0))
    return pl.pallas_call(
        body, name="branch_b_bwd", grid=(n_i,),
        in_specs=[pl.BlockSpec((tm, 3 * D), lambda i: (i, 1)), pl.BlockSpec((tm, D), lambda i: (i, 0)),
                  pl.BlockSpec(memory_space=pl.ANY), wspec, wspec,
                  pl.BlockSpec((CHUNK, D), lambda i: (0, 0)), _vec_spec(1), _vec_spec(1)],
        out_specs=[pl.BlockSpec((tm, 3 * D), lambda i: (i, 1)), pl.BlockSpec((SUB, D), lambda i: (0, 0)),
                   pl.BlockSpec((CHUNK, LANE), lambda i: (0, 0)), wspec],
        out_shape=[_sds(dp.shape, BF16), _sds((SUB, D), F32), _sds((CHUNK, LANE), F32), _sds((HEADS, CHUNK, CHUNK), F32)],
        scratch_shapes=[pltpu.VMEM((tm, D), BF16), pltpu.VMEM((tm, D), F32), pltpu.VMEM((tm, LANE), F32),
                        pltpu.VMEM((tm, D), F32), pltpu.VMEM((tm, D), BF16), pltpu.VMEM((tm, D), F32),
                        pltpu.VMEM((2 * PACK, D), F32), pltpu.VMEM((CHUNK, D), F32)],
        input_output_aliases={2: 0},
        compiler_params=_params(("arbitrary",)),
    )(p, dyb_in, dp, wt, wtt, bias_full, ln_g, ln_b)


def _in_proj_bwd(dp, wg_in, x, dx2, shift, scale, g_pre):
    del shift
    s_len = x.shape[0]
    tm = min(256, s_len)
    n_i = s_len // tm
    wn = wg_in.shape[2]

    def body(dp_ref, w_ref, x_ref, dx2_ref, sc_ref, g_ref, gx_ref, st_ref, acc, acc8):
        i = pl.program_id(0)

        @pl.when(i == 0)
        def _():
            st_ref[...] = jnp.zeros((SUB, D), F32)
            acc8[...] = jnp.zeros((3 * PACK, D), F32)

        dh = _dot_t(dp_ref[:, pl.ds(0, wn)], w_ref[0])
        for j in range(1, N_CHIP):
            dh = dh + _dot_t(dp_ref[:, pl.ds(j * wn, wn)], w_ref[j])
        acc[...] = dh

        def strip(r0):
            xs = x_ref[pl.ds(r0, PACK), :]
            r = lax.rsqrt(_rowmean(xs * xs) + EPS)
            xn = xs * r
            dhs = acc[pl.ds(r0, PACK), :]
            acc8[pl.ds(0, PACK), :] += dhs
            acc8[pl.ds(PACK, PACK), :] += dhs * (xn * g_ref[...])
            dhp = dhs * (1.0 + sc_ref[...])
            acc8[pl.ds(2 * PACK, PACK), :] += dhp * xn
            dxn = dhp * g_ref[...]
            gx_ref[pl.ds(r0, PACK), :] = dx2_ref[pl.ds(r0, PACK), :] + r * (dxn - xn * _rowmean(dxn * xn))
        _strips(tm, PACK, strip)

        @pl.when(i == n_i - 1)
        def _():
            for k in range(3):
                st_ref[pl.ds(k, 1), :] = jnp.sum(acc8[pl.ds(k * PACK, PACK), :], axis=0, keepdims=True)

    tile = pl.BlockSpec((tm, D), lambda i: (i, 0))
    return pl.pallas_call(
        body, name="in_proj_bwd", grid=(n_i,),
        in_specs=[pl.BlockSpec((tm, N_CHIP * wn), lambda i: (i, 0)),
                  pl.BlockSpec((N_CHIP, D, wn), lambda i: (0, 0, 0), pipeline_mode=pl.Buffered(1)),
                  tile, tile, _vec_spec(1), _vec_spec(1)],
        out_specs=[tile, pl.BlockSpec((SUB, D), lambda i: (0, 0))],
        out_shape=[_sds((s_len, D), F32), _sds((SUB, D), F32)],
        scratch_shapes=[pltpu.VMEM((tm, D), F32), pltpu.VMEM((3 * PACK, D), F32)],
        compiler_params=_params(("arbitrary",)),
    )(dp, wg_in, x, dx2, scale, g_pre)


def _grad_matmul(a, b, name):
    s_len, n = b.shape
    cb = min(2 * D, n)
    tn = 512
    per = cb // tn

    def body(a_ref, b_ref, ob_ref):
        ob_ref[0] = lax.dot_general(a_ref[...], b_ref[...], (((0,), (0,)), ((), ())),
                                    preferred_element_type=F32).astype(BF16)

    return pl.pallas_call(
        body, name=name, grid=(n // tn,),
        in_specs=[pl.BlockSpec((s_len, D), lambda j: (0, 0), pipeline_mode=pl.Buffered(1)),
                  pl.BlockSpec((s_len, tn), lambda j: (0, j))],
        out_specs=pl.BlockSpec((1, D, tn), lambda j: (j // per, 0, j % per)),
        out_shape=_sds((n // cb, D, cb), BF16),
        compiler_params=_params(("arbitrary",)),
    )(a, b)


def _local_step(x, target, shift, scale, gate, g_pre, conv_w_full, conv_b, conv_ln_g, conv_ln_b,
                sgu_ln_g, sgu_ln_b, w_sgu, b_sgu, g_final, wg_in, w_co, w_so, w_o):
    conv_wb = jnp.repeat(conv_w_full, SUB, axis=0)
    causal = jnp.tril(jnp.ones((CHUNK, CHUNK), dtype=bool))
    wt = jnp.where(causal[None], w_sgu, 0.0).astype(BF16)
    wtt = jnp.swapaxes(wt, 1, 2)
    bias_full = jnp.repeat(b_sgu.T, LANE, axis=1)

    p, hb = _in_proj(x, shift, scale, g_pre, wg_in)
    ya_in, y1 = _branch_a_fwd(p, conv_wb, conv_b, conv_ln_g, conv_ln_b)
    yb_in = _branch_b_fwd(p, wt, bias_full, sgu_ln_g, sgu_ln_b)
    dx2, dya_in, dyb_in, dp, mb, dob, dyab, dybb, sums_o = _out_proj(
        p, ya_in, yb_in, x, target, gate, g_final, w_co, w_so, w_o)
    dp, st_a = _branch_a_bwd(p, p, y1, dya_in, dp, conv_wb, conv_ln_g, conv_ln_b)
    dp, st_b, gbt, gws = _branch_b_bwd(p, dyb_in, dp, wt, wtt, bias_full, sgu_ln_g, sgu_ln_b)
    grad_x, st_i = _in_proj_bwd(dp, wg_in, x, dx2, shift, scale, g_pre)
    gw_in = _grad_matmul(hb, dp, "grad_w_in")
    gw_o = _grad_matmul(mb, dob, "grad_w_o")
    gw_co = _grad_matmul(ya_in, dyab, "grad_w_conv_out")
    gw_so = _grad_matmul(yb_in, dybb, "grad_w_sgu_out")
    return dict(
        grad_x=grad_x, loss_cols=sums_o[2:3], g_final=sums_o[0:1], d_gate=sums_o[1:2],
        d_shift=st_i[0:1], d_scale=st_i[1:2], g_pre=st_i[2:3],
        conv_ln_g=st_a[0:1], conv_ln_b=st_a[1:2], conv_b=st_a[2:3], conv_w=st_a[SUB:SUB + CONV_K],
        sgu_ln_g=st_b[0:1], sgu_ln_b=st_b[1:2], b_sgu=gbt[:, :HEADS].T, w_sgu=gws,
        w_in=gw_in, w_o=gw_o, w_conv_out=gw_co, w_sgu_out=gw_so)


ANY_SPEC = pl.BlockSpec(memory_space=pl.ANY)
VMEM_SPEC = pl.BlockSpec(memory_space=pltpu.VMEM)


def _place():
    return lax.axis_index("x"), lax.axis_index("y"), lax.axis_index("c")


def _peer(k):
    x, y, c = _place()
    return (1 - x if k & 4 else x, 1 - y if k & 2 else y, 1 - c if k & 1 else c)


def _dev_of(p):
    return 4 * p[0] + 2 * p[1] + p[2]


def _chip_of(p):
    return 2 * p[0] + p[1]


def _rdma(src, dst, send_sem, recv_sem, to):
    return pltpu.make_async_remote_copy(src_ref=src, dst_ref=dst, send_sem=send_sem, recv_sem=recv_sem,
                                        device_id=to, device_id_type=MESH)


CHIP_PEERS = (2, 4, 6)
ALL_PEERS = tuple(range(1, N_DEV))
SIBLING = 1


def _setup_comm(c8, w_ada_s, b_ada_s, convw_s):
    n_mod = w_ada_s.shape[1]
    rows = SUB * N_DEV

    def body(c8_ref, wada_ref, bada_ref, cw_ref, call_ref, mod_ref, cwall_ref, csend, crecv, wsend, wrecv, msend, mrecv):
        me = _place()
        dev, chip = _dev_of(me), _chip_of(me)

        def c_rows(d):
            return call_ref.at[pl.ds(pl.multiple_of(d * SUB, SUB), SUB), :]

        call_ref[pl.ds(pl.multiple_of(dev * SUB, SUB), SUB), :] = c8_ref[...]
        cwall_ref[chip] = cw_ref[...]
        c_out = [_rdma(c8_ref, c_rows(dev), csend.at[k], crecv.at[k], _peer(k)) for k in ALL_PEERS]
        w_out = [_rdma(cw_ref, cwall_ref.at[chip], wsend.at[k], wrecv.at[k], _peer(k)) for k in CHIP_PEERS]
        for cp in c_out + w_out:
            cp.start()
        for k in ALL_PEERS:
            _rdma(c8_ref, c_rows(_dev_of(_peer(k))), csend.at[k], crecv.at[k], _peer(k)).wait_recv()
        part = jnp.dot(call_ref[...].astype(BF16), wada_ref[...].astype(BF16), preferred_element_type=F32) + bada_ref[...]
        mod_ref[chip] = part
        m_out = [_rdma(mod_ref.at[chip], mod_ref.at[chip], msend.at[k], mrecv.at[k], _peer(k)) for k in CHIP_PEERS]
        for cp in m_out:
            cp.start()
        for k in CHIP_PEERS:
            pc = _chip_of(_peer(k))
            _rdma(cw_ref, cwall_ref.at[pc], wsend.at[k], wrecv.at[k], _peer(k)).wait_recv()
            _rdma(mod_ref.at[pc], mod_ref.at[pc], msend.at[k], mrecv.at[k], _peer(k)).wait_recv()
        for cp in c_out + w_out + m_out:
            cp.wait_send()

    return pl.pallas_call(
        body, name="setup_comm",
        in_specs=[VMEM_SPEC] * 4, out_specs=[VMEM_SPEC] * 3,
        out_shape=[_sds((rows, D), F32), _sds((N_CHIP, rows, n_mod), F32), _sds((N_CHIP,) + convw_s.shape, F32)],
        scratch_shapes=[pltpu.SemaphoreType.DMA((N_DEV,))] * 6,
        compiler_params=_params(),
    )(c8, w_ada_s, b_ada_s, convw_s)


def _gather_weights(shards):
    n = len(shards)

    def body(*refs):
        ins, outs = refs[:n], refs[n:2 * n]
        lsem, isend, irecv, dsend, drecv = refs[2 * n:]
        me = _place()
        chip, c = _chip_of(me), me[2]
        local = [pltpu.make_async_copy(ins[t], outs[t].at[chip], lsem.at[t]) for t in range(n)]
        for cp in local:
            cp.start()

        def half(t, which):
            hr = shards[t].shape[0] // 2
            return pl.ds(pl.multiple_of(which * hr, hr), hr)

        sends = []
        for t in range(n):
            for j, k in enumerate(CHIP_PEERS):
                cp = _rdma(ins[t].at[half(t, c)], outs[t].at[chip, half(t, c)], isend.at[t, j], irecv.at[t, j], _peer(k))
                cp.start()
                sends.append(cp)
        for t in range(n):
            for j, k in enumerate(CHIP_PEERS):
                blk = outs[t].at[_chip_of(_peer(k)), half(t, c)]
                _rdma(blk, blk, isend.at[t, j], irecv.at[t, j], _peer(k)).wait_recv()
                cp = _rdma(blk, blk, dsend.at[t, j], drecv.at[t, j], _peer(SIBLING))
                cp.start()
                sends.append(cp)
        for t in range(n):
            for j, k in enumerate(CHIP_PEERS):
                blk = outs[t].at[_chip_of(_peer(k)), half(t, 1 - c)]
                _rdma(blk, blk, dsend.at[t, j], drecv.at[t, j], _peer(SIBLING)).wait_recv()
        for cp in sends:
            cp.wait_send()
        for cp in local:
            cp.wait()

    return pl.pallas_call(
        body, name="gather_weights",
        in_specs=[VMEM_SPEC] * n, out_specs=[VMEM_SPEC] * n,
        out_shape=[_sds((N_CHIP,) + s.shape, s.dtype) for s in shards],
        scratch_shapes=[pltpu.SemaphoreType.DMA((n,))] + [pltpu.SemaphoreType.DMA((n, len(CHIP_PEERS)))] * 4,
        compiler_params=_params(),
    )(*shards)


def _reduce_scatter(grads, name):
    n = len(grads)
    shapes = [g.shape[2:] for g in grads]

    def body(*refs):
        ins, outs = refs[:n], refs[n:2 * n]
        pbufs, rbufs, accs = refs[2 * n:3 * n], refs[3 * n:4 * n], refs[4 * n:5 * n]
        psend, precv, csend, crecv, fsend, frecv = refs[5 * n:]
        me = _place()
        chip, c = _chip_of(me), me[2]
        sib = _peer(SIBLING)

        def to_sibling(t, d):
            return _rdma(ins[t].at[d, 1 - c], pbufs[t].at[d], psend.at[t, d], precv.at[t, d], sib)

        sends = []
        for t in range(n):
            for d in range(N_CHIP):
                cp = to_sibling(t, d)
                cp.start()
                sends.append(cp)
        for j in (1, 2, 3, 0):
            d = jnp.bitwise_xor(chip, j)
            for t in range(n):
                to_sibling(t, d).wait_recv()

                def pair_sum(r0, t=t, d=d, j=j):
                    rows = pl.ds(r0, PACK)
                    s = ins[t][d, c, rows, :].astype(F32) + pbufs[t][d, rows, :].astype(F32)
                    if j == 0:
                        accs[t][rows, :] = s
                    else:
                        pbufs[t][d, rows, :] = s.astype(BF16)
                _strips(shapes[t][0], PACK, pair_sum)
                if j:
                    cp = _rdma(pbufs[t].at[d], rbufs[t].at[j - 1], csend.at[t, j], crecv.at[t, j], _peer(2 * j))
                    cp.start()
                    sends.append(cp)
        for t in range(n):
            for j in (1, 2, 3):
                blk = rbufs[t].at[j - 1]
                _rdma(blk, blk, csend.at[t, j], crecv.at[t, j], _peer(2 * j)).wait_recv()

            def total(r0, t=t):
                rows = pl.ds(r0, PACK)
                s = accs[t][rows, :] + rbufs[t][0, rows, :].astype(F32)
                s = s + rbufs[t][1, rows, :].astype(F32)
                outs[t][c, rows, :] = s + rbufs[t][2, rows, :].astype(F32)
            _strips(shapes[t][0], PACK, total)
            cp = _rdma(outs[t].at[c], outs[t].at[c], fsend.at[t], frecv.at[t], sib)
            cp.start()
            sends.append(cp)
        for t in range(n):
            blk = outs[t].at[1 - c]
            _rdma(blk, blk, fsend.at[t], frecv.at[t], sib).wait_recv()
        for cp in sends:
            cp.wait_send()

    return pl.pallas_call(
        body, name=name,
        in_specs=[VMEM_SPEC] * n, out_specs=[VMEM_SPEC] * n,
        out_shape=[_sds((2,) + s, F32) for s in shapes],
        scratch_shapes=([pltpu.VMEM((N_CHIP,) + s, BF16) for s in shapes] + [pltpu.VMEM((N_CHIP - 1,) + s, BF16) for s in shapes]
                        + [pltpu.VMEM(s, F32) for s in shapes]
                        + [pltpu.SemaphoreType.DMA((n, N_CHIP))] * 4 + [pltpu.SemaphoreType.DMA((n,))] * 2),
        compiler_params=_params(),
    )(*grads)


def _scatter_grads(grads):
    n = len(grads)

    def body(*refs):
        ins, outs = refs[:n], refs[n:2 * n]
        lsem, send, recv = refs[2 * n:]
        me = _place()
        dev, chip, c = _dev_of(me), _chip_of(me), me[2]
        local = [pltpu.make_async_copy(ins[t].at[chip, c], outs[t].at[dev], lsem.at[t]) for t in range(n)]
        for cp in local:
            cp.start()
        sends = []
        for t in range(n):
            for k in ALL_PEERS:
                to = _peer(k)
                cp = _rdma(ins[t].at[_chip_of(to), to[2]], outs[t].at[dev], send.at[t, k], recv.at[t, k], to)
                cp.start()
                sends.append(cp)
        for t in range(n):
            for k in ALL_PEERS:
                blk = outs[t].at[_dev_of(_peer(k))]
                _rdma(blk, blk, send.at[t, k], recv.at[t, k], _peer(k)).wait_recv()
        for cp in sends:
            cp.wait_send()
        for cp in local:
            cp.wait()

    return pl.pallas_call(
        body, name="scatter_grads",
        in_specs=[ANY_SPEC] * n, out_specs=[ANY_SPEC] * n,
        out_shape=[_sds((N_DEV,) + g.shape[2:], g.dtype) for g in grads],
        scratch_shapes=[pltpu.SemaphoreType.DMA((n,))] + [pltpu.SemaphoreType.DMA((n, N_DEV))] * 2,
        compiler_params=_params(),
    )(*grads)


def _sum_devices(parts, name):
    _, r, cols = parts.shape
    tr = min(r, 128)

    def body(in_ref, o_ref):
        acc = in_ref[0].astype(F32)
        for d in range(1, N_DEV):
            acc = acc + in_ref[d].astype(F32)
        o_ref[...] = acc

    return pl.pallas_call(
        body, name=name, grid=(r // tr,),
        in_specs=[pl.BlockSpec((N_DEV, tr, cols), lambda i: (0, i, 0))],
        out_specs=pl.BlockSpec((tr, cols), lambda i: (i, 0)),
        out_shape=_sds((r, cols), F32),
        compiler_params=_params(("arbitrary",)),
    )(parts)


def _share_halves(reds):
    n = len(reds)

    def body(*refs):
        ins, outs = refs[:n], refs[n:2 * n]
        lsem, send, recv = refs[2 * n:]
        me = _place()
        c = me[2]
        local = [pltpu.make_async_copy(ins[t], outs[t].at[c], lsem.at[t]) for t in range(n)]
        sends = [_rdma(ins[t], outs[t].at[c], send.at[t], recv.at[t], _peer(SIBLING)) for t in range(n)]
        for cp in local + sends:
            cp.start()
        for t in range(n):
            _rdma(ins[t], outs[t].at[1 - c], send.at[t], recv.at[t], _peer(SIBLING)).wait_recv()
        for cp in sends:
            cp.wait_send()
        for cp in local:
            cp.wait()

    return pl.pallas_call(
        body, name="share_halves",
        in_specs=[VMEM_SPEC] * n, out_specs=[VMEM_SPEC] * n,
        out_shape=[_sds((2,) + r.shape, r.dtype) for r in reds],
        scratch_shapes=[pltpu.SemaphoreType.DMA((n,))] * 3,
        compiler_params=_params(),
    )(*reds)


def _sum_small(blob):
    rows = blob.shape[0]

    def body(b_ref, o_ref, pbuf, buf4, psend, precv, send, recv):
        me = _place()
        chip = _chip_of(me)
        pair = _rdma(b_ref, pbuf, psend, precv, _peer(SIBLING))
        pair.start()
        pair.wait()
        buf4[chip] = b_ref[...] + pbuf[...]
        out = [_rdma(buf4.at[chip], buf4.at[chip], send.at[k], recv.at[k], _peer(k)) for k in CHIP_PEERS]
        for cp in out:
            cp.start()
        for k in CHIP_PEERS:
            blk = buf4.at[_chip_of(_peer(k))]
            _rdma(blk, blk, send.at[k], recv.at[k], _peer(k)).wait_recv()
        o_ref[...] = (buf4[0] + buf4[1]) + (buf4[2] + buf4[3])
        for cp in out:
            cp.wait_send()

    return pl.pallas_call(
        body, name="sum_small",
        in_specs=[VMEM_SPEC], out_specs=VMEM_SPEC, out_shape=_sds(blob.shape, F32),
        scratch_shapes=[pltpu.VMEM((rows, D), F32), pltpu.VMEM((N_CHIP, rows, D), F32),
                        pltpu.SemaphoreType.DMA, pltpu.SemaphoreType.DMA,
                        pltpu.SemaphoreType.DMA((N_DEV,)), pltpu.SemaphoreType.DMA((N_DEV,))],
        compiler_params=_params(),
    )(blob)


def _adamw_math(w, g, m, v):
    m = ADAM_B1 * m + (1.0 - ADAM_B1) * g
    v = ADAM_B2 * v + (1.0 - ADAM_B2) * (g * g)
    m_hat = m / (1.0 - ADAM_B1 ** ADAM_STEP)
    v_hat = v / (1.0 - ADAM_B2 ** ADAM_STEP)
    delta = -ADAM_LR * (m_hat / (jnp.sqrt(v_hat) + ADAM_EPS) + ADAM_WD * w)
    return delta, m, v


def _row_tile(r, cols):
    if r * cols * 4 <= 2 ** 20:
        return r
    return next(t for t in (512, 256, 128, 64, 32, 16, 8) if r % t == 0 and t * cols * 4 <= 2 ** 20)


def _adamw(w, g, m, v, name):
    r, cols = w.shape
    tr = _row_tile(r, cols)

    def body(w_ref, g_ref, m_ref, v_ref, d_ref, nm_ref, nv_ref):
        d_ref[...], nm_ref[...], nv_ref[...] = _adamw_math(w_ref[...], g_ref[...], m_ref[...], v_ref[...])

    spec = pl.BlockSpec((tr, cols), lambda i: (i, 0))
    return pl.pallas_call(
        body, name=name, grid=(r // tr,), in_specs=[spec] * 4, out_specs=[spec] * 3,
        out_shape=[_sds((r, cols), F32)] * 3, compiler_params=_params(("arbitrary",)),
    )(w, g, m, v)


def _adamw_ada(w, ct, dm, m, v):
    r, cols = w.shape
    tr = _row_tile(r, cols)

    def body(w_ref, ct_ref, dm_ref, m_ref, v_ref, g_ref, d_ref, nm_ref, nv_ref):
        g = jnp.dot(ct_ref[...], dm_ref[...], preferred_element_type=F32)
        g_ref[...] = g
        d_ref[...], nm_ref[...], nv_ref[...] = _adamw_math(w_ref[...], g, m_ref[...], v_ref[...])

    spec = pl.BlockSpec((tr, cols), lambda i: (i, 0))
    return pl.pallas_call(
        body, name="adamw_ada", grid=(r // tr,),
        in_specs=[spec, pl.BlockSpec((tr, LANE), lambda i: (i, 0)), pl.BlockSpec((LANE, cols), lambda i: (0, 0)), spec, spec],
        out_specs=[spec] * 4, out_shape=[_sds((r, cols), F32)] * 4, compiler_params=_params(("arbitrary",)),
    )(w, ct, dm, m, v)


BLOB_VEC, BLOB_CONV, BLOB_SGU, BLOB_ADA, BLOB_DMOD, BLOB_LOSS, BLOB_ROWS = 0, 8, 40, 168, 176, 200, 208
SMALL_CONV, SMALL_SGU, SMALL_ADA, SMALL_ROWS = 8, 16, 144, 152


def _set_rows(buf, row, val):
    return lax.dynamic_update_slice(buf, val.astype(F32), (row, 0))


def _pack_small(vecs, b_sgu, conv_w_s, w_sgu, b_ada):
    buf = jnp.zeros((SMALL_ROWS, D), F32)
    for i, vec in enumerate(vecs):
        buf = _set_rows(buf, i, vec.reshape(1, D))
    buf = _set_rows(buf, 7, b_sgu.reshape(1, D))
    buf = _set_rows(buf, SMALL_CONV, jnp.pad(conv_w_s, ((0, 1), (0, 0))).reshape(SUB, D))
    buf = _set_rows(buf, SMALL_SGU, w_sgu.reshape(CHUNK, D))
    return _set_rows(buf, SMALL_ADA, b_ada.reshape(3, D))


def _unpack_small(buf, conv_cols):
    vecs = [buf[i] for i in range(7)]
    b_sgu = buf[7].reshape(HEADS, CHUNK)
    conv_w_s = buf[SMALL_CONV:SMALL_CONV + SUB].reshape(HALO, conv_cols)[:CONV_K]
    w_sgu = buf[SMALL_SGU:SMALL_SGU + CHUNK].reshape(HEADS, CHUNK, CHUNK)
    b_ada = buf[SMALL_ADA:SMALL_ADA + 3].reshape(1, 3 * D)
    return vecs, b_sgu, conv_w_s, w_sgu, b_ada


def kernel(x, c, w_ada, b_ada, g_pre, w_in, conv_w, conv_b, conv_ln_g, conv_ln_b, w_conv_out, sgu_ln_g, sgu_ln_b, w_sgu, b_sgu, w_sgu_out, w_o, g_final, loss_target, m_w_ada, m_b_ada, m_g_pre, m_w_in, m_conv_w, m_conv_b, m_conv_ln_g, m_conv_ln_b, m_w_conv_out, m_sgu_ln_g, m_sgu_ln_b, m_w_sgu, m_b_sgu, m_w_sgu_out, m_w_o, m_g_final, v_w_ada, v_b_ada, v_g_pre, v_w_in, v_conv_w, v_conv_b, v_conv_ln_g, v_conv_ln_b, v_w_conv_out, v_sgu_ln_g, v_sgu_ln_b, v_w_sgu, v_b_sgu, v_w_sgu_out, v_w_o, v_g_final):
    me = _place()
    dev, chip = _dev_of(me), _chip_of(me)
    n_ada = w_ada.shape[2]
    conv_cols = conv_w.shape[2]

    b_ada_s = lax.dynamic_slice(b_ada, (0, chip * n_ada), (1, n_ada))
    c_all, mod_all, cw_all = _setup_comm(
        jnp.broadcast_to(c, (SUB, D)), w_ada[0], b_ada_s, jnp.pad(conv_w[0], ((0, HALO - CONV_K), (0, 0))))
    mod = lax.dynamic_slice(mod_all, (0, dev * SUB, 0), (N_CHIP, 1, n_ada)).reshape(1, 3 * D)
    shift, scale, gate = mod[:, :D], mod[:, D:2 * D], mod[:, 2 * D:]
    conv_w_full = jnp.swapaxes(cw_all, 0, 1).reshape(HALO, D)[:CONV_K]

    wg_in, wg_co, wg_so, wg_o = _gather_weights(
        [w_in[0].astype(BF16), w_conv_out[0].astype(BF16), w_sgu_out[0].astype(BF16), w_o[0].astype(BF16)])

    loc = _local_step(x[0], loss_target[0], shift, scale, gate, g_pre, conv_w_full, conv_b, conv_ln_g, conv_ln_b,
                      sgu_ln_g, sgu_ln_b, w_sgu[0], b_sgu[0], g_final.reshape(1, D),
                      wg_in, wg_co.reshape(D, D), wg_so.reshape(D, D), wg_o.reshape(D, D))

    big = ["w_in", "w_conv_out", "w_sgu_out", "w_o"]
    contrib = []
    for name in big:
        g16 = loc[name]
        rows_half = (g16.shape[0] * g16.shape[1]) // (2 * N_CHIP) if name != "w_in" else g16.shape[1] // 2
        contrib.append(g16.reshape(N_CHIP, 2, rows_half, g16.shape[2]))
    full = _reduce_scatter(contrib[:1], "reduce_w_in") + _reduce_scatter(contrib[1:], "reduce_w_out")
    g_big = {name: f.reshape(2 * f.shape[1], f.shape[2]) for name, f in zip(big, full)}

    d_mod = jnp.concatenate([loc["d_shift"], loc["d_scale"], loc["d_gate"]], axis=0)
    blob = jnp.zeros((BLOB_ROWS, D), F32)
    for i, name in enumerate(["g_pre", "conv_b", "conv_ln_g", "conv_ln_b", "sgu_ln_g", "sgu_ln_b", "g_final"]):
        blob = _set_rows(blob, BLOB_VEC + i, loc[name])
    blob = _set_rows(blob, BLOB_VEC + 7, loc["b_sgu"].reshape(1, D))
    blob = _set_rows(blob, BLOB_CONV, loc["conv_w"])
    blob = _set_rows(blob, BLOB_SGU, loc["w_sgu"].reshape(CHUNK, D))
    blob = _set_rows(blob, BLOB_ADA, d_mod)
    blob = lax.dynamic_update_slice(blob, d_mod, (BLOB_DMOD + 3 * dev, 0))
    blob = _set_rows(blob, BLOB_LOSS, loc["loss_cols"])
    tot = _sum_small(blob)

    loss = jnp.sum(tot[BLOB_LOSS])
    g_vecs = [tot[BLOB_VEC + i] for i in range(7)]
    g_b_sgu = tot[BLOB_VEC + 7].reshape(HEADS, CHUNK)
    g_conv_s = lax.dynamic_slice(tot, (BLOB_CONV, chip * conv_cols), (CONV_K, conv_cols))
    g_w_sgu = tot[BLOB_SGU:BLOB_SGU + CHUNK].reshape(HEADS, CHUNK, CHUNK)
    g_b_ada = tot[BLOB_ADA:BLOB_ADA + 3].reshape(1, 3 * D)
    d_mod_all = tot[BLOB_DMOD:BLOB_DMOD + 3 * N_DEV].reshape(N_DEV, 3 * D)

    ct = jnp.pad(c_all[::SUB].T, ((0, 0), (0, LANE - N_DEV))).astype(BF16)
    dm = jnp.pad(lax.dynamic_slice(d_mod_all, (0, chip * n_ada), (N_DEV, n_ada)), ((0, LANE - N_DEV), (0, 0))).astype(BF16)
    g_ada, d_ada, nm_ada, nv_ada = _adamw_ada(w_ada[0], ct, dm, m_w_ada[0], v_w_ada[0])

    upd = {}
    for name, w, m, v in [("w_in", w_in, m_w_in, v_w_in), ("w_conv_out", w_conv_out, m_w_conv_out, v_w_conv_out),
                          ("w_sgu_out", w_sgu_out, m_w_sgu_out, v_w_sgu_out), ("w_o", w_o, m_w_o, v_w_o)]:
        upd[name] = _adamw(w[0], g_big[name], m[0], v[0], "adamw_" + name)

    def small(g_p, conv_p, sgu_p, ada_p, vec_ps):
        return _pack_small(vec_ps, g_p, conv_p, sgu_p, ada_p)

    w_s = small(b_sgu[0], conv_w[0], w_sgu[0], b_ada, [g_pre, conv_b, conv_ln_g, conv_ln_b, sgu_ln_g, sgu_ln_b, g_final])
    m_s = small(m_b_sgu[0], m_conv_w[0], m_w_sgu[0], m_b_ada,
                [m_g_pre, m_conv_b, m_conv_ln_g, m_conv_ln_b, m_sgu_ln_g, m_sgu_ln_b, m_g_final])
    v_s = small(v_b_sgu[0], v_conv_w[0], v_w_sgu[0], v_b_ada,
                [v_g_pre, v_conv_b, v_conv_ln_g, v_conv_ln_b, v_sgu_ln_g, v_sgu_ln_b, v_g_final])
    g_s = small(g_b_sgu, g_conv_s, g_w_sgu, g_b_ada, g_vecs)
    small_out = [_unpack_small(a, conv_cols) for a in (g_s,) + tuple(_adamw(w_s, g_s, m_s, v_s, "adamw_small"))]

    def leaves(kind):
        vecs, o_b_sgu, o_conv, o_w_sgu, o_b_ada = small_out[kind]
        ada = (g_ada, d_ada, nm_ada, nv_ada)[kind]
        def bigk(name):
            return (g_big[name] if kind == 0 else upd[name][kind - 1])[None]
        return [ada[None], o_b_ada, vecs[0][None], bigk("w_in"), o_conv[None], vecs[1][None], vecs[2][None], vecs[3][None],
                bigk("w_conv_out"), vecs[4][None], vecs[5][None], o_w_sgu[None], o_b_sgu[None], bigk("w_sgu_out"),
                bigk("w_o"), vecs[6]]

    return (loss, loc["grad_x"][None], *leaves(0), *leaves(1), *leaves(2), *leaves(3))
```

```python
import functools

import jax
import jax.numpy as jnp
from jax import lax
from jax.experimental import pallas as pl
from jax.experimental.pallas import tpu as pltpu

F32 = jnp.float32
BF16 = jnp.bfloat16
MESH = pl.DeviceIdType.MESH

D = 1024
N_SEC = 8
N_CHIP = 4
N_DEV = 8
EPS = 1e-6
CONV_K = 31
HALO = 32
CHUNK = 128
HEADS = 8
LANE = 128
SUB = 8
PACK = 16
VMEM_LIMIT = 56 * 1024 * 1024

ADAM_LR, ADAM_B1, ADAM_B2, ADAM_EPS, ADAM_WD, ADAM_STEP = 0.001, 0.9, 0.999, 1e-08, 0.01, 10

_SQRT_HALF = 0.7071067811865476
_INV_SQRT_2PI = 0.3989422804014327


def _sds(shape, dtype):
    return jax.ShapeDtypeStruct(shape, dtype)


def _params(sem=None):
    if sem is None:
        return pltpu.CompilerParams(vmem_limit_bytes=VMEM_LIMIT)
    return pltpu.CompilerParams(dimension_semantics=sem, vmem_limit_bytes=VMEM_LIMIT)


def _strips(n_rows, rows, fn):
    def step(s, carry):
        fn(pl.multiple_of(s * rows, rows))
        return carry
    lax.fori_loop(0, n_rows // rows, step, 0)


def _sigmoid(v):
    return 1.0 / (1.0 + jnp.exp(-v))


def _gelu(v):
    return 0.5 * v * (1.0 + lax.erf(v * _SQRT_HALF))


def _dgelu(v):
    return 0.5 * (1.0 + lax.erf(v * _SQRT_HALF)) + v * jnp.exp(-0.5 * v * v) * _INV_SQRT_2PI


def _dsilu(v, sg):
    return sg * (1.0 + v * (1.0 - sg))


def _rowmean(v):
    return jnp.mean(v, axis=-1, keepdims=True)


def _vec_spec(grid_rank):
    zeros = (0, 0)
    if grid_rank == 1:
        return pl.BlockSpec((1, D), lambda i: zeros)
    return pl.BlockSpec((1, D), lambda i, j: zeros)


def _in_proj(x, shift, scale, g_pre, wg_in):
    s_len = x.shape[0]
    tm = min(512, s_len)
    n_i = s_len // tm
    wn = wg_in.shape[2]

    def body(x_ref, sh_ref, sc_ref, g_ref, w_ref, p_ref, hb_ref):
        def strip(r0):
            xs = x_ref[pl.ds(r0, PACK), :]
            r = lax.rsqrt(_rowmean(xs * xs) + EPS)
            h = (xs * r) * g_ref[...] * (1.0 + sc_ref[...]) + sh_ref[...]
            hb_ref[pl.ds(r0, PACK), :] = h.astype(BF16)
        _strips(tm, PACK, strip)
        hb = hb_ref[...]
        for j in range(N_CHIP):
            p_ref[:, pl.ds(j * wn, wn)] = jnp.dot(hb, w_ref[j], preferred_element_type=F32).astype(BF16)

    return pl.pallas_call(
        body, name="in_proj", grid=(n_i,),
        in_specs=[pl.BlockSpec((tm, D), lambda i: (i, 0)), _vec_spec(1), _vec_spec(1), _vec_spec(1),
                  pl.BlockSpec((N_CHIP, D, wn), lambda i: (0, 0, 0), pipeline_mode=pl.Buffered(1))],
        out_specs=[pl.BlockSpec((tm, N_CHIP * wn), lambda i: (i, 0)), pl.BlockSpec((tm, D), lambda i: (i, 0))],
        out_shape=[_sds((s_len, N_SEC * D), BF16), _sds((s_len, D), BF16)],
        compiler_params=_params(("arbitrary",)),
    )(x, shift, scale, g_pre, wg_in)


def _conv_taps(win_ref, r0, lt, weight_of_offset, rows):
    lanes = pl.ds(lt * LANE, LANE)
    win = win_ref[pl.ds(r0, rows + HALO), lanes]
    n_out = rows // SUB
    acc = [jnp.zeros((SUB, LANE), F32) for _ in range(n_out)]
    for phase in range(SUB):
        offs = [o for o in weight_of_offset if o % SUB == phase]
        if not offs:
            continue
        q_max = max(o // SUB for o in offs)
        span = (n_out + q_max) * SUB
        sh = win[phase:phase + span, :]
        for o in offs:
            q = o // SUB
            w = weight_of_offset[o](lanes)
            for m in range(n_out):
                acc[m] = acc[m] + w * sh[(m + q) * SUB:(m + q + 1) * SUB, :]
    return acc


def _branch_a_fwd(p, conv_wb, conv_b, ln_g, ln_b):
    s_len = p.shape[0]
    tm = min(256, s_len)
    n_i = s_len // tm
    rows = 32

    def body(p_ref, wb_ref, cb_ref, g_ref, b_ref, ya_ref, y1_ref, abuf):
        @pl.when(pl.program_id(0) == 0)
        def _():
            abuf[pl.ds(0, HALO), :] = jnp.zeros((HALO, D), F32)

        def glu(r0):
            val = p_ref[pl.ds(r0, PACK), pl.ds(0, D)].astype(F32)
            gl = p_ref[pl.ds(r0, PACK), pl.ds(D, D)].astype(F32)
            abuf[pl.ds(HALO + r0, PACK), :] = val * _sigmoid(gl)
        _strips(tm, PACK,glu)

        taps = {HALO - (CONV_K - 1) + k: (lambda lanes, k=k: wb_ref[pl.ds(k * SUB, SUB), lanes]) for k in range(CONV_K)}

        def conv(r0):
            for lt in range(D // LANE):
                acc = _conv_taps(abuf, r0, lt, taps, rows)
                cb = cb_ref[:, pl.ds(lt * LANE, LANE)]
                for m, v in enumerate(acc):
                    y1_ref[pl.ds(r0 + m * SUB, SUB), pl.ds(lt * LANE, LANE)] = v + cb
        _strips(tm, rows, conv)

        def norm(r0):
            y1 = y1_ref[pl.ds(r0, PACK), :]
            mu = _rowmean(y1)
            yc = y1 - mu
            rstd = lax.rsqrt(_rowmean(yc * yc) + EPS)
            l1 = (yc * rstd) * g_ref[...] + b_ref[...]
            z = p_ref[pl.ds(r0, PACK), pl.ds(2 * D, D)].astype(F32)
            ya_ref[pl.ds(r0, PACK), :] = ((l1 * _sigmoid(l1)) * (z * _sigmoid(z))).astype(BF16)
        _strips(tm, PACK,norm)

        abuf[pl.ds(0, HALO), :] = abuf[pl.ds(tm, HALO), :]

    return pl.pallas_call(
        body, name="branch_a_fwd", grid=(n_i,),
        in_specs=[pl.BlockSpec((tm, 3 * D), lambda i: (i, 0)),
                  pl.BlockSpec((CONV_K * SUB, D), lambda i: (0, 0)), _vec_spec(1), _vec_spec(1), _vec_spec(1)],
        out_specs=[pl.BlockSpec((tm, D), lambda i: (i, 0)), pl.BlockSpec((tm, D), lambda i: (i, 0))],
        out_shape=[_sds((s_len, D), BF16), _sds((s_len, D), F32)],
        scratch_shapes=[pltpu.VMEM((tm + HALO, D), F32)],
        compiler_params=_params(("arbitrary",)),
    )(p, conv_wb, conv_b, ln_g, ln_b)


def _branch_b_fwd(p, wt, bias_full, ln_g, ln_b):
    s_len = p.shape[0]
    tm = min(256, s_len)
    n_i = s_len // tm

    def body(p_ref, wt_ref, bias_ref, g_ref, b_ref, yb_ref, vb, sbuf):
        def norm(r0):
            gv = _gelu(p_ref[pl.ds(r0, PACK), pl.ds(D, D)].astype(F32))
            mu = _rowmean(gv)
            vc = gv - mu
            rstd = lax.rsqrt(_rowmean(vc * vc) + EPS)
            vb[pl.ds(r0, PACK), :] = ((vc * rstd) * g_ref[...] + b_ref[...]).astype(BF16)
        _strips(tm, PACK,norm)

        for ck in range(tm // CHUNK):
            for h in range(HEADS):
                blk = (pl.ds(ck * CHUNK, CHUNK), pl.ds(h * LANE, LANE))
                sbuf[blk] = jnp.dot(wt_ref[h], vb[blk], preferred_element_type=F32) + bias_ref[:, pl.ds(h * LANE, LANE)]

        def gate(r0):
            u = _gelu(p_ref[pl.ds(r0, PACK), pl.ds(0, D)].astype(F32))
            z = p_ref[pl.ds(r0, PACK), pl.ds(2 * D, D)].astype(F32)
            yb_ref[pl.ds(r0, PACK), :] = (u * sbuf[pl.ds(r0, PACK), :] * (z * _sigmoid(z))).astype(BF16)
        _strips(tm, PACK,gate)

    return pl.pallas_call(
        body, name="branch_b_fwd", grid=(n_i,),
        in_specs=[pl.BlockSpec((tm, 3 * D), lambda i: (i, 1)),
                  pl.BlockSpec((HEADS, CHUNK, CHUNK), lambda i: (0, 0, 0)),
                  pl.BlockSpec((CHUNK, D), lambda i: (0, 0)), _vec_spec(1), _vec_spec(1)],
        out_specs=pl.BlockSpec((tm, D), lambda i: (i, 0)),
        out_shape=_sds((s_len, D), BF16),
        scratch_shapes=[pltpu.VMEM((tm, D), BF16), pltpu.VMEM((tm, D), F32)],
        compiler_params=_params(("arbitrary",)),
    )(p, wt, bias_full, ln_g, ln_b)


def _dot_t(a, b):
    return lax.dot_general(a, b, (((1,), (1,)), ((), ())), preferred_element_type=F32)


def _out_proj(p, ya_in, yb_in, x, target, gate, g_final, w_co, w_so, w_o):
    s_len = x.shape[0]
    tm = min(256, s_len)
    n_i = s_len // tm

    def body(pg_ref, ya_ref, yb_ref, x_ref, t_ref, gate_ref, gf_ref, wco_ref, wso_ref, wo_ref,
             dx2_ref, dya_ref, dyb_ref, dp_ref, mb_ref, dob_ref, dyab_ref, dybb_ref, sums_ref):
        @pl.when(pl.program_id(0) == 0)
        def _():
            sums_ref[...] = jnp.zeros((SUB, D), F32)

        y_a = jnp.dot(ya_ref[...], wco_ref[...], preferred_element_type=F32)
        y_b = jnp.dot(yb_ref[...], wso_ref[...], preferred_element_type=F32)
        ga = _sigmoid(pg_ref[:, pl.ds(0, D)].astype(F32))
        gb = _sigmoid(pg_ref[:, pl.ds(D, D)].astype(F32))
        mb = (ga * y_a + gb * y_b).astype(BF16)
        mb_ref[...] = mb
        o = jnp.dot(mb, wo_ref[...], preferred_element_type=F32)
        x2 = x_ref[...] + gate_ref[...] * o
        r2 = lax.rsqrt(_rowmean(x2 * x2) + EPS)
        xh = x2 * r2
        e = xh * gf_ref[...] - t_ref[...]
        dy = e * (1.0 / D)
        dxh = dy * gf_ref[...]
        dx2 = r2 * (dxh - xh * _rowmean(dxh * xh))
        dx2_ref[...] = dx2
        sums_ref[pl.ds(0, 1), :] += jnp.sum(dy * xh, axis=0, keepdims=True)
        sums_ref[pl.ds(1, 1), :] += jnp.sum(dx2 * o, axis=0, keepdims=True)
        sums_ref[pl.ds(2, 1), :] += jnp.sum(e * e, axis=0, keepdims=True) * (0.5 / D)
        dob = (gate_ref[...] * dx2).astype(BF16)
        dob_ref[...] = dob
        dm = _dot_t(dob, wo_ref[...])
        dy_a = (ga * dm).astype(BF16)
        dy_b = (gb * dm).astype(BF16)
        dyab_ref[...] = dy_a
        dybb_ref[...] = dy_b
        dp_ref[:, pl.ds(0, D)] = (dm * y_a * ga * (1.0 - ga)).astype(BF16)
        dp_ref[:, pl.ds(D, D)] = (dm * y_b * gb * (1.0 - gb)).astype(BF16)
        dya_ref[...] = _dot_t(dy_a, wco_ref[...])
        dyb_ref[...] = _dot_t(dy_b, wso_ref[...])

    tile = pl.BlockSpec((tm, D), lambda i: (i, 0))
    wspec = pl.BlockSpec((D, D), lambda i: (0, 0))
    return pl.pallas_call(
        body, name="out_proj", grid=(n_i,),
        in_specs=[pl.BlockSpec((tm, 2 * D), lambda i: (i, 3)), tile, tile, tile, tile, _vec_spec(1), _vec_spec(1),
                  wspec, wspec, wspec],
        out_specs=[tile, tile, tile, pl.BlockSpec((tm, 2 * D), lambda i: (i, 3)), tile, tile, tile, tile,
                   pl.BlockSpec((SUB, D), lambda i: (0, 0))],
        out_shape=[_sds((s_len, D), F32), _sds((s_len, D), F32), _sds((s_len, D), F32), _sds((s_len, N_SEC * D), BF16),
                   _sds((s_len, D), BF16), _sds((s_len, D), BF16), _sds((s_len, D), BF16), _sds((s_len, D), BF16),
                   _sds((SUB, D), F32)],
        compiler_params=_params(("arbitrary",)),
    )(p, ya_in, yb_in, x, target, gate, g_final, w_co, w_so, w_o)


A_STATS_ROWS = 8 + HALO


def _branch_a_bwd(p, p_halo_src, y1, dya_in, dp, conv_wb, ln_g, ln_b):
    s_len = p.shape[0]
    tm = min(256, s_len)
    n_i = s_len // tm
    rows = 32
    halo_blocks = tm // HALO

    def tile_of(i):
        return n_i - 1 - i

    def body(p_ref, ph_ref, y1_ref, dya_ref, dp_in, wb_ref, g_ref, b_ref, dp_ref, st_ref, abuf, dybuf, acc8, tapacc):
        del dp_in
        i = pl.program_id(0)
        first_tile = tile_of(i) == 0

        @pl.when(i == 0)
        def _():
            dybuf[pl.ds(tm, HALO), :] = jnp.zeros((HALO, D), F32)
            st_ref[...] = jnp.zeros((A_STATS_ROWS, D), F32)
            acc8[...] = jnp.zeros((3 * PACK, D), F32)
            tapacc[...] = jnp.zeros((CONV_K * SUB, D), F32)

        def glu(r0):
            val = p_ref[pl.ds(r0, PACK), pl.ds(0, D)].astype(F32)
            gl = p_ref[pl.ds(r0, PACK), pl.ds(D, D)].astype(F32)
            abuf[pl.ds(HALO + r0, PACK), :] = val * _sigmoid(gl)
        _strips(tm, PACK,glu)

        def glu_halo(r0):
            val = ph_ref[pl.ds(r0, PACK), pl.ds(0, D)].astype(F32)
            gl = ph_ref[pl.ds(r0, PACK), pl.ds(D, D)].astype(F32)
            abuf[pl.ds(r0, PACK), :] = jnp.where(first_tile, 0.0, val * _sigmoid(gl))
        _strips(HALO, PACK,glu_halo)

        def norm_bwd(r0):
            y1 = y1_ref[pl.ds(r0, PACK), :]
            mu = _rowmean(y1)
            yc = y1 - mu
            rstd = lax.rsqrt(_rowmean(yc * yc) + EPS)
            n1 = yc * rstd
            l1 = n1 * g_ref[...] + b_ref[...]
            sg = _sigmoid(l1)
            z = p_ref[pl.ds(r0, PACK), pl.ds(2 * D, D)].astype(F32)
            sz = _sigmoid(z)
            dya = dya_ref[pl.ds(r0, PACK), :]
            dp_ref[pl.ds(r0, PACK), pl.ds(2 * D, D)] = (dya * (l1 * sg) * _dsilu(z, sz)).astype(BF16)
            dl1 = dya * (z * sz) * _dsilu(l1, sg)
            acc8[pl.ds(0, PACK), :] += dl1 * n1
            acc8[pl.ds(PACK, PACK), :] += dl1
            dn1 = dl1 * g_ref[...]
            dy1 = rstd * (dn1 - _rowmean(dn1) - n1 * _rowmean(dn1 * n1))
            acc8[pl.ds(2 * PACK, PACK), :] += dy1
            dybuf[pl.ds(r0, PACK), :] = dy1
        _strips(tm, PACK,norm_bwd)

        taps_d = {CONV_K - 1 - k: (lambda lanes, k=k: wb_ref[pl.ds(k * SUB, SUB), lanes]) for k in range(CONV_K)}

        def conv_bwd_data(r0):
            for lt in range(D // LANE):
                lanes = pl.ds(lt * LANE, LANE)
                acc = _conv_taps(dybuf, r0, lt, taps_d, rows)
                for m in range(0, len(acc), PACK // SUB):
                    da = jnp.concatenate(acc[m:m + PACK // SUB], axis=0)
                    rr = pl.ds(r0 + m * SUB, PACK)
                    val = p_ref[rr, pl.ds(lt * LANE, LANE)].astype(F32)
                    sg = _sigmoid(p_ref[rr, pl.ds(D + lt * LANE, LANE)].astype(F32))
                    dp_ref[rr, lanes] = (da * sg).astype(BF16)
                    dp_ref[rr, pl.ds(D + lt * LANE, LANE)] = (da * val * sg * (1.0 - sg)).astype(BF16)
        _strips(tm, rows, conv_bwd_data)

        n_out = rows // SUB

        def conv_bwd_w(r0):
            for lt in range(D // LANE):
                lanes = pl.ds(lt * LANE, LANE)
                win = abuf[pl.ds(r0, rows + HALO), lanes]
                dy = [dybuf[pl.ds(r0 + m * SUB, SUB), lanes] for m in range(n_out)]
                for phase in range(SUB):
                    ks = [k for k in range(CONV_K) if (HALO - (CONV_K - 1) + k) % SUB == phase]
                    q_max = max((HALO - (CONV_K - 1) + k) // SUB for k in ks)
                    sh = win[phase:phase + (n_out + q_max) * SUB, :]
                    for k in ks:
                        q = (HALO - (CONV_K - 1) + k) // SUB
                        part = dy[0] * sh[q * SUB:(q + 1) * SUB, :]
                        for m in range(1, n_out):
                            part = part + dy[m] * sh[(m + q) * SUB:(m + q + 1) * SUB, :]
                        tapacc[pl.ds(k * SUB, SUB), lanes] += part
        _strips(tm, rows, conv_bwd_w)

        dybuf[pl.ds(tm, HALO), :] = dybuf[pl.ds(0, HALO), :]

        @pl.when(i == n_i - 1)
        def _():
            for j in range(3):
                st_ref[pl.ds(j, 1), :] = jnp.sum(acc8[pl.ds(j * PACK, PACK), :], axis=0, keepdims=True)
            for k in range(CONV_K):
                st_ref[pl.ds(SUB + k, 1), :] = jnp.sum(tapacc[pl.ds(k * SUB, SUB), :], axis=0, keepdims=True)

    return pl.pallas_call(
        body, name="branch_a_bwd", grid=(n_i,),
        in_specs=[pl.BlockSpec((tm, 3 * D), lambda i: (tile_of(i), 0)),
                  pl.BlockSpec((HALO, 2 * D), lambda i: (jnp.maximum(tile_of(i) * halo_blocks - 1, 0), 0)),
                  pl.BlockSpec((tm, D), lambda i: (tile_of(i), 0)),
                  pl.BlockSpec((tm, D), lambda i: (tile_of(i), 0)),
                  pl.BlockSpec(memory_space=pl.ANY),
                  pl.BlockSpec((CONV_K * SUB, D), lambda i: (0, 0)), _vec_spec(1), _vec_spec(1)],
        out_specs=[pl.BlockSpec((tm, 3 * D), lambda i: (tile_of(i), 0)),
                   pl.BlockSpec((A_STATS_ROWS, D), lambda i: (0, 0))],
        out_shape=[_sds(dp.shape, BF16), _sds((A_STATS_ROWS, D), F32)],
        scratch_shapes=[pltpu.VMEM((tm + HALO, D), F32), pltpu.VMEM((tm + HALO, D), F32), pltpu.VMEM((3 * PACK, D), F32),
                        pltpu.VMEM((CONV_K * SUB, D), F32)],
        input_output_aliases={4: 0},
        compiler_params=_params(("arbitrary",)),
    )(p, p_halo_src, y1, dya_in, dp, conv_wb, ln_g, ln_b)


def _branch_b_bwd(p, dyb_in, dp, wt, wtt, bias_full, ln_g, ln_b):
    s_len = p.shape[0]
    tm = min(256, s_len)
    n_i = s_len // tm

    def body(p_ref, dyb_ref, dp_in, wt_ref, wtt_ref, bias_ref, g_ref, b_ref, dp_ref, st_ref, gbt_ref, gw_ref,
             vb, n2buf, rstdbuf, sbuf, dsb, dvbuf, acc8, gb_ref):
        del dp_in
        i = pl.program_id(0)

        @pl.when(i == 0)
        def _():
            st_ref[...] = jnp.zeros((SUB, D), F32)
            gbt_ref[...] = jnp.zeros((CHUNK, LANE), F32)
            gb_ref[...] = jnp.zeros((CHUNK, D), F32)
            gw_ref[...] = jnp.zeros((HEADS, CHUNK, CHUNK), F32)
            acc8[...] = jnp.zeros((2 * PACK, D), F32)

        def norm(r0):
            gv = _gelu(p_ref[pl.ds(r0, PACK), pl.ds(D, D)].astype(F32))
            mu = _rowmean(gv)
            vc = gv - mu
            rstd = lax.rsqrt(_rowmean(vc * vc) + EPS)
            n2 = vc * rstd
            n2buf[pl.ds(r0, PACK), :] = n2
            rstdbuf[pl.ds(r0, PACK), :] = jnp.broadcast_to(rstd, (PACK, LANE))
            vb[pl.ds(r0, PACK), :] = (n2 * g_ref[...] + b_ref[...]).astype(BF16)
        _strips(tm, PACK,norm)

        for ck in range(tm // CHUNK):
            for h in range(HEADS):
                blk = (pl.ds(ck * CHUNK, CHUNK), pl.ds(h * LANE, LANE))
                sbuf[blk] = jnp.dot(wt_ref[h], vb[blk], preferred_element_type=F32) + bias_ref[:, pl.ds(h * LANE, LANE)]

        def gate_bwd(r0):
            pu = p_ref[pl.ds(r0, PACK), pl.ds(0, D)].astype(F32)
            u = _gelu(pu)
            z = p_ref[pl.ds(r0, PACK), pl.ds(2 * D, D)].astype(F32)
            sg = _sigmoid(z)
            sz = z * sg
            s = sbuf[pl.ds(r0, PACK), :]
            dyb = dyb_ref[pl.ds(r0, PACK), :]
            ds = dyb * u * sz
            dsb[pl.ds(r0, PACK), :] = ds.astype(BF16)
            gb_ref[pl.ds(pl.multiple_of(r0 % CHUNK, PACK), PACK), :] += ds
            dp_ref[pl.ds(r0, PACK), pl.ds(0, D)] = (dyb * s * sz * _dgelu(pu)).astype(BF16)
            dp_ref[pl.ds(r0, PACK), pl.ds(2 * D, D)] = (dyb * u * s * _dsilu(z, sg)).astype(BF16)
        _strips(tm, PACK,gate_bwd)

        for ck in range(tm // CHUNK):
            for h in range(HEADS):
                blk = (pl.ds(ck * CHUNK, CHUNK), pl.ds(h * LANE, LANE))
                d_s = dsb[blk]
                dvbuf[blk] = jnp.dot(wtt_ref[h], d_s, preferred_element_type=F32)
                gw_ref[h] += _dot_t(d_s, vb[blk])

        def norm_bwd(r0):
            dv = dvbuf[pl.ds(r0, PACK), :]
            n2 = n2buf[pl.ds(r0, PACK), :]
            rstd = rstdbuf[pl.ds(r0, PACK), pl.ds(0, 1)]
            acc8[pl.ds(0, PACK), :] += dv * n2
            acc8[pl.ds(PACK, PACK), :] += dv
            dn2 = dv * g_ref[...]
            dgv = rstd * (dn2 - _rowmean(dn2) - n2 * _rowmean(dn2 * n2))
            dp_ref[pl.ds(r0, PACK), pl.ds(D, D)] = (dgv * _dgelu(p_ref[pl.ds(r0, PACK), pl.ds(D, D)].astype(F32))).astype(BF16)
        _strips(tm, PACK,norm_bwd)

        @pl.when(i == n_i - 1)
        def _():
            for j in range(2):
                st_ref[pl.ds(j, 1), :] = jnp.sum(acc8[pl.ds(j * PACK, PACK), :], axis=0, keepdims=True)
            row = lax.broadcasted_iota(jnp.int32, (CHUNK, CHUNK), 0)
            col = lax.broadcasted_iota(jnp.int32, (CHUNK, CHUNK), 1)
            for h in range(HEADS):
                gw_ref[h] = jnp.where(row >= col, gw_ref[h], 0.0)
            lane = lax.broadcasted_iota(jnp.int32, (CHUNK, LANE), 1)
            gbt = jnp.zeros((CHUNK, LANE), F32)
            for h in range(HEADS):
                gbt = jnp.where(lane == h, jnp.sum(gb_ref[:, pl.ds(h * LANE, LANE)], axis=1, keepdims=True), gbt)
            gbt_ref[...] = gbt

    wspec = pl.BlockSpec((HEADS, CHUNK, CHUNK), lambda i: (0, 0, 0))
    return pl.pallas_call(
        body, name="branch_b_bwd", grid=(n_i,),
        in_specs=[pl.BlockSpec((tm, 3 * D), lambda i: (i, 1)), pl.BlockSpec((tm, D), lambda i: (i, 0)),
                  pl.BlockSpec(memory_space=pl.ANY), wspec, wspec,
                  pl.BlockSpec((CHUNK, D), lambda i: (0, 0)), _vec_spec(1), _vec_spec(1)],
        out_specs=[pl.BlockSpec((tm, 3 * D), lambda i: (i, 1)), pl.BlockSpec((SUB, D), lambda i: (0, 0)),
                   pl.BlockSpec((CHUNK, LANE), lambda i: (0, 0)), wspec],
        out_shape=[_sds(dp.shape, BF16), _sds((SUB, D), F32), _sds((CHUNK, LANE), F32), _sds((HEADS, CHUNK, CHUNK), F32)],
        scratch_shapes=[pltpu.VMEM((tm, D), BF16), pltpu.VMEM((tm, D), F32), pltpu.VMEM((tm, LANE), F32),
                        pltpu.VMEM((tm, D), F32), pltpu.VMEM((tm, D), BF16), pltpu.VMEM((tm, D), F32),
                        pltpu.VMEM((2 * PACK, D), F32), pltpu.VMEM((CHUNK, D), F32)],
        input_output_aliases={2: 0},
        compiler_params=_params(("arbitrary",)),
    )(p, dyb_in, dp, wt, wtt, bias_full, ln_g, ln_b)


def _in_proj_bwd(dp, wg_in, x, dx2, shift, scale, g_pre):
    del shift
    s_len = x.shape[0]
    tm = min(512, s_len)
    n_i = s_len // tm
    wn = wg_in.shape[2]

    def body(dp_ref, w_ref, x_ref, dx2_ref, sc_ref, g_ref, gx_ref, st_ref, acc, acc8):
        i = pl.program_id(0)

        @pl.when(i == 0)
        def _():
            st_ref[...] = jnp.zeros((SUB, D), F32)
            acc8[...] = jnp.zeros((3 * PACK, D), F32)

        dh = _dot_t(dp_ref[:, pl.ds(0, wn)], w_ref[0])
        for j in range(1, N_CHIP):
            dh = dh + _dot_t(dp_ref[:, pl.ds(j * wn, wn)], w_ref[j])
        acc[...] = dh

        def strip(r0):
            xs = x_ref[pl.ds(r0, PACK), :]
            r = lax.rsqrt(_rowmean(xs * xs) + EPS)
            xn = xs * r
            dhs = acc[pl.ds(r0, PACK), :]
            acc8[pl.ds(0, PACK), :] += dhs
            acc8[pl.ds(PACK, PACK), :] += dhs * (xn * g_ref[...])
            dhp = dhs * (1.0 + sc_ref[...])
            acc8[pl.ds(2 * PACK, PACK), :] += dhp * xn
            dxn = dhp * g_ref[...]
            gx_ref[pl.ds(r0, PACK), :] = dx2_ref[pl.ds(r0, PACK), :] + r * (dxn - xn * _rowmean(dxn * xn))
        _strips(tm, PACK, strip)

        @pl.when(i == n_i - 1)
        def _():
            for k in range(3):
                st_ref[pl.ds(k, 1), :] = jnp.sum(acc8[pl.ds(k * PACK, PACK), :], axis=0, keepdims=True)

    tile = pl.BlockSpec((tm, D), lambda i: (i, 0))
    return pl.pallas_call(
        body, name="in_proj_bwd", grid=(n_i,),
        in_specs=[pl.BlockSpec((tm, N_CHIP * wn), lambda i: (i, 0)),
                  pl.BlockSpec((N_CHIP, D, wn), lambda i: (0, 0, 0), pipeline_mode=pl.Buffered(1)),
                  tile, tile, _vec_spec(1), _vec_spec(1)],
        out_specs=[tile, pl.BlockSpec((SUB, D), lambda i: (0, 0))],
        out_shape=[_sds((s_len, D), F32), _sds((SUB, D), F32)],
        scratch_shapes=[pltpu.VMEM((tm, D), F32), pltpu.VMEM((3 * PACK, D), F32)],
        compiler_params=_params(("arbitrary",)),
    )(dp, wg_in, x, dx2, scale, g_pre)


def _grad_matmul(a, b, name):
    s_len, n = b.shape
    cb = min(2 * D, n)
    tn = 512
    per = cb // tn

    def body(a_ref, b_ref, ob_ref):
        ob_ref[0] = lax.dot_general(a_ref[...], b_ref[...], (((0,), (0,)), ((), ())),
                                    preferred_element_type=F32).astype(BF16)

    return pl.pallas_call(
        body, name=name, grid=(n // tn,),
        in_specs=[pl.BlockSpec((s_len, D), lambda j: (0, 0), pipeline_mode=pl.Buffered(1)),
                  pl.BlockSpec((s_len, tn), lambda j: (0, j))],
        out_specs=pl.BlockSpec((1, D, tn), lambda j: (j // per, 0, j % per)),
        out_shape=_sds((n // cb, D, cb), BF16),
        compiler_params=_params(("arbitrary",)),
    )(a, b)


def _local_step(x, target, shift, scale, gate, g_pre, conv_w_full, conv_b, conv_ln_g, conv_ln_b,
                sgu_ln_g, sgu_ln_b, w_sgu, b_sgu, g_final, wg_in, w_co, w_so, w_o):
    conv_wb = jnp.repeat(conv_w_full, SUB, axis=0)
    causal = jnp.tril(jnp.ones((CHUNK, CHUNK), dtype=bool))
    wt = jnp.where(causal[None], w_sgu, 0.0).astype(BF16)
    wtt = jnp.swapaxes(wt, 1, 2)
    bias_full = jnp.repeat(b_sgu.T, LANE, axis=1)

    p, hb = _in_proj(x, shift, scale, g_pre, wg_in)
    ya_in, y1 = _branch_a_fwd(p, conv_wb, conv_b, conv_ln_g, conv_ln_b)
    yb_in = _branch_b_fwd(p, wt, bias_full, sgu_ln_g, sgu_ln_b)
    dx2, dya_in, dyb_in, dp, mb, dob, dyab, dybb, sums_o = _out_proj(
        p, ya_in, yb_in, x, target, gate, g_final, w_co, w_so, w_o)
    dp, st_a = _branch_a_bwd(p, p, y1, dya_in, dp, conv_wb, conv_ln_g, conv_ln_b)
    dp, st_b, gbt, gws = _branch_b_bwd(p, dyb_in, dp, wt, wtt, bias_full, sgu_ln_g, sgu_ln_b)
    grad_x, st_i = _in_proj_bwd(dp, wg_in, x, dx2, shift, scale, g_pre)
    gw_in = _grad_matmul(hb, dp, "grad_w_in")
    gw_o = _grad_matmul(mb, dob, "grad_w_o")
    gw_co = _grad_matmul(ya_in, dyab, "grad_w_conv_out")
    gw_so = _grad_matmul(yb_in, dybb, "grad_w_sgu_out")
    return dict(
        grad_x=grad_x, loss_cols=sums_o[2:3], g_final=sums_o[0:1], d_gate=sums_o[1:2],
        d_shift=st_i[0:1], d_scale=st_i[1:2], g_pre=st_i[2:3],
        conv_ln_g=st_a[0:1], conv_ln_b=st_a[1:2], conv_b=st_a[2:3], conv_w=st_a[SUB:SUB + CONV_K],
        sgu_ln_g=st_b[0:1], sgu_ln_b=st_b[1:2], b_sgu=gbt[:, :HEADS].T, w_sgu=gws,
        w_in=gw_in, w_o=gw_o, w_conv_out=gw_co, w_sgu_out=gw_so)


ANY_SPEC = pl.BlockSpec(memory_space=pl.ANY)
VMEM_SPEC = pl.BlockSpec(memory_space=pltpu.VMEM)


def _place():
    return lax.axis_index("x"), lax.axis_index("y"), lax.axis_index("c")


def _peer(k):
    x, y, c = _place()
    return (1 - x if k & 4 else x, 1 - y if k & 2 else y, 1 - c if k & 1 else c)


def _dev_of(p):
    return 4 * p[0] + 2 * p[1] + p[2]


def _chip_of(p):
    return 2 * p[0] + p[1]


def _rdma(src, dst, send_sem, recv_sem, to):
    return pltpu.make_async_remote_copy(src_ref=src, dst_ref=dst, send_sem=send_sem, recv_sem=recv_sem,
                                        device_id=to, device_id_type=MESH)


CHIP_PEERS = (2, 4, 6)
ALL_PEERS = tuple(range(1, N_DEV))
SIBLING = 1


def _setup_comm(c8, w_ada_s, b_ada_s, convw_s):
    n_mod = w_ada_s.shape[1]
    rows = SUB * N_DEV

    def body(c8_ref, wada_ref, bada_ref, cw_ref, call_ref, mod_ref, cwall_ref, csend, crecv, wsend, wrecv, msend, mrecv):
        me = _place()
        dev, chip = _dev_of(me), _chip_of(me)

        def c_rows(d):
            return call_ref.at[pl.ds(pl.multiple_of(d * SUB, SUB), SUB), :]

        call_ref[pl.ds(pl.multiple_of(dev * SUB, SUB), SUB), :] = c8_ref[...]
        cwall_ref[chip] = cw_ref[...]
        c_out = [_rdma(c8_ref, c_rows(dev), csend.at[k], crecv.at[k], _peer(k)) for k in ALL_PEERS]
        w_out = [_rdma(cw_ref, cwall_ref.at[chip], wsend.at[k], wrecv.at[k], _peer(k)) for k in CHIP_PEERS]
        for cp in c_out + w_out:
            cp.start()
        for k in ALL_PEERS:
            _rdma(c8_ref, c_rows(_dev_of(_peer(k))), csend.at[k], crecv.at[k], _peer(k)).wait_recv()
        part = jnp.dot(call_ref[...].astype(BF16), wada_ref[...].astype(BF16), preferred_element_type=F32) + bada_ref[...]
        mod_ref[chip] = part
        m_out = [_rdma(mod_ref.at[chip], mod_ref.at[chip], msend.at[k], mrecv.at[k], _peer(k)) for k in CHIP_PEERS]
        for cp in m_out:
            cp.start()
        for k in CHIP_PEERS:
            pc = _chip_of(_peer(k))
            _rdma(cw_ref, cwall_ref.at[pc], wsend.at[k], wrecv.at[k], _peer(k)).wait_recv()
            _rdma(mod_ref.at[pc], mod_ref.at[pc], msend.at[k], mrecv.at[k], _peer(k)).wait_recv()
        for cp in c_out + w_out + m_out:
            cp.wait_send()

    return pl.pallas_call(
        body, name="setup_comm",
        in_specs=[VMEM_SPEC] * 4, out_specs=[VMEM_SPEC] * 3,
        out_shape=[_sds((rows, D), F32), _sds((N_CHIP, rows, n_mod), F32), _sds((N_CHIP,) + convw_s.shape, F32)],
        scratch_shapes=[pltpu.SemaphoreType.DMA((N_DEV,))] * 6,
        compiler_params=_params(),
    )(c8, w_ada_s, b_ada_s, convw_s)


def _gather_weights(shards):
    n = len(shards)

    def body(*refs):
        ins, outs = refs[:n], refs[n:2 * n]
        lsem, isend, irecv, dsend, drecv = refs[2 * n:]
        me = _place()
        chip, c = _chip_of(me), me[2]
        local = [pltpu.make_async_copy(ins[t], outs[t].at[chip], lsem.at[t]) for t in range(n)]
        for cp in local:
            cp.start()

        def half(t, which):
            hr = shards[t].shape[0] // 2
            return pl.ds(pl.multiple_of(which * hr, hr), hr)

        sends = []
        for t in range(n):
            for j, k in enumerate(CHIP_PEERS):
                cp = _rdma(ins[t].at[half(t, c)], outs[t].at[chip, half(t, c)], isend.at[t, j], irecv.at[t, j], _peer(k))
                cp.start()
                sends.append(cp)
        for t in range(n):
            for j, k in enumerate(CHIP_PEERS):
                blk = outs[t].at[_chip_of(_peer(k)), half(t, c)]
                _rdma(blk, blk, isend.at[t, j], irecv.at[t, j], _peer(k)).wait_recv()
                cp = _rdma(blk, blk, dsend.at[t, j], drecv.at[t, j], _peer(SIBLING))
                cp.start()
                sends.append(cp)
        for t in range(n):
            for j, k in enumerate(CHIP_PEERS):
                blk = outs[t].at[_chip_of(_peer(k)), half(t, 1 - c)]
                _rdma(blk, blk, dsend.at[t, j], drecv.at[t, j], _peer(SIBLING)).wait_recv()
        for cp in sends:
            cp.wait_send()
        for cp in local:
            cp.wait()

    return pl.pallas_call(
        body, name="gather_weights",
        in_specs=[VMEM_SPEC] * n, out_specs=[VMEM_SPEC] * n,
        out_shape=[_sds((N_CHIP,) + s.shape, s.dtype) for s in shards],
        scratch_shapes=[pltpu.SemaphoreType.DMA((n,))] + [pltpu.SemaphoreType.DMA((n, len(CHIP_PEERS)))] * 4,
        compiler_params=_params(),
    )(*shards)


def _reduce_scatter(grads, name):
    n = len(grads)
    shapes = [g.shape[2:] for g in grads]

    def body(*refs):
        ins, outs = refs[:n], refs[n:2 * n]
        pbufs, rbufs, accs = refs[2 * n:3 * n], refs[3 * n:4 * n], refs[4 * n:5 * n]
        psend, precv, csend, crecv, fsend, frecv = refs[5 * n:]
        me = _place()
        chip, c = _chip_of(me), me[2]
        sib = _peer(SIBLING)

        def to_sibling(t, d):
            return _rdma(ins[t].at[d, 1 - c], pbufs[t].at[d], psend.at[t, d], precv.at[t, d], sib)

        sends = []
        for t in range(n):
            for d in range(N_CHIP):
                cp = to_sibling(t, d)
                cp.start()
                sends.append(cp)
        for j in (1, 2, 3, 0):
            d = jnp.bitwise_xor(chip, j)
            for t in range(n):
                to_sibling(t, d).wait_recv()

                def pair_sum(r0, t=t, d=d, j=j):
                    rows = pl.ds(r0, PACK)
                    s = ins[t][d, c, rows, :].astype(F32) + pbufs[t][d, rows, :].astype(F32)
                    if j == 0:
                        accs[t][rows, :] = s
                    else:
                        pbufs[t][d, rows, :] = s.astype(BF16)
                _strips(shapes[t][0], PACK, pair_sum)
                if j:
                    cp = _rdma(pbufs[t].at[d], rbufs[t].at[j - 1], csend.at[t, j], crecv.at[t, j], _peer(2 * j))
                    cp.start()
                    sends.append(cp)
        for t in range(n):
            for j in (1, 2, 3):
                blk = rbufs[t].at[j - 1]
                _rdma(blk, blk, csend.at[t, j], crecv.at[t, j], _peer(2 * j)).wait_recv()

            def total(r0, t=t):
                rows = pl.ds(r0, PACK)
                s = accs[t][rows, :] + rbufs[t][0, rows, :].astype(F32)
                s = s + rbufs[t][1, rows, :].astype(F32)
                outs[t][c, rows, :] = s + rbufs[t][2, rows, :].astype(F32)
            _strips(shapes[t][0], PACK, total)
            cp = _rdma(outs[t].at[c], outs[t].at[c], fsend.at[t], frecv.at[t], sib)
            cp.start()
            sends.append(cp)
        for t in range(n):
            blk = outs[t].at[1 - c]
            _rdma(blk, blk, fsend.at[t], frecv.at[t], sib).wait_recv()
        for cp in sends:
            cp.wait_send()

    return pl.pallas_call(
        body, name=name,
        in_specs=[VMEM_SPEC] * n, out_specs=[VMEM_SPEC] * n,
        out_shape=[_sds((2,) + s, F32) for s in shapes],
        scratch_shapes=([pltpu.VMEM((N_CHIP,) + s, BF16) for s in shapes] + [pltpu.VMEM((N_CHIP - 1,) + s, BF16) for s in shapes]
                        + [pltpu.VMEM(s, F32) for s in shapes]
                        + [pltpu.SemaphoreType.DMA((n, N_CHIP))] * 4 + [pltpu.SemaphoreType.DMA((n,))] * 2),
        compiler_params=_params(),
    )(*grads)


def _scatter_grads(grads):
    n = len(grads)

    def body(*refs):
        ins, outs = refs[:n], refs[n:2 * n]
        lsem, send, recv = refs[2 * n:]
        me = _place()
        dev, chip, c = _dev_of(me), _chip_of(me), me[2]
        local = [pltpu.make_async_copy(ins[t].at[chip, c], outs[t].at[dev], lsem.at[t]) for t in range(n)]
        for cp in local:
            cp.start()
        sends = []
        for t in range(n):
            for k in ALL_PEERS:
                to = _peer(k)
                cp = _rdma(ins[t].at[_chip_of(to), to[2]], outs[t].at[dev], send.at[t, k], recv.at[t, k], to)
                cp.start()
                sends.append(cp)
        for t in range(n):
            for k in ALL_PEERS:
                blk = outs[t].at[_dev_of(_peer(k))]
                _rdma(blk, blk, send.at[t, k], recv.at[t, k], _peer(k)).wait_recv()
        for cp in sends:
            cp.wait_send()
        for cp in local:
            cp.wait()

    return pl.pallas_call(
        body, name="scatter_grads",
        in_specs=[ANY_SPEC] * n, out_specs=[ANY_SPEC] * n,
        out_shape=[_sds((N_DEV,) + g.shape[2:], g.dtype) for g in grads],
        scratch_shapes=[pltpu.SemaphoreType.DMA((n,))] + [pltpu.SemaphoreType.DMA((n, N_DEV))] * 2,
        compiler_params=_params(),
    )(*grads)


def _sum_devices(parts, name):
    _, r, cols = parts.shape
    tr = min(r, 128)

    def body(in_ref, o_ref):
        acc = in_ref[0].astype(F32)
        for d in range(1, N_DEV):
            acc = acc + in_ref[d].astype(F32)
        o_ref[...] = acc

    return pl.pallas_call(
        body, name=name, grid=(r // tr,),
        in_specs=[pl.BlockSpec((N_DEV, tr, cols), lambda i: (0, i, 0))],
        out_specs=pl.BlockSpec((tr, cols), lambda i: (i, 0)),
        out_shape=_sds((r, cols), F32),
        compiler_params=_params(("arbitrary",)),
    )(parts)


def _share_halves(reds):
    n = len(reds)

    def body(*refs):
        ins, outs = refs[:n], refs[n:2 * n]
        lsem, send, recv = refs[2 * n:]
        me = _place()
        c = me[2]
        local = [pltpu.make_async_copy(ins[t], outs[t].at[c], lsem.at[t]) for t in range(n)]
        sends = [_rdma(ins[t], outs[t].at[c], send.at[t], recv.at[t], _peer(SIBLING)) for t in range(n)]
        for cp in local + sends:
            cp.start()
        for t in range(n):
            _rdma(ins[t], outs[t].at[1 - c], send.at[t], recv.at[t], _peer(SIBLING)).wait_recv()
        for cp in sends:
            cp.wait_send()
        for cp in local:
            cp.wait()

    return pl.pallas_call(
        body, name="share_halves",
        in_specs=[VMEM_SPEC] * n, out_specs=[VMEM_SPEC] * n,
        out_shape=[_sds((2,) + r.shape, r.dtype) for r in reds],
        scratch_shapes=[pltpu.SemaphoreType.DMA((n,))] * 3,
        compiler_params=_params(),
    )(*reds)


def _sum_small(blob):
    rows = blob.shape[0]

    def body(b_ref, o_ref, pbuf, buf4, psend, precv, send, recv):
        me = _place()
        chip = _chip_of(me)
        pair = _rdma(b_ref, pbuf, psend, precv, _peer(SIBLING))
        pair.start()
        pair.wait()
        buf4[chip] = b_ref[...] + pbuf[...]
        out = [_rdma(buf4.at[chip], buf4.at[chip], send.at[k], recv.at[k], _peer(k)) for k in CHIP_PEERS]
        for cp in out:
            cp.start()
        for k in CHIP_PEERS:
            blk = buf4.at[_chip_of(_peer(k))]
            _rdma(blk, blk, send.at[k], recv.at[k], _peer(k)).wait_recv()
        o_ref[...] = (buf4[0] + buf4[1]) + (buf4[2] + buf4[3])
        for cp in out:
            cp.wait_send()

    return pl.pallas_call(
        body, name="sum_small",
        in_specs=[VMEM_SPEC], out_specs=VMEM_SPEC, out_shape=_sds(blob.shape, F32),
        scratch_shapes=[pltpu.VMEM((rows, D), F32), pltpu.VMEM((N_CHIP, rows, D), F32),
                        pltpu.SemaphoreType.DMA, pltpu.SemaphoreType.DMA,
                        pltpu.SemaphoreType.DMA((N_DEV,)), pltpu.SemaphoreType.DMA((N_DEV,))],
        compiler_params=_params(),
    )(blob)


def _adamw_math(w, g, m, v):
    m = ADAM_B1 * m + (1.0 - ADAM_B1) * g
    v = ADAM_B2 * v + (1.0 - ADAM_B2) * (g * g)
    m_hat = m / (1.0 - ADAM_B1 ** ADAM_STEP)
    v_hat = v / (1.0 - ADAM_B2 ** ADAM_STEP)
    delta = -ADAM_LR * (m_hat / (jnp.sqrt(v_hat) + ADAM_EPS) + ADAM_WD * w)
    return delta, m, v


def _row_tile(r, cols):
    if r * cols * 4 <= 2 ** 20:
        return r
    return next(t for t in (512, 256, 128, 64, 32, 16, 8) if r % t == 0 and t * cols * 4 <= 2 ** 20)


def _adamw(w, g, m, v, name):
    r, cols = w.shape
    tr = _row_tile(r, cols)

    def body(w_ref, g_ref, m_ref, v_ref, d_ref, nm_ref, nv_ref):
        d_ref[...], nm_ref[...], nv_ref[...] = _adamw_math(w_ref[...], g_ref[...], m_ref[...], v_ref[...])

    spec = pl.BlockSpec((tr, cols), lambda i: (i, 0))
    return pl.pallas_call(
        body, name=name, grid=(r // tr,), in_specs=[spec] * 4, out_specs=[spec] * 3,
        out_shape=[_sds((r, cols), F32)] * 3, compiler_params=_params(("arbitrary",)),
    )(w, g, m, v)


def _adamw_ada(w, ct, dm, m, v):
    r, cols = w.shape
    tr = _row_tile(r, cols)

    def body(w_ref, ct_ref, dm_ref, m_ref, v_ref, g_ref, d_ref, nm_ref, nv_ref):
        g = jnp.dot(ct_ref[...], dm_ref[...], preferred_element_type=F32)
        g_ref[...] = g
        d_ref[...], nm_ref[...], nv_ref[...] = _adamw_math(w_ref[...], g, m_ref[...], v_ref[...])

    spec = pl.BlockSpec((tr, cols), lambda i: (i, 0))
    return pl.pallas_call(
        body, name="adamw_ada", grid=(r // tr,),
        in_specs=[spec, pl.BlockSpec((tr, LANE), lambda i: (i, 0)), pl.BlockSpec((LANE, cols), lambda i: (0, 0)), spec, spec],
        out_specs=[spec] * 4, out_shape=[_sds((r, cols), F32)] * 4, compiler_params=_params(("arbitrary",)),
    )(w, ct, dm, m, v)


BLOB_VEC, BLOB_CONV, BLOB_SGU, BLOB_ADA, BLOB_DMOD, BLOB_LOSS, BLOB_ROWS = 0, 8, 40, 168, 176, 200, 208
SMALL_CONV, SMALL_SGU, SMALL_ADA, SMALL_ROWS = 8, 16, 144, 152


def _set_rows(buf, row, val):
    return lax.dynamic_update_slice(buf, val.astype(F32), (row, 0))


def _pack_small(vecs, b_sgu, conv_w_s, w_sgu, b_ada):
    buf = jnp.zeros((SMALL_ROWS, D), F32)
    for i, vec in enumerate(vecs):
        buf = _set_rows(buf, i, vec.reshape(1, D))
    buf = _set_rows(buf, 7, b_sgu.reshape(1, D))
    buf = _set_rows(buf, SMALL_CONV, jnp.pad(conv_w_s, ((0, 1), (0, 0))).reshape(SUB, D))
    buf = _set_rows(buf, SMALL_SGU, w_sgu.reshape(CHUNK, D))
    return _set_rows(buf, SMALL_ADA, b_ada.reshape(3, D))


def _unpack_small(buf, conv_cols):
    vecs = [buf[i] for i in range(7)]
    b_sgu = buf[7].reshape(HEADS, CHUNK)
    conv_w_s = buf[SMALL_CONV:SMALL_CONV + SUB].reshape(HALO, conv_cols)[:CONV_K]
    w_sgu = buf[SMALL_SGU:SMALL_SGU + CHUNK].reshape(HEADS, CHUNK, CHUNK)
    b_ada = buf[SMALL_ADA:SMALL_ADA + 3].reshape(1, 3 * D)
    return vecs, b_sgu, conv_w_s, w_sgu, b_ada


def kernel(x, c, w_ada, b_ada, g_pre, w_in, conv_w, conv_b, conv_ln_g, conv_ln_b, w_conv_out, sgu_ln_g, sgu_ln_b, w_sgu, b_sgu, w_sgu_out, w_o, g_final, loss_target, m_w_ada, m_b_ada, m_g_pre, m_w_in, m_conv_w, m_conv_b, m_conv_ln_g, m_conv_ln_b, m_w_conv_out, m_sgu_ln_g, m_sgu_ln_b, m_w_sgu, m_b_sgu, m_w_sgu_out, m_w_o, m_g_final, v_w_ada, v_b_ada, v_g_pre, v_w_in, v_conv_w, v_conv_b, v_conv_ln_g, v_conv_ln_b, v_w_conv_out, v_sgu_ln_g, v_sgu_ln_b, v_w_sgu, v_b_sgu, v_w_sgu_out, v_w_o, v_g_final):
    me = _place()
    dev, chip = _dev_of(me), _chip_of(me)
    n_ada = w_ada.shape[2]
    conv_cols = conv_w.shape[2]

    b_ada_s = lax.dynamic_slice(b_ada, (0, chip * n_ada), (1, n_ada))
    c_all, mod_all, cw_all = _setup_comm(
        jnp.broadcast_to(c, (SUB, D)), w_ada[0], b_ada_s, jnp.pad(conv_w[0], ((0, HALO - CONV_K), (0, 0))))
    mod = lax.dynamic_slice(mod_all, (0, dev * SUB, 0), (N_CHIP, 1, n_ada)).reshape(1, 3 * D)
    shift, scale, gate = mod[:, :D], mod[:, D:2 * D], mod[:, 2 * D:]
    conv_w_full = jnp.swapaxes(cw_all, 0, 1).reshape(HALO, D)[:CONV_K]

    wg_in, wg_co, wg_so, wg_o = _gather_weights(
        [w_in[0].astype(BF16), w_conv_out[0].astype(BF16), w_sgu_out[0].astype(BF16), w_o[0].astype(BF16)])

    loc = _local_step(x[0], loss_target[0], shift, scale, gate, g_pre, conv_w_full, conv_b, conv_ln_g, conv_ln_b,
                      sgu_ln_g, sgu_ln_b, w_sgu[0], b_sgu[0], g_final.reshape(1, D),
                      wg_in, wg_co.reshape(D, D), wg_so.reshape(D, D), wg_o.reshape(D, D))

    big = ["w_in", "w_conv_out", "w_sgu_out", "w_o"]
    contrib = []
    for name in big:
        g16 = loc[name]
        rows_half = (g16.shape[0] * g16.shape[1]) // (2 * N_CHIP) if name != "w_in" else g16.shape[1] // 2
        contrib.append(g16.reshape(N_CHIP, 2, rows_half, g16.shape[2]))
    full = _reduce_scatter(contrib[:1], "reduce_w_in") + _reduce_scatter(contrib[1:], "reduce_w_out")
    g_big = {name: f.reshape(2 * f.shape[1], f.shape[2]) for name, f in zip(big, full)}

    d_mod = jnp.concatenate([loc["d_shift"], loc["d_scale"], loc["d_gate"]], axis=0)
    blob = jnp.zeros((BLOB_ROWS, D), F32)
    for i, name in enumerate(["g_pre", "conv_b", "conv_ln_g", "conv_ln_b", "sgu_ln_g", "sgu_ln_b", "g_final"]):
        blob = _set_rows(blob, BLOB_VEC + i, loc[name])
    blob = _set_rows(blob, BLOB_VEC + 7, loc["b_sgu"].reshape(1, D))
    blob = _set_rows(blob, BLOB_CONV, loc["conv_w"])
    blob = _set_rows(blob, BLOB_SGU, loc["w_sgu"].reshape(CHUNK, D))
    blob = _set_rows(blob, BLOB_ADA, d_mod)
    blob = lax.dynamic_update_slice(blob, d_mod, (BLOB_DMOD + 3 * dev, 0))
    blob = _set_rows(blob, BLOB_LOSS, loc["loss_cols"])
    tot = _sum_small(blob)

    loss = jnp.sum(tot[BLOB_LOSS])
    g_vecs = [tot[BLOB_VEC + i] for i in range(7)]
    g_b_sgu = tot[BLOB_VEC + 7].reshape(HEADS, CHUNK)
    g_conv_s = lax.dynamic_slice(tot, (BLOB_CONV, chip * conv_cols), (CONV_K, conv_cols))
    g_w_sgu = tot[BLOB_SGU:BLOB_SGU + CHUNK].reshape(HEADS, CHUNK, CHUNK)
    g_b_ada = tot[BLOB_ADA:BLOB_ADA + 3].reshape(1, 3 * D)
    d_mod_all = tot[BLOB_DMOD:BLOB_DMOD + 3 * N_DEV].reshape(N_DEV, 3 * D)

    ct = jnp.pad(c_all[::SUB].T, ((0, 0), (0, LANE - N_DEV))).astype(BF16)
    dm = jnp.pad(lax.dynamic_slice(d_mod_all, (0, chip * n_ada), (N_DEV, n_ada)), ((0, LANE - N_DEV), (0, 0))).astype(BF16)
    g_ada, d_ada, nm_ada, nv_ada = _adamw_ada(w_ada[0], ct, dm, m_w_ada[0], v_w_ada[0])

    upd = {}
    for name, w, m, v in [("w_in", w_in, m_w_in, v_w_in), ("w_conv_out", w_conv_out, m_w_conv_out, v_w_conv_out),
                          ("w_sgu_out", w_sgu_out, m_w_sgu_out, v_w_sgu_out), ("w_o", w_o, m_w_o, v_w_o)]:
        upd[name] = _adamw(w[0], g_big[name], m[0], v[0], "adamw_" + name)

    def small(g_p, conv_p, sgu_p, ada_p, vec_ps):
        return _pack_small(vec_ps, g_p, conv_p, sgu_p, ada_p)

    w_s = small(b_sgu[0], conv_w[0], w_sgu[0], b_ada, [g_pre, conv_b, conv_ln_g, conv_ln_b, sgu_ln_g, sgu_ln_b, g_final])
    m_s = small(m_b_sgu[0], m_conv_w[0], m_w_sgu[0], m_b_ada,
                [m_g_pre, m_conv_b, m_conv_ln_g, m_conv_ln_b, m_sgu_ln_g, m_sgu_ln_b, m_g_final])
    v_s = small(v_b_sgu[0], v_conv_w[0], v_w_sgu[0], v_b_ada,
                [v_g_pre, v_conv_b, v_conv_ln_g, v_conv_ln_b, v_sgu_ln_g, v_sgu_ln_b, v_g_final])
    g_s = small(g_b_sgu, g_conv_s, g_w_sgu, g_b_ada, g_vecs)
    small_out = [_unpack_small(a, conv_cols) for a in (g_s,) + tuple(_adamw(w_s, g_s, m_s, v_s, "adamw_small"))]

    def leaves(kind):
        vecs, o_b_sgu, o_conv, o_w_sgu, o_b_ada = small_out[kind]
        ada = (g_ada, d_ada, nm_ada, nv_ada)[kind]
        def bigk(name):
            return (g_big[name] if kind == 0 else upd[name][kind - 1])[None]
        return [ada[None], o_b_ada, vecs[0][None], bigk("w_in"), o_conv[None], vecs[1][None], vecs[2][None], vecs[3][None],
                bigk("w_conv_out"), vecs[4][None], vecs[5][None], o_w_sgu[None], o_b_sgu[None], bigk("w_sgu_out"),
                bigk("w_o"), vecs[6]]

    return (loss, loc["grad_x"][None], *leaves(0), *leaves(1), *leaves(2), *leaves(3))
```

```python
import functools

import jax
import jax.numpy as jnp
from jax import lax
from jax.experimental import pallas as pl
from jax.experimental.pallas import tpu as pltpu

F32 = jnp.float32
BF16 = jnp.bfloat16
MESH = pl.DeviceIdType.MESH

D = 1024
N_SEC = 8
N_CHIP = 4
N_DEV = 8
EPS = 1e-6
CONV_K = 31
HALO = 32
CHUNK = 128
HEADS = 8
LANE = 128
SUB = 8
PACK = 16
VMEM_LIMIT = 56 * 1024 * 1024

ADAM_LR, ADAM_B1, ADAM_B2, ADAM_EPS, ADAM_WD, ADAM_STEP = 0.001, 0.9, 0.999, 1e-08, 0.01, 10

_SQRT_HALF = 0.7071067811865476
_INV_SQRT_2PI = 0.3989422804014327


def _sds(shape, dtype):
    return jax.ShapeDtypeStruct(shape, dtype)


def _params(sem=None):
    if sem is None:
        return pltpu.CompilerParams(vmem_limit_bytes=VMEM_LIMIT)
    return pltpu.CompilerParams(dimension_semantics=sem, vmem_limit_bytes=VMEM_LIMIT)


def _strips(n_rows, rows, fn):
    def step(s, carry):
        fn(pl.multiple_of(s * rows, rows))
        return carry
    lax.fori_loop(0, n_rows // rows, step, 0)


def _sigmoid(v):
    return 1.0 / (1.0 + jnp.exp(-v))


def _gelu(v):
    return 0.5 * v * (1.0 + lax.erf(v * _SQRT_HALF))


def _gelu_and_grad(v):
    cdf = 0.5 * (1.0 + lax.erf(v * _SQRT_HALF))
    return v * cdf, cdf + v * jnp.exp(-0.5 * v * v) * _INV_SQRT_2PI


def _dsilu(v, sg):
    return sg * (1.0 + v * (1.0 - sg))


def _rowmean(v):
    return jnp.mean(v, axis=-1, keepdims=True)


def _vec_spec(grid_rank):
    zeros = (0, 0)
    if grid_rank == 1:
        return pl.BlockSpec((1, D), lambda i: zeros)
    return pl.BlockSpec((1, D), lambda i, j: zeros)


def _in_proj(x, shift, scale, g_pre, wg_in):
    s_len = x.shape[0]
    tm = min(512, s_len)
    n_i = s_len // tm
    wn = wg_in.shape[2]

    def body(x_ref, sh_ref, sc_ref, g_ref, w_ref, p_ref, hb_ref):
        def strip(r0):
            xs = x_ref[pl.ds(r0, PACK), :]
            r = lax.rsqrt(_rowmean(xs * xs) + EPS)
            h = (xs * r) * g_ref[...] * (1.0 + sc_ref[...]) + sh_ref[...]
            hb_ref[pl.ds(r0, PACK), :] = h.astype(BF16)
        _strips(tm, PACK, strip)
        hb = hb_ref[...]
        for j in range(N_CHIP):
            p_ref[:, pl.ds(j * wn, wn)] = jnp.dot(hb, w_ref[j], preferred_element_type=F32).astype(BF16)

    return pl.pallas_call(
        body, name="in_proj", grid=(n_i,),
        in_specs=[pl.BlockSpec((tm, D), lambda i: (i, 0)), _vec_spec(1), _vec_spec(1), _vec_spec(1),
                  pl.BlockSpec((N_CHIP, D, wn), lambda i: (0, 0, 0), pipeline_mode=pl.Buffered(1))],
        out_specs=[pl.BlockSpec((tm, N_CHIP * wn), lambda i: (i, 0)), pl.BlockSpec((tm, D), lambda i: (i, 0))],
        out_shape=[_sds((s_len, N_SEC * D), BF16), _sds((s_len, D), BF16)],
        compiler_params=_params(("arbitrary",)),
    )(x, shift, scale, g_pre, wg_in)


def _conv_taps(win_ref, r0, lt, weight_of_offset, rows):
    lanes = pl.ds(lt * LANE, LANE)
    win = win_ref[pl.ds(r0, rows + HALO), lanes]
    n_out = rows // SUB
    acc = [jnp.zeros((SUB, LANE), F32) for _ in range(n_out)]
    for phase in range(SUB):
        offs = [o for o in weight_of_offset if o % SUB == phase]
        if not offs:
            continue
        q_max = max(o // SUB for o in offs)
        span = (n_out + q_max) * SUB
        sh = win[phase:phase + span, :]
        for o in offs:
            q = o // SUB
            w = weight_of_offset[o](lanes)
            for m in range(n_out):
                acc[m] = acc[m] + w * sh[(m + q) * SUB:(m + q + 1) * SUB, :]
    return acc


def _branch_a_fwd(p, conv_wb, conv_b, ln_g, ln_b):
    s_len = p.shape[0]
    tm = min(256, s_len)
    n_i = s_len // tm
    rows = 32

    def body(p_ref, wb_ref, cb_ref, g_ref, b_ref, ya_ref, y1_ref, abuf):
        @pl.when(pl.program_id(0) == 0)
        def _():
            abuf[pl.ds(0, HALO), :] = jnp.zeros((HALO, D), F32)

        def glu(r0):
            val = p_ref[pl.ds(r0, PACK), pl.ds(0, D)].astype(F32)
            gl = p_ref[pl.ds(r0, PACK), pl.ds(D, D)].astype(F32)
            abuf[pl.ds(HALO + r0, PACK), :] = val * _sigmoid(gl)
        _strips(tm, PACK,glu)

        taps = {HALO - (CONV_K - 1) + k: (lambda lanes, k=k: wb_ref[pl.ds(k * SUB, SUB), lanes]) for k in range(CONV_K)}

        def conv(r0):
            for lt in range(D // LANE):
                acc = _conv_taps(abuf, r0, lt, taps, rows)
                cb = cb_ref[:, pl.ds(lt * LANE, LANE)]
                for m, v in enumerate(acc):
                    y1_ref[pl.ds(r0 + m * SUB, SUB), pl.ds(lt * LANE, LANE)] = v + cb
        _strips(tm, rows, conv)

        def norm(r0):
            y1 = y1_ref[pl.ds(r0, PACK), :]
            mu = _rowmean(y1)
            yc = y1 - mu
            rstd = lax.rsqrt(_rowmean(yc * yc) + EPS)
            l1 = (yc * rstd) * g_ref[...] + b_ref[...]
            z = p_ref[pl.ds(r0, PACK), pl.ds(2 * D, D)].astype(F32)
            ya_ref[pl.ds(r0, PACK), :] = ((l1 * _sigmoid(l1)) * (z * _sigmoid(z))).astype(BF16)
        _strips(tm, PACK,norm)

        abuf[pl.ds(0, HALO), :] = abuf[pl.ds(tm, HALO), :]

    return pl.pallas_call(
        body, name="branch_a_fwd", grid=(n_i,),
        in_specs=[pl.BlockSpec((tm, 3 * D), lambda i: (i, 0)),
                  pl.BlockSpec((CONV_K * SUB, D), lambda i: (0, 0)), _vec_spec(1), _vec_spec(1), _vec_spec(1)],
        out_specs=[pl.BlockSpec((tm, D), lambda i: (i, 0)), pl.BlockSpec((tm, D), lambda i: (i, 0))],
        out_shape=[_sds((s_len, D), BF16), _sds((s_len, D), F32)],
        scratch_shapes=[pltpu.VMEM((tm + HALO, D), F32)],
        compiler_params=_params(("arbitrary",)),
    )(p, conv_wb, conv_b, ln_g, ln_b)


def _branch_b_fwd(p, wt, bias_full, ln_g, ln_b):
    s_len = p.shape[0]
    tm = min(256, s_len)
    n_i = s_len // tm

    def body(p_ref, wt_ref, bias_ref, g_ref, b_ref, yb_ref, vb, sbuf):
        def norm(r0):
            gv = _gelu(p_ref[pl.ds(r0, PACK), pl.ds(D, D)].astype(F32))
            mu = _rowmean(gv)
            vc = gv - mu
            rstd = lax.rsqrt(_rowmean(vc * vc) + EPS)
            vb[pl.ds(r0, PACK), :] = ((vc * rstd) * g_ref[...] + b_ref[...]).astype(BF16)
        _strips(tm, PACK,norm)

        for ck in range(tm // CHUNK):
            for h in range(HEADS):
                blk = (pl.ds(ck * CHUNK, CHUNK), pl.ds(h * LANE, LANE))
                sbuf[blk] = jnp.dot(wt_ref[h], vb[blk], preferred_element_type=F32) + bias_ref[:, pl.ds(h * LANE, LANE)]

        def gate(r0):
            u = _gelu(p_ref[pl.ds(r0, PACK), pl.ds(0, D)].astype(F32))
            z = p_ref[pl.ds(r0, PACK), pl.ds(2 * D, D)].astype(F32)
            yb_ref[pl.ds(r0, PACK), :] = (u * sbuf[pl.ds(r0, PACK), :] * (z * _sigmoid(z))).astype(BF16)
        _strips(tm, PACK,gate)

    return pl.pallas_call(
        body, name="branch_b_fwd", grid=(n_i,),
        in_specs=[pl.BlockSpec((tm, 3 * D), lambda i: (i, 1)),
                  pl.BlockSpec((HEADS, CHUNK, CHUNK), lambda i: (0, 0, 0)),
                  pl.BlockSpec((CHUNK, D), lambda i: (0, 0)), _vec_spec(1), _vec_spec(1)],
        out_specs=pl.BlockSpec((tm, D), lambda i: (i, 0)),
        out_shape=_sds((s_len, D), BF16),
        scratch_shapes=[pltpu.VMEM((tm, D), BF16), pltpu.VMEM((tm, D), F32)],
        compiler_params=_params(("arbitrary",)),
    )(p, wt, bias_full, ln_g, ln_b)


def _dot_t(a, b):
    return lax.dot_general(a, b, (((1,), (1,)), ((), ())), preferred_element_type=F32)


def _out_proj(p, ya_in, yb_in, x, target, gate, g_final, w_co, w_so, w_o):
    s_len = x.shape[0]
    tm = min(256, s_len)
    n_i = s_len // tm

    def body(pg_ref, ya_ref, yb_ref, x_ref, t_ref, gate_ref, gf_ref, wco_ref, wso_ref, wo_ref,
             dx2_ref, dya_ref, dyb_ref, dp_ref, mb_ref, dob_ref, dyab_ref, dybb_ref, sums_ref):
        @pl.when(pl.program_id(0) == 0)
        def _():
            sums_ref[...] = jnp.zeros((SUB, D), F32)

        y_a = jnp.dot(ya_ref[...], wco_ref[...], preferred_element_type=F32)
        y_b = jnp.dot(yb_ref[...], wso_ref[...], preferred_element_type=F32)
        ga = _sigmoid(pg_ref[:, pl.ds(0, D)].astype(F32))
        gb = _sigmoid(pg_ref[:, pl.ds(D, D)].astype(F32))
        mb = (ga * y_a + gb * y_b).astype(BF16)
        mb_ref[...] = mb
        o = jnp.dot(mb, wo_ref[...], preferred_element_type=F32)
        x2 = x_ref[...] + gate_ref[...] * o
        r2 = lax.rsqrt(_rowmean(x2 * x2) + EPS)
        xh = x2 * r2
        e = xh * gf_ref[...] - t_ref[...]
        dy = e * (1.0 / D)
        dxh = dy * gf_ref[...]
        dx2 = r2 * (dxh - xh * _rowmean(dxh * xh))
        dx2_ref[...] = dx2
        sums_ref[pl.ds(0, 1), :] += jnp.sum(dy * xh, axis=0, keepdims=True)
        sums_ref[pl.ds(1, 1), :] += jnp.sum(dx2 * o, axis=0, keepdims=True)
        sums_ref[pl.ds(2, 1), :] += jnp.sum(e * e, axis=0, keepdims=True) * (0.5 / D)
        dob = (gate_ref[...] * dx2).astype(BF16)
        dob_ref[...] = dob
        dm = _dot_t(dob, wo_ref[...])
        dy_a = (ga * dm).astype(BF16)
        dy_b = (gb * dm).astype(BF16)
        dyab_ref[...] = dy_a
        dybb_ref[...] = dy_b
        dp_ref[:, pl.ds(0, D)] = (dm * y_a * ga * (1.0 - ga)).astype(BF16)
        dp_ref[:, pl.ds(D, D)] = (dm * y_b * gb * (1.0 - gb)).astype(BF16)
        dya_ref[...] = _dot_t(dy_a, wco_ref[...])
        dyb_ref[...] = _dot_t(dy_b, wso_ref[...])

    tile = pl.BlockSpec((tm, D), lambda i: (i, 0))
    wspec = pl.BlockSpec((D, D), lambda i: (0, 0))
    return pl.pallas_call(
        body, name="out_proj", grid=(n_i,),
        in_specs=[pl.BlockSpec((tm, 2 * D), lambda i: (i, 3)), tile, tile, tile, tile, _vec_spec(1), _vec_spec(1),
                  wspec, wspec, wspec],
        out_specs=[tile, tile, tile, pl.BlockSpec((tm, 2 * D), lambda i: (i, 3)), tile, tile, tile, tile,
                   pl.BlockSpec((SUB, D), lambda i: (0, 0))],
        out_shape=[_sds((s_len, D), F32), _sds((s_len, D), F32), _sds((s_len, D), F32), _sds((s_len, N_SEC * D), BF16),
                   _sds((s_len, D), BF16), _sds((s_len, D), BF16), _sds((s_len, D), BF16), _sds((s_len, D), BF16),
                   _sds((SUB, D), F32)],
        compiler_params=_params(("arbitrary",)),
    )(p, ya_in, yb_in, x, target, gate, g_final, w_co, w_so, w_o)


A_STATS_ROWS = 8 + HALO


def _branch_a_bwd(p, p_halo_src, y1, dya_in, dp, conv_wb, ln_g, ln_b):
    s_len = p.shape[0]
    tm = min(256, s_len)
    n_i = s_len // tm
    rows = 32
    halo_blocks = tm // HALO

    def tile_of(i):
        return n_i - 1 - i

    def body(p_ref, ph_ref, y1_ref, dya_ref, dp_in, wb_ref, g_ref, b_ref, dp_ref, st_ref, abuf, dybuf, acc8, tapacc):
        del dp_in
        i = pl.program_id(0)
        first_tile = tile_of(i) == 0

        @pl.when(i == 0)
        def _():
            dybuf[pl.ds(tm, HALO), :] = jnp.zeros((HALO, D), F32)
            st_ref[...] = jnp.zeros((A_STATS_ROWS, D), F32)
            acc8[...] = jnp.zeros((3 * PACK, D), F32)
            tapacc[...] = jnp.zeros((CONV_K * SUB, D), F32)

        def glu(r0):
            val = p_ref[pl.ds(r0, PACK), pl.ds(0, D)].astype(F32)
            gl = p_ref[pl.ds(r0, PACK), pl.ds(D, D)].astype(F32)
            abuf[pl.ds(HALO + r0, PACK), :] = val * _sigmoid(gl)
        _strips(tm, PACK,glu)

        def glu_halo(r0):
            val = ph_ref[pl.ds(r0, PACK), pl.ds(0, D)].astype(F32)
            gl = ph_ref[pl.ds(r0, PACK), pl.ds(D, D)].astype(F32)
            abuf[pl.ds(r0, PACK), :] = jnp.where(first_tile, 0.0, val * _sigmoid(gl))
        _strips(HALO, PACK,glu_halo)

        def norm_bwd(r0):
            y1 = y1_ref[pl.ds(r0, PACK), :]
            mu = _rowmean(y1)
            yc = y1 - mu
            rstd = lax.rsqrt(_rowmean(yc * yc) + EPS)
            n1 = yc * rstd
            l1 = n1 * g_ref[...] + b_ref[...]
            sg = _sigmoid(l1)
            z = p_ref[pl.ds(r0, PACK), pl.ds(2 * D, D)].astype(F32)
            sz = _sigmoid(z)
            dya = dya_ref[pl.ds(r0, PACK), :]
            dp_ref[pl.ds(r0, PACK), pl.ds(2 * D, D)] = (dya * (l1 * sg) * _dsilu(z, sz)).astype(BF16)
            dl1 = dya * (z * sz) * _dsilu(l1, sg)
            acc8[pl.ds(0, PACK), :] += dl1 * n1
            acc8[pl.ds(PACK, PACK), :] += dl1
            dn1 = dl1 * g_ref[...]
            dy1 = rstd * (dn1 - _rowmean(dn1) - n1 * _rowmean(dn1 * n1))
            acc8[pl.ds(2 * PACK, PACK), :] += dy1
            dybuf[pl.ds(r0, PACK), :] = dy1
        _strips(tm, PACK,norm_bwd)

        taps_d = {CONV_K - 1 - k: (lambda lanes, k=k: wb_ref[pl.ds(k * SUB, SUB), lanes]) for k in range(CONV_K)}

        def conv_bwd_data(r0):
            for lt in range(D // LANE):
                lanes = pl.ds(lt * LANE, LANE)
                acc = _conv_taps(dybuf, r0, lt, taps_d, rows)
                for m in range(0, len(acc), PACK // SUB):
                    da = jnp.concatenate(acc[m:m + PACK // SUB], axis=0)
                    rr = pl.ds(r0 + m * SUB, PACK)
                    val = p_ref[rr, pl.ds(lt * LANE, LANE)].astype(F32)
                    sg = _sigmoid(p_ref[rr, pl.ds(D + lt * LANE, LANE)].astype(F32))
                    dp_ref[rr, lanes] = (da * sg).astype(BF16)
                    dp_ref[rr, pl.ds(D + lt * LANE, LANE)] = (da * val * sg * (1.0 - sg)).astype(BF16)
        _strips(tm, rows, conv_bwd_data)

        n_out = rows // SUB

        def conv_bwd_w(r0):
            for lt in range(D // LANE):
                lanes = pl.ds(lt * LANE, LANE)
                win = abuf[pl.ds(r0, rows + HALO), lanes]
                dy = [dybuf[pl.ds(r0 + m * SUB, SUB), lanes] for m in range(n_out)]
                for phase in range(SUB):
                    ks = [k for k in range(CONV_K) if (HALO - (CONV_K - 1) + k) % SUB == phase]
                    q_max = max((HALO - (CONV_K - 1) + k) // SUB for k in ks)
                    sh = win[phase:phase + (n_out + q_max) * SUB, :]
                    for k in ks:
                        q = (HALO - (CONV_K - 1) + k) // SUB
                        part = dy[0] * sh[q * SUB:(q + 1) * SUB, :]
                        for m in range(1, n_out):
                            part = part + dy[m] * sh[(m + q) * SUB:(m + q + 1) * SUB, :]
                        tapacc[pl.ds(k * SUB, SUB), lanes] += part
        _strips(tm, rows, conv_bwd_w)

        dybuf[pl.ds(tm, HALO), :] = dybuf[pl.ds(0, HALO), :]

        @pl.when(i == n_i - 1)
        def _():
            for j in range(3):
                st_ref[pl.ds(j, 1), :] = jnp.sum(acc8[pl.ds(j * PACK, PACK), :], axis=0, keepdims=True)
            for k in range(CONV_K):
                st_ref[pl.ds(SUB + k, 1), :] = jnp.sum(tapacc[pl.ds(k * SUB, SUB), :], axis=0, keepdims=True)

    return pl.pallas_call(
        body, name="branch_a_bwd", grid=(n_i,),
        in_specs=[pl.BlockSpec((tm, 3 * D), lambda i: (tile_of(i), 0)),
                  pl.BlockSpec((HALO, 2 * D), lambda i: (jnp.maximum(tile_of(i) * halo_blocks - 1, 0), 0)),
                  pl.BlockSpec((tm, D), lambda i: (tile_of(i), 0)),
                  pl.BlockSpec((tm, D), lambda i: (tile_of(i), 0)),
                  pl.BlockSpec(memory_space=pl.ANY),
                  pl.BlockSpec((CONV_K * SUB, D), lambda i: (0, 0)), _vec_spec(1), _vec_spec(1)],
        out_specs=[pl.BlockSpec((tm, 3 * D), lambda i: (tile_of(i), 0)),
                   pl.BlockSpec((A_STATS_ROWS, D), lambda i: (0, 0))],
        out_shape=[_sds(dp.shape, BF16), _sds((A_STATS_ROWS, D), F32)],
        scratch_shapes=[pltpu.VMEM((tm + HALO, D), F32), pltpu.VMEM((tm + HALO, D), F32), pltpu.VMEM((3 * PACK, D), F32),
                        pltpu.VMEM((CONV_K * SUB, D), F32)],
        input_output_aliases={4: 0},
        compiler_params=_params(("arbitrary",)),
    )(p, p_halo_src, y1, dya_in, dp, conv_wb, ln_g, ln_b)


def _branch_b_bwd(p, dyb_in, dp, wt, wtt, bias_full, ln_g, ln_b):
    s_len = p.shape[0]
    tm = min(256, s_len)
    n_i = s_len // tm

    def body(p_ref, dyb_ref, dp_in, wt_ref, wtt_ref, bias_ref, g_ref, b_ref, dp_ref, st_ref, gbt_ref, gw_ref,
             vb, n2buf, rstdbuf, sbuf, dsb, dvbuf, acc8, gb_ref, dgbuf):
        del dp_in
        i = pl.program_id(0)

        @pl.when(i == 0)
        def _():
            st_ref[...] = jnp.zeros((SUB, D), F32)
            gbt_ref[...] = jnp.zeros((CHUNK, LANE), F32)
            gb_ref[...] = jnp.zeros((CHUNK, D), F32)
            gw_ref[...] = jnp.zeros((HEADS, CHUNK, CHUNK), F32)
            acc8[...] = jnp.zeros((2 * PACK, D), F32)

        def norm(r0):
            gv, dgv = _gelu_and_grad(p_ref[pl.ds(r0, PACK), pl.ds(D, D)].astype(F32))
            dgbuf[pl.ds(r0, PACK), :] = dgv
            mu = _rowmean(gv)
            vc = gv - mu
            rstd = lax.rsqrt(_rowmean(vc * vc) + EPS)
            n2 = vc * rstd
            n2buf[pl.ds(r0, PACK), :] = n2
            rstdbuf[pl.ds(r0, PACK), :] = jnp.broadcast_to(rstd, (PACK, LANE))
            vb[pl.ds(r0, PACK), :] = (n2 * g_ref[...] + b_ref[...]).astype(BF16)
        _strips(tm, PACK,norm)

        for ck in range(tm // CHUNK):
            for h in range(HEADS):
                blk = (pl.ds(ck * CHUNK, CHUNK), pl.ds(h * LANE, LANE))
                sbuf[blk] = jnp.dot(wt_ref[h], vb[blk], preferred_element_type=F32) + bias_ref[:, pl.ds(h * LANE, LANE)]

        def gate_bwd(r0):
            pu = p_ref[pl.ds(r0, PACK), pl.ds(0, D)].astype(F32)
            u, du = _gelu_and_grad(pu)
            z = p_ref[pl.ds(r0, PACK), pl.ds(2 * D, D)].astype(F32)
            sg = _sigmoid(z)
            sz = z * sg
            s = sbuf[pl.ds(r0, PACK), :]
            dyb = dyb_ref[pl.ds(r0, PACK), :]
            ds = dyb * u * sz
            dsb[pl.ds(r0, PACK), :] = ds.astype(BF16)
            gb_ref[pl.ds(pl.multiple_of(r0 % CHUNK, PACK), PACK), :] += ds
            dp_ref[pl.ds(r0, PACK), pl.ds(0, D)] = (dyb * s * sz * du).astype(BF16)
            dp_ref[pl.ds(r0, PACK), pl.ds(2 * D, D)] = (dyb * u * s * _dsilu(z, sg)).astype(BF16)
        _strips(tm, PACK,gate_bwd)

        for ck in range(tm // CHUNK):
            for h in range(HEADS):
                blk = (pl.ds(ck * CHUNK, CHUNK), pl.ds(h * LANE, LANE))
                d_s = dsb[blk]
                dvbuf[blk] = jnp.dot(wtt_ref[h], d_s, preferred_element_type=F32)
                gw_ref[h] += _dot_t(d_s, vb[blk])

        def norm_bwd(r0):
            dv = dvbuf[pl.ds(r0, PACK), :]
            n2 = n2buf[pl.ds(r0, PACK), :]
            rstd = rstdbuf[pl.ds(r0, PACK), pl.ds(0, 1)]
            acc8[pl.ds(0, PACK), :] += dv * n2
            acc8[pl.ds(PACK, PACK), :] += dv
            dn2 = dv * g_ref[...]
            dgv = rstd * (dn2 - _rowmean(dn2) - n2 * _rowmean(dn2 * n2))
            dp_ref[pl.ds(r0, PACK), pl.ds(D, D)] = (dgv * dgbuf[pl.ds(r0, PACK), :]).astype(BF16)
        _strips(tm, PACK,norm_bwd)

        @pl.when(i == n_i - 1)
        def _():
            for j in range(2):
                st_ref[pl.ds(j, 1), :] = jnp.sum(acc8[pl.ds(j * PACK, PACK), :], axis=0, keepdims=True)
            row = lax.broadcasted_iota(jnp.int32, (CHUNK, CHUNK), 0)
            col = lax.broadcasted_iota(jnp.int32, (CHUNK, CHUNK), 1)
            for h in range(HEADS):
                gw_ref[h] = jnp.where(row >= col, gw_ref[h], 0.0)
            lane = lax.broadcasted_iota(jnp.int32, (CHUNK, LANE), 1)
            gbt = jnp.zeros((CHUNK, LANE), F32)
            for h in range(HEADS):
                gbt = jnp.where(lane == h, jnp.sum(gb_ref[:, pl.ds(h * LANE, LANE)], axis=1, keepdims=True), gbt)
            gbt_ref[...] = gbt

    wspec = pl.BlockSpec((HEADS, CHUNK, CHUNK), lambda i: (0, 0, 0))
    return pl.pallas_call(
        body, name="branch_b_bwd", grid=(n_i,),
        in_specs=[pl.BlockSpec((tm, 3 * D), lambda i: (i, 1)), pl.BlockSpec((tm, D), lambda i: (i, 0)),
                  pl.BlockSpec(memory_space=pl.ANY), wspec, wspec,
                  pl.BlockSpec((CHUNK, D), lambda i: (0, 0)), _vec_spec(1), _vec_spec(1)],
        out_specs=[pl.BlockSpec((tm, 3 * D), lambda i: (i, 1)), pl.BlockSpec((SUB, D), lambda i: (0, 0)),
                   pl.BlockSpec((CHUNK, LANE), lambda i: (0, 0)), wspec],
        out_shape=[_sds(dp.shape, BF16), _sds((SUB, D), F32), _sds((CHUNK, LANE), F32), _sds((HEADS, CHUNK, CHUNK), F32)],
        scratch_shapes=[pltpu.VMEM((tm, D), BF16), pltpu.VMEM((tm, D), F32), pltpu.VMEM((tm, LANE), F32),
                        pltpu.VMEM((tm, D), F32), pltpu.VMEM((tm, D), BF16), pltpu.VMEM((tm, D), F32),
                        pltpu.VMEM((2 * PACK, D), F32), pltpu.VMEM((CHUNK, D), F32), pltpu.VMEM((tm, D), F32)],
        input_output_aliases={2: 0},
        compiler_params=_params(("arbitrary",)),
    )(p, dyb_in, dp, wt, wtt, bias_full, ln_g, ln_b)


def _in_proj_bwd(dp, wg_in, x, dx2, shift, scale, g_pre):
    del shift
    s_len = x.shape[0]
    tm = min(512, s_len)
    n_i = s_len // tm
    wn = wg_in.shape[2]

    def body(dp_ref, w_ref, x_ref, dx2_ref, sc_ref, g_ref, gx_ref, st_ref, acc, acc8):
        i = pl.program_id(0)

        @pl.when(i == 0)
        def _():
            st_ref[...] = jnp.zeros((SUB, D), F32)
            acc8[...] = jnp.zeros((3 * PACK, D), F32)

        dh = _dot_t(dp_ref[:, pl.ds(0, wn)], w_ref[0])
        for j in range(1, N_CHIP):
            dh = dh + _dot_t(dp_ref[:, pl.ds(j * wn, wn)], w_ref[j])
        acc[...] = dh

        def strip(r0):
            xs = x_ref[pl.ds(r0, PACK), :]
            r = lax.rsqrt(_rowmean(xs * xs) + EPS)
            xn = xs * r
            dhs = acc[pl.ds(r0, PACK), :]
            acc8[pl.ds(0, PACK), :] += dhs
            acc8[pl.ds(PACK, PACK), :] += dhs * (xn * g_ref[...])
            dhp = dhs * (1.0 + sc_ref[...])
            acc8[pl.ds(2 * PACK, PACK), :] += dhp * xn
            dxn = dhp * g_ref[...]
            gx_ref[pl.ds(r0, PACK), :] = dx2_ref[pl.ds(r0, PACK), :] + r * (dxn - xn * _rowmean(dxn * xn))
        _strips(tm, PACK, strip)

        @pl.when(i == n_i - 1)
        def _():
            for k in range(3):
                st_ref[pl.ds(k, 1), :] = jnp.sum(acc8[pl.ds(k * PACK, PACK), :], axis=0, keepdims=True)

    tile = pl.BlockSpec((tm, D), lambda i: (i, 0))
    return pl.pallas_call(
        body, name="in_proj_bwd", grid=(n_i,),
        in_specs=[pl.BlockSpec((tm, N_CHIP * wn), lambda i: (i, 0)),
                  pl.BlockSpec((N_CHIP, D, wn), lambda i: (0, 0, 0), pipeline_mode=pl.Buffered(1)),
                  tile, tile, _vec_spec(1), _vec_spec(1)],
        out_specs=[tile, pl.BlockSpec((SUB, D), lambda i: (0, 0))],
        out_shape=[_sds((s_len, D), F32), _sds((SUB, D), F32)],
        scratch_shapes=[pltpu.VMEM((tm, D), F32), pltpu.VMEM((3 * PACK, D), F32)],
        compiler_params=_params(("arbitrary",)),
    )(dp, wg_in, x, dx2, scale, g_pre)


def _grad_matmul(a, b, name):
    s_len, n = b.shape
    cb = min(2 * D, n)
    tn = 512
    per = cb // tn

    def body(a_ref, b_ref, ob_ref):
        ob_ref[0] = lax.dot_general(a_ref[...], b_ref[...], (((0,), (0,)), ((), ())),
                                    preferred_element_type=F32).astype(BF16)

    return pl.pallas_call(
        body, name=name, grid=(n // tn,),
        in_specs=[pl.BlockSpec((s_len, D), lambda j: (0, 0), pipeline_mode=pl.Buffered(1)),
                  pl.BlockSpec((s_len, tn), lambda j: (0, j))],
        out_specs=pl.BlockSpec((1, D, tn), lambda j: (j // per, 0, j % per)),
        out_shape=_sds((n // cb, D, cb), BF16),
        compiler_params=_params(("arbitrary",)),
    )(a, b)


def _local_step(x, target, shift, scale, gate, g_pre, conv_w_full, conv_b, conv_ln_g, conv_ln_b,
                sgu_ln_g, sgu_ln_b, w_sgu, b_sgu, g_final, wg_in, w_co, w_so, w_o):
    conv_wb = jnp.repeat(conv_w_full, SUB, axis=0)
    causal = jnp.tril(jnp.ones((CHUNK, CHUNK), dtype=bool))
    wt = jnp.where(causal[None], w_sgu, 0.0).astype(BF16)
    wtt = jnp.swapaxes(wt, 1, 2)
    bias_full = jnp.repeat(b_sgu.T, LANE, axis=1)

    p, hb = _in_proj(x, shift, scale, g_pre, wg_in)
    ya_in, y1 = _branch_a_fwd(p, conv_wb, conv_b, conv_ln_g, conv_ln_b)
    yb_in = _branch_b_fwd(p, wt, bias_full, sgu_ln_g, sgu_ln_b)
    dx2, dya_in, dyb_in, dp, mb, dob, dyab, dybb, sums_o = _out_proj(
        p, ya_in, yb_in, x, target, gate, g_final, w_co, w_so, w_o)
    dp, st_a = _branch_a_bwd(p, p, y1, dya_in, dp, conv_wb, conv_ln_g, conv_ln_b)
    dp, st_b, gbt, gws = _branch_b_bwd(p, dyb_in, dp, wt, wtt, bias_full, sgu_ln_g, sgu_ln_b)
    grad_x, st_i = _in_proj_bwd(dp, wg_in, x, dx2, shift, scale, g_pre)
    gw_in = _grad_matmul(hb, dp, "grad_w_in")
    gw_o = _grad_matmul(mb, dob, "grad_w_o")
    gw_co = _grad_matmul(ya_in, dyab, "grad_w_conv_out")
    gw_so = _grad_matmul(yb_in, dybb, "grad_w_sgu_out")
    return dict(
        grad_x=grad_x, loss_cols=sums_o[2:3], g_final=sums_o[0:1], d_gate=sums_o[1:2],
        d_shift=st_i[0:1], d_scale=st_i[1:2], g_pre=st_i[2:3],
        conv_ln_g=st_a[0:1], conv_ln_b=st_a[1:2], conv_b=st_a[2:3], conv_w=st_a[SUB:SUB + CONV_K],
        sgu_ln_g=st_b[0:1], sgu_ln_b=st_b[1:2], b_sgu=gbt[:, :HEADS].T, w_sgu=gws,
        w_in=gw_in, w_o=gw_o, w_conv_out=gw_co, w_sgu_out=gw_so)


ANY_SPEC = pl.BlockSpec(memory_space=pl.ANY)
VMEM_SPEC = pl.BlockSpec(memory_space=pltpu.VMEM)


def _place():
    return lax.axis_index("x"), lax.axis_index("y"), lax.axis_index("c")


def _peer(k):
    x, y, c = _place()
    return (1 - x if k & 4 else x, 1 - y if k & 2 else y, 1 - c if k & 1 else c)


def _dev_of(p):
    return 4 * p[0] + 2 * p[1] + p[2]


def _chip_of(p):
    return 2 * p[0] + p[1]


def _rdma(src, dst, send_sem, recv_sem, to):
    return pltpu.make_async_remote_copy(src_ref=src, dst_ref=dst, send_sem=send_sem, recv_sem=recv_sem,
                                        device_id=to, device_id_type=MESH)


CHIP_PEERS = (2, 4, 6)
ALL_PEERS = tuple(range(1, N_DEV))
SIBLING = 1


def _setup_comm(c8, w_ada_s, b_ada_s, convw_s):
    n_mod = w_ada_s.shape[1]
    rows = SUB * N_DEV

    def body(c8_ref, wada_ref, bada_ref, cw_ref, call_ref, mod_ref, cwall_ref, csend, crecv, wsend, wrecv, msend, mrecv):
        me = _place()
        dev, chip = _dev_of(me), _chip_of(me)

        def c_rows(d):
            return call_ref.at[pl.ds(pl.multiple_of(d * SUB, SUB), SUB), :]

        call_ref[pl.ds(pl.multiple_of(dev * SUB, SUB), SUB), :] = c8_ref[...]
        cwall_ref[chip] = cw_ref[...]
        c_out = [_rdma(c8_ref, c_rows(dev), csend.at[k], crecv.at[k], _peer(k)) for k in ALL_PEERS]
        w_out = [_rdma(cw_ref, cwall_ref.at[chip], wsend.at[k], wrecv.at[k], _peer(k)) for k in CHIP_PEERS]
        for cp in c_out + w_out:
            cp.start()
        for k in ALL_PEERS:
            _rdma(c8_ref, c_rows(_dev_of(_peer(k))), csend.at[k], crecv.at[k], _peer(k)).wait_recv()
        part = jnp.dot(call_ref[...].astype(BF16), wada_ref[...].astype(BF16), preferred_element_type=F32) + bada_ref[...]
        mod_ref[chip] = part
        m_out = [_rdma(mod_ref.at[chip], mod_ref.at[chip], msend.at[k], mrecv.at[k], _peer(k)) for k in CHIP_PEERS]
        for cp in m_out:
            cp.start()
        for k in CHIP_PEERS:
            pc = _chip_of(_peer(k))
            _rdma(cw_ref, cwall_ref.at[pc], wsend.at[k], wrecv.at[k], _peer(k)).wait_recv()
            _rdma(mod_ref.at[pc], mod_ref.at[pc], msend.at[k], mrecv.at[k], _peer(k)).wait_recv()
        for cp in c_out + w_out + m_out:
            cp.wait_send()

    return pl.pallas_call(
        body, name="setup_comm",
        in_specs=[VMEM_SPEC] * 4, out_specs=[VMEM_SPEC] * 3,
        out_shape=[_sds((rows, D), F32), _sds((N_CHIP, rows, n_mod), F32), _sds((N_CHIP,) + convw_s.shape, F32)],
        scratch_shapes=[pltpu.SemaphoreType.DMA((N_DEV,))] * 6,
        compiler_params=_params(),
    )(c8, w_ada_s, b_ada_s, convw_s)


def _gather_weights(shards):
    n = len(shards)

    def body(*refs):
        ins, outs = refs[:n], refs[n:2 * n]
        lsem, isend, irecv, dsend, drecv = refs[2 * n:]
        me = _place()
        chip, c = _chip_of(me), me[2]
        local = [pltpu.make_async_copy(ins[t], outs[t].at[chip], lsem.at[t]) for t in range(n)]
        for cp in local:
            cp.start()

        def half(t, which):
            hr = shards[t].shape[0] // 2
            return pl.ds(pl.multiple_of(which * hr, hr), hr)

        sends = []
        for t in range(n):
            for j, k in enumerate(CHIP_PEERS):
                cp = _rdma(ins[t].at[half(t, c)], outs[t].at[chip, half(t, c)], isend.at[t, j], irecv.at[t, j], _peer(k))
                cp.start()
                sends.append(cp)
        for t in range(n):
            for j, k in enumerate(CHIP_PEERS):
                blk = outs[t].at[_chip_of(_peer(k)), half(t, c)]
                _rdma(blk, blk, isend.at[t, j], irecv.at[t, j], _peer(k)).wait_recv()
                cp = _rdma(blk, blk, dsend.at[t, j], drecv.at[t, j], _peer(SIBLING))
                cp.start()
                sends.append(cp)
        for t in range(n):
            for j, k in enumerate(CHIP_PEERS):
                blk = outs[t].at[_chip_of(_peer(k)), half(t, 1 - c)]
                _rdma(blk, blk, dsend.at[t, j], drecv.at[t, j], _peer(SIBLING)).wait_recv()
        for cp in sends:
            cp.wait_send()
        for cp in local:
            cp.wait()

    return pl.pallas_call(
        body, name="gather_weights",
        in_specs=[VMEM_SPEC] * n, out_specs=[VMEM_SPEC] * n,
        out_shape=[_sds((N_CHIP,) + s.shape, s.dtype) for s in shards],
        scratch_shapes=[pltpu.SemaphoreType.DMA((n,))] + [pltpu.SemaphoreType.DMA((n, len(CHIP_PEERS)))] * 4,
        compiler_params=_params(),
    )(*shards)


def _reduce_scatter(grads, name):
    n = len(grads)
    shapes = [g.shape[2:] for g in grads]

    def body(*refs):
        ins, outs = refs[:n], refs[n:2 * n]
        pbufs, rbufs, accs = refs[2 * n:3 * n], refs[3 * n:4 * n], refs[4 * n:5 * n]
        psend, precv, csend, crecv, fsend, frecv = refs[5 * n:]
        me = _place()
        chip, c = _chip_of(me), me[2]
        sib = _peer(SIBLING)

        def to_sibling(t, d):
            return _rdma(ins[t].at[d, 1 - c], pbufs[t].at[d], psend.at[t, d], precv.at[t, d], sib)

        sends = []
        for t in range(n):
            for d in range(N_CHIP):
                cp = to_sibling(t, d)
                cp.start()
                sends.append(cp)
        for j in (1, 2, 3, 0):
            d = jnp.bitwise_xor(chip, j)
            for t in range(n):
                to_sibling(t, d).wait_recv()

                def pair_sum(r0, t=t, d=d, j=j):
                    rows = pl.ds(r0, PACK)
                    s = ins[t][d, c, rows, :].astype(F32) + pbufs[t][d, rows, :].astype(F32)
                    if j == 0:
                        accs[t][rows, :] = s
                    else:
                        pbufs[t][d, rows, :] = s.astype(BF16)
                _strips(shapes[t][0], PACK, pair_sum)
                if j:
                    cp = _rdma(pbufs[t].at[d], rbufs[t].at[j - 1], csend.at[t, j], crecv.at[t, j], _peer(2 * j))
                    cp.start()
                    sends.append(cp)
        for t in range(n):
            for j in (1, 2, 3):
                blk = rbufs[t].at[j - 1]
                _rdma(blk, blk, csend.at[t, j], crecv.at[t, j], _peer(2 * j)).wait_recv()

            def total(r0, t=t):
                rows = pl.ds(r0, PACK)
                s = accs[t][rows, :] + rbufs[t][0, rows, :].astype(F32)
                s = s + rbufs[t][1, rows, :].astype(F32)
                outs[t][c, rows, :] = s + rbufs[t][2, rows, :].astype(F32)
            _strips(shapes[t][0], PACK, total)
            cp = _rdma(outs[t].at[c], outs[t].at[c], fsend.at[t], frecv.at[t], sib)
            cp.start()
            sends.append(cp)
        for t in range(n):
            blk = outs[t].at[1 - c]
            _rdma(blk, blk, fsend.at[t], frecv.at[t], sib).wait_recv()
        for cp in sends:
            cp.wait_send()

    return pl.pallas_call(
        body, name=name,
        in_specs=[VMEM_SPEC] * n, out_specs=[VMEM_SPEC] * n,
        out_shape=[_sds((2,) + s, F32) for s in shapes],
        scratch_shapes=([pltpu.VMEM((N_CHIP,) + s, BF16) for s in shapes] + [pltpu.VMEM((N_CHIP - 1,) + s, BF16) for s in shapes]
                        + [pltpu.VMEM(s, F32) for s in shapes]
                        + [pltpu.SemaphoreType.DMA((n, N_CHIP))] * 4 + [pltpu.SemaphoreType.DMA((n,))] * 2),
        compiler_params=_params(),
    )(*grads)


def _scatter_grads(grads):
    n = len(grads)

    def body(*refs):
        ins, outs = refs[:n], refs[n:2 * n]
        lsem, send, recv = refs[2 * n:]
        me = _place()
        dev, chip, c = _dev_of(me), _chip_of(me), me[2]
        local = [pltpu.make_async_copy(ins[t].at[chip, c], outs[t].at[dev], lsem.at[t]) for t in range(n)]
        for cp in local:
            cp.start()
        sends = []
        for t in range(n):
            for k in ALL_PEERS:
                to = _peer(k)
                cp = _rdma(ins[t].at[_chip_of(to), to[2]], outs[t].at[dev], send.at[t, k], recv.at[t, k], to)
                cp.start()
                sends.append(cp)
        for t in range(n):
            for k in ALL_PEERS:
                blk = outs[t].at[_dev_of(_peer(k))]
                _rdma(blk, blk, send.at[t, k], recv.at[t, k], _peer(k)).wait_recv()
        for cp in sends:
            cp.wait_send()
        for cp in local:
            cp.wait()

    return pl.pallas_call(
        body, name="scatter_grads",
        in_specs=[ANY_SPEC] * n, out_specs=[ANY_SPEC] * n,
        out_shape=[_sds((N_DEV,) + g.shape[2:], g.dtype) for g in grads],
        scratch_shapes=[pltpu.SemaphoreType.DMA((n,))] + [pltpu.SemaphoreType.DMA((n, N_DEV))] * 2,
        compiler_params=_params(),
    )(*grads)


def _sum_devices(parts, name):
    _, r, cols = parts.shape
    tr = min(r, 128)

    def body(in_ref, o_ref):
        acc = in_ref[0].astype(F32)
        for d in range(1, N_DEV):
            acc = acc + in_ref[d].astype(F32)
        o_ref[...] = acc

    return pl.pallas_call(
        body, name=name, grid=(r // tr,),
        in_specs=[pl.BlockSpec((N_DEV, tr, cols), lambda i: (0, i, 0))],
        out_specs=pl.BlockSpec((tr, cols), lambda i: (i, 0)),
        out_shape=_sds((r, cols), F32),
        compiler_params=_params(("arbitrary",)),
    )(parts)


def _share_halves(reds):
    n = len(reds)

    def body(*refs):
        ins, outs = refs[:n], refs[n:2 * n]
        lsem, send, recv = refs[2 * n:]
        me = _place()
        c = me[2]
        local = [pltpu.make_async_copy(ins[t], outs[t].at[c], lsem.at[t]) for t in range(n)]
        sends = [_rdma(ins[t], outs[t].at[c], send.at[t], recv.at[t], _peer(SIBLING)) for t in range(n)]
        for cp in local + sends:
            cp.start()
        for t in range(n):
            _rdma(ins[t], outs[t].at[1 - c], send.at[t], recv.at[t], _peer(SIBLING)).wait_recv()
        for cp in sends:
            cp.wait_send()
        for cp in local:
            cp.wait()

    return pl.pallas_call(
        body, name="share_halves",
        in_specs=[VMEM_SPEC] * n, out_specs=[VMEM_SPEC] * n,
        out_shape=[_sds((2,) + r.shape, r.dtype) for r in reds],
        scratch_shapes=[pltpu.SemaphoreType.DMA((n,))] * 3,
        compiler_params=_params(),
    )(*reds)


def _sum_small(blobs):
    n = len(blobs)

    def body(*refs):
        ins, outs = refs[:n], refs[n:2 * n]
        pbufs, buf4s = refs[2 * n:3 * n], refs[3 * n:4 * n]
        psend, precv, send, recv = refs[4 * n:]
        me = _place()
        chip = _chip_of(me)
        pairs = [_rdma(ins[t], pbufs[t], psend.at[t], precv.at[t], _peer(SIBLING)) for t in range(n)]
        for cp in pairs:
            cp.start()
        out = []
        for t in range(n):
            pairs[t].wait()
            buf4s[t][chip] = ins[t][...] + pbufs[t][...]
            for k in CHIP_PEERS:
                cp = _rdma(buf4s[t].at[chip], buf4s[t].at[chip], send.at[t, k], recv.at[t, k], _peer(k))
                cp.start()
                out.append(cp)
        for t in range(n):
            for k in CHIP_PEERS:
                blk = buf4s[t].at[_chip_of(_peer(k))]
                _rdma(blk, blk, send.at[t, k], recv.at[t, k], _peer(k)).wait_recv()
            outs[t][...] = (buf4s[t][0] + buf4s[t][1]) + (buf4s[t][2] + buf4s[t][3])
        for cp in out:
            cp.wait_send()

    return pl.pallas_call(
        body, name="sum_small",
        in_specs=[VMEM_SPEC] * n, out_specs=[VMEM_SPEC] * n, out_shape=[_sds(b.shape, F32) for b in blobs],
        scratch_shapes=([pltpu.VMEM(b.shape, F32) for b in blobs] + [pltpu.VMEM((N_CHIP,) + b.shape, F32) for b in blobs]
                        + [pltpu.SemaphoreType.DMA((n,))] * 2 + [pltpu.SemaphoreType.DMA((n, N_DEV))] * 2),
        compiler_params=_params(),
    )(*blobs)


def _adamw_math(w, g, m, v):
    m = ADAM_B1 * m + (1.0 - ADAM_B1) * g
    v = ADAM_B2 * v + (1.0 - ADAM_B2) * (g * g)
    m_hat = m / (1.0 - ADAM_B1 ** ADAM_STEP)
    v_hat = v / (1.0 - ADAM_B2 ** ADAM_STEP)
    delta = -ADAM_LR * (m_hat / (jnp.sqrt(v_hat) + ADAM_EPS) + ADAM_WD * w)
    return delta, m, v


def _row_tile(r, cols):
    if r * cols * 4 <= 2 ** 20:
        return r
    return next(t for t in (512, 256, 128, 64, 32, 16, 8) if r % t == 0 and t * cols * 4 <= 2 ** 20)


def _adamw(w, g, m, v, name):
    r, cols = w.shape
    tr = _row_tile(r, cols)

    def body(w_ref, g_ref, m_ref, v_ref, d_ref, nm_ref, nv_ref):
        d_ref[...], nm_ref[...], nv_ref[...] = _adamw_math(w_ref[...], g_ref[...], m_ref[...], v_ref[...])

    spec = pl.BlockSpec((tr, cols), lambda i: (i, 0))
    return pl.pallas_call(
        body, name=name, grid=(r // tr,), in_specs=[spec] * 4, out_specs=[spec] * 3,
        out_shape=[_sds((r, cols), F32)] * 3, compiler_params=_params(("arbitrary",)),
    )(w, g, m, v)


def _adamw_ada(w, ct, dm, m, v):
    r, cols = w.shape
    tr = _row_tile(r, cols)

    def body(w_ref, ct_ref, dm_ref, m_ref, v_ref, g_ref, d_ref, nm_ref, nv_ref):
        g = jnp.dot(ct_ref[...], dm_ref[...], preferred_element_type=F32)
        g_ref[...] = g
        d_ref[...], nm_ref[...], nv_ref[...] = _adamw_math(w_ref[...], g, m_ref[...], v_ref[...])

    spec = pl.BlockSpec((tr, cols), lambda i: (i, 0))
    return pl.pallas_call(
        body, name="adamw_ada", grid=(r // tr,),
        in_specs=[spec, pl.BlockSpec((tr, LANE), lambda i: (i, 0)), pl.BlockSpec((LANE, cols), lambda i: (0, 0)), spec, spec],
        out_specs=[spec] * 4, out_shape=[_sds((r, cols), F32)] * 4, compiler_params=_params(("arbitrary",)),
    )(w, ct, dm, m, v)


BLOB_VEC, BLOB_BSGU, BLOB_CONV, BLOB_ADA, BLOB_DMOD, BLOB_LOSS, BLOB_ROWS = 0, 8, 16, 48, 56, 80, 88
N_VEC = 7


def _adamw_small(tot, g_w_sgu, g_conv, params):
    n = len(params)

    def body(*refs):
        tot_ref, gws_ref, gconv_ref = refs[:3]
        wmv = refs[3:3 + 3 * n]
        outs = refs[3 + 3 * n:]
        grads = [tot_ref[pl.ds(BLOB_VEC + i, 1), :] for i in range(N_VEC)]
        grads += [tot_ref[pl.ds(BLOB_BSGU, HEADS), pl.ds(0, CHUNK)], gconv_ref[...], gws_ref[...], tot_ref[pl.ds(BLOB_ADA, 3), :]]
        for i, g in enumerate(grads):
            w_ref, m_ref, v_ref = wmv[3 * i:3 * i + 3]
            d, nm, nv = _adamw_math(w_ref[...], g, m_ref[...], v_ref[...])
            outs[4 * i][...] = g
            outs[4 * i + 1][...] = d
            outs[4 * i + 2][...] = nm
            outs[4 * i + 3][...] = nv

    flat = [a for wmv in params for a in wmv]
    return pl.pallas_call(
        body, name="adamw_small",
        in_specs=[VMEM_SPEC] * (3 + len(flat)), out_specs=[VMEM_SPEC] * (4 * n),
        out_shape=[_sds(wmv[0].shape, F32) for wmv in params for _ in range(4)],
        compiler_params=_params(),
    )(tot, g_w_sgu, g_conv, *flat)


def _set_rows(buf, row, val):
    return lax.dynamic_update_slice(buf, val.astype(F32), (row, 0))


def kernel(x, c, w_ada, b_ada, g_pre, w_in, conv_w, conv_b, conv_ln_g, conv_ln_b, w_conv_out, sgu_ln_g, sgu_ln_b, w_sgu, b_sgu, w_sgu_out, w_o, g_final, loss_target, m_w_ada, m_b_ada, m_g_pre, m_w_in, m_conv_w, m_conv_b, m_conv_ln_g, m_conv_ln_b, m_w_conv_out, m_sgu_ln_g, m_sgu_ln_b, m_w_sgu, m_b_sgu, m_w_sgu_out, m_w_o, m_g_final, v_w_ada, v_b_ada, v_g_pre, v_w_in, v_conv_w, v_conv_b, v_conv_ln_g, v_conv_ln_b, v_w_conv_out, v_sgu_ln_g, v_sgu_ln_b, v_w_sgu, v_b_sgu, v_w_sgu_out, v_w_o, v_g_final):
    me = _place()
    dev, chip = _dev_of(me), _chip_of(me)
    n_ada = w_ada.shape[2]
    conv_cols = conv_w.shape[2]

    b_ada_s = lax.dynamic_slice(b_ada, (0, chip * n_ada), (1, n_ada))
    c_all, mod_all, cw_all = _setup_comm(
        jnp.broadcast_to(c, (SUB, D)), w_ada[0], b_ada_s, jnp.pad(conv_w[0], ((0, HALO - CONV_K), (0, 0))))
    mod = lax.dynamic_slice(mod_all, (0, dev * SUB, 0), (N_CHIP, 1, n_ada)).reshape(1, 3 * D)
    shift, scale, gate = mod[:, :D], mod[:, D:2 * D], mod[:, 2 * D:]
    conv_w_full = jnp.swapaxes(cw_all, 0, 1).reshape(HALO, D)[:CONV_K]

    wg_in, wg_co, wg_so, wg_o = _gather_weights(
        [w_in[0].astype(BF16), w_conv_out[0].astype(BF16), w_sgu_out[0].astype(BF16), w_o[0].astype(BF16)])

    loc = _local_step(x[0], loss_target[0], shift, scale, gate, g_pre, conv_w_full, conv_b, conv_ln_g, conv_ln_b,
                      sgu_ln_g, sgu_ln_b, w_sgu[0], b_sgu[0], g_final.reshape(1, D),
                      wg_in, wg_co.reshape(D, D), wg_so.reshape(D, D), wg_o.reshape(D, D))

    big = ["w_in", "w_conv_out", "w_sgu_out", "w_o"]
    contrib = []
    for name in big:
        g16 = loc[name]
        rows_half = (g16.shape[0] * g16.shape[1]) // (2 * N_CHIP) if name != "w_in" else g16.shape[1] // 2
        contrib.append(g16.reshape(N_CHIP, 2, rows_half, g16.shape[2]))
    full = _reduce_scatter(contrib[:1], "reduce_w_in") + _reduce_scatter(contrib[1:], "reduce_w_out")
    g_big = {name: f.reshape(2 * f.shape[1], f.shape[2]) for name, f in zip(big, full)}

    d_mod = jnp.concatenate([loc["d_shift"], loc["d_scale"], loc["d_gate"]], axis=0)
    blob = jnp.zeros((BLOB_ROWS, D), F32)
    for i, name in enumerate(["g_pre", "conv_b", "conv_ln_g", "conv_ln_b", "sgu_ln_g", "sgu_ln_b", "g_final"]):
        blob = _set_rows(blob, BLOB_VEC + i, loc[name])
    blob = _set_rows(blob, BLOB_BSGU, loc["b_sgu"])
    blob = _set_rows(blob, BLOB_CONV, loc["conv_w"])
    blob = _set_rows(blob, BLOB_ADA, d_mod)
    blob = lax.dynamic_update_slice(blob, d_mod, (BLOB_DMOD + 3 * dev, 0))
    blob = _set_rows(blob, BLOB_LOSS, loc["loss_cols"])
    tot, g_w_sgu = _sum_small([blob, loc["w_sgu"].reshape(HEADS * CHUNK, CHUNK)])

    loss = jnp.sum(tot[BLOB_LOSS])
    g_conv_s = lax.dynamic_slice(tot, (BLOB_CONV, chip * conv_cols), (CONV_K, conv_cols))
    d_mod_all = tot[BLOB_DMOD:BLOB_DMOD + 3 * N_DEV].reshape(N_DEV, 3 * D)

    ct = jnp.pad(c_all[::SUB].T, ((0, 0), (0, LANE - N_DEV))).astype(BF16)
    dm = jnp.pad(lax.dynamic_slice(d_mod_all, (0, chip * n_ada), (N_DEV, n_ada)), ((0, LANE - N_DEV), (0, 0))).astype(BF16)
    g_ada, d_ada, nm_ada, nv_ada = _adamw_ada(w_ada[0], ct, dm, m_w_ada[0], v_w_ada[0])

    upd = {}
    for name, w, m, v in [("w_in", w_in, m_w_in, v_w_in), ("w_conv_out", w_conv_out, m_w_conv_out, v_w_conv_out),
                          ("w_sgu_out", w_sgu_out, m_w_sgu_out, v_w_sgu_out), ("w_o", w_o, m_w_o, v_w_o)]:
        upd[name] = _adamw(w[0], g_big[name], m[0], v[0], "adamw_" + name)

    def wmv(w, m, v, shape):
        return tuple(a.reshape(shape) for a in (w, m, v))

    small_params = [wmv(w, m, v, (1, D)) for w, m, v in [
        (g_pre, m_g_pre, v_g_pre), (conv_b, m_conv_b, v_conv_b), (conv_ln_g, m_conv_ln_g, v_conv_ln_g),
        (conv_ln_b, m_conv_ln_b, v_conv_ln_b), (sgu_ln_g, m_sgu_ln_g, v_sgu_ln_g), (sgu_ln_b, m_sgu_ln_b, v_sgu_ln_b),
        (g_final, m_g_final, v_g_final)]]
    small_params += [wmv(b_sgu, m_b_sgu, v_b_sgu, (HEADS, CHUNK)), wmv(conv_w, m_conv_w, v_conv_w, (CONV_K, conv_cols)),
                     wmv(w_sgu, m_w_sgu, v_w_sgu, (HEADS * CHUNK, CHUNK)), wmv(b_ada, m_b_ada, v_b_ada, (3, D))]
    small_out = _adamw_small(tot, g_w_sgu, g_conv_s, small_params)

    def leaves(kind):
        vecs = [small_out[4 * i + kind] for i in range(N_VEC)]
        o_b_sgu, o_conv, o_w_sgu, o_b_ada = (small_out[4 * (N_VEC + i) + kind] for i in range(4))
        ada = (g_ada, d_ada, nm_ada, nv_ada)[kind]
        def bigk(name):
            return (g_big[name] if kind == 0 else upd[name][kind - 1])[None]
        return [ada[None], o_b_ada.reshape(1, 3 * D), vecs[0], bigk("w_in"), o_conv[None], vecs[1], vecs[2], vecs[3],
                bigk("w_conv_out"), vecs[4], vecs[5], o_w_sgu.reshape(1, HEADS, CHUNK, CHUNK), o_b_sgu[None],
                bigk("w_sgu_out"), bigk("w_o"), vecs[6].reshape(D)]

    return (loss, loc["grad_x"][None], *leaves(0), *leaves(1), *leaves(2), *leaves(3))
```

```python
import functools

import jax
import jax.numpy as jnp
from jax import lax
from jax.experimental import pallas as pl
from jax.experimental.pallas import tpu as pltpu

F32 = jnp.float32
BF16 = jnp.bfloat16
MESH = pl.DeviceIdType.MESH

D = 1024
N_SEC = 8
N_CHIP = 4
N_DEV = 8
EPS = 1e-6
CONV_K = 31
HALO = 32
CHUNK = 128
HEADS = 8
LANE = 128
SUB = 8
PACK = 16
VMEM_LIMIT = 56 * 1024 * 1024

ADAM_LR, ADAM_B1, ADAM_B2, ADAM_EPS, ADAM_WD, ADAM_STEP = 0.001, 0.9, 0.999, 1e-08, 0.01, 10

_SQRT_HALF = 0.7071067811865476
_INV_SQRT_2PI = 0.3989422804014327


def _sds(shape, dtype):
    return jax.ShapeDtypeStruct(shape, dtype)


def _params(sem=None):
    if sem is None:
        return pltpu.CompilerParams(vmem_limit_bytes=VMEM_LIMIT)
    return pltpu.CompilerParams(dimension_semantics=sem, vmem_limit_bytes=VMEM_LIMIT)


def _strips(n_rows, rows, fn):
    def step(s, carry):
        fn(pl.multiple_of(s * rows, rows))
        return carry
    lax.fori_loop(0, n_rows // rows, step, 0)


def _sigmoid(v):
    return 1.0 / (1.0 + jnp.exp(-v))


def _gelu(v):
    return 0.5 * v * (1.0 + lax.erf(v * _SQRT_HALF))


def _gelu_and_grad(v):
    cdf = 0.5 * (1.0 + lax.erf(v * _SQRT_HALF))
    return v * cdf, cdf + v * jnp.exp(-0.5 * v * v) * _INV_SQRT_2PI


def _dsilu(v, sg):
    return sg * (1.0 + v * (1.0 - sg))


def _rowmean(v):
    return jnp.mean(v, axis=-1, keepdims=True)


def _vec_spec(grid_rank):
    zeros = (0, 0)
    if grid_rank == 1:
        return pl.BlockSpec((1, D), lambda i: zeros)
    return pl.BlockSpec((1, D), lambda i, j: zeros)


def _in_proj(x, shift, scale, g_pre, wg_in):
    s_len = x.shape[0]
    tm = min(512, s_len)
    n_i = s_len // tm
    wn = wg_in.shape[2]

    def body(x_ref, sh_ref, sc_ref, g_ref, w_ref, p_ref, hb_ref):
        def strip(r0):
            xs = x_ref[pl.ds(r0, PACK), :]
            r = lax.rsqrt(_rowmean(xs * xs) + EPS)
            h = (xs * r) * g_ref[...] * (1.0 + sc_ref[...]) + sh_ref[...]
            hb_ref[pl.ds(r0, PACK), :] = h.astype(BF16)
        _strips(tm, PACK, strip)
        hb = hb_ref[...]
        for j in range(N_CHIP):
            p_ref[:, pl.ds(j * wn, wn)] = jnp.dot(hb, w_ref[j], preferred_element_type=F32).astype(BF16)

    return pl.pallas_call(
        body, name="in_proj", grid=(n_i,),
        in_specs=[pl.BlockSpec((tm, D), lambda i: (i, 0)), _vec_spec(1), _vec_spec(1), _vec_spec(1),
                  pl.BlockSpec((N_CHIP, D, wn), lambda i: (0, 0, 0), pipeline_mode=pl.Buffered(1))],
        out_specs=[pl.BlockSpec((tm, N_CHIP * wn), lambda i: (i, 0)), pl.BlockSpec((tm, D), lambda i: (i, 0))],
        out_shape=[_sds((s_len, N_SEC * D), BF16), _sds((s_len, D), BF16)],
        compiler_params=_params(("arbitrary",)),
    )(x, shift, scale, g_pre, wg_in)


def _conv_taps(win_ref, r0, lt, weight_of_offset, rows):
    lanes = pl.ds(lt * LANE, LANE)
    win = win_ref[pl.ds(r0, rows + HALO), lanes]
    n_out = rows // SUB
    acc = [jnp.zeros((SUB, LANE), F32) for _ in range(n_out)]
    for phase in range(SUB):
        offs = [o for o in weight_of_offset if o % SUB == phase]
        if not offs:
            continue
        q_max = max(o // SUB for o in offs)
        span = (n_out + q_max) * SUB
        sh = win[phase:phase + span, :]
        for o in offs:
            q = o // SUB
            w = weight_of_offset[o](lanes)
            for m in range(n_out):
                acc[m] = acc[m] + w * sh[(m + q) * SUB:(m + q + 1) * SUB, :]
    return acc


def _branch_a_fwd(p, conv_wb, conv_b, ln_g, ln_b):
    s_len = p.shape[0]
    tm = min(256, s_len)
    n_i = s_len // tm
    rows = 32

    def body(p_ref, wb_ref, cb_ref, g_ref, b_ref, ya_ref, y1_ref, abuf):
        @pl.when(pl.program_id(0) == 0)
        def _():
            abuf[pl.ds(0, HALO), :] = jnp.zeros((HALO, D), F32)

        def glu(r0):
            val = p_ref[pl.ds(r0, PACK), pl.ds(0, D)].astype(F32)
            gl = p_ref[pl.ds(r0, PACK), pl.ds(D, D)].astype(F32)
            abuf[pl.ds(HALO + r0, PACK), :] = val * _sigmoid(gl)
        _strips(tm, PACK,glu)

        taps = {HALO - (CONV_K - 1) + k: (lambda lanes, k=k: wb_ref[pl.ds(k * SUB, SUB), lanes]) for k in range(CONV_K)}

        def conv(r0):
            for lt in range(D // LANE):
                acc = _conv_taps(abuf, r0, lt, taps, rows)
                cb = cb_ref[:, pl.ds(lt * LANE, LANE)]
                for m, v in enumerate(acc):
                    y1_ref[pl.ds(r0 + m * SUB, SUB), pl.ds(lt * LANE, LANE)] = v + cb
        _strips(tm, rows, conv)

        def norm(r0):
            y1 = y1_ref[pl.ds(r0, PACK), :]
            mu = _rowmean(y1)
            yc = y1 - mu
            rstd = lax.rsqrt(_rowmean(yc * yc) + EPS)
            l1 = (yc * rstd) * g_ref[...] + b_ref[...]
            z = p_ref[pl.ds(r0, PACK), pl.ds(2 * D, D)].astype(F32)
            ya_ref[pl.ds(r0, PACK), :] = ((l1 * _sigmoid(l1)) * (z * _sigmoid(z))).astype(BF16)
        _strips(tm, PACK,norm)

        abuf[pl.ds(0, HALO), :] = abuf[pl.ds(tm, HALO), :]

    return pl.pallas_call(
        body, name="branch_a_fwd", grid=(n_i,),
        in_specs=[pl.BlockSpec((tm, 3 * D), lambda i: (i, 0)),
                  pl.BlockSpec((CONV_K * SUB, D), lambda i: (0, 0)), _vec_spec(1), _vec_spec(1), _vec_spec(1)],
        out_specs=[pl.BlockSpec((tm, D), lambda i: (i, 0)), pl.BlockSpec((tm, D), lambda i: (i, 0))],
        out_shape=[_sds((s_len, D), BF16), _sds((s_len, D), F32)],
        scratch_shapes=[pltpu.VMEM((tm + HALO, D), F32)],
        compiler_params=_params(("arbitrary",)),
    )(p, conv_wb, conv_b, ln_g, ln_b)


def _branch_b_fwd(p, wt, bias_full, ln_g, ln_b):
    s_len = p.shape[0]
    tm = min(256, s_len)
    n_i = s_len // tm

    def body(p_ref, wt_ref, bias_ref, g_ref, b_ref, yb_ref, vb, sbuf):
        def norm(r0):
            gv = _gelu(p_ref[pl.ds(r0, PACK), pl.ds(D, D)].astype(F32))
            mu = _rowmean(gv)
            vc = gv - mu
            rstd = lax.rsqrt(_rowmean(vc * vc) + EPS)
            vb[pl.ds(r0, PACK), :] = ((vc * rstd) * g_ref[...] + b_ref[...]).astype(BF16)
        _strips(tm, PACK,norm)

        for ck in range(tm // CHUNK):
            for h in range(HEADS):
                blk = (pl.ds(ck * CHUNK, CHUNK), pl.ds(h * LANE, LANE))
                sbuf[blk] = jnp.dot(wt_ref[h], vb[blk], preferred_element_type=F32) + bias_ref[:, pl.ds(h * LANE, LANE)]

        def gate(r0):
            u = _gelu(p_ref[pl.ds(r0, PACK), pl.ds(0, D)].astype(F32))
            z = p_ref[pl.ds(r0, PACK), pl.ds(2 * D, D)].astype(F32)
            yb_ref[pl.ds(r0, PACK), :] = (u * sbuf[pl.ds(r0, PACK), :] * (z * _sigmoid(z))).astype(BF16)
        _strips(tm, PACK,gate)

    return pl.pallas_call(
        body, name="branch_b_fwd", grid=(n_i,),
        in_specs=[pl.BlockSpec((tm, 3 * D), lambda i: (i, 1)),
                  pl.BlockSpec((HEADS, CHUNK, CHUNK), lambda i: (0, 0, 0)),
                  pl.BlockSpec((CHUNK, D), lambda i: (0, 0)), _vec_spec(1), _vec_spec(1)],
        out_specs=pl.BlockSpec((tm, D), lambda i: (i, 0)),
        out_shape=_sds((s_len, D), BF16),
        scratch_shapes=[pltpu.VMEM((tm, D), BF16), pltpu.VMEM((tm, D), F32)],
        compiler_params=_params(("arbitrary",)),
    )(p, wt, bias_full, ln_g, ln_b)


def _dot_t(a, b):
    return lax.dot_general(a, b, (((1,), (1,)), ((), ())), preferred_element_type=F32)


def _out_proj(p, ya_in, yb_in, x, target, gate, g_final, w_co, w_so, w_o):
    s_len = x.shape[0]
    tm = min(256, s_len)
    n_i = s_len // tm

    def body(pg_ref, ya_ref, yb_ref, x_ref, t_ref, gate_ref, gf_ref, wco_ref, wso_ref, wo_ref,
             dx2_ref, dya_ref, dyb_ref, dp_ref, mb_ref, dob_ref, dyab_ref, dybb_ref, sums_ref):
        @pl.when(pl.program_id(0) == 0)
        def _():
            sums_ref[...] = jnp.zeros((SUB, D), F32)

        y_a = jnp.dot(ya_ref[...], wco_ref[...], preferred_element_type=F32)
        y_b = jnp.dot(yb_ref[...], wso_ref[...], preferred_element_type=F32)
        ga = _sigmoid(pg_ref[:, pl.ds(0, D)].astype(F32))
        gb = _sigmoid(pg_ref[:, pl.ds(D, D)].astype(F32))
        mb = (ga * y_a + gb * y_b).astype(BF16)
        mb_ref[...] = mb
        o = jnp.dot(mb, wo_ref[...], preferred_element_type=F32)
        x2 = x_ref[...] + gate_ref[...] * o
        r2 = lax.rsqrt(_rowmean(x2 * x2) + EPS)
        xh = x2 * r2
        e = xh * gf_ref[...] - t_ref[...]
        dy = e * (1.0 / D)
        dxh = dy * gf_ref[...]
        dx2 = r2 * (dxh - xh * _rowmean(dxh * xh))
        dx2_ref[...] = dx2
        sums_ref[pl.ds(0, 1), :] += jnp.sum(dy * xh, axis=0, keepdims=True)
        sums_ref[pl.ds(1, 1), :] += jnp.sum(dx2 * o, axis=0, keepdims=True)
        sums_ref[pl.ds(2, 1), :] += jnp.sum(e * e, axis=0, keepdims=True) * (0.5 / D)
        dob = (gate_ref[...] * dx2).astype(BF16)
        dob_ref[...] = dob
        dm = _dot_t(dob, wo_ref[...])
        dy_a = (ga * dm).astype(BF16)
        dy_b = (gb * dm).astype(BF16)
        dyab_ref[...] = dy_a
        dybb_ref[...] = dy_b
        dp_ref[:, pl.ds(0, D)] = (dm * y_a * ga * (1.0 - ga)).astype(BF16)
        dp_ref[:, pl.ds(D, D)] = (dm * y_b * gb * (1.0 - gb)).astype(BF16)
        dya_ref[...] = _dot_t(dy_a, wco_ref[...])
        dyb_ref[...] = _dot_t(dy_b, wso_ref[...])

    tile = pl.BlockSpec((tm, D), lambda i: (i, 0))
    wspec = pl.BlockSpec((D, D), lambda i: (0, 0))
    return pl.pallas_call(
        body, name="out_proj", grid=(n_i,),
        in_specs=[pl.BlockSpec((tm, 2 * D), lambda i: (i, 3)), tile, tile, tile, tile, _vec_spec(1), _vec_spec(1),
                  wspec, wspec, wspec],
        out_specs=[tile, tile, tile, pl.BlockSpec((tm, 2 * D), lambda i: (i, 3)), tile, tile, tile, tile,
                   pl.BlockSpec((SUB, D), lambda i: (0, 0))],
        out_shape=[_sds((s_len, D), F32), _sds((s_len, D), F32), _sds((s_len, D), F32), _sds((s_len, N_SEC * D), BF16),
                   _sds((s_len, D), BF16), _sds((s_len, D), BF16), _sds((s_len, D), BF16), _sds((s_len, D), BF16),
                   _sds((SUB, D), F32)],
        compiler_params=_params(("arbitrary",)),
    )(p, ya_in, yb_in, x, target, gate, g_final, w_co, w_so, w_o)


A_STATS_ROWS = 8 + HALO


def _branch_a_bwd(p, y1, dya_in, dp, conv_wb, ln_g, ln_b):
    s_len = p.shape[0]
    tm = min(256, s_len)
    n_i = s_len // tm
    rows = 32
    n_out = rows // SUB

    def tile_of(i):
        return n_i - 1 - i

    def body(p_ref, y1_ref, dya_ref, dp_in, wb_ref, g_ref, b_ref, dp_ref, st_ref, dybuf, acc8, tapacc):
        del dp_in
        i = pl.program_id(0)

        @pl.when(i == 0)
        def _():
            dybuf[pl.ds(tm, HALO), :] = jnp.zeros((HALO, D), F32)
            st_ref[...] = jnp.zeros((A_STATS_ROWS, D), F32)
            acc8[...] = jnp.zeros((3 * PACK, D), F32)
            tapacc[...] = jnp.zeros((CONV_K * SUB, D), F32)

        def norm_bwd(r0):
            y1 = y1_ref[pl.ds(r0, PACK), :]
            mu = _rowmean(y1)
            yc = y1 - mu
            rstd = lax.rsqrt(_rowmean(yc * yc) + EPS)
            n1 = yc * rstd
            l1 = n1 * g_ref[...] + b_ref[...]
            sg = _sigmoid(l1)
            z = p_ref[pl.ds(r0, PACK), pl.ds(2 * D, D)].astype(F32)
            sz = _sigmoid(z)
            dya = dya_ref[pl.ds(r0, PACK), :]
            dp_ref[pl.ds(r0, PACK), pl.ds(2 * D, D)] = (dya * (l1 * sg) * _dsilu(z, sz)).astype(BF16)
            dl1 = dya * (z * sz) * _dsilu(l1, sg)
            acc8[pl.ds(0, PACK), :] += dl1 * n1
            acc8[pl.ds(PACK, PACK), :] += dl1
            dn1 = dl1 * g_ref[...]
            dy1 = rstd * (dn1 - _rowmean(dn1) - n1 * _rowmean(dn1 * n1))
            acc8[pl.ds(2 * PACK, PACK), :] += dy1
            dybuf[pl.ds(r0, PACK), :] = dy1
        _strips(tm, PACK,norm_bwd)

        def conv_bwd(r0):
            for lt in range(D // LANE):
                lanes = pl.ds(lt * LANE, LANE)
                glanes = pl.ds(D + lt * LANE, LANE)
                win = dybuf[pl.ds(r0, rows + HALO), lanes]
                sg16, a16 = [], []
                for h in range(rows // PACK):
                    rr = pl.ds(r0 + h * PACK, PACK)
                    s = _sigmoid(p_ref[rr, glanes].astype(F32))
                    sg16.append(s)
                    a16.append(p_ref[rr, lanes].astype(F32) * s)
                a = [a16[m // 2][(m % 2) * SUB:(m % 2 + 1) * SUB, :] for m in range(n_out)]
                da = [jnp.zeros((SUB, LANE), F32) for _ in range(n_out)]
                for phase in range(SUB):
                    offs = [o for o in range(CONV_K) if o % SUB == phase]
                    q_max = max(o // SUB for o in offs)
                    sh = win[phase:phase + (n_out + q_max) * SUB, :]
                    for o in offs:
                        k, q = CONV_K - 1 - o, o // SUB
                        w = wb_ref[pl.ds(k * SUB, SUB), lanes]
                        part = None
                        for m in range(n_out):
                            s = sh[(m + q) * SUB:(m + q + 1) * SUB, :]
                            da[m] = da[m] + w * s
                            part = a[m] * s if part is None else part + a[m] * s
                        tapacc[pl.ds(k * SUB, SUB), lanes] += part
                for h in range(rows // PACK):
                    rr = pl.ds(r0 + h * PACK, PACK)
                    da16 = jnp.concatenate(da[2 * h:2 * h + 2], axis=0)
                    dp_ref[rr, lanes] = (da16 * sg16[h]).astype(BF16)
                    dp_ref[rr, glanes] = (da16 * a16[h] * (1.0 - sg16[h])).astype(BF16)
        _strips(tm, rows, conv_bwd)

        dybuf[pl.ds(tm, HALO), :] = dybuf[pl.ds(0, HALO), :]

        @pl.when(i == n_i - 1)
        def _():
            for j in range(3):
                st_ref[pl.ds(j, 1), :] = jnp.sum(acc8[pl.ds(j * PACK, PACK), :], axis=0, keepdims=True)
            for k in range(CONV_K):
                st_ref[pl.ds(SUB + k, 1), :] = jnp.sum(tapacc[pl.ds(k * SUB, SUB), :], axis=0, keepdims=True)

    return pl.pallas_call(
        body, name="branch_a_bwd", grid=(n_i,),
        in_specs=[pl.BlockSpec((tm, 3 * D), lambda i: (tile_of(i), 0)),
                  pl.BlockSpec((tm, D), lambda i: (tile_of(i), 0)),
                  pl.BlockSpec((tm, D), lambda i: (tile_of(i), 0)),
                  pl.BlockSpec(memory_space=pl.ANY),
                  pl.BlockSpec((CONV_K * SUB, D), lambda i: (0, 0)), _vec_spec(1), _vec_spec(1)],
        out_specs=[pl.BlockSpec((tm, 3 * D), lambda i: (tile_of(i), 0)),
                   pl.BlockSpec((A_STATS_ROWS, D), lambda i: (0, 0))],
        out_shape=[_sds(dp.shape, BF16), _sds((A_STATS_ROWS, D), F32)],
        scratch_shapes=[pltpu.VMEM((tm + HALO, D), F32), pltpu.VMEM((3 * PACK, D), F32), pltpu.VMEM((CONV_K * SUB, D), F32)],
        input_output_aliases={3: 0},
        compiler_params=_params(("arbitrary",)),
    )(p, y1, dya_in, dp, conv_wb, ln_g, ln_b)


def _branch_b_bwd(p, dyb_in, dp, wt, wtt, bias_full, ln_g, ln_b):
    s_len = p.shape[0]
    tm = min(256, s_len)
    n_i = s_len // tm

    def body(p_ref, dyb_ref, dp_in, wt_ref, wtt_ref, bias_ref, g_ref, b_ref, dp_ref, st_ref, gbt_ref, gw_ref,
             vb, n2buf, rstdbuf, sbuf, dsb, dvbuf, acc8, gb_ref, dgbuf):
        del dp_in
        i = pl.program_id(0)

        @pl.when(i == 0)
        def _():
            st_ref[...] = jnp.zeros((SUB, D), F32)
            gbt_ref[...] = jnp.zeros((CHUNK, LANE), F32)
            gb_ref[...] = jnp.zeros((CHUNK, D), F32)
            gw_ref[...] = jnp.zeros((HEADS, CHUNK, CHUNK), F32)
            acc8[...] = jnp.zeros((2 * PACK, D), F32)

        def norm(r0):
            gv, dgv = _gelu_and_grad(p_ref[pl.ds(r0, PACK), pl.ds(D, D)].astype(F32))
            dgbuf[pl.ds(r0, PACK), :] = dgv
            mu = _rowmean(gv)
            vc = gv - mu
            rstd = lax.rsqrt(_rowmean(vc * vc) + EPS)
            n2 = vc * rstd
            n2buf[pl.ds(r0, PACK), :] = n2
            rstdbuf[pl.ds(r0, PACK), :] = jnp.broadcast_to(rstd, (PACK, LANE))
            vb[pl.ds(r0, PACK), :] = (n2 * g_ref[...] + b_ref[...]).astype(BF16)
        _strips(tm, PACK,norm)

        for ck in range(tm // CHUNK):
            for h in range(HEADS):
                blk = (pl.ds(ck * CHUNK, CHUNK), pl.ds(h * LANE, LANE))
                sbuf[blk] = jnp.dot(wt_ref[h], vb[blk], preferred_element_type=F32) + bias_ref[:, pl.ds(h * LANE, LANE)]

        def gate_bwd(r0):
            pu = p_ref[pl.ds(r0, PACK), pl.ds(0, D)].astype(F32)
            u, du = _gelu_and_grad(pu)
            z = p_ref[pl.ds(r0, PACK), pl.ds(2 * D, D)].astype(F32)
            sg = _sigmoid(z)
            sz = z * sg
            s = sbuf[pl.ds(r0, PACK), :]
            dyb = dyb_ref[pl.ds(r0, PACK), :]
            ds = dyb * u * sz
            dsb[pl.ds(r0, PACK), :] = ds.astype(BF16)
            gb_ref[pl.ds(pl.multiple_of(r0 % CHUNK, PACK), PACK), :] += ds
            dp_ref[pl.ds(r0, PACK), pl.ds(0, D)] = (dyb * s * sz * du).astype(BF16)
            dp_ref[pl.ds(r0, PACK), pl.ds(2 * D, D)] = (dyb * u * s * _dsilu(z, sg)).astype(BF16)
        _strips(tm, PACK,gate_bwd)

        for ck in range(tm // CHUNK):
            for h in range(HEADS):
                blk = (pl.ds(ck * CHUNK, CHUNK), pl.ds(h * LANE, LANE))
                d_s = dsb[blk]
                dvbuf[blk] = jnp.dot(wtt_ref[h], d_s, preferred_element_type=F32)
                gw_ref[h] += _dot_t(d_s, vb[blk])

        def norm_bwd(r0):
            dv = dvbuf[pl.ds(r0, PACK), :]
            n2 = n2buf[pl.ds(r0, PACK), :]
            rstd = rstdbuf[pl.ds(r0, PACK), pl.ds(0, 1)]
            acc8[pl.ds(0, PACK), :] += dv * n2
            acc8[pl.ds(PACK, PACK), :] += dv
            dn2 = dv * g_ref[...]
            dgv = rstd * (dn2 - _rowmean(dn2) - n2 * _rowmean(dn2 * n2))
            dp_ref[pl.ds(r0, PACK), pl.ds(D, D)] = (dgv * dgbuf[pl.ds(r0, PACK), :]).astype(BF16)
        _strips(tm, PACK,norm_bwd)

        @pl.when(i == n_i - 1)
        def _():
            for j in range(2):
                st_ref[pl.ds(j, 1), :] = jnp.sum(acc8[pl.ds(j * PACK, PACK), :], axis=0, keepdims=True)
            row = lax.broadcasted_iota(jnp.int32, (CHUNK, CHUNK), 0)
            col = lax.broadcasted_iota(jnp.int32, (CHUNK, CHUNK), 1)
            for h in range(HEADS):
                gw_ref[h] = jnp.where(row >= col, gw_ref[h], 0.0)
            lane = lax.broadcasted_iota(jnp.int32, (CHUNK, LANE), 1)
            gbt = jnp.zeros((CHUNK, LANE), F32)
            for h in range(HEADS):
                gbt = jnp.where(lane == h, jnp.sum(gb_ref[:, pl.ds(h * LANE, LANE)], axis=1, keepdims=True), gbt)
            gbt_ref[...] = gbt

    wspec = pl.BlockSpec((HEADS, CHUNK, CHUNK), lambda i: (0, 0, 0))
    return pl.pallas_call(
        body, name="branch_b_bwd", grid=(n_i,),
        in_specs=[pl.BlockSpec((tm, 3 * D), lambda i: (i, 1)), pl.BlockSpec((tm, D), lambda i: (i, 0)),
                  pl.BlockSpec(memory_space=pl.ANY), wspec, wspec,
                  pl.BlockSpec((CHUNK, D), lambda i: (0, 0)), _vec_spec(1), _vec_spec(1)],
        out_specs=[pl.BlockSpec((tm, 3 * D), lambda i: (i, 1)), pl.BlockSpec((SUB, D), lambda i: (0, 0)),
                   pl.BlockSpec((CHUNK, LANE), lambda i: (0, 0)), wspec],
        out_shape=[_sds(dp.shape, BF16), _sds((SUB, D), F32), _sds((CHUNK, LANE), F32), _sds((HEADS, CHUNK, CHUNK), F32)],
        scratch_shapes=[pltpu.VMEM((tm, D), BF16), pltpu.VMEM((tm, D), F32), pltpu.VMEM((tm, LANE), F32),
                        pltpu.VMEM((tm, D), F32), pltpu.VMEM((tm, D), BF16), pltpu.VMEM((tm, D), F32),
                        pltpu.VMEM((2 * PACK, D), F32), pltpu.VMEM((CHUNK, D), F32), pltpu.VMEM((tm, D), F32)],
        input_output_aliases={2: 0},
        compiler_params=_params(("arbitrary",)),
    )(p, dyb_in, dp, wt, wtt, bias_full, ln_g, ln_b)


def _in_proj_bwd(dp, wg_in, x, dx2, shift, scale, g_pre):
    del shift
    s_len = x.shape[0]
    tm = min(512, s_len)
    n_i = s_len // tm
    wn = wg_in.shape[2]

    def body(dp_ref, w_ref, x_ref, dx2_ref, sc_ref, g_ref, gx_ref, st_ref, acc, acc8):
        i = pl.program_id(0)

        @pl.when(i == 0)
        def _():
            st_ref[...] = jnp.zeros((SUB, D), F32)
            acc8[...] = jnp.zeros((3 * PACK, D), F32)

        dh = _dot_t(dp_ref[:, pl.ds(0, wn)], w_ref[0])
        for j in range(1, N_CHIP):
            dh = dh + _dot_t(dp_ref[:, pl.ds(j * wn, wn)], w_ref[j])
        acc[...] = dh

        def strip(r0):
            xs = x_ref[pl.ds(r0, PACK), :]
            r = lax.rsqrt(_rowmean(xs * xs) + EPS)
            xn = xs * r
            dhs = acc[pl.ds(r0, PACK), :]
            acc8[pl.ds(0, PACK), :] += dhs
            acc8[pl.ds(PACK, PACK), :] += dhs * (xn * g_ref[...])
            dhp = dhs * (1.0 + sc_ref[...])
            acc8[pl.ds(2 * PACK, PACK), :] += dhp * xn
            dxn = dhp * g_ref[...]
            gx_ref[pl.ds(r0, PACK), :] = dx2_ref[pl.ds(r0, PACK), :] + r * (dxn - xn * _rowmean(dxn * xn))
        _strips(tm, PACK, strip)

        @pl.when(i == n_i - 1)
        def _():
            for k in range(3):
                st_ref[pl.ds(k, 1), :] = jnp.sum(acc8[pl.ds(k * PACK, PACK), :], axis=0, keepdims=True)

    tile = pl.BlockSpec((tm, D), lambda i: (i, 0))
    return pl.pallas_call(
        body, name="in_proj_bwd", grid=(n_i,),
        in_specs=[pl.BlockSpec((tm, N_CHIP * wn), lambda i: (i, 0)),
                  pl.BlockSpec((N_CHIP, D, wn), lambda i: (0, 0, 0), pipeline_mode=pl.Buffered(1)),
                  tile, tile, _vec_spec(1), _vec_spec(1)],
        out_specs=[tile, pl.BlockSpec((SUB, D), lambda i: (0, 0))],
        out_shape=[_sds((s_len, D), F32), _sds((SUB, D), F32)],
        scratch_shapes=[pltpu.VMEM((tm, D), F32), pltpu.VMEM((3 * PACK, D), F32)],
        compiler_params=_params(("arbitrary",)),
    )(dp, wg_in, x, dx2, scale, g_pre)


def _grad_matmul(a, b, name):
    s_len, n = b.shape
    cb = min(2 * D, n)
    tn = 512
    per = cb // tn

    def body(a_ref, b_ref, ob_ref):
        ob_ref[0] = lax.dot_general(a_ref[...], b_ref[...], (((0,), (0,)), ((), ())),
                                    preferred_element_type=F32).astype(BF16)

    return pl.pallas_call(
        body, name=name, grid=(n // tn,),
        in_specs=[pl.BlockSpec((s_len, D), lambda j: (0, 0), pipeline_mode=pl.Buffered(1)),
                  pl.BlockSpec((s_len, tn), lambda j: (0, j))],
        out_specs=pl.BlockSpec((1, D, tn), lambda j: (j // per, 0, j % per)),
        out_shape=_sds((n // cb, D, cb), BF16),
        compiler_params=_params(("arbitrary",)),
    )(a, b)


def _local_step(x, target, shift, scale, gate, g_pre, conv_w_full, conv_b, conv_ln_g, conv_ln_b,
                sgu_ln_g, sgu_ln_b, w_sgu, b_sgu, g_final, wg_in, w_co, w_so, w_o):
    conv_wb = jnp.repeat(conv_w_full, SUB, axis=0)
    causal = jnp.tril(jnp.ones((CHUNK, CHUNK), dtype=bool))
    wt = jnp.where(causal[None], w_sgu, 0.0).astype(BF16)
    wtt = jnp.swapaxes(wt, 1, 2)
    bias_full = jnp.repeat(b_sgu.T, LANE, axis=1)

    p, hb = _in_proj(x, shift, scale, g_pre, wg_in)
    ya_in, y1 = _branch_a_fwd(p, conv_wb, conv_b, conv_ln_g, conv_ln_b)
    yb_in = _branch_b_fwd(p, wt, bias_full, sgu_ln_g, sgu_ln_b)
    dx2, dya_in, dyb_in, dp, mb, dob, dyab, dybb, sums_o = _out_proj(
        p, ya_in, yb_in, x, target, gate, g_final, w_co, w_so, w_o)
    dp, st_a = _branch_a_bwd(p, y1, dya_in, dp, conv_wb, conv_ln_g, conv_ln_b)
    dp, st_b, gbt, gws = _branch_b_bwd(p, dyb_in, dp, wt, wtt, bias_full, sgu_ln_g, sgu_ln_b)
    grad_x, st_i = _in_proj_bwd(dp, wg_in, x, dx2, shift, scale, g_pre)
    gw_in = _grad_matmul(hb, dp, "grad_w_in")
    gw_o = _grad_matmul(mb, dob, "grad_w_o")
    gw_co = _grad_matmul(ya_in, dyab, "grad_w_conv_out")
    gw_so = _grad_matmul(yb_in, dybb, "grad_w_sgu_out")
    return dict(
        grad_x=grad_x, loss_cols=sums_o[2:3], g_final=sums_o[0:1], d_gate=sums_o[1:2],
        d_shift=st_i[0:1], d_scale=st_i[1:2], g_pre=st_i[2:3],
        conv_ln_g=st_a[0:1], conv_ln_b=st_a[1:2], conv_b=st_a[2:3], conv_w=st_a[SUB:SUB + CONV_K],
        sgu_ln_g=st_b[0:1], sgu_ln_b=st_b[1:2], b_sgu=gbt[:, :HEADS].T, w_sgu=gws,
        w_in=gw_in, w_o=gw_o, w_conv_out=gw_co, w_sgu_out=gw_so)


ANY_SPEC = pl.BlockSpec(memory_space=pl.ANY)
VMEM_SPEC = pl.BlockSpec(memory_space=pltpu.VMEM)


def _place():
    return lax.axis_index("x"), lax.axis_index("y"), lax.axis_index("c")


def _peer(k):
    x, y, c = _place()
    return (1 - x if k & 4 else x, 1 - y if k & 2 else y, 1 - c if k & 1 else c)


def _dev_of(p):
    return 4 * p[0] + 2 * p[1] + p[2]


def _chip_of(p):
    return 2 * p[0] + p[1]


def _rdma(src, dst, send_sem, recv_sem, to):
    return pltpu.make_async_remote_copy(src_ref=src, dst_ref=dst, send_sem=send_sem, recv_sem=recv_sem,
                                        device_id=to, device_id_type=MESH)


CHIP_PEERS = (2, 4, 6)
ALL_PEERS = tuple(range(1, N_DEV))
SIBLING = 1


def _setup_comm(c8, w_ada_s, b_ada_s, convw_s):
    n_mod = w_ada_s.shape[1]
    rows = SUB * N_DEV

    def body(c8_ref, wada_ref, bada_ref, cw_ref, call_ref, mod_ref, cwall_ref, csend, crecv, wsend, wrecv, msend, mrecv):
        me = _place()
        dev, chip = _dev_of(me), _chip_of(me)

        def c_rows(d):
            return call_ref.at[pl.ds(pl.multiple_of(d * SUB, SUB), SUB), :]

        call_ref[pl.ds(pl.multiple_of(dev * SUB, SUB), SUB), :] = c8_ref[...]
        cwall_ref[chip] = cw_ref[...]
        c_out = [_rdma(c8_ref, c_rows(dev), csend.at[k], crecv.at[k], _peer(k)) for k in ALL_PEERS]
        w_out = [_rdma(cw_ref, cwall_ref.at[chip], wsend.at[k], wrecv.at[k], _peer(k)) for k in CHIP_PEERS]
        for cp in c_out + w_out:
            cp.start()
        for k in ALL_PEERS:
            _rdma(c8_ref, c_rows(_dev_of(_peer(k))), csend.at[k], crecv.at[k], _peer(k)).wait_recv()
        part = jnp.dot(call_ref[...].astype(BF16), wada_ref[...].astype(BF16), preferred_element_type=F32) + bada_ref[...]
        mod_ref[chip] = part
        m_out = [_rdma(mod_ref.at[chip], mod_ref.at[chip], msend.at[k], mrecv.at[k], _peer(k)) for k in CHIP_PEERS]
        for cp in m_out:
            cp.start()
        for k in CHIP_PEERS:
            pc = _chip_of(_peer(k))
            _rdma(cw_ref, cwall_ref.at[pc], wsend.at[k], wrecv.at[k], _peer(k)).wait_recv()
            _rdma(mod_ref.at[pc], mod_ref.at[pc], msend.at[k], mrecv.at[k], _peer(k)).wait_recv()
        for cp in c_out + w_out + m_out:
            cp.wait_send()

    return pl.pallas_call(
        body, name="setup_comm",
        in_specs=[VMEM_SPEC] * 4, out_specs=[VMEM_SPEC] * 3,
        out_shape=[_sds((rows, D), F32), _sds((N_CHIP, rows, n_mod), F32), _sds((N_CHIP,) + convw_s.shape, F32)],
        scratch_shapes=[pltpu.SemaphoreType.DMA((N_DEV,))] * 6,
        compiler_params=_params(),
    )(c8, w_ada_s, b_ada_s, convw_s)


def _gather_weights(shards):
    n = len(shards)

    def body(*refs):
        ins, outs = refs[:n], refs[n:2 * n]
        lsem, isend, irecv, dsend, drecv = refs[2 * n:]
        me = _place()
        chip, c = _chip_of(me), me[2]
        local = [pltpu.make_async_copy(ins[t], outs[t].at[chip], lsem.at[t]) for t in range(n)]
        for cp in local:
            cp.start()

        def half(t, which):
            hr = shards[t].shape[0] // 2
            return pl.ds(pl.multiple_of(which * hr, hr), hr)

        sends = []
        for t in range(n):
            for j, k in enumerate(CHIP_PEERS):
                cp = _rdma(ins[t].at[half(t, c)], outs[t].at[chip, half(t, c)], isend.at[t, j], irecv.at[t, j], _peer(k))
                cp.start()
                sends.append(cp)
        for t in range(n):
            for j, k in enumerate(CHIP_PEERS):
                blk = outs[t].at[_chip_of(_peer(k)), half(t, c)]
                _rdma(blk, blk, isend.at[t, j], irecv.at[t, j], _peer(k)).wait_recv()
                cp = _rdma(blk, blk, dsend.at[t, j], drecv.at[t, j], _peer(SIBLING))
                cp.start()
                sends.append(cp)
        for t in range(n):
            for j, k in enumerate(CHIP_PEERS):
                blk = outs[t].at[_chip_of(_peer(k)), half(t, 1 - c)]
                _rdma(blk, blk, dsend.at[t, j], drecv.at[t, j], _peer(SIBLING)).wait_recv()
        for cp in sends:
            cp.wait_send()
        for cp in local:
            cp.wait()

    return pl.pallas_call(
        body, name="gather_weights",
        in_specs=[VMEM_SPEC] * n, out_specs=[VMEM_SPEC] * n,
        out_shape=[_sds((N_CHIP,) + s.shape, s.dtype) for s in shards],
        scratch_shapes=[pltpu.SemaphoreType.DMA((n,))] + [pltpu.SemaphoreType.DMA((n, len(CHIP_PEERS)))] * 4,
        compiler_params=_params(),
    )(*shards)


def _reduce_scatter(grads, name):
    n = len(grads)
    shapes = [g.shape[2:] for g in grads]

    def body(*refs):
        ins, outs = refs[:n], refs[n:2 * n]
        pbufs, rbufs, accs = refs[2 * n:3 * n], refs[3 * n:4 * n], refs[4 * n:5 * n]
        psend, precv, csend, crecv, fsend, frecv = refs[5 * n:]
        me = _place()
        chip, c = _chip_of(me), me[2]
        sib = _peer(SIBLING)

        def to_sibling(t, d):
            return _rdma(ins[t].at[d, 1 - c], pbufs[t].at[d], psend.at[t, d], precv.at[t, d], sib)

        sends = []
        for t in range(n):
            for d in range(N_CHIP):
                cp = to_sibling(t, d)
                cp.start()
                sends.append(cp)
        for j in (1, 2, 3, 0):
            d = jnp.bitwise_xor(chip, j)
            for t in range(n):
                to_sibling(t, d).wait_recv()

                def pair_sum(r0, t=t, d=d, j=j):
                    rows = pl.ds(r0, PACK)
                    s = ins[t][d, c, rows, :].astype(F32) + pbufs[t][d, rows, :].astype(F32)
                    if j == 0:
                        accs[t][rows, :] = s
                    else:
                        pbufs[t][d, rows, :] = s.astype(BF16)
                _strips(shapes[t][0], PACK, pair_sum)
                if j:
                    cp = _rdma(pbufs[t].at[d], rbufs[t].at[j - 1], csend.at[t, j], crecv.at[t, j], _peer(2 * j))
                    cp.start()
                    sends.append(cp)
        for t in range(n):
            for j in (1, 2, 3):
                blk = rbufs[t].at[j - 1]
                _rdma(blk, blk, csend.at[t, j], crecv.at[t, j], _peer(2 * j)).wait_recv()

            def total(r0, t=t):
                rows = pl.ds(r0, PACK)
                s = accs[t][rows, :] + rbufs[t][0, rows, :].astype(F32)
                s = s + rbufs[t][1, rows, :].astype(F32)
                outs[t][c, rows, :] = s + rbufs[t][2, rows, :].astype(F32)
            _strips(shapes[t][0], PACK, total)
            cp = _rdma(outs[t].at[c], outs[t].at[c], fsend.at[t], frecv.at[t], sib)
            cp.start()
            sends.append(cp)
        for t in range(n):
            blk = outs[t].at[1 - c]
            _rdma(blk, blk, fsend.at[t], frecv.at[t], sib).wait_recv()
        for cp in sends:
            cp.wait_send()

    return pl.pallas_call(
        body, name=name,
        in_specs=[VMEM_SPEC] * n, out_specs=[VMEM_SPEC] * n,
        out_shape=[_sds((2,) + s, F32) for s in shapes],
        scratch_shapes=([pltpu.VMEM((N_CHIP,) + s, BF16) for s in shapes] + [pltpu.VMEM((N_CHIP - 1,) + s, BF16) for s in shapes]
                        + [pltpu.VMEM(s, F32) for s in shapes]
                        + [pltpu.SemaphoreType.DMA((n, N_CHIP))] * 4 + [pltpu.SemaphoreType.DMA((n,))] * 2),
        compiler_params=_params(),
    )(*grads)


def _scatter_grads(grads):
    n = len(grads)

    def body(*refs):
        ins, outs = refs[:n], refs[n:2 * n]
        lsem, send, recv = refs[2 * n:]
        me = _place()
        dev, chip, c = _dev_of(me), _chip_of(me), me[2]
        local = [pltpu.make_async_copy(ins[t].at[chip, c], outs[t].at[dev], lsem.at[t]) for t in range(n)]
        for cp in local:
            cp.start()
        sends = []
        for t in range(n):
            for k in ALL_PEERS:
                to = _peer(k)
                cp = _rdma(ins[t].at[_chip_of(to), to[2]], outs[t].at[dev], send.at[t, k], recv.at[t, k], to)
                cp.start()
                sends.append(cp)
        for t in range(n):
            for k in ALL_PEERS:
                blk = outs[t].at[_dev_of(_peer(k))]
                _rdma(blk, blk, send.at[t, k], recv.at[t, k], _peer(k)).wait_recv()
        for cp in sends:
            cp.wait_send()
        for cp in local:
            cp.wait()

    return pl.pallas_call(
        body, name="scatter_grads",
        in_specs=[ANY_SPEC] * n, out_specs=[ANY_SPEC] * n,
        out_shape=[_sds((N_DEV,) + g.shape[2:], g.dtype) for g in grads],
        scratch_shapes=[pltpu.SemaphoreType.DMA((n,))] + [pltpu.SemaphoreType.DMA((n, N_DEV))] * 2,
        compiler_params=_params(),
    )(*grads)


def _sum_devices(parts, name):
    _, r, cols = parts.shape
    tr = min(r, 128)

    def body(in_ref, o_ref):
        acc = in_ref[0].astype(F32)
        for d in range(1, N_DEV):
            acc = acc + in_ref[d].astype(F32)
        o_ref[...] = acc

    return pl.pallas_call(
        body, name=name, grid=(r // tr,),
        in_specs=[pl.BlockSpec((N_DEV, tr, cols), lambda i: (0, i, 0))],
        out_specs=pl.BlockSpec((tr, cols), lambda i: (i, 0)),
        out_shape=_sds((r, cols), F32),
        compiler_params=_params(("arbitrary",)),
    )(parts)


def _share_halves(reds):
    n = len(reds)

    def body(*refs):
        ins, outs = refs[:n], refs[n:2 * n]
        lsem, send, recv = refs[2 * n:]
        me = _place()
        c = me[2]
        local = [pltpu.make_async_copy(ins[t], outs[t].at[c], lsem.at[t]) for t in range(n)]
        sends = [_rdma(ins[t], outs[t].at[c], send.at[t], recv.at[t], _peer(SIBLING)) for t in range(n)]
        for cp in local + sends:
            cp.start()
        for t in range(n):
            _rdma(ins[t], outs[t].at[1 - c], send.at[t], recv.at[t], _peer(SIBLING)).wait_recv()
        for cp in sends:
            cp.wait_send()
        for cp in local:
            cp.wait()

    return pl.pallas_call(
        body, name="share_halves",
        in_specs=[VMEM_SPEC] * n, out_specs=[VMEM_SPEC] * n,
        out_shape=[_sds((2,) + r.shape, r.dtype) for r in reds],
        scratch_shapes=[pltpu.SemaphoreType.DMA((n,))] * 3,
        compiler_params=_params(),
    )(*reds)


def _sum_small(blobs):
    n = len(blobs)

    def body(*refs):
        ins, outs = refs[:n], refs[n:2 * n]
        pbufs, buf4s = refs[2 * n:3 * n], refs[3 * n:4 * n]
        psend, precv, send, recv = refs[4 * n:]
        me = _place()
        chip = _chip_of(me)
        pairs = [_rdma(ins[t], pbufs[t], psend.at[t], precv.at[t], _peer(SIBLING)) for t in range(n)]
        for cp in pairs:
            cp.start()
        out = []
        for t in range(n):
            pairs[t].wait()
            buf4s[t][chip] = ins[t][...] + pbufs[t][...]
            for k in CHIP_PEERS:
                cp = _rdma(buf4s[t].at[chip], buf4s[t].at[chip], send.at[t, k], recv.at[t, k], _peer(k))
                cp.start()
                out.append(cp)
        for t in range(n):
            for k in CHIP_PEERS:
                blk = buf4s[t].at[_chip_of(_peer(k))]
                _rdma(blk, blk, send.at[t, k], recv.at[t, k], _peer(k)).wait_recv()
            outs[t][...] = (buf4s[t][0] + buf4s[t][1]) + (buf4s[t][2] + buf4s[t][3])
        for cp in out:
            cp.wait_send()

    return pl.pallas_call(
        body, name="sum_small",
        in_specs=[VMEM_SPEC] * n, out_specs=[VMEM_SPEC] * n, out_shape=[_sds(b.shape, F32) for b in blobs],
        scratch_shapes=([pltpu.VMEM(b.shape, F32) for b in blobs] + [pltpu.VMEM((N_CHIP,) + b.shape, F32) for b in blobs]
                        + [pltpu.SemaphoreType.DMA((n,))] * 2 + [pltpu.SemaphoreType.DMA((n, N_DEV))] * 2),
        compiler_params=_params(),
    )(*blobs)


def _adamw_math(w, g, m, v):
    m = ADAM_B1 * m + (1.0 - ADAM_B1) * g
    v = ADAM_B2 * v + (1.0 - ADAM_B2) * (g * g)
    m_hat = m / (1.0 - ADAM_B1 ** ADAM_STEP)
    v_hat = v / (1.0 - ADAM_B2 ** ADAM_STEP)
    delta = -ADAM_LR * (m_hat / (jnp.sqrt(v_hat) + ADAM_EPS) + ADAM_WD * w)
    return delta, m, v


def _row_tile(r, cols):
    if r * cols * 4 <= 2 ** 20:
        return r
    return next(t for t in (512, 256, 128, 64, 32, 16, 8) if r % t == 0 and t * cols * 4 <= 2 ** 20)


def _adamw(w, g, m, v, name):
    r, cols = w.shape
    tr = _row_tile(r, cols)

    def body(w_ref, g_ref, m_ref, v_ref, d_ref, nm_ref, nv_ref):
        d_ref[...], nm_ref[...], nv_ref[...] = _adamw_math(w_ref[...], g_ref[...], m_ref[...], v_ref[...])

    spec = pl.BlockSpec((tr, cols), lambda i: (i, 0))
    return pl.pallas_call(
        body, name=name, grid=(r // tr,), in_specs=[spec] * 4, out_specs=[spec] * 3,
        out_shape=[_sds((r, cols), F32)] * 3, compiler_params=_params(("arbitrary",)),
    )(w, g, m, v)


def _adamw_ada(w, ct, dm, m, v):
    r, cols = w.shape
    tr = _row_tile(r, cols)

    def body(w_ref, ct_ref, dm_ref, m_ref, v_ref, g_ref, d_ref, nm_ref, nv_ref):
        g = jnp.dot(ct_ref[...], dm_ref[...], preferred_element_type=F32)
        g_ref[...] = g
        d_ref[...], nm_ref[...], nv_ref[...] = _adamw_math(w_ref[...], g, m_ref[...], v_ref[...])

    spec = pl.BlockSpec((tr, cols), lambda i: (i, 0))
    return pl.pallas_call(
        body, name="adamw_ada", grid=(r // tr,),
        in_specs=[spec, pl.BlockSpec((tr, LANE), lambda i: (i, 0)), pl.BlockSpec((LANE, cols), lambda i: (0, 0)), spec, spec],
        out_specs=[spec] * 4, out_shape=[_sds((r, cols), F32)] * 4, compiler_params=_params(("arbitrary",)),
    )(w, ct, dm, m, v)


BLOB_VEC, BLOB_BSGU, BLOB_CONV, BLOB_ADA, BLOB_DMOD, BLOB_LOSS, BLOB_ROWS = 0, 8, 16, 48, 56, 80, 88
N_VEC = 7


def _adamw_small(tot, g_w_sgu, g_conv, params):
    n = len(params)

    def body(*refs):
        tot_ref, gws_ref, gconv_ref = refs[:3]
        wmv = refs[3:3 + 3 * n]
        outs = refs[3 + 3 * n:]
        grads = [tot_ref[pl.ds(BLOB_VEC + i, 1), :] for i in range(N_VEC)]
        grads += [tot_ref[pl.ds(BLOB_BSGU, HEADS), pl.ds(0, CHUNK)], gconv_ref[...], gws_ref[...], tot_ref[pl.ds(BLOB_ADA, 3), :]]
        for i, g in enumerate(grads):
            w_ref, m_ref, v_ref = wmv[3 * i:3 * i + 3]
            d, nm, nv = _adamw_math(w_ref[...], g, m_ref[...], v_ref[...])
            outs[4 * i][...] = g
            outs[4 * i + 1][...] = d
            outs[4 * i + 2][...] = nm
            outs[4 * i + 3][...] = nv

    flat = [a for wmv in params for a in wmv]
    return pl.pallas_call(
        body, name="adamw_small",
        in_specs=[VMEM_SPEC] * (3 + len(flat)), out_specs=[VMEM_SPEC] * (4 * n),
        out_shape=[_sds(wmv[0].shape, F32) for wmv in params for _ in range(4)],
        compiler_params=_params(),
    )(tot, g_w_sgu, g_conv, *flat)


def _set_rows(buf, row, val):
    return lax.dynamic_update_slice(buf, val.astype(F32), (row, 0))


def kernel(x, c, w_ada, b_ada, g_pre, w_in, conv_w, conv_b, conv_ln_g, conv_ln_b, w_conv_out, sgu_ln_g, sgu_ln_b, w_sgu, b_sgu, w_sgu_out, w_o, g_final, loss_target, m_w_ada, m_b_ada, m_g_pre, m_w_in, m_conv_w, m_conv_b, m_conv_ln_g, m_conv_ln_b, m_w_conv_out, m_sgu_ln_g, m_sgu_ln_b, m_w_sgu, m_b_sgu, m_w_sgu_out, m_w_o, m_g_final, v_w_ada, v_b_ada, v_g_pre, v_w_in, v_conv_w, v_conv_b, v_conv_ln_g, v_conv_ln_b, v_w_conv_out, v_sgu_ln_g, v_sgu_ln_b, v_w_sgu, v_b_sgu, v_w_sgu_out, v_w_o, v_g_final):
    me = _place()
    dev, chip = _dev_of(me), _chip_of(me)
    n_ada = w_ada.shape[2]
    conv_cols = conv_w.shape[2]

    b_ada_s = lax.dynamic_slice(b_ada, (0, chip * n_ada), (1, n_ada))
    c_all, mod_all, cw_all = _setup_comm(
        jnp.broadcast_to(c, (SUB, D)), w_ada[0], b_ada_s, jnp.pad(conv_w[0], ((0, HALO - CONV_K), (0, 0))))
    mod = lax.dynamic_slice(mod_all, (0, dev * SUB, 0), (N_CHIP, 1, n_ada)).reshape(1, 3 * D)
    shift, scale, gate = mod[:, :D], mod[:, D:2 * D], mod[:, 2 * D:]
    conv_w_full = jnp.swapaxes(cw_all, 0, 1).reshape(HALO, D)[:CONV_K]

    wg_in, wg_co, wg_so, wg_o = _gather_weights(
        [w_in[0].astype(BF16), w_conv_out[0].astype(BF16), w_sgu_out[0].astype(BF16), w_o[0].astype(BF16)])

    loc = _local_step(x[0], loss_target[0], shift, scale, gate, g_pre, conv_w_full, conv_b, conv_ln_g, conv_ln_b,
                      sgu_ln_g, sgu_ln_b, w_sgu[0], b_sgu[0], g_final.reshape(1, D),
                      wg_in, wg_co.reshape(D, D), wg_so.reshape(D, D), wg_o.reshape(D, D))

    big = ["w_in", "w_conv_out", "w_sgu_out", "w_o"]
    contrib = []
    for name in big:
        g16 = loc[name]
        rows_half = (g16.shape[0] * g16.shape[1]) // (2 * N_CHIP) if name != "w_in" else g16.shape[1] // 2
        contrib.append(g16.reshape(N_CHIP, 2, rows_half, g16.shape[2]))
    full = _reduce_scatter(contrib[:1], "reduce_w_in") + _reduce_scatter(contrib[1:], "reduce_w_out")
    g_big = {name: f.reshape(2 * f.shape[1], f.shape[2]) for name, f in zip(big, full)}

    d_mod = jnp.concatenate([loc["d_shift"], loc["d_scale"], loc["d_gate"]], axis=0)
    blob = jnp.zeros((BLOB_ROWS, D), F32)
    for i, name in enumerate(["g_pre", "conv_b", "conv_ln_g", "conv_ln_b", "sgu_ln_g", "sgu_ln_b", "g_final"]):
        blob = _set_rows(blob, BLOB_VEC + i, loc[name])
    blob = _set_rows(blob, BLOB_BSGU, loc["b_sgu"])
    blob = _set_rows(blob, BLOB_CONV, loc["conv_w"])
    blob = _set_rows(blob, BLOB_ADA, d_mod)
    blob = lax.dynamic_update_slice(blob, d_mod, (BLOB_DMOD + 3 * dev, 0))
    blob = _set_rows(blob, BLOB_LOSS, loc["loss_cols"])
    tot, g_w_sgu = _sum_small([blob, loc["w_sgu"].reshape(HEADS * CHUNK, CHUNK)])

    loss = jnp.sum(tot[BLOB_LOSS])
    g_conv_s = lax.dynamic_slice(tot, (BLOB_CONV, chip * conv_cols), (CONV_K, conv_cols))
    d_mod_all = tot[BLOB_DMOD:BLOB_DMOD + 3 * N_DEV].reshape(N_DEV, 3 * D)

    ct = jnp.pad(c_all[::SUB].T, ((0, 0), (0, LANE - N_DEV))).astype(BF16)
    dm = jnp.pad(lax.dynamic_slice(d_mod_all, (0, chip * n_ada), (N_DEV, n_ada)), ((0, LANE - N_DEV), (0, 0))).astype(BF16)
    g_ada, d_ada, nm_ada, nv_ada = _adamw_ada(w_ada[0], ct, dm, m_w_ada[0], v_w_ada[0])

    upd = {}
    for name, w, m, v in [("w_in", w_in, m_w_in, v_w_in), ("w_conv_out", w_conv_out, m_w_conv_out, v_w_conv_out),
                          ("w_sgu_out", w_sgu_out, m_w_sgu_out, v_w_sgu_out), ("w_o", w_o, m_w_o, v_w_o)]:
        upd[name] = _adamw(w[0], g_big[name], m[0], v[0], "adamw_" + name)

    def wmv(w, m, v, shape):
        return tuple(a.reshape(shape) for a in (w, m, v))

    small_params = [wmv(w, m, v, (1, D)) for w, m, v in [
        (g_pre, m_g_pre, v_g_pre), (conv_b, m_conv_b, v_conv_b), (conv_ln_g, m_conv_ln_g, v_conv_ln_g),
        (conv_ln_b, m_conv_ln_b, v_conv_ln_b), (sgu_ln_g, m_sgu_ln_g, v_sgu_ln_g), (sgu_ln_b, m_sgu_ln_b, v_sgu_ln_b),
        (g_final, m_g_final, v_g_final)]]
    small_params += [wmv(b_sgu, m_b_sgu, v_b_sgu, (HEADS, CHUNK)), wmv(conv_w, m_conv_w, v_conv_w, (CONV_K, conv_cols)),
                     wmv(w_sgu, m_w_sgu, v_w_sgu, (HEADS * CHUNK, CHUNK)), wmv(b_ada, m_b_ada, v_b_ada, (3, D))]
    small_out = _adamw_small(tot, g_w_sgu, g_conv_s, small_params)

    def leaves(kind):
        vecs = [small_out[4 * i + kind] for i in range(N_VEC)]
        o_b_sgu, o_conv, o_w_sgu, o_b_ada = (small_out[4 * (N_VEC + i) + kind] for i in range(4))
        ada = (g_ada, d_ada, nm_ada, nv_ada)[kind]
        def bigk(name):
            return (g_big[name] if kind == 0 else upd[name][kind - 1])[None]
        return [ada[None], o_b_ada.reshape(1, 3 * D), vecs[0], bigk("w_in"), o_conv[None], vecs[1], vecs[2], vecs[3],
                bigk("w_conv_out"), vecs[4], vecs[5], o_w_sgu.reshape(1, HEADS, CHUNK, CHUNK), o_b_sgu[None],
                bigk("w_sgu_out"), bigk("w_o"), vecs[6].reshape(D)]

    return (loss, loc["grad_x"][None], *leaves(0), *leaves(1), *leaves(2), *leaves(3))
```

```python
import functools

import jax
import jax.numpy as jnp
from jax import lax
from jax.experimental import pallas as pl
from jax.experimental.pallas import tpu as pltpu

F32 = jnp.float32
BF16 = jnp.bfloat16
MESH = pl.DeviceIdType.MESH

D = 1024
N_SEC = 8
N_CHIP = 4
N_DEV = 8
EPS = 1e-6
CONV_K = 31
HALO = 32
CHUNK = 128
HEADS = 8
LANE = 128
SUB = 8
PACK = 16
VMEM_LIMIT = 56 * 1024 * 1024

ADAM_LR, ADAM_B1, ADAM_B2, ADAM_EPS, ADAM_WD, ADAM_STEP = 0.001, 0.9, 0.999, 1e-08, 0.01, 10

_SQRT_HALF = 0.7071067811865476
_INV_SQRT_2PI = 0.3989422804014327


def _sds(shape, dtype):
    return jax.ShapeDtypeStruct(shape, dtype)


def _params(sem=None):
    if sem is None:
        return pltpu.CompilerParams(vmem_limit_bytes=VMEM_LIMIT)
    return pltpu.CompilerParams(dimension_semantics=sem, vmem_limit_bytes=VMEM_LIMIT)


def _strips(n_rows, rows, fn):
    def step(s, carry):
        fn(pl.multiple_of(s * rows, rows))
        return carry
    lax.fori_loop(0, n_rows // rows, step, 0)


def _sigmoid(v):
    return 1.0 / (1.0 + jnp.exp(-v))


def _gelu(v):
    return 0.5 * v * (1.0 + lax.erf(v * _SQRT_HALF))


def _gelu_and_grad(v):
    cdf = 0.5 * (1.0 + lax.erf(v * _SQRT_HALF))
    return v * cdf, cdf + v * jnp.exp(-0.5 * v * v) * _INV_SQRT_2PI


def _dsilu(v, sg):
    return sg * (1.0 + v * (1.0 - sg))


def _rowmean(v):
    return jnp.mean(v, axis=-1, keepdims=True)


def _vec_spec(grid_rank):
    zeros = (0, 0)
    if grid_rank == 1:
        return pl.BlockSpec((1, D), lambda i: zeros)
    return pl.BlockSpec((1, D), lambda i, j: zeros)


def _in_proj(x, shift, scale, g_pre, wg_in):
    s_len = x.shape[0]
    tm = min(512, s_len)
    n_i = s_len // tm
    wn = wg_in.shape[2]

    def body(x_ref, sh_ref, sc_ref, g_ref, w_ref, p_ref, hb_ref):
        def strip(r0):
            xs = x_ref[pl.ds(r0, PACK), :]
            r = lax.rsqrt(_rowmean(xs * xs) + EPS)
            h = (xs * r) * g_ref[...] * (1.0 + sc_ref[...]) + sh_ref[...]
            hb_ref[pl.ds(r0, PACK), :] = h.astype(BF16)
        _strips(tm, PACK, strip)
        hb = hb_ref[...]
        for j in range(N_CHIP):
            p_ref[:, pl.ds(j * wn, wn)] = jnp.dot(hb, w_ref[j], preferred_element_type=F32).astype(BF16)

    return pl.pallas_call(
        body, name="in_proj", grid=(n_i,),
        in_specs=[pl.BlockSpec((tm, D), lambda i: (i, 0)), _vec_spec(1), _vec_spec(1), _vec_spec(1),
                  pl.BlockSpec((N_CHIP, D, wn), lambda i: (0, 0, 0), pipeline_mode=pl.Buffered(1))],
        out_specs=[pl.BlockSpec((tm, N_CHIP * wn), lambda i: (i, 0)), pl.BlockSpec((tm, D), lambda i: (i, 0))],
        out_shape=[_sds((s_len, N_SEC * D), BF16), _sds((s_len, D), BF16)],
        compiler_params=_params(("arbitrary",)),
    )(x, shift, scale, g_pre, wg_in)


def _conv_taps(win_ref, r0, lt, weight_of_offset, rows):
    lanes = pl.ds(lt * LANE, LANE)
    win = win_ref[pl.ds(r0, rows + HALO), lanes]
    n_out = rows // SUB
    acc = [jnp.zeros((SUB, LANE), F32) for _ in range(n_out)]
    for phase in range(SUB):
        offs = [o for o in weight_of_offset if o % SUB == phase]
        if not offs:
            continue
        q_max = max(o // SUB for o in offs)
        span = (n_out + q_max) * SUB
        sh = win[phase:phase + span, :]
        for o in offs:
            q = o // SUB
            w = weight_of_offset[o](lanes)
            for m in range(n_out):
                acc[m] = acc[m] + w * sh[(m + q) * SUB:(m + q + 1) * SUB, :]
    return acc


def _branch_a_fwd(p, conv_wb, conv_b, ln_g, ln_b):
    s_len = p.shape[0]
    tm = min(256, s_len)
    n_i = s_len // tm
    rows = 32

    def body(p_ref, wb_ref, cb_ref, g_ref, b_ref, ya_ref, y1_ref, abuf):
        @pl.when(pl.program_id(0) == 0)
        def _():
            abuf[pl.ds(0, HALO), :] = jnp.zeros((HALO, D), F32)

        def glu(r0):
            val = p_ref[pl.ds(r0, PACK), pl.ds(0, D)].astype(F32)
            gl = p_ref[pl.ds(r0, PACK), pl.ds(D, D)].astype(F32)
            abuf[pl.ds(HALO + r0, PACK), :] = val * _sigmoid(gl)
        _strips(tm, PACK,glu)

        taps = {HALO - (CONV_K - 1) + k: (lambda lanes, k=k: wb_ref[pl.ds(k * SUB, SUB), lanes]) for k in range(CONV_K)}

        def conv(r0):
            for lt in range(D // LANE):
                acc = _conv_taps(abuf, r0, lt, taps, rows)
                cb = cb_ref[:, pl.ds(lt * LANE, LANE)]
                for m, v in enumerate(acc):
                    y1_ref[pl.ds(r0 + m * SUB, SUB), pl.ds(lt * LANE, LANE)] = v + cb
        _strips(tm, rows, conv)

        def norm(r0):
            y1 = y1_ref[pl.ds(r0, PACK), :]
            mu = _rowmean(y1)
            yc = y1 - mu
            rstd = lax.rsqrt(_rowmean(yc * yc) + EPS)
            l1 = (yc * rstd) * g_ref[...] + b_ref[...]
            z = p_ref[pl.ds(r0, PACK), pl.ds(2 * D, D)].astype(F32)
            ya_ref[pl.ds(r0, PACK), :] = ((l1 * _sigmoid(l1)) * (z * _sigmoid(z))).astype(BF16)
        _strips(tm, PACK,norm)

        abuf[pl.ds(0, HALO), :] = abuf[pl.ds(tm, HALO), :]

    return pl.pallas_call(
        body, name="branch_a_fwd", grid=(n_i,),
        in_specs=[pl.BlockSpec((tm, 3 * D), lambda i: (i, 0)),
                  pl.BlockSpec((CONV_K * SUB, D), lambda i: (0, 0)), _vec_spec(1), _vec_spec(1), _vec_spec(1)],
        out_specs=[pl.BlockSpec((tm, D), lambda i: (i, 0)), pl.BlockSpec((tm, D), lambda i: (i, 0))],
        out_shape=[_sds((s_len, D), BF16), _sds((s_len, D), F32)],
        scratch_shapes=[pltpu.VMEM((tm + HALO, D), F32)],
        compiler_params=_params(("arbitrary",)),
    )(p, conv_wb, conv_b, ln_g, ln_b)


def _branch_b_fwd(p, wt, bias_full, ln_g, ln_b):
    s_len = p.shape[0]
    tm = min(256, s_len)
    n_i = s_len // tm

    def body(p_ref, wt_ref, bias_ref, g_ref, b_ref, yb_ref, vb, sbuf):
        def norm(r0):
            gv = _gelu(p_ref[pl.ds(r0, PACK), pl.ds(D, D)].astype(F32))
            mu = _rowmean(gv)
            vc = gv - mu
            rstd = lax.rsqrt(_rowmean(vc * vc) + EPS)
            vb[pl.ds(r0, PACK), :] = ((vc * rstd) * g_ref[...] + b_ref[...]).astype(BF16)
        _strips(tm, PACK,norm)

        for ck in range(tm // CHUNK):
            for h in range(HEADS):
                blk = (pl.ds(ck * CHUNK, CHUNK), pl.ds(h * LANE, LANE))
                sbuf[blk] = jnp.dot(wt_ref[h], vb[blk], preferred_element_type=F32) + bias_ref[:, pl.ds(h * LANE, LANE)]

        def gate(r0):
            u = _gelu(p_ref[pl.ds(r0, PACK), pl.ds(0, D)].astype(F32))
            z = p_ref[pl.ds(r0, PACK), pl.ds(2 * D, D)].astype(F32)
            yb_ref[pl.ds(r0, PACK), :] = (u * sbuf[pl.ds(r0, PACK), :] * (z * _sigmoid(z))).astype(BF16)
        _strips(tm, PACK,gate)

    return pl.pallas_call(
        body, name="branch_b_fwd", grid=(n_i,),
        in_specs=[pl.BlockSpec((tm, 3 * D), lambda i: (i, 1)),
                  pl.BlockSpec((HEADS, CHUNK, CHUNK), lambda i: (0, 0, 0)),
                  pl.BlockSpec((CHUNK, D), lambda i: (0, 0)), _vec_spec(1), _vec_spec(1)],
        out_specs=pl.BlockSpec((tm, D), lambda i: (i, 0)),
        out_shape=_sds((s_len, D), BF16),
        scratch_shapes=[pltpu.VMEM((tm, D), BF16), pltpu.VMEM((tm, D), F32)],
        compiler_params=_params(("arbitrary",)),
    )(p, wt, bias_full, ln_g, ln_b)


def _dot_t(a, b):
    return lax.dot_general(a, b, (((1,), (1,)), ((), ())), preferred_element_type=F32)


def _out_proj(p, ya_in, yb_in, x, target, gate, g_final, w_co, w_so, w_o):
    s_len = x.shape[0]
    tm = min(256, s_len)
    n_i = s_len // tm

    def body(pg_ref, ya_ref, yb_ref, x_ref, t_ref, gate_ref, gf_ref, wco_ref, wso_ref, wo_ref,
             dx2_ref, dya_ref, dyb_ref, dp_ref, mb_ref, dob_ref, dyab_ref, dybb_ref, sums_ref):
        @pl.when(pl.program_id(0) == 0)
        def _():
            sums_ref[...] = jnp.zeros((SUB, D), F32)

        y_a = jnp.dot(ya_ref[...], wco_ref[...], preferred_element_type=F32)
        y_b = jnp.dot(yb_ref[...], wso_ref[...], preferred_element_type=F32)
        ga = _sigmoid(pg_ref[:, pl.ds(0, D)].astype(F32))
        gb = _sigmoid(pg_ref[:, pl.ds(D, D)].astype(F32))
        mb = (ga * y_a + gb * y_b).astype(BF16)
        mb_ref[...] = mb
        o = jnp.dot(mb, wo_ref[...], preferred_element_type=F32)
        x2 = x_ref[...] + gate_ref[...] * o
        r2 = lax.rsqrt(_rowmean(x2 * x2) + EPS)
        xh = x2 * r2
        e = xh * gf_ref[...] - t_ref[...]
        dy = e * (1.0 / D)
        dxh = dy * gf_ref[...]
        dx2 = r2 * (dxh - xh * _rowmean(dxh * xh))
        dx2_ref[...] = dx2
        sums_ref[pl.ds(0, 1), :] += jnp.sum(dy * xh, axis=0, keepdims=True)
        sums_ref[pl.ds(1, 1), :] += jnp.sum(dx2 * o, axis=0, keepdims=True)
        sums_ref[pl.ds(2, 1), :] += jnp.sum(e * e, axis=0, keepdims=True) * (0.5 / D)
        dob = (gate_ref[...] * dx2).astype(BF16)
        dob_ref[...] = dob
        dm = _dot_t(dob, wo_ref[...])
        dy_a = (ga * dm).astype(BF16)
        dy_b = (gb * dm).astype(BF16)
        dyab_ref[...] = dy_a
        dybb_ref[...] = dy_b
        dp_ref[:, pl.ds(0, D)] = (dm * y_a * ga * (1.0 - ga)).astype(BF16)
        dp_ref[:, pl.ds(D, D)] = (dm * y_b * gb * (1.0 - gb)).astype(BF16)
        dya_ref[...] = _dot_t(dy_a, wco_ref[...])
        dyb_ref[...] = _dot_t(dy_b, wso_ref[...])

    tile = pl.BlockSpec((tm, D), lambda i: (i, 0))
    wspec = pl.BlockSpec((D, D), lambda i: (0, 0))
    return pl.pallas_call(
        body, name="out_proj", grid=(n_i,),
        in_specs=[pl.BlockSpec((tm, 2 * D), lambda i: (i, 3)), tile, tile, tile, tile, _vec_spec(1), _vec_spec(1),
                  wspec, wspec, wspec],
        out_specs=[tile, tile, tile, pl.BlockSpec((tm, 2 * D), lambda i: (i, 3)), tile, tile, tile, tile,
                   pl.BlockSpec((SUB, D), lambda i: (0, 0))],
        out_shape=[_sds((s_len, D), F32), _sds((s_len, D), F32), _sds((s_len, D), F32), _sds((s_len, N_SEC * D), BF16),
                   _sds((s_len, D), BF16), _sds((s_len, D), BF16), _sds((s_len, D), BF16), _sds((s_len, D), BF16),
                   _sds((SUB, D), F32)],
        compiler_params=_params(("arbitrary",)),
    )(p, ya_in, yb_in, x, target, gate, g_final, w_co, w_so, w_o)


A_STATS_ROWS = 8 + HALO


def _branch_a_bwd(p, y1, dya_in, dp, conv_wb, ln_g, ln_b):
    s_len = p.shape[0]
    tm = min(256, s_len)
    n_i = s_len // tm
    rows = 32
    n_out = rows // SUB

    def tile_of(i):
        return n_i - 1 - i

    def body(p_ref, y1_ref, dya_ref, dp_in, wb_ref, g_ref, b_ref, dp_ref, st_ref, dybuf, acc8, tapacc):
        del dp_in
        i = pl.program_id(0)

        @pl.when(i == 0)
        def _():
            dybuf[pl.ds(tm, HALO), :] = jnp.zeros((HALO, D), F32)
            st_ref[...] = jnp.zeros((A_STATS_ROWS, D), F32)
            acc8[...] = jnp.zeros((3 * PACK, D), F32)
            tapacc[...] = jnp.zeros((CONV_K * SUB, D), F32)

        def norm_bwd(r0):
            y1 = y1_ref[pl.ds(r0, PACK), :]
            mu = _rowmean(y1)
            yc = y1 - mu
            rstd = lax.rsqrt(_rowmean(yc * yc) + EPS)
            n1 = yc * rstd
            l1 = n1 * g_ref[...] + b_ref[...]
            sg = _sigmoid(l1)
            z = p_ref[pl.ds(r0, PACK), pl.ds(2 * D, D)].astype(F32)
            sz = _sigmoid(z)
            dya = dya_ref[pl.ds(r0, PACK), :]
            dp_ref[pl.ds(r0, PACK), pl.ds(2 * D, D)] = (dya * (l1 * sg) * _dsilu(z, sz)).astype(BF16)
            dl1 = dya * (z * sz) * _dsilu(l1, sg)
            acc8[pl.ds(0, PACK), :] += dl1 * n1
            acc8[pl.ds(PACK, PACK), :] += dl1
            dn1 = dl1 * g_ref[...]
            dy1 = rstd * (dn1 - _rowmean(dn1) - n1 * _rowmean(dn1 * n1))
            acc8[pl.ds(2 * PACK, PACK), :] += dy1
            dybuf[pl.ds(r0, PACK), :] = dy1
        _strips(tm, PACK,norm_bwd)

        def conv_bwd(r0):
            for lt in range(D // LANE):
                lanes = pl.ds(lt * LANE, LANE)
                glanes = pl.ds(D + lt * LANE, LANE)
                win = dybuf[pl.ds(r0, rows + HALO), lanes]
                sg16, a16 = [], []
                for h in range(rows // PACK):
                    rr = pl.ds(r0 + h * PACK, PACK)
                    s = _sigmoid(p_ref[rr, glanes].astype(F32))
                    sg16.append(s)
                    a16.append(p_ref[rr, lanes].astype(F32) * s)
                a = [a16[m // 2][(m % 2) * SUB:(m % 2 + 1) * SUB, :] for m in range(n_out)]
                da = [jnp.zeros((SUB, LANE), F32) for _ in range(n_out)]
                for phase in range(SUB):
                    offs = [o for o in range(CONV_K) if o % SUB == phase]
                    q_max = max(o // SUB for o in offs)
                    sh = win[phase:phase + (n_out + q_max) * SUB, :]
                    for o in offs:
                        k, q = CONV_K - 1 - o, o // SUB
                        w = wb_ref[pl.ds(k * SUB, SUB), lanes]
                        part = None
                        for m in range(n_out):
                            s = sh[(m + q) * SUB:(m + q + 1) * SUB, :]
                            da[m] = da[m] + w * s
                            part = a[m] * s if part is None else part + a[m] * s
                        tapacc[pl.ds(k * SUB, SUB), lanes] += part
                for h in range(rows // PACK):
                    rr = pl.ds(r0 + h * PACK, PACK)
                    da16 = jnp.concatenate(da[2 * h:2 * h + 2], axis=0)
                    dp_ref[rr, lanes] = (da16 * sg16[h]).astype(BF16)
                    dp_ref[rr, glanes] = (da16 * a16[h] * (1.0 - sg16[h])).astype(BF16)
        _strips(tm, rows, conv_bwd)

        dybuf[pl.ds(tm, HALO), :] = dybuf[pl.ds(0, HALO), :]

        @pl.when(i == n_i - 1)
        def _():
            for j in range(3):
                st_ref[pl.ds(j, 1), :] = jnp.sum(acc8[pl.ds(j * PACK, PACK), :], axis=0, keepdims=True)
            for k in range(CONV_K):
                st_ref[pl.ds(SUB + k, 1), :] = jnp.sum(tapacc[pl.ds(k * SUB, SUB), :], axis=0, keepdims=True)

    return pl.pallas_call(
        body, name="branch_a_bwd", grid=(n_i,),
        in_specs=[pl.BlockSpec((tm, 3 * D), lambda i: (tile_of(i), 0)),
                  pl.BlockSpec((tm, D), lambda i: (tile_of(i), 0)),
                  pl.BlockSpec((tm, D), lambda i: (tile_of(i), 0)),
                  pl.BlockSpec(memory_space=pl.ANY),
                  pl.BlockSpec((CONV_K * SUB, D), lambda i: (0, 0)), _vec_spec(1), _vec_spec(1)],
        out_specs=[pl.BlockSpec((tm, 3 * D), lambda i: (tile_of(i), 0)),
                   pl.BlockSpec((A_STATS_ROWS, D), lambda i: (0, 0))],
        out_shape=[_sds(dp.shape, BF16), _sds((A_STATS_ROWS, D), F32)],
        scratch_shapes=[pltpu.VMEM((tm + HALO, D), F32), pltpu.VMEM((3 * PACK, D), F32), pltpu.VMEM((CONV_K * SUB, D), F32)],
        input_output_aliases={3: 0},
        compiler_params=_params(("arbitrary",)),
    )(p, y1, dya_in, dp, conv_wb, ln_g, ln_b)


def _branch_b_bwd(p, dyb_in, dp, wt, wtt, bias_full, ln_g, ln_b):
    s_len = p.shape[0]
    tm = min(256, s_len)
    n_i = s_len // tm

    def body(p_ref, dyb_ref, dp_in, wt_ref, wtt_ref, bias_ref, g_ref, b_ref, dp_ref, st_ref, gbt_ref, gw_ref,
             vb, n2buf, rstdbuf, sbuf, dsb, dvbuf, acc8, gb_ref, dgbuf):
        del dp_in
        i = pl.program_id(0)

        @pl.when(i == 0)
        def _():
            st_ref[...] = jnp.zeros((SUB, D), F32)
            gbt_ref[...] = jnp.zeros((CHUNK, LANE), F32)
            gb_ref[...] = jnp.zeros((CHUNK, D), F32)
            gw_ref[...] = jnp.zeros((HEADS, CHUNK, CHUNK), F32)
            acc8[...] = jnp.zeros((2 * PACK, D), F32)

        def norm(r0):
            gv, dgv = _gelu_and_grad(p_ref[pl.ds(r0, PACK), pl.ds(D, D)].astype(F32))
            dgbuf[pl.ds(r0, PACK), :] = dgv
            mu = _rowmean(gv)
            vc = gv - mu
            rstd = lax.rsqrt(_rowmean(vc * vc) + EPS)
            n2 = vc * rstd
            n2buf[pl.ds(r0, PACK), :] = n2
            rstdbuf[pl.ds(r0, PACK), :] = jnp.broadcast_to(rstd, (PACK, LANE))
            vb[pl.ds(r0, PACK), :] = (n2 * g_ref[...] + b_ref[...]).astype(BF16)
        _strips(tm, PACK,norm)

        for ck in range(tm // CHUNK):
            for h in range(HEADS):
                blk = (pl.ds(ck * CHUNK, CHUNK), pl.ds(h * LANE, LANE))
                sbuf[blk] = jnp.dot(wt_ref[h], vb[blk], preferred_element_type=F32) + bias_ref[:, pl.ds(h * LANE, LANE)]

        def gate_bwd(r0):
            pu = p_ref[pl.ds(r0, PACK), pl.ds(0, D)].astype(F32)
            u, du = _gelu_and_grad(pu)
            z = p_ref[pl.ds(r0, PACK), pl.ds(2 * D, D)].astype(F32)
            sg = _sigmoid(z)
            sz = z * sg
            s = sbuf[pl.ds(r0, PACK), :]
            dyb = dyb_ref[pl.ds(r0, PACK), :]
            ds = dyb * u * sz
            dsb[pl.ds(r0, PACK), :] = ds.astype(BF16)
            gb_ref[pl.ds(pl.multiple_of(r0 % CHUNK, PACK), PACK), :] += ds
            dp_ref[pl.ds(r0, PACK), pl.ds(0, D)] = (dyb * s * sz * du).astype(BF16)
            dp_ref[pl.ds(r0, PACK), pl.ds(2 * D, D)] = (dyb * u * s * _dsilu(z, sg)).astype(BF16)
        _strips(tm, PACK,gate_bwd)

        for ck in range(tm // CHUNK):
            for h in range(HEADS):
                blk = (pl.ds(ck * CHUNK, CHUNK), pl.ds(h * LANE, LANE))
                d_s = dsb[blk]
                dvbuf[blk] = jnp.dot(wtt_ref[h], d_s, preferred_element_type=F32)
                gw_ref[h] += _dot_t(d_s, vb[blk])

        def norm_bwd(r0):
            dv = dvbuf[pl.ds(r0, PACK), :]
            n2 = n2buf[pl.ds(r0, PACK), :]
            rstd = rstdbuf[pl.ds(r0, PACK), pl.ds(0, 1)]
            acc8[pl.ds(0, PACK), :] += dv * n2
            acc8[pl.ds(PACK, PACK), :] += dv
            dn2 = dv * g_ref[...]
            dgv = rstd * (dn2 - _rowmean(dn2) - n2 * _rowmean(dn2 * n2))
            dp_ref[pl.ds(r0, PACK), pl.ds(D, D)] = (dgv * dgbuf[pl.ds(r0, PACK), :]).astype(BF16)
        _strips(tm, PACK,norm_bwd)

        @pl.when(i == n_i - 1)
        def _():
            for j in range(2):
                st_ref[pl.ds(j, 1), :] = jnp.sum(acc8[pl.ds(j * PACK, PACK), :], axis=0, keepdims=True)
            row = lax.broadcasted_iota(jnp.int32, (CHUNK, CHUNK), 0)
            col = lax.broadcasted_iota(jnp.int32, (CHUNK, CHUNK), 1)
            for h in range(HEADS):
                gw_ref[h] = jnp.where(row >= col, gw_ref[h], 0.0)
            lane = lax.broadcasted_iota(jnp.int32, (CHUNK, LANE), 1)
            gbt = jnp.zeros((CHUNK, LANE), F32)
            for h in range(HEADS):
                gbt = jnp.where(lane == h, jnp.sum(gb_ref[:, pl.ds(h * LANE, LANE)], axis=1, keepdims=True), gbt)
            gbt_ref[...] = gbt

    wspec = pl.BlockSpec((HEADS, CHUNK, CHUNK), lambda i: (0, 0, 0))
    return pl.pallas_call(
        body, name="branch_b_bwd", grid=(n_i,),
        in_specs=[pl.BlockSpec((tm, 3 * D), lambda i: (i, 1)), pl.BlockSpec((tm, D), lambda i: (i, 0)),
                  pl.BlockSpec(memory_space=pl.ANY), wspec, wspec,
                  pl.BlockSpec((CHUNK, D), lambda i: (0, 0)), _vec_spec(1), _vec_spec(1)],
        out_specs=[pl.BlockSpec((tm, 3 * D), lambda i: (i, 1)), pl.BlockSpec((SUB, D), lambda i: (0, 0)),
                   pl.BlockSpec((CHUNK, LANE), lambda i: (0, 0)), wspec],
        out_shape=[_sds(dp.shape, BF16), _sds((SUB, D), F32), _sds((CHUNK, LANE), F32), _sds((HEADS, CHUNK, CHUNK), F32)],
        scratch_shapes=[pltpu.VMEM((tm, D), BF16), pltpu.VMEM((tm, D), F32), pltpu.VMEM((tm, LANE), F32),
                        pltpu.VMEM((tm, D), F32), pltpu.VMEM((tm, D), BF16), pltpu.VMEM((tm, D), F32),
                        pltpu.VMEM((2 * PACK, D), F32), pltpu.VMEM((CHUNK, D), F32), pltpu.VMEM((tm, D), F32)],
        input_output_aliases={2: 0},
        compiler_params=_params(("arbitrary",)),
    )(p, dyb_in, dp, wt, wtt, bias_full, ln_g, ln_b)


def _in_proj_bwd(dp, wg_in, x, dx2, shift, scale, g_pre):
    del shift
    s_len = x.shape[0]
    tm = min(512, s_len)
    n_i = s_len // tm
    wn = wg_in.shape[2]

    def body(dp_ref, w_ref, x_ref, dx2_ref, sc_ref, g_ref, gx_ref, st_ref, acc, acc8):
        i = pl.program_id(0)

        @pl.when(i == 0)
        def _():
            st_ref[...] = jnp.zeros((SUB, D), F32)
            acc8[...] = jnp.zeros((3 * PACK, D), F32)

        dh = _dot_t(dp_ref[:, pl.ds(0, wn)], w_ref[0])
        for j in range(1, N_CHIP):
            dh = dh + _dot_t(dp_ref[:, pl.ds(j * wn, wn)], w_ref[j])
        acc[...] = dh

        def strip(r0):
            xs = x_ref[pl.ds(r0, PACK), :]
            r = lax.rsqrt(_rowmean(xs * xs) + EPS)
            xn = xs * r
            dhs = acc[pl.ds(r0, PACK), :]
            acc8[pl.ds(0, PACK), :] += dhs
            acc8[pl.ds(PACK, PACK), :] += dhs * (xn * g_ref[...])
            dhp = dhs * (1.0 + sc_ref[...])
            acc8[pl.ds(2 * PACK, PACK), :] += dhp * xn
            dxn = dhp * g_ref[...]
            gx_ref[pl.ds(r0, PACK), :] = dx2_ref[pl.ds(r0, PACK), :] + r * (dxn - xn * _rowmean(dxn * xn))
        _strips(tm, PACK, strip)

        @pl.when(i == n_i - 1)
        def _():
            for k in range(3):
                st_ref[pl.ds(k, 1), :] = jnp.sum(acc8[pl.ds(k * PACK, PACK), :], axis=0, keepdims=True)

    tile = pl.BlockSpec((tm, D), lambda i: (i, 0))
    return pl.pallas_call(
        body, name="in_proj_bwd", grid=(n_i,),
        in_specs=[pl.BlockSpec((tm, N_CHIP * wn), lambda i: (i, 0)),
                  pl.BlockSpec((N_CHIP, D, wn), lambda i: (0, 0, 0), pipeline_mode=pl.Buffered(1)),
                  tile, tile, _vec_spec(1), _vec_spec(1)],
        out_specs=[tile, pl.BlockSpec((SUB, D), lambda i: (0, 0))],
        out_shape=[_sds((s_len, D), F32), _sds((SUB, D), F32)],
        scratch_shapes=[pltpu.VMEM((tm, D), F32), pltpu.VMEM((3 * PACK, D), F32)],
        compiler_params=_params(("arbitrary",)),
    )(dp, wg_in, x, dx2, scale, g_pre)


def _grad_matmul(a, b, name):
    s_len, n = b.shape
    cb = min(2 * D, n)
    tn = 512
    per = cb // tn

    def body(a_ref, b_ref, ob_ref):
        ob_ref[0] = lax.dot_general(a_ref[...], b_ref[...], (((0,), (0,)), ((), ())),
                                    preferred_element_type=F32).astype(BF16)

    return pl.pallas_call(
        body, name=name, grid=(n // tn,),
        in_specs=[pl.BlockSpec((s_len, D), lambda j: (0, 0), pipeline_mode=pl.Buffered(1)),
                  pl.BlockSpec((s_len, tn), lambda j: (0, j))],
        out_specs=pl.BlockSpec((1, D, tn), lambda j: (j // per, 0, j % per)),
        out_shape=_sds((n // cb, D, cb), BF16),
        compiler_params=_params(("arbitrary",)),
    )(a, b)


def _local_step(x, target, shift, scale, gate, g_pre, conv_w_full, conv_b, conv_ln_g, conv_ln_b,
                sgu_ln_g, sgu_ln_b, w_sgu, b_sgu, g_final, wg_in, w_co, w_so, w_o):
    conv_wb = jnp.repeat(conv_w_full, SUB, axis=0)
    causal = jnp.tril(jnp.ones((CHUNK, CHUNK), dtype=bool))
    wt = jnp.where(causal[None], w_sgu, 0.0).astype(BF16)
    wtt = jnp.swapaxes(wt, 1, 2)
    bias_full = jnp.repeat(b_sgu.T, LANE, axis=1)

    p, hb = _in_proj(x, shift, scale, g_pre, wg_in)
    ya_in, y1 = _branch_a_fwd(p, conv_wb, conv_b, conv_ln_g, conv_ln_b)
    yb_in = _branch_b_fwd(p, wt, bias_full, sgu_ln_g, sgu_ln_b)
    dx2, dya_in, dyb_in, dp, mb, dob, dyab, dybb, sums_o = _out_proj(
        p, ya_in, yb_in, x, target, gate, g_final, w_co, w_so, w_o)
    dp, st_a = _branch_a_bwd(p, y1, dya_in, dp, conv_wb, conv_ln_g, conv_ln_b)
    dp, st_b, gbt, gws = _branch_b_bwd(p, dyb_in, dp, wt, wtt, bias_full, sgu_ln_g, sgu_ln_b)
    grad_x, st_i = _in_proj_bwd(dp, wg_in, x, dx2, shift, scale, g_pre)
    gw_o = _grad_matmul(mb, dob, "grad_w_o")
    gw_co = _grad_matmul(ya_in, dyab, "grad_w_conv_out")
    gw_so = _grad_matmul(yb_in, dybb, "grad_w_sgu_out")
    return dict(
        grad_x=grad_x, loss_cols=sums_o[2:3], g_final=sums_o[0:1], d_gate=sums_o[1:2],
        d_shift=st_i[0:1], d_scale=st_i[1:2], g_pre=st_i[2:3],
        conv_ln_g=st_a[0:1], conv_ln_b=st_a[1:2], conv_b=st_a[2:3], conv_w=st_a[SUB:SUB + CONV_K],
        sgu_ln_g=st_b[0:1], sgu_ln_b=st_b[1:2], b_sgu=gbt[:, :HEADS].T, w_sgu=gws,
        hb=hb, dp=dp, w_o=gw_o, w_conv_out=gw_co, w_sgu_out=gw_so)


ANY_SPEC = pl.BlockSpec(memory_space=pl.ANY)
VMEM_SPEC = pl.BlockSpec(memory_space=pltpu.VMEM)


def _place():
    return lax.axis_index("x"), lax.axis_index("y"), lax.axis_index("c")


def _peer(k):
    x, y, c = _place()
    return (1 - x if k & 4 else x, 1 - y if k & 2 else y, 1 - c if k & 1 else c)


def _dev_of(p):
    return 4 * p[0] + 2 * p[1] + p[2]


def _chip_of(p):
    return 2 * p[0] + p[1]


def _rdma(src, dst, send_sem, recv_sem, to):
    return pltpu.make_async_remote_copy(src_ref=src, dst_ref=dst, send_sem=send_sem, recv_sem=recv_sem,
                                        device_id=to, device_id_type=MESH)


CHIP_PEERS = (2, 4, 6)
ALL_PEERS = tuple(range(1, N_DEV))
SIBLING = 1


def _setup_comm(c8, w_ada_s, b_ada_s, convw_s):
    n_mod = w_ada_s.shape[1]
    rows = SUB * N_DEV

    def body(c8_ref, wada_ref, bada_ref, cw_ref, call_ref, mod_ref, cwall_ref, csend, crecv, wsend, wrecv, msend, mrecv):
        me = _place()
        dev, chip = _dev_of(me), _chip_of(me)

        def c_rows(d):
            return call_ref.at[pl.ds(pl.multiple_of(d * SUB, SUB), SUB), :]

        call_ref[pl.ds(pl.multiple_of(dev * SUB, SUB), SUB), :] = c8_ref[...]
        cwall_ref[chip] = cw_ref[...]
        c_out = [_rdma(c8_ref, c_rows(dev), csend.at[k], crecv.at[k], _peer(k)) for k in ALL_PEERS]
        w_out = [_rdma(cw_ref, cwall_ref.at[chip], wsend.at[k], wrecv.at[k], _peer(k)) for k in CHIP_PEERS]
        for cp in c_out + w_out:
            cp.start()
        for k in ALL_PEERS:
            _rdma(c8_ref, c_rows(_dev_of(_peer(k))), csend.at[k], crecv.at[k], _peer(k)).wait_recv()
        part = jnp.dot(call_ref[...].astype(BF16), wada_ref[...].astype(BF16), preferred_element_type=F32) + bada_ref[...]
        mod_ref[chip] = part
        m_out = [_rdma(mod_ref.at[chip], mod_ref.at[chip], msend.at[k], mrecv.at[k], _peer(k)) for k in CHIP_PEERS]
        for cp in m_out:
            cp.start()
        for k in CHIP_PEERS:
            pc = _chip_of(_peer(k))
            _rdma(cw_ref, cwall_ref.at[pc], wsend.at[k], wrecv.at[k], _peer(k)).wait_recv()
            _rdma(mod_ref.at[pc], mod_ref.at[pc], msend.at[k], mrecv.at[k], _peer(k)).wait_recv()
        for cp in c_out + w_out + m_out:
            cp.wait_send()

    return pl.pallas_call(
        body, name="setup_comm",
        in_specs=[VMEM_SPEC] * 4, out_specs=[VMEM_SPEC] * 3,
        out_shape=[_sds((rows, D), F32), _sds((N_CHIP, rows, n_mod), F32), _sds((N_CHIP,) + convw_s.shape, F32)],
        scratch_shapes=[pltpu.SemaphoreType.DMA((N_DEV,))] * 6,
        compiler_params=_params(),
    )(c8, w_ada_s, b_ada_s, convw_s)


def _gather_weights(shards):
    n = len(shards)

    def body(*refs):
        ins, outs = refs[:n], refs[n:2 * n]
        lsem, isend, irecv, dsend, drecv = refs[2 * n:]
        me = _place()
        chip, c = _chip_of(me), me[2]
        local = [pltpu.make_async_copy(ins[t], outs[t].at[chip], lsem.at[t]) for t in range(n)]
        for cp in local:
            cp.start()

        def half(t, which):
            hr = shards[t].shape[0] // 2
            return pl.ds(pl.multiple_of(which * hr, hr), hr)

        sends = []
        for t in range(n):
            for j, k in enumerate(CHIP_PEERS):
                cp = _rdma(ins[t].at[half(t, c)], outs[t].at[chip, half(t, c)], isend.at[t, j], irecv.at[t, j], _peer(k))
                cp.start()
                sends.append(cp)
        for t in range(n):
            for j, k in enumerate(CHIP_PEERS):
                blk = outs[t].at[_chip_of(_peer(k)), half(t, c)]
                _rdma(blk, blk, isend.at[t, j], irecv.at[t, j], _peer(k)).wait_recv()
                cp = _rdma(blk, blk, dsend.at[t, j], drecv.at[t, j], _peer(SIBLING))
                cp.start()
                sends.append(cp)
        for t in range(n):
            for j, k in enumerate(CHIP_PEERS):
                blk = outs[t].at[_chip_of(_peer(k)), half(t, 1 - c)]
                _rdma(blk, blk, dsend.at[t, j], drecv.at[t, j], _peer(SIBLING)).wait_recv()
        for cp in sends:
            cp.wait_send()
        for cp in local:
            cp.wait()

    return pl.pallas_call(
        body, name="gather_weights",
        in_specs=[VMEM_SPEC] * n, out_specs=[VMEM_SPEC] * n,
        out_shape=[_sds((N_CHIP,) + s.shape, s.dtype) for s in shards],
        scratch_shapes=[pltpu.SemaphoreType.DMA((n,))] + [pltpu.SemaphoreType.DMA((n, len(CHIP_PEERS)))] * 4,
        compiler_params=_params(),
    )(*shards)


def _reduce_scatter(grads, name):
    n = len(grads)
    shapes = [g.shape[2:] for g in grads]

    def body(*refs):
        ins, outs = refs[:n], refs[n:2 * n]
        pbufs, rbufs, accs = refs[2 * n:3 * n], refs[3 * n:4 * n], refs[4 * n:5 * n]
        psend, precv, csend, crecv, fsend, frecv = refs[5 * n:]
        me = _place()
        chip, c = _chip_of(me), me[2]
        sib = _peer(SIBLING)

        def to_sibling(t, d):
            return _rdma(ins[t].at[d, 1 - c], pbufs[t].at[d], psend.at[t, d], precv.at[t, d], sib)

        sends = []
        for t in range(n):
            for d in range(N_CHIP):
                cp = to_sibling(t, d)
                cp.start()
                sends.append(cp)
        for j in (1, 2, 3, 0):
            d = jnp.bitwise_xor(chip, j)
            for t in range(n):
                to_sibling(t, d).wait_recv()

                def pair_sum(r0, t=t, d=d, j=j):
                    rows = pl.ds(r0, PACK)
                    s = ins[t][d, c, rows, :].astype(F32) + pbufs[t][d, rows, :].astype(F32)
                    if j == 0:
                        accs[t][rows, :] = s
                    else:
                        pbufs[t][d, rows, :] = s.astype(BF16)
                _strips(shapes[t][0], PACK, pair_sum)
                if j:
                    cp = _rdma(pbufs[t].at[d], rbufs[t].at[j - 1], csend.at[t, j], crecv.at[t, j], _peer(2 * j))
                    cp.start()
                    sends.append(cp)
        for t in range(n):
            for j in (1, 2, 3):
                blk = rbufs[t].at[j - 1]
                _rdma(blk, blk, csend.at[t, j], crecv.at[t, j], _peer(2 * j)).wait_recv()

            def total(r0, t=t):
                rows = pl.ds(r0, PACK)
                s = accs[t][rows, :] + rbufs[t][0, rows, :].astype(F32)
                s = s + rbufs[t][1, rows, :].astype(F32)
                outs[t][c, rows, :] = s + rbufs[t][2, rows, :].astype(F32)
            _strips(shapes[t][0], PACK, total)
            cp = _rdma(outs[t].at[c], outs[t].at[c], fsend.at[t], frecv.at[t], sib)
            cp.start()
            sends.append(cp)
        for t in range(n):
            blk = outs[t].at[1 - c]
            _rdma(blk, blk, fsend.at[t], frecv.at[t], sib).wait_recv()
        for cp in sends:
            cp.wait_send()

    return pl.pallas_call(
        body, name=name,
        in_specs=[VMEM_SPEC] * n, out_specs=[VMEM_SPEC] * n,
        out_shape=[_sds((2,) + s, F32) for s in shapes],
        scratch_shapes=([pltpu.VMEM((N_CHIP,) + s, BF16) for s in shapes] + [pltpu.VMEM((N_CHIP - 1,) + s, BF16) for s in shapes]
                        + [pltpu.VMEM(s, F32) for s in shapes]
                        + [pltpu.SemaphoreType.DMA((n, N_CHIP))] * 4 + [pltpu.SemaphoreType.DMA((n,))] * 2),
        compiler_params=_params(),
    )(*grads)


def _reduce_phases(shapes, ins, outs, pbufs, rbufs, accs, sems):
    n = len(shapes)
    psend, precv, csend, crecv, fsend, frecv = sems
    me = _place()
    chip, c = _chip_of(me), me[2]
    sib = _peer(SIBLING)

    def to_sibling(t, d):
        return _rdma(ins[t].at[d, 1 - c], pbufs[t].at[d], psend.at[t, d], precv.at[t, d], sib)

    def to_chip(t, j):
        return _rdma(pbufs[t].at[jnp.bitwise_xor(chip, j)], rbufs[t].at[j - 1], csend.at[t, j], crecv.at[t, j], _peer(2 * j))

    def finished(t):
        return _rdma(outs[t].at[c], outs[t].at[c], fsend.at[t], frecv.at[t], sib)

    def phase_a():
        for t in range(n):
            for d in range(N_CHIP):
                to_sibling(t, d).start()

    def phase_b():
        for j in (1, 2, 3, 0):
            d = jnp.bitwise_xor(chip, j)
            for t in range(n):
                to_sibling(t, d).wait_recv()

                def pair_sum(r0, t=t, d=d, j=j):
                    rows = pl.ds(r0, PACK)
                    s = ins[t][d, c, rows, :].astype(F32) + pbufs[t][d, rows, :].astype(F32)
                    if j == 0:
                        accs[t][rows, :] = s
                    else:
                        pbufs[t][d, rows, :] = s.astype(BF16)
                _strips(shapes[t][0], PACK, pair_sum)
                if j:
                    to_chip(t, j).start()

    def phase_c():
        for t in range(n):
            for j in (1, 2, 3):
                blk = rbufs[t].at[j - 1]
                _rdma(blk, blk, csend.at[t, j], crecv.at[t, j], _peer(2 * j)).wait_recv()

            def total(r0, t=t):
                rows = pl.ds(r0, PACK)
                s = accs[t][rows, :] + rbufs[t][0, rows, :].astype(F32)
                s = s + rbufs[t][1, rows, :].astype(F32)
                outs[t][c, rows, :] = s + rbufs[t][2, rows, :].astype(F32)
            _strips(shapes[t][0], PACK, total)
            finished(t).start()

    def phase_d():
        for t in range(n):
            blk = outs[t].at[1 - c]
            _rdma(blk, blk, fsend.at[t], frecv.at[t], sib).wait_recv()
        for t in range(n):
            for d in range(N_CHIP):
                to_sibling(t, d).wait_send()
            for j in (1, 2, 3):
                to_chip(t, j).wait_send()
            finished(t).wait_send()

    return phase_a, phase_b, phase_c, phase_d


def _reduce_scratch(shapes):
    n = len(shapes)
    return ([pltpu.VMEM((N_CHIP,) + s, BF16) for s in shapes] + [pltpu.VMEM((N_CHIP - 1,) + s, BF16) for s in shapes]
            + [pltpu.VMEM(s, F32) for s in shapes]
            + [pltpu.SemaphoreType.DMA((n, N_CHIP))] * 4 + [pltpu.SemaphoreType.DMA((n,))] * 2)


def _grad_matmul_reduce(a, b, name, grads):
    s_len, n_cols = b.shape
    cb = min(2 * D, n_cols)
    tn = 512
    per = cb // tn
    steps = n_cols // tn
    n = len(grads)
    shapes = [g.shape[2:] for g in grads]

    def body(a_ref, b_ref, *refs):
        ins, ob_ref, outs = refs[:n], refs[n], refs[n + 1:2 * n + 1]
        fulls, scratch = refs[2 * n + 1:3 * n + 1], refs[3 * n + 1:]
        phases = _reduce_phases(shapes, ins, fulls, scratch[:n], scratch[n:2 * n], scratch[2 * n:3 * n], scratch[3 * n:])
        j = pl.program_id(0)
        for step, phase in zip((0, steps // 4, (3 * steps) // 4, steps - 1), phases):
            pl.when(j == step)(phase)

        @pl.when(j == steps - 1)
        def _():
            for t in range(n):
                outs[t][...] = fulls[t][...]
        ob_ref[0] = lax.dot_general(a_ref[...], b_ref[...], (((0,), (0,)), ((), ())),
                                    preferred_element_type=F32).astype(BF16)

    res = pl.pallas_call(
        body, name=name, grid=(steps,),
        in_specs=[pl.BlockSpec((s_len, D), lambda j: (0, 0), pipeline_mode=pl.Buffered(1)),
                  pl.BlockSpec((s_len, tn), lambda j: (0, j))] + [VMEM_SPEC] * n,
        out_specs=[pl.BlockSpec((1, D, tn), lambda j: (j // per, 0, j % per))] + [VMEM_SPEC] * n,
        out_shape=[_sds((n_cols // cb, D, cb), BF16)] + [_sds((2,) + s, F32) for s in shapes],
        scratch_shapes=[pltpu.VMEM((2,) + s, F32) for s in shapes] + _reduce_scratch(shapes),
        compiler_params=_params(("arbitrary",)),
    )(a, b, *grads)
    return res[0], res[1:]


def _scatter_grads(grads):
    n = len(grads)

    def body(*refs):
        ins, outs = refs[:n], refs[n:2 * n]
        lsem, send, recv = refs[2 * n:]
        me = _place()
        dev, chip, c = _dev_of(me), _chip_of(me), me[2]
        local = [pltpu.make_async_copy(ins[t].at[chip, c], outs[t].at[dev], lsem.at[t]) for t in range(n)]
        for cp in local:
            cp.start()
        sends = []
        for t in range(n):
            for k in ALL_PEERS:
                to = _peer(k)
                cp = _rdma(ins[t].at[_chip_of(to), to[2]], outs[t].at[dev], send.at[t, k], recv.at[t, k], to)
                cp.start()
                sends.append(cp)
        for t in range(n):
            for k in ALL_PEERS:
                blk = outs[t].at[_dev_of(_peer(k))]
                _rdma(blk, blk, send.at[t, k], recv.at[t, k], _peer(k)).wait_recv()
        for cp in sends:
            cp.wait_send()
        for cp in local:
            cp.wait()

    return pl.pallas_call(
        body, name="scatter_grads",
        in_specs=[ANY_SPEC] * n, out_specs=[ANY_SPEC] * n,
        out_shape=[_sds((N_DEV,) + g.shape[2:], g.dtype) for g in grads],
        scratch_shapes=[pltpu.SemaphoreType.DMA((n,))] + [pltpu.SemaphoreType.DMA((n, N_DEV))] * 2,
        compiler_params=_params(),
    )(*grads)


def _sum_devices(parts, name):
    _, r, cols = parts.shape
    tr = min(r, 128)

    def body(in_ref, o_ref):
        acc = in_ref[0].astype(F32)
        for d in range(1, N_DEV):
            acc = acc + in_ref[d].astype(F32)
        o_ref[...] = acc

    return pl.pallas_call(
        body, name=name, grid=(r // tr,),
        in_specs=[pl.BlockSpec((N_DEV, tr, cols), lambda i: (0, i, 0))],
        out_specs=pl.BlockSpec((tr, cols), lambda i: (i, 0)),
        out_shape=_sds((r, cols), F32),
        compiler_params=_params(("arbitrary",)),
    )(parts)


def _share_halves(reds):
    n = len(reds)

    def body(*refs):
        ins, outs = refs[:n], refs[n:2 * n]
        lsem, send, recv = refs[2 * n:]
        me = _place()
        c = me[2]
        local = [pltpu.make_async_copy(ins[t], outs[t].at[c], lsem.at[t]) for t in range(n)]
        sends = [_rdma(ins[t], outs[t].at[c], send.at[t], recv.at[t], _peer(SIBLING)) for t in range(n)]
        for cp in local + sends:
            cp.start()
        for t in range(n):
            _rdma(ins[t], outs[t].at[1 - c], send.at[t], recv.at[t], _peer(SIBLING)).wait_recv()
        for cp in sends:
            cp.wait_send()
        for cp in local:
            cp.wait()

    return pl.pallas_call(
        body, name="share_halves",
        in_specs=[VMEM_SPEC] * n, out_specs=[VMEM_SPEC] * n,
        out_shape=[_sds((2,) + r.shape, r.dtype) for r in reds],
        scratch_shapes=[pltpu.SemaphoreType.DMA((n,))] * 3,
        compiler_params=_params(),
    )(*reds)


def _sum_small(blobs):
    n = len(blobs)

    def body(*refs):
        ins, outs = refs[:n], refs[n:2 * n]
        pbufs, buf4s = refs[2 * n:3 * n], refs[3 * n:4 * n]
        psend, precv, send, recv = refs[4 * n:]
        me = _place()
        chip = _chip_of(me)
        pairs = [_rdma(ins[t], pbufs[t], psend.at[t], precv.at[t], _peer(SIBLING)) for t in range(n)]
        for cp in pairs:
            cp.start()
        out = []
        for t in range(n):
            pairs[t].wait()
            buf4s[t][chip] = ins[t][...] + pbufs[t][...]
            for k in CHIP_PEERS:
                cp = _rdma(buf4s[t].at[chip], buf4s[t].at[chip], send.at[t, k], recv.at[t, k], _peer(k))
                cp.start()
                out.append(cp)
        for t in range(n):
            for k in CHIP_PEERS:
                blk = buf4s[t].at[_chip_of(_peer(k))]
                _rdma(blk, blk, send.at[t, k], recv.at[t, k], _peer(k)).wait_recv()
            outs[t][...] = (buf4s[t][0] + buf4s[t][1]) + (buf4s[t][2] + buf4s[t][3])
        for cp in out:
            cp.wait_send()

    return pl.pallas_call(
        body, name="sum_small",
        in_specs=[VMEM_SPEC] * n, out_specs=[VMEM_SPEC] * n, out_shape=[_sds(b.shape, F32) for b in blobs],
        scratch_shapes=([pltpu.VMEM(b.shape, F32) for b in blobs] + [pltpu.VMEM((N_CHIP,) + b.shape, F32) for b in blobs]
                        + [pltpu.SemaphoreType.DMA((n,))] * 2 + [pltpu.SemaphoreType.DMA((n, N_DEV))] * 2),
        compiler_params=_params(),
    )(*blobs)


def _adamw_math(w, g, m, v):
    m = ADAM_B1 * m + (1.0 - ADAM_B1) * g
    v = ADAM_B2 * v + (1.0 - ADAM_B2) * (g * g)
    m_hat = m / (1.0 - ADAM_B1 ** ADAM_STEP)
    v_hat = v / (1.0 - ADAM_B2 ** ADAM_STEP)
    delta = -ADAM_LR * (m_hat / (jnp.sqrt(v_hat) + ADAM_EPS) + ADAM_WD * w)
    return delta, m, v


def _row_tile(r, cols):
    if r * cols * 4 <= 2 ** 20:
        return r
    return next(t for t in (512, 256, 128, 64, 32, 16, 8) if r % t == 0 and t * cols * 4 <= 2 ** 20)


def _adamw(w, g, m, v, name):
    r, cols = w.shape
    tr = _row_tile(r, cols)

    def body(w_ref, g_ref, m_ref, v_ref, d_ref, nm_ref, nv_ref):
        d_ref[...], nm_ref[...], nv_ref[...] = _adamw_math(w_ref[...], g_ref[...], m_ref[...], v_ref[...])

    spec = pl.BlockSpec((tr, cols), lambda i: (i, 0))
    return pl.pallas_call(
        body, name=name, grid=(r // tr,), in_specs=[spec] * 4, out_specs=[spec] * 3,
        out_shape=[_sds((r, cols), F32)] * 3, compiler_params=_params(("arbitrary",)),
    )(w, g, m, v)


def _adamw_ada(w, ct, dm, m, v):
    r, cols = w.shape
    tr = _row_tile(r, cols)

    def body(w_ref, ct_ref, dm_ref, m_ref, v_ref, g_ref, d_ref, nm_ref, nv_ref):
        g = jnp.dot(ct_ref[...], dm_ref[...], preferred_element_type=F32)
        g_ref[...] = g
        d_ref[...], nm_ref[...], nv_ref[...] = _adamw_math(w_ref[...], g, m_ref[...], v_ref[...])

    spec = pl.BlockSpec((tr, cols), lambda i: (i, 0))
    return pl.pallas_call(
        body, name="adamw_ada", grid=(r // tr,),
        in_specs=[spec, pl.BlockSpec((tr, LANE), lambda i: (i, 0)), pl.BlockSpec((LANE, cols), lambda i: (0, 0)), spec, spec],
        out_specs=[spec] * 4, out_shape=[_sds((r, cols), F32)] * 4, compiler_params=_params(("arbitrary",)),
    )(w, ct, dm, m, v)


BLOB_VEC, BLOB_BSGU, BLOB_CONV, BLOB_ADA, BLOB_DMOD, BLOB_LOSS, BLOB_ROWS = 0, 8, 16, 48, 56, 80, 88
N_VEC = 7


def _adamw_small(tot, g_w_sgu, g_conv, params):
    n = len(params)

    def body(*refs):
        tot_ref, gws_ref, gconv_ref = refs[:3]
        wmv = refs[3:3 + 3 * n]
        outs = refs[3 + 3 * n:]
        grads = [tot_ref[pl.ds(BLOB_VEC + i, 1), :] for i in range(N_VEC)]
        grads += [tot_ref[pl.ds(BLOB_BSGU, HEADS), pl.ds(0, CHUNK)], gconv_ref[...], gws_ref[...], tot_ref[pl.ds(BLOB_ADA, 3), :]]
        for i, g in enumerate(grads):
            w_ref, m_ref, v_ref = wmv[3 * i:3 * i + 3]
            d, nm, nv = _adamw_math(w_ref[...], g, m_ref[...], v_ref[...])
            outs[4 * i][...] = g
            outs[4 * i + 1][...] = d
            outs[4 * i + 2][...] = nm
            outs[4 * i + 3][...] = nv

    flat = [a for wmv in params for a in wmv]
    return pl.pallas_call(
        body, name="adamw_small",
        in_specs=[VMEM_SPEC] * (3 + len(flat)), out_specs=[VMEM_SPEC] * (4 * n),
        out_shape=[_sds(wmv[0].shape, F32) for wmv in params for _ in range(4)],
        compiler_params=_params(),
    )(tot, g_w_sgu, g_conv, *flat)


def _set_rows(buf, row, val):
    return lax.dynamic_update_slice(buf, val.astype(F32), (row, 0))


def kernel(x, c, w_ada, b_ada, g_pre, w_in, conv_w, conv_b, conv_ln_g, conv_ln_b, w_conv_out, sgu_ln_g, sgu_ln_b, w_sgu, b_sgu, w_sgu_out, w_o, g_final, loss_target, m_w_ada, m_b_ada, m_g_pre, m_w_in, m_conv_w, m_conv_b, m_conv_ln_g, m_conv_ln_b, m_w_conv_out, m_sgu_ln_g, m_sgu_ln_b, m_w_sgu, m_b_sgu, m_w_sgu_out, m_w_o, m_g_final, v_w_ada, v_b_ada, v_g_pre, v_w_in, v_conv_w, v_conv_b, v_conv_ln_g, v_conv_ln_b, v_w_conv_out, v_sgu_ln_g, v_sgu_ln_b, v_w_sgu, v_b_sgu, v_w_sgu_out, v_w_o, v_g_final):
    me = _place()
    dev, chip = _dev_of(me), _chip_of(me)
    n_ada = w_ada.shape[2]
    conv_cols = conv_w.shape[2]

    b_ada_s = lax.dynamic_slice(b_ada, (0, chip * n_ada), (1, n_ada))
    c_all, mod_all, cw_all = _setup_comm(
        jnp.broadcast_to(c, (SUB, D)), w_ada[0], b_ada_s, jnp.pad(conv_w[0], ((0, HALO - CONV_K), (0, 0))))
    mod = lax.dynamic_slice(mod_all, (0, dev * SUB, 0), (N_CHIP, 1, n_ada)).reshape(1, 3 * D)
    shift, scale, gate = mod[:, :D], mod[:, D:2 * D], mod[:, 2 * D:]
    conv_w_full = jnp.swapaxes(cw_all, 0, 1).reshape(HALO, D)[:CONV_K]

    wg_in, wg_co, wg_so, wg_o = _gather_weights(
        [w_in[0].astype(BF16), w_conv_out[0].astype(BF16), w_sgu_out[0].astype(BF16), w_o[0].astype(BF16)])

    loc = _local_step(x[0], loss_target[0], shift, scale, gate, g_pre, conv_w_full, conv_b, conv_ln_g, conv_ln_b,
                      sgu_ln_g, sgu_ln_b, w_sgu[0], b_sgu[0], g_final.reshape(1, D),
                      wg_in, wg_co.reshape(D, D), wg_so.reshape(D, D), wg_o.reshape(D, D))

    big = ["w_in", "w_conv_out", "w_sgu_out", "w_o"]
    contrib_out = [loc[name].reshape(N_CHIP, 2, D // (2 * N_CHIP), D) for name in big[1:]]
    gw_in, full_out = _grad_matmul_reduce(loc["hb"], loc["dp"], "grad_w_in", contrib_out)
    full_in = _reduce_scatter([gw_in.reshape(N_CHIP, 2, D // 2, gw_in.shape[2])], "reduce_w_in")
    g_big = {name: f.reshape(2 * f.shape[1], f.shape[2]) for name, f in zip(big, list(full_in) + list(full_out))}

    d_mod = jnp.concatenate([loc["d_shift"], loc["d_scale"], loc["d_gate"]], axis=0)
    blob = jnp.zeros((BLOB_ROWS, D), F32)
    for i, name in enumerate(["g_pre", "conv_b", "conv_ln_g", "conv_ln_b", "sgu_ln_g", "sgu_ln_b", "g_final"]):
        blob = _set_rows(blob, BLOB_VEC + i, loc[name])
    blob = _set_rows(blob, BLOB_BSGU, loc["b_sgu"])
    blob = _set_rows(blob, BLOB_CONV, loc["conv_w"])
    blob = _set_rows(blob, BLOB_ADA, d_mod)
    blob = lax.dynamic_update_slice(blob, d_mod, (BLOB_DMOD + 3 * dev, 0))
    blob = _set_rows(blob, BLOB_LOSS, loc["loss_cols"])
    tot, g_w_sgu = _sum_small([blob, loc["w_sgu"].reshape(HEADS * CHUNK, CHUNK)])

    loss = jnp.sum(tot[BLOB_LOSS])
    g_conv_s = lax.dynamic_slice(tot, (BLOB_CONV, chip * conv_cols), (CONV_K, conv_cols))
    d_mod_all = tot[BLOB_DMOD:BLOB_DMOD + 3 * N_DEV].reshape(N_DEV, 3 * D)

    ct = jnp.pad(c_all[::SUB].T, ((0, 0), (0, LANE - N_DEV))).astype(BF16)
    dm = jnp.pad(lax.dynamic_slice(d_mod_all, (0, chip * n_ada), (N_DEV, n_ada)), ((0, LANE - N_DEV), (0, 0))).astype(BF16)
    g_ada, d_ada, nm_ada, nv_ada = _adamw_ada(w_ada[0], ct, dm, m_w_ada[0], v_w_ada[0])

    upd = {}
    for name, w, m, v in [("w_in", w_in, m_w_in, v_w_in), ("w_conv_out", w_conv_out, m_w_conv_out, v_w_conv_out),
                          ("w_sgu_out", w_sgu_out, m_w_sgu_out, v_w_sgu_out), ("w_o", w_o, m_w_o, v_w_o)]:
        upd[name] = _adamw(w[0], g_big[name], m[0], v[0], "adamw_" + name)

    def wmv(w, m, v, shape):
        return tuple(a.reshape(shape) for a in (w, m, v))

    small_params = [wmv(w, m, v, (1, D)) for w, m, v in [
        (g_pre, m_g_pre, v_g_pre), (conv_b, m_conv_b, v_conv_b), (conv_ln_g, m_conv_ln_g, v_conv_ln_g),
        (conv_ln_b, m_conv_ln_b, v_conv_ln_b), (sgu_ln_g, m_sgu_ln_g, v_sgu_ln_g), (sgu_ln_b, m_sgu_ln_b, v_sgu_ln_b),
        (g_final, m_g_final, v_g_final)]]
    small_params += [wmv(b_sgu, m_b_sgu, v_b_sgu, (HEADS, CHUNK)), wmv(conv_w, m_conv_w, v_conv_w, (CONV_K, conv_cols)),
                     wmv(w_sgu, m_w_sgu, v_w_sgu, (HEADS * CHUNK, CHUNK)), wmv(b_ada, m_b_ada, v_b_ada, (3, D))]
    small_out = _adamw_small(tot, g_w_sgu, g_conv_s, small_params)

    def leaves(kind):
        vecs = [small_out[4 * i + kind] for i in range(N_VEC)]
        o_b_sgu, o_conv, o_w_sgu, o_b_ada = (small_out[4 * (N_VEC + i) + kind] for i in range(4))
        ada = (g_ada, d_ada, nm_ada, nv_ada)[kind]
        def bigk(name):
            return (g_big[name] if kind == 0 else upd[name][kind - 1])[None]
        return [ada[None], o_b_ada.reshape(1, 3 * D), vecs[0], bigk("w_in"), o_conv[None], vecs[1], vecs[2], vecs[3],
                bigk("w_conv_out"), vecs[4], vecs[5], o_w_sgu.reshape(1, HEADS, CHUNK, CHUNK), o_b_sgu[None],
                bigk("w_sgu_out"), bigk("w_o"), vecs[6].reshape(D)]

    return (loss, loc["grad_x"][None], *leaves(0), *leaves(1), *leaves(2), *leaves(3))
```

```python
import functools

import jax
import jax.numpy as jnp
from jax import lax
from jax.experimental import pallas as pl
from jax.experimental.pallas import tpu as pltpu

F32 = jnp.float32
BF16 = jnp.bfloat16
MESH = pl.DeviceIdType.MESH

D = 1024
N_SEC = 8
N_CHIP = 4
N_DEV = 8
EPS = 1e-6
CONV_K = 31
HALO = 32
CHUNK = 128
HEADS = 8
LANE = 128
SUB = 8
PACK = 16
VMEM_LIMIT = 56 * 1024 * 1024

ADAM_LR, ADAM_B1, ADAM_B2, ADAM_EPS, ADAM_WD, ADAM_STEP = 0.001, 0.9, 0.999, 1e-08, 0.01, 10

_SQRT_HALF = 0.7071067811865476
_INV_SQRT_2PI = 0.3989422804014327


def _sds(shape, dtype):
    return jax.ShapeDtypeStruct(shape, dtype)


def _params(sem=None):
    if sem is None:
        return pltpu.CompilerParams(vmem_limit_bytes=VMEM_LIMIT)
    return pltpu.CompilerParams(dimension_semantics=sem, vmem_limit_bytes=VMEM_LIMIT)


def _strips(n_rows, rows, fn):
    def step(s, carry):
        fn(pl.multiple_of(s * rows, rows))
        return carry
    lax.fori_loop(0, n_rows // rows, step, 0)


def _sigmoid(v):
    return 1.0 / (1.0 + jnp.exp(-v))


def _gelu(v):
    return 0.5 * v * (1.0 + lax.erf(v * _SQRT_HALF))


def _gelu_and_grad(v):
    cdf = 0.5 * (1.0 + lax.erf(v * _SQRT_HALF))
    return v * cdf, cdf + v * jnp.exp(-0.5 * v * v) * _INV_SQRT_2PI


def _dsilu(v, sg):
    return sg * (1.0 + v * (1.0 - sg))


def _rowmean(v):
    return jnp.mean(v, axis=-1, keepdims=True)


def _vec_spec(grid_rank):
    zeros = (0, 0)
    if grid_rank == 1:
        return pl.BlockSpec((1, D), lambda i: zeros)
    return pl.BlockSpec((1, D), lambda i, j: zeros)


def _in_proj(x, shift, scale, g_pre, wg_in):
    s_len = x.shape[0]
    tm = min(512, s_len)
    n_i = s_len // tm
    wn = wg_in.shape[2]

    def body(x_ref, sh_ref, sc_ref, g_ref, w_ref, p_ref, hb_ref):
        def strip(r0):
            xs = x_ref[pl.ds(r0, PACK), :]
            r = lax.rsqrt(_rowmean(xs * xs) + EPS)
            h = (xs * r) * g_ref[...] * (1.0 + sc_ref[...]) + sh_ref[...]
            hb_ref[pl.ds(r0, PACK), :] = h.astype(BF16)
        _strips(tm, PACK, strip)
        hb = hb_ref[...]
        for j in range(N_CHIP):
            p_ref[:, pl.ds(j * wn, wn)] = jnp.dot(hb, w_ref[j], preferred_element_type=F32).astype(BF16)

    return pl.pallas_call(
        body, name="in_proj", grid=(n_i,),
        in_specs=[pl.BlockSpec((tm, D), lambda i: (i, 0)), _vec_spec(1), _vec_spec(1), _vec_spec(1),
                  pl.BlockSpec((N_CHIP, D, wn), lambda i: (0, 0, 0), pipeline_mode=pl.Buffered(1))],
        out_specs=[pl.BlockSpec((tm, N_CHIP * wn), lambda i: (i, 0)), pl.BlockSpec((tm, D), lambda i: (i, 0))],
        out_shape=[_sds((s_len, N_SEC * D), BF16), _sds((s_len, D), BF16)],
        compiler_params=_params(("arbitrary",)),
    )(x, shift, scale, g_pre, wg_in)


def _conv_taps(win_ref, r0, lt, weight_of_offset, rows):
    lanes = pl.ds(lt * LANE, LANE)
    win = win_ref[pl.ds(r0, rows + HALO), lanes]
    n_out = rows // SUB
    acc = [jnp.zeros((SUB, LANE), F32) for _ in range(n_out)]
    for phase in range(SUB):
        offs = [o for o in weight_of_offset if o % SUB == phase]
        if not offs:
            continue
        q_max = max(o // SUB for o in offs)
        span = (n_out + q_max) * SUB
        sh = win[phase:phase + span, :]
        for o in offs:
            q = o // SUB
            w = weight_of_offset[o](lanes)
            for m in range(n_out):
                acc[m] = acc[m] + w * sh[(m + q) * SUB:(m + q + 1) * SUB, :]
    return acc


def _branch_a_fwd(p, conv_wb, conv_b, ln_g, ln_b):
    s_len = p.shape[0]
    tm = min(256, s_len)
    n_i = s_len // tm
    rows = 32

    def body(p_ref, wb_ref, cb_ref, g_ref, b_ref, ya_ref, y1_ref, abuf):
        @pl.when(pl.program_id(0) == 0)
        def _():
            abuf[pl.ds(0, HALO), :] = jnp.zeros((HALO, D), F32)

        def glu(r0):
            val = p_ref[pl.ds(r0, PACK), pl.ds(0, D)].astype(F32)
            gl = p_ref[pl.ds(r0, PACK), pl.ds(D, D)].astype(F32)
            abuf[pl.ds(HALO + r0, PACK), :] = val * _sigmoid(gl)
        _strips(tm, PACK,glu)

        taps = {HALO - (CONV_K - 1) + k: (lambda lanes, k=k: wb_ref[pl.ds(k * SUB, SUB), lanes]) for k in range(CONV_K)}

        def conv(r0):
            for lt in range(D // LANE):
                acc = _conv_taps(abuf, r0, lt, taps, rows)
                cb = cb_ref[:, pl.ds(lt * LANE, LANE)]
                for m, v in enumerate(acc):
                    y1_ref[pl.ds(r0 + m * SUB, SUB), pl.ds(lt * LANE, LANE)] = v + cb
        _strips(tm, rows, conv)

        def norm(r0):
            y1 = y1_ref[pl.ds(r0, PACK), :]
            mu = _rowmean(y1)
            yc = y1 - mu
            rstd = lax.rsqrt(_rowmean(yc * yc) + EPS)
            l1 = (yc * rstd) * g_ref[...] + b_ref[...]
            z = p_ref[pl.ds(r0, PACK), pl.ds(2 * D, D)].astype(F32)
            ya_ref[pl.ds(r0, PACK), :] = ((l1 * _sigmoid(l1)) * (z * _sigmoid(z))).astype(BF16)
        _strips(tm, PACK,norm)

        abuf[pl.ds(0, HALO), :] = abuf[pl.ds(tm, HALO), :]

    return pl.pallas_call(
        body, name="branch_a_fwd", grid=(n_i,),
        in_specs=[pl.BlockSpec((tm, 3 * D), lambda i: (i, 0)),
                  pl.BlockSpec((CONV_K * SUB, D), lambda i: (0, 0)), _vec_spec(1), _vec_spec(1), _vec_spec(1)],
        out_specs=[pl.BlockSpec((tm, D), lambda i: (i, 0)), pl.BlockSpec((tm, D), lambda i: (i, 0))],
        out_shape=[_sds((s_len, D), BF16), _sds((s_len, D), F32)],
        scratch_shapes=[pltpu.VMEM((tm + HALO, D), F32)],
        compiler_params=_params(("arbitrary",)),
    )(p, conv_wb, conv_b, ln_g, ln_b)


def _branch_b_fwd(p, wt, bias_full, ln_g, ln_b):
    s_len = p.shape[0]
    tm = min(256, s_len)
    n_i = s_len // tm

    def body(p_ref, wt_ref, bias_ref, g_ref, b_ref, yb_ref, vb, sbuf):
        def norm(r0):
            gv = _gelu(p_ref[pl.ds(r0, PACK), pl.ds(D, D)].astype(F32))
            mu = _rowmean(gv)
            vc = gv - mu
            rstd = lax.rsqrt(_rowmean(vc * vc) + EPS)
            vb[pl.ds(r0, PACK), :] = ((vc * rstd) * g_ref[...] + b_ref[...]).astype(BF16)
        _strips(tm, PACK,norm)

        for ck in range(tm // CHUNK):
            for h in range(HEADS):
                blk = (pl.ds(ck * CHUNK, CHUNK), pl.ds(h * LANE, LANE))
                sbuf[blk] = jnp.dot(wt_ref[h], vb[blk], preferred_element_type=F32) + bias_ref[:, pl.ds(h * LANE, LANE)]

        def gate(r0):
            u = _gelu(p_ref[pl.ds(r0, PACK), pl.ds(0, D)].astype(F32))
            z = p_ref[pl.ds(r0, PACK), pl.ds(2 * D, D)].astype(F32)
            yb_ref[pl.ds(r0, PACK), :] = (u * sbuf[pl.ds(r0, PACK), :] * (z * _sigmoid(z))).astype(BF16)
        _strips(tm, PACK,gate)

    return pl.pallas_call(
        body, name="branch_b_fwd", grid=(n_i,),
        in_specs=[pl.BlockSpec((tm, 3 * D), lambda i: (i, 1)),
                  pl.BlockSpec((HEADS, CHUNK, CHUNK), lambda i: (0, 0, 0)),
                  pl.BlockSpec((CHUNK, D), lambda i: (0, 0)), _vec_spec(1), _vec_spec(1)],
        out_specs=pl.BlockSpec((tm, D), lambda i: (i, 0)),
        out_shape=_sds((s_len, D), BF16),
        scratch_shapes=[pltpu.VMEM((tm, D), BF16), pltpu.VMEM((tm, D), F32)],
        compiler_params=_params(("arbitrary",)),
    )(p, wt, bias_full, ln_g, ln_b)


def _dot_t(a, b):
    return lax.dot_general(a, b, (((1,), (1,)), ((), ())), preferred_element_type=F32)


def _out_proj(p, ya_in, yb_in, x, target, gate, g_final, w_co, w_so, w_o):
    s_len = x.shape[0]
    tm = min(256, s_len)
    n_i = s_len // tm

    def body(pg_ref, ya_ref, yb_ref, x_ref, t_ref, gate_ref, gf_ref, wco_ref, wso_ref, wo_ref,
             dx2_ref, dya_ref, dyb_ref, dp_ref, mb_ref, dob_ref, dyab_ref, dybb_ref, sums_ref):
        @pl.when(pl.program_id(0) == 0)
        def _():
            sums_ref[...] = jnp.zeros((SUB, D), F32)

        y_a = jnp.dot(ya_ref[...], wco_ref[...], preferred_element_type=F32)
        y_b = jnp.dot(yb_ref[...], wso_ref[...], preferred_element_type=F32)
        ga = _sigmoid(pg_ref[:, pl.ds(0, D)].astype(F32))
        gb = _sigmoid(pg_ref[:, pl.ds(D, D)].astype(F32))
        mb = (ga * y_a + gb * y_b).astype(BF16)
        mb_ref[...] = mb
        o = jnp.dot(mb, wo_ref[...], preferred_element_type=F32)
        x2 = x_ref[...] + gate_ref[...] * o
        r2 = lax.rsqrt(_rowmean(x2 * x2) + EPS)
        xh = x2 * r2
        e = xh * gf_ref[...] - t_ref[...]
        dy = e * (1.0 / D)
        dxh = dy * gf_ref[...]
        dx2 = r2 * (dxh - xh * _rowmean(dxh * xh))
        dx2_ref[...] = dx2
        sums_ref[pl.ds(0, 1), :] += jnp.sum(dy * xh, axis=0, keepdims=True)
        sums_ref[pl.ds(1, 1), :] += jnp.sum(dx2 * o, axis=0, keepdims=True)
        sums_ref[pl.ds(2, 1), :] += jnp.sum(e * e, axis=0, keepdims=True) * (0.5 / D)
        dob = (gate_ref[...] * dx2).astype(BF16)
        dob_ref[...] = dob
        dm = _dot_t(dob, wo_ref[...])
        dy_a = (ga * dm).astype(BF16)
        dy_b = (gb * dm).astype(BF16)
        dyab_ref[...] = dy_a
        dybb_ref[...] = dy_b
        dp_ref[:, pl.ds(0, D)] = (dm * y_a * ga * (1.0 - ga)).astype(BF16)
        dp_ref[:, pl.ds(D, D)] = (dm * y_b * gb * (1.0 - gb)).astype(BF16)
        dya_ref[...] = _dot_t(dy_a, wco_ref[...])
        dyb_ref[...] = _dot_t(dy_b, wso_ref[...])

    tile = pl.BlockSpec((tm, D), lambda i: (i, 0))
    wspec = pl.BlockSpec((D, D), lambda i: (0, 0))
    return pl.pallas_call(
        body, name="out_proj", grid=(n_i,),
        in_specs=[pl.BlockSpec((tm, 2 * D), lambda i: (i, 3)), tile, tile, tile, tile, _vec_spec(1), _vec_spec(1),
                  wspec, wspec, wspec],
        out_specs=[tile, tile, tile, pl.BlockSpec((tm, 2 * D), lambda i: (i, 3)), tile, tile, tile, tile,
                   pl.BlockSpec((SUB, D), lambda i: (0, 0))],
        out_shape=[_sds((s_len, D), F32), _sds((s_len, D), F32), _sds((s_len, D), F32), _sds((s_len, N_SEC * D), BF16),
                   _sds((s_len, D), BF16), _sds((s_len, D), BF16), _sds((s_len, D), BF16), _sds((s_len, D), BF16),
                   _sds((SUB, D), F32)],
        compiler_params=_params(("arbitrary",)),
    )(p, ya_in, yb_in, x, target, gate, g_final, w_co, w_so, w_o)


A_STATS_ROWS = 8 + HALO


def _branch_a_bwd(p, y1, dya_in, dp, conv_wb, ln_g, ln_b):
    s_len = p.shape[0]
    tm = min(256, s_len)
    n_i = s_len // tm
    rows = 32
    n_out = rows // SUB

    def tile_of(i):
        return n_i - 1 - i

    def body(p_ref, y1_ref, dya_ref, dp_in, wb_ref, g_ref, b_ref, dp_ref, st_ref, dybuf, acc8, tapacc):
        del dp_in
        i = pl.program_id(0)

        @pl.when(i == 0)
        def _():
            dybuf[pl.ds(tm, HALO), :] = jnp.zeros((HALO, D), F32)
            st_ref[...] = jnp.zeros((A_STATS_ROWS, D), F32)
            acc8[...] = jnp.zeros((3 * PACK, D), F32)
            tapacc[...] = jnp.zeros((CONV_K * SUB, D), F32)

        def norm_bwd(r0):
            y1 = y1_ref[pl.ds(r0, PACK), :]
            mu = _rowmean(y1)
            yc = y1 - mu
            rstd = lax.rsqrt(_rowmean(yc * yc) + EPS)
            n1 = yc * rstd
            l1 = n1 * g_ref[...] + b_ref[...]
            sg = _sigmoid(l1)
            z = p_ref[pl.ds(r0, PACK), pl.ds(2 * D, D)].astype(F32)
            sz = _sigmoid(z)
            dya = dya_ref[pl.ds(r0, PACK), :]
            dp_ref[pl.ds(r0, PACK), pl.ds(2 * D, D)] = (dya * (l1 * sg) * _dsilu(z, sz)).astype(BF16)
            dl1 = dya * (z * sz) * _dsilu(l1, sg)
            acc8[pl.ds(0, PACK), :] += dl1 * n1
            acc8[pl.ds(PACK, PACK), :] += dl1
            dn1 = dl1 * g_ref[...]
            dy1 = rstd * (dn1 - _rowmean(dn1) - n1 * _rowmean(dn1 * n1))
            acc8[pl.ds(2 * PACK, PACK), :] += dy1
            dybuf[pl.ds(r0, PACK), :] = dy1
        _strips(tm, PACK,norm_bwd)

        def conv_bwd(r0):
            for lt in range(D // LANE):
                lanes = pl.ds(lt * LANE, LANE)
                glanes = pl.ds(D + lt * LANE, LANE)
                win = dybuf[pl.ds(r0, rows + HALO), lanes]
                sg16, a16 = [], []
                for h in range(rows // PACK):
                    rr = pl.ds(r0 + h * PACK, PACK)
                    s = _sigmoid(p_ref[rr, glanes].astype(F32))
                    sg16.append(s)
                    a16.append(p_ref[rr, lanes].astype(F32) * s)
                a = [a16[m // 2][(m % 2) * SUB:(m % 2 + 1) * SUB, :] for m in range(n_out)]
                da = [jnp.zeros((SUB, LANE), F32) for _ in range(n_out)]
                for phase in range(SUB):
                    offs = [o for o in range(CONV_K) if o % SUB == phase]
                    q_max = max(o // SUB for o in offs)
                    sh = win[phase:phase + (n_out + q_max) * SUB, :]
                    for o in offs:
                        k, q = CONV_K - 1 - o, o // SUB
                        w = wb_ref[pl.ds(k * SUB, SUB), lanes]
                        part = None
                        for m in range(n_out):
                            s = sh[(m + q) * SUB:(m + q + 1) * SUB, :]
                            da[m] = da[m] + w * s
                            part = a[m] * s if part is None else part + a[m] * s
                        tapacc[pl.ds(k * SUB, SUB), lanes] += part
                for h in range(rows // PACK):
                    rr = pl.ds(r0 + h * PACK, PACK)
                    da16 = jnp.concatenate(da[2 * h:2 * h + 2], axis=0)
                    dp_ref[rr, lanes] = (da16 * sg16[h]).astype(BF16)
                    dp_ref[rr, glanes] = (da16 * a16[h] * (1.0 - sg16[h])).astype(BF16)
        _strips(tm, rows, conv_bwd)

        dybuf[pl.ds(tm, HALO), :] = dybuf[pl.ds(0, HALO), :]

        @pl.when(i == n_i - 1)
        def _():
            for j in range(3):
                st_ref[pl.ds(j, 1), :] = jnp.sum(acc8[pl.ds(j * PACK, PACK), :], axis=0, keepdims=True)
            for k in range(CONV_K):
                st_ref[pl.ds(SUB + k, 1), :] = jnp.sum(tapacc[pl.ds(k * SUB, SUB), :], axis=0, keepdims=True)

    return pl.pallas_call(
        body, name="branch_a_bwd", grid=(n_i,),
        in_specs=[pl.BlockSpec((tm, 3 * D), lambda i: (tile_of(i), 0)),
                  pl.BlockSpec((tm, D), lambda i: (tile_of(i), 0)),
                  pl.BlockSpec((tm, D), lambda i: (tile_of(i), 0)),
                  pl.BlockSpec(memory_space=pl.ANY),
                  pl.BlockSpec((CONV_K * SUB, D), lambda i: (0, 0)), _vec_spec(1), _vec_spec(1)],
        out_specs=[pl.BlockSpec((tm, 3 * D), lambda i: (tile_of(i), 0)),
                   pl.BlockSpec((A_STATS_ROWS, D), lambda i: (0, 0))],
        out_shape=[_sds(dp.shape, BF16), _sds((A_STATS_ROWS, D), F32)],
        scratch_shapes=[pltpu.VMEM((tm + HALO, D), F32), pltpu.VMEM((3 * PACK, D), F32), pltpu.VMEM((CONV_K * SUB, D), F32)],
        input_output_aliases={3: 0},
        compiler_params=_params(("arbitrary",)),
    )(p, y1, dya_in, dp, conv_wb, ln_g, ln_b)


def _branch_b_bwd(p, dyb_in, dp, wt, wtt, bias_full, ln_g, ln_b):
    s_len = p.shape[0]
    tm = min(256, s_len)
    n_i = s_len // tm

    def body(p_ref, dyb_ref, dp_in, wt_ref, wtt_ref, bias_ref, g_ref, b_ref, dp_ref, st_ref, gbt_ref, gw_ref,
             vb, n2buf, rstdbuf, sbuf, dsb, dvbuf, acc8, gb_ref, dgbuf):
        del dp_in
        i = pl.program_id(0)

        @pl.when(i == 0)
        def _():
            st_ref[...] = jnp.zeros((SUB, D), F32)
            gbt_ref[...] = jnp.zeros((CHUNK, LANE), F32)
            gb_ref[...] = jnp.zeros((CHUNK, D), F32)
            gw_ref[...] = jnp.zeros((HEADS, CHUNK, CHUNK), F32)
            acc8[...] = jnp.zeros((2 * PACK, D), F32)

        def norm(r0):
            gv, dgv = _gelu_and_grad(p_ref[pl.ds(r0, PACK), pl.ds(D, D)].astype(F32))
            dgbuf[pl.ds(r0, PACK), :] = dgv
            mu = _rowmean(gv)
            vc = gv - mu
            rstd = lax.rsqrt(_rowmean(vc * vc) + EPS)
            n2 = vc * rstd
            n2buf[pl.ds(r0, PACK), :] = n2
            rstdbuf[pl.ds(r0, PACK), :] = jnp.broadcast_to(rstd, (PACK, LANE))
            vb[pl.ds(r0, PACK), :] = (n2 * g_ref[...] + b_ref[...]).astype(BF16)
        _strips(tm, PACK,norm)

        for ck in range(tm // CHUNK):
            for h in range(HEADS):
                blk = (pl.ds(ck * CHUNK, CHUNK), pl.ds(h * LANE, LANE))
                sbuf[blk] = jnp.dot(wt_ref[h], vb[blk], preferred_element_type=F32) + bias_ref[:, pl.ds(h * LANE, LANE)]

        def gate_bwd(r0):
            pu = p_ref[pl.ds(r0, PACK), pl.ds(0, D)].astype(F32)
            u, du = _gelu_and_grad(pu)
            z = p_ref[pl.ds(r0, PACK), pl.ds(2 * D, D)].astype(F32)
            sg = _sigmoid(z)
            sz = z * sg
            s = sbuf[pl.ds(r0, PACK), :]
            dyb = dyb_ref[pl.ds(r0, PACK), :]
            ds = dyb * u * sz
            dsb[pl.ds(r0, PACK), :] = ds.astype(BF16)
            gb_ref[pl.ds(pl.multiple_of(r0 % CHUNK, PACK), PACK), :] += ds
            dp_ref[pl.ds(r0, PACK), pl.ds(0, D)] = (dyb * s * sz * du).astype(BF16)
            dp_ref[pl.ds(r0, PACK), pl.ds(2 * D, D)] = (dyb * u * s * _dsilu(z, sg)).astype(BF16)
        _strips(tm, PACK,gate_bwd)

        for ck in range(tm // CHUNK):
            for h in range(HEADS):
                blk = (pl.ds(ck * CHUNK, CHUNK), pl.ds(h * LANE, LANE))
                d_s = dsb[blk]
                dvbuf[blk] = jnp.dot(wtt_ref[h], d_s, preferred_element_type=F32)
                gw_ref[h] += _dot_t(d_s, vb[blk])

        def norm_bwd(r0):
            dv = dvbuf[pl.ds(r0, PACK), :]
            n2 = n2buf[pl.ds(r0, PACK), :]
            rstd = rstdbuf[pl.ds(r0, PACK), pl.ds(0, 1)]
            acc8[pl.ds(0, PACK), :] += dv * n2
            acc8[pl.ds(PACK, PACK), :] += dv
            dn2 = dv * g_ref[...]
            dgv = rstd * (dn2 - _rowmean(dn2) - n2 * _rowmean(dn2 * n2))
            dp_ref[pl.ds(r0, PACK), pl.ds(D, D)] = (dgv * dgbuf[pl.ds(r0, PACK), :]).astype(BF16)
        _strips(tm, PACK,norm_bwd)

        @pl.when(i == n_i - 1)
        def _():
            for j in range(2):
                st_ref[pl.ds(j, 1), :] = jnp.sum(acc8[pl.ds(j * PACK, PACK), :], axis=0, keepdims=True)
            row = lax.broadcasted_iota(jnp.int32, (CHUNK, CHUNK), 0)
            col = lax.broadcasted_iota(jnp.int32, (CHUNK, CHUNK), 1)
            for h in range(HEADS):
                gw_ref[h] = jnp.where(row >= col, gw_ref[h], 0.0)
            lane = lax.broadcasted_iota(jnp.int32, (CHUNK, LANE), 1)
            gbt = jnp.zeros((CHUNK, LANE), F32)
            for h in range(HEADS):
                gbt = jnp.where(lane == h, jnp.sum(gb_ref[:, pl.ds(h * LANE, LANE)], axis=1, keepdims=True), gbt)
            gbt_ref[...] = gbt

    wspec = pl.BlockSpec((HEADS, CHUNK, CHUNK), lambda i: (0, 0, 0))
    return pl.pallas_call(
        body, name="branch_b_bwd", grid=(n_i,),
        in_specs=[pl.BlockSpec((tm, 3 * D), lambda i: (i, 1)), pl.BlockSpec((tm, D), lambda i: (i, 0)),
                  pl.BlockSpec(memory_space=pl.ANY), wspec, wspec,
                  pl.BlockSpec((CHUNK, D), lambda i: (0, 0)), _vec_spec(1), _vec_spec(1)],
        out_specs=[pl.BlockSpec((tm, 3 * D), lambda i: (i, 1)), pl.BlockSpec((SUB, D), lambda i: (0, 0)),
                   pl.BlockSpec((CHUNK, LANE), lambda i: (0, 0)), wspec],
        out_shape=[_sds(dp.shape, BF16), _sds((SUB, D), F32), _sds((CHUNK, LANE), F32), _sds((HEADS, CHUNK, CHUNK), F32)],
        scratch_shapes=[pltpu.VMEM((tm, D), BF16), pltpu.VMEM((tm, D), F32), pltpu.VMEM((tm, LANE), F32),
                        pltpu.VMEM((tm, D), F32), pltpu.VMEM((tm, D), BF16), pltpu.VMEM((tm, D), F32),
                        pltpu.VMEM((2 * PACK, D), F32), pltpu.VMEM((CHUNK, D), F32), pltpu.VMEM((tm, D), F32)],
        input_output_aliases={2: 0},
        compiler_params=_params(("arbitrary",)),
    )(p, dyb_in, dp, wt, wtt, bias_full, ln_g, ln_b)


def _in_proj_bwd(dp, wg_in, x, dx2, shift, scale, g_pre):
    del shift
    s_len = x.shape[0]
    tm = min(512, s_len)
    n_i = s_len // tm
    wn = wg_in.shape[2]

    def body(dp_ref, w_ref, x_ref, dx2_ref, sc_ref, g_ref, gx_ref, st_ref, acc, acc8):
        i = pl.program_id(0)

        @pl.when(i == 0)
        def _():
            st_ref[...] = jnp.zeros((SUB, D), F32)
            acc8[...] = jnp.zeros((3 * PACK, D), F32)

        dh = _dot_t(dp_ref[:, pl.ds(0, wn)], w_ref[0])
        for j in range(1, N_CHIP):
            dh = dh + _dot_t(dp_ref[:, pl.ds(j * wn, wn)], w_ref[j])
        acc[...] = dh

        def strip(r0):
            xs = x_ref[pl.ds(r0, PACK), :]
            r = lax.rsqrt(_rowmean(xs * xs) + EPS)
            xn = xs * r
            dhs = acc[pl.ds(r0, PACK), :]
            acc8[pl.ds(0, PACK), :] += dhs
            acc8[pl.ds(PACK, PACK), :] += dhs * (xn * g_ref[...])
            dhp = dhs * (1.0 + sc_ref[...])
            acc8[pl.ds(2 * PACK, PACK), :] += dhp * xn
            dxn = dhp * g_ref[...]
            gx_ref[pl.ds(r0, PACK), :] = dx2_ref[pl.ds(r0, PACK), :] + r * (dxn - xn * _rowmean(dxn * xn))
        _strips(tm, PACK, strip)

        @pl.when(i == n_i - 1)
        def _():
            for k in range(3):
                st_ref[pl.ds(k, 1), :] = jnp.sum(acc8[pl.ds(k * PACK, PACK), :], axis=0, keepdims=True)

    tile = pl.BlockSpec((tm, D), lambda i: (i, 0))
    return pl.pallas_call(
        body, name="in_proj_bwd", grid=(n_i,),
        in_specs=[pl.BlockSpec((tm, N_CHIP * wn), lambda i: (i, 0)),
                  pl.BlockSpec((N_CHIP, D, wn), lambda i: (0, 0, 0), pipeline_mode=pl.Buffered(1)),
                  tile, tile, _vec_spec(1), _vec_spec(1)],
        out_specs=[tile, pl.BlockSpec((SUB, D), lambda i: (0, 0))],
        out_shape=[_sds((s_len, D), F32), _sds((SUB, D), F32)],
        scratch_shapes=[pltpu.VMEM((tm, D), F32), pltpu.VMEM((3 * PACK, D), F32)],
        compiler_params=_params(("arbitrary",)),
    )(dp, wg_in, x, dx2, scale, g_pre)


def _grad_matmul(a, b, name):
    s_len, n = b.shape
    cb = min(2 * D, n)
    tn = 512
    per = cb // tn

    def body(a_ref, b_ref, ob_ref):
        ob_ref[0] = lax.dot_general(a_ref[...], b_ref[...], (((0,), (0,)), ((), ())),
                                    preferred_element_type=F32).astype(BF16)

    return pl.pallas_call(
        body, name=name, grid=(n // tn,),
        in_specs=[pl.BlockSpec((s_len, D), lambda j: (0, 0), pipeline_mode=pl.Buffered(1)),
                  pl.BlockSpec((s_len, tn), lambda j: (0, j))],
        out_specs=pl.BlockSpec((1, D, tn), lambda j: (j // per, 0, j % per)),
        out_shape=_sds((n // cb, D, cb), BF16),
        compiler_params=_params(("arbitrary",)),
    )(a, b)


def _local_step(x, target, shift, scale, gate, g_pre, conv_w_full, conv_b, conv_ln_g, conv_ln_b,
                sgu_ln_g, sgu_ln_b, w_sgu, b_sgu, g_final, wg_in, w_co, w_so, w_o):
    conv_wb = jnp.repeat(conv_w_full, SUB, axis=0)
    causal = jnp.tril(jnp.ones((CHUNK, CHUNK), dtype=bool))
    wt = jnp.where(causal[None], w_sgu, 0.0).astype(BF16)
    wtt = jnp.swapaxes(wt, 1, 2)
    bias_full = jnp.repeat(b_sgu.T, LANE, axis=1)

    p, hb = _in_proj(x, shift, scale, g_pre, wg_in)
    ya_in, y1 = _branch_a_fwd(p, conv_wb, conv_b, conv_ln_g, conv_ln_b)
    yb_in = _branch_b_fwd(p, wt, bias_full, sgu_ln_g, sgu_ln_b)
    dx2, dya_in, dyb_in, dp, mb, dob, dyab, dybb, sums_o = _out_proj(
        p, ya_in, yb_in, x, target, gate, g_final, w_co, w_so, w_o)
    dp, st_a = _branch_a_bwd(p, y1, dya_in, dp, conv_wb, conv_ln_g, conv_ln_b)
    dp, st_b, gbt, gws = _branch_b_bwd(p, dyb_in, dp, wt, wtt, bias_full, sgu_ln_g, sgu_ln_b)
    grad_x, st_i = _in_proj_bwd(dp, wg_in, x, dx2, shift, scale, g_pre)
    gw_o = _grad_matmul(mb, dob, "grad_w_o")
    gw_co = _grad_matmul(ya_in, dyab, "grad_w_conv_out")
    gw_so = _grad_matmul(yb_in, dybb, "grad_w_sgu_out")
    return dict(
        grad_x=grad_x, loss_cols=sums_o[2:3], g_final=sums_o[0:1], d_gate=sums_o[1:2],
        d_shift=st_i[0:1], d_scale=st_i[1:2], g_pre=st_i[2:3],
        conv_ln_g=st_a[0:1], conv_ln_b=st_a[1:2], conv_b=st_a[2:3], conv_w=st_a[SUB:SUB + CONV_K],
        sgu_ln_g=st_b[0:1], sgu_ln_b=st_b[1:2], b_sgu=gbt[:, :HEADS].T, w_sgu=gws,
        hb=hb, dp=dp, w_o=gw_o, w_conv_out=gw_co, w_sgu_out=gw_so)


ANY_SPEC = pl.BlockSpec(memory_space=pl.ANY)
VMEM_SPEC = pl.BlockSpec(memory_space=pltpu.VMEM)


def _place():
    return lax.axis_index("x"), lax.axis_index("y"), lax.axis_index("c")


def _peer(k):
    x, y, c = _place()
    return (1 - x if k & 4 else x, 1 - y if k & 2 else y, 1 - c if k & 1 else c)


def _dev_of(p):
    return 4 * p[0] + 2 * p[1] + p[2]


def _chip_of(p):
    return 2 * p[0] + p[1]


def _rdma(src, dst, send_sem, recv_sem, to):
    return pltpu.make_async_remote_copy(src_ref=src, dst_ref=dst, send_sem=send_sem, recv_sem=recv_sem,
                                        device_id=to, device_id_type=MESH)


CHIP_PEERS = (2, 4, 6)
ALL_PEERS = tuple(range(1, N_DEV))
SIBLING = 1


def _setup_comm(c8, w_ada_s, b_ada_s, convw_s):
    n_mod = w_ada_s.shape[1]
    rows = SUB * N_DEV

    def body(c8_ref, wada_ref, bada_ref, cw_ref, call_ref, mod_ref, cwall_ref, csend, crecv, wsend, wrecv, msend, mrecv):
        me = _place()
        dev, chip = _dev_of(me), _chip_of(me)

        def c_rows(d):
            return call_ref.at[pl.ds(pl.multiple_of(d * SUB, SUB), SUB), :]

        call_ref[pl.ds(pl.multiple_of(dev * SUB, SUB), SUB), :] = c8_ref[...]
        cwall_ref[chip] = cw_ref[...]
        c_out = [_rdma(c8_ref, c_rows(dev), csend.at[k], crecv.at[k], _peer(k)) for k in ALL_PEERS]
        w_out = [_rdma(cw_ref, cwall_ref.at[chip], wsend.at[k], wrecv.at[k], _peer(k)) for k in CHIP_PEERS]
        for cp in c_out + w_out:
            cp.start()
        for k in ALL_PEERS:
            _rdma(c8_ref, c_rows(_dev_of(_peer(k))), csend.at[k], crecv.at[k], _peer(k)).wait_recv()
        part = jnp.dot(call_ref[...].astype(BF16), wada_ref[...].astype(BF16), preferred_element_type=F32) + bada_ref[...]
        mod_ref[chip] = part
        m_out = [_rdma(mod_ref.at[chip], mod_ref.at[chip], msend.at[k], mrecv.at[k], _peer(k)) for k in CHIP_PEERS]
        for cp in m_out:
            cp.start()
        for k in CHIP_PEERS:
            pc = _chip_of(_peer(k))
            _rdma(cw_ref, cwall_ref.at[pc], wsend.at[k], wrecv.at[k], _peer(k)).wait_recv()
            _rdma(mod_ref.at[pc], mod_ref.at[pc], msend.at[k], mrecv.at[k], _peer(k)).wait_recv()
        for cp in c_out + w_out + m_out:
            cp.wait_send()

    return pl.pallas_call(
        body, name="setup_comm",
        in_specs=[VMEM_SPEC] * 4, out_specs=[VMEM_SPEC] * 3,
        out_shape=[_sds((rows, D), F32), _sds((N_CHIP, rows, n_mod), F32), _sds((N_CHIP,) + convw_s.shape, F32)],
        scratch_shapes=[pltpu.SemaphoreType.DMA((N_DEV,))] * 6,
        compiler_params=_params(),
    )(c8, w_ada_s, b_ada_s, convw_s)


def _gather_weights(shards):
    n = len(shards)

    def body(*refs):
        ins, outs = refs[:n], refs[n:2 * n]
        lsem, isend, irecv, dsend, drecv = refs[2 * n:]
        me = _place()
        chip, c = _chip_of(me), me[2]
        local = [pltpu.make_async_copy(ins[t], outs[t].at[chip], lsem.at[t]) for t in range(n)]
        for cp in local:
            cp.start()

        def half(t, which):
            hr = shards[t].shape[0] // 2
            return pl.ds(pl.multiple_of(which * hr, hr), hr)

        sends = []
        for t in range(n):
            for j, k in enumerate(CHIP_PEERS):
                cp = _rdma(ins[t].at[half(t, c)], outs[t].at[chip, half(t, c)], isend.at[t, j], irecv.at[t, j], _peer(k))
                cp.start()
                sends.append(cp)
        for t in range(n):
            for j, k in enumerate(CHIP_PEERS):
                blk = outs[t].at[_chip_of(_peer(k)), half(t, c)]
                _rdma(blk, blk, isend.at[t, j], irecv.at[t, j], _peer(k)).wait_recv()
                cp = _rdma(blk, blk, dsend.at[t, j], drecv.at[t, j], _peer(SIBLING))
                cp.start()
                sends.append(cp)
        for t in range(n):
            for j, k in enumerate(CHIP_PEERS):
                blk = outs[t].at[_chip_of(_peer(k)), half(t, 1 - c)]
                _rdma(blk, blk, dsend.at[t, j], drecv.at[t, j], _peer(SIBLING)).wait_recv()
        for cp in sends:
            cp.wait_send()
        for cp in local:
            cp.wait()

    return pl.pallas_call(
        body, name="gather_weights",
        in_specs=[VMEM_SPEC] * n, out_specs=[VMEM_SPEC] * n,
        out_shape=[_sds((N_CHIP,) + s.shape, s.dtype) for s in shards],
        scratch_shapes=[pltpu.SemaphoreType.DMA((n,))] + [pltpu.SemaphoreType.DMA((n, len(CHIP_PEERS)))] * 4,
        compiler_params=_params(),
    )(*shards)


def _reduce_scatter(grads, name):
    n = len(grads)
    shapes = [g.shape[2:] for g in grads]

    def body(*refs):
        ins, outs = refs[:n], refs[n:2 * n]
        pbufs, rbufs, accs = refs[2 * n:3 * n], refs[3 * n:4 * n], refs[4 * n:5 * n]
        psend, precv, csend, crecv, fsend, frecv = refs[5 * n:]
        me = _place()
        chip, c = _chip_of(me), me[2]
        sib = _peer(SIBLING)

        def to_sibling(t, d):
            return _rdma(ins[t].at[d, 1 - c], pbufs[t].at[d], psend.at[t, d], precv.at[t, d], sib)

        sends = []
        for t in range(n):
            for d in range(N_CHIP):
                cp = to_sibling(t, d)
                cp.start()
                sends.append(cp)
        for j in (1, 2, 3, 0):
            d = jnp.bitwise_xor(chip, j)
            for t in range(n):
                to_sibling(t, d).wait_recv()

                def pair_sum(r0, t=t, d=d, j=j):
                    rows = pl.ds(r0, PACK)
                    s = ins[t][d, c, rows, :].astype(F32) + pbufs[t][d, rows, :].astype(F32)
                    if j == 0:
                        accs[t][rows, :] = s
                    else:
                        pbufs[t][d, rows, :] = s.astype(BF16)
                _strips(shapes[t][0], PACK, pair_sum)
                if j:
                    cp = _rdma(pbufs[t].at[d], rbufs[t].at[j - 1], csend.at[t, j], crecv.at[t, j], _peer(2 * j))
                    cp.start()
                    sends.append(cp)
        for t in range(n):
            for j in (1, 2, 3):
                blk = rbufs[t].at[j - 1]
                _rdma(blk, blk, csend.at[t, j], crecv.at[t, j], _peer(2 * j)).wait_recv()

            def total(r0, t=t):
                rows = pl.ds(r0, PACK)
                s = accs[t][rows, :] + rbufs[t][0, rows, :].astype(F32)
                s = s + rbufs[t][1, rows, :].astype(F32)
                outs[t][c, rows, :] = s + rbufs[t][2, rows, :].astype(F32)
            _strips(shapes[t][0], PACK, total)
            cp = _rdma(outs[t].at[c], outs[t].at[c], fsend.at[t], frecv.at[t], sib)
            cp.start()
            sends.append(cp)
        for t in range(n):
            blk = outs[t].at[1 - c]
            _rdma(blk, blk, fsend.at[t], frecv.at[t], sib).wait_recv()
        for cp in sends:
            cp.wait_send()

    return pl.pallas_call(
        body, name=name,
        in_specs=[VMEM_SPEC] * n, out_specs=[VMEM_SPEC] * n,
        out_shape=[_sds((2,) + s, F32) for s in shapes],
        scratch_shapes=([pltpu.VMEM((N_CHIP,) + s, BF16) for s in shapes] + [pltpu.VMEM((N_CHIP - 1,) + s, BF16) for s in shapes]
                        + [pltpu.VMEM(s, F32) for s in shapes]
                        + [pltpu.SemaphoreType.DMA((n, N_CHIP))] * 4 + [pltpu.SemaphoreType.DMA((n,))] * 2),
        compiler_params=_params(),
    )(*grads)


def _reduce_phases(shapes, ins, outs, pbufs, rbufs, accs, sems):
    n = len(shapes)
    psend, precv, csend, crecv, fsend, frecv = sems
    me = _place()
    chip, c = _chip_of(me), me[2]
    sib = _peer(SIBLING)

    def to_sibling(t, d):
        return _rdma(ins[t].at[d, 1 - c], pbufs[t].at[d], psend.at[t, d], precv.at[t, d], sib)

    def to_chip(t, j):
        return _rdma(pbufs[t].at[jnp.bitwise_xor(chip, j)], rbufs[t].at[j - 1], csend.at[t, j], crecv.at[t, j], _peer(2 * j))

    def finished(t):
        return _rdma(outs[t].at[c], outs[t].at[c], fsend.at[t], frecv.at[t], sib)

    def phase_a():
        for t in range(n):
            for d in range(N_CHIP):
                to_sibling(t, d).start()

    def phase_b():
        for j in (1, 2, 3, 0):
            d = jnp.bitwise_xor(chip, j)
            for t in range(n):
                to_sibling(t, d).wait_recv()

                def pair_sum(r0, t=t, d=d, j=j):
                    rows = pl.ds(r0, PACK)
                    s = ins[t][d, c, rows, :].astype(F32) + pbufs[t][d, rows, :].astype(F32)
                    if j == 0:
                        accs[t][rows, :] = s
                    else:
                        pbufs[t][d, rows, :] = s.astype(BF16)
                _strips(shapes[t][0], PACK, pair_sum)
                if j:
                    to_chip(t, j).start()

    def phase_c():
        for t in range(n):
            for j in (1, 2, 3):
                blk = rbufs[t].at[j - 1]
                _rdma(blk, blk, csend.at[t, j], crecv.at[t, j], _peer(2 * j)).wait_recv()

            def total(r0, t=t):
                rows = pl.ds(r0, PACK)
                s = accs[t][rows, :] + rbufs[t][0, rows, :].astype(F32)
                s = s + rbufs[t][1, rows, :].astype(F32)
                outs[t][c, rows, :] = s + rbufs[t][2, rows, :].astype(F32)
            _strips(shapes[t][0], PACK, total)
            finished(t).start()

    def phase_d():
        for t in range(n):
            blk = outs[t].at[1 - c]
            _rdma(blk, blk, fsend.at[t], frecv.at[t], sib).wait_recv()
        for t in range(n):
            for d in range(N_CHIP):
                to_sibling(t, d).wait_send()
            for j in (1, 2, 3):
                to_chip(t, j).wait_send()
            finished(t).wait_send()

    return phase_a, phase_b, phase_c, phase_d


def _sum_small_phases(ins, outs, pbufs, buf4s, sems):
    n = len(ins)
    psend, precv, send, recv = sems
    chip = _chip_of(_place())

    def swap(t):
        return _rdma(ins[t], pbufs[t], psend.at[t], precv.at[t], _peer(SIBLING))

    def to_chip(t, k):
        return _rdma(buf4s[t].at[chip], buf4s[t].at[chip], send.at[t, k], recv.at[t, k], _peer(k))

    def phase_a():
        for t in range(n):
            swap(t).start()

    def phase_b():
        for t in range(n):
            swap(t).wait()
            buf4s[t][chip] = ins[t][...] + pbufs[t][...]
            for k in CHIP_PEERS:
                to_chip(t, k).start()

    def phase_c():
        for t in range(n):
            for k in CHIP_PEERS:
                blk = buf4s[t].at[_chip_of(_peer(k))]
                _rdma(blk, blk, send.at[t, k], recv.at[t, k], _peer(k)).wait_recv()
            outs[t][...] = (buf4s[t][0] + buf4s[t][1]) + (buf4s[t][2] + buf4s[t][3])

    def phase_d():
        for t in range(n):
            for k in CHIP_PEERS:
                to_chip(t, k).wait_send()

    return phase_a, phase_b, phase_c, phase_d


def _sum_small_scratch(blobs):
    n = len(blobs)
    return ([pltpu.VMEM(b.shape, F32) for b in blobs] + [pltpu.VMEM((N_CHIP,) + b.shape, F32) for b in blobs]
            + [pltpu.SemaphoreType.DMA((n,))] * 2 + [pltpu.SemaphoreType.DMA((n, N_DEV))] * 2)


def _reduce_scratch(shapes):
    n = len(shapes)
    return ([pltpu.VMEM((N_CHIP,) + s, BF16) for s in shapes] + [pltpu.VMEM((N_CHIP - 1,) + s, BF16) for s in shapes]
            + [pltpu.VMEM(s, F32) for s in shapes]
            + [pltpu.SemaphoreType.DMA((n, N_CHIP))] * 4 + [pltpu.SemaphoreType.DMA((n,))] * 2)


def _grad_matmul_reduce(a, b, name, grads, blobs):
    s_len, n_cols = b.shape
    cb = min(2 * D, n_cols)
    tn = 512
    per = cb // tn
    steps = n_cols // tn
    n, nb = len(grads), len(blobs)
    shapes = [g.shape[2:] for g in grads]
    n_red = len(_reduce_scratch(shapes))

    def body(a_ref, b_ref, *refs):
        ins, bins = refs[:n], refs[n:n + nb]
        ob_ref, outs, bouts = refs[n + nb], refs[n + nb + 1:2 * n + nb + 1], refs[2 * n + nb + 1:2 * (n + nb) + 1]
        scratch = refs[2 * (n + nb) + 1:]
        fulls, red, small = scratch[:n], scratch[n:n + n_red], scratch[n + n_red:]
        phases = _reduce_phases(shapes, ins, fulls, red[:n], red[n:2 * n], red[2 * n:3 * n], red[3 * n:])
        small_phases = _sum_small_phases(bins, bouts, small[:nb], small[nb:2 * nb], small[2 * nb:])
        j = pl.program_id(0)
        for step, phase in zip((0, steps // 4, (3 * steps) // 4, steps - 1), phases):
            pl.when(j == step)(phase)
        for step, phase in zip((1, steps // 4 + 1, (3 * steps) // 4 + 1, steps - 1), small_phases):
            pl.when(j == step)(phase)

        @pl.when(j == steps - 1)
        def _():
            for t in range(n):
                outs[t][...] = fulls[t][...]
        ob_ref[0] = lax.dot_general(a_ref[...], b_ref[...], (((0,), (0,)), ((), ())),
                                    preferred_element_type=F32).astype(BF16)

    res = pl.pallas_call(
        body, name=name, grid=(steps,),
        in_specs=[pl.BlockSpec((s_len, D), lambda j: (0, 0), pipeline_mode=pl.Buffered(1)),
                  pl.BlockSpec((s_len, tn), lambda j: (0, j))] + [VMEM_SPEC] * (n + nb),
        out_specs=[pl.BlockSpec((1, D, tn), lambda j: (j // per, 0, j % per))] + [VMEM_SPEC] * (n + nb),
        out_shape=([_sds((n_cols // cb, D, cb), BF16)] + [_sds((2,) + s, F32) for s in shapes]
                   + [_sds(bl.shape, F32) for bl in blobs]),
        scratch_shapes=[pltpu.VMEM((2,) + s, F32) for s in shapes] + _reduce_scratch(shapes) + _sum_small_scratch(blobs),
        compiler_params=_params(("arbitrary",)),
    )(a, b, *grads, *blobs)
    return res[0], res[1:1 + n], res[1 + n:]


def _scatter_grads(grads):
    n = len(grads)

    def body(*refs):
        ins, outs = refs[:n], refs[n:2 * n]
        lsem, send, recv = refs[2 * n:]
        me = _place()
        dev, chip, c = _dev_of(me), _chip_of(me), me[2]
        local = [pltpu.make_async_copy(ins[t].at[chip, c], outs[t].at[dev], lsem.at[t]) for t in range(n)]
        for cp in local:
            cp.start()
        sends = []
        for t in range(n):
            for k in ALL_PEERS:
                to = _peer(k)
                cp = _rdma(ins[t].at[_chip_of(to), to[2]], outs[t].at[dev], send.at[t, k], recv.at[t, k], to)
                cp.start()
                sends.append(cp)
        for t in range(n):
            for k in ALL_PEERS:
                blk = outs[t].at[_dev_of(_peer(k))]
                _rdma(blk, blk, send.at[t, k], recv.at[t, k], _peer(k)).wait_recv()
        for cp in sends:
            cp.wait_send()
        for cp in local:
            cp.wait()

    return pl.pallas_call(
        body, name="scatter_grads",
        in_specs=[ANY_SPEC] * n, out_specs=[ANY_SPEC] * n,
        out_shape=[_sds((N_DEV,) + g.shape[2:], g.dtype) for g in grads],
        scratch_shapes=[pltpu.SemaphoreType.DMA((n,))] + [pltpu.SemaphoreType.DMA((n, N_DEV))] * 2,
        compiler_params=_params(),
    )(*grads)


def _sum_devices(parts, name):
    _, r, cols = parts.shape
    tr = min(r, 128)

    def body(in_ref, o_ref):
        acc = in_ref[0].astype(F32)
        for d in range(1, N_DEV):
            acc = acc + in_ref[d].astype(F32)
        o_ref[...] = acc

    return pl.pallas_call(
        body, name=name, grid=(r // tr,),
        in_specs=[pl.BlockSpec((N_DEV, tr, cols), lambda i: (0, i, 0))],
        out_specs=pl.BlockSpec((tr, cols), lambda i: (i, 0)),
        out_shape=_sds((r, cols), F32),
        compiler_params=_params(("arbitrary",)),
    )(parts)


def _share_halves(reds):
    n = len(reds)

    def body(*refs):
        ins, outs = refs[:n], refs[n:2 * n]
        lsem, send, recv = refs[2 * n:]
        me = _place()
        c = me[2]
        local = [pltpu.make_async_copy(ins[t], outs[t].at[c], lsem.at[t]) for t in range(n)]
        sends = [_rdma(ins[t], outs[t].at[c], send.at[t], recv.at[t], _peer(SIBLING)) for t in range(n)]
        for cp in local + sends:
            cp.start()
        for t in range(n):
            _rdma(ins[t], outs[t].at[1 - c], send.at[t], recv.at[t], _peer(SIBLING)).wait_recv()
        for cp in sends:
            cp.wait_send()
        for cp in local:
            cp.wait()

    return pl.pallas_call(
        body, name="share_halves",
        in_specs=[VMEM_SPEC] * n, out_specs=[VMEM_SPEC] * n,
        out_shape=[_sds((2,) + r.shape, r.dtype) for r in reds],
        scratch_shapes=[pltpu.SemaphoreType.DMA((n,))] * 3,
        compiler_params=_params(),
    )(*reds)


def _sum_small(blobs):
    n = len(blobs)

    def body(*refs):
        ins, outs = refs[:n], refs[n:2 * n]
        pbufs, buf4s = refs[2 * n:3 * n], refs[3 * n:4 * n]
        psend, precv, send, recv = refs[4 * n:]
        me = _place()
        chip = _chip_of(me)
        pairs = [_rdma(ins[t], pbufs[t], psend.at[t], precv.at[t], _peer(SIBLING)) for t in range(n)]
        for cp in pairs:
            cp.start()
        out = []
        for t in range(n):
            pairs[t].wait()
            buf4s[t][chip] = ins[t][...] + pbufs[t][...]
            for k in CHIP_PEERS:
                cp = _rdma(buf4s[t].at[chip], buf4s[t].at[chip], send.at[t, k], recv.at[t, k], _peer(k))
                cp.start()
                out.append(cp)
        for t in range(n):
            for k in CHIP_PEERS:
                blk = buf4s[t].at[_chip_of(_peer(k))]
                _rdma(blk, blk, send.at[t, k], recv.at[t, k], _peer(k)).wait_recv()
            outs[t][...] = (buf4s[t][0] + buf4s[t][1]) + (buf4s[t][2] + buf4s[t][3])
        for cp in out:
            cp.wait_send()

    return pl.pallas_call(
        body, name="sum_small",
        in_specs=[VMEM_SPEC] * n, out_specs=[VMEM_SPEC] * n, out_shape=[_sds(b.shape, F32) for b in blobs],
        scratch_shapes=([pltpu.VMEM(b.shape, F32) for b in blobs] + [pltpu.VMEM((N_CHIP,) + b.shape, F32) for b in blobs]
                        + [pltpu.SemaphoreType.DMA((n,))] * 2 + [pltpu.SemaphoreType.DMA((n, N_DEV))] * 2),
        compiler_params=_params(),
    )(*blobs)


def _adamw_math(w, g, m, v):
    m = ADAM_B1 * m + (1.0 - ADAM_B1) * g
    v = ADAM_B2 * v + (1.0 - ADAM_B2) * (g * g)
    m_hat = m / (1.0 - ADAM_B1 ** ADAM_STEP)
    v_hat = v / (1.0 - ADAM_B2 ** ADAM_STEP)
    delta = -ADAM_LR * (m_hat / (jnp.sqrt(v_hat) + ADAM_EPS) + ADAM_WD * w)
    return delta, m, v


def _row_tile(r, cols):
    if r * cols * 4 <= 2 ** 20:
        return r
    return next(t for t in (512, 256, 128, 64, 32, 16, 8) if r % t == 0 and t * cols * 4 <= 2 ** 20)


def _adamw(w, g, m, v, name):
    r, cols = w.shape
    tr = _row_tile(r, cols)

    def body(w_ref, g_ref, m_ref, v_ref, d_ref, nm_ref, nv_ref):
        d_ref[...], nm_ref[...], nv_ref[...] = _adamw_math(w_ref[...], g_ref[...], m_ref[...], v_ref[...])

    spec = pl.BlockSpec((tr, cols), lambda i: (i, 0))
    return pl.pallas_call(
        body, name=name, grid=(r // tr,), in_specs=[spec] * 4, out_specs=[spec] * 3,
        out_shape=[_sds((r, cols), F32)] * 3, compiler_params=_params(("arbitrary",)),
    )(w, g, m, v)


def _adamw_ada(w, ct, dm, m, v):
    r, cols = w.shape
    tr = _row_tile(r, cols)

    def body(w_ref, ct_ref, dm_ref, m_ref, v_ref, g_ref, d_ref, nm_ref, nv_ref):
        g = jnp.dot(ct_ref[...], dm_ref[...], preferred_element_type=F32)
        g_ref[...] = g
        d_ref[...], nm_ref[...], nv_ref[...] = _adamw_math(w_ref[...], g, m_ref[...], v_ref[...])

    spec = pl.BlockSpec((tr, cols), lambda i: (i, 0))
    return pl.pallas_call(
        body, name="adamw_ada", grid=(r // tr,),
        in_specs=[spec, pl.BlockSpec((tr, LANE), lambda i: (i, 0)), pl.BlockSpec((LANE, cols), lambda i: (0, 0)), spec, spec],
        out_specs=[spec] * 4, out_shape=[_sds((r, cols), F32)] * 4, compiler_params=_params(("arbitrary",)),
    )(w, ct, dm, m, v)


BLOB_VEC, BLOB_BSGU, BLOB_CONV, BLOB_ADA, BLOB_DMOD, BLOB_LOSS, BLOB_ROWS = 0, 8, 16, 48, 56, 80, 88
N_VEC = 7


def _adamw_small(tot, g_w_sgu, g_conv, params):
    n = len(params)

    def body(*refs):
        tot_ref, gws_ref, gconv_ref = refs[:3]
        wmv = refs[3:3 + 3 * n]
        outs = refs[3 + 3 * n:]
        grads = [tot_ref[pl.ds(BLOB_VEC + i, 1), :] for i in range(N_VEC)]
        grads += [tot_ref[pl.ds(BLOB_BSGU, HEADS), pl.ds(0, CHUNK)], gconv_ref[...], gws_ref[...], tot_ref[pl.ds(BLOB_ADA, 3), :]]
        for i, g in enumerate(grads):
            w_ref, m_ref, v_ref = wmv[3 * i:3 * i + 3]
            d, nm, nv = _adamw_math(w_ref[...], g, m_ref[...], v_ref[...])
            outs[4 * i][...] = g
            outs[4 * i + 1][...] = d
            outs[4 * i + 2][...] = nm
            outs[4 * i + 3][...] = nv

    flat = [a for wmv in params for a in wmv]
    return pl.pallas_call(
        body, name="adamw_small",
        in_specs=[VMEM_SPEC] * (3 + len(flat)), out_specs=[VMEM_SPEC] * (4 * n),
        out_shape=[_sds(wmv[0].shape, F32) for wmv in params for _ in range(4)],
        compiler_params=_params(),
    )(tot, g_w_sgu, g_conv, *flat)


def _set_rows(buf, row, val):
    return lax.dynamic_update_slice(buf, val.astype(F32), (row, 0))


def kernel(x, c, w_ada, b_ada, g_pre, w_in, conv_w, conv_b, conv_ln_g, conv_ln_b, w_conv_out, sgu_ln_g, sgu_ln_b, w_sgu, b_sgu, w_sgu_out, w_o, g_final, loss_target, m_w_ada, m_b_ada, m_g_pre, m_w_in, m_conv_w, m_conv_b, m_conv_ln_g, m_conv_ln_b, m_w_conv_out, m_sgu_ln_g, m_sgu_ln_b, m_w_sgu, m_b_sgu, m_w_sgu_out, m_w_o, m_g_final, v_w_ada, v_b_ada, v_g_pre, v_w_in, v_conv_w, v_conv_b, v_conv_ln_g, v_conv_ln_b, v_w_conv_out, v_sgu_ln_g, v_sgu_ln_b, v_w_sgu, v_b_sgu, v_w_sgu_out, v_w_o, v_g_final):
    me = _place()
    dev, chip = _dev_of(me), _chip_of(me)
    n_ada = w_ada.shape[2]
    conv_cols = conv_w.shape[2]

    b_ada_s = lax.dynamic_slice(b_ada, (0, chip * n_ada), (1, n_ada))
    c_all, mod_all, cw_all = _setup_comm(
        jnp.broadcast_to(c, (SUB, D)), w_ada[0], b_ada_s, jnp.pad(conv_w[0], ((0, HALO - CONV_K), (0, 0))))
    mod = lax.dynamic_slice(mod_all, (0, dev * SUB, 0), (N_CHIP, 1, n_ada)).reshape(1, 3 * D)
    shift, scale, gate = mod[:, :D], mod[:, D:2 * D], mod[:, 2 * D:]
    conv_w_full = jnp.swapaxes(cw_all, 0, 1).reshape(HALO, D)[:CONV_K]

    wg_in, wg_co, wg_so, wg_o = _gather_weights(
        [w_in[0].astype(BF16), w_conv_out[0].astype(BF16), w_sgu_out[0].astype(BF16), w_o[0].astype(BF16)])

    loc = _local_step(x[0], loss_target[0], shift, scale, gate, g_pre, conv_w_full, conv_b, conv_ln_g, conv_ln_b,
                      sgu_ln_g, sgu_ln_b, w_sgu[0], b_sgu[0], g_final.reshape(1, D),
                      wg_in, wg_co.reshape(D, D), wg_so.reshape(D, D), wg_o.reshape(D, D))

    d_mod = jnp.concatenate([loc["d_shift"], loc["d_scale"], loc["d_gate"]], axis=0)
    blob = jnp.zeros((BLOB_ROWS, D), F32)
    for i, name in enumerate(["g_pre", "conv_b", "conv_ln_g", "conv_ln_b", "sgu_ln_g", "sgu_ln_b", "g_final"]):
        blob = _set_rows(blob, BLOB_VEC + i, loc[name])
    blob = _set_rows(blob, BLOB_BSGU, loc["b_sgu"])
    blob = _set_rows(blob, BLOB_CONV, loc["conv_w"])
    blob = _set_rows(blob, BLOB_ADA, d_mod)
    blob = lax.dynamic_update_slice(blob, d_mod, (BLOB_DMOD + 3 * dev, 0))
    blob = _set_rows(blob, BLOB_LOSS, loc["loss_cols"])

    big = ["w_in", "w_conv_out", "w_sgu_out", "w_o"]
    contrib_out = [loc[name].reshape(N_CHIP, 2, D // (2 * N_CHIP), D) for name in big[1:]]
    gw_in, full_out, (tot, g_w_sgu) = _grad_matmul_reduce(
        loc["hb"], loc["dp"], "grad_w_in", contrib_out, [blob, loc["w_sgu"].reshape(HEADS * CHUNK, CHUNK)])
    full_in = _reduce_scatter([gw_in.reshape(N_CHIP, 2, D // 2, gw_in.shape[2])], "reduce_w_in")
    g_big = {name: f.reshape(2 * f.shape[1], f.shape[2]) for name, f in zip(big, list(full_in) + list(full_out))}

    loss = jnp.sum(tot[BLOB_LOSS])
    g_conv_s = lax.dynamic_slice(tot, (BLOB_CONV, chip * conv_cols), (CONV_K, conv_cols))
    d_mod_all = tot[BLOB_DMOD:BLOB_DMOD + 3 * N_DEV].reshape(N_DEV, 3 * D)

    ct = jnp.pad(c_all[::SUB].T, ((0, 0), (0, LANE - N_DEV))).astype(BF16)
    dm = jnp.pad(lax.dynamic_slice(d_mod_all, (0, chip * n_ada), (N_DEV, n_ada)), ((0, LANE - N_DEV), (0, 0))).astype(BF16)
    g_ada, d_ada, nm_ada, nv_ada = _adamw_ada(w_ada[0], ct, dm, m_w_ada[0], v_w_ada[0])

    upd = {}
    for name, w, m, v in [("w_in", w_in, m_w_in, v_w_in), ("w_conv_out", w_conv_out, m_w_conv_out, v_w_conv_out),
                          ("w_sgu_out", w_sgu_out, m_w_sgu_out, v_w_sgu_out), ("w_o", w_o, m_w_o, v_w_o)]:
        upd[name] = _adamw(w[0], g_big[name], m[0], v[0], "adamw_" + name)

    def wmv(w, m, v, shape):
        return tuple(a.reshape(shape) for a in (w, m, v))

    small_params = [wmv(w, m, v, (1, D)) for w, m, v in [
        (g_pre, m_g_pre, v_g_pre), (conv_b, m_conv_b, v_conv_b), (conv_ln_g, m_conv_ln_g, v_conv_ln_g),
        (conv_ln_b, m_conv_ln_b, v_conv_ln_b), (sgu_ln_g, m_sgu_ln_g, v_sgu_ln_g), (sgu_ln_b, m_sgu_ln_b, v_sgu_ln_b),
        (g_final, m_g_final, v_g_final)]]
    small_params += [wmv(b_sgu, m_b_sgu, v_b_sgu, (HEADS, CHUNK)), wmv(conv_w, m_conv_w, v_conv_w, (CONV_K, conv_cols)),
                     wmv(w_sgu, m_w_sgu, v_w_sgu, (HEADS * CHUNK, CHUNK)), wmv(b_ada, m_b_ada, v_b_ada, (3, D))]
    small_out = _adamw_small(tot, g_w_sgu, g_conv_s, small_params)

    def leaves(kind):
        vecs = [small_out[4 * i + kind] for i in range(N_VEC)]
        o_b_sgu, o_conv, o_w_sgu, o_b_ada = (small_out[4 * (N_VEC + i) + kind] for i in range(4))
        ada = (g_ada, d_ada, nm_ada, nv_ada)[kind]
        def bigk(name):
            return (g_big[name] if kind == 0 else upd[name][kind - 1])[None]
        return [ada[None], o_b_ada.reshape(1, 3 * D), vecs[0], bigk("w_in"), o_conv[None], vecs[1], vecs[2], vecs[3],
                bigk("w_conv_out"), vecs[4], vecs[5], o_w_sgu.reshape(1, HEADS, CHUNK, CHUNK), o_b_sgu[None],
                bigk("w_sgu_out"), bigk("w_o"), vecs[6].reshape(D)]

    return (loss, loc["grad_x"][None], *leaves(0), *leaves(1), *leaves(2), *leaves(3))
```

```python
import functools

import jax
import jax.numpy as jnp
from jax import lax
from jax.experimental import pallas as pl
from jax.experimental.pallas import tpu as pltpu

F32 = jnp.float32
BF16 = jnp.bfloat16
MESH = pl.DeviceIdType.MESH

D = 1024
N_SEC = 8
N_CHIP = 4
N_DEV = 8
EPS = 1e-6
CONV_K = 31
HALO = 32
CHUNK = 128
HEADS = 8
LANE = 128
SUB = 8
PACK = 16
VMEM_LIMIT = 56 * 1024 * 1024

ADAM_LR, ADAM_B1, ADAM_B2, ADAM_EPS, ADAM_WD, ADAM_STEP = 0.001, 0.9, 0.999, 1e-08, 0.01, 10

_SQRT_HALF = 0.7071067811865476
_INV_SQRT_2PI = 0.3989422804014327


def _sds(shape, dtype):
    return jax.ShapeDtypeStruct(shape, dtype)


def _params(sem=None):
    if sem is None:
        return pltpu.CompilerParams(vmem_limit_bytes=VMEM_LIMIT)
    return pltpu.CompilerParams(dimension_semantics=sem, vmem_limit_bytes=VMEM_LIMIT)


def _strips(n_rows, rows, fn):
    def step(s, carry):
        fn(pl.multiple_of(s * rows, rows))
        return carry
    lax.fori_loop(0, n_rows // rows, step, 0)


def _sigmoid(v):
    return 1.0 / (1.0 + jnp.exp(-v))


def _gelu(v):
    return 0.5 * v * (1.0 + lax.erf(v * _SQRT_HALF))


def _gelu_and_grad(v):
    cdf = 0.5 * (1.0 + lax.erf(v * _SQRT_HALF))
    return v * cdf, cdf + v * jnp.exp(-0.5 * v * v) * _INV_SQRT_2PI


def _dsilu(v, sg):
    return sg * (1.0 + v * (1.0 - sg))


def _rowmean(v):
    return jnp.mean(v, axis=-1, keepdims=True)


def _vec_spec(grid_rank):
    zeros = (0, 0)
    if grid_rank == 1:
        return pl.BlockSpec((1, D), lambda i: zeros)
    return pl.BlockSpec((1, D), lambda i, j: zeros)


def _in_proj(x, shift, scale, g_pre, wg_in):
    s_len = x.shape[0]
    tm = min(512, s_len)
    n_i = s_len // tm
    wn = wg_in.shape[2]

    def body(x_ref, sh_ref, sc_ref, g_ref, w_ref, p_ref, hb_ref):
        def strip(r0):
            xs = x_ref[pl.ds(r0, PACK), :]
            r = lax.rsqrt(_rowmean(xs * xs) + EPS)
            h = (xs * r) * g_ref[...] * (1.0 + sc_ref[...]) + sh_ref[...]
            hb_ref[pl.ds(r0, PACK), :] = h.astype(BF16)
        _strips(tm, PACK, strip)
        hb = hb_ref[...]
        for j in range(N_CHIP):
            p_ref[:, pl.ds(j * wn, wn)] = jnp.dot(hb, w_ref[j], preferred_element_type=F32).astype(BF16)

    return pl.pallas_call(
        body, name="in_proj", grid=(n_i,),
        in_specs=[pl.BlockSpec((tm, D), lambda i: (i, 0)), _vec_spec(1), _vec_spec(1), _vec_spec(1),
                  pl.BlockSpec((N_CHIP, D, wn), lambda i: (0, 0, 0), pipeline_mode=pl.Buffered(1))],
        out_specs=[pl.BlockSpec((tm, N_CHIP * wn), lambda i: (i, 0)), pl.BlockSpec((tm, D), lambda i: (i, 0))],
        out_shape=[_sds((s_len, N_SEC * D), BF16), _sds((s_len, D), BF16)],
        compiler_params=_params(("arbitrary",)),
    )(x, shift, scale, g_pre, wg_in)


def _conv_taps(win_ref, r0, lt, weight_of_offset, rows):
    lanes = pl.ds(lt * LANE, LANE)
    win = win_ref[pl.ds(r0, rows + HALO), lanes]
    n_out = rows // SUB
    acc = [jnp.zeros((SUB, LANE), F32) for _ in range(n_out)]
    for phase in range(SUB):
        offs = [o for o in weight_of_offset if o % SUB == phase]
        if not offs:
            continue
        q_max = max(o // SUB for o in offs)
        span = (n_out + q_max) * SUB
        sh = win[phase:phase + span, :]
        for o in offs:
            q = o // SUB
            w = weight_of_offset[o](lanes)
            for m in range(n_out):
                acc[m] = acc[m] + w * sh[(m + q) * SUB:(m + q + 1) * SUB, :]
    return acc


def _branch_a_fwd(p, conv_wb, conv_b, ln_g, ln_b):
    s_len = p.shape[0]
    tm = min(256, s_len)
    n_i = s_len // tm
    rows = 32

    def body(p_ref, wb_ref, cb_ref, g_ref, b_ref, ya_ref, y1_ref, abuf):
        @pl.when(pl.program_id(0) == 0)
        def _():
            abuf[pl.ds(0, HALO), :] = jnp.zeros((HALO, D), F32)

        def glu(r0):
            val = p_ref[pl.ds(r0, PACK), pl.ds(0, D)].astype(F32)
            gl = p_ref[pl.ds(r0, PACK), pl.ds(D, D)].astype(F32)
            abuf[pl.ds(HALO + r0, PACK), :] = val * _sigmoid(gl)
        _strips(tm, PACK,glu)

        taps = {HALO - (CONV_K - 1) + k: (lambda lanes, k=k: wb_ref[pl.ds(k * SUB, SUB), lanes]) for k in range(CONV_K)}

        def conv(r0):
            for lt in range(D // LANE):
                acc = _conv_taps(abuf, r0, lt, taps, rows)
                cb = cb_ref[:, pl.ds(lt * LANE, LANE)]
                for m, v in enumerate(acc):
                    y1_ref[pl.ds(r0 + m * SUB, SUB), pl.ds(lt * LANE, LANE)] = v + cb
        _strips(tm, rows, conv)

        def norm(r0):
            y1 = y1_ref[pl.ds(r0, PACK), :]
            mu = _rowmean(y1)
            yc = y1 - mu
            rstd = lax.rsqrt(_rowmean(yc * yc) + EPS)
            l1 = (yc * rstd) * g_ref[...] + b_ref[...]
            z = p_ref[pl.ds(r0, PACK), pl.ds(2 * D, D)].astype(F32)
            ya_ref[pl.ds(r0, PACK), :] = ((l1 * _sigmoid(l1)) * (z * _sigmoid(z))).astype(BF16)
        _strips(tm, PACK,norm)

        abuf[pl.ds(0, HALO), :] = abuf[pl.ds(tm, HALO), :]

    return pl.pallas_call(
        body, name="branch_a_fwd", grid=(n_i,),
        in_specs=[pl.BlockSpec((tm, 3 * D), lambda i: (i, 0)),
                  pl.BlockSpec((CONV_K * SUB, D), lambda i: (0, 0)), _vec_spec(1), _vec_spec(1), _vec_spec(1)],
        out_specs=[pl.BlockSpec((tm, D), lambda i: (i, 0)), pl.BlockSpec((tm, D), lambda i: (i, 0))],
        out_shape=[_sds((s_len, D), BF16), _sds((s_len, D), F32)],
        scratch_shapes=[pltpu.VMEM((tm + HALO, D), F32)],
        compiler_params=_params(("arbitrary",)),
    )(p, conv_wb, conv_b, ln_g, ln_b)


def _branch_b_fwd(p, wt, bias_full, ln_g, ln_b):
    s_len = p.shape[0]
    tm = min(256, s_len)
    n_i = s_len // tm

    def body(p_ref, wt_ref, bias_ref, g_ref, b_ref, yb_ref, vb, sbuf):
        def norm(r0):
            gv = _gelu(p_ref[pl.ds(r0, PACK), pl.ds(D, D)].astype(F32))
            mu = _rowmean(gv)
            vc = gv - mu
            rstd = lax.rsqrt(_rowmean(vc * vc) + EPS)
            vb[pl.ds(r0, PACK), :] = ((vc * rstd) * g_ref[...] + b_ref[...]).astype(BF16)
        _strips(tm, PACK,norm)

        for ck in range(tm // CHUNK):
            for h in range(HEADS):
                blk = (pl.ds(ck * CHUNK, CHUNK), pl.ds(h * LANE, LANE))
                sbuf[blk] = jnp.dot(wt_ref[h], vb[blk], preferred_element_type=F32) + bias_ref[:, pl.ds(h * LANE, LANE)]

        def gate(r0):
            u = _gelu(p_ref[pl.ds(r0, PACK), pl.ds(0, D)].astype(F32))
            z = p_ref[pl.ds(r0, PACK), pl.ds(2 * D, D)].astype(F32)
            yb_ref[pl.ds(r0, PACK), :] = (u * sbuf[pl.ds(r0, PACK), :] * (z * _sigmoid(z))).astype(BF16)
        _strips(tm, PACK,gate)

    return pl.pallas_call(
        body, name="branch_b_fwd", grid=(n_i,),
        in_specs=[pl.BlockSpec((tm, 3 * D), lambda i: (i, 1)),
                  pl.BlockSpec((HEADS, CHUNK, CHUNK), lambda i: (0, 0, 0)),
                  pl.BlockSpec((CHUNK, D), lambda i: (0, 0)), _vec_spec(1), _vec_spec(1)],
        out_specs=pl.BlockSpec((tm, D), lambda i: (i, 0)),
        out_shape=_sds((s_len, D), BF16),
        scratch_shapes=[pltpu.VMEM((tm, D), BF16), pltpu.VMEM((tm, D), F32)],
        compiler_params=_params(("arbitrary",)),
    )(p, wt, bias_full, ln_g, ln_b)


def _dot_t(a, b):
    return lax.dot_general(a, b, (((1,), (1,)), ((), ())), preferred_element_type=F32)


def _out_proj(p, ya_in, yb_in, x, target, gate, g_final, w_co, w_so, w_o):
    s_len = x.shape[0]
    tm = min(256, s_len)
    n_i = s_len // tm

    def body(pg_ref, ya_ref, yb_ref, x_ref, t_ref, gate_ref, gf_ref, wco_ref, wso_ref, wo_ref,
             dx2_ref, dya_ref, dyb_ref, dp_ref, mb_ref, dob_ref, dyab_ref, dybb_ref, sums_ref):
        @pl.when(pl.program_id(0) == 0)
        def _():
            sums_ref[...] = jnp.zeros((SUB, D), F32)

        y_a = jnp.dot(ya_ref[...], wco_ref[...], preferred_element_type=F32)
        y_b = jnp.dot(yb_ref[...], wso_ref[...], preferred_element_type=F32)
        ga = _sigmoid(pg_ref[:, pl.ds(0, D)].astype(F32))
        gb = _sigmoid(pg_ref[:, pl.ds(D, D)].astype(F32))
        mb = (ga * y_a + gb * y_b).astype(BF16)
        mb_ref[...] = mb
        o = jnp.dot(mb, wo_ref[...], preferred_element_type=F32)
        x2 = x_ref[...] + gate_ref[...] * o
        r2 = lax.rsqrt(_rowmean(x2 * x2) + EPS)
        xh = x2 * r2
        e = xh * gf_ref[...] - t_ref[...]
        dy = e * (1.0 / D)
        dxh = dy * gf_ref[...]
        dx2 = r2 * (dxh - xh * _rowmean(dxh * xh))
        dx2_ref[...] = dx2
        sums_ref[pl.ds(0, 1), :] += jnp.sum(dy * xh, axis=0, keepdims=True)
        sums_ref[pl.ds(1, 1), :] += jnp.sum(dx2 * o, axis=0, keepdims=True)
        sums_ref[pl.ds(2, 1), :] += jnp.sum(e * e, axis=0, keepdims=True) * (0.5 / D)
        dob = (gate_ref[...] * dx2).astype(BF16)
        dob_ref[...] = dob
        dm = _dot_t(dob, wo_ref[...])
        dy_a = (ga * dm).astype(BF16)
        dy_b = (gb * dm).astype(BF16)
        dyab_ref[...] = dy_a
        dybb_ref[...] = dy_b
        dp_ref[:, pl.ds(0, D)] = (dm * y_a * ga * (1.0 - ga)).astype(BF16)
        dp_ref[:, pl.ds(D, D)] = (dm * y_b * gb * (1.0 - gb)).astype(BF16)
        dya_ref[...] = _dot_t(dy_a, wco_ref[...])
        dyb_ref[...] = _dot_t(dy_b, wso_ref[...])

    tile = pl.BlockSpec((tm, D), lambda i: (i, 0))
    wspec = pl.BlockSpec((D, D), lambda i: (0, 0))
    return pl.pallas_call(
        body, name="out_proj", grid=(n_i,),
        in_specs=[pl.BlockSpec((tm, 2 * D), lambda i: (i, 3)), tile, tile, tile, tile, _vec_spec(1), _vec_spec(1),
                  wspec, wspec, wspec],
        out_specs=[tile, tile, tile, pl.BlockSpec((tm, 2 * D), lambda i: (i, 3)), tile, tile, tile, tile,
                   pl.BlockSpec((SUB, D), lambda i: (0, 0))],
        out_shape=[_sds((s_len, D), F32), _sds((s_len, D), F32), _sds((s_len, D), F32), _sds((s_len, N_SEC * D), BF16),
                   _sds((s_len, D), BF16), _sds((s_len, D), BF16), _sds((s_len, D), BF16), _sds((s_len, D), BF16),
                   _sds((SUB, D), F32)],
        compiler_params=_params(("arbitrary",)),
    )(p, ya_in, yb_in, x, target, gate, g_final, w_co, w_so, w_o)


A_STATS_ROWS = 8 + HALO


def _branch_a_bwd(p, y1, dya_in, dp, conv_wb, ln_g, ln_b):
    s_len = p.shape[0]
    tm = min(256, s_len)
    n_i = s_len // tm
    rows = 32
    n_out = rows // SUB

    def tile_of(i):
        return n_i - 1 - i

    def body(p_ref, y1_ref, dya_ref, dp_in, wb_ref, g_ref, b_ref, dp_ref, st_ref, dybuf, acc8, tapacc):
        del dp_in
        i = pl.program_id(0)

        @pl.when(i == 0)
        def _():
            dybuf[pl.ds(tm, HALO), :] = jnp.zeros((HALO, D), F32)
            st_ref[...] = jnp.zeros((A_STATS_ROWS, D), F32)
            acc8[...] = jnp.zeros((3 * PACK, D), F32)
            tapacc[...] = jnp.zeros((CONV_K * SUB, D), F32)

        def norm_bwd(r0):
            y1 = y1_ref[pl.ds(r0, PACK), :]
            mu = _rowmean(y1)
            yc = y1 - mu
            rstd = lax.rsqrt(_rowmean(yc * yc) + EPS)
            n1 = yc * rstd
            l1 = n1 * g_ref[...] + b_ref[...]
            sg = _sigmoid(l1)
            z = p_ref[pl.ds(r0, PACK), pl.ds(2 * D, D)].astype(F32)
            sz = _sigmoid(z)
            dya = dya_ref[pl.ds(r0, PACK), :]
            dp_ref[pl.ds(r0, PACK), pl.ds(2 * D, D)] = (dya * (l1 * sg) * _dsilu(z, sz)).astype(BF16)
            dl1 = dya * (z * sz) * _dsilu(l1, sg)
            acc8[pl.ds(0, PACK), :] += dl1 * n1
            acc8[pl.ds(PACK, PACK), :] += dl1
            dn1 = dl1 * g_ref[...]
            dy1 = rstd * (dn1 - _rowmean(dn1) - n1 * _rowmean(dn1 * n1))
            acc8[pl.ds(2 * PACK, PACK), :] += dy1
            dybuf[pl.ds(r0, PACK), :] = dy1
        _strips(tm, PACK,norm_bwd)

        def conv_bwd(r0):
            for lt in range(D // LANE):
                lanes = pl.ds(lt * LANE, LANE)
                glanes = pl.ds(D + lt * LANE, LANE)
                win = dybuf[pl.ds(r0, rows + HALO), lanes]
                sg16, a16 = [], []
                for h in range(rows // PACK):
                    rr = pl.ds(r0 + h * PACK, PACK)
                    s = _sigmoid(p_ref[rr, glanes].astype(F32))
                    sg16.append(s)
                    a16.append(p_ref[rr, lanes].astype(F32) * s)
                a = [a16[m // 2][(m % 2) * SUB:(m % 2 + 1) * SUB, :] for m in range(n_out)]
                da = [jnp.zeros((SUB, LANE), F32) for _ in range(n_out)]
                for phase in range(SUB):
                    offs = [o for o in range(CONV_K) if o % SUB == phase]
                    q_max = max(o // SUB for o in offs)
                    sh = win[phase:phase + (n_out + q_max) * SUB, :]
                    for o in offs:
                        k, q = CONV_K - 1 - o, o // SUB
                        w = wb_ref[pl.ds(k * SUB, SUB), lanes]
                        part = None
                        for m in range(n_out):
                            s = sh[(m + q) * SUB:(m + q + 1) * SUB, :]
                            da[m] = da[m] + w * s
                            part = a[m] * s if part is None else part + a[m] * s
                        tapacc[pl.ds(k * SUB, SUB), lanes] += part
                for h in range(rows // PACK):
                    rr = pl.ds(r0 + h * PACK, PACK)
                    da16 = jnp.concatenate(da[2 * h:2 * h + 2], axis=0)
                    dp_ref[rr, lanes] = (da16 * sg16[h]).astype(BF16)
                    dp_ref[rr, glanes] = (da16 * a16[h] * (1.0 - sg16[h])).astype(BF16)
        _strips(tm, rows, conv_bwd)

        dybuf[pl.ds(tm, HALO), :] = dybuf[pl.ds(0, HALO), :]

        @pl.when(i == n_i - 1)
        def _():
            for j in range(3):
                st_ref[pl.ds(j, 1), :] = jnp.sum(acc8[pl.ds(j * PACK, PACK), :], axis=0, keepdims=True)
            for k in range(CONV_K):
                st_ref[pl.ds(SUB + k, 1), :] = jnp.sum(tapacc[pl.ds(k * SUB, SUB), :], axis=0, keepdims=True)

    return pl.pallas_call(
        body, name="branch_a_bwd", grid=(n_i,),
        in_specs=[pl.BlockSpec((tm, 3 * D), lambda i: (tile_of(i), 0)),
                  pl.BlockSpec((tm, D), lambda i: (tile_of(i), 0)),
                  pl.BlockSpec((tm, D), lambda i: (tile_of(i), 0)),
                  pl.BlockSpec(memory_space=pl.ANY),
                  pl.BlockSpec((CONV_K * SUB, D), lambda i: (0, 0)), _vec_spec(1), _vec_spec(1)],
        out_specs=[pl.BlockSpec((tm, 3 * D), lambda i: (tile_of(i), 0)),
                   pl.BlockSpec((A_STATS_ROWS, D), lambda i: (0, 0))],
        out_shape=[_sds(dp.shape, BF16), _sds((A_STATS_ROWS, D), F32)],
        scratch_shapes=[pltpu.VMEM((tm + HALO, D), F32), pltpu.VMEM((3 * PACK, D), F32), pltpu.VMEM((CONV_K * SUB, D), F32)],
        input_output_aliases={3: 0},
        compiler_params=_params(("arbitrary",)),
    )(p, y1, dya_in, dp, conv_wb, ln_g, ln_b)


def _branch_b_bwd(p, dyb_in, dp, wt, wtt, bias_full, ln_g, ln_b):
    s_len = p.shape[0]
    tm = min(256, s_len)
    n_i = s_len // tm

    def body(p_ref, dyb_ref, dp_in, wt_ref, wtt_ref, bias_ref, g_ref, b_ref, dp_ref, st_ref, gbt_ref, gw_ref,
             vb, n2buf, rstdbuf, sbuf, dsb, dvbuf, acc8, gb_ref, dgbuf):
        del dp_in
        i = pl.program_id(0)

        @pl.when(i == 0)
        def _():
            st_ref[...] = jnp.zeros((SUB, D), F32)
            gbt_ref[...] = jnp.zeros((CHUNK, LANE), F32)
            gb_ref[...] = jnp.zeros((CHUNK, D), F32)
            gw_ref[...] = jnp.zeros((HEADS, CHUNK, CHUNK), F32)
            acc8[...] = jnp.zeros((2 * PACK, D), F32)

        def norm(r0):
            gv, dgv = _gelu_and_grad(p_ref[pl.ds(r0, PACK), pl.ds(D, D)].astype(F32))
            dgbuf[pl.ds(r0, PACK), :] = dgv
            mu = _rowmean(gv)
            vc = gv - mu
            rstd = lax.rsqrt(_rowmean(vc * vc) + EPS)
            n2 = vc * rstd
            n2buf[pl.ds(r0, PACK), :] = n2
            rstdbuf[pl.ds(r0, PACK), :] = jnp.broadcast_to(rstd, (PACK, LANE))
            vb[pl.ds(r0, PACK), :] = (n2 * g_ref[...] + b_ref[...]).astype(BF16)
        _strips(tm, PACK,norm)

        for ck in range(tm // CHUNK):
            for h in range(HEADS):
                blk = (pl.ds(ck * CHUNK, CHUNK), pl.ds(h * LANE, LANE))
                sbuf[blk] = jnp.dot(wt_ref[h], vb[blk], preferred_element_type=F32) + bias_ref[:, pl.ds(h * LANE, LANE)]

        def gate_bwd(r0):
            pu = p_ref[pl.ds(r0, PACK), pl.ds(0, D)].astype(F32)
            u, du = _gelu_and_grad(pu)
            z = p_ref[pl.ds(r0, PACK), pl.ds(2 * D, D)].astype(F32)
            sg = _sigmoid(z)
            sz = z * sg
            s = sbuf[pl.ds(r0, PACK), :]
            dyb = dyb_ref[pl.ds(r0, PACK), :]
            ds = dyb * u * sz
            dsb[pl.ds(r0, PACK), :] = ds.astype(BF16)
            gb_ref[pl.ds(pl.multiple_of(r0 % CHUNK, PACK), PACK), :] += ds
            dp_ref[pl.ds(r0, PACK), pl.ds(0, D)] = (dyb * s * sz * du).astype(BF16)
            dp_ref[pl.ds(r0, PACK), pl.ds(2 * D, D)] = (dyb * u * s * _dsilu(z, sg)).astype(BF16)
        _strips(tm, PACK,gate_bwd)

        for ck in range(tm // CHUNK):
            for h in range(HEADS):
                blk = (pl.ds(ck * CHUNK, CHUNK), pl.ds(h * LANE, LANE))
                d_s = dsb[blk]
                dvbuf[blk] = jnp.dot(wtt_ref[h], d_s, preferred_element_type=F32)
                gw_ref[h] += _dot_t(d_s, vb[blk])

        def norm_bwd(r0):
            dv = dvbuf[pl.ds(r0, PACK), :]
            n2 = n2buf[pl.ds(r0, PACK), :]
            rstd = rstdbuf[pl.ds(r0, PACK), pl.ds(0, 1)]
            acc8[pl.ds(0, PACK), :] += dv * n2
            acc8[pl.ds(PACK, PACK), :] += dv
            dn2 = dv * g_ref[...]
            dgv = rstd * (dn2 - _rowmean(dn2) - n2 * _rowmean(dn2 * n2))
            dp_ref[pl.ds(r0, PACK), pl.ds(D, D)] = (dgv * dgbuf[pl.ds(r0, PACK), :]).astype(BF16)
        _strips(tm, PACK,norm_bwd)

        @pl.when(i == n_i - 1)
        def _():
            for j in range(2):
                st_ref[pl.ds(j, 1), :] = jnp.sum(acc8[pl.ds(j * PACK, PACK), :], axis=0, keepdims=True)
            row = lax.broadcasted_iota(jnp.int32, (CHUNK, CHUNK), 0)
            col = lax.broadcasted_iota(jnp.int32, (CHUNK, CHUNK), 1)
            for h in range(HEADS):
                gw_ref[h] = jnp.where(row >= col, gw_ref[h], 0.0)
            lane = lax.broadcasted_iota(jnp.int32, (CHUNK, LANE), 1)
            gbt = jnp.zeros((CHUNK, LANE), F32)
            for h in range(HEADS):
                gbt = jnp.where(lane == h, jnp.sum(gb_ref[:, pl.ds(h * LANE, LANE)], axis=1, keepdims=True), gbt)
            gbt_ref[...] = gbt

    wspec = pl.BlockSpec((HEADS, CHUNK, CHUNK), lambda i: (0, 0, 0))
    return pl.pallas_call(
        body, name="branch_b_bwd", grid=(n_i,),
        in_specs=[pl.BlockSpec((tm, 3 * D), lambda i: (i, 1)), pl.BlockSpec((tm, D), lambda i: (i, 0)),
                  pl.BlockSpec(memory_space=pl.ANY), wspec, wspec,
                  pl.BlockSpec((CHUNK, D), lambda i: (0, 0)), _vec_spec(1), _vec_spec(1)],
        out_specs=[pl.BlockSpec((tm, 3 * D), lambda i: (i, 1)), pl.BlockSpec((SUB, D), lambda i: (0, 0)),
                   pl.BlockSpec((CHUNK, LANE), lambda i: (0, 0)), wspec],
        out_shape=[_sds(dp.shape, BF16), _sds((SUB, D), F32), _sds((CHUNK, LANE), F32), _sds((HEADS, CHUNK, CHUNK), F32)],
        scratch_shapes=[pltpu.VMEM((tm, D), BF16), pltpu.VMEM((tm, D), F32), pltpu.VMEM((tm, LANE), F32),
                        pltpu.VMEM((tm, D), F32), pltpu.VMEM((tm, D), BF16), pltpu.VMEM((tm, D), F32),
                        pltpu.VMEM((2 * PACK, D), F32), pltpu.VMEM((CHUNK, D), F32), pltpu.VMEM((tm, D), F32)],
        input_output_aliases={2: 0},
        compiler_params=_params(("arbitrary",)),
    )(p, dyb_in, dp, wt, wtt, bias_full, ln_g, ln_b)


def _in_proj_bwd(dp, wg_in, x, dx2, shift, scale, g_pre):
    del shift
    s_len = x.shape[0]
    tm = min(512, s_len)
    n_i = s_len // tm
    wn = wg_in.shape[2]

    def body(dp_ref, w_ref, x_ref, dx2_ref, sc_ref, g_ref, gx_ref, st_ref, acc, acc8):
        i = pl.program_id(0)

        @pl.when(i == 0)
        def _():
            st_ref[...] = jnp.zeros((SUB, D), F32)
            acc8[...] = jnp.zeros((3 * PACK, D), F32)

        dh = _dot_t(dp_ref[:, pl.ds(0, wn)], w_ref[0])
        for j in range(1, N_CHIP):
            dh = dh + _dot_t(dp_ref[:, pl.ds(j * wn, wn)], w_ref[j])
        acc[...] = dh

        def strip(r0):
            xs = x_ref[pl.ds(r0, PACK), :]
            r = lax.rsqrt(_rowmean(xs * xs) + EPS)
            xn = xs * r
            dhs = acc[pl.ds(r0, PACK), :]
            acc8[pl.ds(0, PACK), :] += dhs
            acc8[pl.ds(PACK, PACK), :] += dhs * (xn * g_ref[...])
            dhp = dhs * (1.0 + sc_ref[...])
            acc8[pl.ds(2 * PACK, PACK), :] += dhp * xn
            dxn = dhp * g_ref[...]
            gx_ref[pl.ds(r0, PACK), :] = dx2_ref[pl.ds(r0, PACK), :] + r * (dxn - xn * _rowmean(dxn * xn))
        _strips(tm, PACK, strip)

        @pl.when(i == n_i - 1)
        def _():
            for k in range(3):
                st_ref[pl.ds(k, 1), :] = jnp.sum(acc8[pl.ds(k * PACK, PACK), :], axis=0, keepdims=True)

    tile = pl.BlockSpec((tm, D), lambda i: (i, 0))
    return pl.pallas_call(
        body, name="in_proj_bwd", grid=(n_i,),
        in_specs=[pl.BlockSpec((tm, N_CHIP * wn), lambda i: (i, 0)),
                  pl.BlockSpec((N_CHIP, D, wn), lambda i: (0, 0, 0), pipeline_mode=pl.Buffered(1)),
                  tile, tile, _vec_spec(1), _vec_spec(1)],
        out_specs=[tile, pl.BlockSpec((SUB, D), lambda i: (0, 0))],
        out_shape=[_sds((s_len, D), F32), _sds((SUB, D), F32)],
        scratch_shapes=[pltpu.VMEM((tm, D), F32), pltpu.VMEM((3 * PACK, D), F32)],
        compiler_params=_params(("arbitrary",)),
    )(dp, wg_in, x, dx2, scale, g_pre)


def _grad_matmul(a, b, name):
    s_len, n = b.shape
    cb = min(2 * D, n)
    tn = 512
    per = cb // tn

    def body(a_ref, b_ref, ob_ref):
        ob_ref[0] = lax.dot_general(a_ref[...], b_ref[...], (((0,), (0,)), ((), ())),
                                    preferred_element_type=F32).astype(BF16)

    return pl.pallas_call(
        body, name=name, grid=(n // tn,),
        in_specs=[pl.BlockSpec((s_len, D), lambda j: (0, 0), pipeline_mode=pl.Buffered(1)),
                  pl.BlockSpec((s_len, tn), lambda j: (0, j))],
        out_specs=pl.BlockSpec((1, D, tn), lambda j: (j // per, 0, j % per)),
        out_shape=_sds((n // cb, D, cb), BF16),
        compiler_params=_params(("arbitrary",)),
    )(a, b)


def _local_step(x, target, shift, scale, gate, g_pre, conv_w_full, conv_b, conv_ln_g, conv_ln_b,
                sgu_ln_g, sgu_ln_b, w_sgu, b_sgu, g_final, wg_in, w_co, w_so, w_o):
    conv_wb = jnp.repeat(conv_w_full, SUB, axis=0)
    causal = jnp.tril(jnp.ones((CHUNK, CHUNK), dtype=bool))
    wt = jnp.where(causal[None], w_sgu, 0.0).astype(BF16)
    wtt = jnp.swapaxes(wt, 1, 2)
    bias_full = jnp.repeat(b_sgu.T, LANE, axis=1)

    p, hb = _in_proj(x, shift, scale, g_pre, wg_in)
    ya_in, y1 = _branch_a_fwd(p, conv_wb, conv_b, conv_ln_g, conv_ln_b)
    yb_in = _branch_b_fwd(p, wt, bias_full, sgu_ln_g, sgu_ln_b)
    dx2, dya_in, dyb_in, dp, mb, dob, dyab, dybb, sums_o = _out_proj(
        p, ya_in, yb_in, x, target, gate, g_final, w_co, w_so, w_o)
    dp, st_a = _branch_a_bwd(p, y1, dya_in, dp, conv_wb, conv_ln_g, conv_ln_b)
    dp, st_b, gbt, gws = _branch_b_bwd(p, dyb_in, dp, wt, wtt, bias_full, sgu_ln_g, sgu_ln_b)
    grad_x, st_i = _in_proj_bwd(dp, wg_in, x, dx2, shift, scale, g_pre)
    gw_o = _grad_matmul(mb, dob, "grad_w_o")
    gw_co = _grad_matmul(ya_in, dyab, "grad_w_conv_out")
    gw_so = _grad_matmul(yb_in, dybb, "grad_w_sgu_out")
    return dict(
        grad_x=grad_x, loss_cols=sums_o[2:3], g_final=sums_o[0:1], d_gate=sums_o[1:2],
        d_shift=st_i[0:1], d_scale=st_i[1:2], g_pre=st_i[2:3],
        conv_ln_g=st_a[0:1], conv_ln_b=st_a[1:2], conv_b=st_a[2:3], conv_w=st_a[SUB:SUB + CONV_K],
        sgu_ln_g=st_b[0:1], sgu_ln_b=st_b[1:2], b_sgu=gbt[:, :HEADS].T, w_sgu=gws,
        hb=hb, dp=dp, w_o=gw_o, w_conv_out=gw_co, w_sgu_out=gw_so)


ANY_SPEC = pl.BlockSpec(memory_space=pl.ANY)
VMEM_SPEC = pl.BlockSpec(memory_space=pltpu.VMEM)


def _place():
    return lax.axis_index("x"), lax.axis_index("y"), lax.axis_index("c")


def _peer(k):
    x, y, c = _place()
    return (1 - x if k & 4 else x, 1 - y if k & 2 else y, 1 - c if k & 1 else c)


def _dev_of(p):
    return 4 * p[0] + 2 * p[1] + p[2]


def _chip_of(p):
    return 2 * p[0] + p[1]


def _rdma(src, dst, send_sem, recv_sem, to):
    return pltpu.make_async_remote_copy(src_ref=src, dst_ref=dst, send_sem=send_sem, recv_sem=recv_sem,
                                        device_id=to, device_id_type=MESH)


CHIP_PEERS = (2, 4, 6)
ALL_PEERS = tuple(range(1, N_DEV))
SIBLING = 1


def _setup_comm(c8, w_ada_s, b_ada_s, convw_s):
    n_mod = w_ada_s.shape[1]
    rows = SUB * N_DEV

    def body(c8_ref, wada_ref, bada_ref, cw_ref, call_ref, mod_ref, cwall_ref, csend, crecv, wsend, wrecv, msend, mrecv):
        me = _place()
        dev, chip = _dev_of(me), _chip_of(me)

        def c_rows(d):
            return call_ref.at[pl.ds(pl.multiple_of(d * SUB, SUB), SUB), :]

        call_ref[pl.ds(pl.multiple_of(dev * SUB, SUB), SUB), :] = c8_ref[...]
        cwall_ref[chip] = cw_ref[...]
        c_out = [_rdma(c8_ref, c_rows(dev), csend.at[k], crecv.at[k], _peer(k)) for k in ALL_PEERS]
        w_out = [_rdma(cw_ref, cwall_ref.at[chip], wsend.at[k], wrecv.at[k], _peer(k)) for k in CHIP_PEERS]
        for cp in c_out + w_out:
            cp.start()
        for k in ALL_PEERS:
            _rdma(c8_ref, c_rows(_dev_of(_peer(k))), csend.at[k], crecv.at[k], _peer(k)).wait_recv()
        part = jnp.dot(call_ref[...].astype(BF16), wada_ref[...].astype(BF16), preferred_element_type=F32) + bada_ref[...]
        mod_ref[chip] = part
        m_out = [_rdma(mod_ref.at[chip], mod_ref.at[chip], msend.at[k], mrecv.at[k], _peer(k)) for k in CHIP_PEERS]
        for cp in m_out:
            cp.start()
        for k in CHIP_PEERS:
            pc = _chip_of(_peer(k))
            _rdma(cw_ref, cwall_ref.at[pc], wsend.at[k], wrecv.at[k], _peer(k)).wait_recv()
            _rdma(mod_ref.at[pc], mod_ref.at[pc], msend.at[k], mrecv.at[k], _peer(k)).wait_recv()
        for cp in c_out + w_out + m_out:
            cp.wait_send()

    return pl.pallas_call(
        body, name="setup_comm",
        in_specs=[VMEM_SPEC] * 4, out_specs=[VMEM_SPEC] * 3,
        out_shape=[_sds((rows, D), F32), _sds((N_CHIP, rows, n_mod), F32), _sds((N_CHIP,) + convw_s.shape, F32)],
        scratch_shapes=[pltpu.SemaphoreType.DMA((N_DEV,))] * 6,
        compiler_params=_params(),
    )(c8, w_ada_s, b_ada_s, convw_s)


def _gather_weights(shards):
    n = len(shards)

    def body(*refs):
        ins, outs = refs[:n], refs[n:2 * n]
        lsem, isend, irecv, dsend, drecv = refs[2 * n:]
        me = _place()
        chip, c = _chip_of(me), me[2]
        local = [pltpu.make_async_copy(ins[t], outs[t].at[chip], lsem.at[t]) for t in range(n)]
        for cp in local:
            cp.start()

        def half(t, which):
            hr = shards[t].shape[0] // 2
            return pl.ds(pl.multiple_of(which * hr, hr), hr)

        sends = []
        for t in range(n):
            for j, k in enumerate(CHIP_PEERS):
                cp = _rdma(ins[t].at[half(t, c)], outs[t].at[chip, half(t, c)], isend.at[t, j], irecv.at[t, j], _peer(k))
                cp.start()
                sends.append(cp)
        for t in range(n):
            for j, k in enumerate(CHIP_PEERS):
                blk = outs[t].at[_chip_of(_peer(k)), half(t, c)]
                _rdma(blk, blk, isend.at[t, j], irecv.at[t, j], _peer(k)).wait_recv()
                cp = _rdma(blk, blk, dsend.at[t, j], drecv.at[t, j], _peer(SIBLING))
                cp.start()
                sends.append(cp)
        for t in range(n):
            for j, k in enumerate(CHIP_PEERS):
                blk = outs[t].at[_chip_of(_peer(k)), half(t, 1 - c)]
                _rdma(blk, blk, dsend.at[t, j], drecv.at[t, j], _peer(SIBLING)).wait_recv()
        for cp in sends:
            cp.wait_send()
        for cp in local:
            cp.wait()

    return pl.pallas_call(
        body, name="gather_weights",
        in_specs=[VMEM_SPEC] * n, out_specs=[VMEM_SPEC] * n,
        out_shape=[_sds((N_CHIP,) + s.shape, s.dtype) for s in shards],
        scratch_shapes=[pltpu.SemaphoreType.DMA((n,))] + [pltpu.SemaphoreType.DMA((n, len(CHIP_PEERS)))] * 4,
        compiler_params=_params(),
    )(*shards)


def _reduce_scatter(grads, name):
    n = len(grads)
    shapes = [g.shape[2:] for g in grads]

    def body(*refs):
        ins, outs = refs[:n], refs[n:2 * n]
        pbufs, rbufs, accs = refs[2 * n:3 * n], refs[3 * n:4 * n], refs[4 * n:5 * n]
        psend, precv, csend, crecv, fsend, frecv = refs[5 * n:]
        me = _place()
        chip, c = _chip_of(me), me[2]
        sib = _peer(SIBLING)

        def to_sibling(t, d):
            return _rdma(ins[t].at[d, 1 - c], pbufs[t].at[d], psend.at[t, d], precv.at[t, d], sib)

        sends = []
        for t in range(n):
            for d in range(N_CHIP):
                cp = to_sibling(t, d)
                cp.start()
                sends.append(cp)
        for j in (1, 2, 3, 0):
            d = jnp.bitwise_xor(chip, j)
            for t in range(n):
                to_sibling(t, d).wait_recv()

                def pair_sum(r0, t=t, d=d, j=j):
                    rows = pl.ds(r0, PACK)
                    s = ins[t][d, c, rows, :].astype(F32) + pbufs[t][d, rows, :].astype(F32)
                    if j == 0:
                        accs[t][rows, :] = s
                    else:
                        pbufs[t][d, rows, :] = s.astype(BF16)
                _strips(shapes[t][0], PACK, pair_sum)
                if j:
                    cp = _rdma(pbufs[t].at[d], rbufs[t].at[j - 1], csend.at[t, j], crecv.at[t, j], _peer(2 * j))
                    cp.start()
                    sends.append(cp)
        for t in range(n):
            for j in (1, 2, 3):
                blk = rbufs[t].at[j - 1]
                _rdma(blk, blk, csend.at[t, j], crecv.at[t, j], _peer(2 * j)).wait_recv()

            def total(r0, t=t):
                rows = pl.ds(r0, PACK)
                s = accs[t][rows, :] + rbufs[t][0, rows, :].astype(F32)
                s = s + rbufs[t][1, rows, :].astype(F32)
                outs[t][c, rows, :] = s + rbufs[t][2, rows, :].astype(F32)
            _strips(shapes[t][0], PACK, total)
            cp = _rdma(outs[t].at[c], outs[t].at[c], fsend.at[t], frecv.at[t], sib)
            cp.start()
            sends.append(cp)
        for t in range(n):
            blk = outs[t].at[1 - c]
            _rdma(blk, blk, fsend.at[t], frecv.at[t], sib).wait_recv()
        for cp in sends:
            cp.wait_send()

    return pl.pallas_call(
        body, name=name,
        in_specs=[VMEM_SPEC] * n, out_specs=[VMEM_SPEC] * n,
        out_shape=[_sds((2,) + s, F32) for s in shapes],
        scratch_shapes=([pltpu.VMEM((N_CHIP,) + s, BF16) for s in shapes] + [pltpu.VMEM((N_CHIP - 1,) + s, BF16) for s in shapes]
                        + [pltpu.VMEM(s, F32) for s in shapes]
                        + [pltpu.SemaphoreType.DMA((n, N_CHIP))] * 4 + [pltpu.SemaphoreType.DMA((n,))] * 2),
        compiler_params=_params(),
    )(*grads)


def _reduce_phases(shapes, ins, outs, pbufs, rbufs, accs, sems):
    n = len(shapes)
    psend, precv, csend, crecv, fsend, frecv = sems
    me = _place()
    chip, c = _chip_of(me), me[2]
    sib = _peer(SIBLING)

    def to_sibling(t, d):
        return _rdma(ins[t].at[d, 1 - c], pbufs[t].at[d], psend.at[t, d], precv.at[t, d], sib)

    def to_chip(t, j):
        return _rdma(pbufs[t].at[jnp.bitwise_xor(chip, j)], rbufs[t].at[j - 1], csend.at[t, j], crecv.at[t, j], _peer(2 * j))

    def finished(t):
        return _rdma(outs[t].at[c], outs[t].at[c], fsend.at[t], frecv.at[t], sib)

    def phase_a():
        for t in range(n):
            for d in range(N_CHIP):
                to_sibling(t, d).start()

    def phase_b():
        for j in (1, 2, 3, 0):
            d = jnp.bitwise_xor(chip, j)
            for t in range(n):
                to_sibling(t, d).wait_recv()

                def pair_sum(r0, t=t, d=d, j=j):
                    rows = pl.ds(r0, PACK)
                    s = ins[t][d, c, rows, :].astype(F32) + pbufs[t][d, rows, :].astype(F32)
                    if j == 0:
                        accs[t][rows, :] = s
                    else:
                        pbufs[t][d, rows, :] = s.astype(BF16)
                _strips(shapes[t][0], PACK, pair_sum)
                if j:
                    to_chip(t, j).start()

    def phase_c():
        for t in range(n):
            for j in (1, 2, 3):
                blk = rbufs[t].at[j - 1]
                _rdma(blk, blk, csend.at[t, j], crecv.at[t, j], _peer(2 * j)).wait_recv()

            def total(r0, t=t):
                rows = pl.ds(r0, PACK)
                s = accs[t][rows, :] + rbufs[t][0, rows, :].astype(F32)
                s = s + rbufs[t][1, rows, :].astype(F32)
                outs[t][c, rows, :] = s + rbufs[t][2, rows, :].astype(F32)
            _strips(shapes[t][0], PACK, total)
            finished(t).start()

    def phase_d():
        for t in range(n):
            blk = outs[t].at[1 - c]
            _rdma(blk, blk, fsend.at[t], frecv.at[t], sib).wait_recv()
        for t in range(n):
            for d in range(N_CHIP):
                to_sibling(t, d).wait_send()
            for j in (1, 2, 3):
                to_chip(t, j).wait_send()
            finished(t).wait_send()

    return phase_a, phase_b, phase_c, phase_d


def _sum_small_phases(ins, outs, pbufs, buf4s, sems):
    n = len(ins)
    psend, precv, send, recv = sems
    chip = _chip_of(_place())

    def swap(t):
        return _rdma(ins[t], pbufs[t], psend.at[t], precv.at[t], _peer(SIBLING))

    def to_chip(t, k):
        return _rdma(buf4s[t].at[chip], buf4s[t].at[chip], send.at[t, k], recv.at[t, k], _peer(k))

    def phase_a():
        for t in range(n):
            swap(t).start()

    def phase_b():
        for t in range(n):
            swap(t).wait()
            buf4s[t][chip] = ins[t][...] + pbufs[t][...]
            for k in CHIP_PEERS:
                to_chip(t, k).start()

    def phase_c():
        for t in range(n):
            for k in CHIP_PEERS:
                blk = buf4s[t].at[_chip_of(_peer(k))]
                _rdma(blk, blk, send.at[t, k], recv.at[t, k], _peer(k)).wait_recv()
            outs[t][...] = (buf4s[t][0] + buf4s[t][1]) + (buf4s[t][2] + buf4s[t][3])

    def phase_d():
        for t in range(n):
            for k in CHIP_PEERS:
                to_chip(t, k).wait_send()

    return phase_a, phase_b, phase_c, phase_d


def _sum_small_scratch(blobs):
    n = len(blobs)
    return ([pltpu.VMEM(b.shape, F32) for b in blobs] + [pltpu.VMEM((N_CHIP,) + b.shape, F32) for b in blobs]
            + [pltpu.SemaphoreType.DMA((n,))] * 2 + [pltpu.SemaphoreType.DMA((n, N_DEV))] * 2)


def _reduce_scratch(shapes):
    n = len(shapes)
    return ([pltpu.VMEM((N_CHIP,) + s, BF16) for s in shapes] + [pltpu.VMEM((N_CHIP - 1,) + s, BF16) for s in shapes]
            + [pltpu.VMEM(s, F32) for s in shapes]
            + [pltpu.SemaphoreType.DMA((n, N_CHIP))] * 4 + [pltpu.SemaphoreType.DMA((n,))] * 2)


def _grad_matmul_reduce(a, b, name, grads, blobs):
    s_len, n_cols = b.shape
    cb = min(2 * D, n_cols)
    tn = 512
    per = cb // tn
    steps = n_cols // tn
    n, nb = len(grads), len(blobs)
    shapes = [g.shape[2:] for g in grads]
    n_red = len(_reduce_scratch(shapes))

    def body(a_ref, b_ref, *refs):
        ins, bins = refs[:n], refs[n:n + nb]
        ob_ref, outs, bouts = refs[n + nb], refs[n + nb + 1:2 * n + nb + 1], refs[2 * n + nb + 1:2 * (n + nb) + 1]
        scratch = refs[2 * (n + nb) + 1:]
        fulls, red, small = scratch[:n], scratch[n:n + n_red], scratch[n + n_red:]
        phases = _reduce_phases(shapes, ins, fulls, red[:n], red[n:2 * n], red[2 * n:3 * n], red[3 * n:])
        small_phases = _sum_small_phases(bins, bouts, small[:nb], small[nb:2 * nb], small[2 * nb:])
        j = pl.program_id(0)
        for step, phase in zip((0, 2, steps - 2, steps - 1), phases):
            pl.when(j == step)(phase)
        for step, phase in zip((1, 3, steps - 2, steps - 1), small_phases):
            pl.when(j == step)(phase)

        @pl.when(j == steps - 1)
        def _():
            for t in range(n):
                outs[t][...] = fulls[t][...]
        ob_ref[0] = lax.dot_general(a_ref[...], b_ref[...], (((0,), (0,)), ((), ())),
                                    preferred_element_type=F32).astype(BF16)

    res = pl.pallas_call(
        body, name=name, grid=(steps,),
        in_specs=[pl.BlockSpec((s_len, D), lambda j: (0, 0), pipeline_mode=pl.Buffered(1)),
                  pl.BlockSpec((s_len, tn), lambda j: (0, j))] + [VMEM_SPEC] * (n + nb),
        out_specs=[pl.BlockSpec((1, D, tn), lambda j: (j // per, 0, j % per))] + [VMEM_SPEC] * (n + nb),
        out_shape=([_sds((n_cols // cb, D, cb), BF16)] + [_sds((2,) + s, F32) for s in shapes]
                   + [_sds(bl.shape, F32) for bl in blobs]),
        scratch_shapes=[pltpu.VMEM((2,) + s, F32) for s in shapes] + _reduce_scratch(shapes) + _sum_small_scratch(blobs),
        compiler_params=_params(("arbitrary",)),
    )(a, b, *grads, *blobs)
    return res[0], res[1:1 + n], res[1 + n:]


def _scatter_grads(grads):
    n = len(grads)

    def body(*refs):
        ins, outs = refs[:n], refs[n:2 * n]
        lsem, send, recv = refs[2 * n:]
        me = _place()
        dev, chip, c = _dev_of(me), _chip_of(me), me[2]
        local = [pltpu.make_async_copy(ins[t].at[chip, c], outs[t].at[dev], lsem.at[t]) for t in range(n)]
        for cp in local:
            cp.start()
        sends = []
        for t in range(n):
            for k in ALL_PEERS:
                to = _peer(k)
                cp = _rdma(ins[t].at[_chip_of(to), to[2]], outs[t].at[dev], send.at[t, k], recv.at[t, k], to)
                cp.start()
                sends.append(cp)
        for t in range(n):
            for k in ALL_PEERS:
                blk = outs[t].at[_dev_of(_peer(k))]
                _rdma(blk, blk, send.at[t, k], recv.at[t, k], _peer(k)).wait_recv()
        for cp in sends:
            cp.wait_send()
        for cp in local:
            cp.wait()

    return pl.pallas_call(
        body, name="scatter_grads",
        in_specs=[ANY_SPEC] * n, out_specs=[ANY_SPEC] * n,
        out_shape=[_sds((N_DEV,) + g.shape[2:], g.dtype) for g in grads],
        scratch_shapes=[pltpu.SemaphoreType.DMA((n,))] + [pltpu.SemaphoreType.DMA((n, N_DEV))] * 2,
        compiler_params=_params(),
    )(*grads)


def _sum_devices(parts, name):
    _, r, cols = parts.shape
    tr = min(r, 128)

    def body(in_ref, o_ref):
        acc = in_ref[0].astype(F32)
        for d in range(1, N_DEV):
            acc = acc + in_ref[d].astype(F32)
        o_ref[...] = acc

    return pl.pallas_call(
        body, name=name, grid=(r // tr,),
        in_specs=[pl.BlockSpec((N_DEV, tr, cols), lambda i: (0, i, 0))],
        out_specs=pl.BlockSpec((tr, cols), lambda i: (i, 0)),
        out_shape=_sds((r, cols), F32),
        compiler_params=_params(("arbitrary",)),
    )(parts)


def _share_halves(reds):
    n = len(reds)

    def body(*refs):
        ins, outs = refs[:n], refs[n:2 * n]
        lsem, send, recv = refs[2 * n:]
        me = _place()
        c = me[2]
        local = [pltpu.make_async_copy(ins[t], outs[t].at[c], lsem.at[t]) for t in range(n)]
        sends = [_rdma(ins[t], outs[t].at[c], send.at[t], recv.at[t], _peer(SIBLING)) for t in range(n)]
        for cp in local + sends:
            cp.start()
        for t in range(n):
            _rdma(ins[t], outs[t].at[1 - c], send.at[t], recv.at[t], _peer(SIBLING)).wait_recv()
        for cp in sends:
            cp.wait_send()
        for cp in local:
            cp.wait()

    return pl.pallas_call(
        body, name="share_halves",
        in_specs=[VMEM_SPEC] * n, out_specs=[VMEM_SPEC] * n,
        out_shape=[_sds((2,) + r.shape, r.dtype) for r in reds],
        scratch_shapes=[pltpu.SemaphoreType.DMA((n,))] * 3,
        compiler_params=_params(),
    )(*reds)


def _sum_small(blobs):
    n = len(blobs)

    def body(*refs):
        ins, outs = refs[:n], refs[n:2 * n]
        pbufs, buf4s = refs[2 * n:3 * n], refs[3 * n:4 * n]
        psend, precv, send, recv = refs[4 * n:]
        me = _place()
        chip = _chip_of(me)
        pairs = [_rdma(ins[t], pbufs[t], psend.at[t], precv.at[t], _peer(SIBLING)) for t in range(n)]
        for cp in pairs:
            cp.start()
        out = []
        for t in range(n):
            pairs[t].wait()
            buf4s[t][chip] = ins[t][...] + pbufs[t][...]
            for k in CHIP_PEERS:
                cp = _rdma(buf4s[t].at[chip], buf4s[t].at[chip], send.at[t, k], recv.at[t, k], _peer(k))
                cp.start()
                out.append(cp)
        for t in range(n):
            for k in CHIP_PEERS:
                blk = buf4s[t].at[_chip_of(_peer(k))]
                _rdma(blk, blk, send.at[t, k], recv.at[t, k], _peer(k)).wait_recv()
            outs[t][...] = (buf4s[t][0] + buf4s[t][1]) + (buf4s[t][2] + buf4s[t][3])
        for cp in out:
            cp.wait_send()

    return pl.pallas_call(
        body, name="sum_small",
        in_specs=[VMEM_SPEC] * n, out_specs=[VMEM_SPEC] * n, out_shape=[_sds(b.shape, F32) for b in blobs],
        scratch_shapes=([pltpu.VMEM(b.shape, F32) for b in blobs] + [pltpu.VMEM((N_CHIP,) + b.shape, F32) for b in blobs]
                        + [pltpu.SemaphoreType.DMA((n,))] * 2 + [pltpu.SemaphoreType.DMA((n, N_DEV))] * 2),
        compiler_params=_params(),
    )(*blobs)


def _adamw_math(w, g, m, v):
    m = ADAM_B1 * m + (1.0 - ADAM_B1) * g
    v = ADAM_B2 * v + (1.0 - ADAM_B2) * (g * g)
    m_hat = m / (1.0 - ADAM_B1 ** ADAM_STEP)
    v_hat = v / (1.0 - ADAM_B2 ** ADAM_STEP)
    delta = -ADAM_LR * (m_hat / (jnp.sqrt(v_hat) + ADAM_EPS) + ADAM_WD * w)
    return delta, m, v


def _row_tile(r, cols):
    if r * cols * 4 <= 2 ** 20:
        return r
    return next(t for t in (512, 256, 128, 64, 32, 16, 8) if r % t == 0 and t * cols * 4 <= 2 ** 20)


def _adamw(w, g, m, v, name):
    r, cols = w.shape
    tr = _row_tile(r, cols)

    def body(w_ref, g_ref, m_ref, v_ref, d_ref, nm_ref, nv_ref):
        d_ref[...], nm_ref[...], nv_ref[...] = _adamw_math(w_ref[...], g_ref[...], m_ref[...], v_ref[...])

    spec = pl.BlockSpec((tr, cols), lambda i: (i, 0))
    return pl.pallas_call(
        body, name=name, grid=(r // tr,), in_specs=[spec] * 4, out_specs=[spec] * 3,
        out_shape=[_sds((r, cols), F32)] * 3, compiler_params=_params(("arbitrary",)),
    )(w, g, m, v)


def _adamw_ada(w, ct, dm, m, v):
    r, cols = w.shape
    tr = _row_tile(r, cols)

    def body(w_ref, ct_ref, dm_ref, m_ref, v_ref, g_ref, d_ref, nm_ref, nv_ref):
        g = jnp.dot(ct_ref[...], dm_ref[...], preferred_element_type=F32)
        g_ref[...] = g
        d_ref[...], nm_ref[...], nv_ref[...] = _adamw_math(w_ref[...], g, m_ref[...], v_ref[...])

    spec = pl.BlockSpec((tr, cols), lambda i: (i, 0))
    return pl.pallas_call(
        body, name="adamw_ada", grid=(r // tr,),
        in_specs=[spec, pl.BlockSpec((tr, LANE), lambda i: (i, 0)), pl.BlockSpec((LANE, cols), lambda i: (0, 0)), spec, spec],
        out_specs=[spec] * 4, out_shape=[_sds((r, cols), F32)] * 4, compiler_params=_params(("arbitrary",)),
    )(w, ct, dm, m, v)


BLOB_VEC, BLOB_BSGU, BLOB_CONV, BLOB_ADA, BLOB_DMOD, BLOB_LOSS, BLOB_ROWS = 0, 8, 16, 48, 56, 80, 88
N_VEC = 7


def _adamw_small(tot, g_w_sgu, g_conv, params):
    n = len(params)

    def body(*refs):
        tot_ref, gws_ref, gconv_ref = refs[:3]
        wmv = refs[3:3 + 3 * n]
        outs = refs[3 + 3 * n:]
        grads = [tot_ref[pl.ds(BLOB_VEC + i, 1), :] for i in range(N_VEC)]
        grads += [tot_ref[pl.ds(BLOB_BSGU, HEADS), pl.ds(0, CHUNK)], gconv_ref[...], gws_ref[...], tot_ref[pl.ds(BLOB_ADA, 3), :]]
        for i, g in enumerate(grads):
            w_ref, m_ref, v_ref = wmv[3 * i:3 * i + 3]
            d, nm, nv = _adamw_math(w_ref[...], g, m_ref[...], v_ref[...])
            outs[4 * i][...] = g
            outs[4 * i + 1][...] = d
            outs[4 * i + 2][...] = nm
            outs[4 * i + 3][...] = nv

    flat = [a for wmv in params for a in wmv]
    return pl.pallas_call(
        body, name="adamw_small",
        in_specs=[VMEM_SPEC] * (3 + len(flat)), out_specs=[VMEM_SPEC] * (4 * n),
        out_shape=[_sds(wmv[0].shape, F32) for wmv in params for _ in range(4)],
        compiler_params=_params(),
    )(tot, g_w_sgu, g_conv, *flat)


def _set_rows(buf, row, val):
    return lax.dynamic_update_slice(buf, val.astype(F32), (row, 0))


def kernel(x, c, w_ada, b_ada, g_pre, w_in, conv_w, conv_b, conv_ln_g, conv_ln_b, w_conv_out, sgu_ln_g, sgu_ln_b, w_sgu, b_sgu, w_sgu_out, w_o, g_final, loss_target, m_w_ada, m_b_ada, m_g_pre, m_w_in, m_conv_w, m_conv_b, m_conv_ln_g, m_conv_ln_b, m_w_conv_out, m_sgu_ln_g, m_sgu_ln_b, m_w_sgu, m_b_sgu, m_w_sgu_out, m_w_o, m_g_final, v_w_ada, v_b_ada, v_g_pre, v_w_in, v_conv_w, v_conv_b, v_conv_ln_g, v_conv_ln_b, v_w_conv_out, v_sgu_ln_g, v_sgu_ln_b, v_w_sgu, v_b_sgu, v_w_sgu_out, v_w_o, v_g_final):
    me = _place()
    dev, chip = _dev_of(me), _chip_of(me)
    n_ada = w_ada.shape[2]
    conv_cols = conv_w.shape[2]

    b_ada_s = lax.dynamic_slice(b_ada, (0, chip * n_ada), (1, n_ada))
    c_all, mod_all, cw_all = _setup_comm(
        jnp.broadcast_to(c, (SUB, D)), w_ada[0], b_ada_s, jnp.pad(conv_w[0], ((0, HALO - CONV_K), (0, 0))))
    mod = lax.dynamic_slice(mod_all, (0, dev * SUB, 0), (N_CHIP, 1, n_ada)).reshape(1, 3 * D)
    shift, scale, gate = mod[:, :D], mod[:, D:2 * D], mod[:, 2 * D:]
    conv_w_full = jnp.swapaxes(cw_all, 0, 1).reshape(HALO, D)[:CONV_K]

    wg_in, wg_co, wg_so, wg_o = _gather_weights(
        [w_in[0].astype(BF16), w_conv_out[0].astype(BF16), w_sgu_out[0].astype(BF16), w_o[0].astype(BF16)])

    loc = _local_step(x[0], loss_target[0], shift, scale, gate, g_pre, conv_w_full, conv_b, conv_ln_g, conv_ln_b,
                      sgu_ln_g, sgu_ln_b, w_sgu[0], b_sgu[0], g_final.reshape(1, D),
                      wg_in, wg_co.reshape(D, D), wg_so.reshape(D, D), wg_o.reshape(D, D))

    d_mod = jnp.concatenate([loc["d_shift"], loc["d_scale"], loc["d_gate"]], axis=0)
    blob = jnp.zeros((BLOB_ROWS, D), F32)
    for i, name in enumerate(["g_pre", "conv_b", "conv_ln_g", "conv_ln_b", "sgu_ln_g", "sgu_ln_b", "g_final"]):
        blob = _set_rows(blob, BLOB_VEC + i, loc[name])
    blob = _set_rows(blob, BLOB_BSGU, loc["b_sgu"])
    blob = _set_rows(blob, BLOB_CONV, loc["conv_w"])
    blob = _set_rows(blob, BLOB_ADA, d_mod)
    blob = lax.dynamic_update_slice(blob, d_mod, (BLOB_DMOD + 3 * dev, 0))
    blob = _set_rows(blob, BLOB_LOSS, loc["loss_cols"])

    big = ["w_in", "w_conv_out", "w_sgu_out", "w_o"]
    contrib_out = [loc[name].reshape(N_CHIP, 2, D // (2 * N_CHIP), D) for name in big[1:]]
    gw_in, full_out, (tot, g_w_sgu) = _grad_matmul_reduce(
        loc["hb"], loc["dp"], "grad_w_in", contrib_out, [blob, loc["w_sgu"].reshape(HEADS * CHUNK, CHUNK)])
    full_in = _reduce_scatter([gw_in.reshape(N_CHIP, 2, D // 2, gw_in.shape[2])], "reduce_w_in")
    g_big = {name: f.reshape(2 * f.shape[1], f.shape[2]) for name, f in zip(big, list(full_in) + list(full_out))}

    loss = jnp.sum(tot[BLOB_LOSS])
    g_conv_s = lax.dynamic_slice(tot, (BLOB_CONV, chip * conv_cols), (CONV_K, conv_cols))
    d_mod_all = tot[BLOB_DMOD:BLOB_DMOD + 3 * N_DEV].reshape(N_DEV, 3 * D)

    ct = jnp.pad(c_all[::SUB].T, ((0, 0), (0, LANE - N_DEV))).astype(BF16)
    dm = jnp.pad(lax.dynamic_slice(d_mod_all, (0, chip * n_ada), (N_DEV, n_ada)), ((0, LANE - N_DEV), (0, 0))).astype(BF16)
    g_ada, d_ada, nm_ada, nv_ada = _adamw_ada(w_ada[0], ct, dm, m_w_ada[0], v_w_ada[0])

    upd = {}
    for name, w, m, v in [("w_in", w_in, m_w_in, v_w_in), ("w_conv_out", w_conv_out, m_w_conv_out, v_w_conv_out),
                          ("w_sgu_out", w_sgu_out, m_w_sgu_out, v_w_sgu_out), ("w_o", w_o, m_w_o, v_w_o)]:
        upd[name] = _adamw(w[0], g_big[name], m[0], v[0], "adamw_" + name)

    def wmv(w, m, v, shape):
        return tuple(a.reshape(shape) for a in (w, m, v))

    small_params = [wmv(w, m, v, (1, D)) for w, m, v in [
        (g_pre, m_g_pre, v_g_pre), (conv_b, m_conv_b, v_conv_b), (conv_ln_g, m_conv_ln_g, v_conv_ln_g),
        (conv_ln_b, m_conv_ln_b, v_conv_ln_b), (sgu_ln_g, m_sgu_ln_g, v_sgu_ln_g), (sgu_ln_b, m_sgu_ln_b, v_sgu_ln_b),
        (g_final, m_g_final, v_g_final)]]
    small_params += [wmv(b_sgu, m_b_sgu, v_b_sgu, (HEADS, CHUNK)), wmv(conv_w, m_conv_w, v_conv_w, (CONV_K, conv_cols)),
                     wmv(w_sgu, m_w_sgu, v_w_sgu, (HEADS * CHUNK, CHUNK)), wmv(b_ada, m_b_ada, v_b_ada, (3, D))]
    small_out = _adamw_small(tot, g_w_sgu, g_conv_s, small_params)

    def leaves(kind):
        vecs = [small_out[4 * i + kind] for i in range(N_VEC)]
        o_b_sgu, o_conv, o_w_sgu, o_b_ada = (small_out[4 * (N_VEC + i) + kind] for i in range(4))
        ada = (g_ada, d_ada, nm_ada, nv_ada)[kind]
        def bigk(name):
            return (g_big[name] if kind == 0 else upd[name][kind - 1])[None]
        return [ada[None], o_b_ada.reshape(1, 3 * D), vecs[0], bigk("w_in"), o_conv[None], vecs[1], vecs[2], vecs[3],
                bigk("w_conv_out"), vecs[4], vecs[5], o_w_sgu.reshape(1, HEADS, CHUNK, CHUNK), o_b_sgu[None],
                bigk("w_sgu_out"), bigk("w_o"), vecs[6].reshape(D)]

    return (loss, loc["grad_x"][None], *leaves(0), *leaves(1), *leaves(2), *leaves(3))
```

```python
import functools

import jax
import jax.numpy as jnp
from jax import lax
from jax.experimental import pallas as pl
from jax.experimental.pallas import tpu as pltpu

F32 = jnp.float32
BF16 = jnp.bfloat16
MESH = pl.DeviceIdType.MESH

D = 1024
N_SEC = 8
N_CHIP = 4
N_DEV = 8
EPS = 1e-6
CONV_K = 31
HALO = 32
CHUNK = 128
HEADS = 8
LANE = 128
SUB = 8
PACK = 16
VMEM_LIMIT = 56 * 1024 * 1024

ADAM_LR, ADAM_B1, ADAM_B2, ADAM_EPS, ADAM_WD, ADAM_STEP = 0.001, 0.9, 0.999, 1e-08, 0.01, 10

_SQRT_HALF = 0.7071067811865476
_INV_SQRT_2PI = 0.3989422804014327


def _sds(shape, dtype):
    return jax.ShapeDtypeStruct(shape, dtype)


def _params(sem=None):
    if sem is None:
        return pltpu.CompilerParams(vmem_limit_bytes=VMEM_LIMIT)
    return pltpu.CompilerParams(dimension_semantics=sem, vmem_limit_bytes=VMEM_LIMIT)


def _strips(n_rows, rows, fn):
    def step(s, carry):
        fn(pl.multiple_of(s * rows, rows))
        return carry
    lax.fori_loop(0, n_rows // rows, step, 0)


def _sigmoid(v):
    return 1.0 / (1.0 + jnp.exp(-v))


def _gelu(v):
    return 0.5 * v * (1.0 + lax.erf(v * _SQRT_HALF))


def _gelu_and_grad(v):
    cdf = 0.5 * (1.0 + lax.erf(v * _SQRT_HALF))
    return v * cdf, cdf + v * jnp.exp(-0.5 * v * v) * _INV_SQRT_2PI


def _dsilu(v, sg):
    return sg * (1.0 + v * (1.0 - sg))


def _rowmean(v):
    return jnp.mean(v, axis=-1, keepdims=True)


def _vec_spec(grid_rank):
    zeros = (0, 0)
    if grid_rank == 1:
        return pl.BlockSpec((1, D), lambda i: zeros)
    return pl.BlockSpec((1, D), lambda i, j: zeros)


def _in_proj(x, shift, scale, g_pre, wg_in):
    s_len = x.shape[0]
    tm = min(512, s_len)
    n_i = s_len // tm
    wn = wg_in.shape[2]

    def body(x_ref, sh_ref, sc_ref, g_ref, w_ref, p_ref, hb_ref):
        def strip(r0):
            xs = x_ref[pl.ds(r0, PACK), :]
            r = lax.rsqrt(_rowmean(xs * xs) + EPS)
            h = (xs * r) * g_ref[...] * (1.0 + sc_ref[...]) + sh_ref[...]
            hb_ref[pl.ds(r0, PACK), :] = h.astype(BF16)
        _strips(tm, PACK, strip)
        hb = hb_ref[...]
        for j in range(N_CHIP):
            p_ref[:, pl.ds(j * wn, wn)] = jnp.dot(hb, w_ref[j], preferred_element_type=F32).astype(BF16)

    return pl.pallas_call(
        body, name="in_proj", grid=(n_i,),
        in_specs=[pl.BlockSpec((tm, D), lambda i: (i, 0)), _vec_spec(1), _vec_spec(1), _vec_spec(1),
                  pl.BlockSpec((N_CHIP, D, wn), lambda i: (0, 0, 0), pipeline_mode=pl.Buffered(1))],
        out_specs=[pl.BlockSpec((tm, N_CHIP * wn), lambda i: (i, 0)), pl.BlockSpec((tm, D), lambda i: (i, 0))],
        out_shape=[_sds((s_len, N_SEC * D), BF16), _sds((s_len, D), BF16)],
        compiler_params=_params(("arbitrary",)),
    )(x, shift, scale, g_pre, wg_in)


def _conv_taps(win_ref, r0, lt, weight_of_offset, rows):
    lanes = pl.ds(lt * LANE, LANE)
    win = win_ref[pl.ds(r0, rows + HALO), lanes]
    n_out = rows // SUB
    acc = [jnp.zeros((SUB, LANE), F32) for _ in range(n_out)]
    for phase in range(SUB):
        offs = [o for o in weight_of_offset if o % SUB == phase]
        if not offs:
            continue
        q_max = max(o // SUB for o in offs)
        span = (n_out + q_max) * SUB
        sh = win[phase:phase + span, :]
        for o in offs:
            q = o // SUB
            w = weight_of_offset[o](lanes)
            for m in range(n_out):
                acc[m] = acc[m] + w * sh[(m + q) * SUB:(m + q + 1) * SUB, :]
    return acc


def _branch_a_fwd(p, conv_wb, conv_b, ln_g, ln_b):
    s_len = p.shape[0]
    tm = min(256, s_len)
    n_i = s_len // tm
    rows = 32

    def body(p_ref, wb_ref, cb_ref, g_ref, b_ref, ya_ref, y1_ref, abuf):
        @pl.when(pl.program_id(0) == 0)
        def _():
            abuf[pl.ds(0, HALO), :] = jnp.zeros((HALO, D), F32)

        def glu(r0):
            val = p_ref[pl.ds(r0, PACK), pl.ds(0, D)].astype(F32)
            gl = p_ref[pl.ds(r0, PACK), pl.ds(D, D)].astype(F32)
            abuf[pl.ds(HALO + r0, PACK), :] = val * _sigmoid(gl)
        _strips(tm, PACK,glu)

        taps = {HALO - (CONV_K - 1) + k: (lambda lanes, k=k: wb_ref[pl.ds(k * SUB, SUB), lanes]) for k in range(CONV_K)}

        def conv(r0):
            for lt in range(D // LANE):
                acc = _conv_taps(abuf, r0, lt, taps, rows)
                cb = cb_ref[:, pl.ds(lt * LANE, LANE)]
                for m, v in enumerate(acc):
                    y1_ref[pl.ds(r0 + m * SUB, SUB), pl.ds(lt * LANE, LANE)] = v + cb
        _strips(tm, rows, conv)

        def norm(r0):
            y1 = y1_ref[pl.ds(r0, PACK), :]
            mu = _rowmean(y1)
            yc = y1 - mu
            rstd = lax.rsqrt(_rowmean(yc * yc) + EPS)
            l1 = (yc * rstd) * g_ref[...] + b_ref[...]
            z = p_ref[pl.ds(r0, PACK), pl.ds(2 * D, D)].astype(F32)
            ya_ref[pl.ds(r0, PACK), :] = ((l1 * _sigmoid(l1)) * (z * _sigmoid(z))).astype(BF16)
        _strips(tm, PACK,norm)

        abuf[pl.ds(0, HALO), :] = abuf[pl.ds(tm, HALO), :]

    return pl.pallas_call(
        body, name="branch_a_fwd", grid=(n_i,),
        in_specs=[pl.BlockSpec((tm, 3 * D), lambda i: (i, 0)),
                  pl.BlockSpec((CONV_K * SUB, D), lambda i: (0, 0)), _vec_spec(1), _vec_spec(1), _vec_spec(1)],
        out_specs=[pl.BlockSpec((tm, D), lambda i: (i, 0)), pl.BlockSpec((tm, D), lambda i: (i, 0))],
        out_shape=[_sds((s_len, D), BF16), _sds((s_len, D), F32)],
        scratch_shapes=[pltpu.VMEM((tm + HALO, D), F32)],
        compiler_params=_params(("arbitrary",)),
    )(p, conv_wb, conv_b, ln_g, ln_b)


def _branch_b_fwd(p, wt, bias_full, ln_g, ln_b):
    s_len = p.shape[0]
    tm = min(256, s_len)
    n_i = s_len // tm

    def body(p_ref, wt_ref, bias_ref, g_ref, b_ref, yb_ref, vb, sbuf):
        def norm(r0):
            gv = _gelu(p_ref[pl.ds(r0, PACK), pl.ds(D, D)].astype(F32))
            mu = _rowmean(gv)
            vc = gv - mu
            rstd = lax.rsqrt(_rowmean(vc * vc) + EPS)
            vb[pl.ds(r0, PACK), :] = ((vc * rstd) * g_ref[...] + b_ref[...]).astype(BF16)
        _strips(tm, PACK,norm)

        for ck in range(tm // CHUNK):
            for h in range(HEADS):
                blk = (pl.ds(ck * CHUNK, CHUNK), pl.ds(h * LANE, LANE))
                sbuf[blk] = jnp.dot(wt_ref[h], vb[blk], preferred_element_type=F32) + bias_ref[:, pl.ds(h * LANE, LANE)]

        def gate(r0):
            u = _gelu(p_ref[pl.ds(r0, PACK), pl.ds(0, D)].astype(F32))
            z = p_ref[pl.ds(r0, PACK), pl.ds(2 * D, D)].astype(F32)
            yb_ref[pl.ds(r0, PACK), :] = (u * sbuf[pl.ds(r0, PACK), :] * (z * _sigmoid(z))).astype(BF16)
        _strips(tm, PACK,gate)

    return pl.pallas_call(
        body, name="branch_b_fwd", grid=(n_i,),
        in_specs=[pl.BlockSpec((tm, 3 * D), lambda i: (i, 1)),
                  pl.BlockSpec((HEADS, CHUNK, CHUNK), lambda i: (0, 0, 0)),
                  pl.BlockSpec((CHUNK, D), lambda i: (0, 0)), _vec_spec(1), _vec_spec(1)],
        out_specs=pl.BlockSpec((tm, D), lambda i: (i, 0)),
        out_shape=_sds((s_len, D), BF16),
        scratch_shapes=[pltpu.VMEM((tm, D), BF16), pltpu.VMEM((tm, D), F32)],
        compiler_params=_params(("arbitrary",)),
    )(p, wt, bias_full, ln_g, ln_b)


def _dot_t(a, b):
    return lax.dot_general(a, b, (((1,), (1,)), ((), ())), preferred_element_type=F32)


def _out_proj(p, ya_in, yb_in, x, target, gate, g_final, w_co, w_so, w_o):
    s_len = x.shape[0]
    tm = min(256, s_len)
    n_i = s_len // tm

    def body(pg_ref, ya_ref, yb_ref, x_ref, t_ref, gate_ref, gf_ref, wco_ref, wso_ref, wo_ref,
             dx2_ref, dya_ref, dyb_ref, dp_ref, mb_ref, dob_ref, dyab_ref, dybb_ref, sums_ref):
        @pl.when(pl.program_id(0) == 0)
        def _():
            sums_ref[...] = jnp.zeros((SUB, D), F32)

        y_a = jnp.dot(ya_ref[...], wco_ref[...], preferred_element_type=F32)
        y_b = jnp.dot(yb_ref[...], wso_ref[...], preferred_element_type=F32)
        ga = _sigmoid(pg_ref[:, pl.ds(0, D)].astype(F32))
        gb = _sigmoid(pg_ref[:, pl.ds(D, D)].astype(F32))
        mb = (ga * y_a + gb * y_b).astype(BF16)
        mb_ref[...] = mb
        o = jnp.dot(mb, wo_ref[...], preferred_element_type=F32)
        x2 = x_ref[...] + gate_ref[...] * o
        r2 = lax.rsqrt(_rowmean(x2 * x2) + EPS)
        xh = x2 * r2
        e = xh * gf_ref[...] - t_ref[...]
        dy = e * (1.0 / D)
        dxh = dy * gf_ref[...]
        dx2 = r2 * (dxh - xh * _rowmean(dxh * xh))
        dx2_ref[...] = dx2
        sums_ref[pl.ds(0, 1), :] += jnp.sum(dy * xh, axis=0, keepdims=True)
        sums_ref[pl.ds(1, 1), :] += jnp.sum(dx2 * o, axis=0, keepdims=True)
        sums_ref[pl.ds(2, 1), :] += jnp.sum(e * e, axis=0, keepdims=True) * (0.5 / D)
        dob = (gate_ref[...] * dx2).astype(BF16)
        dob_ref[...] = dob
        dm = _dot_t(dob, wo_ref[...])
        dy_a = (ga * dm).astype(BF16)
        dy_b = (gb * dm).astype(BF16)
        dyab_ref[...] = dy_a
        dybb_ref[...] = dy_b
        dp_ref[:, pl.ds(0, D)] = (dm * y_a * ga * (1.0 - ga)).astype(BF16)
        dp_ref[:, pl.ds(D, D)] = (dm * y_b * gb * (1.0 - gb)).astype(BF16)
        dya_ref[...] = _dot_t(dy_a, wco_ref[...])
        dyb_ref[...] = _dot_t(dy_b, wso_ref[...])

    tile = pl.BlockSpec((tm, D), lambda i: (i, 0))
    wspec = pl.BlockSpec((D, D), lambda i: (0, 0))
    return pl.pallas_call(
        body, name="out_proj", grid=(n_i,),
        in_specs=[pl.BlockSpec((tm, 2 * D), lambda i: (i, 3)), tile, tile, tile, tile, _vec_spec(1), _vec_spec(1),
                  wspec, wspec, wspec],
        out_specs=[tile, tile, tile, pl.BlockSpec((tm, 2 * D), lambda i: (i, 3)), tile, tile, tile, tile,
                   pl.BlockSpec((SUB, D), lambda i: (0, 0))],
        out_shape=[_sds((s_len, D), F32), _sds((s_len, D), F32), _sds((s_len, D), F32), _sds((s_len, N_SEC * D), BF16),
                   _sds((s_len, D), BF16), _sds((s_len, D), BF16), _sds((s_len, D), BF16), _sds((s_len, D), BF16),
                   _sds((SUB, D), F32)],
        compiler_params=_params(("arbitrary",)),
    )(p, ya_in, yb_in, x, target, gate, g_final, w_co, w_so, w_o)


A_STATS_ROWS = 8 + HALO


def _branch_a_bwd(p, y1, dya_in, dp, conv_wb, ln_g, ln_b):
    s_len = p.shape[0]
    tm = min(256, s_len)
    n_i = s_len // tm
    rows = 32
    n_out = rows // SUB

    def tile_of(i):
        return n_i - 1 - i

    def body(p_ref, y1_ref, dya_ref, dp_in, wb_ref, g_ref, b_ref, dp_ref, st_ref, dybuf, acc8, tapacc):
        del dp_in
        i = pl.program_id(0)

        @pl.when(i == 0)
        def _():
            dybuf[pl.ds(tm, HALO), :] = jnp.zeros((HALO, D), F32)
            st_ref[...] = jnp.zeros((A_STATS_ROWS, D), F32)
            acc8[...] = jnp.zeros((3 * PACK, D), F32)
            tapacc[...] = jnp.zeros((CONV_K * SUB, D), F32)

        def norm_bwd(r0):
            y1 = y1_ref[pl.ds(r0, PACK), :]
            mu = _rowmean(y1)
            yc = y1 - mu
            rstd = lax.rsqrt(_rowmean(yc * yc) + EPS)
            n1 = yc * rstd
            l1 = n1 * g_ref[...] + b_ref[...]
            sg = _sigmoid(l1)
            z = p_ref[pl.ds(r0, PACK), pl.ds(2 * D, D)].astype(F32)
            sz = _sigmoid(z)
            dya = dya_ref[pl.ds(r0, PACK), :]
            dp_ref[pl.ds(r0, PACK), pl.ds(2 * D, D)] = (dya * (l1 * sg) * _dsilu(z, sz)).astype(BF16)
            dl1 = dya * (z * sz) * _dsilu(l1, sg)
            acc8[pl.ds(0, PACK), :] += dl1 * n1
            acc8[pl.ds(PACK, PACK), :] += dl1
            dn1 = dl1 * g_ref[...]
            dy1 = rstd * (dn1 - _rowmean(dn1) - n1 * _rowmean(dn1 * n1))
            acc8[pl.ds(2 * PACK, PACK), :] += dy1
            dybuf[pl.ds(r0, PACK), :] = dy1
        _strips(tm, PACK,norm_bwd)

        def conv_bwd(r0):
            for lt in range(D // LANE):
                lanes = pl.ds(lt * LANE, LANE)
                glanes = pl.ds(D + lt * LANE, LANE)
                win = dybuf[pl.ds(r0, rows + HALO), lanes]
                sg16, a16 = [], []
                for h in range(rows // PACK):
                    rr = pl.ds(r0 + h * PACK, PACK)
                    s = _sigmoid(p_ref[rr, glanes].astype(F32))
                    sg16.append(s)
                    a16.append(p_ref[rr, lanes].astype(F32) * s)
                a = [a16[m // 2][(m % 2) * SUB:(m % 2 + 1) * SUB, :] for m in range(n_out)]
                da = [jnp.zeros((SUB, LANE), F32) for _ in range(n_out)]
                for phase in range(SUB):
                    offs = [o for o in range(CONV_K) if o % SUB == phase]
                    q_max = max(o // SUB for o in offs)
                    sh = win[phase:phase + (n_out + q_max) * SUB, :]
                    for o in offs:
                        k, q = CONV_K - 1 - o, o // SUB
                        w = wb_ref[pl.ds(k * SUB, SUB), lanes]
                        part = None
                        for m in range(n_out):
                            s = sh[(m + q) * SUB:(m + q + 1) * SUB, :]
                            da[m] = da[m] + w * s
                            part = a[m] * s if part is None else part + a[m] * s
                        tapacc[pl.ds(k * SUB, SUB), lanes] += part
                for h in range(rows // PACK):
                    rr = pl.ds(r0 + h * PACK, PACK)
                    da16 = jnp.concatenate(da[2 * h:2 * h + 2], axis=0)
                    dp_ref[rr, lanes] = (da16 * sg16[h]).astype(BF16)
                    dp_ref[rr, glanes] = (da16 * a16[h] * (1.0 - sg16[h])).astype(BF16)
        _strips(tm, rows, conv_bwd)

        dybuf[pl.ds(tm, HALO), :] = dybuf[pl.ds(0, HALO), :]

        @pl.when(i == n_i - 1)
        def _():
            for j in range(3):
                st_ref[pl.ds(j, 1), :] = jnp.sum(acc8[pl.ds(j * PACK, PACK), :], axis=0, keepdims=True)
            for k in range(CONV_K):
                st_ref[pl.ds(SUB + k, 1), :] = jnp.sum(tapacc[pl.ds(k * SUB, SUB), :], axis=0, keepdims=True)

    return pl.pallas_call(
        body, name="branch_a_bwd", grid=(n_i,),
        in_specs=[pl.BlockSpec((tm, 3 * D), lambda i: (tile_of(i), 0)),
                  pl.BlockSpec((tm, D), lambda i: (tile_of(i), 0)),
                  pl.BlockSpec((tm, D), lambda i: (tile_of(i), 0)),
                  pl.BlockSpec(memory_space=pl.ANY),
                  pl.BlockSpec((CONV_K * SUB, D), lambda i: (0, 0)), _vec_spec(1), _vec_spec(1)],
        out_specs=[pl.BlockSpec((tm, 3 * D), lambda i: (tile_of(i), 0)),
                   pl.BlockSpec((A_STATS_ROWS, D), lambda i: (0, 0))],
        out_shape=[_sds(dp.shape, BF16), _sds((A_STATS_ROWS, D), F32)],
        scratch_shapes=[pltpu.VMEM((tm + HALO, D), F32), pltpu.VMEM((3 * PACK, D), F32), pltpu.VMEM((CONV_K * SUB, D), F32)],
        input_output_aliases={3: 0},
        compiler_params=_params(("arbitrary",)),
    )(p, y1, dya_in, dp, conv_wb, ln_g, ln_b)


def _branch_b_bwd(p, dyb_in, dp, wt, wtt, bias_full, ln_g, ln_b):
    s_len = p.shape[0]
    tm = min(256, s_len)
    n_i = s_len // tm

    def body(p_ref, dyb_ref, dp_in, wt_ref, wtt_ref, bias_ref, g_ref, b_ref, dp_ref, st_ref, gbt_ref, gw_ref,
             vb, n2buf, rstdbuf, sbuf, dsb, dvbuf, acc8, gb_ref, dgbuf):
        del dp_in
        i = pl.program_id(0)

        @pl.when(i == 0)
        def _():
            st_ref[...] = jnp.zeros((SUB, D), F32)
            gbt_ref[...] = jnp.zeros((CHUNK, LANE), F32)
            gb_ref[...] = jnp.zeros((CHUNK, D), F32)
            gw_ref[...] = jnp.zeros((HEADS, CHUNK, CHUNK), F32)
            acc8[...] = jnp.zeros((2 * PACK, D), F32)

        def norm(r0):
            gv, dgv = _gelu_and_grad(p_ref[pl.ds(r0, PACK), pl.ds(D, D)].astype(F32))
            dgbuf[pl.ds(r0, PACK), :] = dgv
            mu = _rowmean(gv)
            vc = gv - mu
            rstd = lax.rsqrt(_rowmean(vc * vc) + EPS)
            n2 = vc * rstd
            n2buf[pl.ds(r0, PACK), :] = n2
            rstdbuf[pl.ds(r0, PACK), :] = jnp.broadcast_to(rstd, (PACK, LANE))
            vb[pl.ds(r0, PACK), :] = (n2 * g_ref[...] + b_ref[...]).astype(BF16)
        _strips(tm, PACK,norm)

        for ck in range(tm // CHUNK):
            for h in range(HEADS):
                blk = (pl.ds(ck * CHUNK, CHUNK), pl.ds(h * LANE, LANE))
                sbuf[blk] = jnp.dot(wt_ref[h], vb[blk], preferred_element_type=F32) + bias_ref[:, pl.ds(h * LANE, LANE)]

        def gate_bwd(r0):
            pu = p_ref[pl.ds(r0, PACK), pl.ds(0, D)].astype(F32)
            u, du = _gelu_and_grad(pu)
            z = p_ref[pl.ds(r0, PACK), pl.ds(2 * D, D)].astype(F32)
            sg = _sigmoid(z)
            sz = z * sg
            s = sbuf[pl.ds(r0, PACK), :]
            dyb = dyb_ref[pl.ds(r0, PACK), :]
            ds = dyb * u * sz
            dsb[pl.ds(r0, PACK), :] = ds.astype(BF16)
            gb_ref[pl.ds(pl.multiple_of(r0 % CHUNK, PACK), PACK), :] += ds
            dp_ref[pl.ds(r0, PACK), pl.ds(0, D)] = (dyb * s * sz * du).astype(BF16)
            dp_ref[pl.ds(r0, PACK), pl.ds(2 * D, D)] = (dyb * u * s * _dsilu(z, sg)).astype(BF16)
        _strips(tm, PACK,gate_bwd)

        for ck in range(tm // CHUNK):
            for h in range(HEADS):
                blk = (pl.ds(ck * CHUNK, CHUNK), pl.ds(h * LANE, LANE))
                d_s = dsb[blk]
                dvbuf[blk] = jnp.dot(wtt_ref[h], d_s, preferred_element_type=F32)
                gw_ref[h] += _dot_t(d_s, vb[blk])

        def norm_bwd(r0):
            dv = dvbuf[pl.ds(r0, PACK), :]
            n2 = n2buf[pl.ds(r0, PACK), :]
            rstd = rstdbuf[pl.ds(r0, PACK), pl.ds(0, 1)]
            acc8[pl.ds(0, PACK), :] += dv * n2
            acc8[pl.ds(PACK, PACK), :] += dv
            dn2 = dv * g_ref[...]
            dgv = rstd * (dn2 - _rowmean(dn2) - n2 * _rowmean(dn2 * n2))
            dp_ref[pl.ds(r0, PACK), pl.ds(D, D)] = (dgv * dgbuf[pl.ds(r0, PACK), :]).astype(BF16)
        _strips(tm, PACK,norm_bwd)

        @pl.when(i == n_i - 1)
        def _():
            for j in range(2):
                st_ref[pl.ds(j, 1), :] = jnp.sum(acc8[pl.ds(j * PACK, PACK), :], axis=0, keepdims=True)
            row = lax.broadcasted_iota(jnp.int32, (CHUNK, CHUNK), 0)
            col = lax.broadcasted_iota(jnp.int32, (CHUNK, CHUNK), 1)
            for h in range(HEADS):
                gw_ref[h] = jnp.where(row >= col, gw_ref[h], 0.0)
            lane = lax.broadcasted_iota(jnp.int32, (CHUNK, LANE), 1)
            gbt = jnp.zeros((CHUNK, LANE), F32)
            for h in range(HEADS):
                gbt = jnp.where(lane == h, jnp.sum(gb_ref[:, pl.ds(h * LANE, LANE)], axis=1, keepdims=True), gbt)
            gbt_ref[...] = gbt

    wspec = pl.BlockSpec((HEADS, CHUNK, CHUNK), lambda i: (0, 0, 0))
    return pl.pallas_call(
        body, name="branch_b_bwd", grid=(n_i,),
        in_specs=[pl.BlockSpec((tm, 3 * D), lambda i: (i, 1)), pl.BlockSpec((tm, D), lambda i: (i, 0)),
                  pl.BlockSpec(memory_space=pl.ANY), wspec, wspec,
                  pl.BlockSpec((CHUNK, D), lambda i: (0, 0)), _vec_spec(1), _vec_spec(1)],
        out_specs=[pl.BlockSpec((tm, 3 * D), lambda i: (i, 1)), pl.BlockSpec((SUB, D), lambda i: (0, 0)),
                   pl.BlockSpec((CHUNK, LANE), lambda i: (0, 0)), wspec],
        out_shape=[_sds(dp.shape, BF16), _sds((SUB, D), F32), _sds((CHUNK, LANE), F32), _sds((HEADS, CHUNK, CHUNK), F32)],
        scratch_shapes=[pltpu.VMEM((tm, D), BF16), pltpu.VMEM((tm, D), F32), pltpu.VMEM((tm, LANE), F32),
                        pltpu.VMEM((tm, D), F32), pltpu.VMEM((tm, D), BF16), pltpu.VMEM((tm, D), F32),
                        pltpu.VMEM((2 * PACK, D), F32), pltpu.VMEM((CHUNK, D), F32), pltpu.VMEM((tm, D), F32)],
        input_output_aliases={2: 0},
        compiler_params=_params(("arbitrary",)),
    )(p, dyb_in, dp, wt, wtt, bias_full, ln_g, ln_b)


def _in_proj_bwd(dp, wg_in, x, dx2, shift, scale, g_pre):
    del shift
    s_len = x.shape[0]
    tm = min(512, s_len)
    n_i = s_len // tm
    wn = wg_in.shape[2]

    def body(dp_ref, w_ref, x_ref, dx2_ref, sc_ref, g_ref, gx_ref, st_ref, acc, acc8):
        i = pl.program_id(0)

        @pl.when(i == 0)
        def _():
            st_ref[...] = jnp.zeros((SUB, D), F32)
            acc8[...] = jnp.zeros((3 * PACK, D), F32)

        dh = _dot_t(dp_ref[:, pl.ds(0, wn)], w_ref[0])
        for j in range(1, N_CHIP):
            dh = dh + _dot_t(dp_ref[:, pl.ds(j * wn, wn)], w_ref[j])
        acc[...] = dh

        def strip(r0):
            xs = x_ref[pl.ds(r0, PACK), :]
            r = lax.rsqrt(_rowmean(xs * xs) + EPS)
            xn = xs * r
            dhs = acc[pl.ds(r0, PACK), :]
            acc8[pl.ds(0, PACK), :] += dhs
            acc8[pl.ds(PACK, PACK), :] += dhs * (xn * g_ref[...])
            dhp = dhs * (1.0 + sc_ref[...])
            acc8[pl.ds(2 * PACK, PACK), :] += dhp * xn
            dxn = dhp * g_ref[...]
            gx_ref[pl.ds(r0, PACK), :] = dx2_ref[pl.ds(r0, PACK), :] + r * (dxn - xn * _rowmean(dxn * xn))
        _strips(tm, PACK, strip)

        @pl.when(i == n_i - 1)
        def _():
            for k in range(3):
                st_ref[pl.ds(k, 1), :] = jnp.sum(acc8[pl.ds(k * PACK, PACK), :], axis=0, keepdims=True)

    tile = pl.BlockSpec((tm, D), lambda i: (i, 0))
    return pl.pallas_call(
        body, name="in_proj_bwd", grid=(n_i,),
        in_specs=[pl.BlockSpec((tm, N_CHIP * wn), lambda i: (i, 0)),
                  pl.BlockSpec((N_CHIP, D, wn), lambda i: (0, 0, 0), pipeline_mode=pl.Buffered(1)),
                  tile, tile, _vec_spec(1), _vec_spec(1)],
        out_specs=[tile, pl.BlockSpec((SUB, D), lambda i: (0, 0))],
        out_shape=[_sds((s_len, D), F32), _sds((SUB, D), F32)],
        scratch_shapes=[pltpu.VMEM((tm, D), F32), pltpu.VMEM((3 * PACK, D), F32)],
        compiler_params=_params(("arbitrary",)),
    )(dp, wg_in, x, dx2, scale, g_pre)


def _grad_matmul(a, b, name):
    s_len, n = b.shape
    cb = min(2 * D, n)
    tn = 512
    per = cb // tn

    def body(a_ref, b_ref, ob_ref):
        ob_ref[0] = lax.dot_general(a_ref[...], b_ref[...], (((0,), (0,)), ((), ())),
                                    preferred_element_type=F32).astype(BF16)

    return pl.pallas_call(
        body, name=name, grid=(n // tn,),
        in_specs=[pl.BlockSpec((s_len, D), lambda j: (0, 0), pipeline_mode=pl.Buffered(1)),
                  pl.BlockSpec((s_len, tn), lambda j: (0, j))],
        out_specs=pl.BlockSpec((1, D, tn), lambda j: (j // per, 0, j % per)),
        out_shape=_sds((n // cb, D, cb), BF16),
        compiler_params=_params(("arbitrary",)),
    )(a, b)


def _local_step(x, target, shift, scale, gate, g_pre, conv_w_full, conv_b, conv_ln_g, conv_ln_b,
                sgu_ln_g, sgu_ln_b, w_sgu, b_sgu, g_final, wg_in, out_shards):
    conv_wb = jnp.repeat(conv_w_full, SUB, axis=0)
    causal = jnp.tril(jnp.ones((CHUNK, CHUNK), dtype=bool))
    wt = jnp.where(causal[None], w_sgu, 0.0).astype(BF16)
    wtt = jnp.swapaxes(wt, 1, 2)
    bias_full = jnp.repeat(b_sgu.T, LANE, axis=1)

    p, hb, gathered = _in_proj_gather(x, shift, scale, g_pre, wg_in, out_shards)
    w_co, w_so, w_o = (g.reshape(D, D) for g in gathered)
    ya_in, y1 = _branch_a_fwd(p, conv_wb, conv_b, conv_ln_g, conv_ln_b)
    yb_in = _branch_b_fwd(p, wt, bias_full, sgu_ln_g, sgu_ln_b)
    dx2, dya_in, dyb_in, dp, mb, dob, dyab, dybb, sums_o = _out_proj(
        p, ya_in, yb_in, x, target, gate, g_final, w_co, w_so, w_o)
    dp, st_a = _branch_a_bwd(p, y1, dya_in, dp, conv_wb, conv_ln_g, conv_ln_b)
    dp, st_b, gbt, gws = _branch_b_bwd(p, dyb_in, dp, wt, wtt, bias_full, sgu_ln_g, sgu_ln_b)
    grad_x, st_i = _in_proj_bwd(dp, wg_in, x, dx2, shift, scale, g_pre)
    gw_o = _grad_matmul(mb, dob, "grad_w_o")
    gw_co = _grad_matmul(ya_in, dyab, "grad_w_conv_out")
    gw_so = _grad_matmul(yb_in, dybb, "grad_w_sgu_out")
    return dict(
        grad_x=grad_x, loss_cols=sums_o[2:3], g_final=sums_o[0:1], d_gate=sums_o[1:2],
        d_shift=st_i[0:1], d_scale=st_i[1:2], g_pre=st_i[2:3],
        conv_ln_g=st_a[0:1], conv_ln_b=st_a[1:2], conv_b=st_a[2:3], conv_w=st_a[SUB:SUB + CONV_K],
        sgu_ln_g=st_b[0:1], sgu_ln_b=st_b[1:2], b_sgu=gbt[:, :HEADS].T, w_sgu=gws,
        hb=hb, dp=dp, w_o=gw_o, w_conv_out=gw_co, w_sgu_out=gw_so)


ANY_SPEC = pl.BlockSpec(memory_space=pl.ANY)
VMEM_SPEC = pl.BlockSpec(memory_space=pltpu.VMEM)


def _place():
    return lax.axis_index("x"), lax.axis_index("y"), lax.axis_index("c")


def _peer(k):
    x, y, c = _place()
    return (1 - x if k & 4 else x, 1 - y if k & 2 else y, 1 - c if k & 1 else c)


def _dev_of(p):
    return 4 * p[0] + 2 * p[1] + p[2]


def _chip_of(p):
    return 2 * p[0] + p[1]


def _rdma(src, dst, send_sem, recv_sem, to):
    return pltpu.make_async_remote_copy(src_ref=src, dst_ref=dst, send_sem=send_sem, recv_sem=recv_sem,
                                        device_id=to, device_id_type=MESH)


CHIP_PEERS = (2, 4, 6)
ALL_PEERS = tuple(range(1, N_DEV))
SIBLING = 1


def _setup_comm(c8, w_ada_s, b_ada_s, convw_s):
    n_mod = w_ada_s.shape[1]
    rows = SUB * N_DEV

    def body(c8_ref, wada_ref, bada_ref, cw_ref, call_ref, mod_ref, cwall_ref, csend, crecv, wsend, wrecv, msend, mrecv):
        me = _place()
        dev, chip = _dev_of(me), _chip_of(me)

        def c_rows(d):
            return call_ref.at[pl.ds(pl.multiple_of(d * SUB, SUB), SUB), :]

        call_ref[pl.ds(pl.multiple_of(dev * SUB, SUB), SUB), :] = c8_ref[...]
        cwall_ref[chip] = cw_ref[...]
        c_out = [_rdma(c8_ref, c_rows(dev), csend.at[k], crecv.at[k], _peer(k)) for k in ALL_PEERS]
        w_out = [_rdma(cw_ref, cwall_ref.at[chip], wsend.at[k], wrecv.at[k], _peer(k)) for k in CHIP_PEERS]
        for cp in c_out + w_out:
            cp.start()
        for k in ALL_PEERS:
            _rdma(c8_ref, c_rows(_dev_of(_peer(k))), csend.at[k], crecv.at[k], _peer(k)).wait_recv()
        part = jnp.dot(call_ref[...].astype(BF16), wada_ref[...].astype(BF16), preferred_element_type=F32) + bada_ref[...]
        mod_ref[chip] = part
        m_out = [_rdma(mod_ref.at[chip], mod_ref.at[chip], msend.at[k], mrecv.at[k], _peer(k)) for k in CHIP_PEERS]
        for cp in m_out:
            cp.start()
        for k in CHIP_PEERS:
            pc = _chip_of(_peer(k))
            _rdma(cw_ref, cwall_ref.at[pc], wsend.at[k], wrecv.at[k], _peer(k)).wait_recv()
            _rdma(mod_ref.at[pc], mod_ref.at[pc], msend.at[k], mrecv.at[k], _peer(k)).wait_recv()
        for cp in c_out + w_out + m_out:
            cp.wait_send()

    return pl.pallas_call(
        body, name="setup_comm",
        in_specs=[VMEM_SPEC] * 4, out_specs=[VMEM_SPEC] * 3,
        out_shape=[_sds((rows, D), F32), _sds((N_CHIP, rows, n_mod), F32), _sds((N_CHIP,) + convw_s.shape, F32)],
        scratch_shapes=[pltpu.SemaphoreType.DMA((N_DEV,))] * 6,
        compiler_params=_params(),
    )(c8, w_ada_s, b_ada_s, convw_s)


def _gather_weights(shards):
    n = len(shards)

    def body(*refs):
        ins, outs = refs[:n], refs[n:2 * n]
        lsem, isend, irecv, dsend, drecv = refs[2 * n:]
        me = _place()
        chip, c = _chip_of(me), me[2]
        local = [pltpu.make_async_copy(ins[t], outs[t].at[chip], lsem.at[t]) for t in range(n)]
        for cp in local:
            cp.start()

        def half(t, which):
            hr = shards[t].shape[0] // 2
            return pl.ds(pl.multiple_of(which * hr, hr), hr)

        sends = []
        for t in range(n):
            for j, k in enumerate(CHIP_PEERS):
                cp = _rdma(ins[t].at[half(t, c)], outs[t].at[chip, half(t, c)], isend.at[t, j], irecv.at[t, j], _peer(k))
                cp.start()
                sends.append(cp)
        for t in range(n):
            for j, k in enumerate(CHIP_PEERS):
                blk = outs[t].at[_chip_of(_peer(k)), half(t, c)]
                _rdma(blk, blk, isend.at[t, j], irecv.at[t, j], _peer(k)).wait_recv()
                cp = _rdma(blk, blk, dsend.at[t, j], drecv.at[t, j], _peer(SIBLING))
                cp.start()
                sends.append(cp)
        for t in range(n):
            for j, k in enumerate(CHIP_PEERS):
                blk = outs[t].at[_chip_of(_peer(k)), half(t, 1 - c)]
                _rdma(blk, blk, dsend.at[t, j], drecv.at[t, j], _peer(SIBLING)).wait_recv()
        for cp in sends:
            cp.wait_send()
        for cp in local:
            cp.wait()

    return pl.pallas_call(
        body, name="gather_weights",
        in_specs=[VMEM_SPEC] * n, out_specs=[VMEM_SPEC] * n,
        out_shape=[_sds((N_CHIP,) + s.shape, s.dtype) for s in shards],
        scratch_shapes=[pltpu.SemaphoreType.DMA((n,))] + [pltpu.SemaphoreType.DMA((n, len(CHIP_PEERS)))] * 4,
        compiler_params=_params(),
    )(*shards)


def _gather_phases(row_counts, ins, dsts, sems):
    n = len(row_counts)
    lsem, isend, irecv, dsend, drecv = sems
    me = _place()
    chip, c = _chip_of(me), me[2]

    def half(t, which):
        hr = row_counts[t] // 2
        return pl.ds(pl.multiple_of(which * hr, hr), hr)

    def local(t):
        return pltpu.make_async_copy(ins[t], dsts[t].at[chip], lsem.at[t])

    def to_chip(t, j):
        return _rdma(ins[t].at[half(t, c)], dsts[t].at[chip, half(t, c)], isend.at[t, j], irecv.at[t, j], _peer(CHIP_PEERS[j]))

    def landed(t, j, which):
        return dsts[t].at[_chip_of(_peer(CHIP_PEERS[j])), half(t, which)]

    def to_sibling(t, j):
        return _rdma(landed(t, j, c), landed(t, j, c), dsend.at[t, j], drecv.at[t, j], _peer(SIBLING))

    pairs = [(t, j) for t in range(n) for j in range(len(CHIP_PEERS))]

    def phase_a():
        for t in range(n):
            local(t).start()
        for t, j in pairs:
            to_chip(t, j).start()

    def phase_b():
        for t, j in pairs:
            _rdma(landed(t, j, c), landed(t, j, c), isend.at[t, j], irecv.at[t, j], _peer(CHIP_PEERS[j])).wait_recv()
            to_sibling(t, j).start()

    def phase_c():
        for t, j in pairs:
            _rdma(landed(t, j, 1 - c), landed(t, j, 1 - c), dsend.at[t, j], drecv.at[t, j], _peer(SIBLING)).wait_recv()
        for t, j in pairs:
            to_chip(t, j).wait_send()
            to_sibling(t, j).wait_send()
        for t in range(n):
            local(t).wait()

    return phase_a, phase_b, phase_c


def _in_proj_gather(x, shift, scale, g_pre, wg_in, shards):
    s_len = x.shape[0]
    tm = min(256, s_len)
    n_i = s_len // tm
    wn = wg_in.shape[2]
    n = len(shards)

    def body(x_ref, sh_ref, sc_ref, g_ref, w_ref, *refs):
        ins, p_ref, hb_ref, outs = refs[:n], refs[n], refs[n + 1], refs[n + 2:2 * n + 2]
        gath, sems = refs[2 * n + 2:3 * n + 2], refs[3 * n + 2:]
        phases = _gather_phases([s.shape[0] for s in shards], ins, gath, sems)
        i = pl.program_id(0)
        for step, phase in zip((0, n_i // 2, n_i - 1), phases):
            pl.when(i == step)(phase)

        @pl.when(i == n_i - 1)
        def _():
            for t in range(n):
                outs[t][...] = gath[t][...]

        def strip(r0):
            xs = x_ref[pl.ds(r0, PACK), :]
            r = lax.rsqrt(_rowmean(xs * xs) + EPS)
            h = (xs * r) * g_ref[...] * (1.0 + sc_ref[...]) + sh_ref[...]
            hb_ref[pl.ds(r0, PACK), :] = h.astype(BF16)
        _strips(tm, PACK, strip)
        hb = hb_ref[...]
        for j in range(N_CHIP):
            p_ref[:, pl.ds(j * wn, wn)] = jnp.dot(hb, w_ref[j], preferred_element_type=F32).astype(BF16)

    res = pl.pallas_call(
        body, name="in_proj", grid=(n_i,),
        in_specs=[pl.BlockSpec((tm, D), lambda i: (i, 0)), _vec_spec(1), _vec_spec(1), _vec_spec(1),
                  pl.BlockSpec((N_CHIP, D, wn), lambda i: (0, 0, 0), pipeline_mode=pl.Buffered(1))] + [VMEM_SPEC] * n,
        out_specs=[pl.BlockSpec((tm, N_CHIP * wn), lambda i: (i, 0)), pl.BlockSpec((tm, D), lambda i: (i, 0))] + [VMEM_SPEC] * n,
        out_shape=([_sds((s_len, N_SEC * D), BF16), _sds((s_len, D), BF16)]
                   + [_sds((N_CHIP,) + s.shape, s.dtype) for s in shards]),
        scratch_shapes=([pltpu.VMEM((N_CHIP,) + s.shape, s.dtype) for s in shards]
                        + [pltpu.SemaphoreType.DMA((n,))] + [pltpu.SemaphoreType.DMA((n, len(CHIP_PEERS)))] * 4),
        compiler_params=_params(("arbitrary",)),
    )(x, shift, scale, g_pre, wg_in, *shards)
    return res[0], res[1], res[2:]


def _reduce_scatter(grads, name):
    n = len(grads)
    shapes = [g.shape[2:] for g in grads]

    def body(*refs):
        ins, outs = refs[:n], refs[n:2 * n]
        pbufs, rbufs, accs = refs[2 * n:3 * n], refs[3 * n:4 * n], refs[4 * n:5 * n]
        psend, precv, csend, crecv, fsend, frecv = refs[5 * n:]
        me = _place()
        chip, c = _chip_of(me), me[2]
        sib = _peer(SIBLING)

        def to_sibling(t, d):
            return _rdma(ins[t].at[d, 1 - c], pbufs[t].at[d], psend.at[t, d], precv.at[t, d], sib)

        sends = []
        for t in range(n):
            for d in range(N_CHIP):
                cp = to_sibling(t, d)
                cp.start()
                sends.append(cp)
        for j in (1, 2, 3, 0):
            d = jnp.bitwise_xor(chip, j)
            for t in range(n):
                to_sibling(t, d).wait_recv()

                def pair_sum(r0, t=t, d=d, j=j):
                    rows = pl.ds(r0, PACK)
                    s = ins[t][d, c, rows, :].astype(F32) + pbufs[t][d, rows, :].astype(F32)
                    if j == 0:
                        accs[t][rows, :] = s
                    else:
                        pbufs[t][d, rows, :] = s.astype(BF16)
                _strips(shapes[t][0], PACK, pair_sum)
                if j:
                    cp = _rdma(pbufs[t].at[d], rbufs[t].at[j - 1], csend.at[t, j], crecv.at[t, j], _peer(2 * j))
                    cp.start()
                    sends.append(cp)
        for t in range(n):
            for j in (1, 2, 3):
                blk = rbufs[t].at[j - 1]
                _rdma(blk, blk, csend.at[t, j], crecv.at[t, j], _peer(2 * j)).wait_recv()

            def total(r0, t=t):
                rows = pl.ds(r0, PACK)
                s = accs[t][rows, :] + rbufs[t][0, rows, :].astype(F32)
                s = s + rbufs[t][1, rows, :].astype(F32)
                outs[t][c, rows, :] = s + rbufs[t][2, rows, :].astype(F32)
            _strips(shapes[t][0], PACK, total)
            cp = _rdma(outs[t].at[c], outs[t].at[c], fsend.at[t], frecv.at[t], sib)
            cp.start()
            sends.append(cp)
        for t in range(n):
            blk = outs[t].at[1 - c]
            _rdma(blk, blk, fsend.at[t], frecv.at[t], sib).wait_recv()
        for cp in sends:
            cp.wait_send()

    return pl.pallas_call(
        body, name=name,
        in_specs=[VMEM_SPEC] * n, out_specs=[VMEM_SPEC] * n,
        out_shape=[_sds((2,) + s, F32) for s in shapes],
        scratch_shapes=([pltpu.VMEM((N_CHIP,) + s, BF16) for s in shapes] + [pltpu.VMEM((N_CHIP - 1,) + s, BF16) for s in shapes]
                        + [pltpu.VMEM(s, F32) for s in shapes]
                        + [pltpu.SemaphoreType.DMA((n, N_CHIP))] * 4 + [pltpu.SemaphoreType.DMA((n,))] * 2),
        compiler_params=_params(),
    )(*grads)


def _reduce_phases(shapes, ins, outs, pbufs, rbufs, accs, sems):
    n = len(shapes)
    psend, precv, csend, crecv, fsend, frecv = sems
    me = _place()
    chip, c = _chip_of(me), me[2]
    sib = _peer(SIBLING)

    def to_sibling(t, d):
        return _rdma(ins[t].at[d, 1 - c], pbufs[t].at[d], psend.at[t, d], precv.at[t, d], sib)

    def to_chip(t, j):
        return _rdma(pbufs[t].at[jnp.bitwise_xor(chip, j)], rbufs[t].at[j - 1], csend.at[t, j], crecv.at[t, j], _peer(2 * j))

    def finished(t):
        return _rdma(outs[t].at[c], outs[t].at[c], fsend.at[t], frecv.at[t], sib)

    def phase_a():
        for t in range(n):
            for d in range(N_CHIP):
                to_sibling(t, d).start()

    def phase_b():
        for j in (1, 2, 3, 0):
            d = jnp.bitwise_xor(chip, j)
            for t in range(n):
                to_sibling(t, d).wait_recv()

                def pair_sum(r0, t=t, d=d, j=j):
                    rows = pl.ds(r0, PACK)
                    s = ins[t][d, c, rows, :].astype(F32) + pbufs[t][d, rows, :].astype(F32)
                    if j == 0:
                        accs[t][rows, :] = s
                    else:
                        pbufs[t][d, rows, :] = s.astype(BF16)
                _strips(shapes[t][0], PACK, pair_sum)
                if j:
                    to_chip(t, j).start()

    def phase_c():
        for t in range(n):
            for j in (1, 2, 3):
                blk = rbufs[t].at[j - 1]
                _rdma(blk, blk, csend.at[t, j], crecv.at[t, j], _peer(2 * j)).wait_recv()

            def total(r0, t=t):
                rows = pl.ds(r0, PACK)
                s = accs[t][rows, :] + rbufs[t][0, rows, :].astype(F32)
                s = s + rbufs[t][1, rows, :].astype(F32)
                outs[t][c, rows, :] = s + rbufs[t][2, rows, :].astype(F32)
            _strips(shapes[t][0], PACK, total)
            finished(t).start()

    def phase_d():
        for t in range(n):
            blk = outs[t].at[1 - c]
            _rdma(blk, blk, fsend.at[t], frecv.at[t], sib).wait_recv()
        for t in range(n):
            for d in range(N_CHIP):
                to_sibling(t, d).wait_send()
            for j in (1, 2, 3):
                to_chip(t, j).wait_send()
            finished(t).wait_send()

    return phase_a, phase_b, phase_c, phase_d


def _sum_small_phases(ins, outs, pbufs, buf4s, sems):
    n = len(ins)
    psend, precv, send, recv = sems
    chip = _chip_of(_place())

    def swap(t):
        return _rdma(ins[t], pbufs[t], psend.at[t], precv.at[t], _peer(SIBLING))

    def to_chip(t, k):
        return _rdma(buf4s[t].at[chip], buf4s[t].at[chip], send.at[t, k], recv.at[t, k], _peer(k))

    def phase_a():
        for t in range(n):
            swap(t).start()

    def phase_b():
        for t in range(n):
            swap(t).wait()
            buf4s[t][chip] = ins[t][...] + pbufs[t][...]
            for k in CHIP_PEERS:
                to_chip(t, k).start()

    def phase_c():
        for t in range(n):
            for k in CHIP_PEERS:
                blk = buf4s[t].at[_chip_of(_peer(k))]
                _rdma(blk, blk, send.at[t, k], recv.at[t, k], _peer(k)).wait_recv()
            outs[t][...] = (buf4s[t][0] + buf4s[t][1]) + (buf4s[t][2] + buf4s[t][3])

    def phase_d():
        for t in range(n):
            for k in CHIP_PEERS:
                to_chip(t, k).wait_send()

    return phase_a, phase_b, phase_c, phase_d


def _sum_small_scratch(blobs):
    n = len(blobs)
    return ([pltpu.VMEM(b.shape, F32) for b in blobs] + [pltpu.VMEM((N_CHIP,) + b.shape, F32) for b in blobs]
            + [pltpu.SemaphoreType.DMA((n,))] * 2 + [pltpu.SemaphoreType.DMA((n, N_DEV))] * 2)


def _reduce_scratch(shapes):
    n = len(shapes)
    return ([pltpu.VMEM((N_CHIP,) + s, BF16) for s in shapes] + [pltpu.VMEM((N_CHIP - 1,) + s, BF16) for s in shapes]
            + [pltpu.VMEM(s, F32) for s in shapes]
            + [pltpu.SemaphoreType.DMA((n, N_CHIP))] * 4 + [pltpu.SemaphoreType.DMA((n,))] * 2)


def _grad_matmul_reduce(a, b, name, grads, blobs):
    s_len, n_cols = b.shape
    cb = min(2 * D, n_cols)
    tn = 512
    per = cb // tn
    steps = n_cols // tn
    n, nb = len(grads), len(blobs)
    shapes = [g.shape[2:] for g in grads]
    n_red = len(_reduce_scratch(shapes))

    def body(a_ref, b_ref, *refs):
        ins, bins = refs[:n], refs[n:n + nb]
        ob_ref, outs, bouts = refs[n + nb], refs[n + nb + 1:2 * n + nb + 1], refs[2 * n + nb + 1:2 * (n + nb) + 1]
        scratch = refs[2 * (n + nb) + 1:]
        fulls, red, small = scratch[:n], scratch[n:n + n_red], scratch[n + n_red:]
        phases = _reduce_phases(shapes, ins, fulls, red[:n], red[n:2 * n], red[2 * n:3 * n], red[3 * n:])
        small_phases = _sum_small_phases(bins, bouts, small[:nb], small[nb:2 * nb], small[2 * nb:])
        j = pl.program_id(0)
        for step, phase in zip((0, 2, steps - 2, steps - 1), phases):
            pl.when(j == step)(phase)
        for step, phase in zip((1, 3, steps - 2, steps - 1), small_phases):
            pl.when(j == step)(phase)

        @pl.when(j == steps - 1)
        def _():
            for t in range(n):
                outs[t][...] = fulls[t][...]
        ob_ref[0] = lax.dot_general(a_ref[...], b_ref[...], (((0,), (0,)), ((), ())),
                                    preferred_element_type=F32).astype(BF16)

    res = pl.pallas_call(
        body, name=name, grid=(steps,),
        in_specs=[pl.BlockSpec((s_len, D), lambda j: (0, 0), pipeline_mode=pl.Buffered(1)),
                  pl.BlockSpec((s_len, tn), lambda j: (0, j))] + [VMEM_SPEC] * (n + nb),
        out_specs=[pl.BlockSpec((1, D, tn), lambda j: (j // per, 0, j % per))] + [VMEM_SPEC] * (n + nb),
        out_shape=([_sds((n_cols // cb, D, cb), BF16)] + [_sds((2,) + s, F32) for s in shapes]
                   + [_sds(bl.shape, F32) for bl in blobs]),
        scratch_shapes=[pltpu.VMEM((2,) + s, F32) for s in shapes] + _reduce_scratch(shapes) + _sum_small_scratch(blobs),
        compiler_params=_params(("arbitrary",)),
    )(a, b, *grads, *blobs)
    return res[0], res[1:1 + n], res[1 + n:]


def _scatter_grads(grads):
    n = len(grads)

    def body(*refs):
        ins, outs = refs[:n], refs[n:2 * n]
        lsem, send, recv = refs[2 * n:]
        me = _place()
        dev, chip, c = _dev_of(me), _chip_of(me), me[2]
        local = [pltpu.make_async_copy(ins[t].at[chip, c], outs[t].at[dev], lsem.at[t]) for t in range(n)]
        for cp in local:
            cp.start()
        sends = []
        for t in range(n):
            for k in ALL_PEERS:
                to = _peer(k)
                cp = _rdma(ins[t].at[_chip_of(to), to[2]], outs[t].at[dev], send.at[t, k], recv.at[t, k], to)
                cp.start()
                sends.append(cp)
        for t in range(n):
            for k in ALL_PEERS:
                blk = outs[t].at[_dev_of(_peer(k))]
                _rdma(blk, blk, send.at[t, k], recv.at[t, k], _peer(k)).wait_recv()
        for cp in sends:
            cp.wait_send()
        for cp in local:
            cp.wait()

    return pl.pallas_call(
        body, name="scatter_grads",
        in_specs=[ANY_SPEC] * n, out_specs=[ANY_SPEC] * n,
        out_shape=[_sds((N_DEV,) + g.shape[2:], g.dtype) for g in grads],
        scratch_shapes=[pltpu.SemaphoreType.DMA((n,))] + [pltpu.SemaphoreType.DMA((n, N_DEV))] * 2,
        compiler_params=_params(),
    )(*grads)


def _sum_devices(parts, name):
    _, r, cols = parts.shape
    tr = min(r, 128)

    def body(in_ref, o_ref):
        acc = in_ref[0].astype(F32)
        for d in range(1, N_DEV):
            acc = acc + in_ref[d].astype(F32)
        o_ref[...] = acc

    return pl.pallas_call(
        body, name=name, grid=(r // tr,),
        in_specs=[pl.BlockSpec((N_DEV, tr, cols), lambda i: (0, i, 0))],
        out_specs=pl.BlockSpec((tr, cols), lambda i: (i, 0)),
        out_shape=_sds((r, cols), F32),
        compiler_params=_params(("arbitrary",)),
    )(parts)


def _share_halves(reds):
    n = len(reds)

    def body(*refs):
        ins, outs = refs[:n], refs[n:2 * n]
        lsem, send, recv = refs[2 * n:]
        me = _place()
        c = me[2]
        local = [pltpu.make_async_copy(ins[t], outs[t].at[c], lsem.at[t]) for t in range(n)]
        sends = [_rdma(ins[t], outs[t].at[c], send.at[t], recv.at[t], _peer(SIBLING)) for t in range(n)]
        for cp in local + sends:
            cp.start()
        for t in range(n):
            _rdma(ins[t], outs[t].at[1 - c], send.at[t], recv.at[t], _peer(SIBLING)).wait_recv()
        for cp in sends:
            cp.wait_send()
        for cp in local:
            cp.wait()

    return pl.pallas_call(
        body, name="share_halves",
        in_specs=[VMEM_SPEC] * n, out_specs=[VMEM_SPEC] * n,
        out_shape=[_sds((2,) + r.shape, r.dtype) for r in reds],
        scratch_shapes=[pltpu.SemaphoreType.DMA((n,))] * 3,
        compiler_params=_params(),
    )(*reds)


def _sum_small(blobs):
    n = len(blobs)

    def body(*refs):
        ins, outs = refs[:n], refs[n:2 * n]
        pbufs, buf4s = refs[2 * n:3 * n], refs[3 * n:4 * n]
        psend, precv, send, recv = refs[4 * n:]
        me = _place()
        chip = _chip_of(me)
        pairs = [_rdma(ins[t], pbufs[t], psend.at[t], precv.at[t], _peer(SIBLING)) for t in range(n)]
        for cp in pairs:
            cp.start()
        out = []
        for t in range(n):
            pairs[t].wait()
            buf4s[t][chip] = ins[t][...] + pbufs[t][...]
            for k in CHIP_PEERS:
                cp = _rdma(buf4s[t].at[chip], buf4s[t].at[chip], send.at[t, k], recv.at[t, k], _peer(k))
                cp.start()
                out.append(cp)
        for t in range(n):
            for k in CHIP_PEERS:
                blk = buf4s[t].at[_chip_of(_peer(k))]
                _rdma(blk, blk, send.at[t, k], recv.at[t, k], _peer(k)).wait_recv()
            outs[t][...] = (buf4s[t][0] + buf4s[t][1]) + (buf4s[t][2] + buf4s[t][3])
        for cp in out:
            cp.wait_send()

    return pl.pallas_call(
        body, name="sum_small",
        in_specs=[VMEM_SPEC] * n, out_specs=[VMEM_SPEC] * n, out_shape=[_sds(b.shape, F32) for b in blobs],
        scratch_shapes=([pltpu.VMEM(b.shape, F32) for b in blobs] + [pltpu.VMEM((N_CHIP,) + b.shape, F32) for b in blobs]
                        + [pltpu.SemaphoreType.DMA((n,))] * 2 + [pltpu.SemaphoreType.DMA((n, N_DEV))] * 2),
        compiler_params=_params(),
    )(*blobs)


def _adamw_math(w, g, m, v):
    m = ADAM_B1 * m + (1.0 - ADAM_B1) * g
    v = ADAM_B2 * v + (1.0 - ADAM_B2) * (g * g)
    m_hat = m / (1.0 - ADAM_B1 ** ADAM_STEP)
    v_hat = v / (1.0 - ADAM_B2 ** ADAM_STEP)
    delta = -ADAM_LR * (m_hat / (jnp.sqrt(v_hat) + ADAM_EPS) + ADAM_WD * w)
    return delta, m, v


def _row_tile(r, cols):
    if r * cols * 4 <= 2 ** 20:
        return r
    return next(t for t in (512, 256, 128, 64, 32, 16, 8) if r % t == 0 and t * cols * 4 <= 2 ** 20)


def _adamw(w, g, m, v, name):
    r, cols = w.shape
    tr = _row_tile(r, cols)

    def body(w_ref, g_ref, m_ref, v_ref, d_ref, nm_ref, nv_ref):
        d_ref[...], nm_ref[...], nv_ref[...] = _adamw_math(w_ref[...], g_ref[...], m_ref[...], v_ref[...])

    spec = pl.BlockSpec((tr, cols), lambda i: (i, 0))
    return pl.pallas_call(
        body, name=name, grid=(r // tr,), in_specs=[spec] * 4, out_specs=[spec] * 3,
        out_shape=[_sds((r, cols), F32)] * 3, compiler_params=_params(("arbitrary",)),
    )(w, g, m, v)


def _adamw_ada(w, ct, dm, m, v):
    r, cols = w.shape
    tr = _row_tile(r, cols)

    def body(w_ref, ct_ref, dm_ref, m_ref, v_ref, g_ref, d_ref, nm_ref, nv_ref):
        g = jnp.dot(ct_ref[...], dm_ref[...], preferred_element_type=F32)
        g_ref[...] = g
        d_ref[...], nm_ref[...], nv_ref[...] = _adamw_math(w_ref[...], g, m_ref[...], v_ref[...])

    spec = pl.BlockSpec((tr, cols), lambda i: (i, 0))
    return pl.pallas_call(
        body, name="adamw_ada", grid=(r // tr,),
        in_specs=[spec, pl.BlockSpec((tr, LANE), lambda i: (i, 0)), pl.BlockSpec((LANE, cols), lambda i: (0, 0)), spec, spec],
        out_specs=[spec] * 4, out_shape=[_sds((r, cols), F32)] * 4, compiler_params=_params(("arbitrary",)),
    )(w, ct, dm, m, v)


BLOB_VEC, BLOB_BSGU, BLOB_CONV, BLOB_ADA, BLOB_DMOD, BLOB_LOSS, BLOB_ROWS = 0, 8, 16, 48, 56, 80, 88
N_VEC = 7


def _adamw_small(tot, g_w_sgu, g_conv, params):
    n = len(params)

    def body(*refs):
        tot_ref, gws_ref, gconv_ref = refs[:3]
        wmv = refs[3:3 + 3 * n]
        outs = refs[3 + 3 * n:]
        grads = [tot_ref[pl.ds(BLOB_VEC + i, 1), :] for i in range(N_VEC)]
        grads += [tot_ref[pl.ds(BLOB_BSGU, HEADS), pl.ds(0, CHUNK)], gconv_ref[...], gws_ref[...], tot_ref[pl.ds(BLOB_ADA, 3), :]]
        for i, g in enumerate(grads):
            w_ref, m_ref, v_ref = wmv[3 * i:3 * i + 3]
            d, nm, nv = _adamw_math(w_ref[...], g, m_ref[...], v_ref[...])
            outs[4 * i][...] = g
            outs[4 * i + 1][...] = d
            outs[4 * i + 2][...] = nm
            outs[4 * i + 3][...] = nv

    flat = [a for wmv in params for a in wmv]
    return pl.pallas_call(
        body, name="adamw_small",
        in_specs=[VMEM_SPEC] * (3 + len(flat)), out_specs=[VMEM_SPEC] * (4 * n),
        out_shape=[_sds(wmv[0].shape, F32) for wmv in params for _ in range(4)],
        compiler_params=_params(),
    )(tot, g_w_sgu, g_conv, *flat)


def _set_rows(buf, row, val):
    return lax.dynamic_update_slice(buf, val.astype(F32), (row, 0))


def kernel(x, c, w_ada, b_ada, g_pre, w_in, conv_w, conv_b, conv_ln_g, conv_ln_b, w_conv_out, sgu_ln_g, sgu_ln_b, w_sgu, b_sgu, w_sgu_out, w_o, g_final, loss_target, m_w_ada, m_b_ada, m_g_pre, m_w_in, m_conv_w, m_conv_b, m_conv_ln_g, m_conv_ln_b, m_w_conv_out, m_sgu_ln_g, m_sgu_ln_b, m_w_sgu, m_b_sgu, m_w_sgu_out, m_w_o, m_g_final, v_w_ada, v_b_ada, v_g_pre, v_w_in, v_conv_w, v_conv_b, v_conv_ln_g, v_conv_ln_b, v_w_conv_out, v_sgu_ln_g, v_sgu_ln_b, v_w_sgu, v_b_sgu, v_w_sgu_out, v_w_o, v_g_final):
    me = _place()
    dev, chip = _dev_of(me), _chip_of(me)
    n_ada = w_ada.shape[2]
    conv_cols = conv_w.shape[2]

    b_ada_s = lax.dynamic_slice(b_ada, (0, chip * n_ada), (1, n_ada))
    c_all, mod_all, cw_all = _setup_comm(
        jnp.broadcast_to(c, (SUB, D)), w_ada[0], b_ada_s, jnp.pad(conv_w[0], ((0, HALO - CONV_K), (0, 0))))
    mod = lax.dynamic_slice(mod_all, (0, dev * SUB, 0), (N_CHIP, 1, n_ada)).reshape(1, 3 * D)
    shift, scale, gate = mod[:, :D], mod[:, D:2 * D], mod[:, 2 * D:]
    conv_w_full = jnp.swapaxes(cw_all, 0, 1).reshape(HALO, D)[:CONV_K]

    (wg_in,) = _gather_weights([w_in[0].astype(BF16)])

    loc = _local_step(x[0], loss_target[0], shift, scale, gate, g_pre, conv_w_full, conv_b, conv_ln_g, conv_ln_b,
                      sgu_ln_g, sgu_ln_b, w_sgu[0], b_sgu[0], g_final.reshape(1, D), wg_in,
                      [w_conv_out[0].astype(BF16), w_sgu_out[0].astype(BF16), w_o[0].astype(BF16)])

    d_mod = jnp.concatenate([loc["d_shift"], loc["d_scale"], loc["d_gate"]], axis=0)
    blob = jnp.zeros((BLOB_ROWS, D), F32)
    for i, name in enumerate(["g_pre", "conv_b", "conv_ln_g", "conv_ln_b", "sgu_ln_g", "sgu_ln_b", "g_final"]):
        blob = _set_rows(blob, BLOB_VEC + i, loc[name])
    blob = _set_rows(blob, BLOB_BSGU, loc["b_sgu"])
    blob = _set_rows(blob, BLOB_CONV, loc["conv_w"])
    blob = _set_rows(blob, BLOB_ADA, d_mod)
    blob = lax.dynamic_update_slice(blob, d_mod, (BLOB_DMOD + 3 * dev, 0))
    blob = _set_rows(blob, BLOB_LOSS, loc["loss_cols"])

    big = ["w_in", "w_conv_out", "w_sgu_out", "w_o"]
    contrib_out = [loc[name].reshape(N_CHIP, 2, D // (2 * N_CHIP), D) for name in big[1:]]
    gw_in, full_out, (tot, g_w_sgu) = _grad_matmul_reduce(
        loc["hb"], loc["dp"], "grad_w_in", contrib_out, [blob, loc["w_sgu"].reshape(HEADS * CHUNK, CHUNK)])
    full_in = _reduce_scatter([gw_in.reshape(N_CHIP, 2, D // 2, gw_in.shape[2])], "reduce_w_in")
    g_big = {name: f.reshape(2 * f.shape[1], f.shape[2]) for name, f in zip(big, list(full_in) + list(full_out))}

    loss = jnp.sum(tot[BLOB_LOSS])
    g_conv_s = lax.dynamic_slice(tot, (BLOB_CONV, chip * conv_cols), (CONV_K, conv_cols))
    d_mod_all = tot[BLOB_DMOD:BLOB_DMOD + 3 * N_DEV].reshape(N_DEV, 3 * D)

    ct = jnp.pad(c_all[::SUB].T, ((0, 0), (0, LANE - N_DEV))).astype(BF16)
    dm = jnp.pad(lax.dynamic_slice(d_mod_all, (0, chip * n_ada), (N_DEV, n_ada)), ((0, LANE - N_DEV), (0, 0))).astype(BF16)
    g_ada, d_ada, nm_ada, nv_ada = _adamw_ada(w_ada[0], ct, dm, m_w_ada[0], v_w_ada[0])

    upd = {}
    for name, w, m, v in [("w_in", w_in, m_w_in, v_w_in), ("w_conv_out", w_conv_out, m_w_conv_out, v_w_conv_out),
                          ("w_sgu_out", w_sgu_out, m_w_sgu_out, v_w_sgu_out), ("w_o", w_o, m_w_o, v_w_o)]:
        upd[name] = _adamw(w[0], g_big[name], m[0], v[0], "adamw_" + name)

    def wmv(w, m, v, shape):
        return tuple(a.reshape(shape) for a in (w, m, v))

    small_params = [wmv(w, m, v, (1, D)) for w, m, v in [
        (g_pre, m_g_pre, v_g_pre), (conv_b, m_conv_b, v_conv_b), (conv_ln_g, m_conv_ln_g, v_conv_ln_g),
        (conv_ln_b, m_conv_ln_b, v_conv_ln_b), (sgu_ln_g, m_sgu_ln_g, v_sgu_ln_g), (sgu_ln_b, m_sgu_ln_b, v_sgu_ln_b),
        (g_final, m_g_final, v_g_final)]]
    small_params += [wmv(b_sgu, m_b_sgu, v_b_sgu, (HEADS, CHUNK)), wmv(conv_w, m_conv_w, v_conv_w, (CONV_K, conv_cols)),
                     wmv(w_sgu, m_w_sgu, v_w_sgu, (HEADS * CHUNK, CHUNK)), wmv(b_ada, m_b_ada, v_b_ada, (3, D))]
    small_out = _adamw_small(tot, g_w_sgu, g_conv_s, small_params)

    def leaves(kind):
        vecs = [small_out[4 * i + kind] for i in range(N_VEC)]
        o_b_sgu, o_conv, o_w_sgu, o_b_ada = (small_out[4 * (N_VEC + i) + kind] for i in range(4))
        ada = (g_ada, d_ada, nm_ada, nv_ada)[kind]
        def bigk(name):
            return (g_big[name] if kind == 0 else upd[name][kind - 1])[None]
        return [ada[None], o_b_ada.reshape(1, 3 * D), vecs[0], bigk("w_in"), o_conv[None], vecs[1], vecs[2], vecs[3],
                bigk("w_conv_out"), vecs[4], vecs[5], o_w_sgu.reshape(1, HEADS, CHUNK, CHUNK), o_b_sgu[None],
                bigk("w_sgu_out"), bigk("w_o"), vecs[6].reshape(D)]

    return (loss, loc["grad_x"][None], *leaves(0), *leaves(1), *leaves(2), *leaves(3))
```

```python
import functools

import jax
import jax.numpy as jnp
from jax import lax
from jax.experimental import pallas as pl
from jax.experimental.pallas import tpu as pltpu

F32 = jnp.float32
BF16 = jnp.bfloat16
MESH = pl.DeviceIdType.MESH

D = 1024
N_SEC = 8
N_CHIP = 4
N_DEV = 8
EPS = 1e-6
CONV_K = 31
HALO = 32
CHUNK = 128
HEADS = 8
LANE = 128
SUB = 8
PACK = 16
VMEM_LIMIT = 56 * 1024 * 1024

ADAM_LR, ADAM_B1, ADAM_B2, ADAM_EPS, ADAM_WD, ADAM_STEP = 0.001, 0.9, 0.999, 1e-08, 0.01, 10

_SQRT_HALF = 0.7071067811865476
_INV_SQRT_2PI = 0.3989422804014327


def _sds(shape, dtype):
    return jax.ShapeDtypeStruct(shape, dtype)


def _params(sem=None):
    if sem is None:
        return pltpu.CompilerParams(vmem_limit_bytes=VMEM_LIMIT)
    return pltpu.CompilerParams(dimension_semantics=sem, vmem_limit_bytes=VMEM_LIMIT)


def _strips(n_rows, rows, fn):
    def step(s, carry):
        fn(pl.multiple_of(s * rows, rows))
        return carry
    lax.fori_loop(0, n_rows // rows, step, 0)


def _sigmoid(v):
    return 1.0 / (1.0 + jnp.exp(-v))


def _gelu(v):
    return 0.5 * v * (1.0 + lax.erf(v * _SQRT_HALF))


def _gelu_and_grad(v):
    cdf = 0.5 * (1.0 + lax.erf(v * _SQRT_HALF))
    return v * cdf, cdf + v * jnp.exp(-0.5 * v * v) * _INV_SQRT_2PI


def _dsilu(v, sg):
    return sg * (1.0 + v * (1.0 - sg))


def _rowmean(v):
    return jnp.mean(v, axis=-1, keepdims=True)


def _vec_spec(grid_rank):
    zeros = (0, 0)
    if grid_rank == 1:
        return pl.BlockSpec((1, D), lambda i: zeros)
    return pl.BlockSpec((1, D), lambda i, j: zeros)


def _in_proj(x, shift, scale, g_pre, wg_in):
    s_len = x.shape[0]
    tm = min(512, s_len)
    n_i = s_len // tm
    wn = wg_in.shape[2]

    def body(x_ref, sh_ref, sc_ref, g_ref, w_ref, p_ref, hb_ref):
        def strip(r0):
            xs = x_ref[pl.ds(r0, PACK), :]
            r = lax.rsqrt(_rowmean(xs * xs) + EPS)
            h = (xs * r) * g_ref[...] * (1.0 + sc_ref[...]) + sh_ref[...]
            hb_ref[pl.ds(r0, PACK), :] = h.astype(BF16)
        _strips(tm, PACK, strip)
        hb = hb_ref[...]
        for j in range(N_CHIP):
            p_ref[:, pl.ds(j * wn, wn)] = jnp.dot(hb, w_ref[j], preferred_element_type=F32).astype(BF16)

    return pl.pallas_call(
        body, name="in_proj", grid=(n_i,),
        in_specs=[pl.BlockSpec((tm, D), lambda i: (i, 0)), _vec_spec(1), _vec_spec(1), _vec_spec(1),
                  pl.BlockSpec((N_CHIP, D, wn), lambda i: (0, 0, 0), pipeline_mode=pl.Buffered(1))],
        out_specs=[pl.BlockSpec((tm, N_CHIP * wn), lambda i: (i, 0)), pl.BlockSpec((tm, D), lambda i: (i, 0))],
        out_shape=[_sds((s_len, N_SEC * D), BF16), _sds((s_len, D), BF16)],
        compiler_params=_params(("arbitrary",)),
    )(x, shift, scale, g_pre, wg_in)


def _conv_taps(win_ref, r0, lt, weight_of_offset, rows):
    lanes = pl.ds(lt * LANE, LANE)
    win = win_ref[pl.ds(r0, rows + HALO), lanes]
    n_out = rows // SUB
    acc = [jnp.zeros((SUB, LANE), F32) for _ in range(n_out)]
    for phase in range(SUB):
        offs = [o for o in weight_of_offset if o % SUB == phase]
        if not offs:
            continue
        q_max = max(o // SUB for o in offs)
        span = (n_out + q_max) * SUB
        sh = win[phase:phase + span, :]
        for o in offs:
            q = o // SUB
            w = weight_of_offset[o](lanes)
            for m in range(n_out):
                acc[m] = acc[m] + w * sh[(m + q) * SUB:(m + q + 1) * SUB, :]
    return acc


def _branch_a_fwd(p, conv_wb, conv_b, ln_g, ln_b):
    s_len = p.shape[0]
    tm = min(256, s_len)
    n_i = s_len // tm
    rows = 32

    def body(p_ref, wb_ref, cb_ref, g_ref, b_ref, ya_ref, y1_ref, abuf):
        @pl.when(pl.program_id(0) == 0)
        def _():
            abuf[pl.ds(0, HALO), :] = jnp.zeros((HALO, D), F32)

        def glu(r0):
            val = p_ref[pl.ds(r0, PACK), pl.ds(0, D)].astype(F32)
            gl = p_ref[pl.ds(r0, PACK), pl.ds(D, D)].astype(F32)
            abuf[pl.ds(HALO + r0, PACK), :] = val * _sigmoid(gl)
        _strips(tm, PACK,glu)

        taps = {HALO - (CONV_K - 1) + k: (lambda lanes, k=k: wb_ref[pl.ds(k * SUB, SUB), lanes]) for k in range(CONV_K)}

        def conv(r0):
            for lt in range(D // LANE):
                acc = _conv_taps(abuf, r0, lt, taps, rows)
                cb = cb_ref[:, pl.ds(lt * LANE, LANE)]
                for m, v in enumerate(acc):
                    y1_ref[pl.ds(r0 + m * SUB, SUB), pl.ds(lt * LANE, LANE)] = v + cb
        _strips(tm, rows, conv)

        def norm(r0):
            y1 = y1_ref[pl.ds(r0, PACK), :]
            mu = _rowmean(y1)
            yc = y1 - mu
            rstd = lax.rsqrt(_rowmean(yc * yc) + EPS)
            l1 = (yc * rstd) * g_ref[...] + b_ref[...]
            z = p_ref[pl.ds(r0, PACK), pl.ds(2 * D, D)].astype(F32)
            ya_ref[pl.ds(r0, PACK), :] = ((l1 * _sigmoid(l1)) * (z * _sigmoid(z))).astype(BF16)
        _strips(tm, PACK,norm)

        abuf[pl.ds(0, HALO), :] = abuf[pl.ds(tm, HALO), :]

    return pl.pallas_call(
        body, name="branch_a_fwd", grid=(n_i,),
        in_specs=[pl.BlockSpec((tm, 3 * D), lambda i: (i, 0)),
                  pl.BlockSpec((CONV_K * SUB, D), lambda i: (0, 0)), _vec_spec(1), _vec_spec(1), _vec_spec(1)],
        out_specs=[pl.BlockSpec((tm, D), lambda i: (i, 0)), pl.BlockSpec((tm, D), lambda i: (i, 0))],
        out_shape=[_sds((s_len, D), BF16), _sds((s_len, D), F32)],
        scratch_shapes=[pltpu.VMEM((tm + HALO, D), F32)],
        compiler_params=_params(("arbitrary",)),
    )(p, conv_wb, conv_b, ln_g, ln_b)


def _branch_b_fwd(p, wt, bias_full, ln_g, ln_b):
    s_len = p.shape[0]
    tm = min(256, s_len)
    n_i = s_len // tm

    def body(p_ref, wt_ref, bias_ref, g_ref, b_ref, yb_ref, vb, sbuf):
        def norm(r0):
            gv = _gelu(p_ref[pl.ds(r0, PACK), pl.ds(D, D)].astype(F32))
            mu = _rowmean(gv)
            vc = gv - mu
            rstd = lax.rsqrt(_rowmean(vc * vc) + EPS)
            vb[pl.ds(r0, PACK), :] = ((vc * rstd) * g_ref[...] + b_ref[...]).astype(BF16)
        _strips(tm, PACK,norm)

        for ck in range(tm // CHUNK):
            for h in range(HEADS):
                blk = (pl.ds(ck * CHUNK, CHUNK), pl.ds(h * LANE, LANE))
                sbuf[blk] = jnp.dot(wt_ref[h], vb[blk], preferred_element_type=F32) + bias_ref[:, pl.ds(h * LANE, LANE)]

        def gate(r0):
            u = _gelu(p_ref[pl.ds(r0, PACK), pl.ds(0, D)].astype(F32))
            z = p_ref[pl.ds(r0, PACK), pl.ds(2 * D, D)].astype(F32)
            yb_ref[pl.ds(r0, PACK), :] = (u * sbuf[pl.ds(r0, PACK), :] * (z * _sigmoid(z))).astype(BF16)
        _strips(tm, PACK,gate)

    return pl.pallas_call(
        body, name="branch_b_fwd", grid=(n_i,),
        in_specs=[pl.BlockSpec((tm, 3 * D), lambda i: (i, 1)),
                  pl.BlockSpec((HEADS, CHUNK, CHUNK), lambda i: (0, 0, 0)),
                  pl.BlockSpec((CHUNK, D), lambda i: (0, 0)), _vec_spec(1), _vec_spec(1)],
        out_specs=pl.BlockSpec((tm, D), lambda i: (i, 0)),
        out_shape=_sds((s_len, D), BF16),
        scratch_shapes=[pltpu.VMEM((tm, D), BF16), pltpu.VMEM((tm, D), F32)],
        compiler_params=_params(("arbitrary",)),
    )(p, wt, bias_full, ln_g, ln_b)


def _dot_t(a, b):
    return lax.dot_general(a, b, (((1,), (1,)), ((), ())), preferred_element_type=F32)


def _out_proj(p, ya_in, yb_in, x, target, gate, g_final, w_co, w_so, w_o):
    s_len = x.shape[0]
    tm = min(256, s_len)
    n_i = s_len // tm

    def body(pg_ref, ya_ref, yb_ref, x_ref, t_ref, gate_ref, gf_ref, wco_ref, wso_ref, wo_ref,
             dx2_ref, dya_ref, dyb_ref, dp_ref, mb_ref, dob_ref, dyab_ref, dybb_ref, sums_ref):
        @pl.when(pl.program_id(0) == 0)
        def _():
            sums_ref[...] = jnp.zeros((SUB, D), F32)

        y_a = jnp.dot(ya_ref[...], wco_ref[...], preferred_element_type=F32)
        y_b = jnp.dot(yb_ref[...], wso_ref[...], preferred_element_type=F32)
        ga = _sigmoid(pg_ref[:, pl.ds(0, D)].astype(F32))
        gb = _sigmoid(pg_ref[:, pl.ds(D, D)].astype(F32))
        mb = (ga * y_a + gb * y_b).astype(BF16)
        mb_ref[...] = mb
        o = jnp.dot(mb, wo_ref[...], preferred_element_type=F32)
        x2 = x_ref[...] + gate_ref[...] * o
        r2 = lax.rsqrt(_rowmean(x2 * x2) + EPS)
        xh = x2 * r2
        e = xh * gf_ref[...] - t_ref[...]
        dy = e * (1.0 / D)
        dxh = dy * gf_ref[...]
        dx2 = r2 * (dxh - xh * _rowmean(dxh * xh))
        dx2_ref[...] = dx2
        sums_ref[pl.ds(0, 1), :] += jnp.sum(dy * xh, axis=0, keepdims=True)
        sums_ref[pl.ds(1, 1), :] += jnp.sum(dx2 * o, axis=0, keepdims=True)
        sums_ref[pl.ds(2, 1), :] += jnp.sum(e * e, axis=0, keepdims=True) * (0.5 / D)
        dob = (gate_ref[...] * dx2).astype(BF16)
        dob_ref[...] = dob
        dm = _dot_t(dob, wo_ref[...])
        dy_a = (ga * dm).astype(BF16)
        dy_b = (gb * dm).astype(BF16)
        dyab_ref[...] = dy_a
        dybb_ref[...] = dy_b
        dp_ref[:, pl.ds(0, D)] = (dm * y_a * ga * (1.0 - ga)).astype(BF16)
        dp_ref[:, pl.ds(D, D)] = (dm * y_b * gb * (1.0 - gb)).astype(BF16)
        dya_ref[...] = _dot_t(dy_a, wco_ref[...])
        dyb_ref[...] = _dot_t(dy_b, wso_ref[...])

    tile = pl.BlockSpec((tm, D), lambda i: (i, 0))
    wspec = pl.BlockSpec((D, D), lambda i: (0, 0))
    return pl.pallas_call(
        body, name="out_proj", grid=(n_i,),
        in_specs=[pl.BlockSpec((tm, 2 * D), lambda i: (i, 3)), tile, tile, tile, tile, _vec_spec(1), _vec_spec(1),
                  wspec, wspec, wspec],
        out_specs=[tile, tile, tile, pl.BlockSpec((tm, 2 * D), lambda i: (i, 3)), tile, tile, tile, tile,
                   pl.BlockSpec((SUB, D), lambda i: (0, 0))],
        out_shape=[_sds((s_len, D), F32), _sds((s_len, D), F32), _sds((s_len, D), F32), _sds((s_len, N_SEC * D), BF16),
                   _sds((s_len, D), BF16), _sds((s_len, D), BF16), _sds((s_len, D), BF16), _sds((s_len, D), BF16),
                   _sds((SUB, D), F32)],
        compiler_params=_params(("arbitrary",)),
    )(p, ya_in, yb_in, x, target, gate, g_final, w_co, w_so, w_o)


A_STATS_ROWS = 8 + HALO


def _branch_a_bwd(p, y1, dya_in, dp, conv_wb, ln_g, ln_b):
    s_len = p.shape[0]
    tm = min(256, s_len)
    n_i = s_len // tm
    rows = 32
    n_out = rows // SUB

    def tile_of(i):
        return n_i - 1 - i

    def body(p_ref, y1_ref, dya_ref, dp_in, wb_ref, g_ref, b_ref, dp_ref, st_ref, dybuf, acc8, tapacc):
        del dp_in
        i = pl.program_id(0)

        @pl.when(i == 0)
        def _():
            dybuf[pl.ds(tm, HALO), :] = jnp.zeros((HALO, D), F32)
            st_ref[...] = jnp.zeros((A_STATS_ROWS, D), F32)
            acc8[...] = jnp.zeros((3 * PACK, D), F32)
            tapacc[...] = jnp.zeros((CONV_K * SUB, D), F32)

        def norm_bwd(r0):
            y1 = y1_ref[pl.ds(r0, PACK), :]
            mu = _rowmean(y1)
            yc = y1 - mu
            rstd = lax.rsqrt(_rowmean(yc * yc) + EPS)
            n1 = yc * rstd
            l1 = n1 * g_ref[...] + b_ref[...]
            sg = _sigmoid(l1)
            z = p_ref[pl.ds(r0, PACK), pl.ds(2 * D, D)].astype(F32)
            sz = _sigmoid(z)
            dya = dya_ref[pl.ds(r0, PACK), :]
            dp_ref[pl.ds(r0, PACK), pl.ds(2 * D, D)] = (dya * (l1 * sg) * _dsilu(z, sz)).astype(BF16)
            dl1 = dya * (z * sz) * _dsilu(l1, sg)
            acc8[pl.ds(0, PACK), :] += dl1 * n1
            acc8[pl.ds(PACK, PACK), :] += dl1
            dn1 = dl1 * g_ref[...]
            dy1 = rstd * (dn1 - _rowmean(dn1) - n1 * _rowmean(dn1 * n1))
            acc8[pl.ds(2 * PACK, PACK), :] += dy1
            dybuf[pl.ds(r0, PACK), :] = dy1
        _strips(tm, PACK,norm_bwd)

        def conv_bwd(r0):
            for lt in range(D // LANE):
                lanes = pl.ds(lt * LANE, LANE)
                glanes = pl.ds(D + lt * LANE, LANE)
                win = dybuf[pl.ds(r0, rows + HALO), lanes]
                sg16, a16 = [], []
                for h in range(rows // PACK):
                    rr = pl.ds(r0 + h * PACK, PACK)
                    s = _sigmoid(p_ref[rr, glanes].astype(F32))
                    sg16.append(s)
                    a16.append(p_ref[rr, lanes].astype(F32) * s)
                a = [a16[m // 2][(m % 2) * SUB:(m % 2 + 1) * SUB, :] for m in range(n_out)]
                da = [jnp.zeros((SUB, LANE), F32) for _ in range(n_out)]
                for phase in range(SUB):
                    offs = [o for o in range(CONV_K) if o % SUB == phase]
                    q_max = max(o // SUB for o in offs)
                    sh = win[phase:phase + (n_out + q_max) * SUB, :]
                    for o in offs:
                        k, q = CONV_K - 1 - o, o // SUB
                        w = wb_ref[pl.ds(k * SUB, SUB), lanes]
                        part = None
                        for m in range(n_out):
                            s = sh[(m + q) * SUB:(m + q + 1) * SUB, :]
                            da[m] = da[m] + w * s
                            part = a[m] * s if part is None else part + a[m] * s
                        tapacc[pl.ds(k * SUB, SUB), lanes] += part
                for h in range(rows // PACK):
                    rr = pl.ds(r0 + h * PACK, PACK)
                    da16 = jnp.concatenate(da[2 * h:2 * h + 2], axis=0)
                    dp_ref[rr, lanes] = (da16 * sg16[h]).astype(BF16)
                    dp_ref[rr, glanes] = (da16 * a16[h] * (1.0 - sg16[h])).astype(BF16)
        _strips(tm, rows, conv_bwd)

        dybuf[pl.ds(tm, HALO), :] = dybuf[pl.ds(0, HALO), :]

        @pl.when(i == n_i - 1)
        def _():
            for j in range(3):
                st_ref[pl.ds(j, 1), :] = jnp.sum(acc8[pl.ds(j * PACK, PACK), :], axis=0, keepdims=True)
            for k in range(CONV_K):
                st_ref[pl.ds(SUB + k, 1), :] = jnp.sum(tapacc[pl.ds(k * SUB, SUB), :], axis=0, keepdims=True)

    return pl.pallas_call(
        body, name="branch_a_bwd", grid=(n_i,),
        in_specs=[pl.BlockSpec((tm, 3 * D), lambda i: (tile_of(i), 0)),
                  pl.BlockSpec((tm, D), lambda i: (tile_of(i), 0)),
                  pl.BlockSpec((tm, D), lambda i: (tile_of(i), 0)),
                  pl.BlockSpec(memory_space=pl.ANY),
                  pl.BlockSpec((CONV_K * SUB, D), lambda i: (0, 0)), _vec_spec(1), _vec_spec(1)],
        out_specs=[pl.BlockSpec((tm, 3 * D), lambda i: (tile_of(i), 0)),
                   pl.BlockSpec((A_STATS_ROWS, D), lambda i: (0, 0))],
        out_shape=[_sds(dp.shape, BF16), _sds((A_STATS_ROWS, D), F32)],
        scratch_shapes=[pltpu.VMEM((tm + HALO, D), F32), pltpu.VMEM((3 * PACK, D), F32), pltpu.VMEM((CONV_K * SUB, D), F32)],
        input_output_aliases={3: 0},
        compiler_params=_params(("arbitrary",)),
    )(p, y1, dya_in, dp, conv_wb, ln_g, ln_b)


def _branch_b_bwd(p, dyb_in, dp, wt, wtt, bias_full, ln_g, ln_b):
    s_len = p.shape[0]
    tm = min(256, s_len)
    n_i = s_len // tm

    def body(p_ref, dyb_ref, dp_in, wt_ref, wtt_ref, bias_ref, g_ref, b_ref, dp_ref, st_ref, gbt_ref, gw_ref,
             vb, n2buf, rstdbuf, sbuf, dsb, dvbuf, acc8, gb_ref, dgbuf):
        del dp_in
        i = pl.program_id(0)

        @pl.when(i == 0)
        def _():
            st_ref[...] = jnp.zeros((SUB, D), F32)
            gbt_ref[...] = jnp.zeros((CHUNK, LANE), F32)
            gb_ref[...] = jnp.zeros((CHUNK, D), F32)
            gw_ref[...] = jnp.zeros((HEADS, CHUNK, CHUNK), F32)
            acc8[...] = jnp.zeros((2 * PACK, D), F32)

        def norm(r0):
            gv, dgv = _gelu_and_grad(p_ref[pl.ds(r0, PACK), pl.ds(D, D)].astype(F32))
            dgbuf[pl.ds(r0, PACK), :] = dgv
            mu = _rowmean(gv)
            vc = gv - mu
            rstd = lax.rsqrt(_rowmean(vc * vc) + EPS)
            n2 = vc * rstd
            n2buf[pl.ds(r0, PACK), :] = n2
            rstdbuf[pl.ds(r0, PACK), :] = jnp.broadcast_to(rstd, (PACK, LANE))
            vb[pl.ds(r0, PACK), :] = (n2 * g_ref[...] + b_ref[...]).astype(BF16)
        _strips(tm, PACK,norm)

        for ck in range(tm // CHUNK):
            for h in range(HEADS):
                blk = (pl.ds(ck * CHUNK, CHUNK), pl.ds(h * LANE, LANE))
                sbuf[blk] = jnp.dot(wt_ref[h], vb[blk], preferred_element_type=F32) + bias_ref[:, pl.ds(h * LANE, LANE)]

        def gate_bwd(r0):
            pu = p_ref[pl.ds(r0, PACK), pl.ds(0, D)].astype(F32)
            u, du = _gelu_and_grad(pu)
            z = p_ref[pl.ds(r0, PACK), pl.ds(2 * D, D)].astype(F32)
            sg = _sigmoid(z)
            sz = z * sg
            s = sbuf[pl.ds(r0, PACK), :]
            dyb = dyb_ref[pl.ds(r0, PACK), :]
            ds = dyb * u * sz
            dsb[pl.ds(r0, PACK), :] = ds.astype(BF16)
            gb_ref[pl.ds(pl.multiple_of(r0 % CHUNK, PACK), PACK), :] += ds
            dp_ref[pl.ds(r0, PACK), pl.ds(0, D)] = (dyb * s * sz * du).astype(BF16)
            dp_ref[pl.ds(r0, PACK), pl.ds(2 * D, D)] = (dyb * u * s * _dsilu(z, sg)).astype(BF16)
        _strips(tm, PACK,gate_bwd)

        for ck in range(tm // CHUNK):
            for h in range(HEADS):
                blk = (pl.ds(ck * CHUNK, CHUNK), pl.ds(h * LANE, LANE))
                d_s = dsb[blk]
                dvbuf[blk] = jnp.dot(wtt_ref[h], d_s, preferred_element_type=F32)
                gw_ref[h] += _dot_t(d_s, vb[blk])

        def norm_bwd(r0):
            dv = dvbuf[pl.ds(r0, PACK), :]
            n2 = n2buf[pl.ds(r0, PACK), :]
            rstd = rstdbuf[pl.ds(r0, PACK), pl.ds(0, 1)]
            acc8[pl.ds(0, PACK), :] += dv * n2
            acc8[pl.ds(PACK, PACK), :] += dv
            dn2 = dv * g_ref[...]
            dgv = rstd * (dn2 - _rowmean(dn2) - n2 * _rowmean(dn2 * n2))
            dp_ref[pl.ds(r0, PACK), pl.ds(D, D)] = (dgv * dgbuf[pl.ds(r0, PACK), :]).astype(BF16)
        _strips(tm, PACK,norm_bwd)

        @pl.when(i == n_i - 1)
        def _():
            for j in range(2):
                st_ref[pl.ds(j, 1), :] = jnp.sum(acc8[pl.ds(j * PACK, PACK), :], axis=0, keepdims=True)
            row = lax.broadcasted_iota(jnp.int32, (CHUNK, CHUNK), 0)
            col = lax.broadcasted_iota(jnp.int32, (CHUNK, CHUNK), 1)
            for h in range(HEADS):
                gw_ref[h] = jnp.where(row >= col, gw_ref[h], 0.0)
            lane = lax.broadcasted_iota(jnp.int32, (CHUNK, LANE), 1)
            gbt = jnp.zeros((CHUNK, LANE), F32)
            for h in range(HEADS):
                gbt = jnp.where(lane == h, jnp.sum(gb_ref[:, pl.ds(h * LANE, LANE)], axis=1, keepdims=True), gbt)
            gbt_ref[...] = gbt

    wspec = pl.BlockSpec((HEADS, CHUNK, CHUNK), lambda i: (0, 0, 0))
    return pl.pallas_call(
        body, name="branch_b_bwd", grid=(n_i,),
        in_specs=[pl.BlockSpec((tm, 3 * D), lambda i: (i, 1)), pl.BlockSpec((tm, D), lambda i: (i, 0)),
                  pl.BlockSpec(memory_space=pl.ANY), wspec, wspec,
                  pl.BlockSpec((CHUNK, D), lambda i: (0, 0)), _vec_spec(1), _vec_spec(1)],
        out_specs=[pl.BlockSpec((tm, 3 * D), lambda i: (i, 1)), pl.BlockSpec((SUB, D), lambda i: (0, 0)),
                   pl.BlockSpec((CHUNK, LANE), lambda i: (0, 0)), wspec],
        out_shape=[_sds(dp.shape, BF16), _sds((SUB, D), F32), _sds((CHUNK, LANE), F32), _sds((HEADS, CHUNK, CHUNK), F32)],
        scratch_shapes=[pltpu.VMEM((tm, D), BF16), pltpu.VMEM((tm, D), F32), pltpu.VMEM((tm, LANE), F32),
                        pltpu.VMEM((tm, D), F32), pltpu.VMEM((tm, D), BF16), pltpu.VMEM((tm, D), F32),
                        pltpu.VMEM((2 * PACK, D), F32), pltpu.VMEM((CHUNK, D), F32), pltpu.VMEM((tm, D), F32)],
        input_output_aliases={2: 0},
        compiler_params=_params(("arbitrary",)),
    )(p, dyb_in, dp, wt, wtt, bias_full, ln_g, ln_b)


def _in_proj_bwd(dp, wg_in, x, dx2, shift, scale, g_pre):
    del shift
    s_len = x.shape[0]
    tm = min(512, s_len)
    n_i = s_len // tm
    wn = wg_in.shape[2]

    def body(dp_ref, w_ref, x_ref, dx2_ref, sc_ref, g_ref, gx_ref, st_ref, acc, acc8):
        i = pl.program_id(0)

        @pl.when(i == 0)
        def _():
            st_ref[...] = jnp.zeros((SUB, D), F32)
            acc8[...] = jnp.zeros((3 * PACK, D), F32)

        dh = _dot_t(dp_ref[:, pl.ds(0, wn)], w_ref[0])
        for j in range(1, N_CHIP):
            dh = dh + _dot_t(dp_ref[:, pl.ds(j * wn, wn)], w_ref[j])
        acc[...] = dh

        def strip(r0):
            xs = x_ref[pl.ds(r0, PACK), :]
            r = lax.rsqrt(_rowmean(xs * xs) + EPS)
            xn = xs * r
            dhs = acc[pl.ds(r0, PACK), :]
            acc8[pl.ds(0, PACK), :] += dhs
            acc8[pl.ds(PACK, PACK), :] += dhs * (xn * g_ref[...])
            dhp = dhs * (1.0 + sc_ref[...])
            acc8[pl.ds(2 * PACK, PACK), :] += dhp * xn
            dxn = dhp * g_ref[...]
            gx_ref[pl.ds(r0, PACK), :] = dx2_ref[pl.ds(r0, PACK), :] + r * (dxn - xn * _rowmean(dxn * xn))
        _strips(tm, PACK, strip)

        @pl.when(i == n_i - 1)
        def _():
            for k in range(3):
                st_ref[pl.ds(k, 1), :] = jnp.sum(acc8[pl.ds(k * PACK, PACK), :], axis=0, keepdims=True)

    tile = pl.BlockSpec((tm, D), lambda i: (i, 0))
    return pl.pallas_call(
        body, name="in_proj_bwd", grid=(n_i,),
        in_specs=[pl.BlockSpec((tm, N_CHIP * wn), lambda i: (i, 0)),
                  pl.BlockSpec((N_CHIP, D, wn), lambda i: (0, 0, 0), pipeline_mode=pl.Buffered(1)),
                  tile, tile, _vec_spec(1), _vec_spec(1)],
        out_specs=[tile, pl.BlockSpec((SUB, D), lambda i: (0, 0))],
        out_shape=[_sds((s_len, D), F32), _sds((SUB, D), F32)],
        scratch_shapes=[pltpu.VMEM((tm, D), F32), pltpu.VMEM((3 * PACK, D), F32)],
        compiler_params=_params(("arbitrary",)),
    )(dp, wg_in, x, dx2, scale, g_pre)


def _grad_matmul(a, b, name):
    s_len, n = b.shape
    cb = min(2 * D, n)
    tn = 512
    per = cb // tn

    def body(a_ref, b_ref, ob_ref):
        ob_ref[0] = lax.dot_general(a_ref[...], b_ref[...], (((0,), (0,)), ((), ())),
                                    preferred_element_type=F32).astype(BF16)

    return pl.pallas_call(
        body, name=name, grid=(n // tn,),
        in_specs=[pl.BlockSpec((s_len, D), lambda j: (0, 0), pipeline_mode=pl.Buffered(1)),
                  pl.BlockSpec((s_len, tn), lambda j: (0, j))],
        out_specs=pl.BlockSpec((1, D, tn), lambda j: (j // per, 0, j % per)),
        out_shape=_sds((n // cb, D, cb), BF16),
        compiler_params=_params(("arbitrary",)),
    )(a, b)


def _local_step(x, target, shift, scale, gate, g_pre, conv_w_full, conv_b, conv_ln_g, conv_ln_b,
                sgu_ln_g, sgu_ln_b, w_sgu, b_sgu, g_final, wg_in, out_shards):
    conv_wb = jnp.repeat(conv_w_full, SUB, axis=0)
    causal = jnp.tril(jnp.ones((CHUNK, CHUNK), dtype=bool))
    wt = jnp.where(causal[None], w_sgu, 0.0).astype(BF16)
    wtt = jnp.swapaxes(wt, 1, 2)
    bias_full = jnp.repeat(b_sgu.T, LANE, axis=1)

    p, hb, gathered = _in_proj_gather(x, shift, scale, g_pre, wg_in, out_shards)
    w_co, w_so, w_o = (g.reshape(D, D) for g in gathered)
    ya_in, y1 = _branch_a_fwd(p, conv_wb, conv_b, conv_ln_g, conv_ln_b)
    yb_in = _branch_b_fwd(p, wt, bias_full, sgu_ln_g, sgu_ln_b)
    dx2, dya_in, dyb_in, dp, mb, dob, dyab, dybb, sums_o = _out_proj(
        p, ya_in, yb_in, x, target, gate, g_final, w_co, w_so, w_o)
    dp, st_a = _branch_a_bwd(p, y1, dya_in, dp, conv_wb, conv_ln_g, conv_ln_b)
    dp, st_b, gbt, gws = _branch_b_bwd(p, dyb_in, dp, wt, wtt, bias_full, sgu_ln_g, sgu_ln_b)
    grad_x, st_i = _in_proj_bwd(dp, wg_in, x, dx2, shift, scale, g_pre)
    gw_o = _grad_matmul(mb, dob, "grad_w_o")
    gw_co = _grad_matmul(ya_in, dyab, "grad_w_conv_out")
    gw_so = _grad_matmul(yb_in, dybb, "grad_w_sgu_out")
    return dict(
        grad_x=grad_x, loss_cols=sums_o[2:3], g_final=sums_o[0:1], d_gate=sums_o[1:2],
        d_shift=st_i[0:1], d_scale=st_i[1:2], g_pre=st_i[2:3],
        conv_ln_g=st_a[0:1], conv_ln_b=st_a[1:2], conv_b=st_a[2:3], conv_w=st_a[SUB:SUB + CONV_K],
        sgu_ln_g=st_b[0:1], sgu_ln_b=st_b[1:2], b_sgu=gbt[:, :HEADS].T, w_sgu=gws,
        hb=hb, dp=dp, w_o=gw_o, w_conv_out=gw_co, w_sgu_out=gw_so)


ANY_SPEC = pl.BlockSpec(memory_space=pl.ANY)
VMEM_SPEC = pl.BlockSpec(memory_space=pltpu.VMEM)


def _place():
    return lax.axis_index("x"), lax.axis_index("y"), lax.axis_index("c")


def _peer(k):
    x, y, c = _place()
    return (1 - x if k & 4 else x, 1 - y if k & 2 else y, 1 - c if k & 1 else c)


def _dev_of(p):
    return 4 * p[0] + 2 * p[1] + p[2]


def _chip_of(p):
    return 2 * p[0] + p[1]


def _rdma(src, dst, send_sem, recv_sem, to):
    return pltpu.make_async_remote_copy(src_ref=src, dst_ref=dst, send_sem=send_sem, recv_sem=recv_sem,
                                        device_id=to, device_id_type=MESH)


CHIP_PEERS = (2, 4, 6)
ALL_PEERS = tuple(range(1, N_DEV))
SIBLING = 1


def _setup_comm(c8, w_ada_s, b_ada_s, convw_s, shards):
    n_mod = w_ada_s.shape[1]
    rows = SUB * N_DEV
    n = len(shards)

    def body(c8_ref, wada_ref, bada_ref, cw_ref, *refs):
        ins, (call_ref, mod_ref, cwall_ref), outs = refs[:n], refs[n:n + 3], refs[n + 3:2 * n + 3]
        csend, crecv, wsend, wrecv, msend, mrecv = refs[2 * n + 3:2 * n + 9]
        gather_a, gather_b, gather_c = _gather_phases([s.shape[0] for s in shards], ins, outs, refs[2 * n + 9:])
        gather_a()
        me = _place()
        dev, chip = _dev_of(me), _chip_of(me)

        def c_rows(d):
            return call_ref.at[pl.ds(pl.multiple_of(d * SUB, SUB), SUB), :]

        call_ref[pl.ds(pl.multiple_of(dev * SUB, SUB), SUB), :] = c8_ref[...]
        cwall_ref[chip] = cw_ref[...]
        c_out = [_rdma(c8_ref, c_rows(dev), csend.at[k], crecv.at[k], _peer(k)) for k in ALL_PEERS]
        w_out = [_rdma(cw_ref, cwall_ref.at[chip], wsend.at[k], wrecv.at[k], _peer(k)) for k in CHIP_PEERS]
        for cp in c_out + w_out:
            cp.start()
        for k in ALL_PEERS:
            _rdma(c8_ref, c_rows(_dev_of(_peer(k))), csend.at[k], crecv.at[k], _peer(k)).wait_recv()
        part = jnp.dot(call_ref[...].astype(BF16), wada_ref[...].astype(BF16), preferred_element_type=F32) + bada_ref[...]
        mod_ref[chip] = part
        m_out = [_rdma(mod_ref.at[chip], mod_ref.at[chip], msend.at[k], mrecv.at[k], _peer(k)) for k in CHIP_PEERS]
        for cp in m_out:
            cp.start()
        for k in CHIP_PEERS:
            pc = _chip_of(_peer(k))
            _rdma(cw_ref, cwall_ref.at[pc], wsend.at[k], wrecv.at[k], _peer(k)).wait_recv()
            _rdma(mod_ref.at[pc], mod_ref.at[pc], msend.at[k], mrecv.at[k], _peer(k)).wait_recv()
        for cp in c_out + w_out + m_out:
            cp.wait_send()
        gather_b()
        gather_c()

    res = pl.pallas_call(
        body, name="setup_comm",
        in_specs=[VMEM_SPEC] * (4 + n), out_specs=[VMEM_SPEC] * (3 + n),
        out_shape=([_sds((rows, D), F32), _sds((N_CHIP, rows, n_mod), F32), _sds((N_CHIP,) + convw_s.shape, F32)]
                   + [_sds((N_CHIP,) + s.shape, s.dtype) for s in shards]),
        scratch_shapes=([pltpu.SemaphoreType.DMA((N_DEV,))] * 6
                        + [pltpu.SemaphoreType.DMA((n,))] + [pltpu.SemaphoreType.DMA((n, len(CHIP_PEERS)))] * 4),
        compiler_params=_params(),
    )(c8, w_ada_s, b_ada_s, convw_s, *shards)
    return res[0], res[1], res[2], res[3:]


def _gather_weights(shards):
    n = len(shards)

    def body(*refs):
        ins, outs = refs[:n], refs[n:2 * n]
        lsem, isend, irecv, dsend, drecv = refs[2 * n:]
        me = _place()
        chip, c = _chip_of(me), me[2]
        local = [pltpu.make_async_copy(ins[t], outs[t].at[chip], lsem.at[t]) for t in range(n)]
        for cp in local:
            cp.start()

        def half(t, which):
            hr = shards[t].shape[0] // 2
            return pl.ds(pl.multiple_of(which * hr, hr), hr)

        sends = []
        for t in range(n):
            for j, k in enumerate(CHIP_PEERS):
                cp = _rdma(ins[t].at[half(t, c)], outs[t].at[chip, half(t, c)], isend.at[t, j], irecv.at[t, j], _peer(k))
                cp.start()
                sends.append(cp)
        for t in range(n):
            for j, k in enumerate(CHIP_PEERS):
                blk = outs[t].at[_chip_of(_peer(k)), half(t, c)]
                _rdma(blk, blk, isend.at[t, j], irecv.at[t, j], _peer(k)).wait_recv()
                cp = _rdma(blk, blk, dsend.at[t, j], drecv.at[t, j], _peer(SIBLING))
                cp.start()
                sends.append(cp)
        for t in range(n):
            for j, k in enumerate(CHIP_PEERS):
                blk = outs[t].at[_chip_of(_peer(k)), half(t, 1 - c)]
                _rdma(blk, blk, dsend.at[t, j], drecv.at[t, j], _peer(SIBLING)).wait_recv()
        for cp in sends:
            cp.wait_send()
        for cp in local:
            cp.wait()

    return pl.pallas_call(
        body, name="gather_weights",
        in_specs=[VMEM_SPEC] * n, out_specs=[VMEM_SPEC] * n,
        out_shape=[_sds((N_CHIP,) + s.shape, s.dtype) for s in shards],
        scratch_shapes=[pltpu.SemaphoreType.DMA((n,))] + [pltpu.SemaphoreType.DMA((n, len(CHIP_PEERS)))] * 4,
        compiler_params=_params(),
    )(*shards)


def _gather_phases(row_counts, ins, dsts, sems):
    n = len(row_counts)
    lsem, isend, irecv, dsend, drecv = sems
    me = _place()
    chip, c = _chip_of(me), me[2]

    def half(t, which):
        hr = row_counts[t] // 2
        return pl.ds(pl.multiple_of(which * hr, hr), hr)

    def local(t):
        return pltpu.make_async_copy(ins[t], dsts[t].at[chip], lsem.at[t])

    def to_chip(t, j):
        return _rdma(ins[t].at[half(t, c)], dsts[t].at[chip, half(t, c)], isend.at[t, j], irecv.at[t, j], _peer(CHIP_PEERS[j]))

    def landed(t, j, which):
        return dsts[t].at[_chip_of(_peer(CHIP_PEERS[j])), half(t, which)]

    def to_sibling(t, j):
        return _rdma(landed(t, j, c), landed(t, j, c), dsend.at[t, j], drecv.at[t, j], _peer(SIBLING))

    pairs = [(t, j) for t in range(n) for j in range(len(CHIP_PEERS))]

    def phase_a():
        for t in range(n):
            local(t).start()
        for t, j in pairs:
            to_chip(t, j).start()

    def phase_b():
        for t, j in pairs:
            _rdma(landed(t, j, c), landed(t, j, c), isend.at[t, j], irecv.at[t, j], _peer(CHIP_PEERS[j])).wait_recv()
            to_sibling(t, j).start()

    def phase_c():
        for t, j in pairs:
            _rdma(landed(t, j, 1 - c), landed(t, j, 1 - c), dsend.at[t, j], drecv.at[t, j], _peer(SIBLING)).wait_recv()
        for t, j in pairs:
            to_chip(t, j).wait_send()
            to_sibling(t, j).wait_send()
        for t in range(n):
            local(t).wait()

    return phase_a, phase_b, phase_c


def _in_proj_gather(x, shift, scale, g_pre, wg_in, shards):
    s_len = x.shape[0]
    tm = min(256, s_len)
    n_i = s_len // tm
    wn = wg_in.shape[2]
    n = len(shards)

    def body(x_ref, sh_ref, sc_ref, g_ref, w_ref, *refs):
        ins, p_ref, hb_ref, outs = refs[:n], refs[n], refs[n + 1], refs[n + 2:2 * n + 2]
        gath, sems = refs[2 * n + 2:3 * n + 2], refs[3 * n + 2:]
        phases = _gather_phases([s.shape[0] for s in shards], ins, gath, sems)
        i = pl.program_id(0)
        for step, phase in zip((0, n_i // 2, n_i - 1), phases):
            pl.when(i == step)(phase)

        @pl.when(i == n_i - 1)
        def _():
            for t in range(n):
                outs[t][...] = gath[t][...]

        def strip(r0):
            xs = x_ref[pl.ds(r0, PACK), :]
            r = lax.rsqrt(_rowmean(xs * xs) + EPS)
            h = (xs * r) * g_ref[...] * (1.0 + sc_ref[...]) + sh_ref[...]
            hb_ref[pl.ds(r0, PACK), :] = h.astype(BF16)
        _strips(tm, PACK, strip)
        hb = hb_ref[...]
        for j in range(N_CHIP):
            p_ref[:, pl.ds(j * wn, wn)] = jnp.dot(hb, w_ref[j], preferred_element_type=F32).astype(BF16)

    res = pl.pallas_call(
        body, name="in_proj", grid=(n_i,),
        in_specs=[pl.BlockSpec((tm, D), lambda i: (i, 0)), _vec_spec(1), _vec_spec(1), _vec_spec(1),
                  pl.BlockSpec((N_CHIP, D, wn), lambda i: (0, 0, 0), pipeline_mode=pl.Buffered(1))] + [VMEM_SPEC] * n,
        out_specs=[pl.BlockSpec((tm, N_CHIP * wn), lambda i: (i, 0)), pl.BlockSpec((tm, D), lambda i: (i, 0))] + [VMEM_SPEC] * n,
        out_shape=([_sds((s_len, N_SEC * D), BF16), _sds((s_len, D), BF16)]
                   + [_sds((N_CHIP,) + s.shape, s.dtype) for s in shards]),
        scratch_shapes=([pltpu.VMEM((N_CHIP,) + s.shape, s.dtype) for s in shards]
                        + [pltpu.SemaphoreType.DMA((n,))] + [pltpu.SemaphoreType.DMA((n, len(CHIP_PEERS)))] * 4),
        compiler_params=_params(("arbitrary",)),
    )(x, shift, scale, g_pre, wg_in, *shards)
    return res[0], res[1], res[2:]


def _reduce_scatter(grads, name):
    n = len(grads)
    shapes = [g.shape[2:] for g in grads]

    def body(*refs):
        ins, outs = refs[:n], refs[n:2 * n]
        pbufs, rbufs, accs = refs[2 * n:3 * n], refs[3 * n:4 * n], refs[4 * n:5 * n]
        psend, precv, csend, crecv, fsend, frecv = refs[5 * n:]
        me = _place()
        chip, c = _chip_of(me), me[2]
        sib = _peer(SIBLING)

        def to_sibling(t, d):
            return _rdma(ins[t].at[d, 1 - c], pbufs[t].at[d], psend.at[t, d], precv.at[t, d], sib)

        sends = []
        for t in range(n):
            for d in range(N_CHIP):
                cp = to_sibling(t, d)
                cp.start()
                sends.append(cp)
        for j in (1, 2, 3, 0):
            d = jnp.bitwise_xor(chip, j)
            for t in range(n):
                to_sibling(t, d).wait_recv()

                def pair_sum(r0, t=t, d=d, j=j):
                    rows = pl.ds(r0, PACK)
                    s = ins[t][d, c, rows, :].astype(F32) + pbufs[t][d, rows, :].astype(F32)
                    if j == 0:
                        accs[t][rows, :] = s
                    else:
                        pbufs[t][d, rows, :] = s.astype(BF16)
                _strips(shapes[t][0], PACK, pair_sum)
                if j:
                    cp = _rdma(pbufs[t].at[d], rbufs[t].at[j - 1], csend.at[t, j], crecv.at[t, j], _peer(2 * j))
                    cp.start()
                    sends.append(cp)
        for t in range(n):
            for j in (1, 2, 3):
                blk = rbufs[t].at[j - 1]
                _rdma(blk, blk, csend.at[t, j], crecv.at[t, j], _peer(2 * j)).wait_recv()

            def total(r0, t=t):
                rows = pl.ds(r0, PACK)
                s = accs[t][rows, :] + rbufs[t][0, rows, :].astype(F32)
                s = s + rbufs[t][1, rows, :].astype(F32)
                outs[t][c, rows, :] = s + rbufs[t][2, rows, :].astype(F32)
            _strips(shapes[t][0], PACK, total)
            cp = _rdma(outs[t].at[c], outs[t].at[c], fsend.at[t], frecv.at[t], sib)
            cp.start()
            sends.append(cp)
        for t in range(n):
            blk = outs[t].at[1 - c]
            _rdma(blk, blk, fsend.at[t], frecv.at[t], sib).wait_recv()
        for cp in sends:
            cp.wait_send()

    return pl.pallas_call(
        body, name=name,
        in_specs=[VMEM_SPEC] * n, out_specs=[VMEM_SPEC] * n,
        out_shape=[_sds((2,) + s, F32) for s in shapes],
        scratch_shapes=([pltpu.VMEM((N_CHIP,) + s, BF16) for s in shapes] + [pltpu.VMEM((N_CHIP - 1,) + s, BF16) for s in shapes]
                        + [pltpu.VMEM(s, F32) for s in shapes]
                        + [pltpu.SemaphoreType.DMA((n, N_CHIP))] * 4 + [pltpu.SemaphoreType.DMA((n,))] * 2),
        compiler_params=_params(),
    )(*grads)


def _reduce_phases(shapes, ins, outs, pbufs, rbufs, accs, sems):
    n = len(shapes)
    psend, precv, csend, crecv, fsend, frecv = sems
    me = _place()
    chip, c = _chip_of(me), me[2]
    sib = _peer(SIBLING)

    def to_sibling(t, d):
        return _rdma(ins[t].at[d, 1 - c], pbufs[t].at[d], psend.at[t, d], precv.at[t, d], sib)

    def to_chip(t, j):
        return _rdma(pbufs[t].at[jnp.bitwise_xor(chip, j)], rbufs[t].at[j - 1], csend.at[t, j], crecv.at[t, j], _peer(2 * j))

    def finished(t):
        return _rdma(outs[t].at[c], outs[t].at[c], fsend.at[t], frecv.at[t], sib)

    def phase_a():
        for t in range(n):
            for d in range(N_CHIP):
                to_sibling(t, d).start()

    def phase_b():
        for j in (1, 2, 3, 0):
            d = jnp.bitwise_xor(chip, j)
            for t in range(n):
                to_sibling(t, d).wait_recv()

                def pair_sum(r0, t=t, d=d, j=j):
                    rows = pl.ds(r0, PACK)
                    s = ins[t][d, c, rows, :].astype(F32) + pbufs[t][d, rows, :].astype(F32)
                    if j == 0:
                        accs[t][rows, :] = s
                    else:
                        pbufs[t][d, rows, :] = s.astype(BF16)
                _strips(shapes[t][0], PACK, pair_sum)
                if j:
                    to_chip(t, j).start()

    def phase_c():
        for t in range(n):
            for j in (1, 2, 3):
                blk = rbufs[t].at[j - 1]
                _rdma(blk, blk, csend.at[t, j], crecv.at[t, j], _peer(2 * j)).wait_recv()

            def total(r0, t=t):
                rows = pl.ds(r0, PACK)
                s = accs[t][rows, :] + rbufs[t][0, rows, :].astype(F32)
                s = s + rbufs[t][1, rows, :].astype(F32)
                outs[t][c, rows, :] = s + rbufs[t][2, rows, :].astype(F32)
            _strips(shapes[t][0], PACK, total)
            finished(t).start()

    def phase_d():
        for t in range(n):
            blk = outs[t].at[1 - c]
            _rdma(blk, blk, fsend.at[t], frecv.at[t], sib).wait_recv()
        for t in range(n):
            for d in range(N_CHIP):
                to_sibling(t, d).wait_send()
            for j in (1, 2, 3):
                to_chip(t, j).wait_send()
            finished(t).wait_send()

    return phase_a, phase_b, phase_c, phase_d


def _sum_small_phases(ins, outs, pbufs, buf4s, sems):
    n = len(ins)
    psend, precv, send, recv = sems
    chip = _chip_of(_place())

    def swap(t):
        return _rdma(ins[t], pbufs[t], psend.at[t], precv.at[t], _peer(SIBLING))

    def to_chip(t, k):
        return _rdma(buf4s[t].at[chip], buf4s[t].at[chip], send.at[t, k], recv.at[t, k], _peer(k))

    def phase_a():
        for t in range(n):
            swap(t).start()

    def phase_b():
        for t in range(n):
            swap(t).wait()
            buf4s[t][chip] = ins[t][...] + pbufs[t][...]
            for k in CHIP_PEERS:
                to_chip(t, k).start()

    def phase_c():
        for t in range(n):
            for k in CHIP_PEERS:
                blk = buf4s[t].at[_chip_of(_peer(k))]
                _rdma(blk, blk, send.at[t, k], recv.at[t, k], _peer(k)).wait_recv()
            outs[t][...] = (buf4s[t][0] + buf4s[t][1]) + (buf4s[t][2] + buf4s[t][3])

    def phase_d():
        for t in range(n):
            for k in CHIP_PEERS:
                to_chip(t, k).wait_send()

    return phase_a, phase_b, phase_c, phase_d


def _sum_small_scratch(blobs):
    n = len(blobs)
    return ([pltpu.VMEM(b.shape, F32) for b in blobs] + [pltpu.VMEM((N_CHIP,) + b.shape, F32) for b in blobs]
            + [pltpu.SemaphoreType.DMA((n,))] * 2 + [pltpu.SemaphoreType.DMA((n, N_DEV))] * 2)


def _reduce_scratch(shapes):
    n = len(shapes)
    return ([pltpu.VMEM((N_CHIP,) + s, BF16) for s in shapes] + [pltpu.VMEM((N_CHIP - 1,) + s, BF16) for s in shapes]
            + [pltpu.VMEM(s, F32) for s in shapes]
            + [pltpu.SemaphoreType.DMA((n, N_CHIP))] * 4 + [pltpu.SemaphoreType.DMA((n,))] * 2)


def _grad_matmul_reduce(a, b, name, grads, blobs):
    s_len, n_cols = b.shape
    cb = min(2 * D, n_cols)
    tn = 512
    per = cb // tn
    steps = n_cols // tn
    n, nb = len(grads), len(blobs)
    shapes = [g.shape[2:] for g in grads]
    n_red = len(_reduce_scratch(shapes))

    def body(a_ref, b_ref, *refs):
        ins, bins = refs[:n], refs[n:n + nb]
        ob_ref, outs, bouts = refs[n + nb], refs[n + nb + 1:2 * n + nb + 1], refs[2 * n + nb + 1:2 * (n + nb) + 1]
        scratch = refs[2 * (n + nb) + 1:]
        fulls, red, small = scratch[:n], scratch[n:n + n_red], scratch[n + n_red:]
        phases = _reduce_phases(shapes, ins, fulls, red[:n], red[n:2 * n], red[2 * n:3 * n], red[3 * n:])
        small_phases = _sum_small_phases(bins, bouts, small[:nb], small[nb:2 * nb], small[2 * nb:])
        j = pl.program_id(0)
        for step, phase in zip((0, 2, steps - 2, steps - 1), phases):
            pl.when(j == step)(phase)
        for step, phase in zip((1, 3, steps - 2, steps - 1), small_phases):
            pl.when(j == step)(phase)

        @pl.when(j == steps - 1)
        def _():
            for t in range(n):
                outs[t][...] = fulls[t][...]
        ob_ref[0] = lax.dot_general(a_ref[...], b_ref[...], (((0,), (0,)), ((), ())),
                                    preferred_element_type=F32).astype(BF16)

    res = pl.pallas_call(
        body, name=name, grid=(steps,),
        in_specs=[pl.BlockSpec((s_len, D), lambda j: (0, 0), pipeline_mode=pl.Buffered(1)),
                  pl.BlockSpec((s_len, tn), lambda j: (0, j))] + [VMEM_SPEC] * (n + nb),
        out_specs=[pl.BlockSpec((1, D, tn), lambda j: (j // per, 0, j % per))] + [VMEM_SPEC] * (n + nb),
        out_shape=([_sds((n_cols // cb, D, cb), BF16)] + [_sds((2,) + s, F32) for s in shapes]
                   + [_sds(bl.shape, F32) for bl in blobs]),
        scratch_shapes=[pltpu.VMEM((2,) + s, F32) for s in shapes] + _reduce_scratch(shapes) + _sum_small_scratch(blobs),
        compiler_params=_params(("arbitrary",)),
    )(a, b, *grads, *blobs)
    return res[0], res[1:1 + n], res[1 + n:]


def _scatter_grads(grads):
    n = len(grads)

    def body(*refs):
        ins, outs = refs[:n], refs[n:2 * n]
        lsem, send, recv = refs[2 * n:]
        me = _place()
        dev, chip, c = _dev_of(me), _chip_of(me), me[2]
        local = [pltpu.make_async_copy(ins[t].at[chip, c], outs[t].at[dev], lsem.at[t]) for t in range(n)]
        for cp in local:
            cp.start()
        sends = []
        for t in range(n):
            for k in ALL_PEERS:
                to = _peer(k)
                cp = _rdma(ins[t].at[_chip_of(to), to[2]], outs[t].at[dev], send.at[t, k], recv.at[t, k], to)
                cp.start()
                sends.append(cp)
        for t in range(n):
            for k in ALL_PEERS:
                blk = outs[t].at[_dev_of(_peer(k))]
                _rdma(blk, blk, send.at[t, k], recv.at[t, k], _peer(k)).wait_recv()
        for cp in sends:
            cp.wait_send()
        for cp in local:
            cp.wait()

    return pl.pallas_call(
        body, name="scatter_grads",
        in_specs=[ANY_SPEC] * n, out_specs=[ANY_SPEC] * n,
        out_shape=[_sds((N_DEV,) + g.shape[2:], g.dtype) for g in grads],
        scratch_shapes=[pltpu.SemaphoreType.DMA((n,))] + [pltpu.SemaphoreType.DMA((n, N_DEV))] * 2,
        compiler_params=_params(),
    )(*grads)


def _sum_devices(parts, name):
    _, r, cols = parts.shape
    tr = min(r, 128)

    def body(in_ref, o_ref):
        acc = in_ref[0].astype(F32)
        for d in range(1, N_DEV):
            acc = acc + in_ref[d].astype(F32)
        o_ref[...] = acc

    return pl.pallas_call(
        body, name=name, grid=(r // tr,),
        in_specs=[pl.BlockSpec((N_DEV, tr, cols), lambda i: (0, i, 0))],
        out_specs=pl.BlockSpec((tr, cols), lambda i: (i, 0)),
        out_shape=_sds((r, cols), F32),
        compiler_params=_params(("arbitrary",)),
    )(parts)


def _share_halves(reds):
    n = len(reds)

    def body(*refs):
        ins, outs = refs[:n], refs[n:2 * n]
        lsem, send, recv = refs[2 * n:]
        me = _place()
        c = me[2]
        local = [pltpu.make_async_copy(ins[t], outs[t].at[c], lsem.at[t]) for t in range(n)]
        sends = [_rdma(ins[t], outs[t].at[c], send.at[t], recv.at[t], _peer(SIBLING)) for t in range(n)]
        for cp in local + sends:
            cp.start()
        for t in range(n):
            _rdma(ins[t], outs[t].at[1 - c], send.at[t], recv.at[t], _peer(SIBLING)).wait_recv()
        for cp in sends:
            cp.wait_send()
        for cp in local:
            cp.wait()

    return pl.pallas_call(
        body, name="share_halves",
        in_specs=[VMEM_SPEC] * n, out_specs=[VMEM_SPEC] * n,
        out_shape=[_sds((2,) + r.shape, r.dtype) for r in reds],
        scratch_shapes=[pltpu.SemaphoreType.DMA((n,))] * 3,
        compiler_params=_params(),
    )(*reds)


def _sum_small(blobs):
    n = len(blobs)

    def body(*refs):
        ins, outs = refs[:n], refs[n:2 * n]
        pbufs, buf4s = refs[2 * n:3 * n], refs[3 * n:4 * n]
        psend, precv, send, recv = refs[4 * n:]
        me = _place()
        chip = _chip_of(me)
        pairs = [_rdma(ins[t], pbufs[t], psend.at[t], precv.at[t], _peer(SIBLING)) for t in range(n)]
        for cp in pairs:
            cp.start()
        out = []
        for t in range(n):
            pairs[t].wait()
            buf4s[t][chip] = ins[t][...] + pbufs[t][...]
            for k in CHIP_PEERS:
                cp = _rdma(buf4s[t].at[chip], buf4s[t].at[chip], send.at[t, k], recv.at[t, k], _peer(k))
                cp.start()
                out.append(cp)
        for t in range(n):
            for k in CHIP_PEERS:
                blk = buf4s[t].at[_chip_of(_peer(k))]
                _rdma(blk, blk, send.at[t, k], recv.at[t, k], _peer(k)).wait_recv()
            outs[t][...] = (buf4s[t][0] + buf4s[t][1]) + (buf4s[t][2] + buf4s[t][3])
        for cp in out:
            cp.wait_send()

    return pl.pallas_call(
        body, name="sum_small",
        in_specs=[VMEM_SPEC] * n, out_specs=[VMEM_SPEC] * n, out_shape=[_sds(b.shape, F32) for b in blobs],
        scratch_shapes=([pltpu.VMEM(b.shape, F32) for b in blobs] + [pltpu.VMEM((N_CHIP,) + b.shape, F32) for b in blobs]
                        + [pltpu.SemaphoreType.DMA((n,))] * 2 + [pltpu.SemaphoreType.DMA((n, N_DEV))] * 2),
        compiler_params=_params(),
    )(*blobs)


def _adamw_math(w, g, m, v):
    m = ADAM_B1 * m + (1.0 - ADAM_B1) * g
    v = ADAM_B2 * v + (1.0 - ADAM_B2) * (g * g)
    m_hat = m / (1.0 - ADAM_B1 ** ADAM_STEP)
    v_hat = v / (1.0 - ADAM_B2 ** ADAM_STEP)
    delta = -ADAM_LR * (m_hat / (jnp.sqrt(v_hat) + ADAM_EPS) + ADAM_WD * w)
    return delta, m, v


def _row_tile(r, cols):
    if r * cols * 4 <= 2 ** 20:
        return r
    return next(t for t in (512, 256, 128, 64, 32, 16, 8) if r % t == 0 and t * cols * 4 <= 2 ** 20)


def _adamw(w, g, m, v, name):
    r, cols = w.shape
    tr = _row_tile(r, cols)

    def body(w_ref, g_ref, m_ref, v_ref, d_ref, nm_ref, nv_ref):
        d_ref[...], nm_ref[...], nv_ref[...] = _adamw_math(w_ref[...], g_ref[...], m_ref[...], v_ref[...])

    spec = pl.BlockSpec((tr, cols), lambda i: (i, 0))
    return pl.pallas_call(
        body, name=name, grid=(r // tr,), in_specs=[spec] * 4, out_specs=[spec] * 3,
        out_shape=[_sds((r, cols), F32)] * 3, compiler_params=_params(("arbitrary",)),
    )(w, g, m, v)


def _adamw_ada(w, ct, dm, m, v):
    r, cols = w.shape
    tr = _row_tile(r, cols)

    def body(w_ref, ct_ref, dm_ref, m_ref, v_ref, g_ref, d_ref, nm_ref, nv_ref):
        g = jnp.dot(ct_ref[...], dm_ref[...], preferred_element_type=F32)
        g_ref[...] = g
        d_ref[...], nm_ref[...], nv_ref[...] = _adamw_math(w_ref[...], g, m_ref[...], v_ref[...])

    spec = pl.BlockSpec((tr, cols), lambda i: (i, 0))
    return pl.pallas_call(
        body, name="adamw_ada", grid=(r // tr,),
        in_specs=[spec, pl.BlockSpec((tr, LANE), lambda i: (i, 0)), pl.BlockSpec((LANE, cols), lambda i: (0, 0)), spec, spec],
        out_specs=[spec] * 4, out_shape=[_sds((r, cols), F32)] * 4, compiler_params=_params(("arbitrary",)),
    )(w, ct, dm, m, v)


BLOB_VEC, BLOB_BSGU, BLOB_CONV, BLOB_ADA, BLOB_DMOD, BLOB_LOSS, BLOB_ROWS = 0, 8, 16, 48, 56, 80, 88
N_VEC = 7


def _adamw_small(tot, g_w_sgu, g_conv, params):
    n = len(params)

    def body(*refs):
        tot_ref, gws_ref, gconv_ref = refs[:3]
        wmv = refs[3:3 + 3 * n]
        outs = refs[3 + 3 * n:]
        grads = [tot_ref[pl.ds(BLOB_VEC + i, 1), :] for i in range(N_VEC)]
        grads += [tot_ref[pl.ds(BLOB_BSGU, HEADS), pl.ds(0, CHUNK)], gconv_ref[...], gws_ref[...], tot_ref[pl.ds(BLOB_ADA, 3), :]]
        for i, g in enumerate(grads):
            w_ref, m_ref, v_ref = wmv[3 * i:3 * i + 3]
            d, nm, nv = _adamw_math(w_ref[...], g, m_ref[...], v_ref[...])
            outs[4 * i][...] = g
            outs[4 * i + 1][...] = d
            outs[4 * i + 2][...] = nm
            outs[4 * i + 3][...] = nv

    flat = [a for wmv in params for a in wmv]
    return pl.pallas_call(
        body, name="adamw_small",
        in_specs=[VMEM_SPEC] * (3 + len(flat)), out_specs=[VMEM_SPEC] * (4 * n),
        out_shape=[_sds(wmv[0].shape, F32) for wmv in params for _ in range(4)],
        compiler_params=_params(),
    )(tot, g_w_sgu, g_conv, *flat)


def _set_rows(buf, row, val):
    return lax.dynamic_update_slice(buf, val.astype(F32), (row, 0))


def kernel(x, c, w_ada, b_ada, g_pre, w_in, conv_w, conv_b, conv_ln_g, conv_ln_b, w_conv_out, sgu_ln_g, sgu_ln_b, w_sgu, b_sgu, w_sgu_out, w_o, g_final, loss_target, m_w_ada, m_b_ada, m_g_pre, m_w_in, m_conv_w, m_conv_b, m_conv_ln_g, m_conv_ln_b, m_w_conv_out, m_sgu_ln_g, m_sgu_ln_b, m_w_sgu, m_b_sgu, m_w_sgu_out, m_w_o, m_g_final, v_w_ada, v_b_ada, v_g_pre, v_w_in, v_conv_w, v_conv_b, v_conv_ln_g, v_conv_ln_b, v_w_conv_out, v_sgu_ln_g, v_sgu_ln_b, v_w_sgu, v_b_sgu, v_w_sgu_out, v_w_o, v_g_final):
    me = _place()
    dev, chip = _dev_of(me), _chip_of(me)
    n_ada = w_ada.shape[2]
    conv_cols = conv_w.shape[2]

    b_ada_s = lax.dynamic_slice(b_ada, (0, chip * n_ada), (1, n_ada))
    c_all, mod_all, cw_all, (wg_in,) = _setup_comm(
        jnp.broadcast_to(c, (SUB, D)), w_ada[0], b_ada_s, jnp.pad(conv_w[0], ((0, HALO - CONV_K), (0, 0))),
        [w_in[0].astype(BF16)])
    mod = lax.dynamic_slice(mod_all, (0, dev * SUB, 0), (N_CHIP, 1, n_ada)).reshape(1, 3 * D)
    shift, scale, gate = mod[:, :D], mod[:, D:2 * D], mod[:, 2 * D:]
    conv_w_full = jnp.swapaxes(cw_all, 0, 1).reshape(HALO, D)[:CONV_K]

    loc = _local_step(x[0], loss_target[0], shift, scale, gate, g_pre, conv_w_full, conv_b, conv_ln_g, conv_ln_b,
                      sgu_ln_g, sgu_ln_b, w_sgu[0], b_sgu[0], g_final.reshape(1, D), wg_in,
                      [w_conv_out[0].astype(BF16), w_sgu_out[0].astype(BF16), w_o[0].astype(BF16)])

    d_mod = jnp.concatenate([loc["d_shift"], loc["d_scale"], loc["d_gate"]], axis=0)
    blob = jnp.zeros((BLOB_ROWS, D), F32)
    for i, name in enumerate(["g_pre", "conv_b", "conv_ln_g", "conv_ln_b", "sgu_ln_g", "sgu_ln_b", "g_final"]):
        blob = _set_rows(blob, BLOB_VEC + i, loc[name])
    blob = _set_rows(blob, BLOB_BSGU, loc["b_sgu"])
    blob = _set_rows(blob, BLOB_CONV, loc["conv_w"])
    blob = _set_rows(blob, BLOB_ADA, d_mod)
    blob = lax.dynamic_update_slice(blob, d_mod, (BLOB_DMOD + 3 * dev, 0))
    blob = _set_rows(blob, BLOB_LOSS, loc["loss_cols"])

    big = ["w_in", "w_conv_out", "w_sgu_out", "w_o"]
    contrib_out = [loc[name].reshape(N_CHIP, 2, D // (2 * N_CHIP), D) for name in big[1:]]
    gw_in, full_out, (tot, g_w_sgu) = _grad_matmul_reduce(
        loc["hb"], loc["dp"], "grad_w_in", contrib_out, [blob, loc["w_sgu"].reshape(HEADS * CHUNK, CHUNK)])
    full_in = _reduce_scatter([gw_in.reshape(N_CHIP, 2, D // 2, gw_in.shape[2])], "reduce_w_in")
    g_big = {name: f.reshape(2 * f.shape[1], f.shape[2]) for name, f in zip(big, list(full_in) + list(full_out))}

    loss = jnp.sum(tot[BLOB_LOSS])
    g_conv_s = lax.dynamic_slice(tot, (BLOB_CONV, chip * conv_cols), (CONV_K, conv_cols))
    d_mod_all = tot[BLOB_DMOD:BLOB_DMOD + 3 * N_DEV].reshape(N_DEV, 3 * D)

    ct = jnp.pad(c_all[::SUB].T, ((0, 0), (0, LANE - N_DEV))).astype(BF16)
    dm = jnp.pad(lax.dynamic_slice(d_mod_all, (0, chip * n_ada), (N_DEV, n_ada)), ((0, LANE - N_DEV), (0, 0))).astype(BF16)
    g_ada, d_ada, nm_ada, nv_ada = _adamw_ada(w_ada[0], ct, dm, m_w_ada[0], v_w_ada[0])

    upd = {}
    for name, w, m, v in [("w_in", w_in, m_w_in, v_w_in), ("w_conv_out", w_conv_out, m_w_conv_out, v_w_conv_out),
                          ("w_sgu_out", w_sgu_out, m_w_sgu_out, v_w_sgu_out), ("w_o", w_o, m_w_o, v_w_o)]:
        upd[name] = _adamw(w[0], g_big[name], m[0], v[0], "adamw_" + name)

    def wmv(w, m, v, shape):
        return tuple(a.reshape(shape) for a in (w, m, v))

    small_params = [wmv(w, m, v, (1, D)) for w, m, v in [
        (g_pre, m_g_pre, v_g_pre), (conv_b, m_conv_b, v_conv_b), (conv_ln_g, m_conv_ln_g, v_conv_ln_g),
        (conv_ln_b, m_conv_ln_b, v_conv_ln_b), (sgu_ln_g, m_sgu_ln_g, v_sgu_ln_g), (sgu_ln_b, m_sgu_ln_b, v_sgu_ln_b),
        (g_final, m_g_final, v_g_final)]]
    small_params += [wmv(b_sgu, m_b_sgu, v_b_sgu, (HEADS, CHUNK)), wmv(conv_w, m_conv_w, v_conv_w, (CONV_K, conv_cols)),
                     wmv(w_sgu, m_w_sgu, v_w_sgu, (HEADS * CHUNK, CHUNK)), wmv(b_ada, m_b_ada, v_b_ada, (3, D))]
    small_out = _adamw_small(tot, g_w_sgu, g_conv_s, small_params)

    def leaves(kind):
        vecs = [small_out[4 * i + kind] for i in range(N_VEC)]
        o_b_sgu, o_conv, o_w_sgu, o_b_ada = (small_out[4 * (N_VEC + i) + kind] for i in range(4))
        ada = (g_ada, d_ada, nm_ada, nv_ada)[kind]
        def bigk(name):
            return (g_big[name] if kind == 0 else upd[name][kind - 1])[None]
        return [ada[None], o_b_ada.reshape(1, 3 * D), vecs[0], bigk("w_in"), o_conv[None], vecs[1], vecs[2], vecs[3],
                bigk("w_conv_out"), vecs[4], vecs[5], o_w_sgu.reshape(1, HEADS, CHUNK, CHUNK), o_b_sgu[None],
                bigk("w_sgu_out"), bigk("w_o"), vecs[6].reshape(D)]

    return (loss, loc["grad_x"][None], *leaves(0), *leaves(1), *leaves(2), *leaves(3))
```

```python
import functools

import jax
import jax.numpy as jnp
from jax import lax
from jax.experimental import pallas as pl
from jax.experimental.pallas import tpu as pltpu

F32 = jnp.float32
BF16 = jnp.bfloat16
MESH = pl.DeviceIdType.MESH

D = 1024
N_SEC = 8
N_CHIP = 4
N_DEV = 8
EPS = 1e-6
CONV_K = 31
HALO = 32
CHUNK = 128
HEADS = 8
LANE = 128
SUB = 8
PACK = 16
VMEM_LIMIT = 56 * 1024 * 1024

ADAM_LR, ADAM_B1, ADAM_B2, ADAM_EPS, ADAM_WD, ADAM_STEP = 0.001, 0.9, 0.999, 1e-08, 0.01, 10

_SQRT_HALF = 0.7071067811865476
_INV_SQRT_2PI = 0.3989422804014327


def _sds(shape, dtype):
    return jax.ShapeDtypeStruct(shape, dtype)


def _params(sem=None):
    if sem is None:
        return pltpu.CompilerParams(vmem_limit_bytes=VMEM_LIMIT)
    return pltpu.CompilerParams(dimension_semantics=sem, vmem_limit_bytes=VMEM_LIMIT)


def _strips(n_rows, rows, fn):
    def step(s, carry):
        fn(pl.multiple_of(s * rows, rows))
        return carry
    lax.fori_loop(0, n_rows // rows, step, 0)


def _sigmoid(v):
    return 1.0 / (1.0 + jnp.exp(-v))


def _gelu(v):
    return 0.5 * v * (1.0 + lax.erf(v * _SQRT_HALF))


def _gelu_and_grad(v):
    cdf = 0.5 * (1.0 + lax.erf(v * _SQRT_HALF))
    return v * cdf, cdf + v * jnp.exp(-0.5 * v * v) * _INV_SQRT_2PI


def _dsilu(v, sg):
    return sg * (1.0 + v * (1.0 - sg))


def _rowmean(v):
    return jnp.mean(v, axis=-1, keepdims=True)


def _vec_spec(grid_rank):
    zeros = (0, 0)
    if grid_rank == 1:
        return pl.BlockSpec((1, D), lambda i: zeros)
    return pl.BlockSpec((1, D), lambda i, j: zeros)


def _conv_taps(win_ref, r0, lt, weight_of_offset, rows):
    lanes = pl.ds(lt * LANE, LANE)
    win = win_ref[pl.ds(r0, rows + HALO), lanes]
    n_out = rows // SUB
    acc = [jnp.zeros((SUB, LANE), F32) for _ in range(n_out)]
    for phase in range(SUB):
        offs = [o for o in weight_of_offset if o % SUB == phase]
        if not offs:
            continue
        q_max = max(o // SUB for o in offs)
        span = (n_out + q_max) * SUB
        sh = win[phase:phase + span, :]
        for o in offs:
            q = o // SUB
            w = weight_of_offset[o](lanes)
            for m in range(n_out):
                acc[m] = acc[m] + w * sh[(m + q) * SUB:(m + q + 1) * SUB, :]
    return acc


def _branch_a_fwd(p, conv_wb, conv_b, ln_g, ln_b):
    s_len = p.shape[0]
    tm = min(256, s_len)
    n_i = s_len // tm
    rows = 32

    def body(p_ref, wb_ref, cb_ref, g_ref, b_ref, ya_ref, y1_ref, abuf):
        @pl.when(pl.program_id(0) == 0)
        def _():
            abuf[pl.ds(0, HALO), :] = jnp.zeros((HALO, D), F32)

        def glu(r0):
            val = p_ref[pl.ds(r0, PACK), pl.ds(0, D)].astype(F32)
            gl = p_ref[pl.ds(r0, PACK), pl.ds(D, D)].astype(F32)
            abuf[pl.ds(HALO + r0, PACK), :] = val * _sigmoid(gl)
        _strips(tm, PACK,glu)

        taps = {HALO - (CONV_K - 1) + k: (lambda lanes, k=k: wb_ref[pl.ds(k * SUB, SUB), lanes]) for k in range(CONV_K)}

        def conv(r0):
            for lt in range(D // LANE):
                acc = _conv_taps(abuf, r0, lt, taps, rows)
                cb = cb_ref[:, pl.ds(lt * LANE, LANE)]
                for m, v in enumerate(acc):
                    y1_ref[pl.ds(r0 + m * SUB, SUB), pl.ds(lt * LANE, LANE)] = v + cb
        _strips(tm, rows, conv)

        def norm(r0):
            y1 = y1_ref[pl.ds(r0, PACK), :]
            mu = _rowmean(y1)
            yc = y1 - mu
            rstd = lax.rsqrt(_rowmean(yc * yc) + EPS)
            l1 = (yc * rstd) * g_ref[...] + b_ref[...]
            z = p_ref[pl.ds(r0, PACK), pl.ds(2 * D, D)].astype(F32)
            ya_ref[pl.ds(r0, PACK), :] = ((l1 * _sigmoid(l1)) * (z * _sigmoid(z))).astype(BF16)
        _strips(tm, PACK,norm)

        abuf[pl.ds(0, HALO), :] = abuf[pl.ds(tm, HALO), :]

    return pl.pallas_call(
        body, name="branch_a_fwd", grid=(n_i,),
        in_specs=[pl.BlockSpec((tm, 3 * D), lambda i: (i, 0)),
                  pl.BlockSpec((CONV_K * SUB, D), lambda i: (0, 0)), _vec_spec(1), _vec_spec(1), _vec_spec(1)],
        out_specs=[pl.BlockSpec((tm, D), lambda i: (i, 0)), pl.BlockSpec((tm, D), lambda i: (i, 0))],
        out_shape=[_sds((s_len, D), BF16), _sds((s_len, D), F32)],
        scratch_shapes=[pltpu.VMEM((tm + HALO, D), F32)],
        compiler_params=_params(("arbitrary",)),
    )(p, conv_wb, conv_b, ln_g, ln_b)


def _branch_b_fwd(p, wt, bias_full, ln_g, ln_b):
    s_len = p.shape[0]
    tm = min(256, s_len)
    n_i = s_len // tm

    def body(p_ref, wt_ref, bias_ref, g_ref, b_ref, yb_ref, vb, sbuf):
        def norm(r0):
            gv = _gelu(p_ref[pl.ds(r0, PACK), pl.ds(D, D)].astype(F32))
            mu = _rowmean(gv)
            vc = gv - mu
            rstd = lax.rsqrt(_rowmean(vc * vc) + EPS)
            vb[pl.ds(r0, PACK), :] = ((vc * rstd) * g_ref[...] + b_ref[...]).astype(BF16)
        _strips(tm, PACK,norm)

        for ck in range(tm // CHUNK):
            for h in range(HEADS):
                blk = (pl.ds(ck * CHUNK, CHUNK), pl.ds(h * LANE, LANE))
                sbuf[blk] = jnp.dot(wt_ref[h], vb[blk], preferred_element_type=F32) + bias_ref[:, pl.ds(h * LANE, LANE)]

        def gate(r0):
            u = _gelu(p_ref[pl.ds(r0, PACK), pl.ds(0, D)].astype(F32))
            z = p_ref[pl.ds(r0, PACK), pl.ds(2 * D, D)].astype(F32)
            yb_ref[pl.ds(r0, PACK), :] = (u * sbuf[pl.ds(r0, PACK), :] * (z * _sigmoid(z))).astype(BF16)
        _strips(tm, PACK,gate)

    return pl.pallas_call(
        body, name="branch_b_fwd", grid=(n_i,),
        in_specs=[pl.BlockSpec((tm, 3 * D), lambda i: (i, 1)),
                  pl.BlockSpec((HEADS, CHUNK, CHUNK), lambda i: (0, 0, 0)),
                  pl.BlockSpec((CHUNK, D), lambda i: (0, 0)), _vec_spec(1), _vec_spec(1)],
        out_specs=pl.BlockSpec((tm, D), lambda i: (i, 0)),
        out_shape=_sds((s_len, D), BF16),
        scratch_shapes=[pltpu.VMEM((tm, D), BF16), pltpu.VMEM((tm, D), F32)],
        compiler_params=_params(("arbitrary",)),
    )(p, wt, bias_full, ln_g, ln_b)


def _dot_t(a, b):
    return lax.dot_general(a, b, (((1,), (1,)), ((), ())), preferred_element_type=F32)


def _out_proj(p, ya_in, yb_in, x, target, gate, g_final, w_co, w_so, w_o):
    s_len = x.shape[0]
    tm = min(256, s_len)
    n_i = s_len // tm

    def body(pg_ref, ya_ref, yb_ref, x_ref, t_ref, gate_ref, gf_ref, wco_ref, wso_ref, wo_ref,
             dx2_ref, dya_ref, dyb_ref, dp_ref, mb_ref, dob_ref, dyab_ref, dybb_ref, sums_ref):
        @pl.when(pl.program_id(0) == 0)
        def _():
            sums_ref[...] = jnp.zeros((SUB, D), F32)

        y_a = jnp.dot(ya_ref[...], wco_ref[...], preferred_element_type=F32)
        y_b = jnp.dot(yb_ref[...], wso_ref[...], preferred_element_type=F32)
        ga = _sigmoid(pg_ref[:, pl.ds(0, D)].astype(F32))
        gb = _sigmoid(pg_ref[:, pl.ds(D, D)].astype(F32))
        mb = (ga * y_a + gb * y_b).astype(BF16)
        mb_ref[...] = mb
        o = jnp.dot(mb, wo_ref[...], preferred_element_type=F32)
        x2 = x_ref[...] + gate_ref[...] * o
        r2 = lax.rsqrt(_rowmean(x2 * x2) + EPS)
        xh = x2 * r2
        e = xh * gf_ref[...] - t_ref[...]
        dy = e * (1.0 / D)
        dxh = dy * gf_ref[...]
        dx2 = r2 * (dxh - xh * _rowmean(dxh * xh))
        dx2_ref[...] = dx2
        sums_ref[pl.ds(0, 1), :] += jnp.sum(dy * xh, axis=0, keepdims=True)
        sums_ref[pl.ds(1, 1), :] += jnp.sum(dx2 * o, axis=0, keepdims=True)
        sums_ref[pl.ds(2, 1), :] += jnp.sum(e * e, axis=0, keepdims=True) * (0.5 / D)
        dob = (gate_ref[...] * dx2).astype(BF16)
        dob_ref[...] = dob
        dm = _dot_t(dob, wo_ref[...])
        dy_a = (ga * dm).astype(BF16)
        dy_b = (gb * dm).astype(BF16)
        dyab_ref[...] = dy_a
        dybb_ref[...] = dy_b
        dp_ref[:, pl.ds(0, D)] = (dm * y_a * ga * (1.0 - ga)).astype(BF16)
        dp_ref[:, pl.ds(D, D)] = (dm * y_b * gb * (1.0 - gb)).astype(BF16)
        dya_ref[...] = _dot_t(dy_a, wco_ref[...])
        dyb_ref[...] = _dot_t(dy_b, wso_ref[...])

    tile = pl.BlockSpec((tm, D), lambda i: (i, 0))
    wspec = pl.BlockSpec((D, D), lambda i: (0, 0))
    return pl.pallas_call(
        body, name="out_proj", grid=(n_i,),
        in_specs=[pl.BlockSpec((tm, 2 * D), lambda i: (i, 3)), tile, tile, tile, tile, _vec_spec(1), _vec_spec(1),
                  wspec, wspec, wspec],
        out_specs=[tile, tile, tile, pl.BlockSpec((tm, 2 * D), lambda i: (i, 3)), tile, tile, tile, tile,
                   pl.BlockSpec((SUB, D), lambda i: (0, 0))],
        out_shape=[_sds((s_len, D), F32), _sds((s_len, D), F32), _sds((s_len, D), F32), _sds((s_len, N_SEC * D), BF16),
                   _sds((s_len, D), BF16), _sds((s_len, D), BF16), _sds((s_len, D), BF16), _sds((s_len, D), BF16),
                   _sds((SUB, D), F32)],
        compiler_params=_params(("arbitrary",)),
    )(p, ya_in, yb_in, x, target, gate, g_final, w_co, w_so, w_o)


A_STATS_ROWS = 8 + HALO


def _branch_a_bwd(p, y1, dya_in, dp, conv_wb, ln_g, ln_b):
    s_len = p.shape[0]
    tm = min(256, s_len)
    n_i = s_len // tm
    rows = 32
    n_out = rows // SUB

    def tile_of(i):
        return n_i - 1 - i

    def body(p_ref, y1_ref, dya_ref, dp_in, wb_ref, g_ref, b_ref, dp_ref, st_ref, dybuf, acc8, tapacc):
        del dp_in
        i = pl.program_id(0)

        @pl.when(i == 0)
        def _():
            dybuf[pl.ds(tm, HALO), :] = jnp.zeros((HALO, D), F32)
            st_ref[...] = jnp.zeros((A_STATS_ROWS, D), F32)
            acc8[...] = jnp.zeros((3 * PACK, D), F32)
            tapacc[...] = jnp.zeros((CONV_K * SUB, D), F32)

        def norm_bwd(r0):
            y1 = y1_ref[pl.ds(r0, PACK), :]
            mu = _rowmean(y1)
            yc = y1 - mu
            rstd = lax.rsqrt(_rowmean(yc * yc) + EPS)
            n1 = yc * rstd
            l1 = n1 * g_ref[...] + b_ref[...]
            sg = _sigmoid(l1)
            z = p_ref[pl.ds(r0, PACK), pl.ds(2 * D, D)].astype(F32)
            sz = _sigmoid(z)
            dya = dya_ref[pl.ds(r0, PACK), :]
            dp_ref[pl.ds(r0, PACK), pl.ds(2 * D, D)] = (dya * (l1 * sg) * _dsilu(z, sz)).astype(BF16)
            dl1 = dya * (z * sz) * _dsilu(l1, sg)
            acc8[pl.ds(0, PACK), :] += dl1 * n1
            acc8[pl.ds(PACK, PACK), :] += dl1
            dn1 = dl1 * g_ref[...]
            dy1 = rstd * (dn1 - _rowmean(dn1) - n1 * _rowmean(dn1 * n1))
            acc8[pl.ds(2 * PACK, PACK), :] += dy1
            dybuf[pl.ds(r0, PACK), :] = dy1
        _strips(tm, PACK,norm_bwd)

        def conv_bwd(r0):
            for lt in range(D // LANE):
                lanes = pl.ds(lt * LANE, LANE)
                glanes = pl.ds(D + lt * LANE, LANE)
                win = dybuf[pl.ds(r0, rows + HALO), lanes]
                sg16, a16 = [], []
                for h in range(rows // PACK):
                    rr = pl.ds(r0 + h * PACK, PACK)
                    s = _sigmoid(p_ref[rr, glanes].astype(F32))
                    sg16.append(s)
                    a16.append(p_ref[rr, lanes].astype(F32) * s)
                a = [a16[m // 2][(m % 2) * SUB:(m % 2 + 1) * SUB, :] for m in range(n_out)]
                da = [jnp.zeros((SUB, LANE), F32) for _ in range(n_out)]
                for phase in range(SUB):
                    offs = [o for o in range(CONV_K) if o % SUB == phase]
                    q_max = max(o // SUB for o in offs)
                    sh = win[phase:phase + (n_out + q_max) * SUB, :]
                    for o in offs:
                        k, q = CONV_K - 1 - o, o // SUB
                        w = wb_ref[pl.ds(k * SUB, SUB), lanes]
                        part = None
                        for m in range(n_out):
                            s = sh[(m + q) * SUB:(m + q + 1) * SUB, :]
                            da[m] = da[m] + w * s
                            part = a[m] * s if part is None else part + a[m] * s
                        tapacc[pl.ds(k * SUB, SUB), lanes] += part
                for h in range(rows // PACK):
                    rr = pl.ds(r0 + h * PACK, PACK)
                    da16 = jnp.concatenate(da[2 * h:2 * h + 2], axis=0)
                    dp_ref[rr, lanes] = (da16 * sg16[h]).astype(BF16)
                    dp_ref[rr, glanes] = (da16 * a16[h] * (1.0 - sg16[h])).astype(BF16)
        _strips(tm, rows, conv_bwd)

        dybuf[pl.ds(tm, HALO), :] = dybuf[pl.ds(0, HALO), :]

        @pl.when(i == n_i - 1)
        def _():
            for j in range(3):
                st_ref[pl.ds(j, 1), :] = jnp.sum(acc8[pl.ds(j * PACK, PACK), :], axis=0, keepdims=True)
            for k in range(CONV_K):
                st_ref[pl.ds(SUB + k, 1), :] = jnp.sum(tapacc[pl.ds(k * SUB, SUB), :], axis=0, keepdims=True)

    return pl.pallas_call(
        body, name="branch_a_bwd", grid=(n_i,),
        in_specs=[pl.BlockSpec((tm, 3 * D), lambda i: (tile_of(i), 0)),
                  pl.BlockSpec((tm, D), lambda i: (tile_of(i), 0)),
                  pl.BlockSpec((tm, D), lambda i: (tile_of(i), 0)),
                  pl.BlockSpec(memory_space=pl.ANY),
                  pl.BlockSpec((CONV_K * SUB, D), lambda i: (0, 0)), _vec_spec(1), _vec_spec(1)],
        out_specs=[pl.BlockSpec((tm, 3 * D), lambda i: (tile_of(i), 0)),
                   pl.BlockSpec((A_STATS_ROWS, D), lambda i: (0, 0))],
        out_shape=[_sds(dp.shape, BF16), _sds((A_STATS_ROWS, D), F32)],
        scratch_shapes=[pltpu.VMEM((tm + HALO, D), F32), pltpu.VMEM((3 * PACK, D), F32), pltpu.VMEM((CONV_K * SUB, D), F32)],
        input_output_aliases={3: 0},
        compiler_params=_params(("arbitrary",)),
    )(p, y1, dya_in, dp, conv_wb, ln_g, ln_b)


def _branch_b_bwd(p, dyb_in, dp, wt, wtt, bias_full, ln_g, ln_b):
    s_len = p.shape[0]
    tm = min(256, s_len)
    n_i = s_len // tm

    def body(p_ref, dyb_ref, dp_in, wt_ref, wtt_ref, bias_ref, g_ref, b_ref, dp_ref, st_ref, gbt_ref, gw_ref,
             vb, n2buf, rstdbuf, sbuf, dsb, dvbuf, acc8, gb_ref, dgbuf):
        del dp_in
        i = pl.program_id(0)

        @pl.when(i == 0)
        def _():
            st_ref[...] = jnp.zeros((SUB, D), F32)
            gbt_ref[...] = jnp.zeros((CHUNK, LANE), F32)
            gb_ref[...] = jnp.zeros((CHUNK, D), F32)
            gw_ref[...] = jnp.zeros((HEADS, CHUNK, CHUNK), F32)
            acc8[...] = jnp.zeros((2 * PACK, D), F32)

        def norm(r0):
            gv, dgv = _gelu_and_grad(p_ref[pl.ds(r0, PACK), pl.ds(D, D)].astype(F32))
            dgbuf[pl.ds(r0, PACK), :] = dgv
            mu = _rowmean(gv)
            vc = gv - mu
            rstd = lax.rsqrt(_rowmean(vc * vc) + EPS)
            n2 = vc * rstd
            n2buf[pl.ds(r0, PACK), :] = n2
            rstdbuf[pl.ds(r0, PACK), :] = jnp.broadcast_to(rstd, (PACK, LANE))
            vb[pl.ds(r0, PACK), :] = (n2 * g_ref[...] + b_ref[...]).astype(BF16)
        _strips(tm, PACK,norm)

        for ck in range(tm // CHUNK):
            for h in range(HEADS):
                blk = (pl.ds(ck * CHUNK, CHUNK), pl.ds(h * LANE, LANE))
                sbuf[blk] = jnp.dot(wt_ref[h], vb[blk], preferred_element_type=F32) + bias_ref[:, pl.ds(h * LANE, LANE)]

        def gate_bwd(r0):
            pu = p_ref[pl.ds(r0, PACK), pl.ds(0, D)].astype(F32)
            u, du = _gelu_and_grad(pu)
            z = p_ref[pl.ds(r0, PACK), pl.ds(2 * D, D)].astype(F32)
            sg = _sigmoid(z)
            sz = z * sg
            s = sbuf[pl.ds(r0, PACK), :]
            dyb = dyb_ref[pl.ds(r0, PACK), :]
            ds = dyb * u * sz
            dsb[pl.ds(r0, PACK), :] = ds.astype(BF16)
            gb_ref[pl.ds(pl.multiple_of(r0 % CHUNK, PACK), PACK), :] += ds
            dp_ref[pl.ds(r0, PACK), pl.ds(0, D)] = (dyb * s * sz * du).astype(BF16)
            dp_ref[pl.ds(r0, PACK), pl.ds(2 * D, D)] = (dyb * u * s * _dsilu(z, sg)).astype(BF16)
        _strips(tm, PACK,gate_bwd)

        for ck in range(tm // CHUNK):
            for h in range(HEADS):
                blk = (pl.ds(ck * CHUNK, CHUNK), pl.ds(h * LANE, LANE))
                d_s = dsb[blk]
                dvbuf[blk] = jnp.dot(wtt_ref[h], d_s, preferred_element_type=F32)
                gw_ref[h] += _dot_t(d_s, vb[blk])

        def norm_bwd(r0):
            dv = dvbuf[pl.ds(r0, PACK), :]
            n2 = n2buf[pl.ds(r0, PACK), :]
            rstd = rstdbuf[pl.ds(r0, PACK), pl.ds(0, 1)]
            acc8[pl.ds(0, PACK), :] += dv * n2
            acc8[pl.ds(PACK, PACK), :] += dv
            dn2 = dv * g_ref[...]
            dgv = rstd * (dn2 - _rowmean(dn2) - n2 * _rowmean(dn2 * n2))
            dp_ref[pl.ds(r0, PACK), pl.ds(D, D)] = (dgv * dgbuf[pl.ds(r0, PACK), :]).astype(BF16)
        _strips(tm, PACK,norm_bwd)

        @pl.when(i == n_i - 1)
        def _():
            for j in range(2):
                st_ref[pl.ds(j, 1), :] = jnp.sum(acc8[pl.ds(j * PACK, PACK), :], axis=0, keepdims=True)
            row = lax.broadcasted_iota(jnp.int32, (CHUNK, CHUNK), 0)
            col = lax.broadcasted_iota(jnp.int32, (CHUNK, CHUNK), 1)
            for h in range(HEADS):
                gw_ref[h] = jnp.where(row >= col, gw_ref[h], 0.0)
            lane = lax.broadcasted_iota(jnp.int32, (CHUNK, LANE), 1)
            gbt = jnp.zeros((CHUNK, LANE), F32)
            for h in range(HEADS):
                gbt = jnp.where(lane == h, jnp.sum(gb_ref[:, pl.ds(h * LANE, LANE)], axis=1, keepdims=True), gbt)
            gbt_ref[...] = gbt

    wspec = pl.BlockSpec((HEADS, CHUNK, CHUNK), lambda i: (0, 0, 0))
    return pl.pallas_call(
        body, name="branch_b_bwd", grid=(n_i,),
        in_specs=[pl.BlockSpec((tm, 3 * D), lambda i: (i, 1)), pl.BlockSpec((tm, D), lambda i: (i, 0)),
                  pl.BlockSpec(memory_space=pl.ANY), wspec, wspec,
                  pl.BlockSpec((CHUNK, D), lambda i: (0, 0)), _vec_spec(1), _vec_spec(1)],
        out_specs=[pl.BlockSpec((tm, 3 * D), lambda i: (i, 1)), pl.BlockSpec((SUB, D), lambda i: (0, 0)),
                   pl.BlockSpec((CHUNK, LANE), lambda i: (0, 0)), wspec],
        out_shape=[_sds(dp.shape, BF16), _sds((SUB, D), F32), _sds((CHUNK, LANE), F32), _sds((HEADS, CHUNK, CHUNK), F32)],
        scratch_shapes=[pltpu.VMEM((tm, D), BF16), pltpu.VMEM((tm, D), F32), pltpu.VMEM((tm, LANE), F32),
                        pltpu.VMEM((tm, D), F32), pltpu.VMEM((tm, D), BF16), pltpu.VMEM((tm, D), F32),
                        pltpu.VMEM((2 * PACK, D), F32), pltpu.VMEM((CHUNK, D), F32), pltpu.VMEM((tm, D), F32)],
        input_output_aliases={2: 0},
        compiler_params=_params(("arbitrary",)),
    )(p, dyb_in, dp, wt, wtt, bias_full, ln_g, ln_b)


def _in_proj_bwd(dp, wg_in, x, dx2, shift, scale, g_pre):
    del shift
    s_len = x.shape[0]
    tm = min(512, s_len)
    n_i = s_len // tm
    wn = wg_in.shape[2]

    def body(dp_ref, w_ref, x_ref, dx2_ref, sc_ref, g_ref, gx_ref, st_ref, acc, acc8):
        i = pl.program_id(0)

        @pl.when(i == 0)
        def _():
            st_ref[...] = jnp.zeros((SUB, D), F32)
            acc8[...] = jnp.zeros((3 * PACK, D), F32)

        dh = _dot_t(dp_ref[:, pl.ds(0, wn)], w_ref[0])
        for j in range(1, N_CHIP):
            dh = dh + _dot_t(dp_ref[:, pl.ds(j * wn, wn)], w_ref[j])
        acc[...] = dh

        def strip(r0):
            xs = x_ref[pl.ds(r0, PACK), :]
            r = lax.rsqrt(_rowmean(xs * xs) + EPS)
            xn = xs * r
            dhs = acc[pl.ds(r0, PACK), :]
            acc8[pl.ds(0, PACK), :] += dhs
            acc8[pl.ds(PACK, PACK), :] += dhs * (xn * g_ref[...])
            dhp = dhs * (1.0 + sc_ref[...])
            acc8[pl.ds(2 * PACK, PACK), :] += dhp * xn
            dxn = dhp * g_ref[...]
            gx_ref[pl.ds(r0, PACK), :] = dx2_ref[pl.ds(r0, PACK), :] + r * (dxn - xn * _rowmean(dxn * xn))
        _strips(tm, PACK, strip)

        @pl.when(i == n_i - 1)
        def _():
            for k in range(3):
                st_ref[pl.ds(k, 1), :] = jnp.sum(acc8[pl.ds(k * PACK, PACK), :], axis=0, keepdims=True)

    tile = pl.BlockSpec((tm, D), lambda i: (i, 0))
    return pl.pallas_call(
        body, name="in_proj_bwd", grid=(n_i,),
        in_specs=[pl.BlockSpec((tm, N_CHIP * wn), lambda i: (i, 0)),
                  pl.BlockSpec((N_CHIP, D, wn), lambda i: (0, 0, 0), pipeline_mode=pl.Buffered(1)),
                  tile, tile, _vec_spec(1), _vec_spec(1)],
        out_specs=[tile, pl.BlockSpec((SUB, D), lambda i: (0, 0))],
        out_shape=[_sds((s_len, D), F32), _sds((SUB, D), F32)],
        scratch_shapes=[pltpu.VMEM((tm, D), F32), pltpu.VMEM((3 * PACK, D), F32)],
        compiler_params=_params(("arbitrary",)),
    )(dp, wg_in, x, dx2, scale, g_pre)


def _grad_matmul(a, b, name):
    s_len, n = b.shape
    cb = min(2 * D, n)
    tn = 512
    per = cb // tn

    def body(a_ref, b_ref, ob_ref):
        ob_ref[0] = lax.dot_general(a_ref[...], b_ref[...], (((0,), (0,)), ((), ())),
                                    preferred_element_type=F32).astype(BF16)

    return pl.pallas_call(
        body, name=name, grid=(n // tn,),
        in_specs=[pl.BlockSpec((s_len, D), lambda j: (0, 0), pipeline_mode=pl.Buffered(1)),
                  pl.BlockSpec((s_len, tn), lambda j: (0, j))],
        out_specs=pl.BlockSpec((1, D, tn), lambda j: (j // per, 0, j % per)),
        out_shape=_sds((n // cb, D, cb), BF16),
        compiler_params=_params(("arbitrary",)),
    )(a, b)


def _local_step(x, target, shift, scale, gate, g_pre, conv_w_full, conv_b, conv_ln_g, conv_ln_b,
                sgu_ln_g, sgu_ln_b, w_sgu, b_sgu, g_final, wg_in, out_shards):
    conv_wb = jnp.repeat(conv_w_full, SUB, axis=0)
    causal = jnp.tril(jnp.ones((CHUNK, CHUNK), dtype=bool))
    wt = jnp.where(causal[None], w_sgu, 0.0).astype(BF16)
    wtt = jnp.swapaxes(wt, 1, 2)
    bias_full = jnp.repeat(b_sgu.T, LANE, axis=1)

    p, hb, gathered = _in_proj_gather(x, shift, scale, g_pre, wg_in, out_shards)
    w_co, w_so, w_o = (g.reshape(D, D) for g in gathered)
    ya_in, y1 = _branch_a_fwd(p, conv_wb, conv_b, conv_ln_g, conv_ln_b)
    yb_in = _branch_b_fwd(p, wt, bias_full, sgu_ln_g, sgu_ln_b)
    dx2, dya_in, dyb_in, dp, mb, dob, dyab, dybb, sums_o = _out_proj(
        p, ya_in, yb_in, x, target, gate, g_final, w_co, w_so, w_o)
    dp, st_a = _branch_a_bwd(p, y1, dya_in, dp, conv_wb, conv_ln_g, conv_ln_b)
    dp, st_b, gbt, gws = _branch_b_bwd(p, dyb_in, dp, wt, wtt, bias_full, sgu_ln_g, sgu_ln_b)
    grad_x, st_i = _in_proj_bwd(dp, wg_in, x, dx2, shift, scale, g_pre)
    gw_o = _grad_matmul(mb, dob, "grad_w_o")
    gw_co = _grad_matmul(ya_in, dyab, "grad_w_conv_out")
    gw_so = _grad_matmul(yb_in, dybb, "grad_w_sgu_out")
    return dict(
        grad_x=grad_x, loss_cols=sums_o[2:3], g_final=sums_o[0:1], d_gate=sums_o[1:2],
        d_shift=st_i[0:1], d_scale=st_i[1:2], g_pre=st_i[2:3],
        conv_ln_g=st_a[0:1], conv_ln_b=st_a[1:2], conv_b=st_a[2:3], conv_w=st_a[SUB:SUB + CONV_K],
        sgu_ln_g=st_b[0:1], sgu_ln_b=st_b[1:2], b_sgu=gbt[:, :HEADS].T, w_sgu=gws,
        hb=hb, dp=dp, w_o=gw_o, w_conv_out=gw_co, w_sgu_out=gw_so)


ANY_SPEC = pl.BlockSpec(memory_space=pl.ANY)
VMEM_SPEC = pl.BlockSpec(memory_space=pltpu.VMEM)


def _place():
    return lax.axis_index("x"), lax.axis_index("y"), lax.axis_index("c")


def _peer(k):
    x, y, c = _place()
    return (1 - x if k & 4 else x, 1 - y if k & 2 else y, 1 - c if k & 1 else c)


def _dev_of(p):
    return 4 * p[0] + 2 * p[1] + p[2]


def _chip_of(p):
    return 2 * p[0] + p[1]


def _rdma(src, dst, send_sem, recv_sem, to):
    return pltpu.make_async_remote_copy(src_ref=src, dst_ref=dst, send_sem=send_sem, recv_sem=recv_sem,
                                        device_id=to, device_id_type=MESH)


CHIP_PEERS = (2, 4, 6)
ALL_PEERS = tuple(range(1, N_DEV))
SIBLING = 1


def _setup_comm(c8, w_ada_s, b_ada_s, convw_s, shards):
    n_mod = w_ada_s.shape[1]
    rows = SUB * N_DEV
    n = len(shards)

    def body(c8_ref, wada_ref, bada_ref, cw_ref, *refs):
        ins, (call_ref, mod_ref, cwall_ref), outs = refs[:n], refs[n:n + 3], refs[n + 3:2 * n + 3]
        csend, crecv, wsend, wrecv, msend, mrecv = refs[2 * n + 3:2 * n + 9]
        gather_a, gather_b, gather_c = _gather_phases([s.shape[0] for s in shards], ins, outs, refs[2 * n + 9:])
        gather_a()
        me = _place()
        dev, chip = _dev_of(me), _chip_of(me)

        def c_rows(d):
            return call_ref.at[pl.ds(pl.multiple_of(d * SUB, SUB), SUB), :]

        call_ref[pl.ds(pl.multiple_of(dev * SUB, SUB), SUB), :] = c8_ref[...]
        cwall_ref[chip] = cw_ref[...]
        c_out = [_rdma(c8_ref, c_rows(dev), csend.at[k], crecv.at[k], _peer(k)) for k in ALL_PEERS]
        w_out = [_rdma(cw_ref, cwall_ref.at[chip], wsend.at[k], wrecv.at[k], _peer(k)) for k in CHIP_PEERS]
        for cp in c_out + w_out:
            cp.start()
        for k in ALL_PEERS:
            _rdma(c8_ref, c_rows(_dev_of(_peer(k))), csend.at[k], crecv.at[k], _peer(k)).wait_recv()
        part = jnp.dot(call_ref[...].astype(BF16), wada_ref[...].astype(BF16), preferred_element_type=F32) + bada_ref[...]
        mod_ref[chip] = part
        m_out = [_rdma(mod_ref.at[chip], mod_ref.at[chip], msend.at[k], mrecv.at[k], _peer(k)) for k in CHIP_PEERS]
        for cp in m_out:
            cp.start()
        for k in CHIP_PEERS:
            pc = _chip_of(_peer(k))
            _rdma(cw_ref, cwall_ref.at[pc], wsend.at[k], wrecv.at[k], _peer(k)).wait_recv()
            _rdma(mod_ref.at[pc], mod_ref.at[pc], msend.at[k], mrecv.at[k], _peer(k)).wait_recv()
        for cp in c_out + w_out + m_out:
            cp.wait_send()
        gather_b()
        gather_c()

    res = pl.pallas_call(
        body, name="setup_comm",
        in_specs=[VMEM_SPEC] * (4 + n), out_specs=[VMEM_SPEC] * (3 + n),
        out_shape=([_sds((rows, D), F32), _sds((N_CHIP, rows, n_mod), F32), _sds((N_CHIP,) + convw_s.shape, F32)]
                   + [_sds((N_CHIP,) + s.shape, s.dtype) for s in shards]),
        scratch_shapes=([pltpu.SemaphoreType.DMA((N_DEV,))] * 6
                        + [pltpu.SemaphoreType.DMA((n,))] + [pltpu.SemaphoreType.DMA((n, len(CHIP_PEERS)))] * 4),
        compiler_params=_params(),
    )(c8, w_ada_s, b_ada_s, convw_s, *shards)
    return res[0], res[1], res[2], res[3:]


def _gather_phases(row_counts, ins, dsts, sems):
    n = len(row_counts)
    lsem, isend, irecv, dsend, drecv = sems
    me = _place()
    chip, c = _chip_of(me), me[2]

    def half(t, which):
        hr = row_counts[t] // 2
        return pl.ds(pl.multiple_of(which * hr, hr), hr)

    def local(t):
        return pltpu.make_async_copy(ins[t], dsts[t].at[chip], lsem.at[t])

    def to_chip(t, j):
        return _rdma(ins[t].at[half(t, c)], dsts[t].at[chip, half(t, c)], isend.at[t, j], irecv.at[t, j], _peer(CHIP_PEERS[j]))

    def landed(t, j, which):
        return dsts[t].at[_chip_of(_peer(CHIP_PEERS[j])), half(t, which)]

    def to_sibling(t, j):
        return _rdma(landed(t, j, c), landed(t, j, c), dsend.at[t, j], drecv.at[t, j], _peer(SIBLING))

    pairs = [(t, j) for t in range(n) for j in range(len(CHIP_PEERS))]

    def phase_a():
        for t in range(n):
            local(t).start()
        for t, j in pairs:
            to_chip(t, j).start()

    def phase_b():
        for t, j in pairs:
            _rdma(landed(t, j, c), landed(t, j, c), isend.at[t, j], irecv.at[t, j], _peer(CHIP_PEERS[j])).wait_recv()
            to_sibling(t, j).start()

    def phase_c():
        for t, j in pairs:
            _rdma(landed(t, j, 1 - c), landed(t, j, 1 - c), dsend.at[t, j], drecv.at[t, j], _peer(SIBLING)).wait_recv()
        for t, j in pairs:
            to_chip(t, j).wait_send()
            to_sibling(t, j).wait_send()
        for t in range(n):
            local(t).wait()

    return phase_a, phase_b, phase_c


def _in_proj_gather(x, shift, scale, g_pre, wg_in, shards):
    s_len = x.shape[0]
    tm = min(256, s_len)
    n_i = s_len // tm
    wn = wg_in.shape[2]
    n = len(shards)

    def body(x_ref, sh_ref, sc_ref, g_ref, w_ref, *refs):
        ins, p_ref, hb_ref, outs = refs[:n], refs[n], refs[n + 1], refs[n + 2:2 * n + 2]
        gath, sems = refs[2 * n + 2:3 * n + 2], refs[3 * n + 2:]
        phases = _gather_phases([s.shape[0] for s in shards], ins, gath, sems)
        i = pl.program_id(0)
        for step, phase in zip((0, n_i // 2, n_i - 1), phases):
            pl.when(i == step)(phase)

        @pl.when(i == n_i - 1)
        def _():
            for t in range(n):
                outs[t][...] = gath[t][...]

        def strip(r0):
            xs = x_ref[pl.ds(r0, PACK), :]
            r = lax.rsqrt(_rowmean(xs * xs) + EPS)
            h = (xs * r) * g_ref[...] * (1.0 + sc_ref[...]) + sh_ref[...]
            hb_ref[pl.ds(r0, PACK), :] = h.astype(BF16)
        _strips(tm, PACK, strip)
        hb = hb_ref[...]
        for j in range(N_CHIP):
            p_ref[:, pl.ds(j * wn, wn)] = jnp.dot(hb, w_ref[j], preferred_element_type=F32).astype(BF16)

    res = pl.pallas_call(
        body, name="in_proj", grid=(n_i,),
        in_specs=[pl.BlockSpec((tm, D), lambda i: (i, 0)), _vec_spec(1), _vec_spec(1), _vec_spec(1),
                  pl.BlockSpec((N_CHIP, D, wn), lambda i: (0, 0, 0), pipeline_mode=pl.Buffered(1))] + [VMEM_SPEC] * n,
        out_specs=[pl.BlockSpec((tm, N_CHIP * wn), lambda i: (i, 0)), pl.BlockSpec((tm, D), lambda i: (i, 0))] + [VMEM_SPEC] * n,
        out_shape=([_sds((s_len, N_SEC * D), BF16), _sds((s_len, D), BF16)]
                   + [_sds((N_CHIP,) + s.shape, s.dtype) for s in shards]),
        scratch_shapes=([pltpu.VMEM((N_CHIP,) + s.shape, s.dtype) for s in shards]
                        + [pltpu.SemaphoreType.DMA((n,))] + [pltpu.SemaphoreType.DMA((n, len(CHIP_PEERS)))] * 4),
        compiler_params=_params(("arbitrary",)),
    )(x, shift, scale, g_pre, wg_in, *shards)
    return res[0], res[1], res[2:]


def _reduce_scatter(grads, name):
    n = len(grads)
    shapes = [g.shape[2:] for g in grads]

    def body(*refs):
        ins, outs, scratch = refs[:n], refs[n:2 * n], refs[2 * n:]
        for phase in _reduce_phases(shapes, ins, outs, scratch[:n], scratch[n:2 * n], scratch[2 * n:3 * n], scratch[3 * n:]):
            phase()

    return pl.pallas_call(
        body, name=name,
        in_specs=[VMEM_SPEC] * n, out_specs=[VMEM_SPEC] * n,
        out_shape=[_sds((2,) + s, F32) for s in shapes],
        scratch_shapes=_reduce_scratch(shapes),
        compiler_params=_params(),
    )(*grads)


def _reduce_phases(shapes, ins, outs, pbufs, rbufs, accs, sems):
    n = len(shapes)
    psend, precv, csend, crecv, fsend, frecv = sems
    me = _place()
    chip, c = _chip_of(me), me[2]
    sib = _peer(SIBLING)

    def to_sibling(t, d):
        return _rdma(ins[t].at[d, 1 - c], pbufs[t].at[d], psend.at[t, d], precv.at[t, d], sib)

    def to_chip(t, j):
        return _rdma(pbufs[t].at[jnp.bitwise_xor(chip, j)], rbufs[t].at[j - 1], csend.at[t, j], crecv.at[t, j], _peer(2 * j))

    def finished(t):
        return _rdma(outs[t].at[c], outs[t].at[c], fsend.at[t], frecv.at[t], sib)

    def phase_a():
        for t in range(n):
            for d in range(N_CHIP):
                to_sibling(t, d).start()

    def phase_b():
        for j in (1, 2, 3, 0):
            d = jnp.bitwise_xor(chip, j)
            for t in range(n):
                to_sibling(t, d).wait_recv()

                def pair_sum(r0, t=t, d=d, j=j):
                    rows = pl.ds(r0, PACK)
                    s = ins[t][d, c, rows, :].astype(F32) + pbufs[t][d, rows, :].astype(F32)
                    if j == 0:
                        accs[t][rows, :] = s
                    else:
                        pbufs[t][d, rows, :] = s.astype(BF16)
                _strips(shapes[t][0], PACK, pair_sum)
                if j:
                    to_chip(t, j).start()

    def phase_c():
        for t in range(n):
            for j in (1, 2, 3):
                blk = rbufs[t].at[j - 1]
                _rdma(blk, blk, csend.at[t, j], crecv.at[t, j], _peer(2 * j)).wait_recv()

            def total(r0, t=t):
                rows = pl.ds(r0, PACK)
                s = accs[t][rows, :] + rbufs[t][0, rows, :].astype(F32)
                s = s + rbufs[t][1, rows, :].astype(F32)
                outs[t][c, rows, :] = s + rbufs[t][2, rows, :].astype(F32)
            _strips(shapes[t][0], PACK, total)
            finished(t).start()

    def phase_d():
        for t in range(n):
            blk = outs[t].at[1 - c]
            _rdma(blk, blk, fsend.at[t], frecv.at[t], sib).wait_recv()
        for t in range(n):
            for d in range(N_CHIP):
                to_sibling(t, d).wait_send()
            for j in (1, 2, 3):
                to_chip(t, j).wait_send()
            finished(t).wait_send()

    return phase_a, phase_b, phase_c, phase_d


def _sum_small_phases(ins, outs, pbufs, buf4s, sems):
    n = len(ins)
    psend, precv, send, recv = sems
    chip = _chip_of(_place())

    def swap(t):
        return _rdma(ins[t], pbufs[t], psend.at[t], precv.at[t], _peer(SIBLING))

    def to_chip(t, k):
        return _rdma(buf4s[t].at[chip], buf4s[t].at[chip], send.at[t, k], recv.at[t, k], _peer(k))

    def phase_a():
        for t in range(n):
            swap(t).start()

    def phase_b():
        for t in range(n):
            swap(t).wait()
            buf4s[t][chip] = ins[t][...] + pbufs[t][...]
            for k in CHIP_PEERS:
                to_chip(t, k).start()

    def phase_c():
        for t in range(n):
            for k in CHIP_PEERS:
                blk = buf4s[t].at[_chip_of(_peer(k))]
                _rdma(blk, blk, send.at[t, k], recv.at[t, k], _peer(k)).wait_recv()
            outs[t][...] = (buf4s[t][0] + buf4s[t][1]) + (buf4s[t][2] + buf4s[t][3])

    def phase_d():
        for t in range(n):
            for k in CHIP_PEERS:
                to_chip(t, k).wait_send()

    return phase_a, phase_b, phase_c, phase_d


def _sum_small_scratch(blobs):
    n = len(blobs)
    return ([pltpu.VMEM(b.shape, F32) for b in blobs] + [pltpu.VMEM((N_CHIP,) + b.shape, F32) for b in blobs]
            + [pltpu.SemaphoreType.DMA((n,))] * 2 + [pltpu.SemaphoreType.DMA((n, N_DEV))] * 2)


def _reduce_scratch(shapes):
    n = len(shapes)
    return ([pltpu.VMEM((N_CHIP,) + s, BF16) for s in shapes] + [pltpu.VMEM((N_CHIP - 1,) + s, BF16) for s in shapes]
            + [pltpu.VMEM(s, F32) for s in shapes]
            + [pltpu.SemaphoreType.DMA((n, N_CHIP))] * 4 + [pltpu.SemaphoreType.DMA((n,))] * 2)


def _grad_matmul_reduce(a, b, name, grads, blobs):
    s_len, n_cols = b.shape
    cb = min(2 * D, n_cols)
    tn = 512
    per = cb // tn
    steps = n_cols // tn
    n, nb = len(grads), len(blobs)
    shapes = [g.shape[2:] for g in grads]
    n_red = len(_reduce_scratch(shapes))

    def body(a_ref, b_ref, *refs):
        ins, bins = refs[:n], refs[n:n + nb]
        ob_ref, outs, bouts = refs[n + nb], refs[n + nb + 1:2 * n + nb + 1], refs[2 * n + nb + 1:2 * (n + nb) + 1]
        scratch = refs[2 * (n + nb) + 1:]
        fulls, red, small = scratch[:n], scratch[n:n + n_red], scratch[n + n_red:]
        phases = _reduce_phases(shapes, ins, fulls, red[:n], red[n:2 * n], red[2 * n:3 * n], red[3 * n:])
        small_phases = _sum_small_phases(bins, bouts, small[:nb], small[nb:2 * nb], small[2 * nb:])
        j = pl.program_id(0)
        for step, phase in zip((0, 2, steps - 2, steps - 1), phases):
            pl.when(j == step)(phase)
        for step, phase in zip((1, 3, steps - 2, steps - 1), small_phases):
            pl.when(j == step)(phase)

        @pl.when(j == steps - 1)
        def _():
            for t in range(n):
                outs[t][...] = fulls[t][...]
        ob_ref[0] = lax.dot_general(a_ref[...], b_ref[...], (((0,), (0,)), ((), ())),
                                    preferred_element_type=F32).astype(BF16)

    res = pl.pallas_call(
        body, name=name, grid=(steps,),
        in_specs=[pl.BlockSpec((s_len, D), lambda j: (0, 0), pipeline_mode=pl.Buffered(1)),
                  pl.BlockSpec((s_len, tn), lambda j: (0, j))] + [VMEM_SPEC] * (n + nb),
        out_specs=[pl.BlockSpec((1, D, tn), lambda j: (j // per, 0, j % per))] + [VMEM_SPEC] * (n + nb),
        out_shape=([_sds((n_cols // cb, D, cb), BF16)] + [_sds((2,) + s, F32) for s in shapes]
                   + [_sds(bl.shape, F32) for bl in blobs]),
        scratch_shapes=[pltpu.VMEM((2,) + s, F32) for s in shapes] + _reduce_scratch(shapes) + _sum_small_scratch(blobs),
        compiler_params=_params(("arbitrary",)),
    )(a, b, *grads, *blobs)
    return res[0], res[1:1 + n], res[1 + n:]


def _adamw_math(w, g, m, v):
    m = ADAM_B1 * m + (1.0 - ADAM_B1) * g
    v = ADAM_B2 * v + (1.0 - ADAM_B2) * (g * g)
    m_hat = m / (1.0 - ADAM_B1 ** ADAM_STEP)
    v_hat = v / (1.0 - ADAM_B2 ** ADAM_STEP)
    delta = -ADAM_LR * (m_hat / (jnp.sqrt(v_hat) + ADAM_EPS) + ADAM_WD * w)
    return delta, m, v


def _row_tile(r, cols):
    if r * cols * 4 <= 2 ** 20:
        return r
    return next(t for t in (512, 256, 128, 64, 32, 16, 8) if r % t == 0 and t * cols * 4 <= 2 ** 20)


def _adamw(w, g, m, v, name):
    r, cols = w.shape
    tr = _row_tile(r, cols)

    def body(w_ref, g_ref, m_ref, v_ref, go_ref, d_ref, nm_ref, nv_ref):
        g = g_ref[...]
        go_ref[...] = g
        d_ref[...], nm_ref[...], nv_ref[...] = _adamw_math(w_ref[...], g, m_ref[...], v_ref[...])

    spec = pl.BlockSpec((tr, cols), lambda i: (i, 0))
    return pl.pallas_call(
        body, name=name, grid=(r // tr,), in_specs=[spec] * 4, out_specs=[spec] * 4,
        out_shape=[_sds((r, cols), F32)] * 4, compiler_params=_params(("arbitrary",)),
    )(w, g, m, v)


def _adamw_ada(w, ct, dm, m, v):
    r, cols = w.shape
    tr = _row_tile(r, cols)

    def body(w_ref, ct_ref, dm_ref, m_ref, v_ref, g_ref, d_ref, nm_ref, nv_ref):
        g = jnp.dot(ct_ref[...], dm_ref[...], preferred_element_type=F32)
        g_ref[...] = g
        d_ref[...], nm_ref[...], nv_ref[...] = _adamw_math(w_ref[...], g, m_ref[...], v_ref[...])

    spec = pl.BlockSpec((tr, cols), lambda i: (i, 0))
    return pl.pallas_call(
        body, name="adamw_ada", grid=(r // tr,),
        in_specs=[spec, pl.BlockSpec((tr, LANE), lambda i: (i, 0)), pl.BlockSpec((LANE, cols), lambda i: (0, 0)), spec, spec],
        out_specs=[spec] * 4, out_shape=[_sds((r, cols), F32)] * 4, compiler_params=_params(("arbitrary",)),
    )(w, ct, dm, m, v)


BLOB_VEC, BLOB_BSGU, BLOB_CONV, BLOB_ADA, BLOB_DMOD, BLOB_LOSS, BLOB_ROWS = 0, 8, 16, 48, 56, 80, 88
N_VEC = 7


def _adamw_small(tot, g_w_sgu, g_conv, params):
    n = len(params)

    def body(*refs):
        tot_ref, gws_ref, gconv_ref = refs[:3]
        wmv = refs[3:3 + 3 * n]
        outs = refs[3 + 3 * n:]
        grads = [tot_ref[pl.ds(BLOB_VEC + i, 1), :] for i in range(N_VEC)]
        grads += [tot_ref[pl.ds(BLOB_BSGU, HEADS), pl.ds(0, CHUNK)], gconv_ref[...], gws_ref[...], tot_ref[pl.ds(BLOB_ADA, 3), :]]
        for i, g in enumerate(grads):
            w_ref, m_ref, v_ref = wmv[3 * i:3 * i + 3]
            d, nm, nv = _adamw_math(w_ref[...], g, m_ref[...], v_ref[...])
            outs[4 * i][...] = g
            outs[4 * i + 1][...] = d
            outs[4 * i + 2][...] = nm
            outs[4 * i + 3][...] = nv

    flat = [a for wmv in params for a in wmv]
    return pl.pallas_call(
        body, name="adamw_small",
        in_specs=[VMEM_SPEC] * (3 + len(flat)), out_specs=[VMEM_SPEC] * (4 * n),
        out_shape=[_sds(wmv[0].shape, F32) for wmv in params for _ in range(4)],
        compiler_params=_params(),
    )(tot, g_w_sgu, g_conv, *flat)


def _set_rows(buf, row, val):
    return lax.dynamic_update_slice(buf, val.astype(F32), (row, 0))


def kernel(x, c, w_ada, b_ada, g_pre, w_in, conv_w, conv_b, conv_ln_g, conv_ln_b, w_conv_out, sgu_ln_g, sgu_ln_b, w_sgu, b_sgu, w_sgu_out, w_o, g_final, loss_target, m_w_ada, m_b_ada, m_g_pre, m_w_in, m_conv_w, m_conv_b, m_conv_ln_g, m_conv_ln_b, m_w_conv_out, m_sgu_ln_g, m_sgu_ln_b, m_w_sgu, m_b_sgu, m_w_sgu_out, m_w_o, m_g_final, v_w_ada, v_b_ada, v_g_pre, v_w_in, v_conv_w, v_conv_b, v_conv_ln_g, v_conv_ln_b, v_w_conv_out, v_sgu_ln_g, v_sgu_ln_b, v_w_sgu, v_b_sgu, v_w_sgu_out, v_w_o, v_g_final):
    me = _place()
    dev, chip = _dev_of(me), _chip_of(me)
    n_ada = w_ada.shape[2]
    conv_cols = conv_w.shape[2]

    b_ada_s = lax.dynamic_slice(b_ada, (0, chip * n_ada), (1, n_ada))
    c_all, mod_all, cw_all, (wg_in,) = _setup_comm(
        jnp.broadcast_to(c, (SUB, D)), w_ada[0], b_ada_s, jnp.pad(conv_w[0], ((0, HALO - CONV_K), (0, 0))),
        [w_in[0].astype(BF16)])
    mod = lax.dynamic_slice(mod_all, (0, dev * SUB, 0), (N_CHIP, 1, n_ada)).reshape(1, 3 * D)
    shift, scale, gate = mod[:, :D], mod[:, D:2 * D], mod[:, 2 * D:]
    conv_w_full = jnp.swapaxes(cw_all, 0, 1).reshape(HALO, D)[:CONV_K]

    loc = _local_step(x[0], loss_target[0], shift, scale, gate, g_pre, conv_w_full, conv_b, conv_ln_g, conv_ln_b,
                      sgu_ln_g, sgu_ln_b, w_sgu[0], b_sgu[0], g_final.reshape(1, D), wg_in,
                      [w_conv_out[0].astype(BF16), w_sgu_out[0].astype(BF16), w_o[0].astype(BF16)])

    d_mod = jnp.concatenate([loc["d_shift"], loc["d_scale"], loc["d_gate"]], axis=0)
    blob = jnp.zeros((BLOB_ROWS, D), F32)
    for i, name in enumerate(["g_pre", "conv_b", "conv_ln_g", "conv_ln_b", "sgu_ln_g", "sgu_ln_b", "g_final"]):
        blob = _set_rows(blob, BLOB_VEC + i, loc[name])
    blob = _set_rows(blob, BLOB_BSGU, loc["b_sgu"])
    blob = _set_rows(blob, BLOB_CONV, loc["conv_w"])
    blob = _set_rows(blob, BLOB_ADA, d_mod)
    blob = lax.dynamic_update_slice(blob, d_mod, (BLOB_DMOD + 3 * dev, 0))
    blob = _set_rows(blob, BLOB_LOSS, loc["loss_cols"])

    big = ["w_in", "w_conv_out", "w_sgu_out", "w_o"]
    contrib_out = [loc[name].reshape(N_CHIP, 2, D // (2 * N_CHIP), D) for name in big[1:]]
    gw_in, full_out, (tot, g_w_sgu) = _grad_matmul_reduce(
        loc["hb"], loc["dp"], "grad_w_in", contrib_out, [blob, loc["w_sgu"].reshape(HEADS * CHUNK, CHUNK)])
    full_in = _reduce_scatter([gw_in.reshape(N_CHIP, 2, D // 2, gw_in.shape[2])], "reduce_w_in")
    g_big = {name: f.reshape(2 * f.shape[1], f.shape[2]) for name, f in zip(big, list(full_in) + list(full_out))}

    loss = jnp.sum(tot[BLOB_LOSS])
    g_conv_s = lax.dynamic_slice(tot, (BLOB_CONV, chip * conv_cols), (CONV_K, conv_cols))
    d_mod_all = tot[BLOB_DMOD:BLOB_DMOD + 3 * N_DEV].reshape(N_DEV, 3 * D)

    ct = jnp.pad(c_all[::SUB].T, ((0, 0), (0, LANE - N_DEV))).astype(BF16)
    dm = jnp.pad(lax.dynamic_slice(d_mod_all, (0, chip * n_ada), (N_DEV, n_ada)), ((0, LANE - N_DEV), (0, 0))).astype(BF16)
    g_ada, d_ada, nm_ada, nv_ada = _adamw_ada(w_ada[0], ct, dm, m_w_ada[0], v_w_ada[0])

    upd = {}
    for name, w, m, v in [("w_in", w_in, m_w_in, v_w_in), ("w_conv_out", w_conv_out, m_w_conv_out, v_w_conv_out),
                          ("w_sgu_out", w_sgu_out, m_w_sgu_out, v_w_sgu_out), ("w_o", w_o, m_w_o, v_w_o)]:
        upd[name] = _adamw(w[0], g_big[name], m[0], v[0], "adamw_" + name)

    def wmv(w, m, v, shape):
        return tuple(a.reshape(shape) for a in (w, m, v))

    small_params = [wmv(w, m, v, (1, D)) for w, m, v in [
        (g_pre, m_g_pre, v_g_pre), (conv_b, m_conv_b, v_conv_b), (conv_ln_g, m_conv_ln_g, v_conv_ln_g),
        (conv_ln_b, m_conv_ln_b, v_conv_ln_b), (sgu_ln_g, m_sgu_ln_g, v_sgu_ln_g), (sgu_ln_b, m_sgu_ln_b, v_sgu_ln_b),
        (g_final, m_g_final, v_g_final)]]
    small_params += [wmv(b_sgu, m_b_sgu, v_b_sgu, (HEADS, CHUNK)), wmv(conv_w, m_conv_w, v_conv_w, (CONV_K, conv_cols)),
                     wmv(w_sgu, m_w_sgu, v_w_sgu, (HEADS * CHUNK, CHUNK)), wmv(b_ada, m_b_ada, v_b_ada, (3, D))]
    small_out = _adamw_small(tot, g_w_sgu, g_conv_s, small_params)

    def leaves(kind):
        vecs = [small_out[4 * i + kind] for i in range(N_VEC)]
        o_b_sgu, o_conv, o_w_sgu, o_b_ada = (small_out[4 * (N_VEC + i) + kind] for i in range(4))
        ada = (g_ada, d_ada, nm_ada, nv_ada)[kind]
        def bigk(name):
            return upd[name][kind][None]
        return [ada[None], o_b_ada.reshape(1, 3 * D), vecs[0], bigk("w_in"), o_conv[None], vecs[1], vecs[2], vecs[3],
                bigk("w_conv_out"), vecs[4], vecs[5], o_w_sgu.reshape(1, HEADS, CHUNK, CHUNK), o_b_sgu[None],
                bigk("w_sgu_out"), bigk("w_o"), vecs[6].reshape(D)]

    return (loss, loc["grad_x"][None], *leaves(0), *leaves(1), *leaves(2), *leaves(3))
```

```python
import functools

import jax
import jax.numpy as jnp
from jax import lax
from jax.experimental import pallas as pl
from jax.experimental.pallas import tpu as pltpu

F32 = jnp.float32
BF16 = jnp.bfloat16
MESH = pl.DeviceIdType.MESH

D = 1024
N_SEC = 8
N_CHIP = 4
N_DEV = 8
EPS = 1e-6
CONV_K = 31
HALO = 32
CHUNK = 128
HEADS = 8
LANE = 128
SUB = 8
PACK = 16
VMEM_LIMIT = 56 * 1024 * 1024

ADAM_LR, ADAM_B1, ADAM_B2, ADAM_EPS, ADAM_WD, ADAM_STEP = 0.001, 0.9, 0.999, 1e-08, 0.01, 10

_SQRT_HALF = 0.7071067811865476
_INV_SQRT_2PI = 0.3989422804014327


def _sds(shape, dtype):
    return jax.ShapeDtypeStruct(shape, dtype)


def _params(sem=None):
    if sem is None:
        return pltpu.CompilerParams(vmem_limit_bytes=VMEM_LIMIT)
    return pltpu.CompilerParams(dimension_semantics=sem, vmem_limit_bytes=VMEM_LIMIT)


def _strips(n_rows, rows, fn):
    def step(s, carry):
        fn(pl.multiple_of(s * rows, rows))
        return carry
    lax.fori_loop(0, n_rows // rows, step, 0)


def _sigmoid(v):
    return 1.0 / (1.0 + jnp.exp(-v))


def _gelu(v):
    return 0.5 * v * (1.0 + lax.erf(v * _SQRT_HALF))


def _gelu_and_grad(v):
    cdf = 0.5 * (1.0 + lax.erf(v * _SQRT_HALF))
    return v * cdf, cdf + v * jnp.exp(-0.5 * v * v) * _INV_SQRT_2PI


def _dsilu(v, sg):
    return sg * (1.0 + v * (1.0 - sg))


def _rowmean(v):
    return jnp.mean(v, axis=-1, keepdims=True)


def _vec_spec(grid_rank):
    zeros = (0, 0)
    if grid_rank == 1:
        return pl.BlockSpec((1, D), lambda i: zeros)
    return pl.BlockSpec((1, D), lambda i, j: zeros)


def _conv_taps(win_ref, r0, lt, weight_of_offset, rows):
    lanes = pl.ds(lt * LANE, LANE)
    win = win_ref[pl.ds(r0, rows + HALO), lanes]
    n_out = rows // SUB
    acc = [jnp.zeros((SUB, LANE), F32) for _ in range(n_out)]
    for phase in range(SUB):
        offs = [o for o in weight_of_offset if o % SUB == phase]
        if not offs:
            continue
        q_max = max(o // SUB for o in offs)
        span = (n_out + q_max) * SUB
        sh = win[phase:phase + span, :]
        for o in offs:
            q = o // SUB
            w = weight_of_offset[o](lanes)
            for m in range(n_out):
                acc[m] = acc[m] + w * sh[(m + q) * SUB:(m + q + 1) * SUB, :]
    return acc


def _branch_a_fwd(p, conv_wb, conv_b, ln_g, ln_b):
    s_len = p.shape[0]
    tm = min(256, s_len)
    n_i = s_len // tm
    rows = 32

    def body(p_ref, wb_ref, cb_ref, g_ref, b_ref, ya_ref, y1_ref, abuf):
        @pl.when(pl.program_id(0) == 0)
        def _():
            abuf[pl.ds(0, HALO), :] = jnp.zeros((HALO, D), F32)

        def glu(r0):
            val = p_ref[pl.ds(r0, PACK), pl.ds(0, D)].astype(F32)
            gl = p_ref[pl.ds(r0, PACK), pl.ds(D, D)].astype(F32)
            abuf[pl.ds(HALO + r0, PACK), :] = val * _sigmoid(gl)
        _strips(tm, PACK,glu)

        taps = {HALO - (CONV_K - 1) + k: (lambda lanes, k=k: wb_ref[pl.ds(k * SUB, SUB), lanes]) for k in range(CONV_K)}

        def conv(r0):
            for lt in range(D // LANE):
                acc = _conv_taps(abuf, r0, lt, taps, rows)
                cb = cb_ref[:, pl.ds(lt * LANE, LANE)]
                for m, v in enumerate(acc):
                    y1_ref[pl.ds(r0 + m * SUB, SUB), pl.ds(lt * LANE, LANE)] = v + cb
        _strips(tm, rows, conv)

        def norm(r0):
            y1 = y1_ref[pl.ds(r0, PACK), :]
            mu = _rowmean(y1)
            yc = y1 - mu
            rstd = lax.rsqrt(_rowmean(yc * yc) + EPS)
            l1 = (yc * rstd) * g_ref[...] + b_ref[...]
            z = p_ref[pl.ds(r0, PACK), pl.ds(2 * D, D)].astype(F32)
            ya_ref[pl.ds(r0, PACK), :] = ((l1 * _sigmoid(l1)) * (z * _sigmoid(z))).astype(BF16)
        _strips(tm, PACK,norm)

        abuf[pl.ds(0, HALO), :] = abuf[pl.ds(tm, HALO), :]

    return pl.pallas_call(
        body, name="branch_a_fwd", grid=(n_i,),
        in_specs=[pl.BlockSpec((tm, 3 * D), lambda i: (i, 0)),
                  pl.BlockSpec((CONV_K * SUB, D), lambda i: (0, 0)), _vec_spec(1), _vec_spec(1), _vec_spec(1)],
        out_specs=[pl.BlockSpec((tm, D), lambda i: (i, 0)), pl.BlockSpec((tm, D), lambda i: (i, 0))],
        out_shape=[_sds((s_len, D), BF16), _sds((s_len, D), F32)],
        scratch_shapes=[pltpu.VMEM((tm + HALO, D), F32)],
        compiler_params=_params(("arbitrary",)),
    )(p, conv_wb, conv_b, ln_g, ln_b)


def _branch_b_fwd(p, wt, bias_full, ln_g, ln_b):
    s_len = p.shape[0]
    tm = min(256, s_len)
    n_i = s_len // tm

    def body(p_ref, wt_ref, bias_ref, g_ref, b_ref, yb_ref, vb, sbuf):
        def norm(r0):
            gv = _gelu(p_ref[pl.ds(r0, PACK), pl.ds(D, D)].astype(F32))
            mu = _rowmean(gv)
            vc = gv - mu
            rstd = lax.rsqrt(_rowmean(vc * vc) + EPS)
            vb[pl.ds(r0, PACK), :] = ((vc * rstd) * g_ref[...] + b_ref[...]).astype(BF16)
        _strips(tm, PACK,norm)

        for ck in range(tm // CHUNK):
            for h in range(HEADS):
                blk = (pl.ds(ck * CHUNK, CHUNK), pl.ds(h * LANE, LANE))
                sbuf[blk] = jnp.dot(wt_ref[h], vb[blk], preferred_element_type=F32) + bias_ref[:, pl.ds(h * LANE, LANE)]

        def gate(r0):
            u = _gelu(p_ref[pl.ds(r0, PACK), pl.ds(0, D)].astype(F32))
            z = p_ref[pl.ds(r0, PACK), pl.ds(2 * D, D)].astype(F32)
            yb_ref[pl.ds(r0, PACK), :] = (u * sbuf[pl.ds(r0, PACK), :] * (z * _sigmoid(z))).astype(BF16)
        _strips(tm, PACK,gate)

    return pl.pallas_call(
        body, name="branch_b_fwd", grid=(n_i,),
        in_specs=[pl.BlockSpec((tm, 3 * D), lambda i: (i, 1)),
                  pl.BlockSpec((HEADS, CHUNK, CHUNK), lambda i: (0, 0, 0)),
                  pl.BlockSpec((CHUNK, D), lambda i: (0, 0)), _vec_spec(1), _vec_spec(1)],
        out_specs=pl.BlockSpec((tm, D), lambda i: (i, 0)),
        out_shape=_sds((s_len, D), BF16),
        scratch_shapes=[pltpu.VMEM((tm, D), BF16), pltpu.VMEM((tm, D), F32)],
        compiler_params=_params(("arbitrary",)),
    )(p, wt, bias_full, ln_g, ln_b)


def _dot_t(a, b):
    return lax.dot_general(a, b, (((1,), (1,)), ((), ())), preferred_element_type=F32)


def _out_proj(p, ya_in, yb_in, x, target, gate, g_final, w_co, w_so, w_o):
    s_len = x.shape[0]
    tm = min(256, s_len)
    n_i = s_len // tm

    def body(pg_ref, ya_ref, yb_ref, x_ref, t_ref, gate_ref, gf_ref, wco_ref, wso_ref, wo_ref,
             dx2_ref, dya_ref, dyb_ref, dp_ref, mb_ref, dob_ref, dyab_ref, dybb_ref, sums_ref):
        @pl.when(pl.program_id(0) == 0)
        def _():
            sums_ref[...] = jnp.zeros((SUB, D), F32)

        y_a = jnp.dot(ya_ref[...], wco_ref[...], preferred_element_type=F32)
        y_b = jnp.dot(yb_ref[...], wso_ref[...], preferred_element_type=F32)
        ga = _sigmoid(pg_ref[:, pl.ds(0, D)].astype(F32))
        gb = _sigmoid(pg_ref[:, pl.ds(D, D)].astype(F32))
        mb = (ga * y_a + gb * y_b).astype(BF16)
        mb_ref[...] = mb
        o = jnp.dot(mb, wo_ref[...], preferred_element_type=F32)
        x2 = x_ref[...] + gate_ref[...] * o
        r2 = lax.rsqrt(_rowmean(x2 * x2) + EPS)
        xh = x2 * r2
        e = xh * gf_ref[...] - t_ref[...]
        dy = e * (1.0 / D)
        dxh = dy * gf_ref[...]
        dx2 = r2 * (dxh - xh * _rowmean(dxh * xh))
        dx2_ref[...] = dx2
        sums_ref[pl.ds(0, 1), :] += jnp.sum(dy * xh, axis=0, keepdims=True)
        sums_ref[pl.ds(1, 1), :] += jnp.sum(dx2 * o, axis=0, keepdims=True)
        sums_ref[pl.ds(2, 1), :] += jnp.sum(e * e, axis=0, keepdims=True) * (0.5 / D)
        dob = (gate_ref[...] * dx2).astype(BF16)
        dob_ref[...] = dob
        dm = _dot_t(dob, wo_ref[...])
        dy_a = (ga * dm).astype(BF16)
        dy_b = (gb * dm).astype(BF16)
        dyab_ref[...] = dy_a
        dybb_ref[...] = dy_b
        dp_ref[:, pl.ds(0, D)] = (dm * y_a * ga * (1.0 - ga)).astype(BF16)
        dp_ref[:, pl.ds(D, D)] = (dm * y_b * gb * (1.0 - gb)).astype(BF16)
        dya_ref[...] = _dot_t(dy_a, wco_ref[...])
        dyb_ref[...] = _dot_t(dy_b, wso_ref[...])

    tile = pl.BlockSpec((tm, D), lambda i: (i, 0))
    wspec = pl.BlockSpec((D, D), lambda i: (0, 0))
    return pl.pallas_call(
        body, name="out_proj", grid=(n_i,),
        in_specs=[pl.BlockSpec((tm, 2 * D), lambda i: (i, 3)), tile, tile, tile, tile, _vec_spec(1), _vec_spec(1),
                  wspec, wspec, wspec],
        out_specs=[tile, tile, tile, pl.BlockSpec((tm, 2 * D), lambda i: (i, 3)), tile, tile, tile, tile,
                   pl.BlockSpec((SUB, D), lambda i: (0, 0))],
        out_shape=[_sds((s_len, D), F32), _sds((s_len, D), F32), _sds((s_len, D), F32), _sds((s_len, N_SEC * D), BF16),
                   _sds((s_len, D), BF16), _sds((s_len, D), BF16), _sds((s_len, D), BF16), _sds((s_len, D), BF16),
                   _sds((SUB, D), F32)],
        compiler_params=_params(("arbitrary",)),
    )(p, ya_in, yb_in, x, target, gate, g_final, w_co, w_so, w_o)


A_STATS_ROWS = 8 + HALO


def _branch_a_bwd(p, y1, dya_in, dp, conv_wb, ln_g, ln_b):
    s_len = p.shape[0]
    tm = min(256, s_len)
    n_i = s_len // tm
    rows = 32
    n_out = rows // SUB

    def tile_of(i):
        return n_i - 1 - i

    def body(p_ref, y1_ref, dya_ref, dp_in, wb_ref, g_ref, b_ref, dp_ref, st_ref, dybuf, acc8, tapacc):
        del dp_in
        i = pl.program_id(0)

        @pl.when(i == 0)
        def _():
            dybuf[pl.ds(tm, HALO), :] = jnp.zeros((HALO, D), F32)
            st_ref[...] = jnp.zeros((A_STATS_ROWS, D), F32)
            acc8[...] = jnp.zeros((3 * PACK, D), F32)
            tapacc[...] = jnp.zeros((CONV_K * SUB, D), F32)

        def norm_bwd(r0):
            y1 = y1_ref[pl.ds(r0, PACK), :]
            mu = _rowmean(y1)
            yc = y1 - mu
            rstd = lax.rsqrt(_rowmean(yc * yc) + EPS)
            n1 = yc * rstd
            l1 = n1 * g_ref[...] + b_ref[...]
            sg = _sigmoid(l1)
            z = p_ref[pl.ds(r0, PACK), pl.ds(2 * D, D)].astype(F32)
            sz = _sigmoid(z)
            dya = dya_ref[pl.ds(r0, PACK), :]
            dp_ref[pl.ds(r0, PACK), pl.ds(2 * D, D)] = (dya * (l1 * sg) * _dsilu(z, sz)).astype(BF16)
            dl1 = dya * (z * sz) * _dsilu(l1, sg)
            acc8[pl.ds(0, PACK), :] += dl1 * n1
            acc8[pl.ds(PACK, PACK), :] += dl1
            dn1 = dl1 * g_ref[...]
            dy1 = rstd * (dn1 - _rowmean(dn1) - n1 * _rowmean(dn1 * n1))
            acc8[pl.ds(2 * PACK, PACK), :] += dy1
            dybuf[pl.ds(r0, PACK), :] = dy1
        _strips(tm, PACK,norm_bwd)

        def conv_bwd(r0):
            for lt in range(D // LANE):
                lanes = pl.ds(lt * LANE, LANE)
                glanes = pl.ds(D + lt * LANE, LANE)
                win = dybuf[pl.ds(r0, rows + HALO), lanes]
                sg16, a16 = [], []
                for h in range(rows // PACK):
                    rr = pl.ds(r0 + h * PACK, PACK)
                    s = _sigmoid(p_ref[rr, glanes].astype(F32))
                    sg16.append(s)
                    a16.append(p_ref[rr, lanes].astype(F32) * s)
                a = [a16[m // 2][(m % 2) * SUB:(m % 2 + 1) * SUB, :] for m in range(n_out)]
                da = [jnp.zeros((SUB, LANE), F32) for _ in range(n_out)]
                for phase in range(SUB):
                    offs = [o for o in range(CONV_K) if o % SUB == phase]
                    q_max = max(o // SUB for o in offs)
                    sh = win[phase:phase + (n_out + q_max) * SUB, :]
                    for o in offs:
                        k, q = CONV_K - 1 - o, o // SUB
                        w = wb_ref[pl.ds(k * SUB, SUB), lanes]
                        part = None
                        for m in range(n_out):
                            s = sh[(m + q) * SUB:(m + q + 1) * SUB, :]
                            da[m] = da[m] + w * s
                            part = a[m] * s if part is None else part + a[m] * s
                        tapacc[pl.ds(k * SUB, SUB), lanes] += part
                for h in range(rows // PACK):
                    rr = pl.ds(r0 + h * PACK, PACK)
                    da16 = jnp.concatenate(da[2 * h:2 * h + 2], axis=0)
                    dp_ref[rr, lanes] = (da16 * sg16[h]).astype(BF16)
                    dp_ref[rr, glanes] = (da16 * a16[h] * (1.0 - sg16[h])).astype(BF16)
        _strips(tm, rows, conv_bwd)

        dybuf[pl.ds(tm, HALO), :] = dybuf[pl.ds(0, HALO), :]

        @pl.when(i == n_i - 1)
        def _():
            for j in range(3):
                st_ref[pl.ds(j, 1), :] = jnp.sum(acc8[pl.ds(j * PACK, PACK), :], axis=0, keepdims=True)
            for k in range(CONV_K):
                st_ref[pl.ds(SUB + k, 1), :] = jnp.sum(tapacc[pl.ds(k * SUB, SUB), :], axis=0, keepdims=True)

    return pl.pallas_call(
        body, name="branch_a_bwd", grid=(n_i,),
        in_specs=[pl.BlockSpec((tm, 3 * D), lambda i: (tile_of(i), 0)),
                  pl.BlockSpec((tm, D), lambda i: (tile_of(i), 0)),
                  pl.BlockSpec((tm, D), lambda i: (tile_of(i), 0)),
                  pl.BlockSpec(memory_space=pl.ANY),
                  pl.BlockSpec((CONV_K * SUB, D), lambda i: (0, 0)), _vec_spec(1), _vec_spec(1)],
        out_specs=[pl.BlockSpec((tm, 3 * D), lambda i: (tile_of(i), 0)),
                   pl.BlockSpec((A_STATS_ROWS, D), lambda i: (0, 0))],
        out_shape=[_sds(dp.shape, BF16), _sds((A_STATS_ROWS, D), F32)],
        scratch_shapes=[pltpu.VMEM((tm + HALO, D), F32), pltpu.VMEM((3 * PACK, D), F32), pltpu.VMEM((CONV_K * SUB, D), F32)],
        input_output_aliases={3: 0},
        compiler_params=_params(("arbitrary",)),
    )(p, y1, dya_in, dp, conv_wb, ln_g, ln_b)


def _branch_b_bwd(p, dyb_in, dp, wt, wtt, bias_full, ln_g, ln_b):
    s_len = p.shape[0]
    tm = min(256, s_len)
    n_i = s_len // tm

    def body(p_ref, dyb_ref, dp_in, wt_ref, wtt_ref, bias_ref, g_ref, b_ref, dp_ref, st_ref, gbt_ref, gw_ref,
             vb, n2buf, rstdbuf, sbuf, dsb, dvbuf, acc8, gb_ref, dgbuf):
        del dp_in
        i = pl.program_id(0)

        @pl.when(i == 0)
        def _():
            st_ref[...] = jnp.zeros((SUB, D), F32)
            gbt_ref[...] = jnp.zeros((CHUNK, LANE), F32)
            gb_ref[...] = jnp.zeros((CHUNK, D), F32)
            gw_ref[...] = jnp.zeros((HEADS, CHUNK, CHUNK), F32)
            acc8[...] = jnp.zeros((2 * PACK, D), F32)

        def norm(r0):
            gv, dgv = _gelu_and_grad(p_ref[pl.ds(r0, PACK), pl.ds(D, D)].astype(F32))
            dgbuf[pl.ds(r0, PACK), :] = dgv
            mu = _rowmean(gv)
            vc = gv - mu
            rstd = lax.rsqrt(_rowmean(vc * vc) + EPS)
            n2 = vc * rstd
            n2buf[pl.ds(r0, PACK), :] = n2
            rstdbuf[pl.ds(r0, PACK), :] = jnp.broadcast_to(rstd, (PACK, LANE))
            vb[pl.ds(r0, PACK), :] = (n2 * g_ref[...] + b_ref[...]).astype(BF16)
        _strips(tm, PACK,norm)

        for ck in range(tm // CHUNK):
            for h in range(HEADS):
                blk = (pl.ds(ck * CHUNK, CHUNK), pl.ds(h * LANE, LANE))
                sbuf[blk] = jnp.dot(wt_ref[h], vb[blk], preferred_element_type=F32) + bias_ref[:, pl.ds(h * LANE, LANE)]

        def gate_bwd(r0):
            pu = p_ref[pl.ds(r0, PACK), pl.ds(0, D)].astype(F32)
            u, du = _gelu_and_grad(pu)
            z = p_ref[pl.ds(r0, PACK), pl.ds(2 * D, D)].astype(F32)
            sg = _sigmoid(z)
            sz = z * sg
            s = sbuf[pl.ds(r0, PACK), :]
            dyb = dyb_ref[pl.ds(r0, PACK), :]
            ds = dyb * u * sz
            dsb[pl.ds(r0, PACK), :] = ds.astype(BF16)
            gb_ref[pl.ds(pl.multiple_of(r0 % CHUNK, PACK), PACK), :] += ds
            dp_ref[pl.ds(r0, PACK), pl.ds(0, D)] = (dyb * s * sz * du).astype(BF16)
            dp_ref[pl.ds(r0, PACK), pl.ds(2 * D, D)] = (dyb * u * s * _dsilu(z, sg)).astype(BF16)
        _strips(tm, PACK,gate_bwd)

        for ck in range(tm // CHUNK):
            for h in range(HEADS):
                blk = (pl.ds(ck * CHUNK, CHUNK), pl.ds(h * LANE, LANE))
                d_s = dsb[blk]
                dvbuf[blk] = jnp.dot(wtt_ref[h], d_s, preferred_element_type=F32)
                gw_ref[h] += _dot_t(d_s, vb[blk])

        def norm_bwd(r0):
            dv = dvbuf[pl.ds(r0, PACK), :]
            n2 = n2buf[pl.ds(r0, PACK), :]
            rstd = rstdbuf[pl.ds(r0, PACK), pl.ds(0, 1)]
            acc8[pl.ds(0, PACK), :] += dv * n2
            acc8[pl.ds(PACK, PACK), :] += dv
            dn2 = dv * g_ref[...]
            dgv = rstd * (dn2 - _rowmean(dn2) - n2 * _rowmean(dn2 * n2))
            dp_ref[pl.ds(r0, PACK), pl.ds(D, D)] = (dgv * dgbuf[pl.ds(r0, PACK), :]).astype(BF16)
        _strips(tm, PACK,norm_bwd)

        @pl.when(i == n_i - 1)
        def _():
            for j in range(2):
                st_ref[pl.ds(j, 1), :] = jnp.sum(acc8[pl.ds(j * PACK, PACK), :], axis=0, keepdims=True)
            row = lax.broadcasted_iota(jnp.int32, (CHUNK, CHUNK), 0)
            col = lax.broadcasted_iota(jnp.int32, (CHUNK, CHUNK), 1)
            for h in range(HEADS):
                gw_ref[h] = jnp.where(row >= col, gw_ref[h], 0.0)
            lane = lax.broadcasted_iota(jnp.int32, (CHUNK, LANE), 1)
            gbt = jnp.zeros((CHUNK, LANE), F32)
            for h in range(HEADS):
                gbt = jnp.where(lane == h, jnp.sum(gb_ref[:, pl.ds(h * LANE, LANE)], axis=1, keepdims=True), gbt)
            gbt_ref[...] = gbt

    wspec = pl.BlockSpec((HEADS, CHUNK, CHUNK), lambda i: (0, 0, 0))
    return pl.pallas_call(
        body, name="branch_b_bwd", grid=(n_i,),
        in_specs=[pl.BlockSpec((tm, 3 * D), lambda i: (i, 1)), pl.BlockSpec((tm, D), lambda i: (i, 0)),
                  pl.BlockSpec(memory_space=pl.ANY), wspec, wspec,
                  pl.BlockSpec((CHUNK, D), lambda i: (0, 0)), _vec_spec(1), _vec_spec(1)],
        out_specs=[pl.BlockSpec((tm, 3 * D), lambda i: (i, 1)), pl.BlockSpec((SUB, D), lambda i: (0, 0)),
                   pl.BlockSpec((CHUNK, LANE), lambda i: (0, 0)), wspec],
        out_shape=[_sds(dp.shape, BF16), _sds((SUB, D), F32), _sds((CHUNK, LANE), F32), _sds((HEADS, CHUNK, CHUNK), F32)],
        scratch_shapes=[pltpu.VMEM((tm, D), BF16), pltpu.VMEM((tm, D), F32), pltpu.VMEM((tm, LANE), F32),
                        pltpu.VMEM((tm, D), F32), pltpu.VMEM((tm, D), BF16), pltpu.VMEM((tm, D), F32),
                        pltpu.VMEM((2 * PACK, D), F32), pltpu.VMEM((CHUNK, D), F32), pltpu.VMEM((tm, D), F32)],
        input_output_aliases={2: 0},
        compiler_params=_params(("arbitrary",)),
    )(p, dyb_in, dp, wt, wtt, bias_full, ln_g, ln_b)


def _in_proj_bwd(dp, wg_in, x, dx2, shift, scale, g_pre):
    del shift
    s_len = x.shape[0]
    tm = min(512, s_len)
    n_i = s_len // tm
    wn = wg_in.shape[2]

    def body(dp0, dp1, dp2, dp3, w_ref, x_ref, dx2_ref, sc_ref, g_ref, gx_ref, st_ref, acc, acc8):
        i = pl.program_id(0)

        @pl.when(i == 0)
        def _():
            st_ref[...] = jnp.zeros((SUB, D), F32)
            acc8[...] = jnp.zeros((3 * PACK, D), F32)

        dh = _dot_t(dp0[...], w_ref[0])
        for j, dp_ref in enumerate((dp1, dp2, dp3), start=1):
            dh = dh + _dot_t(dp_ref[...], w_ref[j])
        acc[...] = dh

        def strip(r0):
            xs = x_ref[pl.ds(r0, PACK), :]
            r = lax.rsqrt(_rowmean(xs * xs) + EPS)
            xn = xs * r
            dhs = acc[pl.ds(r0, PACK), :]
            acc8[pl.ds(0, PACK), :] += dhs
            acc8[pl.ds(PACK, PACK), :] += dhs * (xn * g_ref[...])
            dhp = dhs * (1.0 + sc_ref[...])
            acc8[pl.ds(2 * PACK, PACK), :] += dhp * xn
            dxn = dhp * g_ref[...]
            gx_ref[pl.ds(r0, PACK), :] = dx2_ref[pl.ds(r0, PACK), :] + r * (dxn - xn * _rowmean(dxn * xn))
        _strips(tm, PACK, strip)

        @pl.when(i == n_i - 1)
        def _():
            for k in range(3):
                st_ref[pl.ds(k, 1), :] = jnp.sum(acc8[pl.ds(k * PACK, PACK), :], axis=0, keepdims=True)

    tile = pl.BlockSpec((tm, D), lambda i: (i, 0))
    return pl.pallas_call(
        body, name="in_proj_bwd", grid=(n_i,),
        in_specs=[pl.BlockSpec((tm, wn), functools.partial(lambda j, i: (i, j), j)) for j in range(N_CHIP)] + [
                  pl.BlockSpec((N_CHIP, D, wn), lambda i: (0, 0, 0), pipeline_mode=pl.Buffered(1)),
                  tile, tile, _vec_spec(1), _vec_spec(1)],
        out_specs=[tile, pl.BlockSpec((SUB, D), lambda i: (0, 0))],
        out_shape=[_sds((s_len, D), F32), _sds((SUB, D), F32)],
        scratch_shapes=[pltpu.VMEM((tm, D), F32), pltpu.VMEM((3 * PACK, D), F32)],
        compiler_params=_params(("arbitrary",)),
    )(dp, dp, dp, dp, wg_in, x, dx2, scale, g_pre)


def _grad_matmul(a, b, name):
    s_len, n = b.shape
    cb = min(2 * D, n)
    tn = 512
    per = cb // tn

    def body(a_ref, b_ref, ob_ref):
        ob_ref[0] = lax.dot_general(a_ref[...], b_ref[...], (((0,), (0,)), ((), ())),
                                    preferred_element_type=F32).astype(BF16)

    return pl.pallas_call(
        body, name=name, grid=(n // tn,),
        in_specs=[pl.BlockSpec((s_len, D), lambda j: (0, 0), pipeline_mode=pl.Buffered(1)),
                  pl.BlockSpec((s_len, tn), lambda j: (0, j))],
        out_specs=pl.BlockSpec((1, D, tn), lambda j: (j // per, 0, j % per)),
        out_shape=_sds((n // cb, D, cb), BF16),
        compiler_params=_params(("arbitrary",)),
    )(a, b)


def _local_step(x, target, shift, scale, gate, g_pre, conv_w_full, conv_b, conv_ln_g, conv_ln_b,
                sgu_ln_g, sgu_ln_b, w_sgu, b_sgu, g_final, wg_in, out_shards):
    conv_wb = jnp.repeat(conv_w_full, SUB, axis=0)
    causal = jnp.tril(jnp.ones((CHUNK, CHUNK), dtype=bool))
    wt = jnp.where(causal[None], w_sgu, 0.0).astype(BF16)
    wtt = jnp.swapaxes(wt, 1, 2)
    bias_full = jnp.repeat(b_sgu.T, LANE, axis=1)

    p, hb, gathered = _in_proj_gather(x, shift, scale, g_pre, wg_in, out_shards)
    w_co, w_so, w_o = (g.reshape(D, D) for g in gathered)
    ya_in, y1 = _branch_a_fwd(p, conv_wb, conv_b, conv_ln_g, conv_ln_b)
    yb_in = _branch_b_fwd(p, wt, bias_full, sgu_ln_g, sgu_ln_b)
    dx2, dya_in, dyb_in, dp, mb, dob, dyab, dybb, sums_o = _out_proj(
        p, ya_in, yb_in, x, target, gate, g_final, w_co, w_so, w_o)
    dp, st_a = _branch_a_bwd(p, y1, dya_in, dp, conv_wb, conv_ln_g, conv_ln_b)
    dp, st_b, gbt, gws = _branch_b_bwd(p, dyb_in, dp, wt, wtt, bias_full, sgu_ln_g, sgu_ln_b)
    grad_x, st_i = _in_proj_bwd(dp, wg_in, x, dx2, shift, scale, g_pre)
    gw_o = _grad_matmul(mb, dob, "grad_w_o")
    gw_co = _grad_matmul(ya_in, dyab, "grad_w_conv_out")
    gw_so = _grad_matmul(yb_in, dybb, "grad_w_sgu_out")
    return dict(
        grad_x=grad_x, loss_cols=sums_o[2:3], g_final=sums_o[0:1], d_gate=sums_o[1:2],
        d_shift=st_i[0:1], d_scale=st_i[1:2], g_pre=st_i[2:3],
        conv_ln_g=st_a[0:1], conv_ln_b=st_a[1:2], conv_b=st_a[2:3], conv_w=st_a[SUB:SUB + CONV_K],
        sgu_ln_g=st_b[0:1], sgu_ln_b=st_b[1:2], b_sgu=gbt[:, :HEADS].T, w_sgu=gws,
        hb=hb, dp=dp, w_o=gw_o, w_conv_out=gw_co, w_sgu_out=gw_so)


ANY_SPEC = pl.BlockSpec(memory_space=pl.ANY)
VMEM_SPEC = pl.BlockSpec(memory_space=pltpu.VMEM)


def _place():
    return lax.axis_index("x"), lax.axis_index("y"), lax.axis_index("c")


def _peer(k):
    x, y, c = _place()
    return (1 - x if k & 4 else x, 1 - y if k & 2 else y, 1 - c if k & 1 else c)


def _dev_of(p):
    return 4 * p[0] + 2 * p[1] + p[2]


def _chip_of(p):
    return 2 * p[0] + p[1]


def _rdma(src, dst, send_sem, recv_sem, to):
    return pltpu.make_async_remote_copy(src_ref=src, dst_ref=dst, send_sem=send_sem, recv_sem=recv_sem,
                                        device_id=to, device_id_type=MESH)


CHIP_PEERS = (2, 4, 6)
ALL_PEERS = tuple(range(1, N_DEV))
SIBLING = 1


def _setup_comm(c8, w_ada_s, b_ada_s, convw_s, shards):
    n_mod = w_ada_s.shape[1]
    rows = SUB * N_DEV
    n = len(shards)

    def body(c8_ref, wada_ref, bada_ref, cw_ref, *refs):
        ins, (call_ref, mod_ref, cwall_ref), outs = refs[:n], refs[n:n + 3], refs[n + 3:2 * n + 3]
        csend, crecv, wsend, wrecv, msend, mrecv = refs[2 * n + 3:2 * n + 9]
        gather_a, gather_b, gather_c = _gather_phases([s.shape[0] for s in shards], ins, outs, refs[2 * n + 9:])
        gather_a()
        me = _place()
        dev, chip = _dev_of(me), _chip_of(me)

        def c_rows(d):
            return call_ref.at[pl.ds(pl.multiple_of(d * SUB, SUB), SUB), :]

        call_ref[pl.ds(pl.multiple_of(dev * SUB, SUB), SUB), :] = c8_ref[...]
        cwall_ref[chip] = cw_ref[...]
        c_out = [_rdma(c8_ref, c_rows(dev), csend.at[k], crecv.at[k], _peer(k)) for k in ALL_PEERS]
        w_out = [_rdma(cw_ref, cwall_ref.at[chip], wsend.at[k], wrecv.at[k], _peer(k)) for k in CHIP_PEERS]
        for cp in c_out + w_out:
            cp.start()
        for k in ALL_PEERS:
            _rdma(c8_ref, c_rows(_dev_of(_peer(k))), csend.at[k], crecv.at[k], _peer(k)).wait_recv()
        part = jnp.dot(call_ref[...].astype(BF16), wada_ref[...].astype(BF16), preferred_element_type=F32) + bada_ref[...]
        mod_ref[chip] = part
        m_out = [_rdma(mod_ref.at[chip], mod_ref.at[chip], msend.at[k], mrecv.at[k], _peer(k)) for k in CHIP_PEERS]
        for cp in m_out:
            cp.start()
        for k in CHIP_PEERS:
            pc = _chip_of(_peer(k))
            _rdma(cw_ref, cwall_ref.at[pc], wsend.at[k], wrecv.at[k], _peer(k)).wait_recv()
            _rdma(mod_ref.at[pc], mod_ref.at[pc], msend.at[k], mrecv.at[k], _peer(k)).wait_recv()
        for cp in c_out + w_out + m_out:
            cp.wait_send()
        gather_b()
        gather_c()

    res = pl.pallas_call(
        body, name="setup_comm",
        in_specs=[VMEM_SPEC] * (4 + n), out_specs=[VMEM_SPEC] * (3 + n),
        out_shape=([_sds((rows, D), F32), _sds((N_CHIP, rows, n_mod), F32), _sds((N_CHIP,) + convw_s.shape, F32)]
                   + [_sds((N_CHIP,) + s.shape, s.dtype) for s in shards]),
        scratch_shapes=([pltpu.SemaphoreType.DMA((N_DEV,))] * 6
                        + [pltpu.SemaphoreType.DMA((n,))] + [pltpu.SemaphoreType.DMA((n, len(CHIP_PEERS)))] * 4),
        compiler_params=_params(),
    )(c8, w_ada_s, b_ada_s, convw_s, *shards)
    return res[0], res[1], res[2], res[3:]


def _gather_phases(row_counts, ins, dsts, sems):
    n = len(row_counts)
    lsem, isend, irecv, dsend, drecv = sems
    me = _place()
    chip, c = _chip_of(me), me[2]

    def half(t, which):
        hr = row_counts[t] // 2
        return pl.ds(pl.multiple_of(which * hr, hr), hr)

    def local(t):
        return pltpu.make_async_copy(ins[t], dsts[t].at[chip], lsem.at[t])

    def to_chip(t, j):
        return _rdma(ins[t].at[half(t, c)], dsts[t].at[chip, half(t, c)], isend.at[t, j], irecv.at[t, j], _peer(CHIP_PEERS[j]))

    def landed(t, j, which):
        return dsts[t].at[_chip_of(_peer(CHIP_PEERS[j])), half(t, which)]

    def to_sibling(t, j):
        return _rdma(landed(t, j, c), landed(t, j, c), dsend.at[t, j], drecv.at[t, j], _peer(SIBLING))

    pairs = [(t, j) for t in range(n) for j in range(len(CHIP_PEERS))]

    def phase_a():
        for t in range(n):
            local(t).start()
        for t, j in pairs:
            to_chip(t, j).start()

    def phase_b():
        for t, j in pairs:
            _rdma(landed(t, j, c), landed(t, j, c), isend.at[t, j], irecv.at[t, j], _peer(CHIP_PEERS[j])).wait_recv()
            to_sibling(t, j).start()

    def phase_c():
        for t, j in pairs:
            _rdma(landed(t, j, 1 - c), landed(t, j, 1 - c), dsend.at[t, j], drecv.at[t, j], _peer(SIBLING)).wait_recv()
        for t, j in pairs:
            to_chip(t, j).wait_send()
            to_sibling(t, j).wait_send()
        for t in range(n):
            local(t).wait()

    return phase_a, phase_b, phase_c


def _in_proj_gather(x, shift, scale, g_pre, wg_in, shards):
    s_len = x.shape[0]
    tm = min(256, s_len)
    n_i = s_len // tm
    wn = wg_in.shape[2]
    n = len(shards)

    def body(x_ref, sh_ref, sc_ref, g_ref, w_ref, *refs):
        ins, p_ref, hb_ref, outs = refs[:n], refs[n], refs[n + 1], refs[n + 2:2 * n + 2]
        gath, sems = refs[2 * n + 2:3 * n + 2], refs[3 * n + 2:]
        phases = _gather_phases([s.shape[0] for s in shards], ins, gath, sems)
        i = pl.program_id(0)
        for step, phase in zip((0, n_i // 2, n_i - 1), phases):
            pl.when(i == step)(phase)

        @pl.when(i == n_i - 1)
        def _():
            for t in range(n):
                outs[t][...] = gath[t][...]

        def strip(r0):
            xs = x_ref[pl.ds(r0, PACK), :]
            r = lax.rsqrt(_rowmean(xs * xs) + EPS)
            h = (xs * r) * g_ref[...] * (1.0 + sc_ref[...]) + sh_ref[...]
            hb_ref[pl.ds(r0, PACK), :] = h.astype(BF16)
        _strips(tm, PACK, strip)
        hb = hb_ref[...]
        for j in range(N_CHIP):
            p_ref[:, pl.ds(j * wn, wn)] = jnp.dot(hb, w_ref[j], preferred_element_type=F32).astype(BF16)

    res = pl.pallas_call(
        body, name="in_proj", grid=(n_i,),
        in_specs=[pl.BlockSpec((tm, D), lambda i: (i, 0)), _vec_spec(1), _vec_spec(1), _vec_spec(1),
                  pl.BlockSpec((N_CHIP, D, wn), lambda i: (0, 0, 0), pipeline_mode=pl.Buffered(1))] + [VMEM_SPEC] * n,
        out_specs=[pl.BlockSpec((tm, N_CHIP * wn), lambda i: (i, 0)), pl.BlockSpec((tm, D), lambda i: (i, 0))] + [VMEM_SPEC] * n,
        out_shape=([_sds((s_len, N_SEC * D), BF16), _sds((s_len, D), BF16)]
                   + [_sds((N_CHIP,) + s.shape, s.dtype) for s in shards]),
        scratch_shapes=([pltpu.VMEM((N_CHIP,) + s.shape, s.dtype) for s in shards]
                        + [pltpu.SemaphoreType.DMA((n,))] + [pltpu.SemaphoreType.DMA((n, len(CHIP_PEERS)))] * 4),
        compiler_params=_params(("arbitrary",)),
    )(x, shift, scale, g_pre, wg_in, *shards)
    return res[0], res[1], res[2:]


def _reduce_scatter(grads, name):
    n = len(grads)
    shapes = [g.shape[2:] for g in grads]

    def body(*refs):
        ins, outs, scratch = refs[:n], refs[n:2 * n], refs[2 * n:]
        for phase in _reduce_phases(shapes, ins, outs, scratch[:n], scratch[n:2 * n], scratch[2 * n:3 * n], scratch[3 * n:]):
            phase()

    return pl.pallas_call(
        body, name=name,
        in_specs=[VMEM_SPEC] * n, out_specs=[VMEM_SPEC] * n,
        out_shape=[_sds((2,) + s, F32) for s in shapes],
        scratch_shapes=_reduce_scratch(shapes),
        compiler_params=_params(),
    )(*grads)


def _reduce_phases(shapes, ins, outs, pbufs, rbufs, accs, sems):
    n = len(shapes)
    psend, precv, csend, crecv, fsend, frecv = sems
    me = _place()
    chip, c = _chip_of(me), me[2]
    sib = _peer(SIBLING)

    def to_sibling(t, d):
        return _rdma(ins[t].at[d, 1 - c], pbufs[t].at[d], psend.at[t, d], precv.at[t, d], sib)

    def to_chip(t, j):
        return _rdma(pbufs[t].at[jnp.bitwise_xor(chip, j)], rbufs[t].at[j - 1], csend.at[t, j], crecv.at[t, j], _peer(2 * j))

    def finished(t):
        return _rdma(outs[t].at[c], outs[t].at[c], fsend.at[t], frecv.at[t], sib)

    def phase_a():
        for t in range(n):
            for d in range(N_CHIP):
                to_sibling(t, d).start()

    def phase_b():
        for j in (1, 2, 3, 0):
            d = jnp.bitwise_xor(chip, j)
            for t in range(n):
                to_sibling(t, d).wait_recv()

                def pair_sum(r0, t=t, d=d, j=j):
                    rows = pl.ds(r0, PACK)
                    s = ins[t][d, c, rows, :].astype(F32) + pbufs[t][d, rows, :].astype(F32)
                    if j == 0:
                        accs[t][rows, :] = s
                    else:
                        pbufs[t][d, rows, :] = s.astype(BF16)
                _strips(shapes[t][0], PACK, pair_sum)
                if j:
                    to_chip(t, j).start()

    def phase_c():
        for t in range(n):
            for j in (1, 2, 3):
                blk = rbufs[t].at[j - 1]
                _rdma(blk, blk, csend.at[t, j], crecv.at[t, j], _peer(2 * j)).wait_recv()

            def total(r0, t=t):
                rows = pl.ds(r0, PACK)
                s = accs[t][rows, :] + rbufs[t][0, rows, :].astype(F32)
                s = s + rbufs[t][1, rows, :].astype(F32)
                outs[t][c, rows, :] = s + rbufs[t][2, rows, :].astype(F32)
            _strips(shapes[t][0], PACK, total)
            finished(t).start()

    def phase_d():
        for t in range(n):
            blk = outs[t].at[1 - c]
            _rdma(blk, blk, fsend.at[t], frecv.at[t], sib).wait_recv()
        for t in range(n):
            for d in range(N_CHIP):
                to_sibling(t, d).wait_send()
            for j in (1, 2, 3):
                to_chip(t, j).wait_send()
            finished(t).wait_send()

    return phase_a, phase_b, phase_c, phase_d


def _sum_small_phases(ins, outs, pbufs, buf4s, sems):
    n = len(ins)
    psend, precv, send, recv = sems
    chip = _chip_of(_place())

    def swap(t):
        return _rdma(ins[t], pbufs[t], psend.at[t], precv.at[t], _peer(SIBLING))

    def to_chip(t, k):
        return _rdma(buf4s[t].at[chip], buf4s[t].at[chip], send.at[t, k], recv.at[t, k], _peer(k))

    def phase_a():
        for t in range(n):
            swap(t).start()

    def phase_b():
        for t in range(n):
            swap(t).wait()
            buf4s[t][chip] = ins[t][...] + pbufs[t][...]
            for k in CHIP_PEERS:
                to_chip(t, k).start()

    def phase_c():
        for t in range(n):
            for k in CHIP_PEERS:
                blk = buf4s[t].at[_chip_of(_peer(k))]
                _rdma(blk, blk, send.at[t, k], recv.at[t, k], _peer(k)).wait_recv()
            outs[t][...] = (buf4s[t][0] + buf4s[t][1]) + (buf4s[t][2] + buf4s[t][3])

    def phase_d():
        for t in range(n):
            for k in CHIP_PEERS:
                to_chip(t, k).wait_send()

    return phase_a, phase_b, phase_c, phase_d


def _sum_small_scratch(blobs):
    n = len(blobs)
    return ([pltpu.VMEM(b.shape, F32) for b in blobs] + [pltpu.VMEM((N_CHIP,) + b.shape, F32) for b in blobs]
            + [pltpu.SemaphoreType.DMA((n,))] * 2 + [pltpu.SemaphoreType.DMA((n, N_DEV))] * 2)


def _reduce_scratch(shapes):
    n = len(shapes)
    return ([pltpu.VMEM((N_CHIP,) + s, BF16) for s in shapes] + [pltpu.VMEM((N_CHIP - 1,) + s, BF16) for s in shapes]
            + [pltpu.VMEM(s, F32) for s in shapes]
            + [pltpu.SemaphoreType.DMA((n, N_CHIP))] * 4 + [pltpu.SemaphoreType.DMA((n,))] * 2)


def _grad_matmul_reduce(a, b, name, grads, blobs):
    s_len, n_cols = b.shape
    cb = min(2 * D, n_cols)
    tn = 512
    per = cb // tn
    steps = n_cols // tn
    n, nb = len(grads), len(blobs)
    shapes = [g.shape[2:] for g in grads]
    n_red = len(_reduce_scratch(shapes))

    def body(a_ref, b_ref, *refs):
        ins, bins = refs[:n], refs[n:n + nb]
        ob_ref, outs, bouts = refs[n + nb], refs[n + nb + 1:2 * n + nb + 1], refs[2 * n + nb + 1:2 * (n + nb) + 1]
        scratch = refs[2 * (n + nb) + 1:]
        fulls, red, small = scratch[:n], scratch[n:n + n_red], scratch[n + n_red:]
        phases = _reduce_phases(shapes, ins, fulls, red[:n], red[n:2 * n], red[2 * n:3 * n], red[3 * n:])
        small_phases = _sum_small_phases(bins, bouts, small[:nb], small[nb:2 * nb], small[2 * nb:])
        j = pl.program_id(0)
        for step, phase in zip((0, 2, steps - 2, steps - 1), phases):
            pl.when(j == step)(phase)
        for step, phase in zip((1, 3, steps - 2, steps - 1), small_phases):
            pl.when(j == step)(phase)

        @pl.when(j == steps - 1)
        def _():
            for t in range(n):
                outs[t][...] = fulls[t][...]
        ob_ref[0] = lax.dot_general(a_ref[...], b_ref[...], (((0,), (0,)), ((), ())),
                                    preferred_element_type=F32).astype(BF16)

    res = pl.pallas_call(
        body, name=name, grid=(steps,),
        in_specs=[pl.BlockSpec((s_len, D), lambda j: (0, 0), pipeline_mode=pl.Buffered(1)),
                  pl.BlockSpec((s_len, tn), lambda j: (0, j))] + [VMEM_SPEC] * (n + nb),
        out_specs=[pl.BlockSpec((1, D, tn), lambda j: (j // per, 0, j % per))] + [VMEM_SPEC] * (n + nb),
        out_shape=([_sds((n_cols // cb, D, cb), BF16)] + [_sds((2,) + s, F32) for s in shapes]
                   + [_sds(bl.shape, F32) for bl in blobs]),
        scratch_shapes=[pltpu.VMEM((2,) + s, F32) for s in shapes] + _reduce_scratch(shapes) + _sum_small_scratch(blobs),
        compiler_params=_params(("arbitrary",)),
    )(a, b, *grads, *blobs)
    return res[0], res[1:1 + n], res[1 + n:]


def _adamw_math(w, g, m, v):
    m = ADAM_B1 * m + (1.0 - ADAM_B1) * g
    v = ADAM_B2 * v + (1.0 - ADAM_B2) * (g * g)
    m_hat = m / (1.0 - ADAM_B1 ** ADAM_STEP)
    v_hat = v / (1.0 - ADAM_B2 ** ADAM_STEP)
    delta = -ADAM_LR * (m_hat / (jnp.sqrt(v_hat) + ADAM_EPS) + ADAM_WD * w)
    return delta, m, v


def _row_tile(r, cols):
    if r * cols * 4 <= 2 ** 20:
        return r
    return next(t for t in (512, 256, 128, 64, 32, 16, 8) if r % t == 0 and t * cols * 4 <= 2 ** 20)


def _adamw(w, g, m, v, name):
    r, cols = w.shape
    tr = _row_tile(r, cols)

    def body(w_ref, g_ref, m_ref, v_ref, go_ref, d_ref, nm_ref, nv_ref):
        g = g_ref[...]
        go_ref[...] = g
        d_ref[...], nm_ref[...], nv_ref[...] = _adamw_math(w_ref[...], g, m_ref[...], v_ref[...])

    spec = pl.BlockSpec((tr, cols), lambda i: (i, 0))
    return pl.pallas_call(
        body, name=name, grid=(r // tr,), in_specs=[spec] * 4, out_specs=[spec] * 4,
        out_shape=[_sds((r, cols), F32)] * 4, compiler_params=_params(("arbitrary",)),
    )(w, g, m, v)


def _adamw_ada(w, ct, dm, m, v):
    r, cols = w.shape
    tr = _row_tile(r, cols)

    def body(w_ref, ct_ref, dm_ref, m_ref, v_ref, g_ref, d_ref, nm_ref, nv_ref):
        g = jnp.dot(ct_ref[...], dm_ref[...], preferred_element_type=F32)
        g_ref[...] = g
        d_ref[...], nm_ref[...], nv_ref[...] = _adamw_math(w_ref[...], g, m_ref[...], v_ref[...])

    spec = pl.BlockSpec((tr, cols), lambda i: (i, 0))
    return pl.pallas_call(
        body, name="adamw_ada", grid=(r // tr,),
        in_specs=[spec, pl.BlockSpec((tr, LANE), lambda i: (i, 0)), pl.BlockSpec((LANE, cols), lambda i: (0, 0)), spec, spec],
        out_specs=[spec] * 4, out_shape=[_sds((r, cols), F32)] * 4, compiler_params=_params(("arbitrary",)),
    )(w, ct, dm, m, v)


BLOB_VEC, BLOB_BSGU, BLOB_CONV, BLOB_ADA, BLOB_DMOD, BLOB_LOSS, BLOB_ROWS = 0, 8, 16, 48, 56, 80, 88
N_VEC = 7


def _adamw_small(tot, g_w_sgu, g_conv, params):
    n = len(params)

    def body(*refs):
        tot_ref, gws_ref, gconv_ref = refs[:3]
        wmv = refs[3:3 + 3 * n]
        outs = refs[3 + 3 * n:]
        grads = [tot_ref[pl.ds(BLOB_VEC + i, 1), :] for i in range(N_VEC)]
        grads += [tot_ref[pl.ds(BLOB_BSGU, HEADS), pl.ds(0, CHUNK)], gconv_ref[...], gws_ref[...], tot_ref[pl.ds(BLOB_ADA, 3), :]]
        for i, g in enumerate(grads):
            w_ref, m_ref, v_ref = wmv[3 * i:3 * i + 3]
            d, nm, nv = _adamw_math(w_ref[...], g, m_ref[...], v_ref[...])
            outs[4 * i][...] = g
            outs[4 * i + 1][...] = d
            outs[4 * i + 2][...] = nm
            outs[4 * i + 3][...] = nv

    flat = [a for wmv in params for a in wmv]
    return pl.pallas_call(
        body, name="adamw_small",
        in_specs=[VMEM_SPEC] * (3 + len(flat)), out_specs=[VMEM_SPEC] * (4 * n),
        out_shape=[_sds(wmv[0].shape, F32) for wmv in params for _ in range(4)],
        compiler_params=_params(),
    )(tot, g_w_sgu, g_conv, *flat)


def _set_rows(buf, row, val):
    return lax.dynamic_update_slice(buf, val.astype(F32), (row, 0))


def kernel(x, c, w_ada, b_ada, g_pre, w_in, conv_w, conv_b, conv_ln_g, conv_ln_b, w_conv_out, sgu_ln_g, sgu_ln_b, w_sgu, b_sgu, w_sgu_out, w_o, g_final, loss_target, m_w_ada, m_b_ada, m_g_pre, m_w_in, m_conv_w, m_conv_b, m_conv_ln_g, m_conv_ln_b, m_w_conv_out, m_sgu_ln_g, m_sgu_ln_b, m_w_sgu, m_b_sgu, m_w_sgu_out, m_w_o, m_g_final, v_w_ada, v_b_ada, v_g_pre, v_w_in, v_conv_w, v_conv_b, v_conv_ln_g, v_conv_ln_b, v_w_conv_out, v_sgu_ln_g, v_sgu_ln_b, v_w_sgu, v_b_sgu, v_w_sgu_out, v_w_o, v_g_final):
    me = _place()
    dev, chip = _dev_of(me), _chip_of(me)
    n_ada = w_ada.shape[2]
    conv_cols = conv_w.shape[2]

    b_ada_s = lax.dynamic_slice(b_ada, (0, chip * n_ada), (1, n_ada))
    c_all, mod_all, cw_all, (wg_in,) = _setup_comm(
        jnp.broadcast_to(c, (SUB, D)), w_ada[0], b_ada_s, jnp.pad(conv_w[0], ((0, HALO - CONV_K), (0, 0))),
        [w_in[0].astype(BF16)])
    mod = lax.dynamic_slice(mod_all, (0, dev * SUB, 0), (N_CHIP, 1, n_ada)).reshape(1, 3 * D)
    shift, scale, gate = mod[:, :D], mod[:, D:2 * D], mod[:, 2 * D:]
    conv_w_full = jnp.swapaxes(cw_all, 0, 1).reshape(HALO, D)[:CONV_K]

    loc = _local_step(x[0], loss_target[0], shift, scale, gate, g_pre, conv_w_full, conv_b, conv_ln_g, conv_ln_b,
                      sgu_ln_g, sgu_ln_b, w_sgu[0], b_sgu[0], g_final.reshape(1, D), wg_in,
                      [w_conv_out[0].astype(BF16), w_sgu_out[0].astype(BF16), w_o[0].astype(BF16)])

    d_mod = jnp.concatenate([loc["d_shift"], loc["d_scale"], loc["d_gate"]], axis=0)
    blob = jnp.zeros((BLOB_ROWS, D), F32)
    for i, name in enumerate(["g_pre", "conv_b", "conv_ln_g", "conv_ln_b", "sgu_ln_g", "sgu_ln_b", "g_final"]):
        blob = _set_rows(blob, BLOB_VEC + i, loc[name])
    blob = _set_rows(blob, BLOB_BSGU, loc["b_sgu"])
    blob = _set_rows(blob, BLOB_CONV, loc["conv_w"])
    blob = _set_rows(blob, BLOB_ADA, d_mod)
    blob = lax.dynamic_update_slice(blob, d_mod, (BLOB_DMOD + 3 * dev, 0))
    blob = _set_rows(blob, BLOB_LOSS, loc["loss_cols"])

    big = ["w_in", "w_conv_out", "w_sgu_out", "w_o"]
    contrib_out = [loc[name].reshape(N_CHIP, 2, D // (2 * N_CHIP), D) for name in big[1:]]
    gw_in, full_out, (tot, g_w_sgu) = _grad_matmul_reduce(
        loc["hb"], loc["dp"], "grad_w_in", contrib_out, [blob, loc["w_sgu"].reshape(HEADS * CHUNK, CHUNK)])
    full_in = _reduce_scatter([gw_in.reshape(N_CHIP, 2, D // 2, gw_in.shape[2])], "reduce_w_in")
    g_big = {name: f.reshape(2 * f.shape[1], f.shape[2]) for name, f in zip(big, list(full_in) + list(full_out))}

    loss = jnp.sum(tot[BLOB_LOSS])
    g_conv_s = lax.dynamic_slice(tot, (BLOB_CONV, chip * conv_cols), (CONV_K, conv_cols))
    d_mod_all = tot[BLOB_DMOD:BLOB_DMOD + 3 * N_DEV].reshape(N_DEV, 3 * D)

    ct = jnp.pad(c_all[::SUB].T, ((0, 0), (0, LANE - N_DEV))).astype(BF16)
    dm = jnp.pad(lax.dynamic_slice(d_mod_all, (0, chip * n_ada), (N_DEV, n_ada)), ((0, LANE - N_DEV), (0, 0))).astype(BF16)
    g_ada, d_ada, nm_ada, nv_ada = _adamw_ada(w_ada[0], ct, dm, m_w_ada[0], v_w_ada[0])

    upd = {}
    for name, w, m, v in [("w_in", w_in, m_w_in, v_w_in), ("w_conv_out", w_conv_out, m_w_conv_out, v_w_conv_out),
                          ("w_sgu_out", w_sgu_out, m_w_sgu_out, v_w_sgu_out), ("w_o", w_o, m_w_o, v_w_o)]:
        upd[name] = _adamw(w[0], g_big[name], m[0], v[0], "adamw_" + name)

    def wmv(w, m, v, shape):
        return tuple(a.reshape(shape) for a in (w, m, v))

    small_params = [wmv(w, m, v, (1, D)) for w, m, v in [
        (g_pre, m_g_pre, v_g_pre), (conv_b, m_conv_b, v_conv_b), (conv_ln_g, m_conv_ln_g, v_conv_ln_g),
        (conv_ln_b, m_conv_ln_b, v_conv_ln_b), (sgu_ln_g, m_sgu_ln_g, v_sgu_ln_g), (sgu_ln_b, m_sgu_ln_b, v_sgu_ln_b),
        (g_final, m_g_final, v_g_final)]]
    small_params += [wmv(b_sgu, m_b_sgu, v_b_sgu, (HEADS, CHUNK)), wmv(conv_w, m_conv_w, v_conv_w, (CONV_K, conv_cols)),
                     wmv(w_sgu, m_w_sgu, v_w_sgu, (HEADS * CHUNK, CHUNK)), wmv(b_ada, m_b_ada, v_b_ada, (3, D))]
    small_out = _adamw_small(tot, g_w_sgu, g_conv_s, small_params)

    def leaves(kind):
        vecs = [small_out[4 * i + kind] for i in range(N_VEC)]
        o_b_sgu, o_conv, o_w_sgu, o_b_ada = (small_out[4 * (N_VEC + i) + kind] for i in range(4))
        ada = (g_ada, d_ada, nm_ada, nv_ada)[kind]
        def bigk(name):
            return upd[name][kind][None]
        return [ada[None], o_b_ada.reshape(1, 3 * D), vecs[0], bigk("w_in"), o_conv[None], vecs[1], vecs[2], vecs[3],
                bigk("w_conv_out"), vecs[4], vecs[5], o_w_sgu.reshape(1, HEADS, CHUNK, CHUNK), o_b_sgu[None],
                bigk("w_sgu_out"), bigk("w_o"), vecs[6].reshape(D)]

    return (loss, loc["grad_x"][None], *leaves(0), *leaves(1), *leaves(2), *leaves(3))
```

```python
import functools

import jax
import jax.numpy as jnp
from jax import lax
from jax.experimental import pallas as pl
from jax.experimental.pallas import tpu as pltpu

F32 = jnp.float32
BF16 = jnp.bfloat16
MESH = pl.DeviceIdType.MESH

D = 1024
N_SEC = 8
N_CHIP = 4
N_DEV = 8
EPS = 1e-6
CONV_K = 31
HALO = 32
CHUNK = 128
HEADS = 8
LANE = 128
SUB = 8
PACK = 16
VMEM_LIMIT = 56 * 1024 * 1024

ADAM_LR, ADAM_B1, ADAM_B2, ADAM_EPS, ADAM_WD, ADAM_STEP = 0.001, 0.9, 0.999, 1e-08, 0.01, 10

_SQRT_HALF = 0.7071067811865476
_INV_SQRT_2PI = 0.3989422804014327


def _sds(shape, dtype):
    return jax.ShapeDtypeStruct(shape, dtype)


def _params(sem=None):
    if sem is None:
        return pltpu.CompilerParams(vmem_limit_bytes=VMEM_LIMIT)
    return pltpu.CompilerParams(dimension_semantics=sem, vmem_limit_bytes=VMEM_LIMIT)


def _strips(n_rows, rows, fn):
    def step(s, carry):
        fn(pl.multiple_of(s * rows, rows))
        return carry
    lax.fori_loop(0, n_rows // rows, step, 0)


def _sigmoid(v):
    return 1.0 / (1.0 + jnp.exp(-v))


def _gelu(v):
    return 0.5 * v * (1.0 + lax.erf(v * _SQRT_HALF))


def _gelu_and_grad(v):
    cdf = 0.5 * (1.0 + lax.erf(v * _SQRT_HALF))
    return v * cdf, cdf + v * jnp.exp(-0.5 * v * v) * _INV_SQRT_2PI


def _dsilu(v, sg):
    return sg * (1.0 + v * (1.0 - sg))


def _rowmean(v):
    return jnp.mean(v, axis=-1, keepdims=True)


def _vec_spec(grid_rank):
    zeros = (0, 0)
    if grid_rank == 1:
        return pl.BlockSpec((1, D), lambda i: zeros)
    return pl.BlockSpec((1, D), lambda i, j: zeros)


def _conv_taps(win_ref, r0, lt, weight_of_offset, rows):
    lanes = pl.ds(lt * LANE, LANE)
    win = win_ref[pl.ds(r0, rows + HALO), lanes]
    n_out = rows // SUB
    acc = [jnp.zeros((SUB, LANE), F32) for _ in range(n_out)]
    for phase in range(SUB):
        offs = [o for o in weight_of_offset if o % SUB == phase]
        if not offs:
            continue
        q_max = max(o // SUB for o in offs)
        span = (n_out + q_max) * SUB
        sh = win[phase:phase + span, :]
        for o in offs:
            q = o // SUB
            w = weight_of_offset[o](lanes)
            for m in range(n_out):
                acc[m] = acc[m] + w * sh[(m + q) * SUB:(m + q + 1) * SUB, :]
    return acc


def _branch_a_fwd(p, conv_wb, conv_b, ln_g, ln_b):
    s_len = p.shape[0]
    tm = min(256, s_len)
    n_i = s_len // tm
    rows = 32

    def body(p_ref, wb_ref, cb_ref, g_ref, b_ref, ya_ref, y1_ref, abuf):
        @pl.when(pl.program_id(0) == 0)
        def _():
            abuf[pl.ds(0, HALO), :] = jnp.zeros((HALO, D), F32)

        def glu(r0):
            val = p_ref[pl.ds(r0, PACK), pl.ds(0, D)].astype(F32)
            gl = p_ref[pl.ds(r0, PACK), pl.ds(D, D)].astype(F32)
            abuf[pl.ds(HALO + r0, PACK), :] = val * _sigmoid(gl)
        _strips(tm, PACK,glu)

        taps = {HALO - (CONV_K - 1) + k: (lambda lanes, k=k: wb_ref[pl.ds(k * SUB, SUB), lanes]) for k in range(CONV_K)}

        def conv(r0):
            for lt in range(D // LANE):
                acc = _conv_taps(abuf, r0, lt, taps, rows)
                cb = cb_ref[:, pl.ds(lt * LANE, LANE)]
                for m, v in enumerate(acc):
                    y1_ref[pl.ds(r0 + m * SUB, SUB), pl.ds(lt * LANE, LANE)] = v + cb
        _strips(tm, rows, conv)

        def norm(r0):
            y1 = y1_ref[pl.ds(r0, PACK), :]
            mu = _rowmean(y1)
            yc = y1 - mu
            rstd = lax.rsqrt(_rowmean(yc * yc) + EPS)
            l1 = (yc * rstd) * g_ref[...] + b_ref[...]
            z = p_ref[pl.ds(r0, PACK), pl.ds(2 * D, D)].astype(F32)
            ya_ref[pl.ds(r0, PACK), :] = ((l1 * _sigmoid(l1)) * (z * _sigmoid(z))).astype(BF16)
        _strips(tm, PACK,norm)

        abuf[pl.ds(0, HALO), :] = abuf[pl.ds(tm, HALO), :]

    return pl.pallas_call(
        body, name="branch_a_fwd", grid=(n_i,),
        in_specs=[pl.BlockSpec((tm, 3 * D), lambda i: (i, 0)),
                  pl.BlockSpec((CONV_K * SUB, D), lambda i: (0, 0)), _vec_spec(1), _vec_spec(1), _vec_spec(1)],
        out_specs=[pl.BlockSpec((tm, D), lambda i: (i, 0)), pl.BlockSpec((tm, D), lambda i: (i, 0))],
        out_shape=[_sds((s_len, D), BF16), _sds((s_len, D), F32)],
        scratch_shapes=[pltpu.VMEM((tm + HALO, D), F32)],
        compiler_params=_params(("arbitrary",)),
    )(p, conv_wb, conv_b, ln_g, ln_b)


def _branch_b_fwd(p, wt, bias_full, ln_g, ln_b):
    s_len = p.shape[0]
    tm = min(256, s_len)
    n_i = s_len // tm

    def body(p_ref, wt_ref, bias_ref, g_ref, b_ref, yb_ref, vb, sbuf):
        def norm(r0):
            gv = _gelu(p_ref[pl.ds(r0, PACK), pl.ds(D, D)].astype(F32))
            mu = _rowmean(gv)
            vc = gv - mu
            rstd = lax.rsqrt(_rowmean(vc * vc) + EPS)
            vb[pl.ds(r0, PACK), :] = ((vc * rstd) * g_ref[...] + b_ref[...]).astype(BF16)
        _strips(tm, PACK,norm)

        for ck in range(tm // CHUNK):
            for h in range(HEADS):
                blk = (pl.ds(ck * CHUNK, CHUNK), pl.ds(h * LANE, LANE))
                sbuf[blk] = jnp.dot(wt_ref[h], vb[blk], preferred_element_type=F32) + bias_ref[:, pl.ds(h * LANE, LANE)]

        def gate(r0):
            u = _gelu(p_ref[pl.ds(r0, PACK), pl.ds(0, D)].astype(F32))
            z = p_ref[pl.ds(r0, PACK), pl.ds(2 * D, D)].astype(F32)
            yb_ref[pl.ds(r0, PACK), :] = (u * sbuf[pl.ds(r0, PACK), :] * (z * _sigmoid(z))).astype(BF16)
        _strips(tm, PACK,gate)

    return pl.pallas_call(
        body, name="branch_b_fwd", grid=(n_i,),
        in_specs=[pl.BlockSpec((tm, 3 * D), lambda i: (i, 1)),
                  pl.BlockSpec((HEADS, CHUNK, CHUNK), lambda i: (0, 0, 0)),
                  pl.BlockSpec((CHUNK, D), lambda i: (0, 0)), _vec_spec(1), _vec_spec(1)],
        out_specs=pl.BlockSpec((tm, D), lambda i: (i, 0)),
        out_shape=_sds((s_len, D), BF16),
        scratch_shapes=[pltpu.VMEM((tm, D), BF16), pltpu.VMEM((tm, D), F32)],
        compiler_params=_params(("arbitrary",)),
    )(p, wt, bias_full, ln_g, ln_b)


def _dot_t(a, b):
    return lax.dot_general(a, b, (((1,), (1,)), ((), ())), preferred_element_type=F32)


def _out_proj(p, ya_in, yb_in, x, target, gate, g_final, w_co, w_so, w_o):
    s_len = x.shape[0]
    tm = min(256, s_len)
    n_i = s_len // tm

    def body(pg_ref, ya_ref, yb_ref, x_ref, t_ref, gate_ref, gf_ref, wco_ref, wso_ref, wo_ref,
             dx2_ref, dya_ref, dyb_ref, dp_ref, mb_ref, dob_ref, dyab_ref, dybb_ref, sums_ref):
        @pl.when(pl.program_id(0) == 0)
        def _():
            sums_ref[...] = jnp.zeros((SUB, D), F32)

        y_a = jnp.dot(ya_ref[...], wco_ref[...], preferred_element_type=F32)
        y_b = jnp.dot(yb_ref[...], wso_ref[...], preferred_element_type=F32)
        ga = _sigmoid(pg_ref[:, pl.ds(0, D)].astype(F32))
        gb = _sigmoid(pg_ref[:, pl.ds(D, D)].astype(F32))
        mb = (ga * y_a + gb * y_b).astype(BF16)
        mb_ref[...] = mb
        o = jnp.dot(mb, wo_ref[...], preferred_element_type=F32)
        x2 = x_ref[...] + gate_ref[...] * o
        r2 = lax.rsqrt(_rowmean(x2 * x2) + EPS)
        xh = x2 * r2
        e = xh * gf_ref[...] - t_ref[...]
        dy = e * (1.0 / D)
        dxh = dy * gf_ref[...]
        dx2 = r2 * (dxh - xh * _rowmean(dxh * xh))
        dx2_ref[...] = dx2
        sums_ref[pl.ds(0, 1), :] += jnp.sum(dy * xh, axis=0, keepdims=True)
        sums_ref[pl.ds(1, 1), :] += jnp.sum(dx2 * o, axis=0, keepdims=True)
        sums_ref[pl.ds(2, 1), :] += jnp.sum(e * e, axis=0, keepdims=True) * (0.5 / D)
        dob = (gate_ref[...] * dx2).astype(BF16)
        dob_ref[...] = dob
        dm = _dot_t(dob, wo_ref[...])
        dy_a = (ga * dm).astype(BF16)
        dy_b = (gb * dm).astype(BF16)
        dyab_ref[...] = dy_a
        dybb_ref[...] = dy_b
        dp_ref[:, pl.ds(0, D)] = (dm * y_a * ga * (1.0 - ga)).astype(BF16)
        dp_ref[:, pl.ds(D, D)] = (dm * y_b * gb * (1.0 - gb)).astype(BF16)
        dya_ref[...] = _dot_t(dy_a, wco_ref[...])
        dyb_ref[...] = _dot_t(dy_b, wso_ref[...])

    tile = pl.BlockSpec((tm, D), lambda i: (i, 0))
    wspec = pl.BlockSpec((D, D), lambda i: (0, 0))
    return pl.pallas_call(
        body, name="out_proj", grid=(n_i,),
        in_specs=[pl.BlockSpec((tm, 2 * D), lambda i: (i, 3)), tile, tile, tile, tile, _vec_spec(1), _vec_spec(1),
                  wspec, wspec, wspec],
        out_specs=[tile, tile, tile, pl.BlockSpec((tm, 2 * D), lambda i: (i, 3)), tile, tile, tile, tile,
                   pl.BlockSpec((SUB, D), lambda i: (0, 0))],
        out_shape=[_sds((s_len, D), F32), _sds((s_len, D), F32), _sds((s_len, D), F32), _sds((s_len, N_SEC * D), BF16),
                   _sds((s_len, D), BF16), _sds((s_len, D), BF16), _sds((s_len, D), BF16), _sds((s_len, D), BF16),
                   _sds((SUB, D), F32)],
        compiler_params=_params(("arbitrary",)),
    )(p, ya_in, yb_in, x, target, gate, g_final, w_co, w_so, w_o)


A_STATS_ROWS = 8 + HALO


def _branch_a_bwd(p, y1, dya_in, dp, conv_wb, ln_g, ln_b):
    s_len = p.shape[0]
    tm = min(256, s_len)
    n_i = s_len // tm
    rows = 32
    n_out = rows // SUB

    def tile_of(i):
        return n_i - 1 - i

    def body(p_ref, y1_ref, dya_ref, dp_in, wb_ref, g_ref, b_ref, dp_ref, st_ref, dybuf, acc8, tapacc):
        del dp_in
        i = pl.program_id(0)

        @pl.when(i == 0)
        def _():
            dybuf[pl.ds(tm, HALO), :] = jnp.zeros((HALO, D), F32)
            st_ref[...] = jnp.zeros((A_STATS_ROWS, D), F32)
            acc8[...] = jnp.zeros((3 * PACK, D), F32)
            tapacc[...] = jnp.zeros((CONV_K * SUB, D), F32)

        def norm_bwd(r0):
            y1 = y1_ref[pl.ds(r0, PACK), :]
            mu = _rowmean(y1)
            yc = y1 - mu
            rstd = lax.rsqrt(_rowmean(yc * yc) + EPS)
            n1 = yc * rstd
            l1 = n1 * g_ref[...] + b_ref[...]
            sg = _sigmoid(l1)
            z = p_ref[pl.ds(r0, PACK), pl.ds(2 * D, D)].astype(F32)
            sz = _sigmoid(z)
            dya = dya_ref[pl.ds(r0, PACK), :]
            dp_ref[pl.ds(r0, PACK), pl.ds(2 * D, D)] = (dya * (l1 * sg) * _dsilu(z, sz)).astype(BF16)
            dl1 = dya * (z * sz) * _dsilu(l1, sg)
            acc8[pl.ds(0, PACK), :] += dl1 * n1
            acc8[pl.ds(PACK, PACK), :] += dl1
            dn1 = dl1 * g_ref[...]
            dy1 = rstd * (dn1 - _rowmean(dn1) - n1 * _rowmean(dn1 * n1))
            acc8[pl.ds(2 * PACK, PACK), :] += dy1
            dybuf[pl.ds(r0, PACK), :] = dy1
        _strips(tm, PACK,norm_bwd)

        def conv_bwd(r0):
            for lt in range(D // LANE):
                lanes = pl.ds(lt * LANE, LANE)
                glanes = pl.ds(D + lt * LANE, LANE)
                win = dybuf[pl.ds(r0, rows + HALO), lanes]
                sg16, a16 = [], []
                for h in range(rows // PACK):
                    rr = pl.ds(r0 + h * PACK, PACK)
                    s = _sigmoid(p_ref[rr, glanes].astype(F32))
                    sg16.append(s)
                    a16.append(p_ref[rr, lanes].astype(F32) * s)
                a = [a16[m // 2][(m % 2) * SUB:(m % 2 + 1) * SUB, :] for m in range(n_out)]
                da = [jnp.zeros((SUB, LANE), F32) for _ in range(n_out)]
                for phase in range(SUB):
                    offs = [o for o in range(CONV_K) if o % SUB == phase]
                    q_max = max(o // SUB for o in offs)
                    sh = win[phase:phase + (n_out + q_max) * SUB, :]
                    for o in offs:
                        k, q = CONV_K - 1 - o, o // SUB
                        w = wb_ref[pl.ds(k * SUB, SUB), lanes]
                        part = None
                        for m in range(n_out):
                            s = sh[(m + q) * SUB:(m + q + 1) * SUB, :]
                            da[m] = da[m] + w * s
                            part = a[m] * s if part is None else part + a[m] * s
                        tapacc[pl.ds(k * SUB, SUB), lanes] += part
                for h in range(rows // PACK):
                    rr = pl.ds(r0 + h * PACK, PACK)
                    da16 = jnp.concatenate(da[2 * h:2 * h + 2], axis=0)
                    dp_ref[rr, lanes] = (da16 * sg16[h]).astype(BF16)
                    dp_ref[rr, glanes] = (da16 * a16[h] * (1.0 - sg16[h])).astype(BF16)
        _strips(tm, rows, conv_bwd)

        dybuf[pl.ds(tm, HALO), :] = dybuf[pl.ds(0, HALO), :]

        @pl.when(i == n_i - 1)
        def _():
            for j in range(3):
                st_ref[pl.ds(j, 1), :] = jnp.sum(acc8[pl.ds(j * PACK, PACK), :], axis=0, keepdims=True)
            for k in range(CONV_K):
                st_ref[pl.ds(SUB + k, 1), :] = jnp.sum(tapacc[pl.ds(k * SUB, SUB), :], axis=0, keepdims=True)

    return pl.pallas_call(
        body, name="branch_a_bwd", grid=(n_i,),
        in_specs=[pl.BlockSpec((tm, 3 * D), lambda i: (tile_of(i), 0)),
                  pl.BlockSpec((tm, D), lambda i: (tile_of(i), 0)),
                  pl.BlockSpec((tm, D), lambda i: (tile_of(i), 0)),
                  pl.BlockSpec(memory_space=pl.ANY),
                  pl.BlockSpec((CONV_K * SUB, D), lambda i: (0, 0)), _vec_spec(1), _vec_spec(1)],
        out_specs=[pl.BlockSpec((tm, 3 * D), lambda i: (tile_of(i), 0)),
                   pl.BlockSpec((A_STATS_ROWS, D), lambda i: (0, 0))],
        out_shape=[_sds(dp.shape, BF16), _sds((A_STATS_ROWS, D), F32)],
        scratch_shapes=[pltpu.VMEM((tm + HALO, D), F32), pltpu.VMEM((3 * PACK, D), F32), pltpu.VMEM((CONV_K * SUB, D), F32)],
        input_output_aliases={3: 0},
        compiler_params=_params(("arbitrary",)),
    )(p, y1, dya_in, dp, conv_wb, ln_g, ln_b)


def _branch_b_bwd(p, dyb_in, dp, wt, wtt, bias_full, ln_g, ln_b):
    s_len = p.shape[0]
    tm = min(256, s_len)
    n_i = s_len // tm

    def body(p_ref, dyb_ref, dp_in, wt_ref, wtt_ref, bias_ref, g_ref, b_ref, dp_ref, st_ref, gbt_ref, gw_ref,
             vb, n2buf, rstdbuf, sbuf, dsb, dvbuf, acc8, gb_ref, dgbuf):
        del dp_in
        i = pl.program_id(0)

        @pl.when(i == 0)
        def _():
            st_ref[...] = jnp.zeros((SUB, D), F32)
            gbt_ref[...] = jnp.zeros((CHUNK, LANE), F32)
            gb_ref[...] = jnp.zeros((CHUNK, D), F32)
            gw_ref[...] = jnp.zeros((HEADS, CHUNK, CHUNK), F32)
            acc8[...] = jnp.zeros((2 * PACK, D), F32)

        def norm(r0):
            gv, dgv = _gelu_and_grad(p_ref[pl.ds(r0, PACK), pl.ds(D, D)].astype(F32))
            dgbuf[pl.ds(r0, PACK), :] = dgv
            mu = _rowmean(gv)
            vc = gv - mu
            rstd = lax.rsqrt(_rowmean(vc * vc) + EPS)
            n2 = vc * rstd
            n2buf[pl.ds(r0, PACK), :] = n2
            rstdbuf[pl.ds(r0, PACK), :] = jnp.broadcast_to(rstd, (PACK, LANE))
            vb[pl.ds(r0, PACK), :] = (n2 * g_ref[...] + b_ref[...]).astype(BF16)
        _strips(tm, PACK,norm)

        for ck in range(tm // CHUNK):
            for h in range(HEADS):
                blk = (pl.ds(ck * CHUNK, CHUNK), pl.ds(h * LANE, LANE))
                sbuf[blk] = jnp.dot(wt_ref[h], vb[blk], preferred_element_type=F32) + bias_ref[:, pl.ds(h * LANE, LANE)]

        def gate_bwd(r0):
            pu = p_ref[pl.ds(r0, PACK), pl.ds(0, D)].astype(F32)
            u, du = _gelu_and_grad(pu)
            z = p_ref[pl.ds(r0, PACK), pl.ds(2 * D, D)].astype(F32)
            sg = _sigmoid(z)
            sz = z * sg
            s = sbuf[pl.ds(r0, PACK), :]
            dyb = dyb_ref[pl.ds(r0, PACK), :]
            ds = dyb * u * sz
            dsb[pl.ds(r0, PACK), :] = ds.astype(BF16)
            gb_ref[pl.ds(pl.multiple_of(r0 % CHUNK, PACK), PACK), :] += ds
            dp_ref[pl.ds(r0, PACK), pl.ds(0, D)] = (dyb * s * sz * du).astype(BF16)
            dp_ref[pl.ds(r0, PACK), pl.ds(2 * D, D)] = (dyb * u * s * _dsilu(z, sg)).astype(BF16)
        _strips(tm, PACK,gate_bwd)

        for ck in range(tm // CHUNK):
            for h in range(HEADS):
                blk = (pl.ds(ck * CHUNK, CHUNK), pl.ds(h * LANE, LANE))
                d_s = dsb[blk]
                dvbuf[blk] = jnp.dot(wtt_ref[h], d_s, preferred_element_type=F32)
                gw_ref[h] += _dot_t(d_s, vb[blk])

        def norm_bwd(r0):
            dv = dvbuf[pl.ds(r0, PACK), :]
            n2 = n2buf[pl.ds(r0, PACK), :]
            rstd = rstdbuf[pl.ds(r0, PACK), pl.ds(0, 1)]
            acc8[pl.ds(0, PACK), :] += dv * n2
            acc8[pl.ds(PACK, PACK), :] += dv
            dn2 = dv * g_ref[...]
            dgv = rstd * (dn2 - _rowmean(dn2) - n2 * _rowmean(dn2 * n2))
            dp_ref[pl.ds(r0, PACK), pl.ds(D, D)] = (dgv * dgbuf[pl.ds(r0, PACK), :]).astype(BF16)
        _strips(tm, PACK,norm_bwd)

        @pl.when(i == n_i - 1)
        def _():
            for j in range(2):
                st_ref[pl.ds(j, 1), :] = jnp.sum(acc8[pl.ds(j * PACK, PACK), :], axis=0, keepdims=True)
            row = lax.broadcasted_iota(jnp.int32, (CHUNK, CHUNK), 0)
            col = lax.broadcasted_iota(jnp.int32, (CHUNK, CHUNK), 1)
            for h in range(HEADS):
                gw_ref[h] = jnp.where(row >= col, gw_ref[h], 0.0)
            lane = lax.broadcasted_iota(jnp.int32, (CHUNK, LANE), 1)
            gbt = jnp.zeros((CHUNK, LANE), F32)
            for h in range(HEADS):
                gbt = jnp.where(lane == h, jnp.sum(gb_ref[:, pl.ds(h * LANE, LANE)], axis=1, keepdims=True), gbt)
            gbt_ref[...] = gbt

    wspec = pl.BlockSpec((HEADS, CHUNK, CHUNK), lambda i: (0, 0, 0))
    return pl.pallas_call(
        body, name="branch_b_bwd", grid=(n_i,),
        in_specs=[pl.BlockSpec((tm, 3 * D), lambda i: (i, 1)), pl.BlockSpec((tm, D), lambda i: (i, 0)),
                  pl.BlockSpec(memory_space=pl.ANY), wspec, wspec,
                  pl.BlockSpec((CHUNK, D), lambda i: (0, 0)), _vec_spec(1), _vec_spec(1)],
        out_specs=[pl.BlockSpec((tm, 3 * D), lambda i: (i, 1)), pl.BlockSpec((SUB, D), lambda i: (0, 0)),
                   pl.BlockSpec((CHUNK, LANE), lambda i: (0, 0)), wspec],
        out_shape=[_sds(dp.shape, BF16), _sds((SUB, D), F32), _sds((CHUNK, LANE), F32), _sds((HEADS, CHUNK, CHUNK), F32)],
        scratch_shapes=[pltpu.VMEM((tm, D), BF16), pltpu.VMEM((tm, D), F32), pltpu.VMEM((tm, LANE), F32),
                        pltpu.VMEM((tm, D), F32), pltpu.VMEM((tm, D), BF16), pltpu.VMEM((tm, D), F32),
                        pltpu.VMEM((2 * PACK, D), F32), pltpu.VMEM((CHUNK, D), F32), pltpu.VMEM((tm, D), F32)],
        input_output_aliases={2: 0},
        compiler_params=_params(("arbitrary",)),
    )(p, dyb_in, dp, wt, wtt, bias_full, ln_g, ln_b)


def _in_proj_bwd(dp, wg_in, x, dx2, shift, scale, g_pre):
    del shift
    s_len = x.shape[0]
    tm = min(512, s_len)
    n_i = s_len // tm
    wn = wg_in.shape[2]

    def body(dp0, dp1, dp2, dp3, w_ref, x_ref, dx2_ref, sc_ref, g_ref, gx_ref, st_ref, acc, acc8):
        i = pl.program_id(0)

        @pl.when(i == 0)
        def _():
            st_ref[...] = jnp.zeros((SUB, D), F32)
            acc8[...] = jnp.zeros((3 * PACK, D), F32)

        dh = _dot_t(dp0[...], w_ref[0])
        for j, dp_ref in enumerate((dp1, dp2, dp3), start=1):
            dh = dh + _dot_t(dp_ref[...], w_ref[j])
        acc[...] = dh

        def strip(r0):
            xs = x_ref[pl.ds(r0, PACK), :]
            r = lax.rsqrt(_rowmean(xs * xs) + EPS)
            xn = xs * r
            dhs = acc[pl.ds(r0, PACK), :]
            acc8[pl.ds(0, PACK), :] += dhs
            acc8[pl.ds(PACK, PACK), :] += dhs * (xn * g_ref[...])
            dhp = dhs * (1.0 + sc_ref[...])
            acc8[pl.ds(2 * PACK, PACK), :] += dhp * xn
            dxn = dhp * g_ref[...]
            gx_ref[pl.ds(r0, PACK), :] = dx2_ref[pl.ds(r0, PACK), :] + r * (dxn - xn * _rowmean(dxn * xn))
        _strips(tm, PACK, strip)

        @pl.when(i == n_i - 1)
        def _():
            for k in range(3):
                st_ref[pl.ds(k, 1), :] = jnp.sum(acc8[pl.ds(k * PACK, PACK), :], axis=0, keepdims=True)

    tile = pl.BlockSpec((tm, D), lambda i: (i, 0))
    return pl.pallas_call(
        body, name="in_proj_bwd", grid=(n_i,),
        in_specs=[pl.BlockSpec((tm, wn), functools.partial(lambda j, i: (i, j), j)) for j in range(N_CHIP)] + [
                  pl.BlockSpec((N_CHIP, D, wn), lambda i: (0, 0, 0), pipeline_mode=pl.Buffered(1)),
                  tile, tile, _vec_spec(1), _vec_spec(1)],
        out_specs=[tile, pl.BlockSpec((SUB, D), lambda i: (0, 0))],
        out_shape=[_sds((s_len, D), F32), _sds((SUB, D), F32)],
        scratch_shapes=[pltpu.VMEM((tm, D), F32), pltpu.VMEM((3 * PACK, D), F32)],
        compiler_params=_params(("arbitrary",)),
    )(dp, dp, dp, dp, wg_in, x, dx2, scale, g_pre)


def _grad_matmul(a, b, name):
    s_len, n = b.shape
    cb = min(2 * D, n)
    tn = 512
    per = cb // tn

    def body(a_ref, b_ref, ob_ref):
        ob_ref[0] = lax.dot_general(a_ref[...], b_ref[...], (((0,), (0,)), ((), ())),
                                    preferred_element_type=F32).astype(BF16)

    return pl.pallas_call(
        body, name=name, grid=(n // tn,),
        in_specs=[pl.BlockSpec((s_len, D), lambda j: (0, 0), pipeline_mode=pl.Buffered(1)),
                  pl.BlockSpec((s_len, tn), lambda j: (0, j))],
        out_specs=pl.BlockSpec((1, D, tn), lambda j: (j // per, 0, j % per)),
        out_shape=_sds((n // cb, D, cb), BF16),
        compiler_params=_params(("arbitrary",)),
    )(a, b)


def _local_step(x, target, shift, scale, gate, g_pre, conv_w_full, conv_b, conv_ln_g, conv_ln_b,
                sgu_ln_g, sgu_ln_b, w_sgu, b_sgu, g_final, wg_in, out_shards):
    conv_wb = jnp.repeat(conv_w_full, SUB, axis=0)
    causal = jnp.tril(jnp.ones((CHUNK, CHUNK), dtype=bool))
    wt = jnp.where(causal[None], w_sgu, 0.0).astype(BF16)
    wtt = jnp.swapaxes(wt, 1, 2)
    bias_full = jnp.repeat(b_sgu.T, LANE, axis=1)

    p, hb, gathered = _in_proj_gather(x, shift, scale, g_pre, wg_in, out_shards)
    w_co, w_so, w_o = (g.reshape(D, D) for g in gathered)
    ya_in, y1 = _branch_a_fwd(p, conv_wb, conv_b, conv_ln_g, conv_ln_b)
    yb_in = _branch_b_fwd(p, wt, bias_full, sgu_ln_g, sgu_ln_b)
    dx2, dya_in, dyb_in, dp, mb, dob, dyab, dybb, sums_o = _out_proj(
        p, ya_in, yb_in, x, target, gate, g_final, w_co, w_so, w_o)
    dp, st_a = _branch_a_bwd(p, y1, dya_in, dp, conv_wb, conv_ln_g, conv_ln_b)
    dp, st_b, gbt, gws = _branch_b_bwd(p, dyb_in, dp, wt, wtt, bias_full, sgu_ln_g, sgu_ln_b)
    grad_x, st_i = _in_proj_bwd(dp, wg_in, x, dx2, shift, scale, g_pre)
    gw_o = _grad_matmul(mb, dob, "grad_w_o")
    gw_co = _grad_matmul(ya_in, dyab, "grad_w_conv_out")
    gw_so = _grad_matmul(yb_in, dybb, "grad_w_sgu_out")
    return dict(
        grad_x=grad_x, loss_cols=sums_o[2:3], g_final=sums_o[0:1], d_gate=sums_o[1:2],
        d_shift=st_i[0:1], d_scale=st_i[1:2], g_pre=st_i[2:3],
        conv_ln_g=st_a[0:1], conv_ln_b=st_a[1:2], conv_b=st_a[2:3], conv_w=st_a[SUB:SUB + CONV_K],
        sgu_ln_g=st_b[0:1], sgu_ln_b=st_b[1:2], b_sgu=gbt[:, :HEADS].T, w_sgu=gws,
        hb=hb, dp=dp, w_o=gw_o, w_conv_out=gw_co, w_sgu_out=gw_so)


ANY_SPEC = pl.BlockSpec(memory_space=pl.ANY)
VMEM_SPEC = pl.BlockSpec(memory_space=pltpu.VMEM)


def _place():
    return lax.axis_index("x"), lax.axis_index("y"), lax.axis_index("c")


def _peer(k):
    x, y, c = _place()
    return (1 - x if k & 4 else x, 1 - y if k & 2 else y, 1 - c if k & 1 else c)


def _dev_of(p):
    return 4 * p[0] + 2 * p[1] + p[2]


def _chip_of(p):
    return 2 * p[0] + p[1]


def _rdma(src, dst, send_sem, recv_sem, to):
    return pltpu.make_async_remote_copy(src_ref=src, dst_ref=dst, send_sem=send_sem, recv_sem=recv_sem,
                                        device_id=to, device_id_type=MESH)


CHIP_PEERS = (2, 4, 6)
ALL_PEERS = tuple(range(1, N_DEV))
SIBLING = 1


def _setup_comm(c8, w_ada_s, b_ada_s, convw_s, shards):
    n_mod = w_ada_s.shape[1]
    rows = SUB * N_DEV
    n = len(shards)

    def body(c8_ref, wada_ref, bada_ref, cw_ref, *refs):
        ins, (call_ref, mod_ref, cwall_ref), outs = refs[:n], refs[n:n + 3], refs[n + 3:2 * n + 3]
        csend, crecv, wsend, wrecv, msend, mrecv = refs[2 * n + 3:2 * n + 9]
        gather_a, gather_b, gather_c = _gather_phases([s.shape[0] for s in shards], ins, outs, refs[2 * n + 9:])
        gather_a()
        me = _place()
        dev, chip = _dev_of(me), _chip_of(me)

        def c_rows(d):
            return call_ref.at[pl.ds(pl.multiple_of(d * SUB, SUB), SUB), :]

        call_ref[pl.ds(pl.multiple_of(dev * SUB, SUB), SUB), :] = c8_ref[...]
        cwall_ref[chip] = cw_ref[...]
        c_out = [_rdma(c8_ref, c_rows(dev), csend.at[k], crecv.at[k], _peer(k)) for k in ALL_PEERS]
        w_out = [_rdma(cw_ref, cwall_ref.at[chip], wsend.at[k], wrecv.at[k], _peer(k)) for k in CHIP_PEERS]
        for cp in c_out + w_out:
            cp.start()
        for k in ALL_PEERS:
            _rdma(c8_ref, c_rows(_dev_of(_peer(k))), csend.at[k], crecv.at[k], _peer(k)).wait_recv()
        part = jnp.dot(call_ref[...].astype(BF16), wada_ref[...].astype(BF16), preferred_element_type=F32) + bada_ref[...]
        mod_ref[chip] = part
        m_out = [_rdma(mod_ref.at[chip], mod_ref.at[chip], msend.at[k], mrecv.at[k], _peer(k)) for k in CHIP_PEERS]
        for cp in m_out:
            cp.start()
        for k in CHIP_PEERS:
            pc = _chip_of(_peer(k))
            _rdma(cw_ref, cwall_ref.at[pc], wsend.at[k], wrecv.at[k], _peer(k)).wait_recv()
            _rdma(mod_ref.at[pc], mod_ref.at[pc], msend.at[k], mrecv.at[k], _peer(k)).wait_recv()
        for cp in c_out + w_out + m_out:
            cp.wait_send()
        gather_b()
        gather_c()

    res = pl.pallas_call(
        body, name="setup_comm",
        in_specs=[VMEM_SPEC] * (4 + n), out_specs=[VMEM_SPEC] * (3 + n),
        out_shape=([_sds((rows, D), F32), _sds((N_CHIP, rows, n_mod), F32), _sds((N_CHIP,) + convw_s.shape, F32)]
                   + [_sds((N_CHIP,) + s.shape, s.dtype) for s in shards]),
        scratch_shapes=([pltpu.SemaphoreType.DMA((N_DEV,))] * 6
                        + [pltpu.SemaphoreType.DMA((n,))] + [pltpu.SemaphoreType.DMA((n, len(CHIP_PEERS)))] * 4),
        compiler_params=_params(),
    )(c8, w_ada_s, b_ada_s, convw_s, *shards)
    return res[0], res[1], res[2], res[3:]


def _gather_phases(row_counts, ins, dsts, sems):
    n = len(row_counts)
    lsem, isend, irecv, dsend, drecv = sems
    me = _place()
    chip, c = _chip_of(me), me[2]

    def half(t, which):
        hr = row_counts[t] // 2
        return pl.ds(pl.multiple_of(which * hr, hr), hr)

    def local(t):
        return pltpu.make_async_copy(ins[t], dsts[t].at[chip], lsem.at[t])

    def to_chip(t, j):
        return _rdma(ins[t].at[half(t, c)], dsts[t].at[chip, half(t, c)], isend.at[t, j], irecv.at[t, j], _peer(CHIP_PEERS[j]))

    def landed(t, j, which):
        return dsts[t].at[_chip_of(_peer(CHIP_PEERS[j])), half(t, which)]

    def to_sibling(t, j):
        return _rdma(landed(t, j, c), landed(t, j, c), dsend.at[t, j], drecv.at[t, j], _peer(SIBLING))

    pairs = [(t, j) for t in range(n) for j in range(len(CHIP_PEERS))]

    def phase_a():
        for t in range(n):
            local(t).start()
        for t, j in pairs:
            to_chip(t, j).start()

    def phase_b():
        for t, j in pairs:
            _rdma(landed(t, j, c), landed(t, j, c), isend.at[t, j], irecv.at[t, j], _peer(CHIP_PEERS[j])).wait_recv()
            to_sibling(t, j).start()

    def phase_c():
        for t, j in pairs:
            _rdma(landed(t, j, 1 - c), landed(t, j, 1 - c), dsend.at[t, j], drecv.at[t, j], _peer(SIBLING)).wait_recv()
        for t, j in pairs:
            to_chip(t, j).wait_send()
            to_sibling(t, j).wait_send()
        for t in range(n):
            local(t).wait()

    return phase_a, phase_b, phase_c


def _in_proj_gather(x, shift, scale, g_pre, wg_in, shards):
    s_len = x.shape[0]
    tm = min(256, s_len)
    n_i = s_len // tm
    wn = wg_in.shape[2]
    n = len(shards)

    def body(x_ref, sh_ref, sc_ref, g_ref, w_ref, *refs):
        ins, p_ref, hb_ref, outs = refs[:n], refs[n], refs[n + 1], refs[n + 2:2 * n + 2]
        gath, sems = refs[2 * n + 2:3 * n + 2], refs[3 * n + 2:]
        phases = _gather_phases([s.shape[0] for s in shards], ins, gath, sems)
        i = pl.program_id(0)
        for step, phase in zip((0, n_i // 2, n_i - 1), phases):
            pl.when(i == step)(phase)

        @pl.when(i == n_i - 1)
        def _():
            for t in range(n):
                outs[t][...] = gath[t][...]

        def strip(r0):
            xs = x_ref[pl.ds(r0, PACK), :]
            r = lax.rsqrt(_rowmean(xs * xs) + EPS)
            h = (xs * r) * g_ref[...] * (1.0 + sc_ref[...]) + sh_ref[...]
            hb_ref[pl.ds(r0, PACK), :] = h.astype(BF16)
        _strips(tm, PACK, strip)
        hb = hb_ref[...]
        for j in range(N_CHIP):
            p_ref[:, pl.ds(j * wn, wn)] = jnp.dot(hb, w_ref[j], preferred_element_type=F32).astype(BF16)

    res = pl.pallas_call(
        body, name="in_proj", grid=(n_i,),
        in_specs=[pl.BlockSpec((tm, D), lambda i: (i, 0)), _vec_spec(1), _vec_spec(1), _vec_spec(1),
                  pl.BlockSpec((N_CHIP, D, wn), lambda i: (0, 0, 0), pipeline_mode=pl.Buffered(1))] + [VMEM_SPEC] * n,
        out_specs=[pl.BlockSpec((tm, N_CHIP * wn), lambda i: (i, 0)), pl.BlockSpec((tm, D), lambda i: (i, 0))] + [VMEM_SPEC] * n,
        out_shape=([_sds((s_len, N_SEC * D), BF16), _sds((s_len, D), BF16)]
                   + [_sds((N_CHIP,) + s.shape, s.dtype) for s in shards]),
        scratch_shapes=([pltpu.VMEM((N_CHIP,) + s.shape, s.dtype) for s in shards]
                        + [pltpu.SemaphoreType.DMA((n,))] + [pltpu.SemaphoreType.DMA((n, len(CHIP_PEERS)))] * 4),
        compiler_params=_params(("arbitrary",)),
    )(x, shift, scale, g_pre, wg_in, *shards)
    return res[0], res[1], res[2:]


def _reduce_scatter(grads, name):
    n = len(grads)
    shapes = [g.shape[2:] for g in grads]
    parts = 2
    part_shapes = [(r, cols // parts) for r, cols in shapes for _ in range(parts)]

    def body(*refs):
        def halves(group, lead):
            return [ref.at[(slice(None),) * lead + (pl.ds(k * (s[1] // parts), s[1] // parts),)]
                    for ref, s in zip(group, shapes) for k in range(parts)]
        ins, outs = halves(refs[:n], 3), halves(refs[n:2 * n], 2)
        pbufs, rbufs, accs = halves(refs[2 * n:3 * n], 2), halves(refs[3 * n:4 * n], 2), halves(refs[4 * n:5 * n], 1)
        for phase in _reduce_phases(part_shapes, ins, outs, pbufs, rbufs, accs, refs[5 * n:]):
            phase()

    return pl.pallas_call(
        body, name=name,
        in_specs=[VMEM_SPEC] * n, out_specs=[VMEM_SPEC] * n,
        out_shape=[_sds((2,) + s, F32) for s in shapes],
        scratch_shapes=_reduce_scratch(shapes)[:3 * n] + _reduce_scratch(part_shapes)[3 * parts * n:],
        compiler_params=_params(),
    )(*grads)


def _reduce_phases(shapes, ins, outs, pbufs, rbufs, accs, sems):
    n = len(shapes)
    psend, precv, csend, crecv, fsend, frecv = sems
    me = _place()
    chip, c = _chip_of(me), me[2]
    sib = _peer(SIBLING)

    def to_sibling(t, d):
        return _rdma(ins[t].at[d, 1 - c], pbufs[t].at[d], psend.at[t, d], precv.at[t, d], sib)

    def to_chip(t, j):
        return _rdma(pbufs[t].at[jnp.bitwise_xor(chip, j)], rbufs[t].at[j - 1], csend.at[t, j], crecv.at[t, j], _peer(2 * j))

    def finished(t):
        return _rdma(outs[t].at[c], outs[t].at[c], fsend.at[t], frecv.at[t], sib)

    def phase_a():
        for t in range(n):
            for d in range(N_CHIP):
                to_sibling(t, d).start()

    def phase_b():
        for j in (1, 2, 3, 0):
            d = jnp.bitwise_xor(chip, j)
            for t in range(n):
                to_sibling(t, d).wait_recv()

                def pair_sum(r0, t=t, d=d, j=j):
                    rows = pl.ds(r0, PACK)
                    s = ins[t][d, c, rows, :].astype(F32) + pbufs[t][d, rows, :].astype(F32)
                    if j == 0:
                        accs[t][rows, :] = s
                    else:
                        pbufs[t][d, rows, :] = s.astype(BF16)
                _strips(shapes[t][0], PACK, pair_sum)
                if j:
                    to_chip(t, j).start()

    def phase_c():
        for t in range(n):
            for j in (1, 2, 3):
                blk = rbufs[t].at[j - 1]
                _rdma(blk, blk, csend.at[t, j], crecv.at[t, j], _peer(2 * j)).wait_recv()

            def total(r0, t=t):
                rows = pl.ds(r0, PACK)
                s = accs[t][rows, :] + rbufs[t][0, rows, :].astype(F32)
                s = s + rbufs[t][1, rows, :].astype(F32)
                outs[t][c, rows, :] = s + rbufs[t][2, rows, :].astype(F32)
            _strips(shapes[t][0], PACK, total)
            finished(t).start()

    def phase_d():
        for t in range(n):
            blk = outs[t].at[1 - c]
            _rdma(blk, blk, fsend.at[t], frecv.at[t], sib).wait_recv()
        for t in range(n):
            for d in range(N_CHIP):
                to_sibling(t, d).wait_send()
            for j in (1, 2, 3):
                to_chip(t, j).wait_send()
            finished(t).wait_send()

    return phase_a, phase_b, phase_c, phase_d


def _sum_small_phases(ins, outs, pbufs, buf4s, sems):
    n = len(ins)
    psend, precv, send, recv = sems
    chip = _chip_of(_place())

    def swap(t):
        return _rdma(ins[t], pbufs[t], psend.at[t], precv.at[t], _peer(SIBLING))

    def to_chip(t, k):
        return _rdma(buf4s[t].at[chip], buf4s[t].at[chip], send.at[t, k], recv.at[t, k], _peer(k))

    def phase_a():
        for t in range(n):
            swap(t).start()

    def phase_b():
        for t in range(n):
            swap(t).wait()
            buf4s[t][chip] = ins[t][...] + pbufs[t][...]
            for k in CHIP_PEERS:
                to_chip(t, k).start()

    def phase_c():
        for t in range(n):
            for k in CHIP_PEERS:
                blk = buf4s[t].at[_chip_of(_peer(k))]
                _rdma(blk, blk, send.at[t, k], recv.at[t, k], _peer(k)).wait_recv()
            outs[t][...] = (buf4s[t][0] + buf4s[t][1]) + (buf4s[t][2] + buf4s[t][3])

    def phase_d():
        for t in range(n):
            for k in CHIP_PEERS:
                to_chip(t, k).wait_send()

    return phase_a, phase_b, phase_c, phase_d


def _sum_small_scratch(blobs):
    n = len(blobs)
    return ([pltpu.VMEM(b.shape, F32) for b in blobs] + [pltpu.VMEM((N_CHIP,) + b.shape, F32) for b in blobs]
            + [pltpu.SemaphoreType.DMA((n,))] * 2 + [pltpu.SemaphoreType.DMA((n, N_DEV))] * 2)


def _reduce_scratch(shapes):
    n = len(shapes)
    return ([pltpu.VMEM((N_CHIP,) + s, BF16) for s in shapes] + [pltpu.VMEM((N_CHIP - 1,) + s, BF16) for s in shapes]
            + [pltpu.VMEM(s, F32) for s in shapes]
            + [pltpu.SemaphoreType.DMA((n, N_CHIP))] * 4 + [pltpu.SemaphoreType.DMA((n,))] * 2)


def _grad_matmul_reduce(a, b, name, grads, blobs):
    s_len, n_cols = b.shape
    cb = min(2 * D, n_cols)
    tn = 512
    per = cb // tn
    steps = n_cols // tn
    n, nb = len(grads), len(blobs)
    shapes = [g.shape[2:] for g in grads]
    n_red = len(_reduce_scratch(shapes))

    def body(a_ref, b_ref, *refs):
        ins, bins = refs[:n], refs[n:n + nb]
        ob_ref, outs, bouts = refs[n + nb], refs[n + nb + 1:2 * n + nb + 1], refs[2 * n + nb + 1:2 * (n + nb) + 1]
        scratch = refs[2 * (n + nb) + 1:]
        fulls, red, small = scratch[:n], scratch[n:n + n_red], scratch[n + n_red:]
        phases = _reduce_phases(shapes, ins, fulls, red[:n], red[n:2 * n], red[2 * n:3 * n], red[3 * n:])
        small_phases = _sum_small_phases(bins, bouts, small[:nb], small[nb:2 * nb], small[2 * nb:])
        j = pl.program_id(0)
        for step, phase in zip((0, 2, steps - 2, steps - 1), phases):
            pl.when(j == step)(phase)
        for step, phase in zip((1, 3, steps - 2, steps - 1), small_phases):
            pl.when(j == step)(phase)

        @pl.when(j == steps - 1)
        def _():
            for t in range(n):
                outs[t][...] = fulls[t][...]
        ob_ref[0] = lax.dot_general(a_ref[...], b_ref[...], (((0,), (0,)), ((), ())),
                                    preferred_element_type=F32).astype(BF16)

    res = pl.pallas_call(
        body, name=name, grid=(steps,),
        in_specs=[pl.BlockSpec((s_len, D), lambda j: (0, 0), pipeline_mode=pl.Buffered(1)),
                  pl.BlockSpec((s_len, tn), lambda j: (0, j))] + [VMEM_SPEC] * (n + nb),
        out_specs=[pl.BlockSpec((1, D, tn), lambda j: (j // per, 0, j % per))] + [VMEM_SPEC] * (n + nb),
        out_shape=([_sds((n_cols // cb, D, cb), BF16)] + [_sds((2,) + s, F32) for s in shapes]
                   + [_sds(bl.shape, F32) for bl in blobs]),
        scratch_shapes=[pltpu.VMEM((2,) + s, F32) for s in shapes] + _reduce_scratch(shapes) + _sum_small_scratch(blobs),
        compiler_params=_params(("arbitrary",)),
    )(a, b, *grads, *blobs)
    return res[0], res[1:1 + n], res[1 + n:]


def _adamw_math(w, g, m, v):
    m = ADAM_B1 * m + (1.0 - ADAM_B1) * g
    v = ADAM_B2 * v + (1.0 - ADAM_B2) * (g * g)
    m_hat = m / (1.0 - ADAM_B1 ** ADAM_STEP)
    v_hat = v / (1.0 - ADAM_B2 ** ADAM_STEP)
    delta = -ADAM_LR * (m_hat / (jnp.sqrt(v_hat) + ADAM_EPS) + ADAM_WD * w)
    return delta, m, v


def _row_tile(r, cols):
    if r * cols * 4 <= 2 ** 20:
        return r
    return next(t for t in (512, 256, 128, 64, 32, 16, 8) if r % t == 0 and t * cols * 4 <= 2 ** 20)


def _adamw(w, g, m, v, name):
    r, cols = w.shape
    tr = _row_tile(r, cols)

    def body(w_ref, g_ref, m_ref, v_ref, go_ref, d_ref, nm_ref, nv_ref):
        g = g_ref[...]
        go_ref[...] = g
        d_ref[...], nm_ref[...], nv_ref[...] = _adamw_math(w_ref[...], g, m_ref[...], v_ref[...])

    spec = pl.BlockSpec((tr, cols), lambda i: (i, 0))
    return pl.pallas_call(
        body, name=name, grid=(r // tr,), in_specs=[spec] * 4, out_specs=[spec] * 4,
        out_shape=[_sds((r, cols), F32)] * 4, compiler_params=_params(("arbitrary",)),
    )(w, g, m, v)


def _adamw_ada(w, ct, dm, m, v):
    r, cols = w.shape
    tr = _row_tile(r, cols)

    def body(w_ref, ct_ref, dm_ref, m_ref, v_ref, g_ref, d_ref, nm_ref, nv_ref):
        g = jnp.dot(ct_ref[...], dm_ref[...], preferred_element_type=F32)
        g_ref[...] = g
        d_ref[...], nm_ref[...], nv_ref[...] = _adamw_math(w_ref[...], g, m_ref[...], v_ref[...])

    spec = pl.BlockSpec((tr, cols), lambda i: (i, 0))
    return pl.pallas_call(
        body, name="adamw_ada", grid=(r // tr,),
        in_specs=[spec, pl.BlockSpec((tr, LANE), lambda i: (i, 0)), pl.BlockSpec((LANE, cols), lambda i: (0, 0)), spec, spec],
        out_specs=[spec] * 4, out_shape=[_sds((r, cols), F32)] * 4, compiler_params=_params(("arbitrary",)),
    )(w, ct, dm, m, v)


BLOB_VEC, BLOB_BSGU, BLOB_CONV, BLOB_ADA, BLOB_DMOD, BLOB_LOSS, BLOB_ROWS = 0, 8, 16, 48, 56, 80, 88
N_VEC = 7


def _adamw_small(tot, g_w_sgu, g_conv, params):
    n = len(params)

    def body(*refs):
        tot_ref, gws_ref, gconv_ref = refs[:3]
        wmv = refs[3:3 + 3 * n]
        outs = refs[3 + 3 * n:]
        grads = [tot_ref[pl.ds(BLOB_VEC + i, 1), :] for i in range(N_VEC)]
        grads += [tot_ref[pl.ds(BLOB_BSGU, HEADS), pl.ds(0, CHUNK)], gconv_ref[...], gws_ref[...], tot_ref[pl.ds(BLOB_ADA, 3), :]]
        for i, g in enumerate(grads):
            w_ref, m_ref, v_ref = wmv[3 * i:3 * i + 3]
            d, nm, nv = _adamw_math(w_ref[...], g, m_ref[...], v_ref[...])
            outs[4 * i][...] = g
            outs[4 * i + 1][...] = d
            outs[4 * i + 2][...] = nm
            outs[4 * i + 3][...] = nv

    flat = [a for wmv in params for a in wmv]
    return pl.pallas_call(
        body, name="adamw_small",
        in_specs=[VMEM_SPEC] * (3 + len(flat)), out_specs=[VMEM_SPEC] * (4 * n),
        out_shape=[_sds(wmv[0].shape, F32) for wmv in params for _ in range(4)],
        compiler_params=_params(),
    )(tot, g_w_sgu, g_conv, *flat)


def _set_rows(buf, row, val):
    return lax.dynamic_update_slice(buf, val.astype(F32), (row, 0))


def kernel(x, c, w_ada, b_ada, g_pre, w_in, conv_w, conv_b, conv_ln_g, conv_ln_b, w_conv_out, sgu_ln_g, sgu_ln_b, w_sgu, b_sgu, w_sgu_out, w_o, g_final, loss_target, m_w_ada, m_b_ada, m_g_pre, m_w_in, m_conv_w, m_conv_b, m_conv_ln_g, m_conv_ln_b, m_w_conv_out, m_sgu_ln_g, m_sgu_ln_b, m_w_sgu, m_b_sgu, m_w_sgu_out, m_w_o, m_g_final, v_w_ada, v_b_ada, v_g_pre, v_w_in, v_conv_w, v_conv_b, v_conv_ln_g, v_conv_ln_b, v_w_conv_out, v_sgu_ln_g, v_sgu_ln_b, v_w_sgu, v_b_sgu, v_w_sgu_out, v_w_o, v_g_final):
    me = _place()
    dev, chip = _dev_of(me), _chip_of(me)
    n_ada = w_ada.shape[2]
    conv_cols = conv_w.shape[2]

    b_ada_s = lax.dynamic_slice(b_ada, (0, chip * n_ada), (1, n_ada))
    c_all, mod_all, cw_all, (wg_in,) = _setup_comm(
        jnp.broadcast_to(c, (SUB, D)), w_ada[0], b_ada_s, jnp.pad(conv_w[0], ((0, HALO - CONV_K), (0, 0))),
        [w_in[0].astype(BF16)])
    mod = lax.dynamic_slice(mod_all, (0, dev * SUB, 0), (N_CHIP, 1, n_ada)).reshape(1, 3 * D)
    shift, scale, gate = mod[:, :D], mod[:, D:2 * D], mod[:, 2 * D:]
    conv_w_full = jnp.swapaxes(cw_all, 0, 1).reshape(HALO, D)[:CONV_K]

    loc = _local_step(x[0], loss_target[0], shift, scale, gate, g_pre, conv_w_full, conv_b, conv_ln_g, conv_ln_b,
                      sgu_ln_g, sgu_ln_b, w_sgu[0], b_sgu[0], g_final.reshape(1, D), wg_in,
                      [w_conv_out[0].astype(BF16), w_sgu_out[0].astype(BF16), w_o[0].astype(BF16)])

    d_mod = jnp.concatenate([loc["d_shift"], loc["d_scale"], loc["d_gate"]], axis=0)
    blob = jnp.zeros((BLOB_ROWS, D), F32)
    for i, name in enumerate(["g_pre", "conv_b", "conv_ln_g", "conv_ln_b", "sgu_ln_g", "sgu_ln_b", "g_final"]):
        blob = _set_rows(blob, BLOB_VEC + i, loc[name])
    blob = _set_rows(blob, BLOB_BSGU, loc["b_sgu"])
    blob = _set_rows(blob, BLOB_CONV, loc["conv_w"])
    blob = _set_rows(blob, BLOB_ADA, d_mod)
    blob = lax.dynamic_update_slice(blob, d_mod, (BLOB_DMOD + 3 * dev, 0))
    blob = _set_rows(blob, BLOB_LOSS, loc["loss_cols"])

    big = ["w_in", "w_conv_out", "w_sgu_out", "w_o"]
    contrib_out = [loc[name].reshape(N_CHIP, 2, D // (2 * N_CHIP), D) for name in big[1:]]
    gw_in, full_out, (tot, g_w_sgu) = _grad_matmul_reduce(
        loc["hb"], loc["dp"], "grad_w_in", contrib_out, [blob, loc["w_sgu"].reshape(HEADS * CHUNK, CHUNK)])
    full_in = _reduce_scatter([gw_in.reshape(N_CHIP, 2, D // 2, gw_in.shape[2])], "reduce_w_in")
    g_big = {name: f.reshape(2 * f.shape[1], f.shape[2]) for name, f in zip(big, list(full_in) + list(full_out))}

    loss = jnp.sum(tot[BLOB_LOSS])
    g_conv_s = lax.dynamic_slice(tot, (BLOB_CONV, chip * conv_cols), (CONV_K, conv_cols))
    d_mod_all = tot[BLOB_DMOD:BLOB_DMOD + 3 * N_DEV].reshape(N_DEV, 3 * D)

    ct = jnp.pad(c_all[::SUB].T, ((0, 0), (0, LANE - N_DEV))).astype(BF16)
    dm = jnp.pad(lax.dynamic_slice(d_mod_all, (0, chip * n_ada), (N_DEV, n_ada)), ((0, LANE - N_DEV), (0, 0))).astype(BF16)
    g_ada, d_ada, nm_ada, nv_ada = _adamw_ada(w_ada[0], ct, dm, m_w_ada[0], v_w_ada[0])

    upd = {}
    for name, w, m, v in [("w_in", w_in, m_w_in, v_w_in), ("w_conv_out", w_conv_out, m_w_conv_out, v_w_conv_out),
                          ("w_sgu_out", w_sgu_out, m_w_sgu_out, v_w_sgu_out), ("w_o", w_o, m_w_o, v_w_o)]:
        upd[name] = _adamw(w[0], g_big[name], m[0], v[0], "adamw_" + name)

    def wmv(w, m, v, shape):
        return tuple(a.reshape(shape) for a in (w, m, v))

    small_params = [wmv(w, m, v, (1, D)) for w, m, v in [
        (g_pre, m_g_pre, v_g_pre), (conv_b, m_conv_b, v_conv_b), (conv_ln_g, m_conv_ln_g, v_conv_ln_g),
        (conv_ln_b, m_conv_ln_b, v_conv_ln_b), (sgu_ln_g, m_sgu_ln_g, v_sgu_ln_g), (sgu_ln_b, m_sgu_ln_b, v_sgu_ln_b),
        (g_final, m_g_final, v_g_final)]]
    small_params += [wmv(b_sgu, m_b_sgu, v_b_sgu, (HEADS, CHUNK)), wmv(conv_w, m_conv_w, v_conv_w, (CONV_K, conv_cols)),
                     wmv(w_sgu, m_w_sgu, v_w_sgu, (HEADS * CHUNK, CHUNK)), wmv(b_ada, m_b_ada, v_b_ada, (3, D))]
    small_out = _adamw_small(tot, g_w_sgu, g_conv_s, small_params)

    def leaves(kind):
        vecs = [small_out[4 * i + kind] for i in range(N_VEC)]
        o_b_sgu, o_conv, o_w_sgu, o_b_ada = (small_out[4 * (N_VEC + i) + kind] for i in range(4))
        ada = (g_ada, d_ada, nm_ada, nv_ada)[kind]
        def bigk(name):
            return upd[name][kind][None]
        return [ada[None], o_b_ada.reshape(1, 3 * D), vecs[0], bigk("w_in"), o_conv[None], vecs[1], vecs[2], vecs[3],
                bigk("w_conv_out"), vecs[4], vecs[5], o_w_sgu.reshape(1, HEADS, CHUNK, CHUNK), o_b_sgu[None],
                bigk("w_sgu_out"), bigk("w_o"), vecs[6].reshape(D)]

    return (loss, loc["grad_x"][None], *leaves(0), *leaves(1), *leaves(2), *leaves(3))
```

```python
import functools

import jax
import jax.numpy as jnp
from jax import lax
from jax.experimental import pallas as pl
from jax.experimental.pallas import tpu as pltpu

F32 = jnp.float32
BF16 = jnp.bfloat16
MESH = pl.DeviceIdType.MESH

D = 1024
N_SEC = 8
N_CHIP = 4
N_DEV = 8
EPS = 1e-6
CONV_K = 31
HALO = 32
CHUNK = 128
HEADS = 8
LANE = 128
SUB = 8
PACK = 16
VMEM_LIMIT = 56 * 1024 * 1024

ADAM_LR, ADAM_B1, ADAM_B2, ADAM_EPS, ADAM_WD, ADAM_STEP = 0.001, 0.9, 0.999, 1e-08, 0.01, 10

_SQRT_HALF = 0.7071067811865476
_INV_SQRT_2PI = 0.3989422804014327


def _sds(shape, dtype):
    return jax.ShapeDtypeStruct(shape, dtype)


def _params(sem=None):
    if sem is None:
        return pltpu.CompilerParams(vmem_limit_bytes=VMEM_LIMIT)
    return pltpu.CompilerParams(dimension_semantics=sem, vmem_limit_bytes=VMEM_LIMIT)


def _strips(n_rows, rows, fn):
    def step(s, carry):
        fn(pl.multiple_of(s * rows, rows))
        return carry
    lax.fori_loop(0, n_rows // rows, step, 0)


def _sigmoid(v):
    return 1.0 / (1.0 + jnp.exp(-v))


def _gelu(v):
    return 0.5 * v * (1.0 + lax.erf(v * _SQRT_HALF))


def _gelu_and_grad(v):
    cdf = 0.5 * (1.0 + lax.erf(v * _SQRT_HALF))
    return v * cdf, cdf + v * jnp.exp(-0.5 * v * v) * _INV_SQRT_2PI


def _dsilu(v, sg):
    return sg * (1.0 + v * (1.0 - sg))


def _rowmean(v):
    return jnp.mean(v, axis=-1, keepdims=True)


def _vec_spec(grid_rank):
    zeros = (0, 0)
    if grid_rank == 1:
        return pl.BlockSpec((1, D), lambda i: zeros)
    return pl.BlockSpec((1, D), lambda i, j: zeros)


def _conv_taps(win_ref, r0, lt, weight_of_offset, rows):
    lanes = pl.ds(lt * LANE, LANE)
    win = win_ref[pl.ds(r0, rows + HALO), lanes]
    n_out = rows // SUB
    acc = [jnp.zeros((SUB, LANE), F32) for _ in range(n_out)]
    for phase in range(SUB):
        offs = [o for o in weight_of_offset if o % SUB == phase]
        if not offs:
            continue
        q_max = max(o // SUB for o in offs)
        span = (n_out + q_max) * SUB
        sh = win[phase:phase + span, :]
        for o in offs:
            q = o // SUB
            w = weight_of_offset[o](lanes)
            for m in range(n_out):
                acc[m] = acc[m] + w * sh[(m + q) * SUB:(m + q + 1) * SUB, :]
    return acc


def _branch_a_fwd(p, conv_wb, conv_b, ln_g, ln_b):
    s_len = p.shape[0]
    tm = min(256, s_len)
    n_i = s_len // tm
    rows = 32

    def body(p_ref, wb_ref, cb_ref, g_ref, b_ref, ya_ref, y1_ref, abuf):
        @pl.when(pl.program_id(0) == 0)
        def _():
            abuf[pl.ds(0, HALO), :] = jnp.zeros((HALO, D), F32)

        def glu(r0):
            val = p_ref[pl.ds(r0, PACK), pl.ds(0, D)].astype(F32)
            gl = p_ref[pl.ds(r0, PACK), pl.ds(D, D)].astype(F32)
            abuf[pl.ds(HALO + r0, PACK), :] = val * _sigmoid(gl)
        _strips(tm, PACK,glu)

        taps = {HALO - (CONV_K - 1) + k: (lambda lanes, k=k: wb_ref[pl.ds(k * SUB, SUB), lanes]) for k in range(CONV_K)}

        def conv(r0):
            for lt in range(D // LANE):
                acc = _conv_taps(abuf, r0, lt, taps, rows)
                cb = cb_ref[:, pl.ds(lt * LANE, LANE)]
                for m, v in enumerate(acc):
                    y1_ref[pl.ds(r0 + m * SUB, SUB), pl.ds(lt * LANE, LANE)] = v + cb
        _strips(tm, rows, conv)

        def norm(r0):
            y1 = y1_ref[pl.ds(r0, PACK), :]
            mu = _rowmean(y1)
            yc = y1 - mu
            rstd = lax.rsqrt(_rowmean(yc * yc) + EPS)
            l1 = (yc * rstd) * g_ref[...] + b_ref[...]
            z = p_ref[pl.ds(r0, PACK), pl.ds(2 * D, D)].astype(F32)
            ya_ref[pl.ds(r0, PACK), :] = ((l1 * _sigmoid(l1)) * (z * _sigmoid(z))).astype(BF16)
        _strips(tm, PACK,norm)

        abuf[pl.ds(0, HALO), :] = abuf[pl.ds(tm, HALO), :]

    return pl.pallas_call(
        body, name="branch_a_fwd", grid=(n_i,),
        in_specs=[pl.BlockSpec((tm, 3 * D), lambda i: (i, 0)),
                  pl.BlockSpec((CONV_K * SUB, D), lambda i: (0, 0)), _vec_spec(1), _vec_spec(1), _vec_spec(1)],
        out_specs=[pl.BlockSpec((tm, D), lambda i: (i, 0)), pl.BlockSpec((tm, D), lambda i: (i, 0))],
        out_shape=[_sds((s_len, D), BF16), _sds((s_len, D), F32)],
        scratch_shapes=[pltpu.VMEM((tm + HALO, D), F32)],
        compiler_params=_params(("arbitrary",)),
    )(p, conv_wb, conv_b, ln_g, ln_b)


def _branch_b_fwd(p, wt, bias_full, ln_g, ln_b):
    s_len = p.shape[0]
    tm = min(256, s_len)
    n_i = s_len // tm

    def body(p_ref, wt_ref, bias_ref, g_ref, b_ref, yb_ref, vb, sbuf):
        def norm(r0):
            gv = _gelu(p_ref[pl.ds(r0, PACK), pl.ds(D, D)].astype(F32))
            mu = _rowmean(gv)
            vc = gv - mu
            rstd = lax.rsqrt(_rowmean(vc * vc) + EPS)
            vb[pl.ds(r0, PACK), :] = ((vc * rstd) * g_ref[...] + b_ref[...]).astype(BF16)
        _strips(tm, PACK,norm)

        for ck in range(tm // CHUNK):
            for h in range(HEADS):
                blk = (pl.ds(ck * CHUNK, CHUNK), pl.ds(h * LANE, LANE))
                sbuf[blk] = jnp.dot(wt_ref[h], vb[blk], preferred_element_type=F32) + bias_ref[:, pl.ds(h * LANE, LANE)]

        def gate(r0):
            u = _gelu(p_ref[pl.ds(r0, PACK), pl.ds(0, D)].astype(F32))
            z = p_ref[pl.ds(r0, PACK), pl.ds(2 * D, D)].astype(F32)
            yb_ref[pl.ds(r0, PACK), :] = (u * sbuf[pl.ds(r0, PACK), :] * (z * _sigmoid(z))).astype(BF16)
        _strips(tm, PACK,gate)

    return pl.pallas_call(
        body, name="branch_b_fwd", grid=(n_i,),
        in_specs=[pl.BlockSpec((tm, 3 * D), lambda i: (i, 1)),
                  pl.BlockSpec((HEADS, CHUNK, CHUNK), lambda i: (0, 0, 0)),
                  pl.BlockSpec((CHUNK, D), lambda i: (0, 0)), _vec_spec(1), _vec_spec(1)],
        out_specs=pl.BlockSpec((tm, D), lambda i: (i, 0)),
        out_shape=_sds((s_len, D), BF16),
        scratch_shapes=[pltpu.VMEM((tm, D), BF16), pltpu.VMEM((tm, D), F32)],
        compiler_params=_params(("arbitrary",)),
    )(p, wt, bias_full, ln_g, ln_b)


def _dot_t(a, b):
    return lax.dot_general(a, b, (((1,), (1,)), ((), ())), preferred_element_type=F32)


def _out_proj(p, ya_in, yb_in, x, target, gate, g_final, w_co, w_so, w_o):
    s_len = x.shape[0]
    tm = min(256, s_len)
    n_i = s_len // tm

    def body(pg_ref, ya_ref, yb_ref, x_ref, t_ref, gate_ref, gf_ref, wco_ref, wso_ref, wo_ref,
             dx2_ref, dya_ref, dyb_ref, dp_ref, mb_ref, dob_ref, dyab_ref, dybb_ref, sums_ref):
        @pl.when(pl.program_id(0) == 0)
        def _():
            sums_ref[...] = jnp.zeros((SUB, D), F32)

        y_a = jnp.dot(ya_ref[...], wco_ref[...], preferred_element_type=F32)
        y_b = jnp.dot(yb_ref[...], wso_ref[...], preferred_element_type=F32)
        ga = _sigmoid(pg_ref[:, pl.ds(0, D)].astype(F32))
        gb = _sigmoid(pg_ref[:, pl.ds(D, D)].astype(F32))
        mb = (ga * y_a + gb * y_b).astype(BF16)
        mb_ref[...] = mb
        o = jnp.dot(mb, wo_ref[...], preferred_element_type=F32)
        x2 = x_ref[...] + gate_ref[...] * o
        r2 = lax.rsqrt(_rowmean(x2 * x2) + EPS)
        xh = x2 * r2
        e = xh * gf_ref[...] - t_ref[...]
        dy = e * (1.0 / D)
        dxh = dy * gf_ref[...]
        dx2 = r2 * (dxh - xh * _rowmean(dxh * xh))
        dx2_ref[...] = dx2
        sums_ref[pl.ds(0, 1), :] += jnp.sum(dy * xh, axis=0, keepdims=True)
        sums_ref[pl.ds(1, 1), :] += jnp.sum(dx2 * o, axis=0, keepdims=True)
        sums_ref[pl.ds(2, 1), :] += jnp.sum(e * e, axis=0, keepdims=True) * (0.5 / D)
        dob = (gate_ref[...] * dx2).astype(BF16)
        dob_ref[...] = dob
        dm = _dot_t(dob, wo_ref[...])
        dy_a = (ga * dm).astype(BF16)
        dy_b = (gb * dm).astype(BF16)
        dyab_ref[...] = dy_a
        dybb_ref[...] = dy_b
        dp_ref[:, pl.ds(0, D)] = (dm * y_a * ga * (1.0 - ga)).astype(BF16)
        dp_ref[:, pl.ds(D, D)] = (dm * y_b * gb * (1.0 - gb)).astype(BF16)
        dya_ref[...] = _dot_t(dy_a, wco_ref[...])
        dyb_ref[...] = _dot_t(dy_b, wso_ref[...])

    tile = pl.BlockSpec((tm, D), lambda i: (i, 0))
    wspec = pl.BlockSpec((D, D), lambda i: (0, 0))
    return pl.pallas_call(
        body, name="out_proj", grid=(n_i,),
        in_specs=[pl.BlockSpec((tm, 2 * D), lambda i: (i, 3)), tile, tile, tile, tile, _vec_spec(1), _vec_spec(1),
                  wspec, wspec, wspec],
        out_specs=[tile, tile, tile, pl.BlockSpec((tm, 2 * D), lambda i: (i, 3)), tile, tile, tile, tile,
                   pl.BlockSpec((SUB, D), lambda i: (0, 0))],
        out_shape=[_sds((s_len, D), F32), _sds((s_len, D), F32), _sds((s_len, D), F32), _sds((s_len, N_SEC * D), BF16),
                   _sds((s_len, D), BF16), _sds((s_len, D), BF16), _sds((s_len, D), BF16), _sds((s_len, D), BF16),
                   _sds((SUB, D), F32)],
        compiler_params=_params(("arbitrary",)),
    )(p, ya_in, yb_in, x, target, gate, g_final, w_co, w_so, w_o)


A_STATS_ROWS = 8 + HALO


def _branch_a_bwd(p, y1, dya_in, dp, conv_wb, ln_g, ln_b):
    s_len = p.shape[0]
    tm = min(256, s_len)
    n_i = s_len // tm
    rows = 32
    n_out = rows // SUB

    def tile_of(i):
        return n_i - 1 - i

    def body(p_ref, y1_ref, dya_ref, dp_in, wb_ref, g_ref, b_ref, dp_ref, st_ref, dybuf, acc8, tapacc):
        del dp_in
        i = pl.program_id(0)

        @pl.when(i == 0)
        def _():
            dybuf[pl.ds(tm, HALO), :] = jnp.zeros((HALO, D), F32)
            st_ref[...] = jnp.zeros((A_STATS_ROWS, D), F32)
            acc8[...] = jnp.zeros((3 * PACK, D), F32)
            tapacc[...] = jnp.zeros((CONV_K * SUB, D), F32)

        def norm_bwd(r0):
            y1 = y1_ref[pl.ds(r0, PACK), :]
            mu = _rowmean(y1)
            yc = y1 - mu
            rstd = lax.rsqrt(_rowmean(yc * yc) + EPS)
            n1 = yc * rstd
            l1 = n1 * g_ref[...] + b_ref[...]
            sg = _sigmoid(l1)
            z = p_ref[pl.ds(r0, PACK), pl.ds(2 * D, D)].astype(F32)
            sz = _sigmoid(z)
            dya = dya_ref[pl.ds(r0, PACK), :]
            dp_ref[pl.ds(r0, PACK), pl.ds(2 * D, D)] = (dya * (l1 * sg) * _dsilu(z, sz)).astype(BF16)
            dl1 = dya * (z * sz) * _dsilu(l1, sg)
            acc8[pl.ds(0, PACK), :] += dl1 * n1
            acc8[pl.ds(PACK, PACK), :] += dl1
            dn1 = dl1 * g_ref[...]
            dy1 = rstd * (dn1 - _rowmean(dn1) - n1 * _rowmean(dn1 * n1))
            acc8[pl.ds(2 * PACK, PACK), :] += dy1
            dybuf[pl.ds(r0, PACK), :] = dy1
        _strips(tm, PACK,norm_bwd)

        def conv_bwd(r0):
            for lt in range(D // LANE):
                lanes = pl.ds(lt * LANE, LANE)
                glanes = pl.ds(D + lt * LANE, LANE)
                win = dybuf[pl.ds(r0, rows + HALO), lanes]
                sg16, a16 = [], []
                for h in range(rows // PACK):
                    rr = pl.ds(r0 + h * PACK, PACK)
                    s = _sigmoid(p_ref[rr, glanes].astype(F32))
                    sg16.append(s)
                    a16.append(p_ref[rr, lanes].astype(F32) * s)
                a = [a16[m // 2][(m % 2) * SUB:(m % 2 + 1) * SUB, :] for m in range(n_out)]
                da = [jnp.zeros((SUB, LANE), F32) for _ in range(n_out)]
                for phase in range(SUB):
                    offs = [o for o in range(CONV_K) if o % SUB == phase]
                    q_max = max(o // SUB for o in offs)
                    sh = win[phase:phase + (n_out + q_max) * SUB, :]
                    for o in offs:
                        k, q = CONV_K - 1 - o, o // SUB
                        w = wb_ref[pl.ds(k * SUB, SUB), lanes]
                        part = None
                        for m in range(n_out):
                            s = sh[(m + q) * SUB:(m + q + 1) * SUB, :]
                            da[m] = da[m] + w * s
                            part = a[m] * s if part is None else part + a[m] * s
                        tapacc[pl.ds(k * SUB, SUB), lanes] += part
                for h in range(rows // PACK):
                    rr = pl.ds(r0 + h * PACK, PACK)
                    da16 = jnp.concatenate(da[2 * h:2 * h + 2], axis=0)
                    dp_ref[rr, lanes] = (da16 * sg16[h]).astype(BF16)
                    dp_ref[rr, glanes] = (da16 * a16[h] * (1.0 - sg16[h])).astype(BF16)
        _strips(tm, rows, conv_bwd)

        dybuf[pl.ds(tm, HALO), :] = dybuf[pl.ds(0, HALO), :]

        @pl.when(i == n_i - 1)
        def _():
            for j in range(3):
                st_ref[pl.ds(j, 1), :] = jnp.sum(acc8[pl.ds(j * PACK, PACK), :], axis=0, keepdims=True)
            for k in range(CONV_K):
                st_ref[pl.ds(SUB + k, 1), :] = jnp.sum(tapacc[pl.ds(k * SUB, SUB), :], axis=0, keepdims=True)

    return pl.pallas_call(
        body, name="branch_a_bwd", grid=(n_i,),
        in_specs=[pl.BlockSpec((tm, 3 * D), lambda i: (tile_of(i), 0)),
                  pl.BlockSpec((tm, D), lambda i: (tile_of(i), 0)),
                  pl.BlockSpec((tm, D), lambda i: (tile_of(i), 0)),
                  pl.BlockSpec(memory_space=pl.ANY),
                  pl.BlockSpec((CONV_K * SUB, D), lambda i: (0, 0)), _vec_spec(1), _vec_spec(1)],
        out_specs=[pl.BlockSpec((tm, 3 * D), lambda i: (tile_of(i), 0)),
                   pl.BlockSpec((A_STATS_ROWS, D), lambda i: (0, 0))],
        out_shape=[_sds(dp.shape, BF16), _sds((A_STATS_ROWS, D), F32)],
        scratch_shapes=[pltpu.VMEM((tm + HALO, D), F32), pltpu.VMEM((3 * PACK, D), F32), pltpu.VMEM((CONV_K * SUB, D), F32)],
        input_output_aliases={3: 0},
        compiler_params=_params(("arbitrary",)),
    )(p, y1, dya_in, dp, conv_wb, ln_g, ln_b)


def _branch_b_bwd(p, dyb_in, dp, wt, wtt, bias_full, ln_g, ln_b):
    s_len = p.shape[0]
    tm = min(256, s_len)
    n_i = s_len // tm

    def body(p_ref, dyb_ref, dp_in, wt_ref, wtt_ref, bias_ref, g_ref, b_ref, dp_ref, st_ref, gbt_ref, gw_ref,
             vb, n2buf, rstdbuf, sbuf, dsb, dvbuf, acc8, gb_ref, dgbuf):
        del dp_in
        i = pl.program_id(0)

        @pl.when(i == 0)
        def _():
            st_ref[...] = jnp.zeros((SUB, D), F32)
            gbt_ref[...] = jnp.zeros((CHUNK, LANE), F32)
            gb_ref[...] = jnp.zeros((CHUNK, D), F32)
            gw_ref[...] = jnp.zeros((HEADS, CHUNK, CHUNK), F32)
            acc8[...] = jnp.zeros((2 * PACK, D), F32)

        def norm(r0):
            gv, dgv = _gelu_and_grad(p_ref[pl.ds(r0, PACK), pl.ds(D, D)].astype(F32))
            dgbuf[pl.ds(r0, PACK), :] = dgv
            mu = _rowmean(gv)
            vc = gv - mu
            rstd = lax.rsqrt(_rowmean(vc * vc) + EPS)
            n2 = vc * rstd
            n2buf[pl.ds(r0, PACK), :] = n2
            rstdbuf[pl.ds(r0, PACK), :] = jnp.broadcast_to(rstd, (PACK, LANE))
            vb[pl.ds(r0, PACK), :] = (n2 * g_ref[...] + b_ref[...]).astype(BF16)
        _strips(tm, PACK,norm)

        for ck in range(tm // CHUNK):
            for h in range(HEADS):
                blk = (pl.ds(ck * CHUNK, CHUNK), pl.ds(h * LANE, LANE))
                sbuf[blk] = jnp.dot(wt_ref[h], vb[blk], preferred_element_type=F32) + bias_ref[:, pl.ds(h * LANE, LANE)]

        def gate_bwd(r0):
            pu = p_ref[pl.ds(r0, PACK), pl.ds(0, D)].astype(F32)
            u, du = _gelu_and_grad(pu)
            z = p_ref[pl.ds(r0, PACK), pl.ds(2 * D, D)].astype(F32)
            sg = _sigmoid(z)
            sz = z * sg
            s = sbuf[pl.ds(r0, PACK), :]
            dyb = dyb_ref[pl.ds(r0, PACK), :]
            ds = dyb * u * sz
            dsb[pl.ds(r0, PACK), :] = ds.astype(BF16)
            gb_ref[pl.ds(pl.multiple_of(r0 % CHUNK, PACK), PACK), :] += ds
            dp_ref[pl.ds(r0, PACK), pl.ds(0, D)] = (dyb * s * sz * du).astype(BF16)
            dp_ref[pl.ds(r0, PACK), pl.ds(2 * D, D)] = (dyb * u * s * _dsilu(z, sg)).astype(BF16)
        _strips(tm, PACK,gate_bwd)

        for ck in range(tm // CHUNK):
            for h in range(HEADS):
                blk = (pl.ds(ck * CHUNK, CHUNK), pl.ds(h * LANE, LANE))
                d_s = dsb[blk]
                dvbuf[blk] = jnp.dot(wtt_ref[h], d_s, preferred_element_type=F32)
                gw_ref[h] += _dot_t(d_s, vb[blk])

        def norm_bwd(r0):
            dv = dvbuf[pl.ds(r0, PACK), :]
            n2 = n2buf[pl.ds(r0, PACK), :]
            rstd = rstdbuf[pl.ds(r0, PACK), pl.ds(0, 1)]
            acc8[pl.ds(0, PACK), :] += dv * n2
            acc8[pl.ds(PACK, PACK), :] += dv
            dn2 = dv * g_ref[...]
            dgv = rstd * (dn2 - _rowmean(dn2) - n2 * _rowmean(dn2 * n2))
            dp_ref[pl.ds(r0, PACK), pl.ds(D, D)] = (dgv * dgbuf[pl.ds(r0, PACK), :]).astype(BF16)
        _strips(tm, PACK,norm_bwd)

        @pl.when(i == n_i - 1)
        def _():
            for j in range(2):
                st_ref[pl.ds(j, 1), :] = jnp.sum(acc8[pl.ds(j * PACK, PACK), :], axis=0, keepdims=True)
            row = lax.broadcasted_iota(jnp.int32, (CHUNK, CHUNK), 0)
            col = lax.broadcasted_iota(jnp.int32, (CHUNK, CHUNK), 1)
            for h in range(HEADS):
                gw_ref[h] = jnp.where(row >= col, gw_ref[h], 0.0)
            lane = lax.broadcasted_iota(jnp.int32, (CHUNK, LANE), 1)
            gbt = jnp.zeros((CHUNK, LANE), F32)
            for h in range(HEADS):
                gbt = jnp.where(lane == h, jnp.sum(gb_ref[:, pl.ds(h * LANE, LANE)], axis=1, keepdims=True), gbt)
            gbt_ref[...] = gbt

    wspec = pl.BlockSpec((HEADS, CHUNK, CHUNK), lambda i: (0, 0, 0))
    return pl.pallas_call(
        body, name="branch_b_bwd", grid=(n_i,),
        in_specs=[pl.BlockSpec((tm, 3 * D), lambda i: (i, 1)), pl.BlockSpec((tm, D), lambda i: (i, 0)),
                  pl.BlockSpec(memory_space=pl.ANY), wspec, wspec,
                  pl.BlockSpec((CHUNK, D), lambda i: (0, 0)), _vec_spec(1), _vec_spec(1)],
        out_specs=[pl.BlockSpec((tm, 3 * D), lambda i: (i, 1)), pl.BlockSpec((SUB, D), lambda i: (0, 0)),
                   pl.BlockSpec((CHUNK, LANE), lambda i: (0, 0)), wspec],
        out_shape=[_sds(dp.shape, BF16), _sds((SUB, D), F32), _sds((CHUNK, LANE), F32), _sds((HEADS, CHUNK, CHUNK), F32)],
        scratch_shapes=[pltpu.VMEM((tm, D), BF16), pltpu.VMEM((tm, D), F32), pltpu.VMEM((tm, LANE), F32),
                        pltpu.VMEM((tm, D), F32), pltpu.VMEM((tm, D), BF16), pltpu.VMEM((tm, D), F32),
                        pltpu.VMEM((2 * PACK, D), F32), pltpu.VMEM((CHUNK, D), F32), pltpu.VMEM((tm, D), F32)],
        input_output_aliases={2: 0},
        compiler_params=_params(("arbitrary",)),
    )(p, dyb_in, dp, wt, wtt, bias_full, ln_g, ln_b)


def _in_proj_bwd(dp, wg_in, x, dx2, shift, scale, g_pre):
    del shift
    s_len = x.shape[0]
    tm = min(512, s_len)
    n_i = s_len // tm
    wn = wg_in.shape[2]

    def body(dp0, dp1, dp2, dp3, w_ref, x_ref, dx2_ref, sc_ref, g_ref, gx_ref, st_ref, acc, acc8):
        i = pl.program_id(0)

        @pl.when(i == 0)
        def _():
            st_ref[...] = jnp.zeros((SUB, D), F32)
            acc8[...] = jnp.zeros((3 * PACK, D), F32)

        dh = _dot_t(dp0[...], w_ref[0])
        for j, dp_ref in enumerate((dp1, dp2, dp3), start=1):
            dh = dh + _dot_t(dp_ref[...], w_ref[j])
        acc[...] = dh

        def strip(r0):
            xs = x_ref[pl.ds(r0, PACK), :]
            r = lax.rsqrt(_rowmean(xs * xs) + EPS)
            xn = xs * r
            dhs = acc[pl.ds(r0, PACK), :]
            acc8[pl.ds(0, PACK), :] += dhs
            acc8[pl.ds(PACK, PACK), :] += dhs * (xn * g_ref[...])
            dhp = dhs * (1.0 + sc_ref[...])
            acc8[pl.ds(2 * PACK, PACK), :] += dhp * xn
            dxn = dhp * g_ref[...]
            gx_ref[pl.ds(r0, PACK), :] = dx2_ref[pl.ds(r0, PACK), :] + r * (dxn - xn * _rowmean(dxn * xn))
        _strips(tm, PACK, strip)

        @pl.when(i == n_i - 1)
        def _():
            for k in range(3):
                st_ref[pl.ds(k, 1), :] = jnp.sum(acc8[pl.ds(k * PACK, PACK), :], axis=0, keepdims=True)

    tile = pl.BlockSpec((tm, D), lambda i: (i, 0))
    return pl.pallas_call(
        body, name="in_proj_bwd", grid=(n_i,),
        in_specs=[pl.BlockSpec((tm, wn), functools.partial(lambda j, i: (i, j), j)) for j in range(N_CHIP)] + [
                  pl.BlockSpec((N_CHIP, D, wn), lambda i: (0, 0, 0), pipeline_mode=pl.Buffered(1)),
                  tile, tile, _vec_spec(1), _vec_spec(1)],
        out_specs=[tile, pl.BlockSpec((SUB, D), lambda i: (0, 0))],
        out_shape=[_sds((s_len, D), F32), _sds((SUB, D), F32)],
        scratch_shapes=[pltpu.VMEM((tm, D), F32), pltpu.VMEM((3 * PACK, D), F32)],
        compiler_params=_params(("arbitrary",)),
    )(dp, dp, dp, dp, wg_in, x, dx2, scale, g_pre)


def _grad_matmul(a, b, name):
    s_len, n = b.shape
    cb = min(2 * D, n)
    tn = 512
    per = cb // tn

    def body(a_ref, b_ref, ob_ref):
        ob_ref[0] = lax.dot_general(a_ref[...], b_ref[...], (((0,), (0,)), ((), ())),
                                    preferred_element_type=F32).astype(BF16)

    return pl.pallas_call(
        body, name=name, grid=(n // tn,),
        in_specs=[pl.BlockSpec((s_len, D), lambda j: (0, 0), pipeline_mode=pl.Buffered(1)),
                  pl.BlockSpec((s_len, tn), lambda j: (0, j))],
        out_specs=pl.BlockSpec((1, D, tn), lambda j: (j // per, 0, j % per)),
        out_shape=_sds((n // cb, D, cb), BF16),
        compiler_params=_params(("arbitrary",)),
    )(a, b)


def _local_step(x, target, shift, scale, gate, g_pre, conv_w_full, conv_b, conv_ln_g, conv_ln_b,
                sgu_ln_g, sgu_ln_b, w_sgu, b_sgu, g_final, wg_in, out_shards):
    conv_wb = jnp.repeat(conv_w_full, SUB, axis=0)
    causal = jnp.tril(jnp.ones((CHUNK, CHUNK), dtype=bool))
    wt = jnp.where(causal[None], w_sgu, 0.0).astype(BF16)
    wtt = jnp.swapaxes(wt, 1, 2)
    bias_full = jnp.repeat(b_sgu.T, LANE, axis=1)

    p, hb, gathered = _in_proj_gather(x, shift, scale, g_pre, wg_in, out_shards)
    w_co, w_so, w_o = (g.reshape(D, D) for g in gathered)
    ya_in, y1 = _branch_a_fwd(p, conv_wb, conv_b, conv_ln_g, conv_ln_b)
    yb_in = _branch_b_fwd(p, wt, bias_full, sgu_ln_g, sgu_ln_b)
    dx2, dya_in, dyb_in, dp, mb, dob, dyab, dybb, sums_o = _out_proj(
        p, ya_in, yb_in, x, target, gate, g_final, w_co, w_so, w_o)
    dp, st_a = _branch_a_bwd(p, y1, dya_in, dp, conv_wb, conv_ln_g, conv_ln_b)
    dp, st_b, gbt, gws = _branch_b_bwd(p, dyb_in, dp, wt, wtt, bias_full, sgu_ln_g, sgu_ln_b)
    grad_x, st_i = _in_proj_bwd(dp, wg_in, x, dx2, shift, scale, g_pre)
    gw_o = _grad_matmul(mb, dob, "grad_w_o")
    gw_co = _grad_matmul(ya_in, dyab, "grad_w_conv_out")
    gw_so = _grad_matmul(yb_in, dybb, "grad_w_sgu_out")
    return dict(
        grad_x=grad_x, loss_cols=sums_o[2:3], g_final=sums_o[0:1], d_gate=sums_o[1:2],
        d_shift=st_i[0:1], d_scale=st_i[1:2], g_pre=st_i[2:3],
        conv_ln_g=st_a[0:1], conv_ln_b=st_a[1:2], conv_b=st_a[2:3], conv_w=st_a[SUB:SUB + CONV_K],
        sgu_ln_g=st_b[0:1], sgu_ln_b=st_b[1:2], b_sgu=gbt[:, :HEADS].T, w_sgu=gws,
        hb=hb, dp=dp, w_o=gw_o, w_conv_out=gw_co, w_sgu_out=gw_so)


ANY_SPEC = pl.BlockSpec(memory_space=pl.ANY)
VMEM_SPEC = pl.BlockSpec(memory_space=pltpu.VMEM)


def _place():
    return lax.axis_index("x"), lax.axis_index("y"), lax.axis_index("c")


def _peer(k):
    x, y, c = _place()
    return (1 - x if k & 4 else x, 1 - y if k & 2 else y, 1 - c if k & 1 else c)


def _dev_of(p):
    return 4 * p[0] + 2 * p[1] + p[2]


def _chip_of(p):
    return 2 * p[0] + p[1]


def _rdma(src, dst, send_sem, recv_sem, to):
    return pltpu.make_async_remote_copy(src_ref=src, dst_ref=dst, send_sem=send_sem, recv_sem=recv_sem,
                                        device_id=to, device_id_type=MESH)


CHIP_PEERS = (2, 4, 6)
ALL_PEERS = tuple(range(1, N_DEV))
SIBLING = 1


def _setup_comm(c8, w_ada_s, b_ada_s, convw_s, shards):
    n_mod = w_ada_s.shape[1]
    rows = SUB * N_DEV
    n = len(shards)

    def body(c8_ref, wada_ref, bada_ref, cw_ref, *refs):
        ins, (call_ref, mod_ref, cwall_ref), outs = refs[:n], refs[n:n + 3], refs[n + 3:2 * n + 3]
        csend, crecv, wsend, wrecv, msend, mrecv = refs[2 * n + 3:2 * n + 9]
        gather_a, gather_b, gather_c = _gather_phases([s.shape[0] for s in shards], ins, outs, refs[2 * n + 9:])
        gather_a()
        me = _place()
        dev, chip = _dev_of(me), _chip_of(me)

        def c_rows(d):
            return call_ref.at[pl.ds(pl.multiple_of(d * SUB, SUB), SUB), :]

        call_ref[pl.ds(pl.multiple_of(dev * SUB, SUB), SUB), :] = c8_ref[...]
        cwall_ref[chip] = cw_ref[...]
        c_out = [_rdma(c8_ref, c_rows(dev), csend.at[k], crecv.at[k], _peer(k)) for k in ALL_PEERS]
        w_out = [_rdma(cw_ref, cwall_ref.at[chip], wsend.at[k], wrecv.at[k], _peer(k)) for k in CHIP_PEERS]
        for cp in c_out + w_out:
            cp.start()
        for k in ALL_PEERS:
            _rdma(c8_ref, c_rows(_dev_of(_peer(k))), csend.at[k], crecv.at[k], _peer(k)).wait_recv()
        part = jnp.dot(call_ref[...].astype(BF16), wada_ref[...].astype(BF16), preferred_element_type=F32) + bada_ref[...]
        mod_ref[chip] = part
        m_out = [_rdma(mod_ref.at[chip], mod_ref.at[chip], msend.at[k], mrecv.at[k], _peer(k)) for k in CHIP_PEERS]
        for cp in m_out:
            cp.start()
        for k in CHIP_PEERS:
            pc = _chip_of(_peer(k))
            _rdma(cw_ref, cwall_ref.at[pc], wsend.at[k], wrecv.at[k], _peer(k)).wait_recv()
            _rdma(mod_ref.at[pc], mod_ref.at[pc], msend.at[k], mrecv.at[k], _peer(k)).wait_recv()
        for cp in c_out + w_out + m_out:
            cp.wait_send()
        gather_b()
        gather_c()

    res = pl.pallas_call(
        body, name="setup_comm",
        in_specs=[VMEM_SPEC] * (4 + n), out_specs=[VMEM_SPEC] * (3 + n),
        out_shape=([_sds((rows, D), F32), _sds((N_CHIP, rows, n_mod), F32), _sds((N_CHIP,) + convw_s.shape, F32)]
                   + [_sds((N_CHIP,) + s.shape, s.dtype) for s in shards]),
        scratch_shapes=([pltpu.SemaphoreType.DMA((N_DEV,))] * 6
                        + [pltpu.SemaphoreType.DMA((n,))] + [pltpu.SemaphoreType.DMA((n, len(CHIP_PEERS)))] * 4),
        compiler_params=_params(),
    )(c8, w_ada_s, b_ada_s, convw_s, *shards)
    return res[0], res[1], res[2], res[3:]


def _gather_phases(row_counts, ins, dsts, sems):
    n = len(row_counts)
    lsem, isend, irecv, dsend, drecv = sems
    me = _place()
    chip, c = _chip_of(me), me[2]

    def half(t, which):
        hr = row_counts[t] // 2
        return pl.ds(pl.multiple_of(which * hr, hr), hr)

    def local(t):
        return pltpu.make_async_copy(ins[t], dsts[t].at[chip], lsem.at[t])

    def to_chip(t, j):
        return _rdma(ins[t].at[half(t, c)], dsts[t].at[chip, half(t, c)], isend.at[t, j], irecv.at[t, j], _peer(CHIP_PEERS[j]))

    def landed(t, j, which):
        return dsts[t].at[_chip_of(_peer(CHIP_PEERS[j])), half(t, which)]

    def to_sibling(t, j):
        return _rdma(landed(t, j, c), landed(t, j, c), dsend.at[t, j], drecv.at[t, j], _peer(SIBLING))

    pairs = [(t, j) for t in range(n) for j in range(len(CHIP_PEERS))]

    def phase_a():
        for t in range(n):
            local(t).start()
        for t, j in pairs:
            to_chip(t, j).start()

    def phase_b():
        for t, j in pairs:
            _rdma(landed(t, j, c), landed(t, j, c), isend.at[t, j], irecv.at[t, j], _peer(CHIP_PEERS[j])).wait_recv()
            to_sibling(t, j).start()

    def phase_c():
        for t, j in pairs:
            _rdma(landed(t, j, 1 - c), landed(t, j, 1 - c), dsend.at[t, j], drecv.at[t, j], _peer(SIBLING)).wait_recv()
        for t, j in pairs:
            to_chip(t, j).wait_send()
            to_sibling(t, j).wait_send()
        for t in range(n):
            local(t).wait()

    return phase_a, phase_b, phase_c


def _in_proj_gather(x, shift, scale, g_pre, wg_in, shards):
    s_len = x.shape[0]
    tm = min(256, s_len)
    n_i = s_len // tm
    wn = wg_in.shape[2]
    n = len(shards)

    def body(x_ref, sh_ref, sc_ref, g_ref, w_ref, *refs):
        ins, p_ref, hb_ref, outs = refs[:n], refs[n], refs[n + 1], refs[n + 2:2 * n + 2]
        gath, sems = refs[2 * n + 2:3 * n + 2], refs[3 * n + 2:]
        phases = _gather_phases([s.shape[0] for s in shards], ins, gath, sems)
        i = pl.program_id(0)
        for step, phase in zip((0, n_i // 2, n_i - 1), phases):
            pl.when(i == step)(phase)

        @pl.when(i == n_i - 1)
        def _():
            for t in range(n):
                outs[t][...] = gath[t][...]

        def strip(r0):
            xs = x_ref[pl.ds(r0, PACK), :]
            r = lax.rsqrt(_rowmean(xs * xs) + EPS)
            h = (xs * r) * g_ref[...] * (1.0 + sc_ref[...]) + sh_ref[...]
            hb_ref[pl.ds(r0, PACK), :] = h.astype(BF16)
        _strips(tm, PACK, strip)
        hb = hb_ref[...]
        for j in range(N_CHIP):
            p_ref[:, pl.ds(j * wn, wn)] = jnp.dot(hb, w_ref[j], preferred_element_type=F32).astype(BF16)

    res = pl.pallas_call(
        body, name="in_proj", grid=(n_i,),
        in_specs=[pl.BlockSpec((tm, D), lambda i: (i, 0)), _vec_spec(1), _vec_spec(1), _vec_spec(1),
                  pl.BlockSpec((N_CHIP, D, wn), lambda i: (0, 0, 0), pipeline_mode=pl.Buffered(1))] + [VMEM_SPEC] * n,
        out_specs=[pl.BlockSpec((tm, N_CHIP * wn), lambda i: (i, 0)), pl.BlockSpec((tm, D), lambda i: (i, 0))] + [VMEM_SPEC] * n,
        out_shape=([_sds((s_len, N_SEC * D), BF16), _sds((s_len, D), BF16)]
                   + [_sds((N_CHIP,) + s.shape, s.dtype) for s in shards]),
        scratch_shapes=([pltpu.VMEM((N_CHIP,) + s.shape, s.dtype) for s in shards]
                        + [pltpu.SemaphoreType.DMA((n,))] + [pltpu.SemaphoreType.DMA((n, len(CHIP_PEERS)))] * 4),
        compiler_params=_params(("arbitrary",)),
    )(x, shift, scale, g_pre, wg_in, *shards)
    return res[0], res[1], res[2:]


def _reduce_scatter(grads, name):
    n = len(grads)
    shapes = [g.shape[2:] for g in grads]
    parts = 2
    part_shapes = [(r, cols // parts) for r, cols in shapes for _ in range(parts)]

    def body(*refs):
        def halves(group, lead):
            return [ref.at[(slice(None),) * lead + (pl.ds(k * (s[1] // parts), s[1] // parts),)]
                    for ref, s in zip(group, shapes) for k in range(parts)]
        ins, outs = halves(refs[:n], 3), halves(refs[n:2 * n], 2)
        pbufs, rbufs, accs = halves(refs[2 * n:3 * n], 2), halves(refs[3 * n:4 * n], 2), halves(refs[4 * n:5 * n], 1)
        for phase in _reduce_phases(part_shapes, ins, outs, pbufs, rbufs, accs, refs[5 * n:]):
            phase()

    return pl.pallas_call(
        body, name=name,
        in_specs=[VMEM_SPEC] * n, out_specs=[VMEM_SPEC] * n,
        out_shape=[_sds((2,) + s, F32) for s in shapes],
        scratch_shapes=_reduce_scratch(shapes)[:3 * n] + _reduce_scratch(part_shapes)[3 * parts * n:],
        compiler_params=_params(),
    )(*grads)


def _reduce_phases(shapes, ins, outs, pbufs, rbufs, accs, sems):
    n = len(shapes)
    psend, precv, csend, crecv, fsend, frecv = sems
    me = _place()
    chip, c = _chip_of(me), me[2]
    sib = _peer(SIBLING)

    def to_sibling(t, d):
        return _rdma(ins[t].at[d, 1 - c], pbufs[t].at[d], psend.at[t, d], precv.at[t, d], sib)

    def to_chip(t, j):
        return _rdma(pbufs[t].at[jnp.bitwise_xor(chip, j)], rbufs[t].at[j - 1], csend.at[t, j], crecv.at[t, j], _peer(2 * j))

    def finished(t):
        return _rdma(outs[t].at[c], outs[t].at[c], fsend.at[t], frecv.at[t], sib)

    def phase_a():
        for j in (1, 2, 3, 0):
            for t in range(n):
                to_sibling(t, jnp.bitwise_xor(chip, j)).start()

    def phase_b():
        for j in (1, 2, 3, 0):
            d = jnp.bitwise_xor(chip, j)
            for t in range(n):
                to_sibling(t, d).wait_recv()

                def pair_sum(r0, t=t, d=d, j=j):
                    rows = pl.ds(r0, PACK)
                    s = ins[t][d, c, rows, :].astype(F32) + pbufs[t][d, rows, :].astype(F32)
                    if j == 0:
                        accs[t][rows, :] = s
                    else:
                        pbufs[t][d, rows, :] = s.astype(BF16)
                _strips(shapes[t][0], PACK, pair_sum)
                if j:
                    to_chip(t, j).start()

    def phase_c():
        for t in range(n):
            for j in (1, 2, 3):
                blk = rbufs[t].at[j - 1]
                _rdma(blk, blk, csend.at[t, j], crecv.at[t, j], _peer(2 * j)).wait_recv()

            def total(r0, t=t):
                rows = pl.ds(r0, PACK)
                s = accs[t][rows, :] + rbufs[t][0, rows, :].astype(F32)
                s = s + rbufs[t][1, rows, :].astype(F32)
                outs[t][c, rows, :] = s + rbufs[t][2, rows, :].astype(F32)
            _strips(shapes[t][0], PACK, total)
            finished(t).start()

    def phase_d():
        for t in range(n):
            blk = outs[t].at[1 - c]
            _rdma(blk, blk, fsend.at[t], frecv.at[t], sib).wait_recv()
        for t in range(n):
            for d in range(N_CHIP):
                to_sibling(t, d).wait_send()
            for j in (1, 2, 3):
                to_chip(t, j).wait_send()
            finished(t).wait_send()

    return phase_a, phase_b, phase_c, phase_d


def _sum_small_phases(ins, outs, pbufs, buf4s, sems):
    n = len(ins)
    psend, precv, send, recv = sems
    chip = _chip_of(_place())

    def swap(t):
        return _rdma(ins[t], pbufs[t], psend.at[t], precv.at[t], _peer(SIBLING))

    def to_chip(t, k):
        return _rdma(buf4s[t].at[chip], buf4s[t].at[chip], send.at[t, k], recv.at[t, k], _peer(k))

    def phase_a():
        for t in range(n):
            swap(t).start()

    def phase_b():
        for t in range(n):
            swap(t).wait()
            buf4s[t][chip] = ins[t][...] + pbufs[t][...]
            for k in CHIP_PEERS:
                to_chip(t, k).start()

    def phase_c():
        for t in range(n):
            for k in CHIP_PEERS:
                blk = buf4s[t].at[_chip_of(_peer(k))]
                _rdma(blk, blk, send.at[t, k], recv.at[t, k], _peer(k)).wait_recv()
            outs[t][...] = (buf4s[t][0] + buf4s[t][1]) + (buf4s[t][2] + buf4s[t][3])

    def phase_d():
        for t in range(n):
            for k in CHIP_PEERS:
                to_chip(t, k).wait_send()

    return phase_a, phase_b, phase_c, phase_d


def _sum_small_scratch(blobs):
    n = len(blobs)
    return ([pltpu.VMEM(b.shape, F32) for b in blobs] + [pltpu.VMEM((N_CHIP,) + b.shape, F32) for b in blobs]
            + [pltpu.SemaphoreType.DMA((n,))] * 2 + [pltpu.SemaphoreType.DMA((n, N_DEV))] * 2)


def _reduce_scratch(shapes):
    n = len(shapes)
    return ([pltpu.VMEM((N_CHIP,) + s, BF16) for s in shapes] + [pltpu.VMEM((N_CHIP - 1,) + s, BF16) for s in shapes]
            + [pltpu.VMEM(s, F32) for s in shapes]
            + [pltpu.SemaphoreType.DMA((n, N_CHIP))] * 4 + [pltpu.SemaphoreType.DMA((n,))] * 2)


def _grad_matmul_reduce(a, b, name, grads, blobs):
    s_len, n_cols = b.shape
    cb = min(2 * D, n_cols)
    tn = 512
    per = cb // tn
    steps = n_cols // tn
    n, nb = len(grads), len(blobs)
    shapes = [g.shape[2:] for g in grads]
    n_red = len(_reduce_scratch(shapes))

    def body(a_ref, b_ref, *refs):
        ins, bins = refs[:n], refs[n:n + nb]
        ob_ref, outs, bouts = refs[n + nb], refs[n + nb + 1:2 * n + nb + 1], refs[2 * n + nb + 1:2 * (n + nb) + 1]
        scratch = refs[2 * (n + nb) + 1:]
        fulls, red, small = scratch[:n], scratch[n:n + n_red], scratch[n + n_red:]
        phases = _reduce_phases(shapes, ins, fulls, red[:n], red[n:2 * n], red[2 * n:3 * n], red[3 * n:])
        small_phases = _sum_small_phases(bins, bouts, small[:nb], small[nb:2 * nb], small[2 * nb:])
        j = pl.program_id(0)
        for step, phase in zip((0, 2, steps - 2, steps - 1), phases):
            pl.when(j == step)(phase)
        for step, phase in zip((1, 3, steps - 2, steps - 1), small_phases):
            pl.when(j == step)(phase)

        @pl.when(j == steps - 1)
        def _():
            for t in range(n):
                outs[t][...] = fulls[t][...]
        ob_ref[0] = lax.dot_general(a_ref[...], b_ref[...], (((0,), (0,)), ((), ())),
                                    preferred_element_type=F32).astype(BF16)

    res = pl.pallas_call(
        body, name=name, grid=(steps,),
        in_specs=[pl.BlockSpec((s_len, D), lambda j: (0, 0), pipeline_mode=pl.Buffered(1)),
                  pl.BlockSpec((s_len, tn), lambda j: (0, j))] + [VMEM_SPEC] * (n + nb),
        out_specs=[pl.BlockSpec((1, D, tn), lambda j: (j // per, 0, j % per))] + [VMEM_SPEC] * (n + nb),
        out_shape=([_sds((n_cols // cb, D, cb), BF16)] + [_sds((2,) + s, F32) for s in shapes]
                   + [_sds(bl.shape, F32) for bl in blobs]),
        scratch_shapes=[pltpu.VMEM((2,) + s, F32) for s in shapes] + _reduce_scratch(shapes) + _sum_small_scratch(blobs),
        compiler_params=_params(("arbitrary",)),
    )(a, b, *grads, *blobs)
    return res[0], res[1:1 + n], res[1 + n:]


def _adamw_math(w, g, m, v):
    m = ADAM_B1 * m + (1.0 - ADAM_B1) * g
    v = ADAM_B2 * v + (1.0 - ADAM_B2) * (g * g)
    m_hat = m / (1.0 - ADAM_B1 ** ADAM_STEP)
    v_hat = v / (1.0 - ADAM_B2 ** ADAM_STEP)
    delta = -ADAM_LR * (m_hat / (jnp.sqrt(v_hat) + ADAM_EPS) + ADAM_WD * w)
    return delta, m, v


def _row_tile(r, cols):
    if r * cols * 4 <= 2 ** 20:
        return r
    return next(t for t in (512, 256, 128, 64, 32, 16, 8) if r % t == 0 and t * cols * 4 <= 2 ** 20)


def _adamw(w, g, m, v, name):
    r, cols = w.shape
    tr = _row_tile(r, cols)

    def body(w_ref, g_ref, m_ref, v_ref, go_ref, d_ref, nm_ref, nv_ref):
        g = g_ref[...]
        go_ref[...] = g
        d_ref[...], nm_ref[...], nv_ref[...] = _adamw_math(w_ref[...], g, m_ref[...], v_ref[...])

    spec = pl.BlockSpec((tr, cols), lambda i: (i, 0))
    return pl.pallas_call(
        body, name=name, grid=(r // tr,), in_specs=[spec] * 4, out_specs=[spec] * 4,
        out_shape=[_sds((r, cols), F32)] * 4, compiler_params=_params(("arbitrary",)),
    )(w, g, m, v)


def _adamw_ada(w, ct, dm, m, v):
    r, cols = w.shape
    tr = _row_tile(r, cols)

    def body(w_ref, ct_ref, dm_ref, m_ref, v_ref, g_ref, d_ref, nm_ref, nv_ref):
        g = jnp.dot(ct_ref[...], dm_ref[...], preferred_element_type=F32)
        g_ref[...] = g
        d_ref[...], nm_ref[...], nv_ref[...] = _adamw_math(w_ref[...], g, m_ref[...], v_ref[...])

    spec = pl.BlockSpec((tr, cols), lambda i: (i, 0))
    return pl.pallas_call(
        body, name="adamw_ada", grid=(r // tr,),
        in_specs=[spec, pl.BlockSpec((tr, LANE), lambda i: (i, 0)), pl.BlockSpec((LANE, cols), lambda i: (0, 0)), spec, spec],
        out_specs=[spec] * 4, out_shape=[_sds((r, cols), F32)] * 4, compiler_params=_params(("arbitrary",)),
    )(w, ct, dm, m, v)


BLOB_VEC, BLOB_BSGU, BLOB_CONV, BLOB_ADA, BLOB_DMOD, BLOB_LOSS, BLOB_ROWS = 0, 8, 16, 48, 56, 80, 88
N_VEC = 7


def _adamw_small(tot, g_w_sgu, g_conv, params):
    n = len(params)

    def body(*refs):
        tot_ref, gws_ref, gconv_ref = refs[:3]
        wmv = refs[3:3 + 3 * n]
        outs = refs[3 + 3 * n:]
        grads = [tot_ref[pl.ds(BLOB_VEC + i, 1), :] for i in range(N_VEC)]
        grads += [tot_ref[pl.ds(BLOB_BSGU, HEADS), pl.ds(0, CHUNK)], gconv_ref[...], gws_ref[...], tot_ref[pl.ds(BLOB_ADA, 3), :]]
        for i, g in enumerate(grads):
            w_ref, m_ref, v_ref = wmv[3 * i:3 * i + 3]
            d, nm, nv = _adamw_math(w_ref[...], g, m_ref[...], v_ref[...])
            outs[4 * i][...] = g
            outs[4 * i + 1][...] = d
            outs[4 * i + 2][...] = nm
            outs[4 * i + 3][...] = nv

    flat = [a for wmv in params for a in wmv]
    return pl.pallas_call(
        body, name="adamw_small",
        in_specs=[VMEM_SPEC] * (3 + len(flat)), out_specs=[VMEM_SPEC] * (4 * n),
        out_shape=[_sds(wmv[0].shape, F32) for wmv in params for _ in range(4)],
        compiler_params=_params(),
    )(tot, g_w_sgu, g_conv, *flat)


def _set_rows(buf, row, val):
    return lax.dynamic_update_slice(buf, val.astype(F32), (row, 0))


def kernel(x, c, w_ada, b_ada, g_pre, w_in, conv_w, conv_b, conv_ln_g, conv_ln_b, w_conv_out, sgu_ln_g, sgu_ln_b, w_sgu, b_sgu, w_sgu_out, w_o, g_final, loss_target, m_w_ada, m_b_ada, m_g_pre, m_w_in, m_conv_w, m_conv_b, m_conv_ln_g, m_conv_ln_b, m_w_conv_out, m_sgu_ln_g, m_sgu_ln_b, m_w_sgu, m_b_sgu, m_w_sgu_out, m_w_o, m_g_final, v_w_ada, v_b_ada, v_g_pre, v_w_in, v_conv_w, v_conv_b, v_conv_ln_g, v_conv_ln_b, v_w_conv_out, v_sgu_ln_g, v_sgu_ln_b, v_w_sgu, v_b_sgu, v_w_sgu_out, v_w_o, v_g_final):
    me = _place()
    dev, chip = _dev_of(me), _chip_of(me)
    n_ada = w_ada.shape[2]
    conv_cols = conv_w.shape[2]

    b_ada_s = lax.dynamic_slice(b_ada, (0, chip * n_ada), (1, n_ada))
    c_all, mod_all, cw_all, (wg_in,) = _setup_comm(
        jnp.broadcast_to(c, (SUB, D)), w_ada[0], b_ada_s, jnp.pad(conv_w[0], ((0, HALO - CONV_K), (0, 0))),
        [w_in[0].astype(BF16)])
    mod = lax.dynamic_slice(mod_all, (0, dev * SUB, 0), (N_CHIP, 1, n_ada)).reshape(1, 3 * D)
    shift, scale, gate = mod[:, :D], mod[:, D:2 * D], mod[:, 2 * D:]
    conv_w_full = jnp.swapaxes(cw_all, 0, 1).reshape(HALO, D)[:CONV_K]

    loc = _local_step(x[0], loss_target[0], shift, scale, gate, g_pre, conv_w_full, conv_b, conv_ln_g, conv_ln_b,
                      sgu_ln_g, sgu_ln_b, w_sgu[0], b_sgu[0], g_final.reshape(1, D), wg_in,
                      [w_conv_out[0].astype(BF16), w_sgu_out[0].astype(BF16), w_o[0].astype(BF16)])

    d_mod = jnp.concatenate([loc["d_shift"], loc["d_scale"], loc["d_gate"]], axis=0)
    blob = jnp.zeros((BLOB_ROWS, D), F32)
    for i, name in enumerate(["g_pre", "conv_b", "conv_ln_g", "conv_ln_b", "sgu_ln_g", "sgu_ln_b", "g_final"]):
        blob = _set_rows(blob, BLOB_VEC + i, loc[name])
    blob = _set_rows(blob, BLOB_BSGU, loc["b_sgu"])
    blob = _set_rows(blob, BLOB_CONV, loc["conv_w"])
    blob = _set_rows(blob, BLOB_ADA, d_mod)
    blob = lax.dynamic_update_slice(blob, d_mod, (BLOB_DMOD + 3 * dev, 0))
    blob = _set_rows(blob, BLOB_LOSS, loc["loss_cols"])

    big = ["w_in", "w_conv_out", "w_sgu_out", "w_o"]
    contrib_out = [loc[name].reshape(N_CHIP, 2, D // (2 * N_CHIP), D) for name in big[1:]]
    gw_in, full_out, (tot, g_w_sgu) = _grad_matmul_reduce(
        loc["hb"], loc["dp"], "grad_w_in", contrib_out, [blob, loc["w_sgu"].reshape(HEADS * CHUNK, CHUNK)])
    full_in = _reduce_scatter([gw_in.reshape(N_CHIP, 2, D // 2, gw_in.shape[2])], "reduce_w_in")
    g_big = {name: f.reshape(2 * f.shape[1], f.shape[2]) for name, f in zip(big, list(full_in) + list(full_out))}

    loss = jnp.sum(tot[BLOB_LOSS])
    g_conv_s = lax.dynamic_slice(tot, (BLOB_CONV, chip * conv_cols), (CONV_K, conv_cols))
    d_mod_all = tot[BLOB_DMOD:BLOB_DMOD + 3 * N_DEV].reshape(N_DEV, 3 * D)

    ct = jnp.pad(c_all[::SUB].T, ((0, 0), (0, LANE - N_DEV))).astype(BF16)
    dm = jnp.pad(lax.dynamic_slice(d_mod_all, (0, chip * n_ada), (N_DEV, n_ada)), ((0, LANE - N_DEV), (0, 0))).astype(BF16)
    g_ada, d_ada, nm_ada, nv_ada = _adamw_ada(w_ada[0], ct, dm, m_w_ada[0], v_w_ada[0])

    upd = {}
    for name, w, m, v in [("w_in", w_in, m_w_in, v_w_in), ("w_conv_out", w_conv_out, m_w_conv_out, v_w_conv_out),
                          ("w_sgu_out", w_sgu_out, m_w_sgu_out, v_w_sgu_out), ("w_o", w_o, m_w_o, v_w_o)]:
        upd[name] = _adamw(w[0], g_big[name], m[0], v[0], "adamw_" + name)

    def wmv(w, m, v, shape):
        return tuple(a.reshape(shape) for a in (w, m, v))

    small_params = [wmv(w, m, v, (1, D)) for w, m, v in [
        (g_pre, m_g_pre, v_g_pre), (conv_b, m_conv_b, v_conv_b), (conv_ln_g, m_conv_ln_g, v_conv_ln_g),
        (conv_ln_b, m_conv_ln_b, v_conv_ln_b), (sgu_ln_g, m_sgu_ln_g, v_sgu_ln_g), (sgu_ln_b, m_sgu_ln_b, v_sgu_ln_b),
        (g_final, m_g_final, v_g_final)]]
    small_params += [wmv(b_sgu, m_b_sgu, v_b_sgu, (HEADS, CHUNK)), wmv(conv_w, m_conv_w, v_conv_w, (CONV_K, conv_cols)),
                     wmv(w_sgu, m_w_sgu, v_w_sgu, (HEADS * CHUNK, CHUNK)), wmv(b_ada, m_b_ada, v_b_ada, (3, D))]
    small_out = _adamw_small(tot, g_w_sgu, g_conv_s, small_params)

    def leaves(kind):
        vecs = [small_out[4 * i + kind] for i in range(N_VEC)]
        o_b_sgu, o_conv, o_w_sgu, o_b_ada = (small_out[4 * (N_VEC + i) + kind] for i in range(4))
        ada = (g_ada, d_ada, nm_ada, nv_ada)[kind]
        def bigk(name):
            return upd[name][kind][None]
        return [ada[None], o_b_ada.reshape(1, 3 * D), vecs[0], bigk("w_in"), o_conv[None], vecs[1], vecs[2], vecs[3],
                bigk("w_conv_out"), vecs[4], vecs[5], o_w_sgu.reshape(1, HEADS, CHUNK, CHUNK), o_b_sgu[None],
                bigk("w_sgu_out"), bigk("w_o"), vecs[6].reshape(D)]

    return (loss, loc["grad_x"][None], *leaves(0), *leaves(1), *leaves(2), *leaves(3))
```

```python
import functools

import jax
import jax.numpy as jnp
from jax import lax
from jax.experimental import pallas as pl
from jax.experimental.pallas import tpu as pltpu

F32 = jnp.float32
BF16 = jnp.bfloat16
MESH = pl.DeviceIdType.MESH

D = 1024
N_SEC = 8
N_CHIP = 4
N_DEV = 8
EPS = 1e-6
CONV_K = 31
HALO = 32
CHUNK = 128
HEADS = 8
LANE = 128
SUB = 8
PACK = 16
VMEM_LIMIT = 56 * 1024 * 1024

ADAM_LR, ADAM_B1, ADAM_B2, ADAM_EPS, ADAM_WD, ADAM_STEP = 0.001, 0.9, 0.999, 1e-08, 0.01, 10

_SQRT_HALF = 0.7071067811865476
_INV_SQRT_2PI = 0.3989422804014327


def _sds(shape, dtype):
    return jax.ShapeDtypeStruct(shape, dtype)


def _params(sem=None):
    if sem is None:
        return pltpu.CompilerParams(vmem_limit_bytes=VMEM_LIMIT)
    return pltpu.CompilerParams(dimension_semantics=sem, vmem_limit_bytes=VMEM_LIMIT)


def _strips(n_rows, rows, fn):
    def step(s, carry):
        fn(pl.multiple_of(s * rows, rows))
        return carry
    lax.fori_loop(0, n_rows // rows, step, 0)


def _sigmoid(v):
    return 1.0 / (1.0 + jnp.exp(-v))


def _gelu(v):
    return 0.5 * v * (1.0 + lax.erf(v * _SQRT_HALF))


def _gelu_and_grad(v):
    cdf = 0.5 * (1.0 + lax.erf(v * _SQRT_HALF))
    return v * cdf, cdf + v * jnp.exp(-0.5 * v * v) * _INV_SQRT_2PI


def _dsilu(v, sg):
    return sg * (1.0 + v * (1.0 - sg))


def _rowmean(v):
    return jnp.mean(v, axis=-1, keepdims=True)


def _vec_spec(grid_rank):
    zeros = (0, 0)
    if grid_rank == 1:
        return pl.BlockSpec((1, D), lambda i: zeros)
    return pl.BlockSpec((1, D), lambda i, j: zeros)


def _conv_taps(win_ref, r0, lt, weight_of_offset, rows):
    lanes = pl.ds(lt * LANE, LANE)
    win = win_ref[pl.ds(r0, rows + HALO), lanes]
    n_out = rows // SUB
    acc = [jnp.zeros((SUB, LANE), F32) for _ in range(n_out)]
    for phase in range(SUB):
        offs = [o for o in weight_of_offset if o % SUB == phase]
        if not offs:
            continue
        q_max = max(o // SUB for o in offs)
        span = (n_out + q_max) * SUB
        sh = win[phase:phase + span, :]
        for o in offs:
            q = o // SUB
            w = weight_of_offset[o](lanes)
            for m in range(n_out):
                acc[m] = acc[m] + w * sh[(m + q) * SUB:(m + q + 1) * SUB, :]
    return acc


def _branch_a_fwd(p, conv_wb, conv_b, ln_g, ln_b):
    s_len = p.shape[0]
    tm = min(256, s_len)
    n_i = s_len // tm
    rows = 32

    def body(p_ref, wb_ref, cb_ref, g_ref, b_ref, ya_ref, y1_ref, abuf):
        @pl.when(pl.program_id(0) == 0)
        def _():
            abuf[pl.ds(0, HALO), :] = jnp.zeros((HALO, D), F32)

        def glu(r0):
            val = p_ref[pl.ds(r0, PACK), pl.ds(0, D)].astype(F32)
            gl = p_ref[pl.ds(r0, PACK), pl.ds(D, D)].astype(F32)
            abuf[pl.ds(HALO + r0, PACK), :] = val * _sigmoid(gl)
        _strips(tm, PACK,glu)

        taps = {HALO - (CONV_K - 1) + k: (lambda lanes, k=k: wb_ref[pl.ds(k * SUB, SUB), lanes]) for k in range(CONV_K)}

        def conv(r0):
            for lt in range(D // LANE):
                acc = _conv_taps(abuf, r0, lt, taps, rows)
                cb = cb_ref[:, pl.ds(lt * LANE, LANE)]
                for m, v in enumerate(acc):
                    y1_ref[pl.ds(r0 + m * SUB, SUB), pl.ds(lt * LANE, LANE)] = v + cb
        _strips(tm, rows, conv)

        def norm(r0):
            y1 = y1_ref[pl.ds(r0, PACK), :]
            mu = _rowmean(y1)
            yc = y1 - mu
            rstd = lax.rsqrt(_rowmean(yc * yc) + EPS)
            l1 = (yc * rstd) * g_ref[...] + b_ref[...]
            z = p_ref[pl.ds(r0, PACK), pl.ds(2 * D, D)].astype(F32)
            ya_ref[pl.ds(r0, PACK), :] = ((l1 * _sigmoid(l1)) * (z * _sigmoid(z))).astype(BF16)
        _strips(tm, PACK,norm)

        abuf[pl.ds(0, HALO), :] = abuf[pl.ds(tm, HALO), :]

    return pl.pallas_call(
        body, name="branch_a_fwd", grid=(n_i,),
        in_specs=[pl.BlockSpec((tm, 3 * D), lambda i: (i, 0)),
                  pl.BlockSpec((CONV_K * SUB, D), lambda i: (0, 0)), _vec_spec(1), _vec_spec(1), _vec_spec(1)],
        out_specs=[pl.BlockSpec((tm, D), lambda i: (i, 0)), pl.BlockSpec((tm, D), lambda i: (i, 0))],
        out_shape=[_sds((s_len, D), BF16), _sds((s_len, D), F32)],
        scratch_shapes=[pltpu.VMEM((tm + HALO, D), F32)],
        compiler_params=_params(("arbitrary",)),
    )(p, conv_wb, conv_b, ln_g, ln_b)


def _branch_b_fwd(p, wt, bias_full, ln_g, ln_b):
    s_len = p.shape[0]
    tm = min(256, s_len)
    n_i = s_len // tm

    def body(p_ref, wt_ref, bias_ref, g_ref, b_ref, yb_ref, vb, sbuf):
        def norm(r0):
            gv = _gelu(p_ref[pl.ds(r0, PACK), pl.ds(D, D)].astype(F32))
            mu = _rowmean(gv)
            vc = gv - mu
            rstd = lax.rsqrt(_rowmean(vc * vc) + EPS)
            vb[pl.ds(r0, PACK), :] = ((vc * rstd) * g_ref[...] + b_ref[...]).astype(BF16)
        _strips(tm, PACK,norm)

        for ck in range(tm // CHUNK):
            for h in range(HEADS):
                blk = (pl.ds(ck * CHUNK, CHUNK), pl.ds(h * LANE, LANE))
                sbuf[blk] = jnp.dot(wt_ref[h], vb[blk], preferred_element_type=F32) + bias_ref[:, pl.ds(h * LANE, LANE)]

        def gate(r0):
            u = _gelu(p_ref[pl.ds(r0, PACK), pl.ds(0, D)].astype(F32))
            z = p_ref[pl.ds(r0, PACK), pl.ds(2 * D, D)].astype(F32)
            yb_ref[pl.ds(r0, PACK), :] = (u * sbuf[pl.ds(r0, PACK), :] * (z * _sigmoid(z))).astype(BF16)
        _strips(tm, PACK,gate)

    return pl.pallas_call(
        body, name="branch_b_fwd", grid=(n_i,),
        in_specs=[pl.BlockSpec((tm, 3 * D), lambda i: (i, 1)),
                  pl.BlockSpec((HEADS, CHUNK, CHUNK), lambda i: (0, 0, 0)),
                  pl.BlockSpec((CHUNK, D), lambda i: (0, 0)), _vec_spec(1), _vec_spec(1)],
        out_specs=pl.BlockSpec((tm, D), lambda i: (i, 0)),
        out_shape=_sds((s_len, D), BF16),
        scratch_shapes=[pltpu.VMEM((tm, D), BF16), pltpu.VMEM((tm, D), F32)],
        compiler_params=_params(("arbitrary",)),
    )(p, wt, bias_full, ln_g, ln_b)


def _dot_t(a, b):
    return lax.dot_general(a, b, (((1,), (1,)), ((), ())), preferred_element_type=F32)


def _out_proj(p, ya_in, yb_in, x, target, gate, g_final, w_co, w_so, w_o):
    s_len = x.shape[0]
    tm = min(256, s_len)
    n_i = s_len // tm

    def body(pg_ref, ya_ref, yb_ref, x_ref, t_ref, gate_ref, gf_ref, wco_ref, wso_ref, wo_ref,
             dx2_ref, dya_ref, dyb_ref, dp_ref, mb_ref, dob_ref, dyab_ref, dybb_ref, sums_ref):
        @pl.when(pl.program_id(0) == 0)
        def _():
            sums_ref[...] = jnp.zeros((SUB, D), F32)

        y_a = jnp.dot(ya_ref[...], wco_ref[...], preferred_element_type=F32)
        y_b = jnp.dot(yb_ref[...], wso_ref[...], preferred_element_type=F32)
        ga = _sigmoid(pg_ref[:, pl.ds(0, D)].astype(F32))
        gb = _sigmoid(pg_ref[:, pl.ds(D, D)].astype(F32))
        mb = (ga * y_a + gb * y_b).astype(BF16)
        mb_ref[...] = mb
        o = jnp.dot(mb, wo_ref[...], preferred_element_type=F32)
        x2 = x_ref[...] + gate_ref[...] * o
        r2 = lax.rsqrt(_rowmean(x2 * x2) + EPS)
        xh = x2 * r2
        e = xh * gf_ref[...] - t_ref[...]
        dy = e * (1.0 / D)
        dxh = dy * gf_ref[...]
        dx2 = r2 * (dxh - xh * _rowmean(dxh * xh))
        dx2_ref[...] = dx2
        sums_ref[pl.ds(0, 1), :] += jnp.sum(dy * xh, axis=0, keepdims=True)
        sums_ref[pl.ds(1, 1), :] += jnp.sum(dx2 * o, axis=0, keepdims=True)
        sums_ref[pl.ds(2, 1), :] += jnp.sum(e * e, axis=0, keepdims=True) * (0.5 / D)
        dob = (gate_ref[...] * dx2).astype(BF16)
        dob_ref[...] = dob
        dm = _dot_t(dob, wo_ref[...])
        dy_a = (ga * dm).astype(BF16)
        dy_b = (gb * dm).astype(BF16)
        dyab_ref[...] = dy_a
        dybb_ref[...] = dy_b
        dp_ref[:, pl.ds(0, D)] = (dm * y_a * ga * (1.0 - ga)).astype(BF16)
        dp_ref[:, pl.ds(D, D)] = (dm * y_b * gb * (1.0 - gb)).astype(BF16)
        dya_ref[...] = _dot_t(dy_a, wco_ref[...])
        dyb_ref[...] = _dot_t(dy_b, wso_ref[...])

    tile = pl.BlockSpec((tm, D), lambda i: (i, 0))
    wspec = pl.BlockSpec((D, D), lambda i: (0, 0))
    return pl.pallas_call(
        body, name="out_proj", grid=(n_i,),
        in_specs=[pl.BlockSpec((tm, 2 * D), lambda i: (i, 3)), tile, tile, tile, tile, _vec_spec(1), _vec_spec(1),
                  wspec, wspec, wspec],
        out_specs=[tile, tile, tile, pl.BlockSpec((tm, 2 * D), lambda i: (i, 3)), tile, tile, tile, tile,
                   pl.BlockSpec((SUB, D), lambda i: (0, 0))],
        out_shape=[_sds((s_len, D), F32), _sds((s_len, D), F32), _sds((s_len, D), F32), _sds((s_len, N_SEC * D), BF16),
                   _sds((s_len, D), BF16), _sds((s_len, D), BF16), _sds((s_len, D), BF16), _sds((s_len, D), BF16),
                   _sds((SUB, D), F32)],
        compiler_params=_params(("arbitrary",)),
    )(p, ya_in, yb_in, x, target, gate, g_final, w_co, w_so, w_o)


A_STATS_ROWS = 8 + HALO


def _branch_a_bwd(p, y1, dya_in, dp, conv_wb, ln_g, ln_b):
    s_len = p.shape[0]
    tm = min(256, s_len)
    n_i = s_len // tm
    rows = 32
    n_out = rows // SUB

    def tile_of(i):
        return n_i - 1 - i

    def body(p_ref, y1_ref, dya_ref, dp_in, wb_ref, g_ref, b_ref, dp_ref, st_ref, dybuf, acc8, tapacc):
        del dp_in
        i = pl.program_id(0)

        @pl.when(i == 0)
        def _():
            dybuf[pl.ds(tm, HALO), :] = jnp.zeros((HALO, D), F32)
            st_ref[...] = jnp.zeros((A_STATS_ROWS, D), F32)
            acc8[...] = jnp.zeros((3 * PACK, D), F32)
            tapacc[...] = jnp.zeros((CONV_K * SUB, D), F32)

        def norm_bwd(r0):
            y1 = y1_ref[pl.ds(r0, PACK), :]
            mu = _rowmean(y1)
            yc = y1 - mu
            rstd = lax.rsqrt(_rowmean(yc * yc) + EPS)
            n1 = yc * rstd
            l1 = n1 * g_ref[...] + b_ref[...]
            sg = _sigmoid(l1)
            z = p_ref[pl.ds(r0, PACK), pl.ds(2 * D, D)].astype(F32)
            sz = _sigmoid(z)
            dya = dya_ref[pl.ds(r0, PACK), :]
            dp_ref[pl.ds(r0, PACK), pl.ds(2 * D, D)] = (dya * (l1 * sg) * _dsilu(z, sz)).astype(BF16)
            dl1 = dya * (z * sz) * _dsilu(l1, sg)
            acc8[pl.ds(0, PACK), :] += dl1 * n1
            acc8[pl.ds(PACK, PACK), :] += dl1
            dn1 = dl1 * g_ref[...]
            dy1 = rstd * (dn1 - _rowmean(dn1) - n1 * _rowmean(dn1 * n1))
            acc8[pl.ds(2 * PACK, PACK), :] += dy1
            dybuf[pl.ds(r0, PACK), :] = dy1
        _strips(tm, PACK,norm_bwd)

        def conv_bwd(r0):
            for lt in range(D // LANE):
                lanes = pl.ds(lt * LANE, LANE)
                glanes = pl.ds(D + lt * LANE, LANE)
                win = dybuf[pl.ds(r0, rows + HALO), lanes]
                sg16, a16 = [], []
                for h in range(rows // PACK):
                    rr = pl.ds(r0 + h * PACK, PACK)
                    s = _sigmoid(p_ref[rr, glanes].astype(F32))
                    sg16.append(s)
                    a16.append(p_ref[rr, lanes].astype(F32) * s)
                a = [a16[m // 2][(m % 2) * SUB:(m % 2 + 1) * SUB, :] for m in range(n_out)]
                da = [jnp.zeros((SUB, LANE), F32) for _ in range(n_out)]
                for phase in range(SUB):
                    offs = [o for o in range(CONV_K) if o % SUB == phase]
                    q_max = max(o // SUB for o in offs)
                    sh = win[phase:phase + (n_out + q_max) * SUB, :]
                    for o in offs:
                        k, q = CONV_K - 1 - o, o // SUB
                        w = wb_ref[pl.ds(k * SUB, SUB), lanes]
                        part = None
                        for m in range(n_out):
                            s = sh[(m + q) * SUB:(m + q + 1) * SUB, :]
                            da[m] = da[m] + w * s
                            part = a[m] * s if part is None else part + a[m] * s
                        tapacc[pl.ds(k * SUB, SUB), lanes] += part
                for h in range(rows // PACK):
                    rr = pl.ds(r0 + h * PACK, PACK)
                    da16 = jnp.concatenate(da[2 * h:2 * h + 2], axis=0)
                    dp_ref[rr, lanes] = (da16 * sg16[h]).astype(BF16)
                    dp_ref[rr, glanes] = (da16 * a16[h] * (1.0 - sg16[h])).astype(BF16)
        _strips(tm, rows, conv_bwd)

        dybuf[pl.ds(tm, HALO), :] = dybuf[pl.ds(0, HALO), :]

        @pl.when(i == n_i - 1)
        def _():
            for j in range(3):
                st_ref[pl.ds(j, 1), :] = jnp.sum(acc8[pl.ds(j * PACK, PACK), :], axis=0, keepdims=True)
            for k in range(CONV_K):
                st_ref[pl.ds(SUB + k, 1), :] = jnp.sum(tapacc[pl.ds(k * SUB, SUB), :], axis=0, keepdims=True)

    return pl.pallas_call(
        body, name="branch_a_bwd", grid=(n_i,),
        in_specs=[pl.BlockSpec((tm, 3 * D), lambda i: (tile_of(i), 0)),
                  pl.BlockSpec((tm, D), lambda i: (tile_of(i), 0)),
                  pl.BlockSpec((tm, D), lambda i: (tile_of(i), 0)),
                  pl.BlockSpec(memory_space=pl.ANY),
                  pl.BlockSpec((CONV_K * SUB, D), lambda i: (0, 0)), _vec_spec(1), _vec_spec(1)],
        out_specs=[pl.BlockSpec((tm, 3 * D), lambda i: (tile_of(i), 0)),
                   pl.BlockSpec((A_STATS_ROWS, D), lambda i: (0, 0))],
        out_shape=[_sds(dp.shape, BF16), _sds((A_STATS_ROWS, D), F32)],
        scratch_shapes=[pltpu.VMEM((tm + HALO, D), F32), pltpu.VMEM((3 * PACK, D), F32), pltpu.VMEM((CONV_K * SUB, D), F32)],
        input_output_aliases={3: 0},
        compiler_params=_params(("arbitrary",)),
    )(p, y1, dya_in, dp, conv_wb, ln_g, ln_b)


def _branch_b_bwd(p, dyb_in, dp, wt, wtt, bias_full, ln_g, ln_b):
    s_len = p.shape[0]
    tm = min(256, s_len)
    n_i = s_len // tm

    def body(p_ref, dyb_ref, dp_in, wt_ref, wtt_ref, bias_ref, g_ref, b_ref, dp_ref, st_ref, gbt_ref, gw_ref,
             vb, n2buf, rstdbuf, sbuf, dsb, dvbuf, acc8, gb_ref, dgbuf):
        del dp_in
        i = pl.program_id(0)

        @pl.when(i == 0)
        def _():
            st_ref[...] = jnp.zeros((SUB, D), F32)
            gbt_ref[...] = jnp.zeros((CHUNK, LANE), F32)
            gb_ref[...] = jnp.zeros((CHUNK, D), F32)
            gw_ref[...] = jnp.zeros((HEADS, CHUNK, CHUNK), F32)
            acc8[...] = jnp.zeros((2 * PACK, D), F32)

        def norm(r0):
            gv, dgv = _gelu_and_grad(p_ref[pl.ds(r0, PACK), pl.ds(D, D)].astype(F32))
            dgbuf[pl.ds(r0, PACK), :] = dgv
            mu = _rowmean(gv)
            vc = gv - mu
            rstd = lax.rsqrt(_rowmean(vc * vc) + EPS)
            n2 = vc * rstd
            n2buf[pl.ds(r0, PACK), :] = n2
            rstdbuf[pl.ds(r0, PACK), :] = jnp.broadcast_to(rstd, (PACK, LANE))
            vb[pl.ds(r0, PACK), :] = (n2 * g_ref[...] + b_ref[...]).astype(BF16)
        _strips(tm, PACK,norm)

        for ck in range(tm // CHUNK):
            for h in range(HEADS):
                blk = (pl.ds(ck * CHUNK, CHUNK), pl.ds(h * LANE, LANE))
                sbuf[blk] = jnp.dot(wt_ref[h], vb[blk], preferred_element_type=F32) + bias_ref[:, pl.ds(h * LANE, LANE)]

        def gate_bwd(r0):
            pu = p_ref[pl.ds(r0, PACK), pl.ds(0, D)].astype(F32)
            u, du = _gelu_and_grad(pu)
            z = p_ref[pl.ds(r0, PACK), pl.ds(2 * D, D)].astype(F32)
            sg = _sigmoid(z)
            sz = z * sg
            s = sbuf[pl.ds(r0, PACK), :]
            dyb = dyb_ref[pl.ds(r0, PACK), :]
            ds = dyb * u * sz
            dsb[pl.ds(r0, PACK), :] = ds.astype(BF16)
            gb_ref[pl.ds(pl.multiple_of(r0 % CHUNK, PACK), PACK), :] += ds
            dp_ref[pl.ds(r0, PACK), pl.ds(0, D)] = (dyb * s * sz * du).astype(BF16)
            dp_ref[pl.ds(r0, PACK), pl.ds(2 * D, D)] = (dyb * u * s * _dsilu(z, sg)).astype(BF16)
        _strips(tm, PACK,gate_bwd)

        for ck in range(tm // CHUNK):
            for h in range(HEADS):
                blk = (pl.ds(ck * CHUNK, CHUNK), pl.ds(h * LANE, LANE))
                d_s = dsb[blk]
                dvbuf[blk] = jnp.dot(wtt_ref[h], d_s, preferred_element_type=F32)
                gw_ref[h] += _dot_t(d_s, vb[blk])

        def norm_bwd(r0):
            dv = dvbuf[pl.ds(r0, PACK), :]
            n2 = n2buf[pl.ds(r0, PACK), :]
            rstd = rstdbuf[pl.ds(r0, PACK), pl.ds(0, 1)]
            acc8[pl.ds(0, PACK), :] += dv * n2
            acc8[pl.ds(PACK, PACK), :] += dv
            dn2 = dv * g_ref[...]
            dgv = rstd * (dn2 - _rowmean(dn2) - n2 * _rowmean(dn2 * n2))
            dp_ref[pl.ds(r0, PACK), pl.ds(D, D)] = (dgv * dgbuf[pl.ds(r0, PACK), :]).astype(BF16)
        _strips(tm, PACK,norm_bwd)

        @pl.when(i == n_i - 1)
        def _():
            for j in range(2):
                st_ref[pl.ds(j, 1), :] = jnp.sum(acc8[pl.ds(j * PACK, PACK), :], axis=0, keepdims=True)
            row = lax.broadcasted_iota(jnp.int32, (CHUNK, CHUNK), 0)
            col = lax.broadcasted_iota(jnp.int32, (CHUNK, CHUNK), 1)
            for h in range(HEADS):
                gw_ref[h] = jnp.where(row >= col, gw_ref[h], 0.0)
            lane = lax.broadcasted_iota(jnp.int32, (CHUNK, LANE), 1)
            gbt = jnp.zeros((CHUNK, LANE), F32)
            for h in range(HEADS):
                gbt = jnp.where(lane == h, jnp.sum(gb_ref[:, pl.ds(h * LANE, LANE)], axis=1, keepdims=True), gbt)
            gbt_ref[...] = gbt

    wspec = pl.BlockSpec((HEADS, CHUNK, CHUNK), lambda i: (0, 0, 0))
    return pl.pallas_call(
        body, name="branch_b_bwd", grid=(n_i,),
        in_specs=[pl.BlockSpec((tm, 3 * D), lambda i: (i, 1)), pl.BlockSpec((tm, D), lambda i: (i, 0)),
                  pl.BlockSpec(memory_space=pl.ANY), wspec, wspec,
                  pl.BlockSpec((CHUNK, D), lambda i: (0, 0)), _vec_spec(1), _vec_spec(1)],
        out_specs=[pl.BlockSpec((tm, 3 * D), lambda i: (i, 1)), pl.BlockSpec((SUB, D), lambda i: (0, 0)),
                   pl.BlockSpec((CHUNK, LANE), lambda i: (0, 0)), wspec],
        out_shape=[_sds(dp.shape, BF16), _sds((SUB, D), F32), _sds((CHUNK, LANE), F32), _sds((HEADS, CHUNK, CHUNK), F32)],
        scratch_shapes=[pltpu.VMEM((tm, D), BF16), pltpu.VMEM((tm, D), F32), pltpu.VMEM((tm, LANE), F32),
                        pltpu.VMEM((tm, D), F32), pltpu.VMEM((tm, D), BF16), pltpu.VMEM((tm, D), F32),
                        pltpu.VMEM((2 * PACK, D), F32), pltpu.VMEM((CHUNK, D), F32), pltpu.VMEM((tm, D), F32)],
        input_output_aliases={2: 0},
        compiler_params=_params(("arbitrary",)),
    )(p, dyb_in, dp, wt, wtt, bias_full, ln_g, ln_b)


def _in_proj_bwd(dp, wg_in, x, dx2, shift, scale, g_pre):
    del shift
    s_len = x.shape[0]
    tm = min(512, s_len)
    n_i = s_len // tm
    wn = wg_in.shape[2]

    def body(dp0, dp1, dp2, dp3, w_ref, x_ref, dx2_ref, sc_ref, g_ref, gx_ref, st_ref, acc, acc8):
        i = pl.program_id(0)

        @pl.when(i == 0)
        def _():
            st_ref[...] = jnp.zeros((SUB, D), F32)
            acc8[...] = jnp.zeros((3 * PACK, D), F32)

        dh = _dot_t(dp0[...], w_ref[0])
        for j, dp_ref in enumerate((dp1, dp2, dp3), start=1):
            dh = dh + _dot_t(dp_ref[...], w_ref[j])
        acc[...] = dh

        def strip(r0):
            xs = x_ref[pl.ds(r0, PACK), :]
            r = lax.rsqrt(_rowmean(xs * xs) + EPS)
            xn = xs * r
            dhs = acc[pl.ds(r0, PACK), :]
            acc8[pl.ds(0, PACK), :] += dhs
            acc8[pl.ds(PACK, PACK), :] += dhs * (xn * g_ref[...])
            dhp = dhs * (1.0 + sc_ref[...])
            acc8[pl.ds(2 * PACK, PACK), :] += dhp * xn
            dxn = dhp * g_ref[...]
            gx_ref[pl.ds(r0, PACK), :] = dx2_ref[pl.ds(r0, PACK), :] + r * (dxn - xn * _rowmean(dxn * xn))
        _strips(tm, PACK, strip)

        @pl.when(i == n_i - 1)
        def _():
            for k in range(3):
                st_ref[pl.ds(k, 1), :] = jnp.sum(acc8[pl.ds(k * PACK, PACK), :], axis=0, keepdims=True)

    tile = pl.BlockSpec((tm, D), lambda i: (i, 0))
    return pl.pallas_call(
        body, name="in_proj_bwd", grid=(n_i,),
        in_specs=[pl.BlockSpec((tm, wn), functools.partial(lambda j, i: (i, j), j)) for j in range(N_CHIP)] + [
                  pl.BlockSpec((N_CHIP, D, wn), lambda i: (0, 0, 0), pipeline_mode=pl.Buffered(1)),
                  tile, tile, _vec_spec(1), _vec_spec(1)],
        out_specs=[tile, pl.BlockSpec((SUB, D), lambda i: (0, 0))],
        out_shape=[_sds((s_len, D), F32), _sds((SUB, D), F32)],
        scratch_shapes=[pltpu.VMEM((tm, D), F32), pltpu.VMEM((3 * PACK, D), F32)],
        compiler_params=_params(("arbitrary",)),
    )(dp, dp, dp, dp, wg_in, x, dx2, scale, g_pre)


def _grad_matmul(a, b, name):
    s_len, n = b.shape
    cb = min(2 * D, n)
    tn = 512
    per = cb // tn

    def body(a_ref, b_ref, ob_ref):
        ob_ref[0] = lax.dot_general(a_ref[...], b_ref[...], (((0,), (0,)), ((), ())),
                                    preferred_element_type=F32).astype(BF16)

    return pl.pallas_call(
        body, name=name, grid=(n // tn,),
        in_specs=[pl.BlockSpec((s_len, D), lambda j: (0, 0), pipeline_mode=pl.Buffered(1)),
                  pl.BlockSpec((s_len, tn), lambda j: (0, j))],
        out_specs=pl.BlockSpec((1, D, tn), lambda j: (j // per, 0, j % per)),
        out_shape=_sds((n // cb, D, cb), BF16),
        compiler_params=_params(("arbitrary",)),
    )(a, b)


def _local_step(x, target, shift, scale, gate, g_pre, conv_w_full, conv_b, conv_ln_g, conv_ln_b,
                sgu_ln_g, sgu_ln_b, w_sgu, b_sgu, g_final, wg_in, out_shards):
    conv_wb = jnp.repeat(conv_w_full, SUB, axis=0)
    causal = jnp.tril(jnp.ones((CHUNK, CHUNK), dtype=bool))
    wt = jnp.where(causal[None], w_sgu, 0.0).astype(BF16)
    wtt = jnp.swapaxes(wt, 1, 2)
    bias_full = jnp.repeat(b_sgu.T, LANE, axis=1)

    p, hb, gathered = _in_proj_gather(x, shift, scale, g_pre, wg_in, out_shards)
    w_co, w_so, w_o = (g.reshape(D, D) for g in gathered)
    ya_in, y1 = _branch_a_fwd(p, conv_wb, conv_b, conv_ln_g, conv_ln_b)
    yb_in = _branch_b_fwd(p, wt, bias_full, sgu_ln_g, sgu_ln_b)
    dx2, dya_in, dyb_in, dp, mb, dob, dyab, dybb, sums_o = _out_proj(
        p, ya_in, yb_in, x, target, gate, g_final, w_co, w_so, w_o)
    dp, st_a = _branch_a_bwd(p, y1, dya_in, dp, conv_wb, conv_ln_g, conv_ln_b)
    dp, st_b, gbt, gws = _branch_b_bwd(p, dyb_in, dp, wt, wtt, bias_full, sgu_ln_g, sgu_ln_b)
    grad_x, st_i = _in_proj_bwd(dp, wg_in, x, dx2, shift, scale, g_pre)
    gw_o = _grad_matmul(mb, dob, "grad_w_o")
    gw_co = _grad_matmul(ya_in, dyab, "grad_w_conv_out")
    gw_so = _grad_matmul(yb_in, dybb, "grad_w_sgu_out")
    return dict(
        grad_x=grad_x, loss_cols=sums_o[2:3], g_final=sums_o[0:1], d_gate=sums_o[1:2],
        d_shift=st_i[0:1], d_scale=st_i[1:2], g_pre=st_i[2:3],
        conv_ln_g=st_a[0:1], conv_ln_b=st_a[1:2], conv_b=st_a[2:3], conv_w=st_a[SUB:SUB + CONV_K],
        sgu_ln_g=st_b[0:1], sgu_ln_b=st_b[1:2], b_sgu=gbt[:, :HEADS].T, w_sgu=gws,
        hb=hb, dp=dp, w_o=gw_o, w_conv_out=gw_co, w_sgu_out=gw_so)


ANY_SPEC = pl.BlockSpec(memory_space=pl.ANY)
VMEM_SPEC = pl.BlockSpec(memory_space=pltpu.VMEM)


def _place():
    return lax.axis_index("x"), lax.axis_index("y"), lax.axis_index("c")


def _peer(k):
    x, y, c = _place()
    return (1 - x if k & 4 else x, 1 - y if k & 2 else y, 1 - c if k & 1 else c)


def _dev_of(p):
    return 4 * p[0] + 2 * p[1] + p[2]


def _chip_of(p):
    return 2 * p[0] + p[1]


def _rdma(src, dst, send_sem, recv_sem, to):
    return pltpu.make_async_remote_copy(src_ref=src, dst_ref=dst, send_sem=send_sem, recv_sem=recv_sem,
                                        device_id=to, device_id_type=MESH)


CHIP_PEERS = (2, 4, 6)
ALL_PEERS = tuple(range(1, N_DEV))
SIBLING = 1


def _setup_comm(c8, w_ada_s, b_ada_s, convw_s, shards):
    n_mod = w_ada_s.shape[1]
    rows = SUB * N_DEV
    n = len(shards)

    def body(c8_ref, wada_ref, bada_ref, cw_ref, *refs):
        ins, (call_ref, mod_ref, cwall_ref), outs = refs[:n], refs[n:n + 3], refs[n + 3:2 * n + 3]
        csend, crecv, wsend, wrecv, msend, mrecv = refs[2 * n + 3:2 * n + 9]
        gather_a, gather_b, gather_c = _gather_phases([s.shape[0] for s in shards], ins, outs, refs[2 * n + 9:])
        me = _place()
        dev, chip = _dev_of(me), _chip_of(me)

        def c_rows(d):
            return call_ref.at[pl.ds(pl.multiple_of(d * SUB, SUB), SUB), :]

        call_ref[pl.ds(pl.multiple_of(dev * SUB, SUB), SUB), :] = c8_ref[...]
        cwall_ref[chip] = cw_ref[...]
        c_out = [_rdma(c8_ref, c_rows(dev), csend.at[k], crecv.at[k], _peer(k)) for k in ALL_PEERS]
        w_out = [_rdma(cw_ref, cwall_ref.at[chip], wsend.at[k], wrecv.at[k], _peer(k)) for k in CHIP_PEERS]
        for cp in c_out + w_out:
            cp.start()
        for k in ALL_PEERS:
            _rdma(c8_ref, c_rows(_dev_of(_peer(k))), csend.at[k], crecv.at[k], _peer(k)).wait_recv()
        part = jnp.dot(call_ref[...].astype(BF16), wada_ref[...].astype(BF16), preferred_element_type=F32) + bada_ref[...]
        mod_ref[chip] = part
        m_out = [_rdma(mod_ref.at[chip], mod_ref.at[chip], msend.at[k], mrecv.at[k], _peer(k)) for k in CHIP_PEERS]
        for cp in m_out:
            cp.start()
        gather_a()
        for k in CHIP_PEERS:
            pc = _chip_of(_peer(k))
            _rdma(cw_ref, cwall_ref.at[pc], wsend.at[k], wrecv.at[k], _peer(k)).wait_recv()
            _rdma(mod_ref.at[pc], mod_ref.at[pc], msend.at[k], mrecv.at[k], _peer(k)).wait_recv()
        for cp in c_out + w_out + m_out:
            cp.wait_send()
        gather_b()
        gather_c()

    res = pl.pallas_call(
        body, name="setup_comm",
        in_specs=[VMEM_SPEC] * (4 + n), out_specs=[VMEM_SPEC] * (3 + n),
        out_shape=([_sds((rows, D), F32), _sds((N_CHIP, rows, n_mod), F32), _sds((N_CHIP,) + convw_s.shape, F32)]
                   + [_sds((N_CHIP,) + s.shape, s.dtype) for s in shards]),
        scratch_shapes=([pltpu.SemaphoreType.DMA((N_DEV,))] * 6
                        + [pltpu.SemaphoreType.DMA((n,))] + [pltpu.SemaphoreType.DMA((n, len(CHIP_PEERS)))] * 4),
        compiler_params=_params(),
    )(c8, w_ada_s, b_ada_s, convw_s, *shards)
    return res[0], res[1], res[2], res[3:]


def _gather_phases(row_counts, ins, dsts, sems):
    n = len(row_counts)
    lsem, isend, irecv, dsend, drecv = sems
    me = _place()
    chip, c = _chip_of(me), me[2]

    def half(t, which):
        hr = row_counts[t] // 2
        return pl.ds(pl.multiple_of(which * hr, hr), hr)

    def local(t):
        return pltpu.make_async_copy(ins[t], dsts[t].at[chip], lsem.at[t])

    def to_chip(t, j):
        return _rdma(ins[t].at[half(t, c)], dsts[t].at[chip, half(t, c)], isend.at[t, j], irecv.at[t, j], _peer(CHIP_PEERS[j]))

    def landed(t, j, which):
        return dsts[t].at[_chip_of(_peer(CHIP_PEERS[j])), half(t, which)]

    def to_sibling(t, j):
        return _rdma(landed(t, j, c), landed(t, j, c), dsend.at[t, j], drecv.at[t, j], _peer(SIBLING))

    pairs = [(t, j) for t in range(n) for j in range(len(CHIP_PEERS))]

    def phase_a():
        for t in range(n):
            local(t).start()
        for t, j in pairs:
            to_chip(t, j).start()

    def phase_b():
        for t, j in pairs:
            _rdma(landed(t, j, c), landed(t, j, c), isend.at[t, j], irecv.at[t, j], _peer(CHIP_PEERS[j])).wait_recv()
            to_sibling(t, j).start()

    def phase_c():
        for t, j in pairs:
            _rdma(landed(t, j, 1 - c), landed(t, j, 1 - c), dsend.at[t, j], drecv.at[t, j], _peer(SIBLING)).wait_recv()
        for t, j in pairs:
            to_chip(t, j).wait_send()
            to_sibling(t, j).wait_send()
        for t in range(n):
            local(t).wait()

    return phase_a, phase_b, phase_c


def _in_proj_gather(x, shift, scale, g_pre, wg_in, shards):
    s_len = x.shape[0]
    tm = min(256, s_len)
    n_i = s_len // tm
    wn = wg_in.shape[2]
    n = len(shards)

    def body(x_ref, sh_ref, sc_ref, g_ref, w_ref, *refs):
        ins, p_ref, hb_ref, outs = refs[:n], refs[n], refs[n + 1], refs[n + 2:2 * n + 2]
        gath, sems = refs[2 * n + 2:3 * n + 2], refs[3 * n + 2:]
        phases = _gather_phases([s.shape[0] for s in shards], ins, gath, sems)
        i = pl.program_id(0)
        for step, phase in zip((0, n_i // 2, n_i - 1), phases):
            pl.when(i == step)(phase)

        @pl.when(i == n_i - 1)
        def _():
            for t in range(n):
                outs[t][...] = gath[t][...]

        def strip(r0):
            xs = x_ref[pl.ds(r0, PACK), :]
            r = lax.rsqrt(_rowmean(xs * xs) + EPS)
            h = (xs * r) * g_ref[...] * (1.0 + sc_ref[...]) + sh_ref[...]
            hb_ref[pl.ds(r0, PACK), :] = h.astype(BF16)
        _strips(tm, PACK, strip)
        hb = hb_ref[...]
        for j in range(N_CHIP):
            p_ref[:, pl.ds(j * wn, wn)] = jnp.dot(hb, w_ref[j], preferred_element_type=F32).astype(BF16)

    res = pl.pallas_call(
        body, name="in_proj", grid=(n_i,),
        in_specs=[pl.BlockSpec((tm, D), lambda i: (i, 0)), _vec_spec(1), _vec_spec(1), _vec_spec(1),
                  pl.BlockSpec((N_CHIP, D, wn), lambda i: (0, 0, 0), pipeline_mode=pl.Buffered(1))] + [VMEM_SPEC] * n,
        out_specs=[pl.BlockSpec((tm, N_CHIP * wn), lambda i: (i, 0)), pl.BlockSpec((tm, D), lambda i: (i, 0))] + [VMEM_SPEC] * n,
        out_shape=([_sds((s_len, N_SEC * D), BF16), _sds((s_len, D), BF16)]
                   + [_sds((N_CHIP,) + s.shape, s.dtype) for s in shards]),
        scratch_shapes=([pltpu.VMEM((N_CHIP,) + s.shape, s.dtype) for s in shards]
                        + [pltpu.SemaphoreType.DMA((n,))] + [pltpu.SemaphoreType.DMA((n, len(CHIP_PEERS)))] * 4),
        compiler_params=_params(("arbitrary",)),
    )(x, shift, scale, g_pre, wg_in, *shards)
    return res[0], res[1], res[2:]


def _reduce_scatter(grads, name):
    n = len(grads)
    shapes = [g.shape[2:] for g in grads]
    parts = 4
    part_shapes = [(r, cols // parts) for r, cols in shapes for _ in range(parts)]

    def body(*refs):
        def halves(group, lead):
            return [ref.at[(slice(None),) * lead + (pl.ds(k * (s[1] // parts), s[1] // parts),)]
                    for ref, s in zip(group, shapes) for k in range(parts)]
        ins, outs = halves(refs[:n], 3), halves(refs[n:2 * n], 2)
        pbufs, rbufs, accs = halves(refs[2 * n:3 * n], 2), halves(refs[3 * n:4 * n], 2), halves(refs[4 * n:5 * n], 1)
        for phase in _reduce_phases(part_shapes, ins, outs, pbufs, rbufs, accs, refs[5 * n:]):
            phase()

    return pl.pallas_call(
        body, name=name,
        in_specs=[VMEM_SPEC] * n, out_specs=[VMEM_SPEC] * n,
        out_shape=[_sds((2,) + s, F32) for s in shapes],
        scratch_shapes=_reduce_scratch(shapes)[:3 * n] + _reduce_scratch(part_shapes)[3 * parts * n:],
        compiler_params=_params(),
    )(*grads)


def _reduce_phases(shapes, ins, outs, pbufs, rbufs, accs, sems):
    n = len(shapes)
    psend, precv, csend, crecv, fsend, frecv = sems
    me = _place()
    chip, c = _chip_of(me), me[2]
    sib = _peer(SIBLING)

    def to_sibling(t, d):
        return _rdma(ins[t].at[d, 1 - c], pbufs[t].at[d], psend.at[t, d], precv.at[t, d], sib)

    def to_chip(t, j):
        return _rdma(pbufs[t].at[jnp.bitwise_xor(chip, j)], rbufs[t].at[j - 1], csend.at[t, j], crecv.at[t, j], _peer(2 * j))

    def finished(t):
        return _rdma(outs[t].at[c], outs[t].at[c], fsend.at[t], frecv.at[t], sib)

    def phase_a():
        for j in (1, 2, 3, 0):
            for t in range(n):
                to_sibling(t, jnp.bitwise_xor(chip, j)).start()

    def phase_b():
        for j in (1, 2, 3, 0):
            d = jnp.bitwise_xor(chip, j)
            for t in range(n):
                to_sibling(t, d).wait_recv()

                def pair_sum(r0, t=t, d=d, j=j):
                    rows = pl.ds(r0, PACK)
                    s = ins[t][d, c, rows, :].astype(F32) + pbufs[t][d, rows, :].astype(F32)
                    if j == 0:
                        accs[t][rows, :] = s
                    else:
                        pbufs[t][d, rows, :] = s.astype(BF16)
                _strips(shapes[t][0], PACK, pair_sum)
                if j:
                    to_chip(t, j).start()

    def phase_c():
        for t in range(n):
            for j in (1, 2, 3):
                blk = rbufs[t].at[j - 1]
                _rdma(blk, blk, csend.at[t, j], crecv.at[t, j], _peer(2 * j)).wait_recv()

            def total(r0, t=t):
                rows = pl.ds(r0, PACK)
                s = accs[t][rows, :] + rbufs[t][0, rows, :].astype(F32)
                s = s + rbufs[t][1, rows, :].astype(F32)
                outs[t][c, rows, :] = s + rbufs[t][2, rows, :].astype(F32)
            _strips(shapes[t][0], PACK, total)
            finished(t).start()

    def phase_d():
        for t in range(n):
            blk = outs[t].at[1 - c]
            _rdma(blk, blk, fsend.at[t], frecv.at[t], sib).wait_recv()
        for t in range(n):
            for d in range(N_CHIP):
                to_sibling(t, d).wait_send()
            for j in (1, 2, 3):
                to_chip(t, j).wait_send()
            finished(t).wait_send()

    return phase_a, phase_b, phase_c, phase_d


def _sum_small_phases(ins, outs, pbufs, buf4s, sems):
    n = len(ins)
    psend, precv, send, recv = sems
    chip = _chip_of(_place())

    def swap(t):
        return _rdma(ins[t], pbufs[t], psend.at[t], precv.at[t], _peer(SIBLING))

    def to_chip(t, k):
        return _rdma(buf4s[t].at[chip], buf4s[t].at[chip], send.at[t, k], recv.at[t, k], _peer(k))

    def phase_a():
        for t in range(n):
            swap(t).start()

    def phase_b():
        for t in range(n):
            swap(t).wait()
            buf4s[t][chip] = ins[t][...] + pbufs[t][...]
            for k in CHIP_PEERS:
                to_chip(t, k).start()

    def phase_c():
        for t in range(n):
            for k in CHIP_PEERS:
                blk = buf4s[t].at[_chip_of(_peer(k))]
                _rdma(blk, blk, send.at[t, k], recv.at[t, k], _peer(k)).wait_recv()
            outs[t][...] = (buf4s[t][0] + buf4s[t][1]) + (buf4s[t][2] + buf4s[t][3])

    def phase_d():
        for t in range(n):
            for k in CHIP_PEERS:
                to_chip(t, k).wait_send()

    return phase_a, phase_b, phase_c, phase_d


def _sum_small_scratch(blobs):
    n = len(blobs)
    return ([pltpu.VMEM(b.shape, F32) for b in blobs] + [pltpu.VMEM((N_CHIP,) + b.shape, F32) for b in blobs]
            + [pltpu.SemaphoreType.DMA((n,))] * 2 + [pltpu.SemaphoreType.DMA((n, N_DEV))] * 2)


def _reduce_scratch(shapes):
    n = len(shapes)
    return ([pltpu.VMEM((N_CHIP,) + s, BF16) for s in shapes] + [pltpu.VMEM((N_CHIP - 1,) + s, BF16) for s in shapes]
            + [pltpu.VMEM(s, F32) for s in shapes]
            + [pltpu.SemaphoreType.DMA((n, N_CHIP))] * 4 + [pltpu.SemaphoreType.DMA((n,))] * 2)


def _grad_matmul_reduce(a, b, name, grads, blobs):
    s_len, n_cols = b.shape
    cb = min(2 * D, n_cols)
    tn = 512
    per = cb // tn
    steps = n_cols // tn
    n, nb = len(grads), len(blobs)
    shapes = [g.shape[2:] for g in grads]
    n_red = len(_reduce_scratch(shapes))

    def body(a_ref, b_ref, *refs):
        ins, bins = refs[:n], refs[n:n + nb]
        ob_ref, outs, bouts = refs[n + nb], refs[n + nb + 1:2 * n + nb + 1], refs[2 * n + nb + 1:2 * (n + nb) + 1]
        scratch = refs[2 * (n + nb) + 1:]
        fulls, red, small = scratch[:n], scratch[n:n + n_red], scratch[n + n_red:]
        phases = _reduce_phases(shapes, ins, fulls, red[:n], red[n:2 * n], red[2 * n:3 * n], red[3 * n:])
        small_phases = _sum_small_phases(bins, bouts, small[:nb], small[nb:2 * nb], small[2 * nb:])
        j = pl.program_id(0)
        for step, phase in zip((0, 2, steps - 2, steps - 1), phases):
            pl.when(j == step)(phase)
        for step, phase in zip((1, 3, steps - 2, steps - 1), small_phases):
            pl.when(j == step)(phase)

        @pl.when(j == steps - 1)
        def _():
            for t in range(n):
                outs[t][...] = fulls[t][...]
        ob_ref[0] = lax.dot_general(a_ref[...], b_ref[...], (((0,), (0,)), ((), ())),
                                    preferred_element_type=F32).astype(BF16)

    res = pl.pallas_call(
        body, name=name, grid=(steps,),
        in_specs=[pl.BlockSpec((s_len, D), lambda j: (0, 0), pipeline_mode=pl.Buffered(1)),
                  pl.BlockSpec((s_len, tn), lambda j: (0, j))] + [VMEM_SPEC] * (n + nb),
        out_specs=[pl.BlockSpec((1, D, tn), lambda j: (j // per, 0, j % per))] + [VMEM_SPEC] * (n + nb),
        out_shape=([_sds((n_cols // cb, D, cb), BF16)] + [_sds((2,) + s, F32) for s in shapes]
                   + [_sds(bl.shape, F32) for bl in blobs]),
        scratch_shapes=[pltpu.VMEM((2,) + s, F32) for s in shapes] + _reduce_scratch(shapes) + _sum_small_scratch(blobs),
        compiler_params=_params(("arbitrary",)),
    )(a, b, *grads, *blobs)
    return res[0], res[1:1 + n], res[1 + n:]


def _adamw_math(w, g, m, v):
    m = ADAM_B1 * m + (1.0 - ADAM_B1) * g
    v = ADAM_B2 * v + (1.0 - ADAM_B2) * (g * g)
    m_hat = m / (1.0 - ADAM_B1 ** ADAM_STEP)
    v_hat = v / (1.0 - ADAM_B2 ** ADAM_STEP)
    delta = -ADAM_LR * (m_hat / (jnp.sqrt(v_hat) + ADAM_EPS) + ADAM_WD * w)
    return delta, m, v


def _row_tile(r, cols):
    if r * cols * 4 <= 2 ** 20:
        return r
    return next(t for t in (512, 256, 128, 64, 32, 16, 8) if r % t == 0 and t * cols * 4 <= 2 ** 20)


def _adamw(w, g, m, v, name):
    r, cols = w.shape
    tr = _row_tile(r, cols)

    def body(w_ref, g_ref, m_ref, v_ref, go_ref, d_ref, nm_ref, nv_ref):
        g = g_ref[...]
        go_ref[...] = g
        d_ref[...], nm_ref[...], nv_ref[...] = _adamw_math(w_ref[...], g, m_ref[...], v_ref[...])

    spec = pl.BlockSpec((tr, cols), lambda i: (i, 0))
    return pl.pallas_call(
        body, name=name, grid=(r // tr,), in_specs=[spec] * 4, out_specs=[spec] * 4,
        out_shape=[_sds((r, cols), F32)] * 4, compiler_params=_params(("arbitrary",)),
    )(w, g, m, v)


def _adamw_ada(w, ct, dm, m, v):
    r, cols = w.shape
    tr = _row_tile(r, cols)

    def body(w_ref, ct_ref, dm_ref, m_ref, v_ref, g_ref, d_ref, nm_ref, nv_ref):
        g = jnp.dot(ct_ref[...], dm_ref[...], preferred_element_type=F32)
        g_ref[...] = g
        d_ref[...], nm_ref[...], nv_ref[...] = _adamw_math(w_ref[...], g, m_ref[...], v_ref[...])

    spec = pl.BlockSpec((tr, cols), lambda i: (i, 0))
    return pl.pallas_call(
        body, name="adamw_ada", grid=(r // tr,),
        in_specs=[spec, pl.BlockSpec((tr, LANE), lambda i: (i, 0)), pl.BlockSpec((LANE, cols), lambda i: (0, 0)), spec, spec],
        out_specs=[spec] * 4, out_shape=[_sds((r, cols), F32)] * 4, compiler_params=_params(("arbitrary",)),
    )(w, ct, dm, m, v)


BLOB_VEC, BLOB_BSGU, BLOB_CONV, BLOB_ADA, BLOB_DMOD, BLOB_LOSS, BLOB_ROWS = 0, 8, 16, 48, 56, 80, 88
N_VEC = 7


def _adamw_small(tot, g_w_sgu, g_conv, params):
    n = len(params)

    def body(*refs):
        tot_ref, gws_ref, gconv_ref = refs[:3]
        wmv = refs[3:3 + 3 * n]
        outs = refs[3 + 3 * n:]
        grads = [tot_ref[pl.ds(BLOB_VEC + i, 1), :] for i in range(N_VEC)]
        grads += [tot_ref[pl.ds(BLOB_BSGU, HEADS), pl.ds(0, CHUNK)], gconv_ref[...], gws_ref[...], tot_ref[pl.ds(BLOB_ADA, 3), :]]
        for i, g in enumerate(grads):
            w_ref, m_ref, v_ref = wmv[3 * i:3 * i + 3]
            d, nm, nv = _adamw_math(w_ref[...], g, m_ref[...], v_ref[...])
            outs[4 * i][...] = g
            outs[4 * i + 1][...] = d
            outs[4 * i + 2][...] = nm
            outs[4 * i + 3][...] = nv

    flat = [a for wmv in params for a in wmv]
    return pl.pallas_call(
        body, name="adamw_small",
        in_specs=[VMEM_SPEC] * (3 + len(flat)), out_specs=[VMEM_SPEC] * (4 * n),
        out_shape=[_sds(wmv[0].shape, F32) for wmv in params for _ in range(4)],
        compiler_params=_params(),
    )(tot, g_w_sgu, g_conv, *flat)


def _set_rows(buf, row, val):
    return lax.dynamic_update_slice(buf, val.astype(F32), (row, 0))


def kernel(x, c, w_ada, b_ada, g_pre, w_in, conv_w, conv_b, conv_ln_g, conv_ln_b, w_conv_out, sgu_ln_g, sgu_ln_b, w_sgu, b_sgu, w_sgu_out, w_o, g_final, loss_target, m_w_ada, m_b_ada, m_g_pre, m_w_in, m_conv_w, m_conv_b, m_conv_ln_g, m_conv_ln_b, m_w_conv_out, m_sgu_ln_g, m_sgu_ln_b, m_w_sgu, m_b_sgu, m_w_sgu_out, m_w_o, m_g_final, v_w_ada, v_b_ada, v_g_pre, v_w_in, v_conv_w, v_conv_b, v_conv_ln_g, v_conv_ln_b, v_w_conv_out, v_sgu_ln_g, v_sgu_ln_b, v_w_sgu, v_b_sgu, v_w_sgu_out, v_w_o, v_g_final):
    me = _place()
    dev, chip = _dev_of(me), _chip_of(me)
    n_ada = w_ada.shape[2]
    conv_cols = conv_w.shape[2]

    b_ada_s = lax.dynamic_slice(b_ada, (0, chip * n_ada), (1, n_ada))
    c_all, mod_all, cw_all, (wg_in,) = _setup_comm(
        jnp.broadcast_to(c, (SUB, D)), w_ada[0], b_ada_s, jnp.pad(conv_w[0], ((0, HALO - CONV_K), (0, 0))),
        [w_in[0].astype(BF16)])
    mod = lax.dynamic_slice(mod_all, (0, dev * SUB, 0), (N_CHIP, 1, n_ada)).reshape(1, 3 * D)
    shift, scale, gate = mod[:, :D], mod[:, D:2 * D], mod[:, 2 * D:]
    conv_w_full = jnp.swapaxes(cw_all, 0, 1).reshape(HALO, D)[:CONV_K]

    loc = _local_step(x[0], loss_target[0], shift, scale, gate, g_pre, conv_w_full, conv_b, conv_ln_g, conv_ln_b,
                      sgu_ln_g, sgu_ln_b, w_sgu[0], b_sgu[0], g_final.reshape(1, D), wg_in,
                      [w_conv_out[0].astype(BF16), w_sgu_out[0].astype(BF16), w_o[0].astype(BF16)])

    d_mod = jnp.concatenate([loc["d_shift"], loc["d_scale"], loc["d_gate"]], axis=0)
    blob = jnp.zeros((BLOB_ROWS, D), F32)
    for i, name in enumerate(["g_pre", "conv_b", "conv_ln_g", "conv_ln_b", "sgu_ln_g", "sgu_ln_b", "g_final"]):
        blob = _set_rows(blob, BLOB_VEC + i, loc[name])
    blob = _set_rows(blob, BLOB_BSGU, loc["b_sgu"])
    blob = _set_rows(blob, BLOB_CONV, loc["conv_w"])
    blob = _set_rows(blob, BLOB_ADA, d_mod)
    blob = lax.dynamic_update_slice(blob, d_mod, (BLOB_DMOD + 3 * dev, 0))
    blob = _set_rows(blob, BLOB_LOSS, loc["loss_cols"])

    big = ["w_in", "w_conv_out", "w_sgu_out", "w_o"]
    contrib_out = [loc[name].reshape(N_CHIP, 2, D // (2 * N_CHIP), D) for name in big[1:]]
    gw_in, full_out, (tot, g_w_sgu) = _grad_matmul_reduce(
        loc["hb"], loc["dp"], "grad_w_in", contrib_out, [blob, loc["w_sgu"].reshape(HEADS * CHUNK, CHUNK)])
    full_in = _reduce_scatter([gw_in.reshape(N_CHIP, 2, D // 2, gw_in.shape[2])], "reduce_w_in")
    g_big = {name: f.reshape(2 * f.shape[1], f.shape[2]) for name, f in zip(big, list(full_in) + list(full_out))}

    loss = jnp.sum(tot[BLOB_LOSS])
    g_conv_s = lax.dynamic_slice(tot, (BLOB_CONV, chip * conv_cols), (CONV_K, conv_cols))
    d_mod_all = tot[BLOB_DMOD:BLOB_DMOD + 3 * N_DEV].reshape(N_DEV, 3 * D)

    ct = jnp.pad(c_all[::SUB].T, ((0, 0), (0, LANE - N_DEV))).astype(BF16)
    dm = jnp.pad(lax.dynamic_slice(d_mod_all, (0, chip * n_ada), (N_DEV, n_ada)), ((0, LANE - N_DEV), (0, 0))).astype(BF16)
    g_ada, d_ada, nm_ada, nv_ada = _adamw_ada(w_ada[0], ct, dm, m_w_ada[0], v_w_ada[0])

    upd = {}
    for name, w, m, v in [("w_in", w_in, m_w_in, v_w_in), ("w_conv_out", w_conv_out, m_w_conv_out, v_w_conv_out),
                          ("w_sgu_out", w_sgu_out, m_w_sgu_out, v_w_sgu_out), ("w_o", w_o, m_w_o, v_w_o)]:
        upd[name] = _adamw(w[0], g_big[name], m[0], v[0], "adamw_" + name)

    def wmv(w, m, v, shape):
        return tuple(a.reshape(shape) for a in (w, m, v))

    small_params = [wmv(w, m, v, (1, D)) for w, m, v in [
        (g_pre, m_g_pre, v_g_pre), (conv_b, m_conv_b, v_conv_b), (conv_ln_g, m_conv_ln_g, v_conv_ln_g),
        (conv_ln_b, m_conv_ln_b, v_conv_ln_b), (sgu_ln_g, m_sgu_ln_g, v_sgu_ln_g), (sgu_ln_b, m_sgu_ln_b, v_sgu_ln_b),
        (g_final, m_g_final, v_g_final)]]
    small_params += [wmv(b_sgu, m_b_sgu, v_b_sgu, (HEADS, CHUNK)), wmv(conv_w, m_conv_w, v_conv_w, (CONV_K, conv_cols)),
                     wmv(w_sgu, m_w_sgu, v_w_sgu, (HEADS * CHUNK, CHUNK)), wmv(b_ada, m_b_ada, v_b_ada, (3, D))]
    small_out = _adamw_small(tot, g_w_sgu, g_conv_s, small_params)

    def leaves(kind):
        vecs = [small_out[4 * i + kind] for i in range(N_VEC)]
        o_b_sgu, o_conv, o_w_sgu, o_b_ada = (small_out[4 * (N_VEC + i) + kind] for i in range(4))
        ada = (g_ada, d_ada, nm_ada, nv_ada)[kind]
        def bigk(name):
            return upd[name][kind][None]
        return [ada[None], o_b_ada.reshape(1, 3 * D), vecs[0], bigk("w_in"), o_conv[None], vecs[1], vecs[2], vecs[3],
                bigk("w_conv_out"), vecs[4], vecs[5], o_w_sgu.reshape(1, HEADS, CHUNK, CHUNK), o_b_sgu[None],
                bigk("w_sgu_out"), bigk("w_o"), vecs[6].reshape(D)]

    return (loss, loc["grad_x"][None], *leaves(0), *leaves(1), *leaves(2), *leaves(3))
```

```python
import functools

import jax
import jax.numpy as jnp
from jax import lax
from jax.experimental import pallas as pl
from jax.experimental.pallas import tpu as pltpu

F32 = jnp.float32
BF16 = jnp.bfloat16
MESH = pl.DeviceIdType.MESH

D = 1024
N_SEC = 8
N_CHIP = 4
N_DEV = 8
EPS = 1e-6
CONV_K = 31
HALO = 32
CHUNK = 128
HEADS = 8
LANE = 128
SUB = 8
PACK = 16
VMEM_LIMIT = 56 * 1024 * 1024

ADAM_LR, ADAM_B1, ADAM_B2, ADAM_EPS, ADAM_WD, ADAM_STEP = 0.001, 0.9, 0.999, 1e-08, 0.01, 10

_SQRT_HALF = 0.7071067811865476
_INV_SQRT_2PI = 0.3989422804014327


def _sds(shape, dtype):
    return jax.ShapeDtypeStruct(shape, dtype)


def _params(sem=None):
    if sem is None:
        return pltpu.CompilerParams(vmem_limit_bytes=VMEM_LIMIT)
    return pltpu.CompilerParams(dimension_semantics=sem, vmem_limit_bytes=VMEM_LIMIT)


def _strips(n_rows, rows, fn):
    def step(s, carry):
        fn(pl.multiple_of(s * rows, rows))
        return carry
    lax.fori_loop(0, n_rows // rows, step, 0)


def _sigmoid(v):
    return 1.0 / (1.0 + jnp.exp(-v))


def _gelu(v):
    return 0.5 * v * (1.0 + lax.erf(v * _SQRT_HALF))


def _gelu_and_grad(v):
    cdf = 0.5 * (1.0 + lax.erf(v * _SQRT_HALF))
    return v * cdf, cdf + v * jnp.exp(-0.5 * v * v) * _INV_SQRT_2PI


def _dsilu(v, sg):
    return sg * (1.0 + v * (1.0 - sg))


def _rowmean(v):
    return jnp.mean(v, axis=-1, keepdims=True)


def _vec_spec(grid_rank):
    zeros = (0, 0)
    if grid_rank == 1:
        return pl.BlockSpec((1, D), lambda i: zeros)
    return pl.BlockSpec((1, D), lambda i, j: zeros)


def _conv_taps(win_ref, r0, lt, weight_of_offset, rows):
    lanes = pl.ds(lt * LANE, LANE)
    win = win_ref[pl.ds(r0, rows + HALO), lanes]
    n_out = rows // SUB
    acc = [jnp.zeros((SUB, LANE), F32) for _ in range(n_out)]
    for phase in range(SUB):
        offs = [o for o in weight_of_offset if o % SUB == phase]
        if not offs:
            continue
        q_max = max(o // SUB for o in offs)
        span = (n_out + q_max) * SUB
        sh = win[phase:phase + span, :]
        for o in offs:
            q = o // SUB
            w = weight_of_offset[o](lanes)
            for m in range(n_out):
                acc[m] = acc[m] + w * sh[(m + q) * SUB:(m + q + 1) * SUB, :]
    return acc


def _branch_a_fwd(p, conv_wb, conv_b, ln_g, ln_b):
    s_len = p.shape[0]
    tm = min(256, s_len)
    n_i = s_len // tm
    rows = 32

    def body(p_ref, wb_ref, cb_ref, g_ref, b_ref, ya_ref, y1_ref, abuf):
        @pl.when(pl.program_id(0) == 0)
        def _():
            abuf[pl.ds(0, HALO), :] = jnp.zeros((HALO, D), F32)

        def glu(r0):
            val = p_ref[pl.ds(r0, PACK), pl.ds(0, D)].astype(F32)
            gl = p_ref[pl.ds(r0, PACK), pl.ds(D, D)].astype(F32)
            abuf[pl.ds(HALO + r0, PACK), :] = val * _sigmoid(gl)
        _strips(tm, PACK,glu)

        taps = {HALO - (CONV_K - 1) + k: (lambda lanes, k=k: wb_ref[pl.ds(k * SUB, SUB), lanes]) for k in range(CONV_K)}

        def conv(r0):
            for lt in range(D // LANE):
                acc = _conv_taps(abuf, r0, lt, taps, rows)
                cb = cb_ref[:, pl.ds(lt * LANE, LANE)]
                for m, v in enumerate(acc):
                    y1_ref[pl.ds(r0 + m * SUB, SUB), pl.ds(lt * LANE, LANE)] = v + cb
        _strips(tm, rows, conv)

        def norm(r0):
            y1 = y1_ref[pl.ds(r0, PACK), :]
            mu = _rowmean(y1)
            yc = y1 - mu
            rstd = lax.rsqrt(_rowmean(yc * yc) + EPS)
            l1 = (yc * rstd) * g_ref[...] + b_ref[...]
            z = p_ref[pl.ds(r0, PACK), pl.ds(2 * D, D)].astype(F32)
            ya_ref[pl.ds(r0, PACK), :] = ((l1 * _sigmoid(l1)) * (z * _sigmoid(z))).astype(BF16)
        _strips(tm, PACK,norm)

        abuf[pl.ds(0, HALO), :] = abuf[pl.ds(tm, HALO), :]

    return pl.pallas_call(
        body, name="branch_a_fwd", grid=(n_i,),
        in_specs=[pl.BlockSpec((tm, 3 * D), lambda i: (i, 0)),
                  pl.BlockSpec((CONV_K * SUB, D), lambda i: (0, 0)), _vec_spec(1), _vec_spec(1), _vec_spec(1)],
        out_specs=[pl.BlockSpec((tm, D), lambda i: (i, 0)), pl.BlockSpec((tm, D), lambda i: (i, 0))],
        out_shape=[_sds((s_len, D), BF16), _sds((s_len, D), F32)],
        scratch_shapes=[pltpu.VMEM((tm + HALO, D), F32)],
        compiler_params=_params(("arbitrary",)),
    )(p, conv_wb, conv_b, ln_g, ln_b)


def _branch_b_fwd(p, wt, bias_full, ln_g, ln_b):
    s_len = p.shape[0]
    tm = min(256, s_len)
    n_i = s_len // tm

    def body(p_ref, wt_ref, bias_ref, g_ref, b_ref, yb_ref, vb, sbuf):
        def norm(r0):
            gv = _gelu(p_ref[pl.ds(r0, PACK), pl.ds(D, D)].astype(F32))
            mu = _rowmean(gv)
            vc = gv - mu
            rstd = lax.rsqrt(_rowmean(vc * vc) + EPS)
            vb[pl.ds(r0, PACK), :] = ((vc * rstd) * g_ref[...] + b_ref[...]).astype(BF16)
        _strips(tm, PACK,norm)

        for ck in range(tm // CHUNK):
            for h in range(HEADS):
                blk = (pl.ds(ck * CHUNK, CHUNK), pl.ds(h * LANE, LANE))
                sbuf[blk] = jnp.dot(wt_ref[h], vb[blk], preferred_element_type=F32) + bias_ref[:, pl.ds(h * LANE, LANE)]

        def gate(r0):
            u = _gelu(p_ref[pl.ds(r0, PACK), pl.ds(0, D)].astype(F32))
            z = p_ref[pl.ds(r0, PACK), pl.ds(2 * D, D)].astype(F32)
            yb_ref[pl.ds(r0, PACK), :] = (u * sbuf[pl.ds(r0, PACK), :] * (z * _sigmoid(z))).astype(BF16)
        _strips(tm, PACK,gate)

    return pl.pallas_call(
        body, name="branch_b_fwd", grid=(n_i,),
        in_specs=[pl.BlockSpec((tm, 3 * D), lambda i: (i, 1)),
                  pl.BlockSpec((HEADS, CHUNK, CHUNK), lambda i: (0, 0, 0)),
                  pl.BlockSpec((CHUNK, D), lambda i: (0, 0)), _vec_spec(1), _vec_spec(1)],
        out_specs=pl.BlockSpec((tm, D), lambda i: (i, 0)),
        out_shape=_sds((s_len, D), BF16),
        scratch_shapes=[pltpu.VMEM((tm, D), BF16), pltpu.VMEM((tm, D), F32)],
        compiler_params=_params(("arbitrary",)),
    )(p, wt, bias_full, ln_g, ln_b)


def _dot_t(a, b):
    return lax.dot_general(a, b, (((1,), (1,)), ((), ())), preferred_element_type=F32)


def _out_proj(p, ya_in, yb_in, x, target, gate, g_final, w_co, w_so, w_o):
    s_len = x.shape[0]
    tm = min(256, s_len)
    n_i = s_len // tm

    def body(pg_ref, ya_ref, yb_ref, x_ref, t_ref, gate_ref, gf_ref, wco_ref, wso_ref, wo_ref,
             dx2_ref, dya_ref, dyb_ref, dp_ref, mb_ref, dob_ref, dyab_ref, dybb_ref, sums_ref):
        @pl.when(pl.program_id(0) == 0)
        def _():
            sums_ref[...] = jnp.zeros((SUB, D), F32)

        y_a = jnp.dot(ya_ref[...], wco_ref[...], preferred_element_type=F32)
        y_b = jnp.dot(yb_ref[...], wso_ref[...], preferred_element_type=F32)
        ga = _sigmoid(pg_ref[:, pl.ds(0, D)].astype(F32))
        gb = _sigmoid(pg_ref[:, pl.ds(D, D)].astype(F32))
        mb = (ga * y_a + gb * y_b).astype(BF16)
        mb_ref[...] = mb
        o = jnp.dot(mb, wo_ref[...], preferred_element_type=F32)
        x2 = x_ref[...] + gate_ref[...] * o
        r2 = lax.rsqrt(_rowmean(x2 * x2) + EPS)
        xh = x2 * r2
        e = xh * gf_ref[...] - t_ref[...]
        dy = e * (1.0 / D)
        dxh = dy * gf_ref[...]
        dx2 = r2 * (dxh - xh * _rowmean(dxh * xh))
        dx2_ref[...] = dx2
        sums_ref[pl.ds(0, 1), :] += jnp.sum(dy * xh, axis=0, keepdims=True)
        sums_ref[pl.ds(1, 1), :] += jnp.sum(dx2 * o, axis=0, keepdims=True)
        sums_ref[pl.ds(2, 1), :] += jnp.sum(e * e, axis=0, keepdims=True) * (0.5 / D)
        dob = (gate_ref[...] * dx2).astype(BF16)
        dob_ref[...] = dob
        dm = _dot_t(dob, wo_ref[...])
        dy_a = (ga * dm).astype(BF16)
        dy_b = (gb * dm).astype(BF16)
        dyab_ref[...] = dy_a
        dybb_ref[...] = dy_b
        dp_ref[:, pl.ds(0, D)] = (dm * y_a * ga * (1.0 - ga)).astype(BF16)
        dp_ref[:, pl.ds(D, D)] = (dm * y_b * gb * (1.0 - gb)).astype(BF16)
        dya_ref[...] = _dot_t(dy_a, wco_ref[...])
        dyb_ref[...] = _dot_t(dy_b, wso_ref[...])

    tile = pl.BlockSpec((tm, D), lambda i: (i, 0))
    wspec = pl.BlockSpec((D, D), lambda i: (0, 0))
    return pl.pallas_call(
        body, name="out_proj", grid=(n_i,),
        in_specs=[pl.BlockSpec((tm, 2 * D), lambda i: (i, 3)), tile, tile, tile, tile, _vec_spec(1), _vec_spec(1),
                  wspec, wspec, wspec],
        out_specs=[tile, tile, tile, pl.BlockSpec((tm, 2 * D), lambda i: (i, 3)), tile, tile, tile, tile,
                   pl.BlockSpec((SUB, D), lambda i: (0, 0))],
        out_shape=[_sds((s_len, D), F32), _sds((s_len, D), F32), _sds((s_len, D), F32), _sds((s_len, N_SEC * D), BF16),
                   _sds((s_len, D), BF16), _sds((s_len, D), BF16), _sds((s_len, D), BF16), _sds((s_len, D), BF16),
                   _sds((SUB, D), F32)],
        compiler_params=_params(("arbitrary",)),
    )(p, ya_in, yb_in, x, target, gate, g_final, w_co, w_so, w_o)


A_STATS_ROWS = 8 + HALO


def _branch_a_bwd(p, y1, dya_in, dp, conv_wb, ln_g, ln_b):
    s_len = p.shape[0]
    tm = min(256, s_len)
    n_i = s_len // tm
    rows = 64
    n_out = rows // SUB

    def tile_of(i):
        return n_i - 1 - i

    def body(p_ref, y1_ref, dya_ref, dp_in, wb_ref, g_ref, b_ref, dp_ref, st_ref, dybuf, acc8, tapacc):
        del dp_in
        i = pl.program_id(0)

        @pl.when(i == 0)
        def _():
            dybuf[pl.ds(tm, HALO), :] = jnp.zeros((HALO, D), F32)
            st_ref[...] = jnp.zeros((A_STATS_ROWS, D), F32)
            acc8[...] = jnp.zeros((3 * PACK, D), F32)
            tapacc[...] = jnp.zeros((CONV_K * SUB, D), F32)

        def norm_bwd(r0):
            y1 = y1_ref[pl.ds(r0, PACK), :]
            mu = _rowmean(y1)
            yc = y1 - mu
            rstd = lax.rsqrt(_rowmean(yc * yc) + EPS)
            n1 = yc * rstd
            l1 = n1 * g_ref[...] + b_ref[...]
            sg = _sigmoid(l1)
            z = p_ref[pl.ds(r0, PACK), pl.ds(2 * D, D)].astype(F32)
            sz = _sigmoid(z)
            dya = dya_ref[pl.ds(r0, PACK), :]
            dp_ref[pl.ds(r0, PACK), pl.ds(2 * D, D)] = (dya * (l1 * sg) * _dsilu(z, sz)).astype(BF16)
            dl1 = dya * (z * sz) * _dsilu(l1, sg)
            acc8[pl.ds(0, PACK), :] += dl1 * n1
            acc8[pl.ds(PACK, PACK), :] += dl1
            dn1 = dl1 * g_ref[...]
            dy1 = rstd * (dn1 - _rowmean(dn1) - n1 * _rowmean(dn1 * n1))
            acc8[pl.ds(2 * PACK, PACK), :] += dy1
            dybuf[pl.ds(r0, PACK), :] = dy1
        _strips(tm, PACK,norm_bwd)

        def conv_bwd(r0):
            for lt in range(D // LANE):
                lanes = pl.ds(lt * LANE, LANE)
                glanes = pl.ds(D + lt * LANE, LANE)
                win = dybuf[pl.ds(r0, rows + HALO), lanes]
                sg16, a16 = [], []
                for h in range(rows // PACK):
                    rr = pl.ds(r0 + h * PACK, PACK)
                    s = _sigmoid(p_ref[rr, glanes].astype(F32))
                    sg16.append(s)
                    a16.append(p_ref[rr, lanes].astype(F32) * s)
                a = [a16[m // 2][(m % 2) * SUB:(m % 2 + 1) * SUB, :] for m in range(n_out)]
                da = [jnp.zeros((SUB, LANE), F32) for _ in range(n_out)]
                for phase in range(SUB):
                    offs = [o for o in range(CONV_K) if o % SUB == phase]
                    q_max = max(o // SUB for o in offs)
                    sh = win[phase:phase + (n_out + q_max) * SUB, :]
                    for o in offs:
                        k, q = CONV_K - 1 - o, o // SUB
                        w = wb_ref[pl.ds(k * SUB, SUB), lanes]
                        part = None
                        for m in range(n_out):
                            s = sh[(m + q) * SUB:(m + q + 1) * SUB, :]
                            da[m] = da[m] + w * s
                            part = a[m] * s if part is None else part + a[m] * s
                        tapacc[pl.ds(k * SUB, SUB), lanes] += part
                for h in range(rows // PACK):
                    rr = pl.ds(r0 + h * PACK, PACK)
                    da16 = jnp.concatenate(da[2 * h:2 * h + 2], axis=0)
                    dp_ref[rr, lanes] = (da16 * sg16[h]).astype(BF16)
                    dp_ref[rr, glanes] = (da16 * a16[h] * (1.0 - sg16[h])).astype(BF16)
        _strips(tm, rows, conv_bwd)

        dybuf[pl.ds(tm, HALO), :] = dybuf[pl.ds(0, HALO), :]

        @pl.when(i == n_i - 1)
        def _():
            for j in range(3):
                st_ref[pl.ds(j, 1), :] = jnp.sum(acc8[pl.ds(j * PACK, PACK), :], axis=0, keepdims=True)
            for k in range(CONV_K):
                st_ref[pl.ds(SUB + k, 1), :] = jnp.sum(tapacc[pl.ds(k * SUB, SUB), :], axis=0, keepdims=True)

    return pl.pallas_call(
        body, name="branch_a_bwd", grid=(n_i,),
        in_specs=[pl.BlockSpec((tm, 3 * D), lambda i: (tile_of(i), 0)),
                  pl.BlockSpec((tm, D), lambda i: (tile_of(i), 0)),
                  pl.BlockSpec((tm, D), lambda i: (tile_of(i), 0)),
                  pl.BlockSpec(memory_space=pl.ANY),
                  pl.BlockSpec((CONV_K * SUB, D), lambda i: (0, 0)), _vec_spec(1), _vec_spec(1)],
        out_specs=[pl.BlockSpec((tm, 3 * D), lambda i: (tile_of(i), 0)),
                   pl.BlockSpec((A_STATS_ROWS, D), lambda i: (0, 0))],
        out_shape=[_sds(dp.shape, BF16), _sds((A_STATS_ROWS, D), F32)],
        scratch_shapes=[pltpu.VMEM((tm + HALO, D), F32), pltpu.VMEM((3 * PACK, D), F32), pltpu.VMEM((CONV_K * SUB, D), F32)],
        input_output_aliases={3: 0},
        compiler_params=_params(("arbitrary",)),
    )(p, y1, dya_in, dp, conv_wb, ln_g, ln_b)


def _branch_b_bwd(p, dyb_in, dp, wt, wtt, bias_full, ln_g, ln_b):
    s_len = p.shape[0]
    tm = min(256, s_len)
    n_i = s_len // tm

    def body(p_ref, dyb_ref, dp_in, wt_ref, wtt_ref, bias_ref, g_ref, b_ref, dp_ref, st_ref, gbt_ref, gw_ref,
             vb, n2buf, rstdbuf, sbuf, dsb, dvbuf, acc8, gb_ref, dgbuf):
        del dp_in
        i = pl.program_id(0)

        @pl.when(i == 0)
        def _():
            st_ref[...] = jnp.zeros((SUB, D), F32)
            gbt_ref[...] = jnp.zeros((CHUNK, LANE), F32)
            gb_ref[...] = jnp.zeros((CHUNK, D), F32)
            gw_ref[...] = jnp.zeros((HEADS, CHUNK, CHUNK), F32)
            acc8[...] = jnp.zeros((2 * PACK, D), F32)

        def norm(r0):
            gv, dgv = _gelu_and_grad(p_ref[pl.ds(r0, PACK), pl.ds(D, D)].astype(F32))
            dgbuf[pl.ds(r0, PACK), :] = dgv
            mu = _rowmean(gv)
            vc = gv - mu
            rstd = lax.rsqrt(_rowmean(vc * vc) + EPS)
            n2 = vc * rstd
            n2buf[pl.ds(r0, PACK), :] = n2
            rstdbuf[pl.ds(r0, PACK), :] = jnp.broadcast_to(rstd, (PACK, LANE))
            vb[pl.ds(r0, PACK), :] = (n2 * g_ref[...] + b_ref[...]).astype(BF16)
        _strips(tm, PACK,norm)

        for ck in range(tm // CHUNK):
            for h in range(HEADS):
                blk = (pl.ds(ck * CHUNK, CHUNK), pl.ds(h * LANE, LANE))
                sbuf[blk] = jnp.dot(wt_ref[h], vb[blk], preferred_element_type=F32) + bias_ref[:, pl.ds(h * LANE, LANE)]

        def gate_bwd(r0):
            pu = p_ref[pl.ds(r0, PACK), pl.ds(0, D)].astype(F32)
            u, du = _gelu_and_grad(pu)
            z = p_ref[pl.ds(r0, PACK), pl.ds(2 * D, D)].astype(F32)
            sg = _sigmoid(z)
            sz = z * sg
            s = sbuf[pl.ds(r0, PACK), :]
            dyb = dyb_ref[pl.ds(r0, PACK), :]
            ds = dyb * u * sz
            dsb[pl.ds(r0, PACK), :] = ds.astype(BF16)
            gb_ref[pl.ds(pl.multiple_of(r0 % CHUNK, PACK), PACK), :] += ds
            dp_ref[pl.ds(r0, PACK), pl.ds(0, D)] = (dyb * s * sz * du).astype(BF16)
            dp_ref[pl.ds(r0, PACK), pl.ds(2 * D, D)] = (dyb * u * s * _dsilu(z, sg)).astype(BF16)
        _strips(tm, PACK,gate_bwd)

        for ck in range(tm // CHUNK):
            for h in range(HEADS):
                blk = (pl.ds(ck * CHUNK, CHUNK), pl.ds(h * LANE, LANE))
                d_s = dsb[blk]
                dvbuf[blk] = jnp.dot(wtt_ref[h], d_s, preferred_element_type=F32)
                gw_ref[h] += _dot_t(d_s, vb[blk])

        def norm_bwd(r0):
            dv = dvbuf[pl.ds(r0, PACK), :]
            n2 = n2buf[pl.ds(r0, PACK), :]
            rstd = rstdbuf[pl.ds(r0, PACK), pl.ds(0, 1)]
            acc8[pl.ds(0, PACK), :] += dv * n2
            acc8[pl.ds(PACK, PACK), :] += dv
            dn2 = dv * g_ref[...]
            dgv = rstd * (dn2 - _rowmean(dn2) - n2 * _rowmean(dn2 * n2))
            dp_ref[pl.ds(r0, PACK), pl.ds(D, D)] = (dgv * dgbuf[pl.ds(r0, PACK), :]).astype(BF16)
        _strips(tm, PACK,norm_bwd)

        @pl.when(i == n_i - 1)
        def _():
            for j in range(2):
                st_ref[pl.ds(j, 1), :] = jnp.sum(acc8[pl.ds(j * PACK, PACK), :], axis=0, keepdims=True)
            row = lax.broadcasted_iota(jnp.int32, (CHUNK, CHUNK), 0)
            col = lax.broadcasted_iota(jnp.int32, (CHUNK, CHUNK), 1)
            for h in range(HEADS):
                gw_ref[h] = jnp.where(row >= col, gw_ref[h], 0.0)
            lane = lax.broadcasted_iota(jnp.int32, (CHUNK, LANE), 1)
            gbt = jnp.zeros((CHUNK, LANE), F32)
            for h in range(HEADS):
                gbt = jnp.where(lane == h, jnp.sum(gb_ref[:, pl.ds(h * LANE, LANE)], axis=1, keepdims=True), gbt)
            gbt_ref[...] = gbt

    wspec = pl.BlockSpec((HEADS, CHUNK, CHUNK), lambda i: (0, 0, 0))
    return pl.pallas_call(
        body, name="branch_b_bwd", grid=(n_i,),
        in_specs=[pl.BlockSpec((tm, 3 * D), lambda i: (i, 1)), pl.BlockSpec((tm, D), lambda i: (i, 0)),
                  pl.BlockSpec(memory_space=pl.ANY), wspec, wspec,
                  pl.BlockSpec((CHUNK, D), lambda i: (0, 0)), _vec_spec(1), _vec_spec(1)],
        out_specs=[pl.BlockSpec((tm, 3 * D), lambda i: (i, 1)), pl.BlockSpec((SUB, D), lambda i: (0, 0)),
                   pl.BlockSpec((CHUNK, LANE), lambda i: (0, 0)), wspec],
        out_shape=[_sds(dp.shape, BF16), _sds((SUB, D), F32), _sds((CHUNK, LANE), F32), _sds((HEADS, CHUNK, CHUNK), F32)],
        scratch_shapes=[pltpu.VMEM((tm, D), BF16), pltpu.VMEM((tm, D), F32), pltpu.VMEM((tm, LANE), F32),
                        pltpu.VMEM((tm, D), F32), pltpu.VMEM((tm, D), BF16), pltpu.VMEM((tm, D), F32),
                        pltpu.VMEM((2 * PACK, D), F32), pltpu.VMEM((CHUNK, D), F32), pltpu.VMEM((tm, D), F32)],
        input_output_aliases={2: 0},
        compiler_params=_params(("arbitrary",)),
    )(p, dyb_in, dp, wt, wtt, bias_full, ln_g, ln_b)


def _in_proj_bwd(dp, wg_in, x, dx2, shift, scale, g_pre):
    del shift
    s_len = x.shape[0]
    tm = min(512, s_len)
    n_i = s_len // tm
    wn = wg_in.shape[2]

    def body(dp0, dp1, dp2, dp3, w_ref, x_ref, dx2_ref, sc_ref, g_ref, gx_ref, st_ref, acc, acc8):
        i = pl.program_id(0)

        @pl.when(i == 0)
        def _():
            st_ref[...] = jnp.zeros((SUB, D), F32)
            acc8[...] = jnp.zeros((3 * PACK, D), F32)

        dh = _dot_t(dp0[...], w_ref[0])
        for j, dp_ref in enumerate((dp1, dp2, dp3), start=1):
            dh = dh + _dot_t(dp_ref[...], w_ref[j])
        acc[...] = dh

        def strip(r0):
            xs = x_ref[pl.ds(r0, PACK), :]
            r = lax.rsqrt(_rowmean(xs * xs) + EPS)
            xn = xs * r
            dhs = acc[pl.ds(r0, PACK), :]
            acc8[pl.ds(0, PACK), :] += dhs
            acc8[pl.ds(PACK, PACK), :] += dhs * (xn * g_ref[...])
            dhp = dhs * (1.0 + sc_ref[...])
            acc8[pl.ds(2 * PACK, PACK), :] += dhp * xn
            dxn = dhp * g_ref[...]
            gx_ref[pl.ds(r0, PACK), :] = dx2_ref[pl.ds(r0, PACK), :] + r * (dxn - xn * _rowmean(dxn * xn))
        _strips(tm, PACK, strip)

        @pl.when(i == n_i - 1)
        def _():
            for k in range(3):
                st_ref[pl.ds(k, 1), :] = jnp.sum(acc8[pl.ds(k * PACK, PACK), :], axis=0, keepdims=True)

    tile = pl.BlockSpec((tm, D), lambda i: (i, 0))
    return pl.pallas_call(
        body, name="in_proj_bwd", grid=(n_i,),
        in_specs=[pl.BlockSpec((tm, wn), functools.partial(lambda j, i: (i, j), j)) for j in range(N_CHIP)] + [
                  pl.BlockSpec((N_CHIP, D, wn), lambda i: (0, 0, 0), pipeline_mode=pl.Buffered(1)),
                  tile, tile, _vec_spec(1), _vec_spec(1)],
        out_specs=[tile, pl.BlockSpec((SUB, D), lambda i: (0, 0))],
        out_shape=[_sds((s_len, D), F32), _sds((SUB, D), F32)],
        scratch_shapes=[pltpu.VMEM((tm, D), F32), pltpu.VMEM((3 * PACK, D), F32)],
        compiler_params=_params(("arbitrary",)),
    )(dp, dp, dp, dp, wg_in, x, dx2, scale, g_pre)


def _grad_matmul(a, b, name):
    s_len, n = b.shape
    cb = min(2 * D, n)
    tn = 512
    per = cb // tn

    def body(a_ref, b_ref, ob_ref):
        ob_ref[0] = lax.dot_general(a_ref[...], b_ref[...], (((0,), (0,)), ((), ())),
                                    preferred_element_type=F32).astype(BF16)

    return pl.pallas_call(
        body, name=name, grid=(n // tn,),
        in_specs=[pl.BlockSpec((s_len, D), lambda j: (0, 0), pipeline_mode=pl.Buffered(1)),
                  pl.BlockSpec((s_len, tn), lambda j: (0, j))],
        out_specs=pl.BlockSpec((1, D, tn), lambda j: (j // per, 0, j % per)),
        out_shape=_sds((n // cb, D, cb), BF16),
        compiler_params=_params(("arbitrary",)),
    )(a, b)


def _local_step(x, target, shift, scale, gate, g_pre, conv_w_full, conv_b, conv_ln_g, conv_ln_b,
                sgu_ln_g, sgu_ln_b, w_sgu, b_sgu, g_final, wg_in, out_shards):
    conv_wb = jnp.repeat(conv_w_full, SUB, axis=0)
    causal = jnp.tril(jnp.ones((CHUNK, CHUNK), dtype=bool))
    wt = jnp.where(causal[None], w_sgu, 0.0).astype(BF16)
    wtt = jnp.swapaxes(wt, 1, 2)
    bias_full = jnp.repeat(b_sgu.T, LANE, axis=1)

    p, hb, gathered = _in_proj_gather(x, shift, scale, g_pre, wg_in, out_shards)
    w_co, w_so, w_o = (g.reshape(D, D) for g in gathered)
    ya_in, y1 = _branch_a_fwd(p, conv_wb, conv_b, conv_ln_g, conv_ln_b)
    yb_in = _branch_b_fwd(p, wt, bias_full, sgu_ln_g, sgu_ln_b)
    dx2, dya_in, dyb_in, dp, mb, dob, dyab, dybb, sums_o = _out_proj(
        p, ya_in, yb_in, x, target, gate, g_final, w_co, w_so, w_o)
    dp, st_a = _branch_a_bwd(p, y1, dya_in, dp, conv_wb, conv_ln_g, conv_ln_b)
    dp, st_b, gbt, gws = _branch_b_bwd(p, dyb_in, dp, wt, wtt, bias_full, sgu_ln_g, sgu_ln_b)
    grad_x, st_i = _in_proj_bwd(dp, wg_in, x, dx2, shift, scale, g_pre)
    gw_o = _grad_matmul(mb, dob, "grad_w_o")
    gw_co = _grad_matmul(ya_in, dyab, "grad_w_conv_out")
    gw_so = _grad_matmul(yb_in, dybb, "grad_w_sgu_out")
    return dict(
        grad_x=grad_x, loss_cols=sums_o[2:3], g_final=sums_o[0:1], d_gate=sums_o[1:2],
        d_shift=st_i[0:1], d_scale=st_i[1:2], g_pre=st_i[2:3],
        conv_ln_g=st_a[0:1], conv_ln_b=st_a[1:2], conv_b=st_a[2:3], conv_w=st_a[SUB:SUB + CONV_K],
        sgu_ln_g=st_b[0:1], sgu_ln_b=st_b[1:2], b_sgu=gbt[:, :HEADS].T, w_sgu=gws,
        hb=hb, dp=dp, w_o=gw_o, w_conv_out=gw_co, w_sgu_out=gw_so)


ANY_SPEC = pl.BlockSpec(memory_space=pl.ANY)
VMEM_SPEC = pl.BlockSpec(memory_space=pltpu.VMEM)


def _place():
    return lax.axis_index("x"), lax.axis_index("y"), lax.axis_index("c")


def _peer(k):
    x, y, c = _place()
    return (1 - x if k & 4 else x, 1 - y if k & 2 else y, 1 - c if k & 1 else c)


def _dev_of(p):
    return 4 * p[0] + 2 * p[1] + p[2]


def _chip_of(p):
    return 2 * p[0] + p[1]


def _rdma(src, dst, send_sem, recv_sem, to):
    return pltpu.make_async_remote_copy(src_ref=src, dst_ref=dst, send_sem=send_sem, recv_sem=recv_sem,
                                        device_id=to, device_id_type=MESH)


CHIP_PEERS = (2, 4, 6)
ALL_PEERS = tuple(range(1, N_DEV))
SIBLING = 1


def _setup_comm(c8, w_ada_s, b_ada_s, convw_s, shards):
    n_mod = w_ada_s.shape[1]
    rows = SUB * N_DEV
    n = len(shards)
    parts = 4

    def body(c8_ref, wada_ref, bada_ref, cw_ref, *refs):
        ins, (call_ref, mod_ref, cwall_ref), outs = refs[:n], refs[n:n + 3], refs[n + 3:2 * n + 3]
        csend, crecv, wsend, wrecv, msend, mrecv = refs[2 * n + 3:2 * n + 9]
        ins = [r.at[:, pl.ds(k * (s.shape[1] // parts), s.shape[1] // parts)] for r, s in zip(ins, shards) for k in range(parts)]
        outs = [r.at[:, :, pl.ds(k * (s.shape[1] // parts), s.shape[1] // parts)] for r, s in zip(outs, shards) for k in range(parts)]
        gather_a, gather_b, gather_c = _gather_phases([s.shape[0] for s in shards for _ in range(parts)], ins, outs, refs[2 * n + 9:])
        me = _place()
        dev, chip = _dev_of(me), _chip_of(me)

        def c_rows(d):
            return call_ref.at[pl.ds(pl.multiple_of(d * SUB, SUB), SUB), :]

        call_ref[pl.ds(pl.multiple_of(dev * SUB, SUB), SUB), :] = c8_ref[...]
        cwall_ref[chip] = cw_ref[...]
        c_out = [_rdma(c8_ref, c_rows(dev), csend.at[k], crecv.at[k], _peer(k)) for k in ALL_PEERS]
        w_out = [_rdma(cw_ref, cwall_ref.at[chip], wsend.at[k], wrecv.at[k], _peer(k)) for k in CHIP_PEERS]
        for cp in c_out + w_out:
            cp.start()
        for k in ALL_PEERS:
            _rdma(c8_ref, c_rows(_dev_of(_peer(k))), csend.at[k], crecv.at[k], _peer(k)).wait_recv()
        part = jnp.dot(call_ref[...].astype(BF16), wada_ref[...].astype(BF16), preferred_element_type=F32) + bada_ref[...]
        mod_ref[chip] = part
        m_out = [_rdma(mod_ref.at[chip], mod_ref.at[chip], msend.at[k], mrecv.at[k], _peer(k)) for k in CHIP_PEERS]
        for cp in m_out:
            cp.start()
        gather_a()
        for k in CHIP_PEERS:
            pc = _chip_of(_peer(k))
            _rdma(cw_ref, cwall_ref.at[pc], wsend.at[k], wrecv.at[k], _peer(k)).wait_recv()
            _rdma(mod_ref.at[pc], mod_ref.at[pc], msend.at[k], mrecv.at[k], _peer(k)).wait_recv()
        for cp in c_out + w_out + m_out:
            cp.wait_send()
        gather_b()
        gather_c()

    res = pl.pallas_call(
        body, name="setup_comm",
        in_specs=[VMEM_SPEC] * (4 + n), out_specs=[VMEM_SPEC] * (3 + n),
        out_shape=([_sds((rows, D), F32), _sds((N_CHIP, rows, n_mod), F32), _sds((N_CHIP,) + convw_s.shape, F32)]
                   + [_sds((N_CHIP,) + s.shape, s.dtype) for s in shards]),
        scratch_shapes=([pltpu.SemaphoreType.DMA((N_DEV,))] * 6 + [pltpu.SemaphoreType.DMA((n * parts,))]
                        + [pltpu.SemaphoreType.DMA((n * parts, len(CHIP_PEERS)))] * 4),
        compiler_params=_params(),
    )(c8, w_ada_s, b_ada_s, convw_s, *shards)
    return res[0], res[1], res[2], res[3:]


def _gather_phases(row_counts, ins, dsts, sems):
    n = len(row_counts)
    lsem, isend, irecv, dsend, drecv = sems
    me = _place()
    chip, c = _chip_of(me), me[2]

    def half(t, which):
        hr = row_counts[t] // 2
        return pl.ds(pl.multiple_of(which * hr, hr), hr)

    def local(t):
        return pltpu.make_async_copy(ins[t], dsts[t].at[chip], lsem.at[t])

    def to_chip(t, j):
        return _rdma(ins[t].at[half(t, c)], dsts[t].at[chip, half(t, c)], isend.at[t, j], irecv.at[t, j], _peer(CHIP_PEERS[j]))

    def landed(t, j, which):
        return dsts[t].at[_chip_of(_peer(CHIP_PEERS[j])), half(t, which)]

    def to_sibling(t, j):
        return _rdma(landed(t, j, c), landed(t, j, c), dsend.at[t, j], drecv.at[t, j], _peer(SIBLING))

    pairs = [(t, j) for t in range(n) for j in range(len(CHIP_PEERS))]

    def phase_a():
        for t in range(n):
            local(t).start()
        for t, j in pairs:
            to_chip(t, j).start()

    def phase_b():
        for t, j in pairs:
            _rdma(landed(t, j, c), landed(t, j, c), isend.at[t, j], irecv.at[t, j], _peer(CHIP_PEERS[j])).wait_recv()
            to_sibling(t, j).start()

    def phase_c():
        for t, j in pairs:
            _rdma(landed(t, j, 1 - c), landed(t, j, 1 - c), dsend.at[t, j], drecv.at[t, j], _peer(SIBLING)).wait_recv()
        for t, j in pairs:
            to_chip(t, j).wait_send()
            to_sibling(t, j).wait_send()
        for t in range(n):
            local(t).wait()

    return phase_a, phase_b, phase_c


def _in_proj_gather(x, shift, scale, g_pre, wg_in, shards):
    s_len = x.shape[0]
    tm = min(256, s_len)
    n_i = s_len // tm
    wn = wg_in.shape[2]
    n = len(shards)

    def body(x_ref, sh_ref, sc_ref, g_ref, w_ref, *refs):
        ins, p_ref, hb_ref, outs = refs[:n], refs[n], refs[n + 1], refs[n + 2:2 * n + 2]
        gath, sems = refs[2 * n + 2:3 * n + 2], refs[3 * n + 2:]
        phases = _gather_phases([s.shape[0] for s in shards], ins, gath, sems)
        i = pl.program_id(0)
        for step, phase in zip((0, n_i // 2, n_i - 1), phases):
            pl.when(i == step)(phase)

        @pl.when(i == n_i - 1)
        def _():
            for t in range(n):
                outs[t][...] = gath[t][...]

        def strip(r0):
            xs = x_ref[pl.ds(r0, PACK), :]
            r = lax.rsqrt(_rowmean(xs * xs) + EPS)
            h = (xs * r) * g_ref[...] * (1.0 + sc_ref[...]) + sh_ref[...]
            hb_ref[pl.ds(r0, PACK), :] = h.astype(BF16)
        _strips(tm, PACK, strip)
        hb = hb_ref[...]
        for j in range(N_CHIP):
            p_ref[:, pl.ds(j * wn, wn)] = jnp.dot(hb, w_ref[j], preferred_element_type=F32).astype(BF16)

    res = pl.pallas_call(
        body, name="in_proj", grid=(n_i,),
        in_specs=[pl.BlockSpec((tm, D), lambda i: (i, 0)), _vec_spec(1), _vec_spec(1), _vec_spec(1),
                  pl.BlockSpec((N_CHIP, D, wn), lambda i: (0, 0, 0), pipeline_mode=pl.Buffered(1))] + [VMEM_SPEC] * n,
        out_specs=[pl.BlockSpec((tm, N_CHIP * wn), lambda i: (i, 0)), pl.BlockSpec((tm, D), lambda i: (i, 0))] + [VMEM_SPEC] * n,
        out_shape=([_sds((s_len, N_SEC * D), BF16), _sds((s_len, D), BF16)]
                   + [_sds((N_CHIP,) + s.shape, s.dtype) for s in shards]),
        scratch_shapes=([pltpu.VMEM((N_CHIP,) + s.shape, s.dtype) for s in shards]
                        + [pltpu.SemaphoreType.DMA((n,))] + [pltpu.SemaphoreType.DMA((n, len(CHIP_PEERS)))] * 4),
        compiler_params=_params(("arbitrary",)),
    )(x, shift, scale, g_pre, wg_in, *shards)
    return res[0], res[1], res[2:]


def _reduce_scatter(grads, name):
    n = len(grads)
    shapes = [g.shape[2:] for g in grads]
    parts = 4
    part_shapes = [(r, cols // parts) for r, cols in shapes for _ in range(parts)]

    def body(*refs):
        def halves(group, lead):
            return [ref.at[(slice(None),) * lead + (pl.ds(k * (s[1] // parts), s[1] // parts),)]
                    for ref, s in zip(group, shapes) for k in range(parts)]
        ins, outs = halves(refs[:n], 3), halves(refs[n:2 * n], 2)
        pbufs, rbufs, accs = halves(refs[2 * n:3 * n], 2), halves(refs[3 * n:4 * n], 2), halves(refs[4 * n:5 * n], 1)
        for phase in _reduce_phases(part_shapes, ins, outs, pbufs, rbufs, accs, refs[5 * n:]):
            phase()

    return pl.pallas_call(
        body, name=name,
        in_specs=[VMEM_SPEC] * n, out_specs=[VMEM_SPEC] * n,
        out_shape=[_sds((2,) + s, F32) for s in shapes],
        scratch_shapes=_reduce_scratch(shapes)[:3 * n] + _reduce_scratch(part_shapes)[3 * parts * n:],
        compiler_params=_params(),
    )(*grads)


def _reduce_phases(shapes, ins, outs, pbufs, rbufs, accs, sems):
    n = len(shapes)
    psend, precv, csend, crecv, fsend, frecv = sems
    me = _place()
    chip, c = _chip_of(me), me[2]
    sib = _peer(SIBLING)

    def to_sibling(t, d):
        return _rdma(ins[t].at[d, 1 - c], pbufs[t].at[d], psend.at[t, d], precv.at[t, d], sib)

    def to_chip(t, j):
        return _rdma(pbufs[t].at[jnp.bitwise_xor(chip, j)], rbufs[t].at[j - 1], csend.at[t, j], crecv.at[t, j], _peer(2 * j))

    def finished(t):
        return _rdma(outs[t].at[c], outs[t].at[c], fsend.at[t], frecv.at[t], sib)

    def phase_a():
        for j in (1, 2, 3, 0):
            for t in range(n):
                to_sibling(t, jnp.bitwise_xor(chip, j)).start()

    def phase_b():
        for j in (1, 2, 3, 0):
            d = jnp.bitwise_xor(chip, j)
            for t in range(n):
                to_sibling(t, d).wait_recv()

                def pair_sum(r0, t=t, d=d, j=j):
                    rows = pl.ds(r0, PACK)
                    s = ins[t][d, c, rows, :].astype(F32) + pbufs[t][d, rows, :].astype(F32)
                    if j == 0:
                        accs[t][rows, :] = s
                    else:
                        pbufs[t][d, rows, :] = s.astype(BF16)
                _strips(shapes[t][0], PACK, pair_sum)
                if j:
                    to_chip(t, j).start()

    def phase_c():
        for t in range(n):
            for j in (1, 2, 3):
                blk = rbufs[t].at[j - 1]
                _rdma(blk, blk, csend.at[t, j], crecv.at[t, j], _peer(2 * j)).wait_recv()

            def total(r0, t=t):
                rows = pl.ds(r0, PACK)
                s = accs[t][rows, :] + rbufs[t][0, rows, :].astype(F32)
                s = s + rbufs[t][1, rows, :].astype(F32)
                outs[t][c, rows, :] = s + rbufs[t][2, rows, :].astype(F32)
            _strips(shapes[t][0], PACK, total)
            finished(t).start()

    def phase_d():
        for t in range(n):
            blk = outs[t].at[1 - c]
            _rdma(blk, blk, fsend.at[t], frecv.at[t], sib).wait_recv()
        for t in range(n):
            for d in range(N_CHIP):
                to_sibling(t, d).wait_send()
            for j in (1, 2, 3):
                to_chip(t, j).wait_send()
            finished(t).wait_send()

    return phase_a, phase_b, phase_c, phase_d


def _sum_small_phases(ins, outs, pbufs, buf4s, sems):
    n = len(ins)
    psend, precv, send, recv = sems
    chip = _chip_of(_place())

    def swap(t):
        return _rdma(ins[t], pbufs[t], psend.at[t], precv.at[t], _peer(SIBLING))

    def to_chip(t, k):
        return _rdma(buf4s[t].at[chip], buf4s[t].at[chip], send.at[t, k], recv.at[t, k], _peer(k))

    def phase_a():
        for t in range(n):
            swap(t).start()

    def phase_b():
        for t in range(n):
            swap(t).wait()
            buf4s[t][chip] = ins[t][...] + pbufs[t][...]
            for k in CHIP_PEERS:
                to_chip(t, k).start()

    def phase_c():
        for t in range(n):
            for k in CHIP_PEERS:
                blk = buf4s[t].at[_chip_of(_peer(k))]
                _rdma(blk, blk, send.at[t, k], recv.at[t, k], _peer(k)).wait_recv()
            outs[t][...] = (buf4s[t][0] + buf4s[t][1]) + (buf4s[t][2] + buf4s[t][3])

    def phase_d():
        for t in range(n):
            for k in CHIP_PEERS:
                to_chip(t, k).wait_send()

    return phase_a, phase_b, phase_c, phase_d


def _sum_small_scratch(blobs):
    n = len(blobs)
    return ([pltpu.VMEM(b.shape, F32) for b in blobs] + [pltpu.VMEM((N_CHIP,) + b.shape, F32) for b in blobs]
            + [pltpu.SemaphoreType.DMA((n,))] * 2 + [pltpu.SemaphoreType.DMA((n, N_DEV))] * 2)


def _reduce_scratch(shapes):
    n = len(shapes)
    return ([pltpu.VMEM((N_CHIP,) + s, BF16) for s in shapes] + [pltpu.VMEM((N_CHIP - 1,) + s, BF16) for s in shapes]
            + [pltpu.VMEM(s, F32) for s in shapes]
            + [pltpu.SemaphoreType.DMA((n, N_CHIP))] * 4 + [pltpu.SemaphoreType.DMA((n,))] * 2)


def _grad_matmul_reduce(a, b, name, grads, blobs):
    s_len, n_cols = b.shape
    cb = min(2 * D, n_cols)
    tn = 512
    per = cb // tn
    steps = n_cols // tn
    n, nb = len(grads), len(blobs)
    shapes = [g.shape[2:] for g in grads]
    n_red = len(_reduce_scratch(shapes))

    def body(a_ref, b_ref, *refs):
        ins, bins = refs[:n], refs[n:n + nb]
        ob_ref, outs, bouts = refs[n + nb], refs[n + nb + 1:2 * n + nb + 1], refs[2 * n + nb + 1:2 * (n + nb) + 1]
        scratch = refs[2 * (n + nb) + 1:]
        fulls, red, small = scratch[:n], scratch[n:n + n_red], scratch[n + n_red:]
        phases = _reduce_phases(shapes, ins, fulls, red[:n], red[n:2 * n], red[2 * n:3 * n], red[3 * n:])
        small_phases = _sum_small_phases(bins, bouts, small[:nb], small[nb:2 * nb], small[2 * nb:])
        j = pl.program_id(0)
        for step, phase in zip((0, 2, steps - 2, steps - 1), phases):
            pl.when(j == step)(phase)
        for step, phase in zip((1, 3, steps - 2, steps - 1), small_phases):
            pl.when(j == step)(phase)

        @pl.when(j == steps - 1)
        def _():
            for t in range(n):
                outs[t][...] = fulls[t][...]
        ob_ref[0] = lax.dot_general(a_ref[...], b_ref[...], (((0,), (0,)), ((), ())),
                                    preferred_element_type=F32).astype(BF16)

    res = pl.pallas_call(
        body, name=name, grid=(steps,),
        in_specs=[pl.BlockSpec((s_len, D), lambda j: (0, 0), pipeline_mode=pl.Buffered(1)),
                  pl.BlockSpec((s_len, tn), lambda j: (0, j))] + [VMEM_SPEC] * (n + nb),
        out_specs=[pl.BlockSpec((1, D, tn), lambda j: (j // per, 0, j % per))] + [VMEM_SPEC] * (n + nb),
        out_shape=([_sds((n_cols // cb, D, cb), BF16)] + [_sds((2,) + s, F32) for s in shapes]
                   + [_sds(bl.shape, F32) for bl in blobs]),
        scratch_shapes=[pltpu.VMEM((2,) + s, F32) for s in shapes] + _reduce_scratch(shapes) + _sum_small_scratch(blobs),
        compiler_params=_params(("arbitrary",)),
    )(a, b, *grads, *blobs)
    return res[0], res[1:1 + n], res[1 + n:]


def _adamw_math(w, g, m, v):
    m = ADAM_B1 * m + (1.0 - ADAM_B1) * g
    v = ADAM_B2 * v + (1.0 - ADAM_B2) * (g * g)
    m_hat = m / (1.0 - ADAM_B1 ** ADAM_STEP)
    v_hat = v / (1.0 - ADAM_B2 ** ADAM_STEP)
    delta = -ADAM_LR * (m_hat / (jnp.sqrt(v_hat) + ADAM_EPS) + ADAM_WD * w)
    return delta, m, v


def _row_tile(r, cols):
    if r * cols * 4 <= 2 ** 20:
        return r
    return next(t for t in (512, 256, 128, 64, 32, 16, 8) if r % t == 0 and t * cols * 4 <= 2 ** 20)


def _adamw(w, g, m, v, name):
    r, cols = w.shape
    tr = _row_tile(r, cols)

    def body(w_ref, g_ref, m_ref, v_ref, go_ref, d_ref, nm_ref, nv_ref):
        g = g_ref[...]
        go_ref[...] = g
        d_ref[...], nm_ref[...], nv_ref[...] = _adamw_math(w_ref[...], g, m_ref[...], v_ref[...])

    spec = pl.BlockSpec((tr, cols), lambda i: (i, 0))
    return pl.pallas_call(
        body, name=name, grid=(r // tr,), in_specs=[spec] * 4, out_specs=[spec] * 4,
        out_shape=[_sds((r, cols), F32)] * 4, compiler_params=_params(("arbitrary",)),
    )(w, g, m, v)


def _adamw_ada(w, ct, dm, m, v):
    r, cols = w.shape
    tr = _row_tile(r, cols)

    def body(w_ref, ct_ref, dm_ref, m_ref, v_ref, g_ref, d_ref, nm_ref, nv_ref):
        g = jnp.dot(ct_ref[...], dm_ref[...], preferred_element_type=F32)
        g_ref[...] = g
        d_ref[...], nm_ref[...], nv_ref[...] = _adamw_math(w_ref[...], g, m_ref[...], v_ref[...])

    spec = pl.BlockSpec((tr, cols), lambda i: (i, 0))
    return pl.pallas_call(
        body, name="adamw_ada", grid=(r // tr,),
        in_specs=[spec, pl.BlockSpec((tr, LANE), lambda i: (i, 0)), pl.BlockSpec((LANE, cols), lambda i: (0, 0)), spec, spec],
        out_specs=[spec] * 4, out_shape=[_sds((r, cols), F32)] * 4, compiler_params=_params(("arbitrary",)),
    )(w, ct, dm, m, v)


BLOB_VEC, BLOB_BSGU, BLOB_CONV, BLOB_ADA, BLOB_DMOD, BLOB_LOSS, BLOB_ROWS = 0, 8, 16, 48, 56, 80, 88
N_VEC = 7


def _adamw_small(tot, g_w_sgu, g_conv, params):
    n = len(params)

    def body(*refs):
        tot_ref, gws_ref, gconv_ref = refs[:3]
        wmv = refs[3:3 + 3 * n]
        outs = refs[3 + 3 * n:]
        grads = [tot_ref[pl.ds(BLOB_VEC + i, 1), :] for i in range(N_VEC)]
        grads += [tot_ref[pl.ds(BLOB_BSGU, HEADS), pl.ds(0, CHUNK)], gconv_ref[...], gws_ref[...], tot_ref[pl.ds(BLOB_ADA, 3), :]]
        for i, g in enumerate(grads):
            w_ref, m_ref, v_ref = wmv[3 * i:3 * i + 3]
            d, nm, nv = _adamw_math(w_ref[...], g, m_ref[...], v_ref[...])
            outs[4 * i][...] = g
            outs[4 * i + 1][...] = d
            outs[4 * i + 2][...] = nm
            outs[4 * i + 3][...] = nv

    flat = [a for wmv in params for a in wmv]
    return pl.pallas_call(
        body, name="adamw_small",
        in_specs=[VMEM_SPEC] * (3 + len(flat)), out_specs=[VMEM_SPEC] * (4 * n),
        out_shape=[_sds(wmv[0].shape, F32) for wmv in params for _ in range(4)],
        compiler_params=_params(),
    )(tot, g_w_sgu, g_conv, *flat)


def _set_rows(buf, row, val):
    return lax.dynamic_update_slice(buf, val.astype(F32), (row, 0))


def kernel(x, c, w_ada, b_ada, g_pre, w_in, conv_w, conv_b, conv_ln_g, conv_ln_b, w_conv_out, sgu_ln_g, sgu_ln_b, w_sgu, b_sgu, w_sgu_out, w_o, g_final, loss_target, m_w_ada, m_b_ada, m_g_pre, m_w_in, m_conv_w, m_conv_b, m_conv_ln_g, m_conv_ln_b, m_w_conv_out, m_sgu_ln_g, m_sgu_ln_b, m_w_sgu, m_b_sgu, m_w_sgu_out, m_w_o, m_g_final, v_w_ada, v_b_ada, v_g_pre, v_w_in, v_conv_w, v_conv_b, v_conv_ln_g, v_conv_ln_b, v_w_conv_out, v_sgu_ln_g, v_sgu_ln_b, v_w_sgu, v_b_sgu, v_w_sgu_out, v_w_o, v_g_final):
    me = _place()
    dev, chip = _dev_of(me), _chip_of(me)
    n_ada = w_ada.shape[2]
    conv_cols = conv_w.shape[2]

    b_ada_s = lax.dynamic_slice(b_ada, (0, chip * n_ada), (1, n_ada))
    c_all, mod_all, cw_all, (wg_in,) = _setup_comm(
        jnp.broadcast_to(c, (SUB, D)), w_ada[0], b_ada_s, jnp.pad(conv_w[0], ((0, HALO - CONV_K), (0, 0))),
        [w_in[0].astype(BF16)])
    mod = lax.dynamic_slice(mod_all, (0, dev * SUB, 0), (N_CHIP, 1, n_ada)).reshape(1, 3 * D)
    shift, scale, gate = mod[:, :D], mod[:, D:2 * D], mod[:, 2 * D:]
    conv_w_full = jnp.swapaxes(cw_all, 0, 1).reshape(HALO, D)[:CONV_K]

    loc = _local_step(x[0], loss_target[0], shift, scale, gate, g_pre, conv_w_full, conv_b, conv_ln_g, conv_ln_b,
                      sgu_ln_g, sgu_ln_b, w_sgu[0], b_sgu[0], g_final.reshape(1, D), wg_in,
                      [w_conv_out[0].astype(BF16), w_sgu_out[0].astype(BF16), w_o[0].astype(BF16)])

    d_mod = jnp.concatenate([loc["d_shift"], loc["d_scale"], loc["d_gate"]], axis=0)
    blob = jnp.zeros((BLOB_ROWS, D), F32)
    for i, name in enumerate(["g_pre", "conv_b", "conv_ln_g", "conv_ln_b", "sgu_ln_g", "sgu_ln_b", "g_final"]):
        blob = _set_rows(blob, BLOB_VEC + i, loc[name])
    blob = _set_rows(blob, BLOB_BSGU, loc["b_sgu"])
    blob = _set_rows(blob, BLOB_CONV, loc["conv_w"])
    blob = _set_rows(blob, BLOB_ADA, d_mod)
    blob = lax.dynamic_update_slice(blob, d_mod, (BLOB_DMOD + 3 * dev, 0))
    blob = _set_rows(blob, BLOB_LOSS, loc["loss_cols"])

    big = ["w_in", "w_conv_out", "w_sgu_out", "w_o"]
    contrib_out = [loc[name].reshape(N_CHIP, 2, D // (2 * N_CHIP), D) for name in big[1:]]
    gw_in, full_out, (tot, g_w_sgu) = _grad_matmul_reduce(
        loc["hb"], loc["dp"], "grad_w_in", contrib_out, [blob, loc["w_sgu"].reshape(HEADS * CHUNK, CHUNK)])
    full_in = _reduce_scatter([gw_in.reshape(N_CHIP, 2, D // 2, gw_in.shape[2])], "reduce_w_in")
    g_big = {name: f.reshape(2 * f.shape[1], f.shape[2]) for name, f in zip(big, list(full_in) + list(full_out))}

    loss = jnp.sum(tot[BLOB_LOSS])
    g_conv_s = lax.dynamic_slice(tot, (BLOB_CONV, chip * conv_cols), (CONV_K, conv_cols))
    d_mod_all = tot[BLOB_DMOD:BLOB_DMOD + 3 * N_DEV].reshape(N_DEV, 3 * D)

    ct = jnp.pad(c_all[::SUB].T, ((0, 0), (0, LANE - N_DEV))).astype(BF16)
    dm = jnp.pad(lax.dynamic_slice(d_mod_all, (0, chip * n_ada), (N_DEV, n_ada)), ((0, LANE - N_DEV), (0, 0))).astype(BF16)
    g_ada, d_ada, nm_ada, nv_ada = _adamw_ada(w_ada[0], ct, dm, m_w_ada[0], v_w_ada[0])

    upd = {}
    for name, w, m, v in [("w_in", w_in, m_w_in, v_w_in), ("w_conv_out", w_conv_out, m_w_conv_out, v_w_conv_out),
                          ("w_sgu_out", w_sgu_out, m_w_sgu_out, v_w_sgu_out), ("w_o", w_o, m_w_o, v_w_o)]:
        upd[name] = _adamw(w[0], g_big[name], m[0], v[0], "adamw_" + name)

    def wmv(w, m, v, shape):
        return tuple(a.reshape(shape) for a in (w, m, v))

    small_params = [wmv(w, m, v, (1, D)) for w, m, v in [
        (g_pre, m_g_pre, v_g_pre), (conv_b, m_conv_b, v_conv_b), (conv_ln_g, m_conv_ln_g, v_conv_ln_g),
        (conv_ln_b, m_conv_ln_b, v_conv_ln_b), (sgu_ln_g, m_sgu_ln_g, v_sgu_ln_g), (sgu_ln_b, m_sgu_ln_b, v_sgu_ln_b),
        (g_final, m_g_final, v_g_final)]]
    small_params += [wmv(b_sgu, m_b_sgu, v_b_sgu, (HEADS, CHUNK)), wmv(conv_w, m_conv_w, v_conv_w, (CONV_K, conv_cols)),
                     wmv(w_sgu, m_w_sgu, v_w_sgu, (HEADS * CHUNK, CHUNK)), wmv(b_ada, m_b_ada, v_b_ada, (3, D))]
    small_out = _adamw_small(tot, g_w_sgu, g_conv_s, small_params)

    def leaves(kind):
        vecs = [small_out[4 * i + kind] for i in range(N_VEC)]
        o_b_sgu, o_conv, o_w_sgu, o_b_ada = (small_out[4 * (N_VEC + i) + kind] for i in range(4))
        ada = (g_ada, d_ada, nm_ada, nv_ada)[kind]
        def bigk(name):
            return upd[name][kind][None]
        return [ada[None], o_b_ada.reshape(1, 3 * D), vecs[0], bigk("w_in"), o_conv[None], vecs[1], vecs[2], vecs[3],
                bigk("w_conv_out"), vecs[4], vecs[5], o_w_sgu.reshape(1, HEADS, CHUNK, CHUNK), o_b_sgu[None],
                bigk("w_sgu_out"), bigk("w_o"), vecs[6].reshape(D)]

    return (loss, loc["grad_x"][None], *leaves(0), *leaves(1), *leaves(2), *leaves(3))
```

```python
import functools

import jax
import jax.numpy as jnp
from jax import lax
from jax.experimental import pallas as pl
from jax.experimental.pallas import tpu as pltpu

F32 = jnp.float32
BF16 = jnp.bfloat16
MESH = pl.DeviceIdType.MESH

D = 1024
N_SEC = 8
N_CHIP = 4
N_DEV = 8
EPS = 1e-6
CONV_K = 31
HALO = 32
CHUNK = 128
HEADS = 8
LANE = 128
SUB = 8
PACK = 16
VMEM_LIMIT = 56 * 1024 * 1024

ADAM_LR, ADAM_B1, ADAM_B2, ADAM_EPS, ADAM_WD, ADAM_STEP = 0.001, 0.9, 0.999, 1e-08, 0.01, 10

_SQRT_HALF = 0.7071067811865476
_INV_SQRT_2PI = 0.3989422804014327


def _sds(shape, dtype):
    return jax.ShapeDtypeStruct(shape, dtype)


def _params(sem=None):
    if sem is None:
        return pltpu.CompilerParams(vmem_limit_bytes=VMEM_LIMIT)
    return pltpu.CompilerParams(dimension_semantics=sem, vmem_limit_bytes=VMEM_LIMIT)


def _strips(n_rows, rows, fn):
    def step(s, carry):
        fn(pl.multiple_of(s * rows, rows))
        return carry
    lax.fori_loop(0, n_rows // rows, step, 0)


def _sigmoid(v):
    return 1.0 / (1.0 + jnp.exp(-v))


def _gelu(v):
    return 0.5 * v * (1.0 + lax.erf(v * _SQRT_HALF))


def _gelu_and_grad(v):
    cdf = 0.5 * (1.0 + lax.erf(v * _SQRT_HALF))
    return v * cdf, cdf + v * jnp.exp(-0.5 * v * v) * _INV_SQRT_2PI


def _dsilu(v, sg):
    return sg * (1.0 + v * (1.0 - sg))


def _rowmean(v):
    return jnp.mean(v, axis=-1, keepdims=True)


def _vec_spec(grid_rank):
    zeros = (0, 0)
    if grid_rank == 1:
        return pl.BlockSpec((1, D), lambda i: zeros)
    return pl.BlockSpec((1, D), lambda i, j: zeros)


def _conv_taps(win_ref, r0, lt, weight_of_offset, rows):
    lanes = pl.ds(lt * LANE, LANE)
    win = win_ref[pl.ds(r0, rows + HALO), lanes]
    n_out = rows // SUB
    acc = [jnp.zeros((SUB, LANE), F32) for _ in range(n_out)]
    for phase in range(SUB):
        offs = [o for o in weight_of_offset if o % SUB == phase]
        if not offs:
            continue
        q_max = max(o // SUB for o in offs)
        span = (n_out + q_max) * SUB
        sh = win[phase:phase + span, :]
        for o in offs:
            q = o // SUB
            w = weight_of_offset[o](lanes)
            for m in range(n_out):
                acc[m] = acc[m] + w * sh[(m + q) * SUB:(m + q + 1) * SUB, :]
    return acc


def _branch_a_fwd(p, conv_wb, conv_b, ln_g, ln_b):
    s_len = p.shape[0]
    tm = min(256, s_len)
    n_i = s_len // tm
    rows = 64

    def body(p_ref, wb_ref, cb_ref, g_ref, b_ref, ya_ref, y1_ref, abuf):
        @pl.when(pl.program_id(0) == 0)
        def _():
            abuf[pl.ds(0, HALO), :] = jnp.zeros((HALO, D), F32)

        def glu(r0):
            val = p_ref[pl.ds(r0, PACK), pl.ds(0, D)].astype(F32)
            gl = p_ref[pl.ds(r0, PACK), pl.ds(D, D)].astype(F32)
            abuf[pl.ds(HALO + r0, PACK), :] = val * _sigmoid(gl)
        _strips(tm, PACK,glu)

        taps = {HALO - (CONV_K - 1) + k: (lambda lanes, k=k: wb_ref[pl.ds(k * SUB, SUB), lanes]) for k in range(CONV_K)}

        def conv(r0):
            for lt in range(D // LANE):
                acc = _conv_taps(abuf, r0, lt, taps, rows)
                cb = cb_ref[:, pl.ds(lt * LANE, LANE)]
                for m, v in enumerate(acc):
                    y1_ref[pl.ds(r0 + m * SUB, SUB), pl.ds(lt * LANE, LANE)] = v + cb
        _strips(tm, rows, conv)

        def norm(r0):
            y1 = y1_ref[pl.ds(r0, PACK), :]
            mu = _rowmean(y1)
            yc = y1 - mu
            rstd = lax.rsqrt(_rowmean(yc * yc) + EPS)
            l1 = (yc * rstd) * g_ref[...] + b_ref[...]
            z = p_ref[pl.ds(r0, PACK), pl.ds(2 * D, D)].astype(F32)
            ya_ref[pl.ds(r0, PACK), :] = ((l1 * _sigmoid(l1)) * (z * _sigmoid(z))).astype(BF16)
        _strips(tm, PACK,norm)

        abuf[pl.ds(0, HALO), :] = abuf[pl.ds(tm, HALO), :]

    return pl.pallas_call(
        body, name="branch_a_fwd", grid=(n_i,),
        in_specs=[pl.BlockSpec((tm, 3 * D), lambda i: (i, 0)),
                  pl.BlockSpec((CONV_K * SUB, D), lambda i: (0, 0)), _vec_spec(1), _vec_spec(1), _vec_spec(1)],
        out_specs=[pl.BlockSpec((tm, D), lambda i: (i, 0)), pl.BlockSpec((tm, D), lambda i: (i, 0))],
        out_shape=[_sds((s_len, D), BF16), _sds((s_len, D), F32)],
        scratch_shapes=[pltpu.VMEM((tm + HALO, D), F32)],
        compiler_params=_params(("arbitrary",)),
    )(p, conv_wb, conv_b, ln_g, ln_b)


def _branch_b_fwd(p, wt, bias_full, ln_g, ln_b):
    s_len = p.shape[0]
    tm = min(256, s_len)
    n_i = s_len // tm

    def body(p_ref, wt_ref, bias_ref, g_ref, b_ref, yb_ref, vb, sbuf):
        def norm(r0):
            gv = _gelu(p_ref[pl.ds(r0, PACK), pl.ds(D, D)].astype(F32))
            mu = _rowmean(gv)
            vc = gv - mu
            rstd = lax.rsqrt(_rowmean(vc * vc) + EPS)
            vb[pl.ds(r0, PACK), :] = ((vc * rstd) * g_ref[...] + b_ref[...]).astype(BF16)
        _strips(tm, PACK,norm)

        for ck in range(tm // CHUNK):
            for h in range(HEADS):
                blk = (pl.ds(ck * CHUNK, CHUNK), pl.ds(h * LANE, LANE))
                sbuf[blk] = jnp.dot(wt_ref[h], vb[blk], preferred_element_type=F32) + bias_ref[:, pl.ds(h * LANE, LANE)]

        def gate(r0):
            u = _gelu(p_ref[pl.ds(r0, PACK), pl.ds(0, D)].astype(F32))
            z = p_ref[pl.ds(r0, PACK), pl.ds(2 * D, D)].astype(F32)
            yb_ref[pl.ds(r0, PACK), :] = (u * sbuf[pl.ds(r0, PACK), :] * (z * _sigmoid(z))).astype(BF16)
        _strips(tm, PACK,gate)

    return pl.pallas_call(
        body, name="branch_b_fwd", grid=(n_i,),
        in_specs=[pl.BlockSpec((tm, 3 * D), lambda i: (i, 1)),
                  pl.BlockSpec((HEADS, CHUNK, CHUNK), lambda i: (0, 0, 0)),
                  pl.BlockSpec((CHUNK, D), lambda i: (0, 0)), _vec_spec(1), _vec_spec(1)],
        out_specs=pl.BlockSpec((tm, D), lambda i: (i, 0)),
        out_shape=_sds((s_len, D), BF16),
        scratch_shapes=[pltpu.VMEM((tm, D), BF16), pltpu.VMEM((tm, D), F32)],
        compiler_params=_params(("arbitrary",)),
    )(p, wt, bias_full, ln_g, ln_b)


def _dot_t(a, b):
    return lax.dot_general(a, b, (((1,), (1,)), ((), ())), preferred_element_type=F32)


def _out_proj(p, ya_in, yb_in, x, target, gate, g_final, w_co, w_so, w_o):
    s_len = x.shape[0]
    tm = min(256, s_len)
    n_i = s_len // tm

    def body(pg_ref, ya_ref, yb_ref, x_ref, t_ref, gate_ref, gf_ref, wco_ref, wso_ref, wo_ref,
             dx2_ref, dya_ref, dyb_ref, dp_ref, mb_ref, dob_ref, dyab_ref, dybb_ref, sums_ref):
        @pl.when(pl.program_id(0) == 0)
        def _():
            sums_ref[...] = jnp.zeros((SUB, D), F32)

        y_a = jnp.dot(ya_ref[...], wco_ref[...], preferred_element_type=F32)
        y_b = jnp.dot(yb_ref[...], wso_ref[...], preferred_element_type=F32)
        ga = _sigmoid(pg_ref[:, pl.ds(0, D)].astype(F32))
        gb = _sigmoid(pg_ref[:, pl.ds(D, D)].astype(F32))
        mb = (ga * y_a + gb * y_b).astype(BF16)
        mb_ref[...] = mb
        o = jnp.dot(mb, wo_ref[...], preferred_element_type=F32)
        x2 = x_ref[...] + gate_ref[...] * o
        r2 = lax.rsqrt(_rowmean(x2 * x2) + EPS)
        xh = x2 * r2
        e = xh * gf_ref[...] - t_ref[...]
        dy = e * (1.0 / D)
        dxh = dy * gf_ref[...]
        dx2 = r2 * (dxh - xh * _rowmean(dxh * xh))
        dx2_ref[...] = dx2
        sums_ref[pl.ds(0, 1), :] += jnp.sum(dy * xh, axis=0, keepdims=True)
        sums_ref[pl.ds(1, 1), :] += jnp.sum(dx2 * o, axis=0, keepdims=True)
        sums_ref[pl.ds(2, 1), :] += jnp.sum(e * e, axis=0, keepdims=True) * (0.5 / D)
        dob = (gate_ref[...] * dx2).astype(BF16)
        dob_ref[...] = dob
        dm = _dot_t(dob, wo_ref[...])
        dy_a = (ga * dm).astype(BF16)
        dy_b = (gb * dm).astype(BF16)
        dyab_ref[...] = dy_a
        dybb_ref[...] = dy_b
        dp_ref[:, pl.ds(0, D)] = (dm * y_a * ga * (1.0 - ga)).astype(BF16)
        dp_ref[:, pl.ds(D, D)] = (dm * y_b * gb * (1.0 - gb)).astype(BF16)
        dya_ref[...] = _dot_t(dy_a, wco_ref[...])
        dyb_ref[...] = _dot_t(dy_b, wso_ref[...])

    tile = pl.BlockSpec((tm, D), lambda i: (i, 0))
    wspec = pl.BlockSpec((D, D), lambda i: (0, 0))
    return pl.pallas_call(
        body, name="out_proj", grid=(n_i,),
        in_specs=[pl.BlockSpec((tm, 2 * D), lambda i: (i, 3)), tile, tile, tile, tile, _vec_spec(1), _vec_spec(1),
                  wspec, wspec, wspec],
        out_specs=[tile, tile, tile, pl.BlockSpec((tm, 2 * D), lambda i: (i, 3)), tile, tile, tile, tile,
                   pl.BlockSpec((SUB, D), lambda i: (0, 0))],
        out_shape=[_sds((s_len, D), F32), _sds((s_len, D), F32), _sds((s_len, D), F32), _sds((s_len, N_SEC * D), BF16),
                   _sds((s_len, D), BF16), _sds((s_len, D), BF16), _sds((s_len, D), BF16), _sds((s_len, D), BF16),
                   _sds((SUB, D), F32)],
        compiler_params=_params(("arbitrary",)),
    )(p, ya_in, yb_in, x, target, gate, g_final, w_co, w_so, w_o)


A_STATS_ROWS = 8 + HALO


def _branch_a_bwd(p, y1, dya_in, dp, conv_wb, ln_g, ln_b):
    s_len = p.shape[0]
    tm = min(256, s_len)
    n_i = s_len // tm
    rows = 64
    n_out = rows // SUB

    def tile_of(i):
        return n_i - 1 - i

    def body(p_ref, y1_ref, dya_ref, dp_in, wb_ref, g_ref, b_ref, dp_ref, st_ref, dybuf, acc8, tapacc):
        del dp_in
        i = pl.program_id(0)

        @pl.when(i == 0)
        def _():
            dybuf[pl.ds(tm, HALO), :] = jnp.zeros((HALO, D), F32)
            st_ref[...] = jnp.zeros((A_STATS_ROWS, D), F32)
            acc8[...] = jnp.zeros((3 * PACK, D), F32)
            tapacc[...] = jnp.zeros((CONV_K * SUB, D), F32)

        def norm_bwd(r0):
            y1 = y1_ref[pl.ds(r0, PACK), :]
            mu = _rowmean(y1)
            yc = y1 - mu
            rstd = lax.rsqrt(_rowmean(yc * yc) + EPS)
            n1 = yc * rstd
            l1 = n1 * g_ref[...] + b_ref[...]
            sg = _sigmoid(l1)
            z = p_ref[pl.ds(r0, PACK), pl.ds(2 * D, D)].astype(F32)
            sz = _sigmoid(z)
            dya = dya_ref[pl.ds(r0, PACK), :]
            dp_ref[pl.ds(r0, PACK), pl.ds(2 * D, D)] = (dya * (l1 * sg) * _dsilu(z, sz)).astype(BF16)
            dl1 = dya * (z * sz) * _dsilu(l1, sg)
            acc8[pl.ds(0, PACK), :] += dl1 * n1
            acc8[pl.ds(PACK, PACK), :] += dl1
            dn1 = dl1 * g_ref[...]
            dy1 = rstd * (dn1 - _rowmean(dn1) - n1 * _rowmean(dn1 * n1))
            acc8[pl.ds(2 * PACK, PACK), :] += dy1
            dybuf[pl.ds(r0, PACK), :] = dy1
        _strips(tm, PACK,norm_bwd)

        def conv_bwd(r0):
            for lt in range(D // LANE):
                lanes = pl.ds(lt * LANE, LANE)
                glanes = pl.ds(D + lt * LANE, LANE)
                win = dybuf[pl.ds(r0, rows + HALO), lanes]
                sg16, a16 = [], []
                for h in range(rows // PACK):
                    rr = pl.ds(r0 + h * PACK, PACK)
                    s = _sigmoid(p_ref[rr, glanes].astype(F32))
                    sg16.append(s)
                    a16.append(p_ref[rr, lanes].astype(F32) * s)
                a = [a16[m // 2][(m % 2) * SUB:(m % 2 + 1) * SUB, :] for m in range(n_out)]
                da = [jnp.zeros((SUB, LANE), F32) for _ in range(n_out)]
                for phase in range(SUB):
                    offs = [o for o in range(CONV_K) if o % SUB == phase]
                    q_max = max(o // SUB for o in offs)
                    sh = win[phase:phase + (n_out + q_max) * SUB, :]
                    for o in offs:
                        k, q = CONV_K - 1 - o, o // SUB
                        w = wb_ref[pl.ds(k * SUB, SUB), lanes]
                        part = None
                        for m in range(n_out):
                            s = sh[(m + q) * SUB:(m + q + 1) * SUB, :]
                            da[m] = da[m] + w * s
                            part = a[m] * s if part is None else part + a[m] * s
                        tapacc[pl.ds(k * SUB, SUB), lanes] += part
                for h in range(rows // PACK):
                    rr = pl.ds(r0 + h * PACK, PACK)
                    da16 = jnp.concatenate(da[2 * h:2 * h + 2], axis=0)
                    dp_ref[rr, lanes] = (da16 * sg16[h]).astype(BF16)
                    dp_ref[rr, glanes] = (da16 * a16[h] * (1.0 - sg16[h])).astype(BF16)
        _strips(tm, rows, conv_bwd)

        dybuf[pl.ds(tm, HALO), :] = dybuf[pl.ds(0, HALO), :]

        @pl.when(i == n_i - 1)
        def _():
            for j in range(3):
                st_ref[pl.ds(j, 1), :] = jnp.sum(acc8[pl.ds(j * PACK, PACK), :], axis=0, keepdims=True)
            for k in range(CONV_K):
                st_ref[pl.ds(SUB + k, 1), :] = jnp.sum(tapacc[pl.ds(k * SUB, SUB), :], axis=0, keepdims=True)

    return pl.pallas_call(
        body, name="branch_a_bwd", grid=(n_i,),
        in_specs=[pl.BlockSpec((tm, 3 * D), lambda i: (tile_of(i), 0)),
                  pl.BlockSpec((tm, D), lambda i: (tile_of(i), 0)),
                  pl.BlockSpec((tm, D), lambda i: (tile_of(i), 0)),
                  pl.BlockSpec(memory_space=pl.ANY),
                  pl.BlockSpec((CONV_K * SUB, D), lambda i: (0, 0)), _vec_spec(1), _vec_spec(1)],
        out_specs=[pl.BlockSpec((tm, 3 * D), lambda i: (tile_of(i), 0)),
                   pl.BlockSpec((A_STATS_ROWS, D), lambda i: (0, 0))],
        out_shape=[_sds(dp.shape, BF16), _sds((A_STATS_ROWS, D), F32)],
        scratch_shapes=[pltpu.VMEM((tm + HALO, D), F32), pltpu.VMEM((3 * PACK, D), F32), pltpu.VMEM((CONV_K * SUB, D), F32)],
        input_output_aliases={3: 0},
        compiler_params=_params(("arbitrary",)),
    )(p, y1, dya_in, dp, conv_wb, ln_g, ln_b)


def _branch_b_bwd(p, dyb_in, dp, wt, wtt, bias_full, ln_g, ln_b):
    s_len = p.shape[0]
    tm = min(256, s_len)
    n_i = s_len // tm

    def body(p_ref, dyb_ref, dp_in, wt_ref, wtt_ref, bias_ref, g_ref, b_ref, dp_ref, st_ref, gbt_ref, gw_ref,
             vb, n2buf, rstdbuf, sbuf, dsb, dvbuf, acc8, gb_ref, dgbuf):
        del dp_in
        i = pl.program_id(0)

        @pl.when(i == 0)
        def _():
            st_ref[...] = jnp.zeros((SUB, D), F32)
            gbt_ref[...] = jnp.zeros((CHUNK, LANE), F32)
            gb_ref[...] = jnp.zeros((CHUNK, D), F32)
            gw_ref[...] = jnp.zeros((HEADS, CHUNK, CHUNK), F32)
            acc8[...] = jnp.zeros((2 * PACK, D), F32)

        def norm(r0):
            gv, dgv = _gelu_and_grad(p_ref[pl.ds(r0, PACK), pl.ds(D, D)].astype(F32))
            dgbuf[pl.ds(r0, PACK), :] = dgv
            mu = _rowmean(gv)
            vc = gv - mu
            rstd = lax.rsqrt(_rowmean(vc * vc) + EPS)
            n2 = vc * rstd
            n2buf[pl.ds(r0, PACK), :] = n2
            rstdbuf[pl.ds(r0, PACK), :] = jnp.broadcast_to(rstd, (PACK, LANE))
            vb[pl.ds(r0, PACK), :] = (n2 * g_ref[...] + b_ref[...]).astype(BF16)
        _strips(tm, PACK,norm)

        for ck in range(tm // CHUNK):
            for h in range(HEADS):
                blk = (pl.ds(ck * CHUNK, CHUNK), pl.ds(h * LANE, LANE))
                sbuf[blk] = jnp.dot(wt_ref[h], vb[blk], preferred_element_type=F32) + bias_ref[:, pl.ds(h * LANE, LANE)]

        def gate_bwd(r0):
            pu = p_ref[pl.ds(r0, PACK), pl.ds(0, D)].astype(F32)
            u, du = _gelu_and_grad(pu)
            z = p_ref[pl.ds(r0, PACK), pl.ds(2 * D, D)].astype(F32)
            sg = _sigmoid(z)
            sz = z * sg
            s = sbuf[pl.ds(r0, PACK), :]
            dyb = dyb_ref[pl.ds(r0, PACK), :]
            ds = dyb * u * sz
            dsb[pl.ds(r0, PACK), :] = ds.astype(BF16)
            gb_ref[pl.ds(pl.multiple_of(r0 % CHUNK, PACK), PACK), :] += ds
            dp_ref[pl.ds(r0, PACK), pl.ds(0, D)] = (dyb * s * sz * du).astype(BF16)
            dp_ref[pl.ds(r0, PACK), pl.ds(2 * D, D)] = (dyb * u * s * _dsilu(z, sg)).astype(BF16)
        _strips(tm, PACK,gate_bwd)

        for ck in range(tm // CHUNK):
            for h in range(HEADS):
                blk = (pl.ds(ck * CHUNK, CHUNK), pl.ds(h * LANE, LANE))
                d_s = dsb[blk]
                dvbuf[blk] = jnp.dot(wtt_ref[h], d_s, preferred_element_type=F32)
                gw_ref[h] += _dot_t(d_s, vb[blk])

        def norm_bwd(r0):
            dv = dvbuf[pl.ds(r0, PACK), :]
            n2 = n2buf[pl.ds(r0, PACK), :]
            rstd = rstdbuf[pl.ds(r0, PACK), pl.ds(0, 1)]
            acc8[pl.ds(0, PACK), :] += dv * n2
            acc8[pl.ds(PACK, PACK), :] += dv
            dn2 = dv * g_ref[...]
            dgv = rstd * (dn2 - _rowmean(dn2) - n2 * _rowmean(dn2 * n2))
            dp_ref[pl.ds(r0, PACK), pl.ds(D, D)] = (dgv * dgbuf[pl.ds(r0, PACK), :]).astype(BF16)
        _strips(tm, PACK,norm_bwd)

        @pl.when(i == n_i - 1)
        def _():
            for j in range(2):
                st_ref[pl.ds(j, 1), :] = jnp.sum(acc8[pl.ds(j * PACK, PACK), :], axis=0, keepdims=True)
            row = lax.broadcasted_iota(jnp.int32, (CHUNK, CHUNK), 0)
            col = lax.broadcasted_iota(jnp.int32, (CHUNK, CHUNK), 1)
            for h in range(HEADS):
                gw_ref[h] = jnp.where(row >= col, gw_ref[h], 0.0)
            lane = lax.broadcasted_iota(jnp.int32, (CHUNK, LANE), 1)
            gbt = jnp.zeros((CHUNK, LANE), F32)
            for h in range(HEADS):
                gbt = jnp.where(lane == h, jnp.sum(gb_ref[:, pl.ds(h * LANE, LANE)], axis=1, keepdims=True), gbt)
            gbt_ref[...] = gbt

    wspec = pl.BlockSpec((HEADS, CHUNK, CHUNK), lambda i: (0, 0, 0))
    return pl.pallas_call(
        body, name="branch_b_bwd", grid=(n_i,),
        in_specs=[pl.BlockSpec((tm, 3 * D), lambda i: (i, 1)), pl.BlockSpec((tm, D), lambda i: (i, 0)),
                  pl.BlockSpec(memory_space=pl.ANY), wspec, wspec,
                  pl.BlockSpec((CHUNK, D), lambda i: (0, 0)), _vec_spec(1), _vec_spec(1)],
        out_specs=[pl.BlockSpec((tm, 3 * D), lambda i: (i, 1)), pl.BlockSpec((SUB, D), lambda i: (0, 0)),
                   pl.BlockSpec((CHUNK, LANE), lambda i: (0, 0)), wspec],
        out_shape=[_sds(dp.shape, BF16), _sds((SUB, D), F32), _sds((CHUNK, LANE), F32), _sds((HEADS, CHUNK, CHUNK), F32)],
        scratch_shapes=[pltpu.VMEM((tm, D), BF16), pltpu.VMEM((tm, D), F32), pltpu.VMEM((tm, LANE), F32),
                        pltpu.VMEM((tm, D), F32), pltpu.VMEM((tm, D), BF16), pltpu.VMEM((tm, D), F32),
                        pltpu.VMEM((2 * PACK, D), F32), pltpu.VMEM((CHUNK, D), F32), pltpu.VMEM((tm, D), F32)],
        input_output_aliases={2: 0},
        compiler_params=_params(("arbitrary",)),
    )(p, dyb_in, dp, wt, wtt, bias_full, ln_g, ln_b)


def _in_proj_bwd(dp, wg_in, x, dx2, shift, scale, g_pre):
    del shift
    s_len = x.shape[0]
    tm = min(512, s_len)
    n_i = s_len // tm
    wn = wg_in.shape[2]

    def body(dp0, dp1, dp2, dp3, w_ref, x_ref, dx2_ref, sc_ref, g_ref, gx_ref, st_ref, acc, acc8):
        i = pl.program_id(0)

        @pl.when(i == 0)
        def _():
            st_ref[...] = jnp.zeros((SUB, D), F32)
            acc8[...] = jnp.zeros((3 * PACK, D), F32)

        dh = _dot_t(dp0[...], w_ref[0])
        for j, dp_ref in enumerate((dp1, dp2, dp3), start=1):
            dh = dh + _dot_t(dp_ref[...], w_ref[j])
        acc[...] = dh

        def strip(r0):
            xs = x_ref[pl.ds(r0, PACK), :]
            r = lax.rsqrt(_rowmean(xs * xs) + EPS)
            xn = xs * r
            dhs = acc[pl.ds(r0, PACK), :]
            acc8[pl.ds(0, PACK), :] += dhs
            acc8[pl.ds(PACK, PACK), :] += dhs * (xn * g_ref[...])
            dhp = dhs * (1.0 + sc_ref[...])
            acc8[pl.ds(2 * PACK, PACK), :] += dhp * xn
            dxn = dhp * g_ref[...]
            gx_ref[pl.ds(r0, PACK), :] = dx2_ref[pl.ds(r0, PACK), :] + r * (dxn - xn * _rowmean(dxn * xn))
        _strips(tm, PACK, strip)

        @pl.when(i == n_i - 1)
        def _():
            for k in range(3):
                st_ref[pl.ds(k, 1), :] = jnp.sum(acc8[pl.ds(k * PACK, PACK), :], axis=0, keepdims=True)

    tile = pl.BlockSpec((tm, D), lambda i: (i, 0))
    return pl.pallas_call(
        body, name="in_proj_bwd", grid=(n_i,),
        in_specs=[pl.BlockSpec((tm, wn), functools.partial(lambda j, i: (i, j), j)) for j in range(N_CHIP)] + [
                  pl.BlockSpec((N_CHIP, D, wn), lambda i: (0, 0, 0), pipeline_mode=pl.Buffered(1)),
                  tile, tile, _vec_spec(1), _vec_spec(1)],
        out_specs=[tile, pl.BlockSpec((SUB, D), lambda i: (0, 0))],
        out_shape=[_sds((s_len, D), F32), _sds((SUB, D), F32)],
        scratch_shapes=[pltpu.VMEM((tm, D), F32), pltpu.VMEM((3 * PACK, D), F32)],
        compiler_params=_params(("arbitrary",)),
    )(dp, dp, dp, dp, wg_in, x, dx2, scale, g_pre)


def _grad_matmul(a, b, name):
    s_len, n = b.shape
    cb = min(2 * D, n)
    tn = 512
    per = cb // tn

    def body(a_ref, b_ref, ob_ref):
        ob_ref[0] = lax.dot_general(a_ref[...], b_ref[...], (((0,), (0,)), ((), ())),
                                    preferred_element_type=F32).astype(BF16)

    return pl.pallas_call(
        body, name=name, grid=(n // tn,),
        in_specs=[pl.BlockSpec((s_len, D), lambda j: (0, 0), pipeline_mode=pl.Buffered(1)),
                  pl.BlockSpec((s_len, tn), lambda j: (0, j))],
        out_specs=pl.BlockSpec((1, D, tn), lambda j: (j // per, 0, j % per)),
        out_shape=_sds((n // cb, D, cb), BF16),
        compiler_params=_params(("arbitrary",)),
    )(a, b)


def _local_step(x, target, shift, scale, gate, g_pre, conv_w_full, conv_b, conv_ln_g, conv_ln_b,
                sgu_ln_g, sgu_ln_b, w_sgu, b_sgu, g_final, wg_in, out_shards):
    conv_wb = jnp.repeat(conv_w_full, SUB, axis=0)
    causal = jnp.tril(jnp.ones((CHUNK, CHUNK), dtype=bool))
    wt = jnp.where(causal[None], w_sgu, 0.0).astype(BF16)
    wtt = jnp.swapaxes(wt, 1, 2)
    bias_full = jnp.repeat(b_sgu.T, LANE, axis=1)

    p, hb, gathered = _in_proj_gather(x, shift, scale, g_pre, wg_in, out_shards)
    w_co, w_so, w_o = (g.reshape(D, D) for g in gathered)
    ya_in, y1 = _branch_a_fwd(p, conv_wb, conv_b, conv_ln_g, conv_ln_b)
    yb_in = _branch_b_fwd(p, wt, bias_full, sgu_ln_g, sgu_ln_b)
    dx2, dya_in, dyb_in, dp, mb, dob, dyab, dybb, sums_o = _out_proj(
        p, ya_in, yb_in, x, target, gate, g_final, w_co, w_so, w_o)
    dp, st_a = _branch_a_bwd(p, y1, dya_in, dp, conv_wb, conv_ln_g, conv_ln_b)
    dp, st_b, gbt, gws = _branch_b_bwd(p, dyb_in, dp, wt, wtt, bias_full, sgu_ln_g, sgu_ln_b)
    grad_x, st_i = _in_proj_bwd(dp, wg_in, x, dx2, shift, scale, g_pre)
    gw_o = _grad_matmul(mb, dob, "grad_w_o")
    gw_co = _grad_matmul(ya_in, dyab, "grad_w_conv_out")
    gw_so = _grad_matmul(yb_in, dybb, "grad_w_sgu_out")
    return dict(
        grad_x=grad_x, loss_cols=sums_o[2:3], g_final=sums_o[0:1], d_gate=sums_o[1:2],
        d_shift=st_i[0:1], d_scale=st_i[1:2], g_pre=st_i[2:3],
        conv_ln_g=st_a[0:1], conv_ln_b=st_a[1:2], conv_b=st_a[2:3], conv_w=st_a[SUB:SUB + CONV_K],
        sgu_ln_g=st_b[0:1], sgu_ln_b=st_b[1:2], b_sgu=gbt[:, :HEADS].T, w_sgu=gws,
        hb=hb, dp=dp, w_o=gw_o, w_conv_out=gw_co, w_sgu_out=gw_so)


ANY_SPEC = pl.BlockSpec(memory_space=pl.ANY)
VMEM_SPEC = pl.BlockSpec(memory_space=pltpu.VMEM)


def _place():
    return lax.axis_index("x"), lax.axis_index("y"), lax.axis_index("c")


def _peer(k):
    x, y, c = _place()
    return (1 - x if k & 4 else x, 1 - y if k & 2 else y, 1 - c if k & 1 else c)


def _dev_of(p):
    return 4 * p[0] + 2 * p[1] + p[2]


def _chip_of(p):
    return 2 * p[0] + p[1]


def _rdma(src, dst, send_sem, recv_sem, to):
    return pltpu.make_async_remote_copy(src_ref=src, dst_ref=dst, send_sem=send_sem, recv_sem=recv_sem,
                                        device_id=to, device_id_type=MESH)


CHIP_PEERS = (2, 4, 6)
ALL_PEERS = tuple(range(1, N_DEV))
SIBLING = 1


def _setup_comm(c8, w_ada_s, b_ada_s, convw_s, shards):
    n_mod = w_ada_s.shape[1]
    rows = SUB * N_DEV
    n = len(shards)
    parts = 4

    def body(c8_ref, wada_ref, bada_ref, cw_ref, *refs):
        ins, (call_ref, mod_ref, cwall_ref), outs = refs[:n], refs[n:n + 3], refs[n + 3:2 * n + 3]
        csend, crecv, wsend, wrecv, msend, mrecv = refs[2 * n + 3:2 * n + 9]
        ins = [r.at[:, pl.ds(k * (s.shape[1] // parts), s.shape[1] // parts)] for r, s in zip(ins, shards) for k in range(parts)]
        outs = [r.at[:, :, pl.ds(k * (s.shape[1] // parts), s.shape[1] // parts)] for r, s in zip(outs, shards) for k in range(parts)]
        gather_a, gather_b, gather_c = _gather_phases([s.shape[0] for s in shards for _ in range(parts)], ins, outs, refs[2 * n + 9:])
        me = _place()
        dev, chip = _dev_of(me), _chip_of(me)

        def c_rows(d):
            return call_ref.at[pl.ds(pl.multiple_of(d * SUB, SUB), SUB), :]

        call_ref[pl.ds(pl.multiple_of(dev * SUB, SUB), SUB), :] = c8_ref[...]
        cwall_ref[chip] = cw_ref[...]
        c_out = [_rdma(c8_ref, c_rows(dev), csend.at[k], crecv.at[k], _peer(k)) for k in ALL_PEERS]
        w_out = [_rdma(cw_ref, cwall_ref.at[chip], wsend.at[k], wrecv.at[k], _peer(k)) for k in CHIP_PEERS]
        for cp in c_out + w_out:
            cp.start()
        for k in ALL_PEERS:
            _rdma(c8_ref, c_rows(_dev_of(_peer(k))), csend.at[k], crecv.at[k], _peer(k)).wait_recv()
        part = jnp.dot(call_ref[...].astype(BF16), wada_ref[...].astype(BF16), preferred_element_type=F32) + bada_ref[...]
        mod_ref[chip] = part
        m_out = [_rdma(mod_ref.at[chip], mod_ref.at[chip], msend.at[k], mrecv.at[k], _peer(k)) for k in CHIP_PEERS]
        for cp in m_out:
            cp.start()
        gather_a()
        for k in CHIP_PEERS:
            pc = _chip_of(_peer(k))
            _rdma(cw_ref, cwall_ref.at[pc], wsend.at[k], wrecv.at[k], _peer(k)).wait_recv()
            _rdma(mod_ref.at[pc], mod_ref.at[pc], msend.at[k], mrecv.at[k], _peer(k)).wait_recv()
        for cp in c_out + w_out + m_out:
            cp.wait_send()
        gather_b()
        gather_c()

    res = pl.pallas_call(
        body, name="setup_comm",
        in_specs=[VMEM_SPEC] * (4 + n), out_specs=[VMEM_SPEC] * (3 + n),
        out_shape=([_sds((rows, D), F32), _sds((N_CHIP, rows, n_mod), F32), _sds((N_CHIP,) + convw_s.shape, F32)]
                   + [_sds((N_CHIP,) + s.shape, s.dtype) for s in shards]),
        scratch_shapes=([pltpu.SemaphoreType.DMA((N_DEV,))] * 6 + [pltpu.SemaphoreType.DMA((n * parts,))]
                        + [pltpu.SemaphoreType.DMA((n * parts, len(CHIP_PEERS)))] * 4),
        compiler_params=_params(),
    )(c8, w_ada_s, b_ada_s, convw_s, *shards)
    return res[0], res[1], res[2], res[3:]


def _gather_phases(row_counts, ins, dsts, sems):
    n = len(row_counts)
    lsem, isend, irecv, dsend, drecv = sems
    me = _place()
    chip, c = _chip_of(me), me[2]

    def half(t, which):
        hr = row_counts[t] // 2
        return pl.ds(pl.multiple_of(which * hr, hr), hr)

    def local(t):
        return pltpu.make_async_copy(ins[t], dsts[t].at[chip], lsem.at[t])

    def to_chip(t, j):
        return _rdma(ins[t].at[half(t, c)], dsts[t].at[chip, half(t, c)], isend.at[t, j], irecv.at[t, j], _peer(CHIP_PEERS[j]))

    def landed(t, j, which):
        return dsts[t].at[_chip_of(_peer(CHIP_PEERS[j])), half(t, which)]

    def to_sibling(t, j):
        return _rdma(landed(t, j, c), landed(t, j, c), dsend.at[t, j], drecv.at[t, j], _peer(SIBLING))

    pairs = [(t, j) for t in range(n) for j in range(len(CHIP_PEERS))]

    def phase_a():
        for t in range(n):
            local(t).start()
        for t, j in pairs:
            to_chip(t, j).start()

    def phase_b():
        for t, j in pairs:
            _rdma(landed(t, j, c), landed(t, j, c), isend.at[t, j], irecv.at[t, j], _peer(CHIP_PEERS[j])).wait_recv()
            to_sibling(t, j).start()

    def phase_c():
        for t, j in pairs:
            _rdma(landed(t, j, 1 - c), landed(t, j, 1 - c), dsend.at[t, j], drecv.at[t, j], _peer(SIBLING)).wait_recv()
        for t, j in pairs:
            to_chip(t, j).wait_send()
            to_sibling(t, j).wait_send()
        for t in range(n):
            local(t).wait()

    return phase_a, phase_b, phase_c


def _in_proj_gather(x, shift, scale, g_pre, wg_in, shards):
    s_len = x.shape[0]
    tm = min(256, s_len)
    n_i = s_len // tm
    wn = wg_in.shape[2]
    n = len(shards)

    def body(x_ref, sh_ref, sc_ref, g_ref, w_ref, *refs):
        ins, p_ref, hb_ref, outs = refs[:n], refs[n], refs[n + 1], refs[n + 2:2 * n + 2]
        gath, sems = refs[2 * n + 2:3 * n + 2], refs[3 * n + 2:]
        phases = _gather_phases([s.shape[0] for s in shards], ins, gath, sems)
        i = pl.program_id(0)
        for step, phase in zip((0, n_i // 2, n_i - 1), phases):
            pl.when(i == step)(phase)

        @pl.when(i == n_i - 1)
        def _():
            for t in range(n):
                outs[t][...] = gath[t][...]

        def strip(r0):
            xs = x_ref[pl.ds(r0, PACK), :]
            r = lax.rsqrt(_rowmean(xs * xs) + EPS)
            h = (xs * r) * g_ref[...] * (1.0 + sc_ref[...]) + sh_ref[...]
            hb_ref[pl.ds(r0, PACK), :] = h.astype(BF16)
        _strips(tm, PACK, strip)
        hb = hb_ref[...]
        for j in range(N_CHIP):
            p_ref[:, pl.ds(j * wn, wn)] = jnp.dot(hb, w_ref[j], preferred_element_type=F32).astype(BF16)

    res = pl.pallas_call(
        body, name="in_proj", grid=(n_i,),
        in_specs=[pl.BlockSpec((tm, D), lambda i: (i, 0)), _vec_spec(1), _vec_spec(1), _vec_spec(1),
                  pl.BlockSpec((N_CHIP, D, wn), lambda i: (0, 0, 0), pipeline_mode=pl.Buffered(1))] + [VMEM_SPEC] * n,
        out_specs=[pl.BlockSpec((tm, N_CHIP * wn), lambda i: (i, 0)), pl.BlockSpec((tm, D), lambda i: (i, 0))] + [VMEM_SPEC] * n,
        out_shape=([_sds((s_len, N_SEC * D), BF16), _sds((s_len, D), BF16)]
                   + [_sds((N_CHIP,) + s.shape, s.dtype) for s in shards]),
        scratch_shapes=([pltpu.VMEM((N_CHIP,) + s.shape, s.dtype) for s in shards]
                        + [pltpu.SemaphoreType.DMA((n,))] + [pltpu.SemaphoreType.DMA((n, len(CHIP_PEERS)))] * 4),
        compiler_params=_params(("arbitrary",)),
    )(x, shift, scale, g_pre, wg_in, *shards)
    return res[0], res[1], res[2:]


def _reduce_scatter(grads, name):
    n = len(grads)
    shapes = [g.shape[2:] for g in grads]
    parts = 4
    part_shapes = [(r, cols // parts) for r, cols in shapes for _ in range(parts)]

    def body(*refs):
        def halves(group, lead):
            return [ref.at[(slice(None),) * lead + (pl.ds(k * (s[1] // parts), s[1] // parts),)]
                    for ref, s in zip(group, shapes) for k in range(parts)]
        ins, outs = halves(refs[:n], 3), halves(refs[n:2 * n], 2)
        pbufs, rbufs, accs = halves(refs[2 * n:3 * n], 2), halves(refs[3 * n:4 * n], 2), halves(refs[4 * n:5 * n], 1)
        for phase in _reduce_phases(part_shapes, ins, outs, pbufs, rbufs, accs, refs[5 * n:]):
            phase()

    return pl.pallas_call(
        body, name=name,
        in_specs=[VMEM_SPEC] * n, out_specs=[VMEM_SPEC] * n,
        out_shape=[_sds((2,) + s, F32) for s in shapes],
        scratch_shapes=_reduce_scratch(shapes)[:3 * n] + _reduce_scratch(part_shapes)[3 * parts * n:],
        compiler_params=_params(),
    )(*grads)


def _reduce_phases(shapes, ins, outs, pbufs, rbufs, accs, sems):
    n = len(shapes)
    psend, precv, csend, crecv, fsend, frecv = sems
    me = _place()
    chip, c = _chip_of(me), me[2]
    sib = _peer(SIBLING)

    def to_sibling(t, d):
        return _rdma(ins[t].at[d, 1 - c], pbufs[t].at[d], psend.at[t, d], precv.at[t, d], sib)

    def to_chip(t, j):
        return _rdma(pbufs[t].at[jnp.bitwise_xor(chip, j)], rbufs[t].at[j - 1], csend.at[t, j], crecv.at[t, j], _peer(2 * j))

    def finished(t):
        return _rdma(outs[t].at[c], outs[t].at[c], fsend.at[t], frecv.at[t], sib)

    def phase_a():
        for j in (1, 2, 3, 0):
            for t in range(n):
                to_sibling(t, jnp.bitwise_xor(chip, j)).start()

    def phase_b():
        for j in (1, 2, 3, 0):
            d = jnp.bitwise_xor(chip, j)
            for t in range(n):
                to_sibling(t, d).wait_recv()

                def pair_sum(r0, t=t, d=d, j=j):
                    rows = pl.ds(r0, PACK)
                    s = ins[t][d, c, rows, :].astype(F32) + pbufs[t][d, rows, :].astype(F32)
                    if j == 0:
                        accs[t][rows, :] = s
                    else:
                        pbufs[t][d, rows, :] = s.astype(BF16)
                _strips(shapes[t][0], PACK, pair_sum)
                if j:
                    to_chip(t, j).start()

    def phase_c():
        for t in range(n):
            for j in (1, 2, 3):
                blk = rbufs[t].at[j - 1]
                _rdma(blk, blk, csend.at[t, j], crecv.at[t, j], _peer(2 * j)).wait_recv()

            def total(r0, t=t):
                rows = pl.ds(r0, PACK)
                s = accs[t][rows, :] + rbufs[t][0, rows, :].astype(F32)
                s = s + rbufs[t][1, rows, :].astype(F32)
                outs[t][c, rows, :] = s + rbufs[t][2, rows, :].astype(F32)
            _strips(shapes[t][0], PACK, total)
            finished(t).start()

    def phase_d():
        for t in range(n):
            blk = outs[t].at[1 - c]
            _rdma(blk, blk, fsend.at[t], frecv.at[t], sib).wait_recv()
        for t in range(n):
            for d in range(N_CHIP):
                to_sibling(t, d).wait_send()
            for j in (1, 2, 3):
                to_chip(t, j).wait_send()
            finished(t).wait_send()

    return phase_a, phase_b, phase_c, phase_d


def _sum_small_phases(ins, outs, pbufs, buf4s, sems):
    n = len(ins)
    psend, precv, send, recv = sems
    chip = _chip_of(_place())

    def swap(t):
        return _rdma(ins[t], pbufs[t], psend.at[t], precv.at[t], _peer(SIBLING))

    def to_chip(t, k):
        return _rdma(buf4s[t].at[chip], buf4s[t].at[chip], send.at[t, k], recv.at[t, k], _peer(k))

    def phase_a():
        for t in range(n):
            swap(t).start()

    def phase_b():
        for t in range(n):
            swap(t).wait()
            buf4s[t][chip] = ins[t][...] + pbufs[t][...]
            for k in CHIP_PEERS:
                to_chip(t, k).start()

    def phase_c():
        for t in range(n):
            for k in CHIP_PEERS:
                blk = buf4s[t].at[_chip_of(_peer(k))]
                _rdma(blk, blk, send.at[t, k], recv.at[t, k], _peer(k)).wait_recv()
            outs[t][...] = (buf4s[t][0] + buf4s[t][1]) + (buf4s[t][2] + buf4s[t][3])

    def phase_d():
        for t in range(n):
            for k in CHIP_PEERS:
                to_chip(t, k).wait_send()

    return phase_a, phase_b, phase_c, phase_d


def _sum_small_scratch(blobs):
    n = len(blobs)
    return ([pltpu.VMEM(b.shape, F32) for b in blobs] + [pltpu.VMEM((N_CHIP,) + b.shape, F32) for b in blobs]
            + [pltpu.SemaphoreType.DMA((n,))] * 2 + [pltpu.SemaphoreType.DMA((n, N_DEV))] * 2)


def _reduce_scratch(shapes):
    n = len(shapes)
    return ([pltpu.VMEM((N_CHIP,) + s, BF16) for s in shapes] + [pltpu.VMEM((N_CHIP - 1,) + s, BF16) for s in shapes]
            + [pltpu.VMEM(s, F32) for s in shapes]
            + [pltpu.SemaphoreType.DMA((n, N_CHIP))] * 4 + [pltpu.SemaphoreType.DMA((n,))] * 2)


def _grad_matmul_reduce(a, b, name, grads, blobs):
    s_len, n_cols = b.shape
    cb = min(2 * D, n_cols)
    tn = 512
    per = cb // tn
    steps = n_cols // tn
    n, nb = len(grads), len(blobs)
    shapes = [g.shape[2:] for g in grads]
    n_red = len(_reduce_scratch(shapes))

    def body(a_ref, b_ref, *refs):
        ins, bins = refs[:n], refs[n:n + nb]
        ob_ref, outs, bouts = refs[n + nb], refs[n + nb + 1:2 * n + nb + 1], refs[2 * n + nb + 1:2 * (n + nb) + 1]
        scratch = refs[2 * (n + nb) + 1:]
        fulls, red, small = scratch[:n], scratch[n:n + n_red], scratch[n + n_red:]
        phases = _reduce_phases(shapes, ins, fulls, red[:n], red[n:2 * n], red[2 * n:3 * n], red[3 * n:])
        small_phases = _sum_small_phases(bins, bouts, small[:nb], small[nb:2 * nb], small[2 * nb:])
        j = pl.program_id(0)
        for step, phase in zip((0, 2, steps - 2, steps - 1), phases):
            pl.when(j == step)(phase)
        for step, phase in zip((1, 3, steps - 2, steps - 1), small_phases):
            pl.when(j == step)(phase)

        @pl.when(j == steps - 1)
        def _():
            for t in range(n):
                outs[t][...] = fulls[t][...]
        ob_ref[0] = lax.dot_general(a_ref[...], b_ref[...], (((0,), (0,)), ((), ())),
                                    preferred_element_type=F32).astype(BF16)

    res = pl.pallas_call(
        body, name=name, grid=(steps,),
        in_specs=[pl.BlockSpec((s_len, D), lambda j: (0, 0), pipeline_mode=pl.Buffered(1)),
                  pl.BlockSpec((s_len, tn), lambda j: (0, j))] + [VMEM_SPEC] * (n + nb),
        out_specs=[pl.BlockSpec((1, D, tn), lambda j: (j // per, 0, j % per))] + [VMEM_SPEC] * (n + nb),
        out_shape=([_sds((n_cols // cb, D, cb), BF16)] + [_sds((2,) + s, F32) for s in shapes]
                   + [_sds(bl.shape, F32) for bl in blobs]),
        scratch_shapes=[pltpu.VMEM((2,) + s, F32) for s in shapes] + _reduce_scratch(shapes) + _sum_small_scratch(blobs),
        compiler_params=_params(("arbitrary",)),
    )(a, b, *grads, *blobs)
    return res[0], res[1:1 + n], res[1 + n:]


def _adamw_math(w, g, m, v):
    m = ADAM_B1 * m + (1.0 - ADAM_B1) * g
    v = ADAM_B2 * v + (1.0 - ADAM_B2) * (g * g)
    m_hat = m / (1.0 - ADAM_B1 ** ADAM_STEP)
    v_hat = v / (1.0 - ADAM_B2 ** ADAM_STEP)
    delta = -ADAM_LR * (m_hat / (jnp.sqrt(v_hat) + ADAM_EPS) + ADAM_WD * w)
    return delta, m, v


def _row_tile(r, cols):
    if r * cols * 4 <= 2 ** 20:
        return r
    return next(t for t in (512, 256, 128, 64, 32, 16, 8) if r % t == 0 and t * cols * 4 <= 2 ** 20)


def _adamw(w, g, m, v, name):
    r, cols = w.shape
    tr = _row_tile(r, cols)

    def body(w_ref, g_ref, m_ref, v_ref, go_ref, d_ref, nm_ref, nv_ref):
        g = g_ref[...]
        go_ref[...] = g
        d_ref[...], nm_ref[...], nv_ref[...] = _adamw_math(w_ref[...], g, m_ref[...], v_ref[...])

    spec = pl.BlockSpec((tr, cols), lambda i: (i, 0))
    return pl.pallas_call(
        body, name=name, grid=(r // tr,), in_specs=[spec] * 4, out_specs=[spec] * 4,
        out_shape=[_sds((r, cols), F32)] * 4, compiler_params=_params(("arbitrary",)),
    )(w, g, m, v)


def _adamw_ada(w, ct, dm, m, v):
    r, cols = w.shape
    tr = _row_tile(r, cols)

    def body(w_ref, ct_ref, dm_ref, m_ref, v_ref, g_ref, d_ref, nm_ref, nv_ref):
        g = jnp.dot(ct_ref[...], dm_ref[...], preferred_element_type=F32)
        g_ref[...] = g
        d_ref[...], nm_ref[...], nv_ref[...] = _adamw_math(w_ref[...], g, m_ref[...], v_ref[...])

    spec = pl.BlockSpec((tr, cols), lambda i: (i, 0))
    return pl.pallas_call(
        body, name="adamw_ada", grid=(r // tr,),
        in_specs=[spec, pl.BlockSpec((tr, LANE), lambda i: (i, 0)), pl.BlockSpec((LANE, cols), lambda i: (0, 0)), spec, spec],
        out_specs=[spec] * 4, out_shape=[_sds((r, cols), F32)] * 4, compiler_params=_params(("arbitrary",)),
    )(w, ct, dm, m, v)


BLOB_VEC, BLOB_BSGU, BLOB_CONV, BLOB_ADA, BLOB_DMOD, BLOB_LOSS, BLOB_ROWS = 0, 8, 16, 48, 56, 80, 88
N_VEC = 7


def _adamw_small(tot, g_w_sgu, g_conv, params):
    n = len(params)

    def body(*refs):
        tot_ref, gws_ref, gconv_ref = refs[:3]
        wmv = refs[3:3 + 3 * n]
        outs = refs[3 + 3 * n:]
        grads = [tot_ref[pl.ds(BLOB_VEC + i, 1), :] for i in range(N_VEC)]
        grads += [tot_ref[pl.ds(BLOB_BSGU, HEADS), pl.ds(0, CHUNK)], gconv_ref[...], gws_ref[...], tot_ref[pl.ds(BLOB_ADA, 3), :]]
        for i, g in enumerate(grads):
            w_ref, m_ref, v_ref = wmv[3 * i:3 * i + 3]
            d, nm, nv = _adamw_math(w_ref[...], g, m_ref[...], v_ref[...])
            outs[4 * i][...] = g
            outs[4 * i + 1][...] = d
            outs[4 * i + 2][...] = nm
            outs[4 * i + 3][...] = nv

    flat = [a for wmv in params for a in wmv]
    return pl.pallas_call(
        body, name="adamw_small",
        in_specs=[VMEM_SPEC] * (3 + len(flat)), out_specs=[VMEM_SPEC] * (4 * n),
        out_shape=[_sds(wmv[0].shape, F32) for wmv in params for _ in range(4)],
        compiler_params=_params(),
    )(tot, g_w_sgu, g_conv, *flat)


def _set_rows(buf, row, val):
    return lax.dynamic_update_slice(buf, val.astype(F32), (row, 0))


def kernel(x, c, w_ada, b_ada, g_pre, w_in, conv_w, conv_b, conv_ln_g, conv_ln_b, w_conv_out, sgu_ln_g, sgu_ln_b, w_sgu, b_sgu, w_sgu_out, w_o, g_final, loss_target, m_w_ada, m_b_ada, m_g_pre, m_w_in, m_conv_w, m_conv_b, m_conv_ln_g, m_conv_ln_b, m_w_conv_out, m_sgu_ln_g, m_sgu_ln_b, m_w_sgu, m_b_sgu, m_w_sgu_out, m_w_o, m_g_final, v_w_ada, v_b_ada, v_g_pre, v_w_in, v_conv_w, v_conv_b, v_conv_ln_g, v_conv_ln_b, v_w_conv_out, v_sgu_ln_g, v_sgu_ln_b, v_w_sgu, v_b_sgu, v_w_sgu_out, v_w_o, v_g_final):
    me = _place()
    dev, chip = _dev_of(me), _chip_of(me)
    n_ada = w_ada.shape[2]
    conv_cols = conv_w.shape[2]

    b_ada_s = lax.dynamic_slice(b_ada, (0, chip * n_ada), (1, n_ada))
    c_all, mod_all, cw_all, (wg_in,) = _setup_comm(
        jnp.broadcast_to(c, (SUB, D)), w_ada[0], b_ada_s, jnp.pad(conv_w[0], ((0, HALO - CONV_K), (0, 0))),
        [w_in[0].astype(BF16)])
    mod = lax.dynamic_slice(mod_all, (0, dev * SUB, 0), (N_CHIP, 1, n_ada)).reshape(1, 3 * D)
    shift, scale, gate = mod[:, :D], mod[:, D:2 * D], mod[:, 2 * D:]
    conv_w_full = jnp.swapaxes(cw_all, 0, 1).reshape(HALO, D)[:CONV_K]

    loc = _local_step(x[0], loss_target[0], shift, scale, gate, g_pre, conv_w_full, conv_b, conv_ln_g, conv_ln_b,
                      sgu_ln_g, sgu_ln_b, w_sgu[0], b_sgu[0], g_final.reshape(1, D), wg_in,
                      [w_conv_out[0].astype(BF16), w_sgu_out[0].astype(BF16), w_o[0].astype(BF16)])

    d_mod = jnp.concatenate([loc["d_shift"], loc["d_scale"], loc["d_gate"]], axis=0)
    blob = jnp.zeros((BLOB_ROWS, D), F32)
    for i, name in enumerate(["g_pre", "conv_b", "conv_ln_g", "conv_ln_b", "sgu_ln_g", "sgu_ln_b", "g_final"]):
        blob = _set_rows(blob, BLOB_VEC + i, loc[name])
    blob = _set_rows(blob, BLOB_BSGU, loc["b_sgu"])
    blob = _set_rows(blob, BLOB_CONV, loc["conv_w"])
    blob = _set_rows(blob, BLOB_ADA, d_mod)
    blob = lax.dynamic_update_slice(blob, d_mod, (BLOB_DMOD + 3 * dev, 0))
    blob = _set_rows(blob, BLOB_LOSS, loc["loss_cols"])

    big = ["w_in", "w_conv_out", "w_sgu_out", "w_o"]
    contrib_out = [loc[name].reshape(N_CHIP, 2, D // (2 * N_CHIP), D) for name in big[1:]]
    gw_in, full_out, (tot, g_w_sgu) = _grad_matmul_reduce(
        loc["hb"], loc["dp"], "grad_w_in", contrib_out, [blob, loc["w_sgu"].reshape(HEADS * CHUNK, CHUNK)])
    full_in = _reduce_scatter([gw_in.reshape(N_CHIP, 2, D // 2, gw_in.shape[2])], "reduce_w_in")
    g_big = {name: f.reshape(2 * f.shape[1], f.shape[2]) for name, f in zip(big, list(full_in) + list(full_out))}

    loss = jnp.sum(tot[BLOB_LOSS])
    g_conv_s = lax.dynamic_slice(tot, (BLOB_CONV, chip * conv_cols), (CONV_K, conv_cols))
    d_mod_all = tot[BLOB_DMOD:BLOB_DMOD + 3 * N_DEV].reshape(N_DEV, 3 * D)

    ct = jnp.pad(c_all[::SUB].T, ((0, 0), (0, LANE - N_DEV))).astype(BF16)
    dm = jnp.pad(lax.dynamic_slice(d_mod_all, (0, chip * n_ada), (N_DEV, n_ada)), ((0, LANE - N_DEV), (0, 0))).astype(BF16)
    g_ada, d_ada, nm_ada, nv_ada = _adamw_ada(w_ada[0], ct, dm, m_w_ada[0], v_w_ada[0])

    upd = {}
    for name, w, m, v in [("w_in", w_in, m_w_in, v_w_in), ("w_conv_out", w_conv_out, m_w_conv_out, v_w_conv_out),
                          ("w_sgu_out", w_sgu_out, m_w_sgu_out, v_w_sgu_out), ("w_o", w_o, m_w_o, v_w_o)]:
        upd[name] = _adamw(w[0], g_big[name], m[0], v[0], "adamw_" + name)

    def wmv(w, m, v, shape):
        return tuple(a.reshape(shape) for a in (w, m, v))

    small_params = [wmv(w, m, v, (1, D)) for w, m, v in [
        (g_pre, m_g_pre, v_g_pre), (conv_b, m_conv_b, v_conv_b), (conv_ln_g, m_conv_ln_g, v_conv_ln_g),
        (conv_ln_b, m_conv_ln_b, v_conv_ln_b), (sgu_ln_g, m_sgu_ln_g, v_sgu_ln_g), (sgu_ln_b, m_sgu_ln_b, v_sgu_ln_b),
        (g_final, m_g_final, v_g_final)]]
    small_params += [wmv(b_sgu, m_b_sgu, v_b_sgu, (HEADS, CHUNK)), wmv(conv_w, m_conv_w, v_conv_w, (CONV_K, conv_cols)),
                     wmv(w_sgu, m_w_sgu, v_w_sgu, (HEADS * CHUNK, CHUNK)), wmv(b_ada, m_b_ada, v_b_ada, (3, D))]
    small_out = _adamw_small(tot, g_w_sgu, g_conv_s, small_params)

    def leaves(kind):
        vecs = [small_out[4 * i + kind] for i in range(N_VEC)]
        o_b_sgu, o_conv, o_w_sgu, o_b_ada = (small_out[4 * (N_VEC + i) + kind] for i in range(4))
        ada = (g_ada, d_ada, nm_ada, nv_ada)[kind]
        def bigk(name):
            return upd[name][kind][None]
        return [ada[None], o_b_ada.reshape(1, 3 * D), vecs[0], bigk("w_in"), o_conv[None], vecs[1], vecs[2], vecs[3],
                bigk("w_conv_out"), vecs[4], vecs[5], o_w_sgu.reshape(1, HEADS, CHUNK, CHUNK), o_b_sgu[None],
                bigk("w_sgu_out"), bigk("w_o"), vecs[6].reshape(D)]

    return (loss, loc["grad_x"][None], *leaves(0), *leaves(1), *leaves(2), *leaves(3))
```

```python
import functools

import jax
import jax.numpy as jnp
from jax import lax
from jax.experimental import pallas as pl
from jax.experimental.pallas import tpu as pltpu

F32 = jnp.float32
BF16 = jnp.bfloat16
MESH = pl.DeviceIdType.MESH

D = 1024
N_SEC = 8
N_CHIP = 4
N_DEV = 8
EPS = 1e-6
CONV_K = 31
HALO = 32
CHUNK = 128
HEADS = 8
LANE = 128
SUB = 8
PACK = 16
VMEM_LIMIT = 56 * 1024 * 1024

ADAM_LR, ADAM_B1, ADAM_B2, ADAM_EPS, ADAM_WD, ADAM_STEP = 0.001, 0.9, 0.999, 1e-08, 0.01, 10

_SQRT_HALF = 0.7071067811865476
_INV_SQRT_2PI = 0.3989422804014327


def _sds(shape, dtype):
    return jax.ShapeDtypeStruct(shape, dtype)


def _params(sem=None):
    if sem is None:
        return pltpu.CompilerParams(vmem_limit_bytes=VMEM_LIMIT)
    return pltpu.CompilerParams(dimension_semantics=sem, vmem_limit_bytes=VMEM_LIMIT)


def _strips(n_rows, rows, fn):
    def step(s, carry):
        fn(pl.multiple_of(s * rows, rows))
        return carry
    lax.fori_loop(0, n_rows // rows, step, 0)


def _sigmoid(v):
    return 1.0 / (1.0 + jnp.exp(-v))


def _gelu(v):
    return 0.5 * v * (1.0 + lax.erf(v * _SQRT_HALF))


def _gelu_and_grad(v):
    cdf = 0.5 * (1.0 + lax.erf(v * _SQRT_HALF))
    return v * cdf, cdf + v * jnp.exp(-0.5 * v * v) * _INV_SQRT_2PI


def _dsilu(v, sg):
    return sg * (1.0 + v * (1.0 - sg))


def _rowmean(v):
    return jnp.mean(v, axis=-1, keepdims=True)


def _vec_spec(grid_rank):
    zeros = (0, 0)
    if grid_rank == 1:
        return pl.BlockSpec((1, D), lambda i: zeros)
    return pl.BlockSpec((1, D), lambda i, j: zeros)


def _conv_taps(win_ref, r0, lt, weight_of_offset, rows):
    lanes = pl.ds(lt * LANE, LANE)
    win = win_ref[pl.ds(r0, rows + HALO), lanes]
    n_out = rows // SUB
    acc = [jnp.zeros((SUB, LANE), F32) for _ in range(n_out)]
    for phase in range(SUB):
        offs = [o for o in weight_of_offset if o % SUB == phase]
        if not offs:
            continue
        q_max = max(o // SUB for o in offs)
        span = (n_out + q_max) * SUB
        sh = win[phase:phase + span, :]
        for o in offs:
            q = o // SUB
            w = weight_of_offset[o](lanes)
            for m in range(n_out):
                acc[m] = acc[m] + w * sh[(m + q) * SUB:(m + q + 1) * SUB, :]
    return acc


def _branch_a_fwd(p, conv_wb, conv_b, ln_g, ln_b):
    s_len = p.shape[0]
    tm = min(256, s_len)
    n_i = s_len // tm
    rows = 64

    def body(p_ref, wb_ref, cb_ref, g_ref, b_ref, ya_ref, y1_ref, abuf):
        @pl.when(pl.program_id(0) == 0)
        def _():
            abuf[pl.ds(0, HALO), :] = jnp.zeros((HALO, D), F32)

        def glu(r0):
            val = p_ref[pl.ds(r0, PACK), pl.ds(0, D)].astype(F32)
            gl = p_ref[pl.ds(r0, PACK), pl.ds(D, D)].astype(F32)
            abuf[pl.ds(HALO + r0, PACK), :] = val * _sigmoid(gl)
        _strips(tm, PACK,glu)

        taps = {HALO - (CONV_K - 1) + k: (lambda lanes, k=k: wb_ref[pl.ds(k * SUB, SUB), lanes]) for k in range(CONV_K)}

        def conv(r0):
            for lt in range(D // LANE):
                acc = _conv_taps(abuf, r0, lt, taps, rows)
                cb = cb_ref[:, pl.ds(lt * LANE, LANE)]
                for m, v in enumerate(acc):
                    y1_ref[pl.ds(r0 + m * SUB, SUB), pl.ds(lt * LANE, LANE)] = v + cb
        _strips(tm, rows, conv)

        def norm(r0):
            y1 = y1_ref[pl.ds(r0, PACK), :]
            mu = _rowmean(y1)
            yc = y1 - mu
            rstd = lax.rsqrt(_rowmean(yc * yc) + EPS)
            l1 = (yc * rstd) * g_ref[...] + b_ref[...]
            z = p_ref[pl.ds(r0, PACK), pl.ds(2 * D, D)].astype(F32)
            ya_ref[pl.ds(r0, PACK), :] = ((l1 * _sigmoid(l1)) * (z * _sigmoid(z))).astype(BF16)
        _strips(tm, PACK,norm)

        abuf[pl.ds(0, HALO), :] = abuf[pl.ds(tm, HALO), :]

    return pl.pallas_call(
        body, name="branch_a_fwd", grid=(n_i,),
        in_specs=[pl.BlockSpec((tm, 3 * D), lambda i: (i, 0)),
                  pl.BlockSpec((CONV_K * SUB, D), lambda i: (0, 0)), _vec_spec(1), _vec_spec(1), _vec_spec(1)],
        out_specs=[pl.BlockSpec((tm, D), lambda i: (i, 0)), pl.BlockSpec((tm, D), lambda i: (i, 0))],
        out_shape=[_sds((s_len, D), BF16), _sds((s_len, D), F32)],
        scratch_shapes=[pltpu.VMEM((tm + HALO, D), F32)],
        compiler_params=_params(("arbitrary",)),
    )(p, conv_wb, conv_b, ln_g, ln_b)


def _branch_b_fwd(p, wt, bias_full, ln_g, ln_b):
    s_len = p.shape[0]
    tm = min(256, s_len)
    n_i = s_len // tm

    def body(p_ref, wt_ref, bias_ref, g_ref, b_ref, yb_ref, vb, sbuf):
        def norm(r0):
            gv = _gelu(p_ref[pl.ds(r0, PACK), pl.ds(D, D)].astype(F32))
            mu = _rowmean(gv)
            vc = gv - mu
            rstd = lax.rsqrt(_rowmean(vc * vc) + EPS)
            vb[pl.ds(r0, PACK), :] = ((vc * rstd) * g_ref[...] + b_ref[...]).astype(BF16)
        _strips(tm, PACK,norm)

        for ck in range(tm // CHUNK):
            for h in range(HEADS):
                blk = (pl.ds(ck * CHUNK, CHUNK), pl.ds(h * LANE, LANE))
                sbuf[blk] = jnp.dot(wt_ref[h], vb[blk], preferred_element_type=F32) + bias_ref[:, pl.ds(h * LANE, LANE)]

        def gate(r0):
            u = _gelu(p_ref[pl.ds(r0, PACK), pl.ds(0, D)].astype(F32))
            z = p_ref[pl.ds(r0, PACK), pl.ds(2 * D, D)].astype(F32)
            yb_ref[pl.ds(r0, PACK), :] = (u * sbuf[pl.ds(r0, PACK), :] * (z * _sigmoid(z))).astype(BF16)
        _strips(tm, PACK,gate)

    return pl.pallas_call(
        body, name="branch_b_fwd", grid=(n_i,),
        in_specs=[pl.BlockSpec((tm, 3 * D), lambda i: (i, 1)),
                  pl.BlockSpec((HEADS, CHUNK, CHUNK), lambda i: (0, 0, 0)),
                  pl.BlockSpec((CHUNK, D), lambda i: (0, 0)), _vec_spec(1), _vec_spec(1)],
        out_specs=pl.BlockSpec((tm, D), lambda i: (i, 0)),
        out_shape=_sds((s_len, D), BF16),
        scratch_shapes=[pltpu.VMEM((tm, D), BF16), pltpu.VMEM((tm, D), F32)],
        compiler_params=_params(("arbitrary",)),
    )(p, wt, bias_full, ln_g, ln_b)


def _dot_t(a, b):
    return lax.dot_general(a, b, (((1,), (1,)), ((), ())), preferred_element_type=F32)


def _out_proj(p, ya_in, yb_in, x, target, gate, g_final, w_co, w_so, w_o):
    s_len = x.shape[0]
    tm = min(256, s_len)
    n_i = s_len // tm

    def body(pg_ref, ya_ref, yb_ref, x_ref, t_ref, gate_ref, gf_ref, wco_ref, wso_ref, wo_ref,
             dx2_ref, dya_ref, dyb_ref, dp_ref, mb_ref, dob_ref, dyab_ref, dybb_ref, sums_ref):
        @pl.when(pl.program_id(0) == 0)
        def _():
            sums_ref[...] = jnp.zeros((SUB, D), F32)

        y_a = jnp.dot(ya_ref[...], wco_ref[...], preferred_element_type=F32)
        y_b = jnp.dot(yb_ref[...], wso_ref[...], preferred_element_type=F32)
        ga = _sigmoid(pg_ref[:, pl.ds(0, D)].astype(F32))
        gb = _sigmoid(pg_ref[:, pl.ds(D, D)].astype(F32))
        mb = (ga * y_a + gb * y_b).astype(BF16)
        mb_ref[...] = mb
        o = jnp.dot(mb, wo_ref[...], preferred_element_type=F32)
        x2 = x_ref[...] + gate_ref[...] * o
        r2 = lax.rsqrt(_rowmean(x2 * x2) + EPS)
        xh = x2 * r2
        e = xh * gf_ref[...] - t_ref[...]
        dy = e * (1.0 / D)
        dxh = dy * gf_ref[...]
        dx2 = r2 * (dxh - xh * _rowmean(dxh * xh))
        dx2_ref[...] = dx2
        sums_ref[pl.ds(0, 1), :] += jnp.sum(dy * xh, axis=0, keepdims=True)
        sums_ref[pl.ds(1, 1), :] += jnp.sum(dx2 * o, axis=0, keepdims=True)
        sums_ref[pl.ds(2, 1), :] += jnp.sum(e * e, axis=0, keepdims=True) * (0.5 / D)
        dob = (gate_ref[...] * dx2).astype(BF16)
        dob_ref[...] = dob
        dm = _dot_t(dob, wo_ref[...])
        dy_a = (ga * dm).astype(BF16)
        dy_b = (gb * dm).astype(BF16)
        dyab_ref[...] = dy_a
        dybb_ref[...] = dy_b
        dp_ref[:, pl.ds(0, D)] = (dm * y_a * ga * (1.0 - ga)).astype(BF16)
        dp_ref[:, pl.ds(D, D)] = (dm * y_b * gb * (1.0 - gb)).astype(BF16)
        dya_ref[...] = _dot_t(dy_a, wco_ref[...])
        dyb_ref[...] = _dot_t(dy_b, wso_ref[...])

    tile = pl.BlockSpec((tm, D), lambda i: (i, 0))
    wspec = pl.BlockSpec((D, D), lambda i: (0, 0))
    return pl.pallas_call(
        body, name="out_proj", grid=(n_i,),
        in_specs=[pl.BlockSpec((tm, 2 * D), lambda i: (i, 3)), tile, tile, tile, tile, _vec_spec(1), _vec_spec(1),
                  wspec, wspec, wspec],
        out_specs=[tile, tile, tile, pl.BlockSpec((tm, 2 * D), lambda i: (i, 3)), tile, tile, tile, tile,
                   pl.BlockSpec((SUB, D), lambda i: (0, 0))],
        out_shape=[_sds((s_len, D), F32), _sds((s_len, D), F32), _sds((s_len, D), F32), _sds((s_len, N_SEC * D), BF16),
                   _sds((s_len, D), BF16), _sds((s_len, D), BF16), _sds((s_len, D), BF16), _sds((s_len, D), BF16),
                   _sds((SUB, D), F32)],
        compiler_params=_params(("arbitrary",)),
    )(p, ya_in, yb_in, x, target, gate, g_final, w_co, w_so, w_o)


A_STATS_ROWS = 8 + HALO


def _branch_a_bwd(p, y1, dya_in, dp, conv_wb, ln_g, ln_b):
    s_len = p.shape[0]
    tm = min(256, s_len)
    n_i = s_len // tm
    rows = 64
    n_out = rows // SUB

    def tile_of(i):
        return n_i - 1 - i

    def body(p_ref, y1_ref, dya_ref, dp_in, wb_ref, g_ref, b_ref, dp_ref, st_ref, dybuf, acc8, tapacc):
        del dp_in
        i = pl.program_id(0)

        @pl.when(i == 0)
        def _():
            dybuf[pl.ds(tm, HALO), :] = jnp.zeros((HALO, D), F32)
            st_ref[...] = jnp.zeros((A_STATS_ROWS, D), F32)
            acc8[...] = jnp.zeros((3 * PACK, D), F32)
            tapacc[...] = jnp.zeros((CONV_K * SUB, D), F32)

        def norm_bwd(r0):
            y1 = y1_ref[pl.ds(r0, PACK), :]
            mu = _rowmean(y1)
            yc = y1 - mu
            rstd = lax.rsqrt(_rowmean(yc * yc) + EPS)
            n1 = yc * rstd
            l1 = n1 * g_ref[...] + b_ref[...]
            sg = _sigmoid(l1)
            z = p_ref[pl.ds(r0, PACK), pl.ds(2 * D, D)].astype(F32)
            sz = _sigmoid(z)
            dya = dya_ref[pl.ds(r0, PACK), :]
            dp_ref[pl.ds(r0, PACK), pl.ds(2 * D, D)] = (dya * (l1 * sg) * _dsilu(z, sz)).astype(BF16)
            dl1 = dya * (z * sz) * _dsilu(l1, sg)
            acc8[pl.ds(0, PACK), :] += dl1 * n1
            acc8[pl.ds(PACK, PACK), :] += dl1
            dn1 = dl1 * g_ref[...]
            dy1 = rstd * (dn1 - _rowmean(dn1) - n1 * _rowmean(dn1 * n1))
            acc8[pl.ds(2 * PACK, PACK), :] += dy1
            dybuf[pl.ds(r0, PACK), :] = dy1
        _strips(tm, PACK,norm_bwd)

        def conv_bwd(r0):
            for lt in range(D // LANE):
                lanes = pl.ds(lt * LANE, LANE)
                glanes = pl.ds(D + lt * LANE, LANE)
                win = dybuf[pl.ds(r0, rows + HALO), lanes]
                sg16, a16 = [], []
                for h in range(rows // PACK):
                    rr = pl.ds(r0 + h * PACK, PACK)
                    s = _sigmoid(p_ref[rr, glanes].astype(F32))
                    sg16.append(s)
                    a16.append(p_ref[rr, lanes].astype(F32) * s)
                a = [a16[m // 2][(m % 2) * SUB:(m % 2 + 1) * SUB, :] for m in range(n_out)]
                da = [jnp.zeros((SUB, LANE), F32) for _ in range(n_out)]
                for phase in range(SUB):
                    offs = [o for o in range(CONV_K) if o % SUB == phase]
                    q_max = max(o // SUB for o in offs)
                    sh = win[phase:phase + (n_out + q_max) * SUB, :]
                    for o in offs:
                        k, q = CONV_K - 1 - o, o // SUB
                        w = wb_ref[pl.ds(k * SUB, SUB), lanes]
                        part = None
                        for m in range(n_out):
                            s = sh[(m + q) * SUB:(m + q + 1) * SUB, :]
                            da[m] = da[m] + w * s
                            part = a[m] * s if part is None else part + a[m] * s
                        tapacc[pl.ds(k * SUB, SUB), lanes] += part
                for h in range(rows // PACK):
                    rr = pl.ds(r0 + h * PACK, PACK)
                    da16 = jnp.concatenate(da[2 * h:2 * h + 2], axis=0)
                    dp_ref[rr, lanes] = (da16 * sg16[h]).astype(BF16)
                    dp_ref[rr, glanes] = (da16 * a16[h] * (1.0 - sg16[h])).astype(BF16)
        _strips(tm, rows, conv_bwd)

        dybuf[pl.ds(tm, HALO), :] = dybuf[pl.ds(0, HALO), :]

        @pl.when(i == n_i - 1)
        def _():
            for j in range(3):
                st_ref[pl.ds(j, 1), :] = jnp.sum(acc8[pl.ds(j * PACK, PACK), :], axis=0, keepdims=True)
            for k in range(CONV_K):
                st_ref[pl.ds(SUB + k, 1), :] = jnp.sum(tapacc[pl.ds(k * SUB, SUB), :], axis=0, keepdims=True)

    return pl.pallas_call(
        body, name="branch_a_bwd", grid=(n_i,),
        in_specs=[pl.BlockSpec((tm, 3 * D), lambda i: (tile_of(i), 0)),
                  pl.BlockSpec((tm, D), lambda i: (tile_of(i), 0)),
                  pl.BlockSpec((tm, D), lambda i: (tile_of(i), 0)),
                  pl.BlockSpec(memory_space=pl.ANY),
                  pl.BlockSpec((CONV_K * SUB, D), lambda i: (0, 0)), _vec_spec(1), _vec_spec(1)],
        out_specs=[pl.BlockSpec((tm, 3 * D), lambda i: (tile_of(i), 0)),
                   pl.BlockSpec((A_STATS_ROWS, D), lambda i: (0, 0))],
        out_shape=[_sds(dp.shape, BF16), _sds((A_STATS_ROWS, D), F32)],
        scratch_shapes=[pltpu.VMEM((tm + HALO, D), F32), pltpu.VMEM((3 * PACK, D), F32), pltpu.VMEM((CONV_K * SUB, D), F32)],
        input_output_aliases={3: 0},
        compiler_params=_params(("arbitrary",)),
    )(p, y1, dya_in, dp, conv_wb, ln_g, ln_b)


def _branch_b_bwd(p, dyb_in, dp, wt, wtt, bias_full, ln_g, ln_b):
    s_len = p.shape[0]
    tm = min(256, s_len)
    n_i = s_len // tm

    def body(p_ref, dyb_ref, dp_in, wt_ref, wtt_ref, bias_ref, g_ref, b_ref, dp_ref, st_ref, gbt_ref, gw_ref,
             vb, n2buf, rstdbuf, sbuf, dsb, dvbuf, acc8, gb_ref, dgbuf):
        del dp_in
        i = pl.program_id(0)

        @pl.when(i == 0)
        def _():
            st_ref[...] = jnp.zeros((SUB, D), F32)
            gbt_ref[...] = jnp.zeros((CHUNK, LANE), F32)
            gb_ref[...] = jnp.zeros((CHUNK, D), F32)
            gw_ref[...] = jnp.zeros((HEADS, CHUNK, CHUNK), F32)
            acc8[...] = jnp.zeros((2 * PACK, D), F32)

        def norm(r0):
            gv, dgv = _gelu_and_grad(p_ref[pl.ds(r0, PACK), pl.ds(D, D)].astype(F32))
            dgbuf[pl.ds(r0, PACK), :] = dgv
            mu = _rowmean(gv)
            vc = gv - mu
            rstd = lax.rsqrt(_rowmean(vc * vc) + EPS)
            n2 = vc * rstd
            n2buf[pl.ds(r0, PACK), :] = n2
            rstdbuf[pl.ds(r0, PACK), :] = jnp.broadcast_to(rstd, (PACK, LANE))
            vb[pl.ds(r0, PACK), :] = (n2 * g_ref[...] + b_ref[...]).astype(BF16)
        _strips(tm, PACK,norm)

        for ck in range(tm // CHUNK):
            for h in range(HEADS):
                blk = (pl.ds(ck * CHUNK, CHUNK), pl.ds(h * LANE, LANE))
                sbuf[blk] = jnp.dot(wt_ref[h], vb[blk], preferred_element_type=F32) + bias_ref[:, pl.ds(h * LANE, LANE)]

        def gate_bwd(r0):
            pu = p_ref[pl.ds(r0, PACK), pl.ds(0, D)].astype(F32)
            u, du = _gelu_and_grad(pu)
            z = p_ref[pl.ds(r0, PACK), pl.ds(2 * D, D)].astype(F32)
            sg = _sigmoid(z)
            sz = z * sg
            s = sbuf[pl.ds(r0, PACK), :]
            dyb = dyb_ref[pl.ds(r0, PACK), :]
            ds = dyb * u * sz
            dsb[pl.ds(r0, PACK), :] = ds.astype(BF16)
            gb_ref[pl.ds(pl.multiple_of(r0 % CHUNK, PACK), PACK), :] += ds
            dp_ref[pl.ds(r0, PACK), pl.ds(0, D)] = (dyb * s * sz * du).astype(BF16)
            dp_ref[pl.ds(r0, PACK), pl.ds(2 * D, D)] = (dyb * u * s * _dsilu(z, sg)).astype(BF16)
        _strips(tm, PACK,gate_bwd)

        for ck in range(tm // CHUNK):
            for h in range(HEADS):
                blk = (pl.ds(ck * CHUNK, CHUNK), pl.ds(h * LANE, LANE))
                d_s = dsb[blk]
                dvbuf[blk] = jnp.dot(wtt_ref[h], d_s, preferred_element_type=F32)
                gw_ref[h] += _dot_t(d_s, vb[blk])

        def norm_bwd(r0):
            dv = dvbuf[pl.ds(r0, PACK), :]
            n2 = n2buf[pl.ds(r0, PACK), :]
            rstd = rstdbuf[pl.ds(r0, PACK), pl.ds(0, 1)]
            acc8[pl.ds(0, PACK), :] += dv * n2
            acc8[pl.ds(PACK, PACK), :] += dv
            dn2 = dv * g_ref[...]
            dgv = rstd * (dn2 - _rowmean(dn2) - n2 * _rowmean(dn2 * n2))
            dp_ref[pl.ds(r0, PACK), pl.ds(D, D)] = (dgv * dgbuf[pl.ds(r0, PACK), :]).astype(BF16)
        _strips(tm, PACK,norm_bwd)

        @pl.when(i == n_i - 1)
        def _():
            for j in range(2):
                st_ref[pl.ds(j, 1), :] = jnp.sum(acc8[pl.ds(j * PACK, PACK), :], axis=0, keepdims=True)
            row = lax.broadcasted_iota(jnp.int32, (CHUNK, CHUNK), 0)
            col = lax.broadcasted_iota(jnp.int32, (CHUNK, CHUNK), 1)
            for h in range(HEADS):
                gw_ref[h] = jnp.where(row >= col, gw_ref[h], 0.0)
            lane = lax.broadcasted_iota(jnp.int32, (CHUNK, LANE), 1)
            gbt = jnp.zeros((CHUNK, LANE), F32)
            for h in range(HEADS):
                gbt = jnp.where(lane == h, jnp.sum(gb_ref[:, pl.ds(h * LANE, LANE)], axis=1, keepdims=True), gbt)
            gbt_ref[...] = gbt

    wspec = pl.BlockSpec((HEADS, CHUNK, CHUNK), lambda i: (0, 0, 0))
    return pl.pallas_call(
        body, name="branch_b_bwd", grid=(n_i,),
        in_specs=[pl.BlockSpec((tm, 3 * D), lambda i: (i, 1)), pl.BlockSpec((tm, D), lambda i: (i, 0)),
                  pl.BlockSpec(memory_space=pl.ANY), wspec, wspec,
                  pl.BlockSpec((CHUNK, D), lambda i: (0, 0)), _vec_spec(1), _vec_spec(1)],
        out_specs=[pl.BlockSpec((tm, 3 * D), lambda i: (i, 1)), pl.BlockSpec((SUB, D), lambda i: (0, 0)),
                   pl.BlockSpec((CHUNK, LANE), lambda i: (0, 0)), wspec],
        out_shape=[_sds(dp.shape, BF16), _sds((SUB, D), F32), _sds((CHUNK, LANE), F32), _sds((HEADS, CHUNK, CHUNK), F32)],
        scratch_shapes=[pltpu.VMEM((tm, D), BF16), pltpu.VMEM((tm, D), F32), pltpu.VMEM((tm, LANE), F32),
                        pltpu.VMEM((tm, D), F32), pltpu.VMEM((tm, D), BF16), pltpu.VMEM((tm, D), F32),
                        pltpu.VMEM((2 * PACK, D), F32), pltpu.VMEM((CHUNK, D), F32), pltpu.VMEM((tm, D), F32)],
        input_output_aliases={2: 0},
        compiler_params=_params(("arbitrary",)),
    )(p, dyb_in, dp, wt, wtt, bias_full, ln_g, ln_b)


def _in_proj_bwd(dp, wg_in, x, dx2, shift, scale, g_pre):
    del shift
    s_len = x.shape[0]
    tm = min(512, s_len)
    n_i = s_len // tm
    wn = wg_in.shape[1]

    def body(dp0, dp1, dp2, dp3, w_ref, x_ref, dx2_ref, sc_ref, g_ref, gx_ref, st_ref, acc, acc8):
        i = pl.program_id(0)

        @pl.when(i == 0)
        def _():
            st_ref[...] = jnp.zeros((SUB, D), F32)
            acc8[...] = jnp.zeros((3 * PACK, D), F32)

        dh = jnp.dot(dp0[...], w_ref[0], preferred_element_type=F32)
        for j, dp_ref in enumerate((dp1, dp2, dp3), start=1):
            dh = dh + jnp.dot(dp_ref[...], w_ref[j], preferred_element_type=F32)
        acc[...] = dh

        def strip(r0):
            xs = x_ref[pl.ds(r0, PACK), :]
            r = lax.rsqrt(_rowmean(xs * xs) + EPS)
            xn = xs * r
            dhs = acc[pl.ds(r0, PACK), :]
            acc8[pl.ds(0, PACK), :] += dhs
            acc8[pl.ds(PACK, PACK), :] += dhs * (xn * g_ref[...])
            dhp = dhs * (1.0 + sc_ref[...])
            acc8[pl.ds(2 * PACK, PACK), :] += dhp * xn
            dxn = dhp * g_ref[...]
            gx_ref[pl.ds(r0, PACK), :] = dx2_ref[pl.ds(r0, PACK), :] + r * (dxn - xn * _rowmean(dxn * xn))
        _strips(tm, PACK, strip)

        @pl.when(i == n_i - 1)
        def _():
            for k in range(3):
                st_ref[pl.ds(k, 1), :] = jnp.sum(acc8[pl.ds(k * PACK, PACK), :], axis=0, keepdims=True)

    tile = pl.BlockSpec((tm, D), lambda i: (i, 0))
    return pl.pallas_call(
        body, name="in_proj_bwd", grid=(n_i,),
        in_specs=[pl.BlockSpec((tm, wn), functools.partial(lambda j, i: (i, j), j)) for j in range(N_CHIP)] + [
                  pl.BlockSpec((N_CHIP, wn, D), lambda i: (0, 0, 0), pipeline_mode=pl.Buffered(1)),
                  tile, tile, _vec_spec(1), _vec_spec(1)],
        out_specs=[tile, pl.BlockSpec((SUB, D), lambda i: (0, 0))],
        out_shape=[_sds((s_len, D), F32), _sds((SUB, D), F32)],
        scratch_shapes=[pltpu.VMEM((tm, D), F32), pltpu.VMEM((3 * PACK, D), F32)],
        compiler_params=_params(("arbitrary",)),
    )(dp, dp, dp, dp, wg_in, x, dx2, scale, g_pre)


def _grad_matmul(a, b, name):
    s_len, n = b.shape
    cb = min(2 * D, n)
    tn = 512
    per = cb // tn

    def body(a_ref, b_ref, ob_ref):
        ob_ref[0] = lax.dot_general(a_ref[...], b_ref[...], (((0,), (0,)), ((), ())),
                                    preferred_element_type=F32).astype(BF16)

    return pl.pallas_call(
        body, name=name, grid=(n // tn,),
        in_specs=[pl.BlockSpec((s_len, D), lambda j: (0, 0), pipeline_mode=pl.Buffered(1)),
                  pl.BlockSpec((s_len, tn), lambda j: (0, j))],
        out_specs=pl.BlockSpec((1, D, tn), lambda j: (j // per, 0, j % per)),
        out_shape=_sds((n // cb, D, cb), BF16),
        compiler_params=_params(("arbitrary",)),
    )(a, b)


def _local_step(x, target, shift, scale, gate, g_pre, conv_w_full, conv_b, conv_ln_g, conv_ln_b,
                sgu_ln_g, sgu_ln_b, w_sgu, b_sgu, g_final, wg_in, out_shards):
    conv_wb = jnp.repeat(conv_w_full, SUB, axis=0)
    causal = jnp.tril(jnp.ones((CHUNK, CHUNK), dtype=bool))
    wt = jnp.where(causal[None], w_sgu, 0.0).astype(BF16)
    wtt = jnp.swapaxes(wt, 1, 2)
    bias_full = jnp.repeat(b_sgu.T, LANE, axis=1)

    p, hb, gathered = _in_proj_gather(x, shift, scale, g_pre, wg_in, out_shards)
    w_co, w_so, w_o = (g.reshape(D, D) for g in gathered)
    ya_in, y1 = _branch_a_fwd(p, conv_wb, conv_b, conv_ln_g, conv_ln_b)
    yb_in = _branch_b_fwd(p, wt, bias_full, sgu_ln_g, sgu_ln_b)
    dx2, dya_in, dyb_in, dp, mb, dob, dyab, dybb, sums_o = _out_proj(
        p, ya_in, yb_in, x, target, gate, g_final, w_co, w_so, w_o)
    dp, st_a = _branch_a_bwd(p, y1, dya_in, dp, conv_wb, conv_ln_g, conv_ln_b)
    dp, st_b, gbt, gws = _branch_b_bwd(p, dyb_in, dp, wt, wtt, bias_full, sgu_ln_g, sgu_ln_b)
    grad_x, st_i = _in_proj_bwd(dp, jnp.swapaxes(wg_in, 1, 2), x, dx2, shift, scale, g_pre)
    gw_o = _grad_matmul(mb, dob, "grad_w_o")
    gw_co = _grad_matmul(ya_in, dyab, "grad_w_conv_out")
    gw_so = _grad_matmul(yb_in, dybb, "grad_w_sgu_out")
    return dict(
        grad_x=grad_x, loss_cols=sums_o[2:3], g_final=sums_o[0:1], d_gate=sums_o[1:2],
        d_shift=st_i[0:1], d_scale=st_i[1:2], g_pre=st_i[2:3],
        conv_ln_g=st_a[0:1], conv_ln_b=st_a[1:2], conv_b=st_a[2:3], conv_w=st_a[SUB:SUB + CONV_K],
        sgu_ln_g=st_b[0:1], sgu_ln_b=st_b[1:2], b_sgu=gbt[:, :HEADS].T, w_sgu=gws,
        hb=hb, dp=dp, w_o=gw_o, w_conv_out=gw_co, w_sgu_out=gw_so)


ANY_SPEC = pl.BlockSpec(memory_space=pl.ANY)
VMEM_SPEC = pl.BlockSpec(memory_space=pltpu.VMEM)


def _place():
    return lax.axis_index("x"), lax.axis_index("y"), lax.axis_index("c")


def _peer(k):
    x, y, c = _place()
    return (1 - x if k & 4 else x, 1 - y if k & 2 else y, 1 - c if k & 1 else c)


def _dev_of(p):
    return 4 * p[0] + 2 * p[1] + p[2]


def _chip_of(p):
    return 2 * p[0] + p[1]


def _rdma(src, dst, send_sem, recv_sem, to):
    return pltpu.make_async_remote_copy(src_ref=src, dst_ref=dst, send_sem=send_sem, recv_sem=recv_sem,
                                        device_id=to, device_id_type=MESH)


CHIP_PEERS = (2, 4, 6)
ALL_PEERS = tuple(range(1, N_DEV))
SIBLING = 1


def _setup_comm(c8, w_ada_s, b_ada_s, convw_s, shards):
    n_mod = w_ada_s.shape[1]
    rows = SUB * N_DEV
    n = len(shards)
    parts = 4

    def body(c8_ref, wada_ref, bada_ref, cw_ref, *refs):
        ins, (call_ref, mod_ref, cwall_ref), outs = refs[:n], refs[n:n + 3], refs[n + 3:2 * n + 3]
        csend, crecv, wsend, wrecv, msend, mrecv = refs[2 * n + 3:2 * n + 9]
        ins = [r.at[:, pl.ds(k * (s.shape[1] // parts), s.shape[1] // parts)] for r, s in zip(ins, shards) for k in range(parts)]
        outs = [r.at[:, :, pl.ds(k * (s.shape[1] // parts), s.shape[1] // parts)] for r, s in zip(outs, shards) for k in range(parts)]
        gather_a, gather_b, gather_c = _gather_phases([s.shape[0] for s in shards for _ in range(parts)], ins, outs, refs[2 * n + 9:])
        me = _place()
        dev, chip = _dev_of(me), _chip_of(me)

        def c_rows(d):
            return call_ref.at[pl.ds(pl.multiple_of(d * SUB, SUB), SUB), :]

        call_ref[pl.ds(pl.multiple_of(dev * SUB, SUB), SUB), :] = c8_ref[...]
        cwall_ref[chip] = cw_ref[...]
        c_out = [_rdma(c8_ref, c_rows(dev), csend.at[k], crecv.at[k], _peer(k)) for k in ALL_PEERS]
        w_out = [_rdma(cw_ref, cwall_ref.at[chip], wsend.at[k], wrecv.at[k], _peer(k)) for k in CHIP_PEERS]
        for cp in c_out + w_out:
            cp.start()
        for k in ALL_PEERS:
            _rdma(c8_ref, c_rows(_dev_of(_peer(k))), csend.at[k], crecv.at[k], _peer(k)).wait_recv()
        part = jnp.dot(call_ref[...].astype(BF16), wada_ref[...].astype(BF16), preferred_element_type=F32) + bada_ref[...]
        mod_ref[chip] = part
        m_out = [_rdma(mod_ref.at[chip], mod_ref.at[chip], msend.at[k], mrecv.at[k], _peer(k)) for k in CHIP_PEERS]
        for cp in m_out:
            cp.start()
        gather_a()
        for k in CHIP_PEERS:
            pc = _chip_of(_peer(k))
            _rdma(cw_ref, cwall_ref.at[pc], wsend.at[k], wrecv.at[k], _peer(k)).wait_recv()
            _rdma(mod_ref.at[pc], mod_ref.at[pc], msend.at[k], mrecv.at[k], _peer(k)).wait_recv()
        for cp in c_out + w_out + m_out:
            cp.wait_send()
        gather_b()
        gather_c()

    res = pl.pallas_call(
        body, name="setup_comm",
        in_specs=[VMEM_SPEC] * (4 + n), out_specs=[VMEM_SPEC] * (3 + n),
        out_shape=([_sds((rows, D), F32), _sds((N_CHIP, rows, n_mod), F32), _sds((N_CHIP,) + convw_s.shape, F32)]
                   + [_sds((N_CHIP,) + s.shape, s.dtype) for s in shards]),
        scratch_shapes=([pltpu.SemaphoreType.DMA((N_DEV,))] * 6 + [pltpu.SemaphoreType.DMA((n * parts,))]
                        + [pltpu.SemaphoreType.DMA((n * parts, len(CHIP_PEERS)))] * 4),
        compiler_params=_params(),
    )(c8, w_ada_s, b_ada_s, convw_s, *shards)
    return res[0], res[1], res[2], res[3:]


def _gather_phases(row_counts, ins, dsts, sems):
    n = len(row_counts)
    lsem, isend, irecv, dsend, drecv = sems
    me = _place()
    chip, c = _chip_of(me), me[2]

    def half(t, which):
        hr = row_counts[t] // 2
        return pl.ds(pl.multiple_of(which * hr, hr), hr)

    def local(t):
        return pltpu.make_async_copy(ins[t], dsts[t].at[chip], lsem.at[t])

    def to_chip(t, j):
        return _rdma(ins[t].at[half(t, c)], dsts[t].at[chip, half(t, c)], isend.at[t, j], irecv.at[t, j], _peer(CHIP_PEERS[j]))

    def landed(t, j, which):
        return dsts[t].at[_chip_of(_peer(CHIP_PEERS[j])), half(t, which)]

    def to_sibling(t, j):
        return _rdma(landed(t, j, c), landed(t, j, c), dsend.at[t, j], drecv.at[t, j], _peer(SIBLING))

    pairs = [(t, j) for t in range(n) for j in range(len(CHIP_PEERS))]

    def phase_a():
        for t in range(n):
            local(t).start()
        for t, j in pairs:
            to_chip(t, j).start()

    def phase_b():
        for t, j in pairs:
            _rdma(landed(t, j, c), landed(t, j, c), isend.at[t, j], irecv.at[t, j], _peer(CHIP_PEERS[j])).wait_recv()
            to_sibling(t, j).start()

    def phase_c():
        for t, j in pairs:
            _rdma(landed(t, j, 1 - c), landed(t, j, 1 - c), dsend.at[t, j], drecv.at[t, j], _peer(SIBLING)).wait_recv()
        for t, j in pairs:
            to_chip(t, j).wait_send()
            to_sibling(t, j).wait_send()
        for t in range(n):
            local(t).wait()

    return phase_a, phase_b, phase_c


def _in_proj_gather(x, shift, scale, g_pre, wg_in, shards):
    s_len = x.shape[0]
    tm = min(256, s_len)
    n_i = s_len // tm
    wn = wg_in.shape[2]
    n = len(shards)

    def body(x_ref, sh_ref, sc_ref, g_ref, w_ref, *refs):
        ins, p_ref, hb_ref, outs = refs[:n], refs[n], refs[n + 1], refs[n + 2:2 * n + 2]
        gath, sems = refs[2 * n + 2:3 * n + 2], refs[3 * n + 2:]
        phases = _gather_phases([s.shape[0] for s in shards], ins, gath, sems)
        i = pl.program_id(0)
        for step, phase in zip((0, n_i // 2, n_i - 1), phases):
            pl.when(i == step)(phase)

        @pl.when(i == n_i - 1)
        def _():
            for t in range(n):
                outs[t][...] = gath[t][...]

        def strip(r0):
            xs = x_ref[pl.ds(r0, PACK), :]
            r = lax.rsqrt(_rowmean(xs * xs) + EPS)
            h = (xs * r) * g_ref[...] * (1.0 + sc_ref[...]) + sh_ref[...]
            hb_ref[pl.ds(r0, PACK), :] = h.astype(BF16)
        _strips(tm, PACK, strip)
        hb = hb_ref[...]
        for j in range(N_CHIP):
            p_ref[:, pl.ds(j * wn, wn)] = jnp.dot(hb, w_ref[j], preferred_element_type=F32).astype(BF16)

    res = pl.pallas_call(
        body, name="in_proj", grid=(n_i,),
        in_specs=[pl.BlockSpec((tm, D), lambda i: (i, 0)), _vec_spec(1), _vec_spec(1), _vec_spec(1),
                  pl.BlockSpec((N_CHIP, D, wn), lambda i: (0, 0, 0), pipeline_mode=pl.Buffered(1))] + [VMEM_SPEC] * n,
        out_specs=[pl.BlockSpec((tm, N_CHIP * wn), lambda i: (i, 0)), pl.BlockSpec((tm, D), lambda i: (i, 0))] + [VMEM_SPEC] * n,
        out_shape=([_sds((s_len, N_SEC * D), BF16), _sds((s_len, D), BF16)]
                   + [_sds((N_CHIP,) + s.shape, s.dtype) for s in shards]),
        scratch_shapes=([pltpu.VMEM((N_CHIP,) + s.shape, s.dtype) for s in shards]
                        + [pltpu.SemaphoreType.DMA((n,))] + [pltpu.SemaphoreType.DMA((n, len(CHIP_PEERS)))] * 4),
        compiler_params=_params(("arbitrary",)),
    )(x, shift, scale, g_pre, wg_in, *shards)
    return res[0], res[1], res[2:]


def _reduce_scatter(grads, name):
    n = len(grads)
    shapes = [g.shape[2:] for g in grads]
    parts = 4
    part_shapes = [(r, cols // parts) for r, cols in shapes for _ in range(parts)]

    def body(*refs):
        def halves(group, lead):
            return [ref.at[(slice(None),) * lead + (pl.ds(k * (s[1] // parts), s[1] // parts),)]
                    for ref, s in zip(group, shapes) for k in range(parts)]
        ins, outs = halves(refs[:n], 3), halves(refs[n:2 * n], 2)
        pbufs, rbufs, accs = halves(refs[2 * n:3 * n], 2), halves(refs[3 * n:4 * n], 2), halves(refs[4 * n:5 * n], 1)
        for phase in _reduce_phases(part_shapes, ins, outs, pbufs, rbufs, accs, refs[5 * n:]):
            phase()

    return pl.pallas_call(
        body, name=name,
        in_specs=[VMEM_SPEC] * n, out_specs=[VMEM_SPEC] * n,
        out_shape=[_sds((2,) + s, F32) for s in shapes],
        scratch_shapes=_reduce_scratch(shapes)[:3 * n] + _reduce_scratch(part_shapes)[3 * parts * n:],
        compiler_params=_params(),
    )(*grads)


def _reduce_phases(shapes, ins, outs, pbufs, rbufs, accs, sems):
    n = len(shapes)
    psend, precv, csend, crecv, fsend, frecv = sems
    me = _place()
    chip, c = _chip_of(me), me[2]
    sib = _peer(SIBLING)

    def to_sibling(t, d):
        return _rdma(ins[t].at[d, 1 - c], pbufs[t].at[d], psend.at[t, d], precv.at[t, d], sib)

    def to_chip(t, j):
        return _rdma(pbufs[t].at[jnp.bitwise_xor(chip, j)], rbufs[t].at[j - 1], csend.at[t, j], crecv.at[t, j], _peer(2 * j))

    def finished(t):
        return _rdma(outs[t].at[c], outs[t].at[c], fsend.at[t], frecv.at[t], sib)

    def phase_a():
        for j in (1, 2, 3, 0):
            for t in range(n):
                to_sibling(t, jnp.bitwise_xor(chip, j)).start()

    def phase_b():
        for j in (1, 2, 3, 0):
            d = jnp.bitwise_xor(chip, j)
            for t in range(n):
                to_sibling(t, d).wait_recv()

                def pair_sum(r0, t=t, d=d, j=j):
                    rows = pl.ds(r0, PACK)
                    s = ins[t][d, c, rows, :].astype(F32) + pbufs[t][d, rows, :].astype(F32)
                    if j == 0:
                        accs[t][rows, :] = s
                    else:
                        pbufs[t][d, rows, :] = s.astype(BF16)
                _strips(shapes[t][0], PACK, pair_sum)
                if j:
                    to_chip(t, j).start()

    def phase_c():
        for t in range(n):
            for j in (1, 2, 3):
                blk = rbufs[t].at[j - 1]
                _rdma(blk, blk, csend.at[t, j], crecv.at[t, j], _peer(2 * j)).wait_recv()

            def total(r0, t=t):
                rows = pl.ds(r0, PACK)
                s = accs[t][rows, :] + rbufs[t][0, rows, :].astype(F32)
                s = s + rbufs[t][1, rows, :].astype(F32)
                outs[t][c, rows, :] = s + rbufs[t][2, rows, :].astype(F32)
            _strips(shapes[t][0], PACK, total)
            finished(t).start()

    def phase_d():
        for t in range(n):
            blk = outs[t].at[1 - c]
            _rdma(blk, blk, fsend.at[t], frecv.at[t], sib).wait_recv()
        for t in range(n):
            for d in range(N_CHIP):
                to_sibling(t, d).wait_send()
            for j in (1, 2, 3):
                to_chip(t, j).wait_send()
            finished(t).wait_send()

    return phase_a, phase_b, phase_c, phase_d


def _sum_small_phases(ins, outs, pbufs, buf4s, sems):
    n = len(ins)
    psend, precv, send, recv = sems
    chip = _chip_of(_place())

    def swap(t):
        return _rdma(ins[t], pbufs[t], psend.at[t], precv.at[t], _peer(SIBLING))

    def to_chip(t, k):
        return _rdma(buf4s[t].at[chip], buf4s[t].at[chip], send.at[t, k], recv.at[t, k], _peer(k))

    def phase_a():
        for t in range(n):
            swap(t).start()

    def phase_b():
        for t in range(n):
            swap(t).wait()
            buf4s[t][chip] = ins[t][...] + pbufs[t][...]
            for k in CHIP_PEERS:
                to_chip(t, k).start()

    def phase_c():
        for t in range(n):
            for k in CHIP_PEERS:
                blk = buf4s[t].at[_chip_of(_peer(k))]
                _rdma(blk, blk, send.at[t, k], recv.at[t, k], _peer(k)).wait_recv()
            outs[t][...] = (buf4s[t][0] + buf4s[t][1]) + (buf4s[t][2] + buf4s[t][3])

    def phase_d():
        for t in range(n):
            for k in CHIP_PEERS:
                to_chip(t, k).wait_send()

    return phase_a, phase_b, phase_c, phase_d


def _sum_small_scratch(blobs):
    n = len(blobs)
    return ([pltpu.VMEM(b.shape, F32) for b in blobs] + [pltpu.VMEM((N_CHIP,) + b.shape, F32) for b in blobs]
            + [pltpu.SemaphoreType.DMA((n,))] * 2 + [pltpu.SemaphoreType.DMA((n, N_DEV))] * 2)


def _reduce_scratch(shapes):
    n = len(shapes)
    return ([pltpu.VMEM((N_CHIP,) + s, BF16) for s in shapes] + [pltpu.VMEM((N_CHIP - 1,) + s, BF16) for s in shapes]
            + [pltpu.VMEM(s, F32) for s in shapes]
            + [pltpu.SemaphoreType.DMA((n, N_CHIP))] * 4 + [pltpu.SemaphoreType.DMA((n,))] * 2)


def _grad_matmul_reduce(a, b, name, grads, blobs):
    s_len, n_cols = b.shape
    cb = min(2 * D, n_cols)
    tn = 512
    per = cb // tn
    steps = n_cols // tn
    n, nb = len(grads), len(blobs)
    shapes = [g.shape[2:] for g in grads]
    n_red = len(_reduce_scratch(shapes))

    def body(a_ref, b_ref, *refs):
        ins, bins = refs[:n], refs[n:n + nb]
        ob_ref, outs, bouts = refs[n + nb], refs[n + nb + 1:2 * n + nb + 1], refs[2 * n + nb + 1:2 * (n + nb) + 1]
        scratch = refs[2 * (n + nb) + 1:]
        fulls, red, small = scratch[:n], scratch[n:n + n_red], scratch[n + n_red:]
        phases = _reduce_phases(shapes, ins, fulls, red[:n], red[n:2 * n], red[2 * n:3 * n], red[3 * n:])
        small_phases = _sum_small_phases(bins, bouts, small[:nb], small[nb:2 * nb], small[2 * nb:])
        j = pl.program_id(0)
        for step, phase in zip((0, 2, steps - 2, steps - 1), phases):
            pl.when(j == step)(phase)
        for step, phase in zip((1, 3, steps - 2, steps - 1), small_phases):
            pl.when(j == step)(phase)

        @pl.when(j == steps - 1)
        def _():
            for t in range(n):
                outs[t][...] = fulls[t][...]
        ob_ref[0] = lax.dot_general(a_ref[...], b_ref[...], (((0,), (0,)), ((), ())),
                                    preferred_element_type=F32).astype(BF16)

    res = pl.pallas_call(
        body, name=name, grid=(steps,),
        in_specs=[pl.BlockSpec((s_len, D), lambda j: (0, 0), pipeline_mode=pl.Buffered(1)),
                  pl.BlockSpec((s_len, tn), lambda j: (0, j))] + [VMEM_SPEC] * (n + nb),
        out_specs=[pl.BlockSpec((1, D, tn), lambda j: (j // per, 0, j % per))] + [VMEM_SPEC] * (n + nb),
        out_shape=([_sds((n_cols // cb, D, cb), BF16)] + [_sds((2,) + s, F32) for s in shapes]
                   + [_sds(bl.shape, F32) for bl in blobs]),
        scratch_shapes=[pltpu.VMEM((2,) + s, F32) for s in shapes] + _reduce_scratch(shapes) + _sum_small_scratch(blobs),
        compiler_params=_params(("arbitrary",)),
    )(a, b, *grads, *blobs)
    return res[0], res[1:1 + n], res[1 + n:]


def _adamw_math(w, g, m, v):
    m = ADAM_B1 * m + (1.0 - ADAM_B1) * g
    v = ADAM_B2 * v + (1.0 - ADAM_B2) * (g * g)
    m_hat = m / (1.0 - ADAM_B1 ** ADAM_STEP)
    v_hat = v / (1.0 - ADAM_B2 ** ADAM_STEP)
    delta = -ADAM_LR * (m_hat / (jnp.sqrt(v_hat) + ADAM_EPS) + ADAM_WD * w)
    return delta, m, v


def _row_tile(r, cols):
    if r * cols * 4 <= 2 ** 20:
        return r
    return next(t for t in (512, 256, 128, 64, 32, 16, 8) if r % t == 0 and t * cols * 4 <= 2 ** 20)


def _adamw(w, g, m, v, name):
    r, cols = w.shape
    tr = _row_tile(r, cols)

    def body(w_ref, g_ref, m_ref, v_ref, go_ref, d_ref, nm_ref, nv_ref):
        g = g_ref[...]
        go_ref[...] = g
        d_ref[...], nm_ref[...], nv_ref[...] = _adamw_math(w_ref[...], g, m_ref[...], v_ref[...])

    spec = pl.BlockSpec((tr, cols), lambda i: (i, 0))
    return pl.pallas_call(
        body, name=name, grid=(r // tr,), in_specs=[spec] * 4, out_specs=[spec] * 4,
        out_shape=[_sds((r, cols), F32)] * 4, compiler_params=_params(("arbitrary",)),
    )(w, g, m, v)


def _adamw_ada(w, ct, dm, m, v):
    r, cols = w.shape
    tr = _row_tile(r, cols)

    def body(w_ref, ct_ref, dm_ref, m_ref, v_ref, g_ref, d_ref, nm_ref, nv_ref):
        g = jnp.dot(ct_ref[...], dm_ref[...], preferred_element_type=F32)
        g_ref[...] = g
        d_ref[...], nm_ref[...], nv_ref[...] = _adamw_math(w_ref[...], g, m_ref[...], v_ref[...])

    spec = pl.BlockSpec((tr, cols), lambda i: (i, 0))
    return pl.pallas_call(
        body, name="adamw_ada", grid=(r // tr,),
        in_specs=[spec, pl.BlockSpec((tr, LANE), lambda i: (i, 0)), pl.BlockSpec((LANE, cols), lambda i: (0, 0)), spec, spec],
        out_specs=[spec] * 4, out_shape=[_sds((r, cols), F32)] * 4, compiler_params=_params(("arbitrary",)),
    )(w, ct, dm, m, v)


BLOB_VEC, BLOB_BSGU, BLOB_CONV, BLOB_ADA, BLOB_DMOD, BLOB_LOSS, BLOB_ROWS = 0, 8, 16, 48, 56, 80, 88
N_VEC = 7


def _adamw_small(tot, g_w_sgu, g_conv, params):
    n = len(params)

    def body(*refs):
        tot_ref, gws_ref, gconv_ref = refs[:3]
        wmv = refs[3:3 + 3 * n]
        outs = refs[3 + 3 * n:]
        grads = [tot_ref[pl.ds(BLOB_VEC + i, 1), :] for i in range(N_VEC)]
        grads += [tot_ref[pl.ds(BLOB_BSGU, HEADS), pl.ds(0, CHUNK)], gconv_ref[...], gws_ref[...], tot_ref[pl.ds(BLOB_ADA, 3), :]]
        for i, g in enumerate(grads):
            w_ref, m_ref, v_ref = wmv[3 * i:3 * i + 3]
            d, nm, nv = _adamw_math(w_ref[...], g, m_ref[...], v_ref[...])
            outs[4 * i][...] = g
            outs[4 * i + 1][...] = d
            outs[4 * i + 2][...] = nm
            outs[4 * i + 3][...] = nv

    flat = [a for wmv in params for a in wmv]
    return pl.pallas_call(
        body, name="adamw_small",
        in_specs=[VMEM_SPEC] * (3 + len(flat)), out_specs=[VMEM_SPEC] * (4 * n),
        out_shape=[_sds(wmv[0].shape, F32) for wmv in params for _ in range(4)],
        compiler_params=_params(),
    )(tot, g_w_sgu, g_conv, *flat)


def _set_rows(buf, row, val):
    return lax.dynamic_update_slice(buf, val.astype(F32), (row, 0))


def kernel(x, c, w_ada, b_ada, g_pre, w_in, conv_w, conv_b, conv_ln_g, conv_ln_b, w_conv_out, sgu_ln_g, sgu_ln_b, w_sgu, b_sgu, w_sgu_out, w_o, g_final, loss_target, m_w_ada, m_b_ada, m_g_pre, m_w_in, m_conv_w, m_conv_b, m_conv_ln_g, m_conv_ln_b, m_w_conv_out, m_sgu_ln_g, m_sgu_ln_b, m_w_sgu, m_b_sgu, m_w_sgu_out, m_w_o, m_g_final, v_w_ada, v_b_ada, v_g_pre, v_w_in, v_conv_w, v_conv_b, v_conv_ln_g, v_conv_ln_b, v_w_conv_out, v_sgu_ln_g, v_sgu_ln_b, v_w_sgu, v_b_sgu, v_w_sgu_out, v_w_o, v_g_final):
    me = _place()
    dev, chip = _dev_of(me), _chip_of(me)
    n_ada = w_ada.shape[2]
    conv_cols = conv_w.shape[2]

    b_ada_s = lax.dynamic_slice(b_ada, (0, chip * n_ada), (1, n_ada))
    c_all, mod_all, cw_all, (wg_in,) = _setup_comm(
        jnp.broadcast_to(c, (SUB, D)), w_ada[0], b_ada_s, jnp.pad(conv_w[0], ((0, HALO - CONV_K), (0, 0))),
        [w_in[0].astype(BF16)])
    mod = lax.dynamic_slice(mod_all, (0, dev * SUB, 0), (N_CHIP, 1, n_ada)).reshape(1, 3 * D)
    shift, scale, gate = mod[:, :D], mod[:, D:2 * D], mod[:, 2 * D:]
    conv_w_full = jnp.swapaxes(cw_all, 0, 1).reshape(HALO, D)[:CONV_K]

    loc = _local_step(x[0], loss_target[0], shift, scale, gate, g_pre, conv_w_full, conv_b, conv_ln_g, conv_ln_b,
                      sgu_ln_g, sgu_ln_b, w_sgu[0], b_sgu[0], g_final.reshape(1, D), wg_in,
                      [w_conv_out[0].astype(BF16), w_sgu_out[0].astype(BF16), w_o[0].astype(BF16)])

    d_mod = jnp.concatenate([loc["d_shift"], loc["d_scale"], loc["d_gate"]], axis=0)
    blob = jnp.zeros((BLOB_ROWS, D), F32)
    for i, name in enumerate(["g_pre", "conv_b", "conv_ln_g", "conv_ln_b", "sgu_ln_g", "sgu_ln_b", "g_final"]):
        blob = _set_rows(blob, BLOB_VEC + i, loc[name])
    blob = _set_rows(blob, BLOB_BSGU, loc["b_sgu"])
    blob = _set_rows(blob, BLOB_CONV, loc["conv_w"])
    blob = _set_rows(blob, BLOB_ADA, d_mod)
    blob = lax.dynamic_update_slice(blob, d_mod, (BLOB_DMOD + 3 * dev, 0))
    blob = _set_rows(blob, BLOB_LOSS, loc["loss_cols"])

    big = ["w_in", "w_conv_out", "w_sgu_out", "w_o"]
    contrib_out = [loc[name].reshape(N_CHIP, 2, D // (2 * N_CHIP), D) for name in big[1:]]
    gw_in, full_out, (tot, g_w_sgu) = _grad_matmul_reduce(
        loc["hb"], loc["dp"], "grad_w_in", contrib_out, [blob, loc["w_sgu"].reshape(HEADS * CHUNK, CHUNK)])
    full_in = _reduce_scatter([gw_in.reshape(N_CHIP, 2, D // 2, gw_in.shape[2])], "reduce_w_in")
    g_big = {name: f.reshape(2 * f.shape[1], f.shape[2]) for name, f in zip(big, list(full_in) + list(full_out))}

    loss = jnp.sum(tot[BLOB_LOSS])
    g_conv_s = lax.dynamic_slice(tot, (BLOB_CONV, chip * conv_cols), (CONV_K, conv_cols))
    d_mod_all = tot[BLOB_DMOD:BLOB_DMOD + 3 * N_DEV].reshape(N_DEV, 3 * D)

    ct = jnp.pad(c_all[::SUB].T, ((0, 0), (0, LANE - N_DEV))).astype(BF16)
    dm = jnp.pad(lax.dynamic_slice(d_mod_all, (0, chip * n_ada), (N_DEV, n_ada)), ((0, LANE - N_DEV), (0, 0))).astype(BF16)
    g_ada, d_ada, nm_ada, nv_ada = _adamw_ada(w_ada[0], ct, dm, m_w_ada[0], v_w_ada[0])

    upd = {}
    for name, w, m, v in [("w_in", w_in, m_w_in, v_w_in), ("w_conv_out", w_conv_out, m_w_conv_out, v_w_conv_out),
                          ("w_sgu_out", w_sgu_out, m_w_sgu_out, v_w_sgu_out), ("w_o", w_o, m_w_o, v_w_o)]:
        upd[name] = _adamw(w[0], g_big[name], m[0], v[0], "adamw_" + name)

    def wmv(w, m, v, shape):
        return tuple(a.reshape(shape) for a in (w, m, v))

    small_params = [wmv(w, m, v, (1, D)) for w, m, v in [
        (g_pre, m_g_pre, v_g_pre), (conv_b, m_conv_b, v_conv_b), (conv_ln_g, m_conv_ln_g, v_conv_ln_g),
        (conv_ln_b, m_conv_ln_b, v_conv_ln_b), (sgu_ln_g, m_sgu_ln_g, v_sgu_ln_g), (sgu_ln_b, m_sgu_ln_b, v_sgu_ln_b),
        (g_final, m_g_final, v_g_final)]]
    small_params += [wmv(b_sgu, m_b_sgu, v_b_sgu, (HEADS, CHUNK)), wmv(conv_w, m_conv_w, v_conv_w, (CONV_K, conv_cols)),
                     wmv(w_sgu, m_w_sgu, v_w_sgu, (HEADS * CHUNK, CHUNK)), wmv(b_ada, m_b_ada, v_b_ada, (3, D))]
    small_out = _adamw_small(tot, g_w_sgu, g_conv_s, small_params)

    def leaves(kind):
        vecs = [small_out[4 * i + kind] for i in range(N_VEC)]
        o_b_sgu, o_conv, o_w_sgu, o_b_ada = (small_out[4 * (N_VEC + i) + kind] for i in range(4))
        ada = (g_ada, d_ada, nm_ada, nv_ada)[kind]
        def bigk(name):
            return upd[name][kind][None]
        return [ada[None], o_b_ada.reshape(1, 3 * D), vecs[0], bigk("w_in"), o_conv[None], vecs[1], vecs[2], vecs[3],
                bigk("w_conv_out"), vecs[4], vecs[5], o_w_sgu.reshape(1, HEADS, CHUNK, CHUNK), o_b_sgu[None],
                bigk("w_sgu_out"), bigk("w_o"), vecs[6].reshape(D)]

    return (loss, loc["grad_x"][None], *leaves(0), *leaves(1), *leaves(2), *leaves(3))
```

```python
import functools

import jax
import jax.numpy as jnp
from jax import lax
from jax.experimental import pallas as pl
from jax.experimental.pallas import tpu as pltpu

F32 = jnp.float32
BF16 = jnp.bfloat16
MESH = pl.DeviceIdType.MESH

D = 1024
N_SEC = 8
N_CHIP = 4
N_DEV = 8
EPS = 1e-6
CONV_K = 31
HALO = 32
CHUNK = 128
HEADS = 8
LANE = 128
SUB = 8
PACK = 16
VMEM_LIMIT = 56 * 1024 * 1024

ADAM_LR, ADAM_B1, ADAM_B2, ADAM_EPS, ADAM_WD, ADAM_STEP = 0.001, 0.9, 0.999, 1e-08, 0.01, 10

_SQRT_HALF = 0.7071067811865476
_INV_SQRT_2PI = 0.3989422804014327


def _sds(shape, dtype):
    return jax.ShapeDtypeStruct(shape, dtype)


def _params(sem=None):
    if sem is None:
        return pltpu.CompilerParams(vmem_limit_bytes=VMEM_LIMIT)
    return pltpu.CompilerParams(dimension_semantics=sem, vmem_limit_bytes=VMEM_LIMIT)


def _strips(n_rows, rows, fn):
    def step(s, carry):
        fn(pl.multiple_of(s * rows, rows))
        return carry
    lax.fori_loop(0, n_rows // rows, step, 0)


def _sigmoid(v):
    return 1.0 / (1.0 + jnp.exp(-v))


def _gelu(v):
    return 0.5 * v * (1.0 + lax.erf(v * _SQRT_HALF))


def _gelu_and_grad(v):
    cdf = 0.5 * (1.0 + lax.erf(v * _SQRT_HALF))
    return v * cdf, cdf + v * jnp.exp(-0.5 * v * v) * _INV_SQRT_2PI


def _dsilu(v, sg):
    return sg * (1.0 + v * (1.0 - sg))


def _rowmean(v):
    return jnp.mean(v, axis=-1, keepdims=True)


def _vec_spec(grid_rank):
    zeros = (0, 0)
    if grid_rank == 1:
        return pl.BlockSpec((1, D), lambda i: zeros)
    return pl.BlockSpec((1, D), lambda i, j: zeros)


def _conv_taps(win_ref, r0, lt, weight_of_offset, rows):
    lanes = pl.ds(lt * LANE, LANE)
    win = win_ref[pl.ds(r0, rows + HALO), lanes]
    n_out = rows // SUB
    acc = [jnp.zeros((SUB, LANE), F32) for _ in range(n_out)]
    for phase in range(SUB):
        offs = [o for o in weight_of_offset if o % SUB == phase]
        if not offs:
            continue
        q_max = max(o // SUB for o in offs)
        span = (n_out + q_max) * SUB
        sh = win[phase:phase + span, :]
        for o in offs:
            q = o // SUB
            w = weight_of_offset[o](lanes)
            for m in range(n_out):
                acc[m] = acc[m] + w * sh[(m + q) * SUB:(m + q + 1) * SUB, :]
    return acc


def _branch_a_fwd(p, conv_wb, conv_b, ln_g, ln_b):
    s_len = p.shape[0]
    tm = min(512, s_len)
    n_i = s_len // tm
    rows = 64

    def body(p_ref, wb_ref, cb_ref, g_ref, b_ref, ya_ref, y1_ref, abuf):
        @pl.when(pl.program_id(0) == 0)
        def _():
            abuf[pl.ds(0, HALO), :] = jnp.zeros((HALO, D), F32)

        def glu(r0):
            val = p_ref[pl.ds(r0, PACK), pl.ds(0, D)].astype(F32)
            gl = p_ref[pl.ds(r0, PACK), pl.ds(D, D)].astype(F32)
            abuf[pl.ds(HALO + r0, PACK), :] = val * _sigmoid(gl)
        _strips(tm, PACK,glu)

        taps = {HALO - (CONV_K - 1) + k: (lambda lanes, k=k: wb_ref[pl.ds(k * SUB, SUB), lanes]) for k in range(CONV_K)}

        def conv(r0):
            for lt in range(D // LANE):
                acc = _conv_taps(abuf, r0, lt, taps, rows)
                cb = cb_ref[:, pl.ds(lt * LANE, LANE)]
                for m, v in enumerate(acc):
                    y1_ref[pl.ds(r0 + m * SUB, SUB), pl.ds(lt * LANE, LANE)] = v + cb
        _strips(tm, rows, conv)

        def norm(r0):
            y1 = y1_ref[pl.ds(r0, PACK), :]
            mu = _rowmean(y1)
            yc = y1 - mu
            rstd = lax.rsqrt(_rowmean(yc * yc) + EPS)
            l1 = (yc * rstd) * g_ref[...] + b_ref[...]
            z = p_ref[pl.ds(r0, PACK), pl.ds(2 * D, D)].astype(F32)
            ya_ref[pl.ds(r0, PACK), :] = ((l1 * _sigmoid(l1)) * (z * _sigmoid(z))).astype(BF16)
        _strips(tm, PACK,norm)

        abuf[pl.ds(0, HALO), :] = abuf[pl.ds(tm, HALO), :]

    return pl.pallas_call(
        body, name="branch_a_fwd", grid=(n_i,),
        in_specs=[pl.BlockSpec((tm, 3 * D), lambda i: (i, 0)),
                  pl.BlockSpec((CONV_K * SUB, D), lambda i: (0, 0)), _vec_spec(1), _vec_spec(1), _vec_spec(1)],
        out_specs=[pl.BlockSpec((tm, D), lambda i: (i, 0)), pl.BlockSpec((tm, D), lambda i: (i, 0))],
        out_shape=[_sds((s_len, D), BF16), _sds((s_len, D), F32)],
        scratch_shapes=[pltpu.VMEM((tm + HALO, D), F32)],
        compiler_params=_params(("arbitrary",)),
    )(p, conv_wb, conv_b, ln_g, ln_b)


def _branch_b_fwd(p, wt, bias_full, ln_g, ln_b):
    s_len = p.shape[0]
    tm = min(256, s_len)
    n_i = s_len // tm

    def body(p_ref, wt_ref, bias_ref, g_ref, b_ref, yb_ref, vb, sbuf):
        def norm(r0):
            gv = _gelu(p_ref[pl.ds(r0, PACK), pl.ds(D, D)].astype(F32))
            mu = _rowmean(gv)
            vc = gv - mu
            rstd = lax.rsqrt(_rowmean(vc * vc) + EPS)
            vb[pl.ds(r0, PACK), :] = ((vc * rstd) * g_ref[...] + b_ref[...]).astype(BF16)
        _strips(tm, PACK,norm)

        for ck in range(tm // CHUNK):
            for h in range(HEADS):
                blk = (pl.ds(ck * CHUNK, CHUNK), pl.ds(h * LANE, LANE))
                sbuf[blk] = jnp.dot(wt_ref[h], vb[blk], preferred_element_type=F32) + bias_ref[:, pl.ds(h * LANE, LANE)]

        def gate(r0):
            u = _gelu(p_ref[pl.ds(r0, PACK), pl.ds(0, D)].astype(F32))
            z = p_ref[pl.ds(r0, PACK), pl.ds(2 * D, D)].astype(F32)
            yb_ref[pl.ds(r0, PACK), :] = (u * sbuf[pl.ds(r0, PACK), :] * (z * _sigmoid(z))).astype(BF16)
        _strips(tm, PACK,gate)

    return pl.pallas_call(
        body, name="branch_b_fwd", grid=(n_i,),
        in_specs=[pl.BlockSpec((tm, 3 * D), lambda i: (i, 1)),
                  pl.BlockSpec((HEADS, CHUNK, CHUNK), lambda i: (0, 0, 0)),
                  pl.BlockSpec((CHUNK, D), lambda i: (0, 0)), _vec_spec(1), _vec_spec(1)],
        out_specs=pl.BlockSpec((tm, D), lambda i: (i, 0)),
        out_shape=_sds((s_len, D), BF16),
        scratch_shapes=[pltpu.VMEM((tm, D), BF16), pltpu.VMEM((tm, D), F32)],
        compiler_params=_params(("arbitrary",)),
    )(p, wt, bias_full, ln_g, ln_b)


def _dot_t(a, b):
    return lax.dot_general(a, b, (((1,), (1,)), ((), ())), preferred_element_type=F32)


def _out_proj(p, ya_in, yb_in, x, target, gate, g_final, w_co, w_so, w_o):
    s_len = x.shape[0]
    tm = min(256, s_len)
    n_i = s_len // tm

    def body(pg_ref, ya_ref, yb_ref, x_ref, t_ref, gate_ref, gf_ref, wco_ref, wso_ref, wo_ref,
             dx2_ref, dya_ref, dyb_ref, dp_ref, mb_ref, dob_ref, dyab_ref, dybb_ref, sums_ref):
        @pl.when(pl.program_id(0) == 0)
        def _():
            sums_ref[...] = jnp.zeros((SUB, D), F32)

        y_a = jnp.dot(ya_ref[...], wco_ref[...], preferred_element_type=F32)
        y_b = jnp.dot(yb_ref[...], wso_ref[...], preferred_element_type=F32)
        ga = _sigmoid(pg_ref[:, pl.ds(0, D)].astype(F32))
        gb = _sigmoid(pg_ref[:, pl.ds(D, D)].astype(F32))
        mb = (ga * y_a + gb * y_b).astype(BF16)
        mb_ref[...] = mb
        o = jnp.dot(mb, wo_ref[...], preferred_element_type=F32)
        x2 = x_ref[...] + gate_ref[...] * o
        r2 = lax.rsqrt(_rowmean(x2 * x2) + EPS)
        xh = x2 * r2
        e = xh * gf_ref[...] - t_ref[...]
        dy = e * (1.0 / D)
        dxh = dy * gf_ref[...]
        dx2 = r2 * (dxh - xh * _rowmean(dxh * xh))
        dx2_ref[...] = dx2
        sums_ref[pl.ds(0, 1), :] += jnp.sum(dy * xh, axis=0, keepdims=True)
        sums_ref[pl.ds(1, 1), :] += jnp.sum(dx2 * o, axis=0, keepdims=True)
        sums_ref[pl.ds(2, 1), :] += jnp.sum(e * e, axis=0, keepdims=True) * (0.5 / D)
        dob = (gate_ref[...] * dx2).astype(BF16)
        dob_ref[...] = dob
        dm = _dot_t(dob, wo_ref[...])
        dy_a = (ga * dm).astype(BF16)
        dy_b = (gb * dm).astype(BF16)
        dyab_ref[...] = dy_a
        dybb_ref[...] = dy_b
        dp_ref[:, pl.ds(0, D)] = (dm * y_a * ga * (1.0 - ga)).astype(BF16)
        dp_ref[:, pl.ds(D, D)] = (dm * y_b * gb * (1.0 - gb)).astype(BF16)
        dya_ref[...] = _dot_t(dy_a, wco_ref[...])
        dyb_ref[...] = _dot_t(dy_b, wso_ref[...])

    tile = pl.BlockSpec((tm, D), lambda i: (i, 0))
    wspec = pl.BlockSpec((D, D), lambda i: (0, 0))
    return pl.pallas_call(
        body, name="out_proj", grid=(n_i,),
        in_specs=[pl.BlockSpec((tm, 2 * D), lambda i: (i, 3)), tile, tile, tile, tile, _vec_spec(1), _vec_spec(1),
                  wspec, wspec, wspec],
        out_specs=[tile, tile, tile, pl.BlockSpec((tm, 2 * D), lambda i: (i, 3)), tile, tile, tile, tile,
                   pl.BlockSpec((SUB, D), lambda i: (0, 0))],
        out_shape=[_sds((s_len, D), F32), _sds((s_len, D), F32), _sds((s_len, D), F32), _sds((s_len, N_SEC * D), BF16),
                   _sds((s_len, D), BF16), _sds((s_len, D), BF16), _sds((s_len, D), BF16), _sds((s_len, D), BF16),
                   _sds((SUB, D), F32)],
        compiler_params=_params(("arbitrary",)),
    )(p, ya_in, yb_in, x, target, gate, g_final, w_co, w_so, w_o)


A_STATS_ROWS = 8 + HALO


def _branch_a_bwd(p, y1, dya_in, dp, conv_wb, ln_g, ln_b):
    s_len = p.shape[0]
    tm = min(512, s_len)
    n_i = s_len // tm
    rows = 64
    n_out = rows // SUB

    def tile_of(i):
        return n_i - 1 - i

    def body(p_ref, y1_ref, dya_ref, dp_in, wb_ref, g_ref, b_ref, dp_ref, st_ref, dybuf, acc8, tapacc):
        del dp_in
        i = pl.program_id(0)

        @pl.when(i == 0)
        def _():
            dybuf[pl.ds(tm, HALO), :] = jnp.zeros((HALO, D), F32)
            st_ref[...] = jnp.zeros((A_STATS_ROWS, D), F32)
            acc8[...] = jnp.zeros((3 * PACK, D), F32)
            tapacc[...] = jnp.zeros((CONV_K * SUB, D), F32)

        def norm_bwd(r0):
            y1 = y1_ref[pl.ds(r0, PACK), :]
            mu = _rowmean(y1)
            yc = y1 - mu
            rstd = lax.rsqrt(_rowmean(yc * yc) + EPS)
            n1 = yc * rstd
            l1 = n1 * g_ref[...] + b_ref[...]
            sg = _sigmoid(l1)
            z = p_ref[pl.ds(r0, PACK), pl.ds(2 * D, D)].astype(F32)
            sz = _sigmoid(z)
            dya = dya_ref[pl.ds(r0, PACK), :]
            dp_ref[pl.ds(r0, PACK), pl.ds(2 * D, D)] = (dya * (l1 * sg) * _dsilu(z, sz)).astype(BF16)
            dl1 = dya * (z * sz) * _dsilu(l1, sg)
            acc8[pl.ds(0, PACK), :] += dl1 * n1
            acc8[pl.ds(PACK, PACK), :] += dl1
            dn1 = dl1 * g_ref[...]
            dy1 = rstd * (dn1 - _rowmean(dn1) - n1 * _rowmean(dn1 * n1))
            acc8[pl.ds(2 * PACK, PACK), :] += dy1
            dybuf[pl.ds(r0, PACK), :] = dy1
        _strips(tm, PACK,norm_bwd)

        def conv_bwd(r0):
            for lt in range(D // LANE):
                lanes = pl.ds(lt * LANE, LANE)
                glanes = pl.ds(D + lt * LANE, LANE)
                win = dybuf[pl.ds(r0, rows + HALO), lanes]
                sg16, a16 = [], []
                for h in range(rows // PACK):
                    rr = pl.ds(r0 + h * PACK, PACK)
                    s = _sigmoid(p_ref[rr, glanes].astype(F32))
                    sg16.append(s)
                    a16.append(p_ref[rr, lanes].astype(F32) * s)
                a = [a16[m // 2][(m % 2) * SUB:(m % 2 + 1) * SUB, :] for m in range(n_out)]
                da = [jnp.zeros((SUB, LANE), F32) for _ in range(n_out)]
                for phase in range(SUB):
                    offs = [o for o in range(CONV_K) if o % SUB == phase]
                    q_max = max(o // SUB for o in offs)
                    sh = win[phase:phase + (n_out + q_max) * SUB, :]
                    for o in offs:
                        k, q = CONV_K - 1 - o, o // SUB
                        w = wb_ref[pl.ds(k * SUB, SUB), lanes]
                        part = None
                        for m in range(n_out):
                            s = sh[(m + q) * SUB:(m + q + 1) * SUB, :]
                            da[m] = da[m] + w * s
                            part = a[m] * s if part is None else part + a[m] * s
                        tapacc[pl.ds(k * SUB, SUB), lanes] += part
                for h in range(rows // PACK):
                    rr = pl.ds(r0 + h * PACK, PACK)
                    da16 = jnp.concatenate(da[2 * h:2 * h + 2], axis=0)
                    dp_ref[rr, lanes] = (da16 * sg16[h]).astype(BF16)
                    dp_ref[rr, glanes] = (da16 * a16[h] * (1.0 - sg16[h])).astype(BF16)
        _strips(tm, rows, conv_bwd)

        dybuf[pl.ds(tm, HALO), :] = dybuf[pl.ds(0, HALO), :]

        @pl.when(i == n_i - 1)
        def _():
            for j in range(3):
                st_ref[pl.ds(j, 1), :] = jnp.sum(acc8[pl.ds(j * PACK, PACK), :], axis=0, keepdims=True)
            for k in range(CONV_K):
                st_ref[pl.ds(SUB + k, 1), :] = jnp.sum(tapacc[pl.ds(k * SUB, SUB), :], axis=0, keepdims=True)

    return pl.pallas_call(
        body, name="branch_a_bwd", grid=(n_i,),
        in_specs=[pl.BlockSpec((tm, 3 * D), lambda i: (tile_of(i), 0)),
                  pl.BlockSpec((tm, D), lambda i: (tile_of(i), 0)),
                  pl.BlockSpec((tm, D), lambda i: (tile_of(i), 0)),
                  pl.BlockSpec(memory_space=pl.ANY),
                  pl.BlockSpec((CONV_K * SUB, D), lambda i: (0, 0)), _vec_spec(1), _vec_spec(1)],
        out_specs=[pl.BlockSpec((tm, 3 * D), lambda i: (tile_of(i), 0)),
                   pl.BlockSpec((A_STATS_ROWS, D), lambda i: (0, 0))],
        out_shape=[_sds(dp.shape, BF16), _sds((A_STATS_ROWS, D), F32)],
        scratch_shapes=[pltpu.VMEM((tm + HALO, D), F32), pltpu.VMEM((3 * PACK, D), F32), pltpu.VMEM((CONV_K * SUB, D), F32)],
        input_output_aliases={3: 0},
        compiler_params=_params(("arbitrary",)),
    )(p, y1, dya_in, dp, conv_wb, ln_g, ln_b)


def _branch_b_bwd(p, dyb_in, dp, wt, wtt, bias_full, ln_g, ln_b):
    s_len = p.shape[0]
    tm = min(256, s_len)
    n_i = s_len // tm

    def body(p_ref, dyb_ref, dp_in, wt_ref, wtt_ref, bias_ref, g_ref, b_ref, dp_ref, st_ref, gbt_ref, gw_ref,
             vb, n2buf, rstdbuf, sbuf, dsb, dvbuf, acc8, gb_ref, dgbuf):
        del dp_in
        i = pl.program_id(0)

        @pl.when(i == 0)
        def _():
            st_ref[...] = jnp.zeros((SUB, D), F32)
            gbt_ref[...] = jnp.zeros((CHUNK, LANE), F32)
            gb_ref[...] = jnp.zeros((CHUNK, D), F32)
            gw_ref[...] = jnp.zeros((HEADS, CHUNK, CHUNK), F32)
            acc8[...] = jnp.zeros((2 * PACK, D), F32)

        def norm(r0):
            gv, dgv = _gelu_and_grad(p_ref[pl.ds(r0, PACK), pl.ds(D, D)].astype(F32))
            dgbuf[pl.ds(r0, PACK), :] = dgv
            mu = _rowmean(gv)
            vc = gv - mu
            rstd = lax.rsqrt(_rowmean(vc * vc) + EPS)
            n2 = vc * rstd
            n2buf[pl.ds(r0, PACK), :] = n2
            rstdbuf[pl.ds(r0, PACK), :] = jnp.broadcast_to(rstd, (PACK, LANE))
            vb[pl.ds(r0, PACK), :] = (n2 * g_ref[...] + b_ref[...]).astype(BF16)
        _strips(tm, PACK,norm)

        for ck in range(tm // CHUNK):
            for h in range(HEADS):
                blk = (pl.ds(ck * CHUNK, CHUNK), pl.ds(h * LANE, LANE))
                sbuf[blk] = jnp.dot(wt_ref[h], vb[blk], preferred_element_type=F32) + bias_ref[:, pl.ds(h * LANE, LANE)]

        def gate_bwd(r0):
            pu = p_ref[pl.ds(r0, PACK), pl.ds(0, D)].astype(F32)
            u, du = _gelu_and_grad(pu)
            z = p_ref[pl.ds(r0, PACK), pl.ds(2 * D, D)].astype(F32)
            sg = _sigmoid(z)
            sz = z * sg
            s = sbuf[pl.ds(r0, PACK), :]
            dyb = dyb_ref[pl.ds(r0, PACK), :]
            ds = dyb * u * sz
            dsb[pl.ds(r0, PACK), :] = ds.astype(BF16)
            gb_ref[pl.ds(pl.multiple_of(r0 % CHUNK, PACK), PACK), :] += ds
            dp_ref[pl.ds(r0, PACK), pl.ds(0, D)] = (dyb * s * sz * du).astype(BF16)
            dp_ref[pl.ds(r0, PACK), pl.ds(2 * D, D)] = (dyb * u * s * _dsilu(z, sg)).astype(BF16)
        _strips(tm, PACK,gate_bwd)

        for ck in range(tm // CHUNK):
            for h in range(HEADS):
                blk = (pl.ds(ck * CHUNK, CHUNK), pl.ds(h * LANE, LANE))
                d_s = dsb[blk]
                dvbuf[blk] = jnp.dot(wtt_ref[h], d_s, preferred_element_type=F32)
                gw_ref[h] += _dot_t(d_s, vb[blk])

        def norm_bwd(r0):
            dv = dvbuf[pl.ds(r0, PACK), :]
            n2 = n2buf[pl.ds(r0, PACK), :]
            rstd = rstdbuf[pl.ds(r0, PACK), pl.ds(0, 1)]
            acc8[pl.ds(0, PACK), :] += dv * n2
            acc8[pl.ds(PACK, PACK), :] += dv
            dn2 = dv * g_ref[...]
            dgv = rstd * (dn2 - _rowmean(dn2) - n2 * _rowmean(dn2 * n2))
            dp_ref[pl.ds(r0, PACK), pl.ds(D, D)] = (dgv * dgbuf[pl.ds(r0, PACK), :]).astype(BF16)
        _strips(tm, PACK,norm_bwd)

        @pl.when(i == n_i - 1)
        def _():
            for j in range(2):
                st_ref[pl.ds(j, 1), :] = jnp.sum(acc8[pl.ds(j * PACK, PACK), :], axis=0, keepdims=True)
            row = lax.broadcasted_iota(jnp.int32, (CHUNK, CHUNK), 0)
            col = lax.broadcasted_iota(jnp.int32, (CHUNK, CHUNK), 1)
            for h in range(HEADS):
                gw_ref[h] = jnp.where(row >= col, gw_ref[h], 0.0)
            lane = lax.broadcasted_iota(jnp.int32, (CHUNK, LANE), 1)
            gbt = jnp.zeros((CHUNK, LANE), F32)
            for h in range(HEADS):
                gbt = jnp.where(lane == h, jnp.sum(gb_ref[:, pl.ds(h * LANE, LANE)], axis=1, keepdims=True), gbt)
            gbt_ref[...] = gbt

    wspec = pl.BlockSpec((HEADS, CHUNK, CHUNK), lambda i: (0, 0, 0))
    return pl.pallas_call(
        body, name="branch_b_bwd", grid=(n_i,),
        in_specs=[pl.BlockSpec((tm, 3 * D), lambda i: (i, 1)), pl.BlockSpec((tm, D), lambda i: (i, 0)),
                  pl.BlockSpec(memory_space=pl.ANY), wspec, wspec,
                  pl.BlockSpec((CHUNK, D), lambda i: (0, 0)), _vec_spec(1), _vec_spec(1)],
        out_specs=[pl.BlockSpec((tm, 3 * D), lambda i: (i, 1)), pl.BlockSpec((SUB, D), lambda i: (0, 0)),
                   pl.BlockSpec((CHUNK, LANE), lambda i: (0, 0)), wspec],
        out_shape=[_sds(dp.shape, BF16), _sds((SUB, D), F32), _sds((CHUNK, LANE), F32), _sds((HEADS, CHUNK, CHUNK), F32)],
        scratch_shapes=[pltpu.VMEM((tm, D), BF16), pltpu.VMEM((tm, D), F32), pltpu.VMEM((tm, LANE), F32),
                        pltpu.VMEM((tm, D), F32), pltpu.VMEM((tm, D), BF16), pltpu.VMEM((tm, D), F32),
                        pltpu.VMEM((2 * PACK, D), F32), pltpu.VMEM((CHUNK, D), F32), pltpu.VMEM((tm, D), F32)],
        input_output_aliases={2: 0},
        compiler_params=_params(("arbitrary",)),
    )(p, dyb_in, dp, wt, wtt, bias_full, ln_g, ln_b)


def _in_proj_bwd(dp, wg_in, x, dx2, shift, scale, g_pre):
    del shift
    s_len = x.shape[0]
    tm = min(512, s_len)
    n_i = s_len // tm
    wn = wg_in.shape[2]

    def body(dp0, dp1, dp2, dp3, w_ref, x_ref, dx2_ref, sc_ref, g_ref, gx_ref, st_ref, acc, acc8):
        i = pl.program_id(0)

        @pl.when(i == 0)
        def _():
            st_ref[...] = jnp.zeros((SUB, D), F32)
            acc8[...] = jnp.zeros((3 * PACK, D), F32)

        dh = _dot_t(dp0[...], w_ref[0])
        for j, dp_ref in enumerate((dp1, dp2, dp3), start=1):
            dh = dh + _dot_t(dp_ref[...], w_ref[j])
        acc[...] = dh

        def strip(r0):
            xs = x_ref[pl.ds(r0, PACK), :]
            r = lax.rsqrt(_rowmean(xs * xs) + EPS)
            xn = xs * r
            dhs = acc[pl.ds(r0, PACK), :]
            acc8[pl.ds(0, PACK), :] += dhs
            acc8[pl.ds(PACK, PACK), :] += dhs * (xn * g_ref[...])
            dhp = dhs * (1.0 + sc_ref[...])
            acc8[pl.ds(2 * PACK, PACK), :] += dhp * xn
            dxn = dhp * g_ref[...]
            gx_ref[pl.ds(r0, PACK), :] = dx2_ref[pl.ds(r0, PACK), :] + r * (dxn - xn * _rowmean(dxn * xn))
        _strips(tm, PACK, strip)

        @pl.when(i == n_i - 1)
        def _():
            for k in range(3):
                st_ref[pl.ds(k, 1), :] = jnp.sum(acc8[pl.ds(k * PACK, PACK), :], axis=0, keepdims=True)

    tile = pl.BlockSpec((tm, D), lambda i: (i, 0))
    return pl.pallas_call(
        body, name="in_proj_bwd", grid=(n_i,),
        in_specs=[pl.BlockSpec((tm, wn), functools.partial(lambda j, i: (i, j), j)) for j in range(N_CHIP)] + [
                  pl.BlockSpec((N_CHIP, D, wn), lambda i: (0, 0, 0), pipeline_mode=pl.Buffered(1)),
                  tile, tile, _vec_spec(1), _vec_spec(1)],
        out_specs=[tile, pl.BlockSpec((SUB, D), lambda i: (0, 0))],
        out_shape=[_sds((s_len, D), F32), _sds((SUB, D), F32)],
        scratch_shapes=[pltpu.VMEM((tm, D), F32), pltpu.VMEM((3 * PACK, D), F32)],
        compiler_params=_params(("arbitrary",)),
    )(dp, dp, dp, dp, wg_in, x, dx2, scale, g_pre)


def _grad_matmul(a, b, name):
    s_len, n = b.shape
    cb = min(2 * D, n)
    tn = 512
    per = cb // tn

    def body(a_ref, b_ref, ob_ref):
        ob_ref[0] = lax.dot_general(a_ref[...], b_ref[...], (((0,), (0,)), ((), ())),
                                    preferred_element_type=F32).astype(BF16)

    return pl.pallas_call(
        body, name=name, grid=(n // tn,),
        in_specs=[pl.BlockSpec((s_len, D), lambda j: (0, 0), pipeline_mode=pl.Buffered(1)),
                  pl.BlockSpec((s_len, tn), lambda j: (0, j))],
        out_specs=pl.BlockSpec((1, D, tn), lambda j: (j // per, 0, j % per)),
        out_shape=_sds((n // cb, D, cb), BF16),
        compiler_params=_params(("arbitrary",)),
    )(a, b)


def _local_step(x, target, shift, scale, gate, g_pre, conv_w_full, conv_b, conv_ln_g, conv_ln_b,
                sgu_ln_g, sgu_ln_b, w_sgu, b_sgu, g_final, wg_in, out_shards):
    conv_wb = jnp.repeat(conv_w_full, SUB, axis=0)
    causal = jnp.tril(jnp.ones((CHUNK, CHUNK), dtype=bool))
    wt = jnp.where(causal[None], w_sgu, 0.0).astype(BF16)
    wtt = jnp.swapaxes(wt, 1, 2)
    bias_full = jnp.repeat(b_sgu.T, LANE, axis=1)

    p, hb, gathered = _in_proj_gather(x, shift, scale, g_pre, wg_in, out_shards)
    w_co, w_so, w_o = (g.reshape(D, D) for g in gathered)
    ya_in, y1 = _branch_a_fwd(p, conv_wb, conv_b, conv_ln_g, conv_ln_b)
    yb_in = _branch_b_fwd(p, wt, bias_full, sgu_ln_g, sgu_ln_b)
    dx2, dya_in, dyb_in, dp, mb, dob, dyab, dybb, sums_o = _out_proj(
        p, ya_in, yb_in, x, target, gate, g_final, w_co, w_so, w_o)
    dp, st_a = _branch_a_bwd(p, y1, dya_in, dp, conv_wb, conv_ln_g, conv_ln_b)
    dp, st_b, gbt, gws = _branch_b_bwd(p, dyb_in, dp, wt, wtt, bias_full, sgu_ln_g, sgu_ln_b)
    grad_x, st_i = _in_proj_bwd(dp, wg_in, x, dx2, shift, scale, g_pre)
    gw_o = _grad_matmul(mb, dob, "grad_w_o")
    gw_co = _grad_matmul(ya_in, dyab, "grad_w_conv_out")
    gw_so = _grad_matmul(yb_in, dybb, "grad_w_sgu_out")
    return dict(
        grad_x=grad_x, loss_cols=sums_o[2:3], g_final=sums_o[0:1], d_gate=sums_o[1:2],
        d_shift=st_i[0:1], d_scale=st_i[1:2], g_pre=st_i[2:3],
        conv_ln_g=st_a[0:1], conv_ln_b=st_a[1:2], conv_b=st_a[2:3], conv_w=st_a[SUB:SUB + CONV_K],
        sgu_ln_g=st_b[0:1], sgu_ln_b=st_b[1:2], b_sgu=gbt[:, :HEADS].T, w_sgu=gws,
        hb=hb, dp=dp, w_o=gw_o, w_conv_out=gw_co, w_sgu_out=gw_so)


ANY_SPEC = pl.BlockSpec(memory_space=pl.ANY)
VMEM_SPEC = pl.BlockSpec(memory_space=pltpu.VMEM)


def _place():
    return lax.axis_index("x"), lax.axis_index("y"), lax.axis_index("c")


def _peer(k):
    x, y, c = _place()
    return (1 - x if k & 4 else x, 1 - y if k & 2 else y, 1 - c if k & 1 else c)


def _dev_of(p):
    return 4 * p[0] + 2 * p[1] + p[2]


def _chip_of(p):
    return 2 * p[0] + p[1]


def _rdma(src, dst, send_sem, recv_sem, to):
    return pltpu.make_async_remote_copy(src_ref=src, dst_ref=dst, send_sem=send_sem, recv_sem=recv_sem,
                                        device_id=to, device_id_type=MESH)


CHIP_PEERS = (2, 4, 6)
ALL_PEERS = tuple(range(1, N_DEV))
SIBLING = 1


def _setup_comm(c8, w_ada_s, b_ada_s, convw_s, shards):
    n_mod = w_ada_s.shape[1]
    rows = SUB * N_DEV
    n = len(shards)
    parts = 4

    def body(c8_ref, wada_ref, bada_ref, cw_ref, *refs):
        ins, (call_ref, mod_ref, cwall_ref), outs = refs[:n], refs[n:n + 3], refs[n + 3:2 * n + 3]
        csend, crecv, wsend, wrecv, msend, mrecv = refs[2 * n + 3:2 * n + 9]
        ins = [r.at[:, pl.ds(k * (s.shape[1] // parts), s.shape[1] // parts)] for r, s in zip(ins, shards) for k in range(parts)]
        outs = [r.at[:, :, pl.ds(k * (s.shape[1] // parts), s.shape[1] // parts)] for r, s in zip(outs, shards) for k in range(parts)]
        gather_a, gather_b, gather_c = _gather_phases([s.shape[0] for s in shards for _ in range(parts)], ins, outs, refs[2 * n + 9:])
        me = _place()
        dev, chip = _dev_of(me), _chip_of(me)

        def c_rows(d):
            return call_ref.at[pl.ds(pl.multiple_of(d * SUB, SUB), SUB), :]

        call_ref[pl.ds(pl.multiple_of(dev * SUB, SUB), SUB), :] = c8_ref[...]
        cwall_ref[chip] = cw_ref[...]
        c_out = [_rdma(c8_ref, c_rows(dev), csend.at[k], crecv.at[k], _peer(k)) for k in ALL_PEERS]
        w_out = [_rdma(cw_ref, cwall_ref.at[chip], wsend.at[k], wrecv.at[k], _peer(k)) for k in CHIP_PEERS]
        for cp in c_out + w_out:
            cp.start()
        for k in ALL_PEERS:
            _rdma(c8_ref, c_rows(_dev_of(_peer(k))), csend.at[k], crecv.at[k], _peer(k)).wait_recv()
        part = jnp.dot(call_ref[...].astype(BF16), wada_ref[...].astype(BF16), preferred_element_type=F32) + bada_ref[...]
        mod_ref[chip] = part
        m_out = [_rdma(mod_ref.at[chip], mod_ref.at[chip], msend.at[k], mrecv.at[k], _peer(k)) for k in CHIP_PEERS]
        for cp in m_out:
            cp.start()
        gather_a()
        for k in CHIP_PEERS:
            pc = _chip_of(_peer(k))
            _rdma(cw_ref, cwall_ref.at[pc], wsend.at[k], wrecv.at[k], _peer(k)).wait_recv()
            _rdma(mod_ref.at[pc], mod_ref.at[pc], msend.at[k], mrecv.at[k], _peer(k)).wait_recv()
        for cp in c_out + w_out + m_out:
            cp.wait_send()
        gather_b()
        gather_c()

    res = pl.pallas_call(
        body, name="setup_comm",
        in_specs=[VMEM_SPEC] * (4 + n), out_specs=[VMEM_SPEC] * (3 + n),
        out_shape=([_sds((rows, D), F32), _sds((N_CHIP, rows, n_mod), F32), _sds((N_CHIP,) + convw_s.shape, F32)]
                   + [_sds((N_CHIP,) + s.shape, s.dtype) for s in shards]),
        scratch_shapes=([pltpu.SemaphoreType.DMA((N_DEV,))] * 6 + [pltpu.SemaphoreType.DMA((n * parts,))]
                        + [pltpu.SemaphoreType.DMA((n * parts, len(CHIP_PEERS)))] * 4),
        compiler_params=_params(),
    )(c8, w_ada_s, b_ada_s, convw_s, *shards)
    return res[0], res[1], res[2], res[3:]


def _gather_phases(row_counts, ins, dsts, sems):
    n = len(row_counts)
    lsem, isend, irecv, dsend, drecv = sems
    me = _place()
    chip, c = _chip_of(me), me[2]

    def half(t, which):
        hr = row_counts[t] // 2
        return pl.ds(pl.multiple_of(which * hr, hr), hr)

    def local(t):
        return pltpu.make_async_copy(ins[t], dsts[t].at[chip], lsem.at[t])

    def to_chip(t, j):
        return _rdma(ins[t].at[half(t, c)], dsts[t].at[chip, half(t, c)], isend.at[t, j], irecv.at[t, j], _peer(CHIP_PEERS[j]))

    def landed(t, j, which):
        return dsts[t].at[_chip_of(_peer(CHIP_PEERS[j])), half(t, which)]

    def to_sibling(t, j):
        return _rdma(landed(t, j, c), landed(t, j, c), dsend.at[t, j], drecv.at[t, j], _peer(SIBLING))

    pairs = [(t, j) for t in range(n) for j in range(len(CHIP_PEERS))]

    def phase_a():
        for t in range(n):
            local(t).start()
        for t, j in pairs:
            to_chip(t, j).start()

    def phase_b():
        for t, j in pairs:
            _rdma(landed(t, j, c), landed(t, j, c), isend.at[t, j], irecv.at[t, j], _peer(CHIP_PEERS[j])).wait_recv()
            to_sibling(t, j).start()

    def phase_c():
        for t, j in pairs:
            _rdma(landed(t, j, 1 - c), landed(t, j, 1 - c), dsend.at[t, j], drecv.at[t, j], _peer(SIBLING)).wait_recv()
        for t, j in pairs:
            to_chip(t, j).wait_send()
            to_sibling(t, j).wait_send()
        for t in range(n):
            local(t).wait()

    return phase_a, phase_b, phase_c


def _in_proj_gather(x, shift, scale, g_pre, wg_in, shards):
    s_len = x.shape[0]
    tm = min(256, s_len)
    n_i = s_len // tm
    wn = wg_in.shape[2]
    n = len(shards)

    def body(x_ref, sh_ref, sc_ref, g_ref, w_ref, *refs):
        ins, p_ref, hb_ref, outs = refs[:n], refs[n], refs[n + 1], refs[n + 2:2 * n + 2]
        gath, sems = refs[2 * n + 2:3 * n + 2], refs[3 * n + 2:]
        phases = _gather_phases([s.shape[0] for s in shards], ins, gath, sems)
        i = pl.program_id(0)
        for step, phase in zip((0, n_i // 2, n_i - 1), phases):
            pl.when(i == step)(phase)

        @pl.when(i == n_i - 1)
        def _():
            for t in range(n):
                outs[t][...] = gath[t][...]

        def strip(r0):
            xs = x_ref[pl.ds(r0, PACK), :]
            r = lax.rsqrt(_rowmean(xs * xs) + EPS)
            h = (xs * r) * g_ref[...] * (1.0 + sc_ref[...]) + sh_ref[...]
            hb_ref[pl.ds(r0, PACK), :] = h.astype(BF16)
        _strips(tm, PACK, strip)
        hb = hb_ref[...]
        for j in range(N_CHIP):
            p_ref[:, pl.ds(j * wn, wn)] = jnp.dot(hb, w_ref[j], preferred_element_type=F32).astype(BF16)

    res = pl.pallas_call(
        body, name="in_proj", grid=(n_i,),
        in_specs=[pl.BlockSpec((tm, D), lambda i: (i, 0)), _vec_spec(1), _vec_spec(1), _vec_spec(1),
                  pl.BlockSpec((N_CHIP, D, wn), lambda i: (0, 0, 0), pipeline_mode=pl.Buffered(1))] + [VMEM_SPEC] * n,
        out_specs=[pl.BlockSpec((tm, N_CHIP * wn), lambda i: (i, 0)), pl.BlockSpec((tm, D), lambda i: (i, 0))] + [VMEM_SPEC] * n,
        out_shape=([_sds((s_len, N_SEC * D), BF16), _sds((s_len, D), BF16)]
                   + [_sds((N_CHIP,) + s.shape, s.dtype) for s in shards]),
        scratch_shapes=([pltpu.VMEM((N_CHIP,) + s.shape, s.dtype) for s in shards]
                        + [pltpu.SemaphoreType.DMA((n,))] + [pltpu.SemaphoreType.DMA((n, len(CHIP_PEERS)))] * 4),
        compiler_params=_params(("arbitrary",)),
    )(x, shift, scale, g_pre, wg_in, *shards)
    return res[0], res[1], res[2:]


def _reduce_scatter(grads, name):
    n = len(grads)
    shapes = [g.shape[2:] for g in grads]
    parts = 4
    part_shapes = [(r, cols // parts) for r, cols in shapes for _ in range(parts)]

    def body(*refs):
        def halves(group, lead):
            return [ref.at[(slice(None),) * lead + (pl.ds(k * (s[1] // parts), s[1] // parts),)]
                    for ref, s in zip(group, shapes) for k in range(parts)]
        ins, outs = halves(refs[:n], 3), halves(refs[n:2 * n], 2)
        pbufs, rbufs, accs = halves(refs[2 * n:3 * n], 2), halves(refs[3 * n:4 * n], 2), halves(refs[4 * n:5 * n], 1)
        for phase in _reduce_phases(part_shapes, ins, outs, pbufs, rbufs, accs, refs[5 * n:]):
            phase()

    return pl.pallas_call(
        body, name=name,
        in_specs=[VMEM_SPEC] * n, out_specs=[VMEM_SPEC] * n,
        out_shape=[_sds((2,) + s, F32) for s in shapes],
        scratch_shapes=_reduce_scratch(shapes)[:3 * n] + _reduce_scratch(part_shapes)[3 * parts * n:],
        compiler_params=_params(),
    )(*grads)


def _reduce_phases(shapes, ins, outs, pbufs, rbufs, accs, sems):
    n = len(shapes)
    psend, precv, csend, crecv, fsend, frecv = sems
    me = _place()
    chip, c = _chip_of(me), me[2]
    sib = _peer(SIBLING)

    def to_sibling(t, d):
        return _rdma(ins[t].at[d, 1 - c], pbufs[t].at[d], psend.at[t, d], precv.at[t, d], sib)

    def to_chip(t, j):
        return _rdma(pbufs[t].at[jnp.bitwise_xor(chip, j)], rbufs[t].at[j - 1], csend.at[t, j], crecv.at[t, j], _peer(2 * j))

    def finished(t):
        return _rdma(outs[t].at[c], outs[t].at[c], fsend.at[t], frecv.at[t], sib)

    def phase_a():
        for j in (1, 2, 3, 0):
            for t in range(n):
                to_sibling(t, jnp.bitwise_xor(chip, j)).start()

    def phase_b():
        for j in (1, 2, 3, 0):
            d = jnp.bitwise_xor(chip, j)
            for t in range(n):
                to_sibling(t, d).wait_recv()

                def pair_sum(r0, t=t, d=d, j=j):
                    rows = pl.ds(r0, PACK)
                    s = ins[t][d, c, rows, :].astype(F32) + pbufs[t][d, rows, :].astype(F32)
                    if j == 0:
                        accs[t][rows, :] = s
                    else:
                        pbufs[t][d, rows, :] = s.astype(BF16)
                _strips(shapes[t][0], PACK, pair_sum)
                if j:
                    to_chip(t, j).start()

    def phase_c():
        for t in range(n):
            for j in (1, 2, 3):
                blk = rbufs[t].at[j - 1]
                _rdma(blk, blk, csend.at[t, j], crecv.at[t, j], _peer(2 * j)).wait_recv()

            def total(r0, t=t):
                rows = pl.ds(r0, PACK)
                s = accs[t][rows, :] + rbufs[t][0, rows, :].astype(F32)
                s = s + rbufs[t][1, rows, :].astype(F32)
                outs[t][c, rows, :] = s + rbufs[t][2, rows, :].astype(F32)
            _strips(shapes[t][0], PACK, total)
            finished(t).start()

    def phase_d():
        for t in range(n):
            blk = outs[t].at[1 - c]
            _rdma(blk, blk, fsend.at[t], frecv.at[t], sib).wait_recv()
        for t in range(n):
            for d in range(N_CHIP):
                to_sibling(t, d).wait_send()
            for j in (1, 2, 3):
                to_chip(t, j).wait_send()
            finished(t).wait_send()

    return phase_a, phase_b, phase_c, phase_d


def _sum_small_phases(ins, outs, pbufs, buf4s, sems):
    n = len(ins)
    psend, precv, send, recv = sems
    chip = _chip_of(_place())

    def swap(t):
        return _rdma(ins[t], pbufs[t], psend.at[t], precv.at[t], _peer(SIBLING))

    def to_chip(t, k):
        return _rdma(buf4s[t].at[chip], buf4s[t].at[chip], send.at[t, k], recv.at[t, k], _peer(k))

    def phase_a():
        for t in range(n):
            swap(t).start()

    def phase_b():
        for t in range(n):
            swap(t).wait()
            buf4s[t][chip] = ins[t][...] + pbufs[t][...]
            for k in CHIP_PEERS:
                to_chip(t, k).start()

    def phase_c():
        for t in range(n):
            for k in CHIP_PEERS:
                blk = buf4s[t].at[_chip_of(_peer(k))]
                _rdma(blk, blk, send.at[t, k], recv.at[t, k], _peer(k)).wait_recv()
            outs[t][...] = (buf4s[t][0] + buf4s[t][1]) + (buf4s[t][2] + buf4s[t][3])

    def phase_d():
        for t in range(n):
            for k in CHIP_PEERS:
                to_chip(t, k).wait_send()

    return phase_a, phase_b, phase_c, phase_d


def _sum_small_scratch(blobs):
    n = len(blobs)
    return ([pltpu.VMEM(b.shape, F32) for b in blobs] + [pltpu.VMEM((N_CHIP,) + b.shape, F32) for b in blobs]
            + [pltpu.SemaphoreType.DMA((n,))] * 2 + [pltpu.SemaphoreType.DMA((n, N_DEV))] * 2)


def _reduce_scratch(shapes):
    n = len(shapes)
    return ([pltpu.VMEM((N_CHIP,) + s, BF16) for s in shapes] + [pltpu.VMEM((N_CHIP - 1,) + s, BF16) for s in shapes]
            + [pltpu.VMEM(s, F32) for s in shapes]
            + [pltpu.SemaphoreType.DMA((n, N_CHIP))] * 4 + [pltpu.SemaphoreType.DMA((n,))] * 2)


def _grad_matmul_reduce(a, b, name, grads, blobs):
    s_len, n_cols = b.shape
    cb = min(2 * D, n_cols)
    tn = 512
    per = cb // tn
    steps = n_cols // tn
    n, nb = len(grads), len(blobs)
    shapes = [g.shape[2:] for g in grads]
    n_red = len(_reduce_scratch(shapes))

    def body(a_ref, b_ref, *refs):
        ins, bins = refs[:n], refs[n:n + nb]
        ob_ref, outs, bouts = refs[n + nb], refs[n + nb + 1:2 * n + nb + 1], refs[2 * n + nb + 1:2 * (n + nb) + 1]
        scratch = refs[2 * (n + nb) + 1:]
        fulls, red, small = scratch[:n], scratch[n:n + n_red], scratch[n + n_red:]
        phases = _reduce_phases(shapes, ins, fulls, red[:n], red[n:2 * n], red[2 * n:3 * n], red[3 * n:])
        small_phases = _sum_small_phases(bins, bouts, small[:nb], small[nb:2 * nb], small[2 * nb:])
        j = pl.program_id(0)
        for step, phase in zip((0, 2, steps - 2, steps - 1), phases):
            pl.when(j == step)(phase)
        for step, phase in zip((1, 3, steps - 2, steps - 1), small_phases):
            pl.when(j == step)(phase)

        @pl.when(j == steps - 1)
        def _():
            for t in range(n):
                outs[t][...] = fulls[t][...]
        ob_ref[0] = lax.dot_general(a_ref[...], b_ref[...], (((0,), (0,)), ((), ())),
                                    preferred_element_type=F32).astype(BF16)

    res = pl.pallas_call(
        body, name=name, grid=(steps,),
        in_specs=[pl.BlockSpec((s_len, D), lambda j: (0, 0), pipeline_mode=pl.Buffered(1)),
                  pl.BlockSpec((s_len, tn), lambda j: (0, j))] + [VMEM_SPEC] * (n + nb),
        out_specs=[pl.BlockSpec((1, D, tn), lambda j: (j // per, 0, j % per))] + [VMEM_SPEC] * (n + nb),
        out_shape=([_sds((n_cols // cb, D, cb), BF16)] + [_sds((2,) + s, F32) for s in shapes]
                   + [_sds(bl.shape, F32) for bl in blobs]),
        scratch_shapes=[pltpu.VMEM((2,) + s, F32) for s in shapes] + _reduce_scratch(shapes) + _sum_small_scratch(blobs),
        compiler_params=_params(("arbitrary",)),
    )(a, b, *grads, *blobs)
    return res[0], res[1:1 + n], res[1 + n:]


def _adamw_math(w, g, m, v):
    m = ADAM_B1 * m + (1.0 - ADAM_B1) * g
    v = ADAM_B2 * v + (1.0 - ADAM_B2) * (g * g)
    m_hat = m / (1.0 - ADAM_B1 ** ADAM_STEP)
    v_hat = v / (1.0 - ADAM_B2 ** ADAM_STEP)
    delta = -ADAM_LR * (m_hat / (jnp.sqrt(v_hat) + ADAM_EPS) + ADAM_WD * w)
    return delta, m, v


def _row_tile(r, cols):
    if r * cols * 4 <= 2 ** 20:
        return r
    return next(t for t in (512, 256, 128, 64, 32, 16, 8) if r % t == 0 and t * cols * 4 <= 2 ** 20)


def _adamw(w, g, m, v, name):
    r, cols = w.shape
    tr = _row_tile(r, cols)

    def body(w_ref, g_ref, m_ref, v_ref, go_ref, d_ref, nm_ref, nv_ref):
        g = g_ref[...]
        go_ref[...] = g
        d_ref[...], nm_ref[...], nv_ref[...] = _adamw_math(w_ref[...], g, m_ref[...], v_ref[...])

    spec = pl.BlockSpec((tr, cols), lambda i: (i, 0))
    return pl.pallas_call(
        body, name=name, grid=(r // tr,), in_specs=[spec] * 4, out_specs=[spec] * 4,
        out_shape=[_sds((r, cols), F32)] * 4, compiler_params=_params(("arbitrary",)),
    )(w, g, m, v)


def _adamw_ada(w, ct, dm, m, v):
    r, cols = w.shape
    tr = _row_tile(r, cols)

    def body(w_ref, ct_ref, dm_ref, m_ref, v_ref, g_ref, d_ref, nm_ref, nv_ref):
        g = jnp.dot(ct_ref[...], dm_ref[...], preferred_element_type=F32)
        g_ref[...] = g
        d_ref[...], nm_ref[...], nv_ref[...] = _adamw_math(w_ref[...], g, m_ref[...], v_ref[...])

    spec = pl.BlockSpec((tr, cols), lambda i: (i, 0))
    return pl.pallas_call(
        body, name="adamw_ada", grid=(r // tr,),
        in_specs=[spec, pl.BlockSpec((tr, LANE), lambda i: (i, 0)), pl.BlockSpec((LANE, cols), lambda i: (0, 0)), spec, spec],
        out_specs=[spec] * 4, out_shape=[_sds((r, cols), F32)] * 4, compiler_params=_params(("arbitrary",)),
    )(w, ct, dm, m, v)


BLOB_VEC, BLOB_BSGU, BLOB_CONV, BLOB_ADA, BLOB_DMOD, BLOB_LOSS, BLOB_ROWS = 0, 8, 16, 48, 56, 80, 88
N_VEC = 7


def _adamw_small(tot, g_w_sgu, g_conv, params):
    n = len(params)

    def body(*refs):
        tot_ref, gws_ref, gconv_ref = refs[:3]
        wmv = refs[3:3 + 3 * n]
        outs = refs[3 + 3 * n:]
        grads = [tot_ref[pl.ds(BLOB_VEC + i, 1), :] for i in range(N_VEC)]
        grads += [tot_ref[pl.ds(BLOB_BSGU, HEADS), pl.ds(0, CHUNK)], gconv_ref[...], gws_ref[...], tot_ref[pl.ds(BLOB_ADA, 3), :]]
        for i, g in enumerate(grads):
            w_ref, m_ref, v_ref = wmv[3 * i:3 * i + 3]
            d, nm, nv = _adamw_math(w_ref[...], g, m_ref[...], v_ref[...])
            outs[4 * i][...] = g
            outs[4 * i + 1][...] = d
            outs[4 * i + 2][...] = nm
            outs[4 * i + 3][...] = nv

    flat = [a for wmv in params for a in wmv]
    return pl.pallas_call(
        body, name="adamw_small",
        in_specs=[VMEM_SPEC] * (3 + len(flat)), out_specs=[VMEM_SPEC] * (4 * n),
        out_shape=[_sds(wmv[0].shape, F32) for wmv in params for _ in range(4)],
        compiler_params=_params(),
    )(tot, g_w_sgu, g_conv, *flat)


def _set_rows(buf, row, val):
    return lax.dynamic_update_slice(buf, val.astype(F32), (row, 0))


def kernel(x, c, w_ada, b_ada, g_pre, w_in, conv_w, conv_b, conv_ln_g, conv_ln_b, w_conv_out, sgu_ln_g, sgu_ln_b, w_sgu, b_sgu, w_sgu_out, w_o, g_final, loss_target, m_w_ada, m_b_ada, m_g_pre, m_w_in, m_conv_w, m_conv_b, m_conv_ln_g, m_conv_ln_b, m_w_conv_out, m_sgu_ln_g, m_sgu_ln_b, m_w_sgu, m_b_sgu, m_w_sgu_out, m_w_o, m_g_final, v_w_ada, v_b_ada, v_g_pre, v_w_in, v_conv_w, v_conv_b, v_conv_ln_g, v_conv_ln_b, v_w_conv_out, v_sgu_ln_g, v_sgu_ln_b, v_w_sgu, v_b_sgu, v_w_sgu_out, v_w_o, v_g_final):
    me = _place()
    dev, chip = _dev_of(me), _chip_of(me)
    n_ada = w_ada.shape[2]
    conv_cols = conv_w.shape[2]

    b_ada_s = lax.dynamic_slice(b_ada, (0, chip * n_ada), (1, n_ada))
    c_all, mod_all, cw_all, (wg_in,) = _setup_comm(
        jnp.broadcast_to(c, (SUB, D)), w_ada[0], b_ada_s, jnp.pad(conv_w[0], ((0, HALO - CONV_K), (0, 0))),
        [w_in[0].astype(BF16)])
    mod = lax.dynamic_slice(mod_all, (0, dev * SUB, 0), (N_CHIP, 1, n_ada)).reshape(1, 3 * D)
    shift, scale, gate = mod[:, :D], mod[:, D:2 * D], mod[:, 2 * D:]
    conv_w_full = jnp.swapaxes(cw_all, 0, 1).reshape(HALO, D)[:CONV_K]

    loc = _local_step(x[0], loss_target[0], shift, scale, gate, g_pre, conv_w_full, conv_b, conv_ln_g, conv_ln_b,
                      sgu_ln_g, sgu_ln_b, w_sgu[0], b_sgu[0], g_final.reshape(1, D), wg_in,
                      [w_conv_out[0].astype(BF16), w_sgu_out[0].astype(BF16), w_o[0].astype(BF16)])

    d_mod = jnp.concatenate([loc["d_shift"], loc["d_scale"], loc["d_gate"]], axis=0)
    blob = jnp.zeros((BLOB_ROWS, D), F32)
    for i, name in enumerate(["g_pre", "conv_b", "conv_ln_g", "conv_ln_b", "sgu_ln_g", "sgu_ln_b", "g_final"]):
        blob = _set_rows(blob, BLOB_VEC + i, loc[name])
    blob = _set_rows(blob, BLOB_BSGU, loc["b_sgu"])
    blob = _set_rows(blob, BLOB_CONV, loc["conv_w"])
    blob = _set_rows(blob, BLOB_ADA, d_mod)
    blob = lax.dynamic_update_slice(blob, d_mod, (BLOB_DMOD + 3 * dev, 0))
    blob = _set_rows(blob, BLOB_LOSS, loc["loss_cols"])

    big = ["w_in", "w_conv_out", "w_sgu_out", "w_o"]
    contrib_out = [loc[name].reshape(N_CHIP, 2, D // (2 * N_CHIP), D) for name in big[1:]]
    gw_in, full_out, (tot, g_w_sgu) = _grad_matmul_reduce(
        loc["hb"], loc["dp"], "grad_w_in", contrib_out, [blob, loc["w_sgu"].reshape(HEADS * CHUNK, CHUNK)])
    full_in = _reduce_scatter([gw_in.reshape(N_CHIP, 2, D // 2, gw_in.shape[2])], "reduce_w_in")
    g_big = {name: f.reshape(2 * f.shape[1], f.shape[2]) for name, f in zip(big, list(full_in) + list(full_out))}

    loss = jnp.sum(tot[BLOB_LOSS])
    g_conv_s = lax.dynamic_slice(tot, (BLOB_CONV, chip * conv_cols), (CONV_K, conv_cols))
    d_mod_all = tot[BLOB_DMOD:BLOB_DMOD + 3 * N_DEV].reshape(N_DEV, 3 * D)

    ct = jnp.pad(c_all[::SUB].T, ((0, 0), (0, LANE - N_DEV))).astype(BF16)
    dm = jnp.pad(lax.dynamic_slice(d_mod_all, (0, chip * n_ada), (N_DEV, n_ada)), ((0, LANE - N_DEV), (0, 0))).astype(BF16)
    g_ada, d_ada, nm_ada, nv_ada = _adamw_ada(w_ada[0], ct, dm, m_w_ada[0], v_w_ada[0])

    upd = {}
    for name, w, m, v in [("w_in", w_in, m_w_in, v_w_in), ("w_conv_out", w_conv_out, m_w_conv_out, v_w_conv_out),
                          ("w_sgu_out", w_sgu_out, m_w_sgu_out, v_w_sgu_out), ("w_o", w_o, m_w_o, v_w_o)]:
        upd[name] = _adamw(w[0], g_big[name], m[0], v[0], "adamw_" + name)

    def wmv(w, m, v, shape):
        return tuple(a.reshape(shape) for a in (w, m, v))

    small_params = [wmv(w, m, v, (1, D)) for w, m, v in [
        (g_pre, m_g_pre, v_g_pre), (conv_b, m_conv_b, v_conv_b), (conv_ln_g, m_conv_ln_g, v_conv_ln_g),
        (conv_ln_b, m_conv_ln_b, v_conv_ln_b), (sgu_ln_g, m_sgu_ln_g, v_sgu_ln_g), (sgu_ln_b, m_sgu_ln_b, v_sgu_ln_b),
        (g_final, m_g_final, v_g_final)]]
    small_params += [wmv(b_sgu, m_b_sgu, v_b_sgu, (HEADS, CHUNK)), wmv(conv_w, m_conv_w, v_conv_w, (CONV_K, conv_cols)),
                     wmv(w_sgu, m_w_sgu, v_w_sgu, (HEADS * CHUNK, CHUNK)), wmv(b_ada, m_b_ada, v_b_ada, (3, D))]
    small_out = _adamw_small(tot, g_w_sgu, g_conv_s, small_params)

    def leaves(kind):
        vecs = [small_out[4 * i + kind] for i in range(N_VEC)]
        o_b_sgu, o_conv, o_w_sgu, o_b_ada = (small_out[4 * (N_VEC + i) + kind] for i in range(4))
        ada = (g_ada, d_ada, nm_ada, nv_ada)[kind]
        def bigk(name):
            return upd[name][kind][None]
        return [ada[None], o_b_ada.reshape(1, 3 * D), vecs[0], bigk("w_in"), o_conv[None], vecs[1], vecs[2], vecs[3],
                bigk("w_conv_out"), vecs[4], vecs[5], o_w_sgu.reshape(1, HEADS, CHUNK, CHUNK), o_b_sgu[None],
                bigk("w_sgu_out"), bigk("w_o"), vecs[6].reshape(D)]

    return (loss, loc["grad_x"][None], *leaves(0), *leaves(1), *leaves(2), *leaves(3))
```
